```python
import jax
import jax.numpy as jnp
from jax import lax
import numpy as np

D_MODEL = 1024
BATCH = 8
SEQ = 4096
DEPTH = 4

N_MEM = 256
BLOCK = 128
ROPE_THETA = 10000.0
EPS = 1e-6
NEG = -1e30
MAX_POS_OFFSET = 1024

A_HEADS = 8
A_KV_HEADS = 2
A_GROUP = A_HEADS // A_KV_HEADS
A_HD = 64
A_WINDOW = 128
B_HEADS = 8
B_Q_LORA = 384
B_KV_LORA = 256
B_NOPE = 64
B_ROPE = 32
B_V = 64
C_PATTERNS = ((128, 1), (512, 4), (2048, 16))
N_C_GROUPS = len(C_PATTERNS)
C_HEADS = 8
C_HD = 64
M_HEADS = 4
M_HD = 128

N_BRANCH = 4
BRANCH_W = 512
D_FF = 4 * D_MODEL

IN_SIZES = ((A_HEADS * A_HD, A_KV_HEADS * A_HD, A_KV_HEADS * A_HD, B_Q_LORA, B_KV_LORA, B_ROPE)
            + (C_HEADS * C_HD,) * (3 * N_C_GROUPS)
            + (M_HEADS * M_HD, N_BRANCH * D_MODEL))
N_IN = sum(IN_SIZES)
IN_SPLITS = tuple(int(s) for s in np.cumsum(IN_SIZES)[:-1])

kernel_name = 'hybrid_gated_swa_mla_dilated_mem_block'


def rms_norm(x, g):
    xf = x.astype(jnp.float32)
    y = xf * lax.rsqrt(jnp.mean(xf * xf, axis=-1, keepdims=True) + EPS)
    return (y * g.astype(jnp.float32)).astype(x.dtype)


def rope_tables(positions, dim):
    inv = ROPE_THETA ** (-jnp.arange(0, dim, 2, dtype=jnp.float32) / dim)
    ang = positions.astype(jnp.float32)[..., None] * inv
    return jnp.cos(ang), jnp.sin(ang)


def apply_rope(x, cos, sin):
    xf = x.astype(jnp.float32)
    x1, x2 = jnp.split(xf, 2, axis=-1)
    c = cos[:, :, None, :]
    s = sin[:, :, None, :]
    return jnp.concatenate([x1 * c - x2 * s, x2 * c + x1 * s], axis=-1).astype(x.dtype)


def banded_attention(q, k, v, max_dist, sinks=None):
    n, length, hk, grp, hd = q.shape
    nb = -(-length // BLOCK)
    pad = nb * BLOCK - length
    if pad:
        q = jnp.pad(q, ((0, 0), (0, pad), (0, 0), (0, 0), (0, 0)))
        k = jnp.pad(k, ((0, 0), (0, pad), (0, 0), (0, 0)))
        v = jnp.pad(v, ((0, 0), (0, pad), (0, 0), (0, 0)))
    qb = q.reshape(n, nb, BLOCK, hk, grp, hd)

    def two_blocks(t):
        tb = t.reshape(n, nb, BLOCK, hk, t.shape[-1])
        prev = jnp.pad(tb, ((0, 0), (1, 0), (0, 0), (0, 0), (0, 0)))[:, :-1]
        return jnp.concatenate([prev, tb], axis=2)

    kk = two_blocks(k)
    vv = two_blocks(v)
    s = jnp.einsum('nbqhgd,nbkhd->nbhgqk', qb, kk).astype(jnp.float32) * (hd ** -0.5)
    qi = jnp.arange(BLOCK)[:, None]
    kj = jnp.arange(2 * BLOCK)[None, :]
    dist = qi - kj + BLOCK
    band = (dist >= 0) & (dist <= max_dist)
    valid_prev = (jnp.arange(nb)[:, None, None] > 0) | (kj >= BLOCK)[None]
    mask = band[None] & valid_prev
    s = jnp.where(mask[None, :, None, None], s, NEG)
    lse = jax.nn.logsumexp(s, axis=-1)
    if sinks is not None:
        lse = jnp.logaddexp(lse, sinks.astype(jnp.float32)[None, None, :, :, None])
    p = jnp.exp(s - lse[..., None]).astype(v.dtype)
    o = jnp.einsum('nbhgqk,nbkhd->nbqhgd', p, vv).reshape(n, nb * BLOCK, hk, grp, hd)[:, :length]
    lse = lse.transpose(0, 1, 4, 2, 3).reshape(n, nb * BLOCK, hk, grp)[:, :length]
    return o, lse


def dilated_group(q, k, v, window, dilation):
    b, s, h, hd = q.shape
    length = s // dilation

    def gather(t):
        return t.reshape(b, length, dilation, h, hd).transpose(0, 2, 1, 3, 4).reshape(b * dilation, length, h, hd)

    o, lse = banded_attention(gather(q)[:, :, :, None, :], gather(k), gather(v), window // dilation)
    o = o[:, :, :, 0].reshape(b, dilation, length, h, hd).transpose(0, 2, 1, 3, 4).reshape(b, s, h, hd)
    lse = lse[..., 0].reshape(b, dilation, length, h).transpose(0, 2, 1, 3).reshape(b, s, h)
    return o, lse


def mla_causal_attention(qn, qp, kn, kp, v):
    b, s, h, _ = qn.shape
    nb = s // BLOCK
    scale = (qn.shape[-1] + qp.shape[-1]) ** -0.5
    kpos = jnp.arange(s)

    def one_block(i):
        start = i * BLOCK
        qnb = lax.dynamic_slice_in_dim(qn, start, BLOCK, axis=1)
        qpb = lax.dynamic_slice_in_dim(qp, start, BLOCK, axis=1)
        sc = (jnp.einsum('bqhd,bkhd->bhqk', qnb, kn)
              + jnp.einsum('bqhr,bkr->bhqk', qpb, kp)).astype(jnp.float32) * scale
        qpos = start + jnp.arange(BLOCK)
        sc = jnp.where(kpos[None, :] <= qpos[:, None], sc, NEG)
        p = jax.nn.softmax(sc, axis=-1).astype(v.dtype)
        return jnp.einsum('bhqk,bkhd->bqhd', p, v)

    o = lax.map(one_block, jnp.arange(nb))
    return o.transpose(1, 0, 2, 3, 4).reshape(b, s, h, v.shape[-1])


def memory_attention(q, k, v):
    s = jnp.einsum('bshd,bmhd->bhsm', q, k).astype(jnp.float32) * (q.shape[-1] ** -0.5)
    p = jax.nn.softmax(s, axis=-1).astype(v.dtype)
    return jnp.einsum('bhsm,bmhd->bshd', p, v)


def _fwd_setup_inputs(seed: int = 0) -> dict:
    key = jax.random.key(seed)
    ks = jax.random.split(key, 32)
    f32 = jnp.float32
    L = DEPTH
    D = D_MODEL

    def nrm(k, shape, scale):
        return jax.random.normal(k, shape, f32) * scale

    def gain(k, shape):
        return 1.0 + 0.1 * jax.random.normal(k, shape, f32)

    x = nrm(ks[0], (BATCH, SEQ, D), 1.0)
    mem = nrm(ks[1], (BATCH, N_MEM, D), 1.0)
    offsets = jax.random.randint(ks[2], (BATCH, 1), 0, MAX_POS_OFFSET, dtype=jnp.int32)
    positions = (offsets + jnp.arange(SEQ, dtype=jnp.int32)[None, :]).astype(jnp.int32)
    return {
        'x': x,
        'mem': mem,
        'positions': positions,
        'g_mix': gain(ks[3], (L, D)),
        'w_in': nrm(ks[4], (L, D, N_IN), D ** -0.5),
        'b_gate': nrm(ks[5], (L, N_BRANCH * D), 0.1),
        'a_qn': gain(ks[6], (L, A_HD)),
        'a_kn': gain(ks[7], (L, A_HD)),
        'a_sink': nrm(ks[8], (L, A_HEADS), 0.5),
        'b_qa_norm': gain(ks[9], (L, B_Q_LORA)),
        'b_kva_norm': gain(ks[10], (L, B_KV_LORA)),
        'b_w_uq': nrm(ks[11], (L, B_Q_LORA, B_HEADS * (B_NOPE + B_ROPE)), B_Q_LORA ** -0.5),
        'b_w_ukv': nrm(ks[12], (L, B_KV_LORA, B_HEADS * (B_NOPE + B_V)), B_KV_LORA ** -0.5),
        'b_qn': gain(ks[13], (L, B_NOPE + B_ROPE)),
        'b_kn': gain(ks[14], (L, B_NOPE + B_ROPE)),
        'c_qn': gain(ks[15], (L, N_C_GROUPS, C_HD)),
        'c_kn': gain(ks[16], (L, N_C_GROUPS, C_HD)),
        'm_g_mem': gain(ks[17], (L, D)),
        'm_w_kv': nrm(ks[18], (L, D, 2 * M_HEADS * M_HD), D ** -0.5),
        'm_qn': gain(ks[19], (L, M_HD)),
        'm_kn': gain(ks[20], (L, M_HD)),
        'w_branch': nrm(ks[21], (L, N_BRANCH, BRANCH_W, D), BRANCH_W ** -0.5),
        'w_out': nrm(ks[22], (L, D, D), D ** -0.5),
        'g_mlp': gain(ks[23], (L, D)),
        'w_up': nrm(ks[24], (L, D, D_FF), D ** -0.5),
        'w_down': nrm(ks[25], (L, D_FF, D), D_FF ** -0.5),
    }


def _fwd_reference(x, mem, positions, g_mix, w_in, b_gate, a_qn, a_kn, a_sink, b_qa_norm, b_kva_norm,
              b_w_uq, b_w_ukv, b_qn, b_kn, c_qn, c_kn, m_g_mem, m_w_kv, m_qn, m_kn,
              w_branch, w_out, g_mlp, w_up, w_down):
    B, S, D = x.shape
    cos_h, sin_h = rope_tables(positions, A_HD)
    cos_r, sin_r = rope_tables(positions, B_ROPE)
    for l in range(DEPTH):
        h = rms_norm(x, g_mix[l])
        parts = jnp.split(h @ w_in[l], IN_SPLITS, axis=-1)
        a_q, a_k, a_v, b_cq, b_ckv, b_kr = parts[:6]
        c_parts = parts[6:6 + 3 * N_C_GROUPS]
        m_q, gate_pre = parts[6 + 3 * N_C_GROUPS:]

        qa = apply_rope(rms_norm(a_q.reshape(B, S, A_HEADS, A_HD), a_qn[l]), cos_h, sin_h)
        ka = apply_rope(rms_norm(a_k.reshape(B, S, A_KV_HEADS, A_HD), a_kn[l]), cos_h, sin_h)
        va = a_v.reshape(B, S, A_KV_HEADS, A_HD)
        o_a, _ = banded_attention(qa.reshape(B, S, A_KV_HEADS, A_GROUP, A_HD), ka, va,
                                  A_WINDOW - 1, a_sink[l].reshape(A_KV_HEADS, A_GROUP))
        o_a = o_a.reshape(B, S, BRANCH_W)

        q_up = (rms_norm(b_cq, b_qa_norm[l]) @ b_w_uq[l]).reshape(B, S, B_HEADS, B_NOPE + B_ROPE)
        kv_up = (rms_norm(b_ckv, b_kva_norm[l]) @ b_w_ukv[l]).reshape(B, S, B_HEADS, B_NOPE + B_V)
        qn = rms_norm(q_up[..., :B_NOPE], b_qn[l, :B_NOPE])
        qp = apply_rope(rms_norm(q_up[..., B_NOPE:], b_qn[l, B_NOPE:]), cos_r, sin_r)
        kn = rms_norm(kv_up[..., :B_NOPE], b_kn[l, :B_NOPE])
        vb = kv_up[..., B_NOPE:]
        kp = apply_rope(rms_norm(b_kr, b_kn[l, B_NOPE:])[:, :, None, :], cos_r, sin_r)[:, :, 0, :]
        o_b = mla_causal_attention(qn, qp, kn, kp, vb).reshape(B, S, BRANCH_W)

        outs = []
        lses = []
        for g, (win, dil) in enumerate(C_PATTERNS):
            cq, ck, cv = c_parts[3 * g:3 * g + 3]
            qc = apply_rope(rms_norm(cq.reshape(B, S, C_HEADS, C_HD), c_qn[l, g]), cos_h, sin_h)
            kc = apply_rope(rms_norm(ck.reshape(B, S, C_HEADS, C_HD), c_kn[l, g]), cos_h, sin_h)
            vc = cv.reshape(B, S, C_HEADS, C_HD)
            o_g, lse_g = dilated_group(qc, kc, vc, win, dil)
            outs.append(o_g)
            lses.append(lse_g)
        wts = jax.nn.softmax(jnp.stack(lses, axis=0), axis=0).astype(x.dtype)
        o_c = jnp.einsum('gbsh,gbshd->bshd', wts, jnp.stack(outs, axis=0)).reshape(B, S, BRANCH_W)

        mkv = rms_norm(mem, m_g_mem[l]) @ m_w_kv[l]
        mk = rms_norm(mkv[..., :M_HEADS * M_HD].reshape(B, N_MEM, M_HEADS, M_HD), m_kn[l])
        mv = mkv[..., M_HEADS * M_HD:].reshape(B, N_MEM, M_HEADS, M_HD)
        mq = rms_norm(m_q.reshape(B, S, M_HEADS, M_HD), m_qn[l])
        o_m = memory_attention(mq, mk, mv).reshape(B, S, BRANCH_W)

        o = jnp.stack([o_a, o_b, o_c, o_m], axis=2)
        y = jnp.einsum('bsnc,ncd->bsnd', o, w_branch[l])
        gates = jax.nn.sigmoid((gate_pre + b_gate[l]).astype(jnp.float32)).astype(x.dtype)
        gates = gates.reshape(B, S, N_BRANCH, D)
        x = x + jnp.einsum('bsnd,de->bse', gates * y, w_out[l])

        u = rms_norm(x, g_mlp[l]) @ w_up[l]
        x = x + jnp.square(jax.nn.relu(u)) @ w_down[l]
    return x


import jax as _jax
import jax.numpy as _jnp

TWIN_FORMAT = 'train_step'
FWD_PARAMS = ['x', 'mem', 'positions', 'g_mix', 'w_in', 'b_gate', 'a_qn', 'a_kn', 'a_sink', 'b_qa_norm', 'b_kva_norm', 'b_w_uq', 'b_w_ukv', 'b_qn', 'b_kn', 'c_qn', 'c_kn', 'm_g_mem', 'm_w_kv', 'm_qn', 'm_kn', 'w_branch', 'w_out', 'g_mlp', 'w_up', 'w_down']
TWIN_WEIGHTS = ['g_mix', 'w_in', 'b_gate', 'a_qn', 'a_kn', 'a_sink', 'b_qa_norm', 'b_kva_norm', 'b_w_uq', 'b_w_ukv', 'b_qn', 'b_kn', 'c_qn', 'c_kn', 'm_g_mem', 'm_w_kv', 'm_qn', 'm_kn', 'w_branch', 'w_out', 'g_mlp', 'w_up', 'w_down']
TWIN_DIFF_INPUT = 'x'
TWIN_INPUTS = ['x', 'mem', 'positions', 'g_mix', 'w_in', 'b_gate', 'a_qn', 'a_kn', 'a_sink', 'b_qa_norm', 'b_kva_norm', 'b_w_uq', 'b_w_ukv', 'b_qn', 'b_kn', 'c_qn', 'c_kn', 'm_g_mem', 'm_w_kv', 'm_qn', 'm_kn', 'w_branch', 'w_out', 'g_mlp', 'w_up', 'w_down', 'loss_target', 'm_g_mix', 'm_w_in', 'm_b_gate', 'm_a_qn', 'm_a_kn', 'm_a_sink', 'm_b_qa_norm', 'm_b_kva_norm', 'm_b_w_uq', 'm_b_w_ukv', 'm_b_qn', 'm_b_kn', 'm_c_qn', 'm_c_kn', 'm_m_g_mem', 'm_m_w_kv', 'm_m_qn', 'm_m_kn', 'm_w_branch', 'm_w_out', 'm_g_mlp', 'm_w_up', 'm_w_down', 'v_g_mix', 'v_w_in', 'v_b_gate', 'v_a_qn', 'v_a_kn', 'v_a_sink', 'v_b_qa_norm', 'v_b_kva_norm', 'v_b_w_uq', 'v_b_w_ukv', 'v_b_qn', 'v_b_kn', 'v_c_qn', 'v_c_kn', 'v_m_g_mem', 'v_m_w_kv', 'v_m_qn', 'v_m_kn', 'v_w_branch', 'v_w_out', 'v_g_mlp', 'v_w_up', 'v_w_down']
TWIN_OUTPUTS = ['loss', 'grad_x', 'grad_g_mix', 'grad_w_in', 'grad_b_gate', 'grad_a_qn', 'grad_a_kn', 'grad_a_sink', 'grad_b_qa_norm', 'grad_b_kva_norm', 'grad_b_w_uq', 'grad_b_w_ukv', 'grad_b_qn', 'grad_b_kn', 'grad_c_qn', 'grad_c_kn', 'grad_m_g_mem', 'grad_m_w_kv', 'grad_m_qn', 'grad_m_kn', 'grad_w_branch', 'grad_w_out', 'grad_g_mlp', 'grad_w_up', 'grad_w_down', 'delta_g_mix', 'delta_w_in', 'delta_b_gate', 'delta_a_qn', 'delta_a_kn', 'delta_a_sink', 'delta_b_qa_norm', 'delta_b_kva_norm', 'delta_b_w_uq', 'delta_b_w_ukv', 'delta_b_qn', 'delta_b_kn', 'delta_c_qn', 'delta_c_kn', 'delta_m_g_mem', 'delta_m_w_kv', 'delta_m_qn', 'delta_m_kn', 'delta_w_branch', 'delta_w_out', 'delta_g_mlp', 'delta_w_up', 'delta_w_down', 'new_m_g_mix', 'new_m_w_in', 'new_m_b_gate', 'new_m_a_qn', 'new_m_a_kn', 'new_m_a_sink', 'new_m_b_qa_norm', 'new_m_b_kva_norm', 'new_m_b_w_uq', 'new_m_b_w_ukv', 'new_m_b_qn', 'new_m_b_kn', 'new_m_c_qn', 'new_m_c_kn', 'new_m_m_g_mem', 'new_m_m_w_kv', 'new_m_m_qn', 'new_m_m_kn', 'new_m_w_branch', 'new_m_w_out', 'new_m_g_mlp', 'new_m_w_up', 'new_m_w_down', 'new_v_g_mix', 'new_v_w_in', 'new_v_b_gate', 'new_v_a_qn', 'new_v_a_kn', 'new_v_a_sink', 'new_v_b_qa_norm', 'new_v_b_kva_norm', 'new_v_b_w_uq', 'new_v_b_w_ukv', 'new_v_b_qn', 'new_v_b_kn', 'new_v_c_qn', 'new_v_c_kn', 'new_v_m_g_mem', 'new_v_m_w_kv', 'new_v_m_qn', 'new_v_m_kn', 'new_v_w_branch', 'new_v_w_out', 'new_v_g_mlp', 'new_v_w_up', 'new_v_w_down']
TWIN_LEAF_KINDS = {'loss': 'loss', 'grad_x': 'grad_x', 'grad_g_mix': 'grad_w', 'grad_w_in': 'grad_w', 'grad_b_gate': 'grad_w', 'grad_a_qn': 'grad_w', 'grad_a_kn': 'grad_w', 'grad_a_sink': 'grad_w', 'grad_b_qa_norm': 'grad_w', 'grad_b_kva_norm': 'grad_w', 'grad_b_w_uq': 'grad_w', 'grad_b_w_ukv': 'grad_w', 'grad_b_qn': 'grad_w', 'grad_b_kn': 'grad_w', 'grad_c_qn': 'grad_w', 'grad_c_kn': 'grad_w', 'grad_m_g_mem': 'grad_w', 'grad_m_w_kv': 'grad_w', 'grad_m_qn': 'grad_w', 'grad_m_kn': 'grad_w', 'grad_w_branch': 'grad_w', 'grad_w_out': 'grad_w', 'grad_g_mlp': 'grad_w', 'grad_w_up': 'grad_w', 'grad_w_down': 'grad_w', 'delta_g_mix': 'delta_w', 'delta_w_in': 'delta_w', 'delta_b_gate': 'delta_w', 'delta_a_qn': 'delta_w', 'delta_a_kn': 'delta_w', 'delta_a_sink': 'delta_w', 'delta_b_qa_norm': 'delta_w', 'delta_b_kva_norm': 'delta_w', 'delta_b_w_uq': 'delta_w', 'delta_b_w_ukv': 'delta_w', 'delta_b_qn': 'delta_w', 'delta_b_kn': 'delta_w', 'delta_c_qn': 'delta_w', 'delta_c_kn': 'delta_w', 'delta_m_g_mem': 'delta_w', 'delta_m_w_kv': 'delta_w', 'delta_m_qn': 'delta_w', 'delta_m_kn': 'delta_w', 'delta_w_branch': 'delta_w', 'delta_w_out': 'delta_w', 'delta_g_mlp': 'delta_w', 'delta_w_up': 'delta_w', 'delta_w_down': 'delta_w', 'new_m_g_mix': 'new_m', 'new_m_w_in': 'new_m', 'new_m_b_gate': 'new_m', 'new_m_a_qn': 'new_m', 'new_m_a_kn': 'new_m', 'new_m_a_sink': 'new_m', 'new_m_b_qa_norm': 'new_m', 'new_m_b_kva_norm': 'new_m', 'new_m_b_w_uq': 'new_m', 'new_m_b_w_ukv': 'new_m', 'new_m_b_qn': 'new_m', 'new_m_b_kn': 'new_m', 'new_m_c_qn': 'new_m', 'new_m_c_kn': 'new_m', 'new_m_m_g_mem': 'new_m', 'new_m_m_w_kv': 'new_m', 'new_m_m_qn': 'new_m', 'new_m_m_kn': 'new_m', 'new_m_w_branch': 'new_m', 'new_m_w_out': 'new_m', 'new_m_g_mlp': 'new_m', 'new_m_w_up': 'new_m', 'new_m_w_down': 'new_m', 'new_v_g_mix': 'new_v', 'new_v_w_in': 'new_v', 'new_v_b_gate': 'new_v', 'new_v_a_qn': 'new_v', 'new_v_a_kn': 'new_v', 'new_v_a_sink': 'new_v', 'new_v_b_qa_norm': 'new_v', 'new_v_b_kva_norm': 'new_v', 'new_v_b_w_uq': 'new_v', 'new_v_b_w_ukv': 'new_v', 'new_v_b_qn': 'new_v', 'new_v_b_kn': 'new_v', 'new_v_c_qn': 'new_v', 'new_v_c_kn': 'new_v', 'new_v_m_g_mem': 'new_v', 'new_v_m_w_kv': 'new_v', 'new_v_m_qn': 'new_v', 'new_v_m_kn': 'new_v', 'new_v_w_branch': 'new_v', 'new_v_w_out': 'new_v', 'new_v_g_mlp': 'new_v', 'new_v_w_up': 'new_v', 'new_v_w_down': 'new_v'}


def _forward(args):
    return _fwd_reference(*[args[k] for k in FWD_PARAMS])


def _output_shape():
    out = _jax.eval_shape(lambda: _forward(_fwd_setup_inputs(0)))
    return out.shape, out.dtype

N_MICROBATCH = 1
ADAM_LR = 0.001
ADAM_B1 = 0.9
ADAM_B2 = 0.999
ADAM_EPS = 1e-08
ADAM_WD = 0.01
ADAM_STEP = 10
PER_EXAMPLE_BATCH_AXIS = {'x': 0, 'mem': 0, 'positions': 0, 'loss_target': 0}
SHARED_INPUTS = []
_WEIGHT_DTYPES = {'g_mix': _jnp.float32, 'w_in': _jnp.float32, 'b_gate': _jnp.float32, 'a_qn': _jnp.float32, 'a_kn': _jnp.float32, 'a_sink': _jnp.float32, 'b_qa_norm': _jnp.float32, 'b_kva_norm': _jnp.float32, 'b_w_uq': _jnp.float32, 'b_w_ukv': _jnp.float32, 'b_qn': _jnp.float32, 'b_kn': _jnp.float32, 'c_qn': _jnp.float32, 'c_kn': _jnp.float32, 'm_g_mem': _jnp.float32, 'm_w_kv': _jnp.float32, 'm_qn': _jnp.float32, 'm_kn': _jnp.float32, 'w_branch': _jnp.float32, 'w_out': _jnp.float32, 'g_mlp': _jnp.float32, 'w_up': _jnp.float32, 'w_down': _jnp.float32}
MOMENT_SCALE = {'g_mix': 1.484054e+01, 'w_in': 4.666283e+00, 'b_gate': 2.875366e+00, 'a_qn': 1.301568e+00, 'a_kn': 1.245647e+00, 'a_sink': 9.772931e-01, 'b_qa_norm': 4.795318e-01, 'b_kva_norm': 2.061420e+01, 'b_w_uq': 3.127239e-01, 'b_w_ukv': 1.017259e+01, 'b_qn': 9.272376e-01, 'b_kn': 9.457911e-01, 'c_qn': 1.436651e+00, 'c_kn': 1.410804e+00, 'm_g_mem': 1.286546e+00, 'm_w_kv': 1.214602e+00, 'm_qn': 1.290160e+00, 'm_kn': 1.291542e+00, 'w_branch': 7.112060e+00, 'w_out': 1.370204e+01, 'g_mlp': 1.020432e+02, 'w_up': 9.977128e+00, 'w_down': 3.447124e+01}


def _to_microbatches(a, axis):
    t = _jnp.moveaxis(a, axis, 0)
    t = t.reshape((N_MICROBATCH, t.shape[0] // N_MICROBATCH) + t.shape[1:])
    return _jnp.moveaxis(t, 1, axis + 1)


def setup_inputs(seed: int = 0) -> dict:
    inp = _fwd_setup_inputs(seed)
    key = _jax.random.fold_in(_jax.random.key(seed), 7919)
    shape, _ = _output_shape()
    out = dict(inp)
    out["loss_target"] = _jax.random.normal(_jax.random.fold_in(key, 0), shape, _jnp.float32)
    for i, name in enumerate(TWIN_WEIGHTS):
        w = inp[name].astype(_jnp.float32)
        if MOMENT_SCALE is None:
            s = _jnp.sqrt(_jnp.mean(_jnp.square(w)) + 1e-30)
        else:
            s = MOMENT_SCALE[name]
        km, kv = _jax.random.split(_jax.random.fold_in(key, i + 1))
        out[name] = w
        out["m_" + name] = s * _jax.random.normal(km, w.shape, _jnp.float32)
        out["v_" + name] = (s * s) * _jax.random.uniform(kv, w.shape, _jnp.float32, 0.5, 1.5)
    if N_MICROBATCH > 1:
        for name, axis in PER_EXAMPLE_BATCH_AXIS.items():
            out[name] = _to_microbatches(out[name], axis)
    return {'x': out['x'], 'mem': out['mem'], 'positions': out['positions'], 'g_mix': out['g_mix'], 'w_in': out['w_in'], 'b_gate': out['b_gate'], 'a_qn': out['a_qn'], 'a_kn': out['a_kn'], 'a_sink': out['a_sink'], 'b_qa_norm': out['b_qa_norm'], 'b_kva_norm': out['b_kva_norm'], 'b_w_uq': out['b_w_uq'], 'b_w_ukv': out['b_w_ukv'], 'b_qn': out['b_qn'], 'b_kn': out['b_kn'], 'c_qn': out['c_qn'], 'c_kn': out['c_kn'], 'm_g_mem': out['m_g_mem'], 'm_w_kv': out['m_w_kv'], 'm_qn': out['m_qn'], 'm_kn': out['m_kn'], 'w_branch': out['w_branch'], 'w_out': out['w_out'], 'g_mlp': out['g_mlp'], 'w_up': out['w_up'], 'w_down': out['w_down'], 'loss_target': out['loss_target'], 'm_g_mix': out['m_g_mix'], 'm_w_in': out['m_w_in'], 'm_b_gate': out['m_b_gate'], 'm_a_qn': out['m_a_qn'], 'm_a_kn': out['m_a_kn'], 'm_a_sink': out['m_a_sink'], 'm_b_qa_norm': out['m_b_qa_norm'], 'm_b_kva_norm': out['m_b_kva_norm'], 'm_b_w_uq': out['m_b_w_uq'], 'm_b_w_ukv': out['m_b_w_ukv'], 'm_b_qn': out['m_b_qn'], 'm_b_kn': out['m_b_kn'], 'm_c_qn': out['m_c_qn'], 'm_c_kn': out['m_c_kn'], 'm_m_g_mem': out['m_m_g_mem'], 'm_m_w_kv': out['m_m_w_kv'], 'm_m_qn': out['m_m_qn'], 'm_m_kn': out['m_m_kn'], 'm_w_branch': out['m_w_branch'], 'm_w_out': out['m_w_out'], 'm_g_mlp': out['m_g_mlp'], 'm_w_up': out['m_w_up'], 'm_w_down': out['m_w_down'], 'v_g_mix': out['v_g_mix'], 'v_w_in': out['v_w_in'], 'v_b_gate': out['v_b_gate'], 'v_a_qn': out['v_a_qn'], 'v_a_kn': out['v_a_kn'], 'v_a_sink': out['v_a_sink'], 'v_b_qa_norm': out['v_b_qa_norm'], 'v_b_kva_norm': out['v_b_kva_norm'], 'v_b_w_uq': out['v_b_w_uq'], 'v_b_w_ukv': out['v_b_w_ukv'], 'v_b_qn': out['v_b_qn'], 'v_b_kn': out['v_b_kn'], 'v_c_qn': out['v_c_qn'], 'v_c_kn': out['v_c_kn'], 'v_m_g_mem': out['v_m_g_mem'], 'v_m_w_kv': out['v_m_w_kv'], 'v_m_qn': out['v_m_qn'], 'v_m_kn': out['v_m_kn'], 'v_w_branch': out['v_w_branch'], 'v_w_out': out['v_w_out'], 'v_g_mlp': out['v_g_mlp'], 'v_w_up': out['v_w_up'], 'v_w_down': out['v_w_down']}


def _loss(weights, diff, rest, loss_target):
    with _jax.named_scope("forward"):
        args = {**rest, TWIN_DIFF_INPUT: diff, **{k: w.astype(_WEIGHT_DTYPES[k]) for k, w in weights.items()}}
        y = _forward(args)
    with _jax.named_scope("loss_head"):
        err = _jnp.square(y.astype(_jnp.float32) - loss_target)
        return 0.5 * _jnp.sum(_jnp.mean(err, axis=-1)) if err.ndim else 0.5 * err


def _adamw(w, g, m, v):
    m = ADAM_B1 * m + (1.0 - ADAM_B1) * g
    v = ADAM_B2 * v + (1.0 - ADAM_B2) * _jnp.square(g)
    m_hat = m / (1.0 - ADAM_B1 ** ADAM_STEP)
    v_hat = v / (1.0 - ADAM_B2 ** ADAM_STEP)
    delta = -ADAM_LR * (m_hat / (_jnp.sqrt(v_hat) + ADAM_EPS) + ADAM_WD * w)
    return delta, m, v


def reference(x, mem, positions, g_mix, w_in, b_gate, a_qn, a_kn, a_sink, b_qa_norm, b_kva_norm, b_w_uq, b_w_ukv, b_qn, b_kn, c_qn, c_kn, m_g_mem, m_w_kv, m_qn, m_kn, w_branch, w_out, g_mlp, w_up, w_down, loss_target, m_g_mix, m_w_in, m_b_gate, m_a_qn, m_a_kn, m_a_sink, m_b_qa_norm, m_b_kva_norm, m_b_w_uq, m_b_w_ukv, m_b_qn, m_b_kn, m_c_qn, m_c_kn, m_m_g_mem, m_m_w_kv, m_m_qn, m_m_kn, m_w_branch, m_w_out, m_g_mlp, m_w_up, m_w_down, v_g_mix, v_w_in, v_b_gate, v_a_qn, v_a_kn, v_a_sink, v_b_qa_norm, v_b_kva_norm, v_b_w_uq, v_b_w_ukv, v_b_qn, v_b_kn, v_c_qn, v_c_kn, v_m_g_mem, v_m_w_kv, v_m_qn, v_m_kn, v_w_branch, v_w_out, v_g_mlp, v_w_up, v_w_down):
    given = dict(x=x, mem=mem, positions=positions, g_mix=g_mix, w_in=w_in, b_gate=b_gate, a_qn=a_qn, a_kn=a_kn, a_sink=a_sink, b_qa_norm=b_qa_norm, b_kva_norm=b_kva_norm, b_w_uq=b_w_uq, b_w_ukv=b_w_ukv, b_qn=b_qn, b_kn=b_kn, c_qn=c_qn, c_kn=c_kn, m_g_mem=m_g_mem, m_w_kv=m_w_kv, m_qn=m_qn, m_kn=m_kn, w_branch=w_branch, w_out=w_out, g_mlp=g_mlp, w_up=w_up, w_down=w_down, loss_target=loss_target, m_g_mix=m_g_mix, m_w_in=m_w_in, m_b_gate=m_b_gate, m_a_qn=m_a_qn, m_a_kn=m_a_kn, m_a_sink=m_a_sink, m_b_qa_norm=m_b_qa_norm, m_b_kva_norm=m_b_kva_norm, m_b_w_uq=m_b_w_uq, m_b_w_ukv=m_b_w_ukv, m_b_qn=m_b_qn, m_b_kn=m_b_kn, m_c_qn=m_c_qn, m_c_kn=m_c_kn, m_m_g_mem=m_m_g_mem, m_m_w_kv=m_m_w_kv, m_m_qn=m_m_qn, m_m_kn=m_m_kn, m_w_branch=m_w_branch, m_w_out=m_w_out, m_g_mlp=m_g_mlp, m_w_up=m_w_up, m_w_down=m_w_down, v_g_mix=v_g_mix, v_w_in=v_w_in, v_b_gate=v_b_gate, v_a_qn=v_a_qn, v_a_kn=v_a_kn, v_a_sink=v_a_sink, v_b_qa_norm=v_b_qa_norm, v_b_kva_norm=v_b_kva_norm, v_b_w_uq=v_b_w_uq, v_b_w_ukv=v_b_w_ukv, v_b_qn=v_b_qn, v_b_kn=v_b_kn, v_c_qn=v_c_qn, v_c_kn=v_c_kn, v_m_g_mem=v_m_g_mem, v_m_w_kv=v_m_w_kv, v_m_qn=v_m_qn, v_m_kn=v_m_kn, v_w_branch=v_w_branch, v_w_out=v_w_out, v_g_mlp=v_g_mlp, v_w_up=v_w_up, v_w_down=v_w_down)
    weights = {n: given[n] for n in TWIN_WEIGHTS}
    shared = {n: given[n] for n in SHARED_INPUTS}
    per_example = {n: given[n] for n in ['x', 'mem', 'positions']}
    grad_fn = _jax.value_and_grad(_loss, argnums=(0, 1))

    def one_microbatch(ex, loss_target):
        ex = dict(ex)
        diff = ex.pop(TWIN_DIFF_INPUT)
        return grad_fn(weights, diff, {**shared, **ex}, loss_target)

    if N_MICROBATCH == 1:
        loss, (grad_w, grad_x) = one_microbatch(per_example, given["loss_target"])
    else:
        def body(carry, xs):
            loss_sum, grad_sum = carry
            l_k, (gw_k, gx_k) = one_microbatch(xs[0], xs[1])
            with _jax.named_scope("update"):
                return (loss_sum + l_k, _jax.tree.map(_jnp.add, grad_sum, gw_k)), gx_k

        init = (_jnp.zeros((), _jnp.float32), _jax.tree.map(_jnp.zeros_like, weights))
        (loss, grad_w), grad_x = _jax.lax.scan(body, init, (per_example, given["loss_target"]))
    with _jax.named_scope("update"):
        delta_w, new_m, new_v = {}, {}, {}
        for n in TWIN_WEIGHTS:
            delta_w[n], new_m[n], new_v[n] = _adamw(weights[n], grad_w[n], given["m_" + n], given["v_" + n])
    return (loss, grad_x, *[grad_w[n] for n in TWIN_WEIGHTS], *[delta_w[n] for n in TWIN_WEIGHTS],
            *[new_m[n] for n in TWIN_WEIGHTS], *[new_v[n] for n in TWIN_WEIGHTS])
```

```python
import functools
import math

import jax
import jax.numpy as jnp
import numpy as np
from jax import lax
from jax.experimental import pallas as pl
from jax.experimental.pallas import tpu as pltpu

F32 = jnp.float32
BF16 = jnp.bfloat16

DEPTH = 4
BLOCK = 128
ROPE_THETA = 10000.0
EPS = 1e-6
NEG = -1e30
A_HEADS, A_KV_HEADS, A_HD, A_WINDOW = 8, 2, 64, 128
B_HEADS, B_Q_LORA, B_KV_LORA, B_NOPE, B_ROPE, B_V = 8, 384, 256, 64, 32, 64
C_PATTERNS = ((128, 1), (512, 4), (2048, 16))
C_HEADS, C_HD = 8, 64
M_HEADS, M_HD = 4, 128
N_BRANCH, BRANCH_W = 4, 512
ADAM_LR, ADAM_B1, ADAM_B2, ADAM_EPS, ADAM_WD, ADAM_STEP = 0.001, 0.9, 0.999, 1e-08, 0.01, 10

LANE = 128
VMEM_LIMIT = 56 * 1024 * 1024


def _pcall(body, **kw):
    return pl.pallas_call(body, **kw)


def _cparams(sem):
    return pltpu.CompilerParams(dimension_semantics=sem, vmem_limit_bytes=VMEM_LIMIT)


def _tile(n, target):
    if n <= target:
        return n
    best = None
    for t in range(LANE, target + 1, LANE):
        if n % t == 0:
            best = t
    return best if best is not None else n


def _mm(a, b, *, ta=False, tb=False, out_dtype=F32, pro_a=None, epi=None, extra=None, name,
        tm=512, tn=1024, tk=1024):
    if ta:
        K, M = a.shape
    else:
        M, K = a.shape
    if tb:
        N, K2 = b.shape
    else:
        K2, N = b.shape
    assert K == K2, (a.shape, b.shape, ta, tb)
    tm, tn, tk = _tile(M, tm), _tile(N, tn), _tile(K, tk)
    nk = K // tk
    a_spec = pl.BlockSpec((tk, tm), lambda i, j, k: (k, i)) if ta else pl.BlockSpec((tm, tk), lambda i, j, k: (i, k))
    b_spec = pl.BlockSpec((tn, tk), lambda i, j, k: (j, k)) if tb else pl.BlockSpec((tk, tn), lambda i, j, k: (k, j))
    o_spec = pl.BlockSpec((tm, tn), lambda i, j, k: (i, j))
    dims = (((0,) if ta else (1,), (1,) if tb else (0,)), ((), ()))
    has_extra = extra is not None

    def body(*refs):
        if has_extra:
            a_ref, b_ref, e_ref, o_ref, acc_ref = refs
        else:
            a_ref, b_ref, o_ref, acc_ref = refs
            e_ref = None
        k = pl.program_id(2)
        av = a_ref[...]
        if pro_a is not None:
            av = pro_a(av.astype(F32))
        part = lax.dot_general(av.astype(BF16), b_ref[...].astype(BF16), dims, preferred_element_type=F32)

        @pl.when(k == 0)
        def _():
            acc_ref[...] = part

        @pl.when(k > 0)
        def _():
            acc_ref[...] += part

        @pl.when(k == nk - 1)
        def _():
            r = acc_ref[...]
            if epi is not None:
                r = epi(r, e_ref[...]) if has_extra else epi(r)
            o_ref[...] = r.astype(out_dtype)

    in_specs = [a_spec, b_spec] + ([o_spec] if has_extra else [])
    args = (a, b) + ((extra,) if has_extra else ())
    return _pcall(
        body, name=name, grid=(M // tm, N // tn, nk), in_specs=in_specs, out_specs=o_spec,
        out_shape=jax.ShapeDtypeStruct((M, N), out_dtype),
        scratch_shapes=[pltpu.VMEM((tm, tn), F32)],
        compiler_params=_cparams(("parallel", "parallel", "arbitrary")),
    )(*args)


def _piece(p):
    if isinstance(p, tuple):
        return p
    return (p, p.shape[1], 0)


def _row_spec(width, idx, tb):
    return pl.BlockSpec((tb, width), lambda i, idx=idx: (i, idx))


def _full_spec(arr):
    nd = arr.ndim
    return pl.BlockSpec(arr.shape, lambda i, nd=nd: (0,) * nd)


def _rowmap(f, rows, params, outs, *, tb, name):
    rows = [_piece(p) for p in rows]
    R = rows[0][0].shape[0]
    tb = min(tb, R)
    nr, npar, nout = len(rows), len(params), len(outs)

    def body(*refs):
        rv = [r[...] for r in refs[:nr]]
        pv = [r[...] for r in refs[nr:nr + npar]]
        res = f(*rv, *pv)
        for o_ref, val in zip(refs[nr + npar:], res):
            o_ref[...] = val.astype(o_ref.dtype)

    return _pcall(
        body, name=name, grid=(R // tb,),
        in_specs=[_row_spec(w, idx, tb) for (_, w, idx) in rows] + [_full_spec(p) for p in params],
        out_specs=[_row_spec(w, 0, tb) for (w, _) in outs],
        out_shape=[jax.ShapeDtypeStruct((R, w), dt) for (w, dt) in outs],
        compiler_params=_cparams(("parallel",)),
    )(*[r[0] for r in rows], *params)


def _rowmap_bwd(f, rows, params, couts, *, diff, out_dtypes, adds=None, tb, name):
    rows = [_piece(p) for p in rows]
    couts = [_piece(p) for p in couts]
    R = rows[0][0].shape[0]
    tb = min(tb, R)
    nr, npar, nc = len(rows), len(params), len(couts)
    didx = [i for i, d in enumerate(diff) if d]
    adds = [None] * len(didx) if adds is None else adds
    add_ops = [_piece(a) for a in adds if a is not None]
    na = len(add_ops)

    def body(*refs):
        rv = [r[...] for r in refs[:nr]]
        pv = [r[...] for r in refs[nr:nr + npar]]
        cv = [r[...] for r in refs[nr + npar:nr + npar + nc]]
        av = [r[...] for r in refs[nr + npar + nc:nr + npar + nc + na]]
        o_refs = refs[nr + npar + nc + na:]
        drow_refs, dpar_refs = o_refs[:len(didx)], o_refs[len(didx):]
        nondiff = {i: rv[i] for i in range(nr) if not diff[i]}

        def g(*dv):
            full = []
            it = iter(dv[:len(didx)])
            for i in range(nr):
                full.append(nondiff[i] if i in nondiff else next(it))
            return f(*full, *dv[len(didx):])

        res, vjp = jax.vjp(g, *[rv[i].astype(F32) for i in didx], *pv)
        cts = tuple(c.astype(r.dtype) for c, r in zip(cv, res))
        grads = vjp(cts)
        ai = 0
        for n, o_ref in enumerate(drow_refs):
            val = grads[n]
            if adds[n] is not None:
                val = val + av[ai].astype(F32)
                ai += 1
            o_ref[...] = val.astype(o_ref.dtype)
        first = pl.program_id(0) == 0
        for n, o_ref in enumerate(dpar_refs):
            gp = grads[len(didx) + n].astype(F32)

            @pl.when(first)
            def _(o_ref=o_ref, gp=gp):
                o_ref[...] = gp

            @pl.when(jnp.logical_not(first))
            def _(o_ref=o_ref, gp=gp):
                o_ref[...] += gp

    outs = _pcall(
        body, name=name, grid=(R // tb,),
        in_specs=([_row_spec(w, idx, tb) for (_, w, idx) in rows] + [_full_spec(p) for p in params]
                  + [_row_spec(w, idx, tb) for (_, w, idx) in couts] + [_row_spec(w, idx, tb) for (_, w, idx) in add_ops]),
        out_specs=[_row_spec(rows[i][1], 0, tb) for i in didx] + [_full_spec(p) for p in params],
        out_shape=([jax.ShapeDtypeStruct((R, rows[i][1]), dt) for i, dt in zip(didx, out_dtypes)]
                   + [jax.ShapeDtypeStruct(p.shape, F32) for p in params]),
        compiler_params=_cparams(("arbitrary",)),
    )(*[r[0] for r in rows], *params, *[c[0] for c in couts], *[a[0] for a in add_ops])
    return outs[:len(didx)], outs[len(didx):]


@functools.partial(jax.custom_vjp, nondiff_argnums=(1,))
def _lane_roll(x, shift):
    return pltpu.roll(x, shift % x.shape[-1], axis=x.ndim - 1)


def _lane_roll_fwd(x, shift):
    return _lane_roll(x, shift), None


def _lane_roll_bwd(shift, _, g):
    return (_lane_roll(g, -shift),)


_lane_roll.defvjp(_lane_roll_fwd, _lane_roll_bwd)


def _group_matrix(width, group):
    r = lax.broadcasted_iota(jnp.int32, (width, width), 0) // group
    c = lax.broadcasted_iota(jnp.int32, (width, width), 1) // group
    return jnp.where(r == c, 1.0 / group, 0.0).astype(F32)


def _exact_dot(x, m):
    return jnp.dot(x, m, precision=lax.Precision.HIGHEST, preferred_element_type=F32)


def _head_norm(x, gain_tiled, group):
    ms = _exact_dot(x * x, _group_matrix(x.shape[-1], group))
    return x * lax.rsqrt(ms + EPS) * gain_tiled


def _row_norm(x, gain):
    ms = jnp.mean(x * x, axis=-1, keepdims=True)
    return x * lax.rsqrt(ms + EPS) * gain


def _rope(x, cos, sin, hd):
    half = hd // 2
    lane = lax.broadcasted_iota(jnp.int32, x.shape, x.ndim - 1) % hd
    other = jnp.where(lane < half, -_lane_roll(x, -half), _lane_roll(x, half))
    return x * cos + other * sin


class _AttnGeom:
    def __init__(self, mode, lq, lk, max_dist=0):
        self.mode, self.lq, self.lk, self.max_dist = mode, lq, lk, max_dist
        if mode == "band":
            self.bq = self.bk = BLOCK
            self.nt_q = 2
            self.nt_k = 2
        elif mode == "causal":
            self.bq = self.bk = min(256, lq)
            self.nt_q = lk // self.bk
            self.nt_k = lq // self.bq
        else:
            self.bq = min(256, lq)
            self.bk = lk
            self.nt_q = 1
            self.nt_k = lq // self.bq
        self.nq, self.nk = lq // self.bq, lk // self.bk

    def kv_block(self, i, t):
        if self.mode == "band":
            return jnp.maximum(i - t, 0)
        if self.mode == "causal":
            return jnp.minimum(t, i)
        return 0 * i

    def kv_active(self, i, t):
        if self.mode == "band":
            return i - t >= 0
        if self.mode == "causal":
            return t <= i
        return None

    def q_block(self, j, t):
        if self.mode == "band":
            return jnp.minimum(j + t, self.nq - 1)
        if self.mode == "causal":
            return jnp.maximum(t, j)
        return t

    def q_active(self, j, t):
        if self.mode == "band":
            return j + t <= self.nq - 1
        if self.mode == "causal":
            return t >= j
        return None

    def mask(self, qb, kb):
        if self.mode == "full":
            return None
        qp = qb * self.bq + lax.broadcasted_iota(jnp.int32, (self.bq, self.bk), 0)
        kp = kb * self.bk + lax.broadcasted_iota(jnp.int32, (self.bq, self.bk), 1)
        d = qp - kp
        if self.mode == "band":
            return (d >= 0) & (d <= self.max_dist)
        return d >= 0


def _when(cond, fn):
    if cond is None:
        fn()
    else:
        pl.when(cond)(fn)


def _dil_view(p, dil):
    arr, w, idx = _piece(p)
    R, C = arr.shape
    assert C % w == 0, (C, w)
    return arr.reshape(R // dil, dil * C), w, idx, C // w


def _seq_spec(view, rows, blk_fn):
    _, w, idx, cpw = view
    return pl.BlockSpec((rows, w), lambda s, i, t: (blk_fn(i, t), s * cpw + idx))


_NT = (((1,), (1,)), ((), ()))
_TN = (((0,), (0,)), ((), ()))


def _scores(geom, scale, q, k, qp, kp, h, g, hd, rope, qb, kb):
    s = lax.dot_general(q[:, h * hd:(h + 1) * hd], k[:, g * hd:(g + 1) * hd], _NT, preferred_element_type=F32)
    if rope:
        s = s + lax.dot_general(qp[:, h * rope:(h + 1) * rope], kp[:, :rope], _NT, preferred_element_type=F32)
    s = s * scale
    m = geom.mask(qb, kb)
    return s, m


def _attn_fwd(geom, q, k, v, *, hq, hk, hd, hdv, scale, dil=1, qp=None, kp=None, rope=0, name):
    qv, kv, vv = _dil_view(q, dil), _dil_view(k, dil), _dil_view(v, dil)
    R = _piece(q)[0].shape[0]
    grp = hq // hk
    bq, bk, nt = geom.bq, geom.bk, geom.nt_q
    ops = [qv, kv, vv]
    specs = [_seq_spec(qv, bq, lambda i, t: i), _seq_spec(kv, bk, geom.kv_block), _seq_spec(vv, bk, geom.kv_block)]
    if rope:
        qpv, kpv = _dil_view(qp, dil), _dil_view(kp, dil)
        ops += [qpv, kpv]
        specs += [_seq_spec(qpv, bq, lambda i, t: i), _seq_spec(kpv, bk, geom.kv_block)]
    ow = hq * hdv
    o_view = (None, ow, 0, 1)
    o_spec = pl.BlockSpec((bq, ow), lambda s, i, t: (i, s))

    def body(*refs):
        if rope:
            q_ref, k_ref, v_ref, qp_ref, kp_ref, o_ref, lse_ref, m_sc, l_sc, acc_sc = refs
        else:
            q_ref, k_ref, v_ref, o_ref, lse_ref, m_sc, l_sc, acc_sc = refs
            qp_ref = kp_ref = None
        i, t = pl.program_id(1), pl.program_id(2)

        @pl.when(t == 0)
        def _():
            m_sc[...] = jnp.full(m_sc.shape, NEG, F32)
            l_sc[...] = jnp.zeros(l_sc.shape, F32)
            acc_sc[...] = jnp.zeros(acc_sc.shape, F32)

        def step():
            qa, ka, va = q_ref[...].astype(BF16), k_ref[...].astype(BF16), v_ref[...].astype(BF16)
            qpa = qp_ref[...].astype(BF16) if rope else None
            kpa = kp_ref[...].astype(BF16) if rope else None
            kb = geom.kv_block(i, t)
            for h in range(hq):
                g = h // grp
                s, msk = _scores(geom, scale, qa, ka, qpa, kpa, h, g, hd, rope, i, kb)
                if msk is not None:
                    s = jnp.where(msk, s, NEG)
                m_old = m_sc[h]
                m_new = jnp.maximum(m_old, jnp.max(s, axis=1, keepdims=True))
                p = jnp.exp(s - m_new)
                alpha = jnp.exp(m_old - m_new)
                l_sc[h] = alpha * l_sc[h] + jnp.sum(p, axis=1, keepdims=True)
                pv = jnp.dot(p.astype(BF16), va[:, g * hdv:(g + 1) * hdv], preferred_element_type=F32)
                acc_sc[h] = alpha * acc_sc[h] + pv
                m_sc[h] = m_new

        _when(geom.kv_active(i, t), step)

        @pl.when(t == nt - 1)
        def _():
            for h in range(hq):
                l = l_sc[h]
                o_ref[:, h * hdv:(h + 1) * hdv] = acc_sc[h] / l
                lse_ref[:, h * hdv:(h + 1) * hdv] = jnp.broadcast_to(m_sc[h] + jnp.log(l), (bq, hdv))

    o, lse = _pcall(
        body, name=name, grid=(dil, geom.nq, nt), in_specs=specs, out_specs=[o_spec, o_spec],
        out_shape=[jax.ShapeDtypeStruct((R // dil, dil * ow), F32)] * 2,
        scratch_shapes=[pltpu.VMEM((hq, bq, 1), F32), pltpu.VMEM((hq, bq, 1), F32), pltpu.VMEM((hq, bq, hdv), F32)],
        compiler_params=_cparams(("parallel", "parallel", "arbitrary")),
    )(*[o_[0] for o_ in ops])
    return o.reshape(R, ow), lse.reshape(R, ow)


def _attn_dq(geom, q, k, v, do, o, lse, dlse, *, hq, hk, hd, hdv, scale, dil=1, qp=None, kp=None, rope=0,
             out_dtype=F32, name):
    qv, kv, vv = _dil_view(q, dil), _dil_view(k, dil), _dil_view(v, dil)
    dov, ov, lv = _dil_view(do, dil), _dil_view(o, dil), _dil_view(lse, dil)
    R = _piece(q)[0].shape[0]
    grp = hq // hk
    bq, bk, nt = geom.bq, geom.bk, geom.nt_q
    qi = lambda i, t: i
    ops = [qv, kv, vv, dov, ov, lv]
    specs = [_seq_spec(qv, bq, qi), _seq_spec(kv, bk, geom.kv_block), _seq_spec(vv, bk, geom.kv_block),
             _seq_spec(dov, bq, qi), _seq_spec(ov, bq, qi), _seq_spec(lv, bq, qi)]
    has_dl = dlse is not None
    if has_dl:
        dlv = _dil_view(dlse, dil)
        ops.append(dlv)
        specs.append(_seq_spec(dlv, bq, qi))
    if rope:
        qpv, kpv = _dil_view(qp, dil), _dil_view(kp, dil)
        ops += [qpv, kpv]
        specs += [_seq_spec(qpv, bq, qi), _seq_spec(kpv, bk, geom.kv_block)]
    qw = hq * hd
    out_specs = [pl.BlockSpec((bq, qw), lambda s, i, t: (i, s))]
    out_shape = [jax.ShapeDtypeStruct((R // dil, dil * qw), out_dtype)]
    scratch = [pltpu.VMEM((hq, bq, 1), F32), pltpu.VMEM((hq, bq, hd), F32)]
    if rope:
        out_specs.append(pl.BlockSpec((bq, hq * rope), lambda s, i, t: (i, s)))
        out_shape.append(jax.ShapeDtypeStruct((R // dil, dil * hq * rope), out_dtype))
        scratch.append(pltpu.VMEM((hq, bq, rope), F32))

    def body(*refs):
        refs = list(refs)
        q_ref, k_ref, v_ref, do_ref, o_ref, l_ref = refs[:6]
        pos = 6
        dl_ref = None
        if has_dl:
            dl_ref = refs[pos]
            pos += 1
        qp_ref = kp_ref = None
        if rope:
            qp_ref, kp_ref = refs[pos:pos + 2]
            pos += 2
        dq_ref = refs[pos]
        pos += 1
        dqp_ref = None
        if rope:
            dqp_ref = refs[pos]
            pos += 1
        dl_sc, dq_sc = refs[pos:pos + 2]
        dqp_sc = refs[pos + 2] if rope else None
        i, t = pl.program_id(1), pl.program_id(2)

        @pl.when(t == 0)
        def _():
            dov_, ov_ = do_ref[...].astype(F32), o_ref[...].astype(F32)
            prod = dov_ * ov_
            for h in range(hq):
                d = jnp.sum(prod[:, h * hdv:(h + 1) * hdv], axis=1, keepdims=True)
                if has_dl:
                    d = d - jnp.sum(dl_ref[:, h * hdv:(h + 1) * hdv].astype(F32), axis=1, keepdims=True)
                dl_sc[h] = d
            dq_sc[...] = jnp.zeros(dq_sc.shape, F32)
            if rope:
                dqp_sc[...] = jnp.zeros(dqp_sc.shape, F32)

        def step():
            qa, ka, va = q_ref[...].astype(BF16), k_ref[...].astype(BF16), v_ref[...].astype(BF16)
            doa = do_ref[...].astype(BF16)
            qpa = qp_ref[...].astype(BF16) if rope else None
            kpa = kp_ref[...].astype(BF16) if rope else None
            kb = geom.kv_block(i, t)
            for h in range(hq):
                g = h // grp
                s, msk = _scores(geom, scale, qa, ka, qpa, kpa, h, g, hd, rope, i, kb)
                p = jnp.exp(s - l_ref[:, h * hdv:h * hdv + 1])
                if msk is not None:
                    p = jnp.where(msk, p, 0.0)
                dp = lax.dot_general(doa[:, h * hdv:(h + 1) * hdv], va[:, g * hdv:(g + 1) * hdv], _NT,
                                     preferred_element_type=F32)
                ds = (p * (dp - dl_sc[h]) * scale).astype(BF16)
                dq_sc[h] += jnp.dot(ds, ka[:, g * hd:(g + 1) * hd], preferred_element_type=F32)
                if rope:
                    dqp_sc[h] += jnp.dot(ds, kpa[:, :rope], preferred_element_type=F32)

        _when(geom.kv_active(i, t), step)

        @pl.when(t == nt - 1)
        def _():
            for h in range(hq):
                dq_ref[:, h * hd:(h + 1) * hd] = dq_sc[h].astype(dq_ref.dtype)
                if rope:
                    dqp_ref[:, h * rope:(h + 1) * rope] = dqp_sc[h].astype(dqp_ref.dtype)

    outs = _pcall(
        body, name=name, grid=(dil, geom.nq, nt), in_specs=specs, out_specs=out_specs, out_shape=out_shape,
        scratch_shapes=scratch, compiler_params=_cparams(("parallel", "parallel", "arbitrary")),
    )(*[o_[0] for o_ in ops])
    dq = outs[0].reshape(R, qw)
    if rope:
        return dq, outs[1].reshape(R, hq * rope)
    return dq


def _attn_dkv(geom, q, k, v, do, o, lse, dlse, *, hq, hk, hd, hdv, scale, dil=1, qp=None, kp=None, rope=0,
              out_dtype=F32, name):
    qv, kv, vv = _dil_view(q, dil), _dil_view(k, dil), _dil_view(v, dil)
    dov, ov, lv = _dil_view(do, dil), _dil_view(o, dil), _dil_view(lse, dil)
    Rk = _piece(k)[0].shape[0]
    grp = hq // hk
    bq, bk, nt = geom.bq, geom.bk, geom.nt_k
    kj = lambda j, t: j
    ops = [qv, kv, vv, dov, ov, lv]
    specs = [_seq_spec(qv, bq, geom.q_block), _seq_spec(kv, bk, kj), _seq_spec(vv, bk, kj),
             _seq_spec(dov, bq, geom.q_block), _seq_spec(ov, bq, geom.q_block), _seq_spec(lv, bq, geom.q_block)]
    has_dl = dlse is not None
    if has_dl:
        dlv = _dil_view(dlse, dil)
        ops.append(dlv)
        specs.append(_seq_spec(dlv, bq, geom.q_block))
    if rope:
        qpv, kpv = _dil_view(qp, dil), _dil_view(kp, dil)
        ops += [qpv, kpv]
        specs += [_seq_spec(qpv, bq, geom.q_block), _seq_spec(kpv, bk, kj)]
    kw, vw = hk * hd, hk * hdv
    out_specs = [pl.BlockSpec((bk, kw), lambda s, j, t: (j, s)), pl.BlockSpec((bk, vw), lambda s, j, t: (j, s))]
    out_shape = [jax.ShapeDtypeStruct((Rk // dil, dil * kw), out_dtype), jax.ShapeDtypeStruct((Rk // dil, dil * vw), out_dtype)]
    scratch = [pltpu.VMEM((hk, bk, hd), F32), pltpu.VMEM((hk, bk, hdv), F32)]
    if rope:
        out_specs.append(pl.BlockSpec((bk, LANE), lambda s, j, t: (j, s)))
        out_shape.append(jax.ShapeDtypeStruct((Rk // dil, dil * LANE), out_dtype))
        scratch.append(pltpu.VMEM((bk, rope), F32))

    def body(*refs):
        refs = list(refs)
        q_ref, k_ref, v_ref, do_ref, o_ref, l_ref = refs[:6]
        pos = 6
        dl_ref = None
        if has_dl:
            dl_ref = refs[pos]
            pos += 1
        qp_ref = kp_ref = None
        if rope:
            qp_ref, kp_ref = refs[pos:pos + 2]
            pos += 2
        dk_ref, dv_ref = refs[pos:pos + 2]
        pos += 2
        dkp_ref = None
        if rope:
            dkp_ref = refs[pos]
            pos += 1
        dk_sc, dv_sc = refs[pos:pos + 2]
        dkp_sc = refs[pos + 2] if rope else None
        j, t = pl.program_id(1), pl.program_id(2)

        @pl.when(t == 0)
        def _():
            dk_sc[...] = jnp.zeros(dk_sc.shape, F32)
            dv_sc[...] = jnp.zeros(dv_sc.shape, F32)
            if rope:
                dkp_sc[...] = jnp.zeros(dkp_sc.shape, F32)

        def step():
            qa, ka, va = q_ref[...].astype(BF16), k_ref[...].astype(BF16), v_ref[...].astype(BF16)
            dof = do_ref[...].astype(F32)
            doa = dof.astype(BF16)
            prod = dof * o_ref[...].astype(F32)
            qpa = qp_ref[...].astype(BF16) if rope else None
            kpa = kp_ref[...].astype(BF16) if rope else None
            qb = geom.q_block(j, t)
            for h in range(hq):
                g = h // grp
                s, msk = _scores(geom, scale, qa, ka, qpa, kpa, h, g, hd, rope, qb, j)
                p = jnp.exp(s - l_ref[:, h * hdv:h * hdv + 1])
                if msk is not None:
                    p = jnp.where(msk, p, 0.0)
                delta = jnp.sum(prod[:, h * hdv:(h + 1) * hdv], axis=1, keepdims=True)
                if has_dl:
                    delta = delta - jnp.sum(dl_ref[:, h * hdv:(h + 1) * hdv].astype(F32), axis=1, keepdims=True)
                do_h = doa[:, h * hdv:(h + 1) * hdv]
                dv_sc[g] += lax.dot_general(p.astype(BF16), do_h, _TN, preferred_element_type=F32)
                dp = lax.dot_general(do_h, va[:, g * hdv:(g + 1) * hdv], _NT, preferred_element_type=F32)
                ds = (p * (dp - delta) * scale).astype(BF16)
                dk_sc[g] += lax.dot_general(ds, qa[:, h * hd:(h + 1) * hd], _TN, preferred_element_type=F32)
                if rope:
                    dkp_sc[...] += lax.dot_general(ds, qpa[:, h * rope:(h + 1) * rope], _TN, preferred_element_type=F32)

        _when(geom.q_active(j, t), step)

        @pl.when(t == nt - 1)
        def _():
            for g in range(hk):
                dk_ref[:, g * hd:(g + 1) * hd] = dk_sc[g].astype(dk_ref.dtype)
                dv_ref[:, g * hdv:(g + 1) * hdv] = dv_sc[g].astype(dv_ref.dtype)
            if rope:
                dkp_ref[...] = jnp.zeros(dkp_ref.shape, dkp_ref.dtype)
                dkp_ref[:, :rope] = dkp_sc[...].astype(dkp_ref.dtype)

    outs = _pcall(
        body, name=name, grid=(dil, geom.nk, nt), in_specs=specs, out_specs=out_specs, out_shape=out_shape,
        scratch_shapes=scratch, compiler_params=_cparams(("parallel", "parallel", "arbitrary")),
    )(*[o_[0] for o_ in ops])
    dk, dv = outs[0].reshape(Rk, kw), outs[1].reshape(Rk, vw)
    if rope:
        return dk, dv, outs[2].reshape(Rk, LANE)
    return dk, dv


@jax.custom_vjp
def _bdot(x, w):
    return jnp.dot(x.astype(BF16), w.astype(BF16), preferred_element_type=F32)


def _bdot_fwd(x, w):
    return _bdot(x, w), (x, w)


def _bdot_bwd(res, g):
    x, w = res
    gb = g.astype(BF16)
    dx = lax.dot_general(gb, w.astype(BF16), _NT, preferred_element_type=F32)
    dw = lax.dot_general(x.astype(BF16), gb, _TN, preferred_element_type=F32)
    return dx.astype(x.dtype), dw.astype(w.dtype)


_bdot.defvjp(_bdot_fwd, _bdot_bwd)


def _tile_matrix(hd, width):
    r = lax.broadcasted_iota(jnp.int32, (hd, width), 0)
    c = lax.broadcasted_iota(jnp.int32, (hd, width), 1) % hd
    return jnp.where(r == c, 1.0, 0.0).astype(F32)


def _spread_matrix(heads, width):
    per = width // heads
    r = lax.broadcasted_iota(jnp.int32, (heads, width), 0)
    c = lax.broadcasted_iota(jnp.int32, (heads, width), 1) // per
    return jnp.where(r == c, 1.0, 0.0).astype(F32)


def _wide(t, width):
    n = width // t.shape[-1]
    return jnp.concatenate([t] * n, axis=1) if n > 1 else t


def _norm_heads(x, gain, hd):
    return _head_norm(x, _exact_dot(gain, _tile_matrix(hd, x.shape[-1])), hd)


def _norm_rope(x, gain, cos, sin, hd):
    w = x.shape[-1]
    return _rope(_norm_heads(x, gain, hd), _wide(cos, w), _wide(sin, w), hd)


def _f_norm(x, g):
    return (_row_norm(x, g),)


def _f_prep_acm(aq, ak, c0q, c0k, c1q, c1k, c2q, c2k, mq, cos, sin, g_aq, g_ak, g0q, g0k, g1q, g1k, g2q, g2k, g_mq):
    outs = [_norm_rope(aq, g_aq, cos, sin, A_HD), _norm_rope(ak, g_ak, cos, sin, A_HD)]
    for cx, gx in ((c0q, g0q), (c0k, g0k), (c1q, g1q), (c1k, g1k), (c2q, g2q), (c2k, g2k)):
        outs.append(_norm_rope(cx, gx, cos, sin, C_HD))
    outs.append(_norm_heads(mq, g_mq, M_HD))
    return tuple(outs)


def _f_prep_b(ckv, cq, kr, cos, sin, g_qa, g_kva, w_uq, w_ukv, g_qn, g_qr, g_kn, g_kr):
    nq = B_HEADS * B_NOPE
    q_up = _bdot(_row_norm(cq, g_qa), w_uq)
    qn = _norm_heads(q_up[:, :nq], g_qn, B_NOPE)
    qp = _norm_rope(q_up[:, nq:], g_qr, cos, sin, B_ROPE)
    kv_up = _bdot(_row_norm(ckv, g_kva), w_ukv)
    kn = _norm_heads(kv_up[:, :nq], g_kn, B_NOPE)
    vb = kv_up[:, nq:]
    kp = _norm_rope(kr, g_kr, cos, sin, B_ROPE)
    return qn, qp, kn, vb, kp


def _f_mem_k(k, g):
    return (_norm_heads(k, g, M_HD),)


def _f_sink(o, lse, sink):
    sb = _exact_dot(sink, _spread_matrix(A_HEADS, o.shape[-1]))
    m = jnp.maximum(lse, sb)
    tot = m + jnp.log(jnp.exp(lse - m) + jnp.exp(sb - m))
    return (o * jnp.exp(lse - tot),)


def _f_combine(o0, o1, o2, l0, l1, l2):
    m = jnp.maximum(jnp.maximum(l0, l1), l2)
    e0, e1, e2 = jnp.exp(l0 - m), jnp.exp(l1 - m), jnp.exp(l2 - m)
    inv = 1.0 / (e0 + e1 + e2)
    return ((e0 * o0 + e1 * o1 + e2 * o2) * inv,)


def _f_gatemix(gp, y0, y1, y2, y3, bg):
    d = y0.shape[-1]
    gates = 1.0 / (1.0 + jnp.exp(-(gp + bg)))
    mix = gates[:, :d] * y0
    for n, y in enumerate((y1, y2, y3), start=1):
        mix = mix + gates[:, n * d:(n + 1) * d] * y
    return (mix,)


def _relu2(u):
    return jnp.square(jnp.maximum(u, 0.0))


def _add(r, e):
    return r + e.astype(F32)


def _relu2_grad(r, u):
    return r * (2.0 * jnp.maximum(u, 0.0))


def _loss_and_grad(y, target, *, tb=512):
    R, D = y.shape
    tb = min(tb, R)

    def body(y_ref, t_ref, dy_ref, l_ref):
        err = y_ref[...] - t_ref[...]
        dy_ref[...] = err * (1.0 / D)
        part = 0.5 * jnp.sum(jnp.sum(err * err, axis=1, keepdims=True) * (1.0 / D), axis=0, keepdims=True)
        first = pl.program_id(0) == 0

        @pl.when(first)
        def _():
            l_ref[...] = jnp.broadcast_to(part, l_ref.shape)

        @pl.when(jnp.logical_not(first))
        def _():
            l_ref[...] += jnp.broadcast_to(part, l_ref.shape)

    dy, l = _pcall(
        body, name="loss", grid=(R // tb,),
        in_specs=[pl.BlockSpec((tb, D), lambda i: (i, 0))] * 2,
        out_specs=[pl.BlockSpec((tb, D), lambda i: (i, 0)), pl.BlockSpec((8, LANE), lambda i: (0, 0))],
        out_shape=[jax.ShapeDtypeStruct((R, D), F32), jax.ShapeDtypeStruct((8, LANE), F32)],
        compiler_params=_cparams(("arbitrary",)),
    )(y, target)
    return l[0, 0], dy


def _z_layout(d):
    assert d == 1024, "the aligned layout below is laid out for D_MODEL = 1024"
    lay = {"gates": (4 * d, 0)}
    for g in range(3):
        for n, nm in enumerate("qkv"):
            lay[f"c{g}{nm}"] = (512, 8 + 3 * g + n)
    lay.update(aq=(512, 17), mq=(512, 18), ckv=(256, 38), cq=(384, 26), ak=(128, 81), av=(128, 82), kr=(128, 83))
    return lay, 10752


def _geoms(S):
    gA = _AttnGeom("band", S, S, A_WINDOW - 1)
    gB = _AttnGeom("causal", S, S)
    gC = [_AttnGeom("band", S // dil, S // dil, win // dil) for (win, dil) in C_PATTERNS]
    return gA, gB, gC


_KW_A = dict(hq=A_HEADS, hk=A_KV_HEADS, hd=A_HD, hdv=A_HD, scale=A_HD ** -0.5)
_KW_B = dict(hq=B_HEADS, hk=B_HEADS, hd=B_NOPE, hdv=B_V, scale=(B_NOPE + B_ROPE) ** -0.5, rope=B_ROPE)
_KW_C = dict(hq=C_HEADS, hk=C_HEADS, hd=C_HD, hdv=C_HD, scale=C_HD ** -0.5)
_KW_M = dict(hq=M_HEADS, hk=M_HEADS, hd=M_HD, hdv=M_HD, scale=M_HD ** -0.5)


def _layer_fwd(l, x, mem, w, tabs):
    S, D = x.shape
    lay, _ = _z_layout(D)
    cosA, sinA, cosB, sinB = tabs
    gA, gB, gC = _geoms(S)
    gM = _AttnGeom("full", S, mem.shape[0])
    nm = lambda s: f"l{l}_{s}"
    sv = {}
    hn = _rowmap(_f_norm, [x], [w["g_mix"]], [(D, BF16)], tb=512, name=nm("norm1"))[0]
    z = _mm(hn, w["in"], name=nm("in"), tn=768)
    zp = {k: (z, wd, idx) for k, (wd, idx) in lay.items()}
    acm_rows = [zp["aq"], zp["ak"], zp["c0q"], zp["c0k"], zp["c1q"], zp["c1k"], zp["c2q"], zp["c2k"], zp["mq"], cosA, sinA]
    acm_par = [w["a_qn"], w["a_kn"], w["c0q"], w["c0k"], w["c1q"], w["c1k"], w["c2q"], w["c2k"], w["m_qn"]]
    acm = _rowmap(_f_prep_acm, acm_rows, acm_par, [(p[1], BF16) for p in acm_rows[:9]], tb=256, name=nm("prep_acm"))
    qa, ka, qc0, kc0, qc1, kc1, qc2, kc2, mq = acm
    b_rows = [zp["ckv"], zp["cq"], zp["kr"], cosB, sinB]
    b_par = [w["b_qa"], w["b_kva"], w["uq"], w["ukv"], w["b_qn"], w["b_qr"], w["b_kn"], w["b_kr"]]
    qn, qp, kn, vb, kp = _rowmap(_f_prep_b, b_rows, b_par,
                                 [(512, BF16), (256, BF16), (512, BF16), (512, BF16), (LANE, BF16)], tb=256, name=nm("prep_b"))
    memn = _rowmap(_f_norm, [mem], [w["m_g_mem"]], [(D, BF16)], tb=256, name=nm("mem_norm"))[0]
    mkv = _mm(memn, w["mkv"], name=nm("mem_kv"))
    mk = _rowmap(_f_mem_k, [(mkv, 512, 0)], [w["m_kn"]], [(512, BF16)], tb=256, name=nm("mem_k"))[0]
    mv = (mkv, 512, 1)

    oa_raw, lse_a = _attn_fwd(gA, qa, ka, zp["av"], name=nm("attn_a"), **_KW_A)
    o_a = _rowmap(_f_sink, [oa_raw, lse_a], [w["a_sink"]], [(512, BF16)], tb=512, name=nm("sink"))[0]
    o_b, lse_b = _attn_fwd(gB, qn, kn, vb, qp=qp, kp=kp, name=nm("attn_b"), **_KW_B)
    oc, lc = [], []
    for g, ((win, dil), qc, kc) in enumerate(zip(C_PATTERNS, (qc0, qc1, qc2), (kc0, kc1, kc2))):
        o_g, l_g = _attn_fwd(gC[g], qc, kc, zp[f"c{g}v"], dil=dil, name=nm(f"attn_c{g}"), **_KW_C)
        oc.append(o_g)
        lc.append(l_g)
    o_c = _rowmap(_f_combine, oc + lc, [], [(512, BF16)], tb=512, name=nm("combine"))[0]
    o_m, lse_m = _attn_fwd(gM, mq, mk, mv, name=nm("attn_m"), **_KW_M)

    o_n = [o_a, o_b, o_c, o_m]
    ys = [_mm(o_n[n], w["branch"][n], name=nm(f"branch{n}")) for n in range(N_BRANCH)]
    mix = _rowmap(_f_gatemix, [zp["gates"]] + ys, [w["b_gate"]], [(D, BF16)], tb=256, name=nm("gatemix"))[0]
    x1 = _mm(mix, w["out"], extra=x, epi=_add, name=nm("out"))
    hn2 = _rowmap(_f_norm, [x1], [w["g_mlp"]], [(D, BF16)], tb=512, name=nm("norm2"))[0]
    u = _mm(hn2, w["up"], name=nm("up"))
    x2 = _mm(u, w["down"], pro_a=_relu2, extra=x1, epi=_add, name=nm("down"))
    sv.update(x=x, hn=hn, z=z, acm=acm, bq=(qn, qp, kn, vb, kp), memn=memn, mkv=mkv, mk=mk, oa_raw=oa_raw, lse_a=lse_a,
              o_b=o_b, lse_b=lse_b, oc=oc, lc=lc, o_m=o_m, lse_m=lse_m, o_n=o_n, ys=ys, mix=mix, x1=x1, hn2=hn2, u=u)
    return x2, sv


def _layer_bwd(l, dx2, mem, w, tabs, sv):
    x, z, x1, u = sv["x"], sv["z"], sv["x1"], sv["u"]
    S, D = x.shape
    lay, zw = _z_layout(D)
    cosA, sinA, cosB, sinB = tabs
    gA, gB, gC = _geoms(S)
    gM = _AttnGeom("full", S, mem.shape[0])
    nm = lambda s: f"l{l}_{s}"
    zp = {k: (z, wd, idx) for k, (wd, idx) in lay.items()}
    g = {}
    du = _mm(dx2, w["down"], tb=True, extra=u, epi=_relu2_grad, out_dtype=BF16, name=nm("d_down_x"))
    g["down"] = _mm(u, dx2, ta=True, pro_a=_relu2, name=nm("d_down_w"))
    dhn2 = _mm(du, w["up"], tb=True, name=nm("d_up_x"))
    g["up"] = _mm(sv["hn2"], du, ta=True, name=nm("d_up_w"))
    (dx1,), (g["g_mlp"],) = _rowmap_bwd(_f_norm, [x1], [w["g_mlp"]], [dhn2], diff=[True], out_dtypes=[F32], adds=[dx2],
                                        tb=256, name=nm("d_norm2"))
    dmix = _mm(dx1, w["out"], tb=True, name=nm("d_out_x"))
    g["out"] = _mm(sv["mix"], dx1, ta=True, name=nm("d_out_w"))
    (dgates, dy0, dy1, dy2, dy3), (g["b_gate"],) = _rowmap_bwd(
        _f_gatemix, [zp["gates"]] + sv["ys"], [w["b_gate"]], [dmix], diff=[True] * 5, out_dtypes=[BF16] * 5,
        tb=128, name=nm("d_gatemix"))
    dys = [dy0, dy1, dy2, dy3]
    do = [_mm(dys[n], w["branch"][n], tb=True, name=nm(f"d_branch{n}_x")) for n in range(N_BRANCH)]
    g["branch"] = [_mm(sv["o_n"][n], dys[n], ta=True, name=nm(f"d_branch{n}_w")) for n in range(N_BRANCH)]
    qa, ka, qc0, kc0, qc1, kc1, qc2, kc2, mq = sv["acm"]
    qn, qp, kn, vb, kp = sv["bq"]
    mkv, mk = sv["mkv"], sv["mk"]
    mv = (mkv, 512, 1)
    dmq = _attn_dq(gM, mq, mk, mv, do[3], sv["o_m"], sv["lse_m"], None, name=nm("attn_m_dq"), **_KW_M)
    dmk, dmv = _attn_dkv(gM, mq, mk, mv, do[3], sv["o_m"], sv["lse_m"], None, name=nm("attn_m_dkv"), **_KW_M)
    (doc0, doc1, doc2, dl0, dl1, dl2), _ = _rowmap_bwd(_f_combine, sv["oc"] + sv["lc"], [], [do[2]], diff=[True] * 6,
                                                      out_dtypes=[F32] * 6, tb=256, name=nm("d_combine"))
    dqc, dkc, dvc = [], [], []
    for gi, ((win, dil), qc, kc, doc, dl) in enumerate(zip(C_PATTERNS, (qc0, qc1, qc2), (kc0, kc1, kc2),
                                                          (doc0, doc1, doc2), (dl0, dl1, dl2))):
        args = (gC[gi], qc, kc, zp[f"c{gi}v"], doc, sv["oc"][gi], sv["lc"][gi], dl)
        dqc.append(_attn_dq(*args, dil=dil, name=nm(f"attn_c{gi}_dq"), **_KW_C))
        dk_, dv_ = _attn_dkv(*args, dil=dil, out_dtype=BF16, name=nm(f"attn_c{gi}_dkv"), **_KW_C)
        dkc.append(dk_)
        dvc.append(dv_)
    argsb = (gB, qn, kn, vb, do[1], sv["o_b"], sv["lse_b"], None)
    dqn, dqp = _attn_dq(*argsb, qp=qp, kp=kp, name=nm("attn_b_dq"), **_KW_B)
    dkn, dvb, dkp = _attn_dkv(*argsb, qp=qp, kp=kp, name=nm("attn_b_dkv"), **_KW_B)
    (doa_raw, dlse_a), (g["a_sink"],) = _rowmap_bwd(_f_sink, [sv["oa_raw"], sv["lse_a"]], [w["a_sink"]], [do[0]],
                                                   diff=[True, True], out_dtypes=[F32, F32], tb=256, name=nm("d_sink"))
    argsa = (gA, qa, ka, zp["av"], doa_raw, sv["oa_raw"], sv["lse_a"], dlse_a)
    dqa = _attn_dq(*argsa, name=nm("attn_a_dq"), **_KW_A)
    dka, dva = _attn_dkv(*argsa, out_dtype=BF16, name=nm("attn_a_dkv"), **_KW_A)
    acm_rows = [zp["aq"], zp["ak"], zp["c0q"], zp["c0k"], zp["c1q"], zp["c1k"], zp["c2q"], zp["c2k"], zp["mq"], cosA, sinA]
    acm_par = [w["a_qn"], w["a_kn"], w["c0q"], w["c0k"], w["c1q"], w["c1k"], w["c2q"], w["c2k"], w["m_qn"]]
    acm_ct = [dqa, dka, dqc[0], dkc[0], dqc[1], dkc[1], dqc[2], dkc[2], dmq]
    dacm, (g["a_qn"], g["a_kn"], g["c0q"], g["c0k"], g["c1q"], g["c1k"], g["c2q"], g["c2k"], g["m_qn"]) = _rowmap_bwd(
        _f_prep_acm, acm_rows, acm_par, acm_ct, diff=[True] * 9 + [False, False], out_dtypes=[BF16] * 9, tb=128,
        name=nm("d_prep_acm"))
    d_aq, d_ak, d_c0q, d_c0k, d_c1q, d_c1k, d_c2q, d_c2k, d_mq = dacm
    b_rows = [zp["ckv"], zp["cq"], zp["kr"], cosB, sinB]
    b_par = [w["b_qa"], w["b_kva"], w["uq"], w["ukv"], w["b_qn"], w["b_qr"], w["b_kn"], w["b_kr"]]
    (d_ckv, d_cq, d_kr), gb = _rowmap_bwd(_f_prep_b, b_rows, b_par, [dqn, dqp, dkn, dvb, dkp],
                                          diff=[True, True, True, False, False], out_dtypes=[BF16] * 3, tb=256, name=nm("d_prep_b"))
    g["b_qa"], g["b_kva"], g["uq"], g["ukv"], g["b_qn"], g["b_qr"], g["b_kn"], g["b_kr"] = gb
    (dmkv_k,), (g["m_kn"],) = _rowmap_bwd(_f_mem_k, [(mkv, 512, 0)], [w["m_kn"]], [dmk], diff=[True], out_dtypes=[F32],
                                          tb=256, name=nm("d_mem_k"))
    dmkv = jnp.concatenate([dmkv_k, dmv], axis=1)
    dmemn = _mm(dmkv, w["mkv"], tb=True, name=nm("d_mem_kv_x"))
    g["mkv"] = _mm(sv["memn"], dmkv, ta=True, name=nm("d_mem_kv_w"))
    _, (g["m_g_mem"],) = _rowmap_bwd(_f_norm, [mem], [w["m_g_mem"]], [dmemn], diff=[True], out_dtypes=[F32], tb=256,
                                     name=nm("d_mem_norm"))
    dz = jnp.concatenate([dgates, d_c0q, d_c0k, dvc[0], d_c1q, d_c1k, dvc[1], d_c2q, d_c2k, dvc[2], d_aq, d_mq, d_ckv, d_cq,
                          d_ak, dva, d_kr], axis=1)
    assert dz.shape[1] == zw
    dhn = _mm(dz, w["in"], tb=True, name=nm("d_in_x"), tk=768)
    g["in"] = _mm(sv["hn"], dz, ta=True, name=nm("d_in_w"), tn=768)
    (dx,), (g["g_mix"],) = _rowmap_bwd(_f_norm, [x], [w["g_mix"]], [dhn], diff=[True], out_dtypes=[F32], adds=[dx1],
                                       tb=256, name=nm("d_norm1"))
    return dx, g


_IN_ORIG = dict(aq=(0, 512), ak=(512, 640), av=(640, 768), cq=(768, 1152), ckv=(1152, 1408), kr=(1408, 1440),
                c=(1440, 6048), mq=(6048, 6560), gates=(6560, 10656))
_IN_OURS = dict(gates=(0, 4096), c=(4096, 8704), aq=(8704, 9216), mq=(9216, 9728), ckv=(9728, 9984), cq=(9984, 10368),
                ak=(10368, 10496), av=(10496, 10624), kr=(10624, 10656))
_IN_ORDER_ORIG = ("aq", "ak", "av", "cq", "ckv", "kr", "c", "mq", "gates")


def _in_to_ours(w_in):
    pc = {k: w_in[..., a:b] for k, (a, b) in _IN_ORIG.items()}
    zeros = lambda n: jnp.zeros(w_in.shape[:-1] + (n,), w_in.dtype)
    return jnp.concatenate([pc["gates"], pc["c"], pc["aq"], pc["mq"], pc["ckv"], pc["cq"], pc["ak"], pc["av"], pc["kr"],
                            zeros(96)], axis=-1)


def _in_from_ours(g_in):
    return jnp.concatenate([g_in[..., _IN_OURS[k][0]:_IN_OURS[k][1]] for k in _IN_ORDER_ORIG], axis=-1)


def _heads_split(w, heads, first):
    per = w.shape[-1] // heads
    w4 = w.reshape(w.shape[:-1] + (heads, per))
    a = w4[..., :first].reshape(w.shape[:-1] + (heads * first,))
    b = w4[..., first:].reshape(w.shape[:-1] + (heads * (per - first),))
    return jnp.concatenate([a, b], axis=-1)


def _heads_merge(w, heads, first):
    per = w.shape[-1] // heads
    a = w[..., :heads * first].reshape(w.shape[:-1] + (heads, first))
    b = w[..., heads * first:].reshape(w.shape[:-1] + (heads, per - first))
    return jnp.concatenate([a, b], axis=-1).reshape(w.shape)


def _layer_weights(big, small, l):
    row = lambda a: a[l].reshape(1, -1)
    w = dict(big_l for big_l in ((k, v[l]) for k, v in big.items()))
    w.update(g_mix=row(small["g_mix"]), b_gate=row(small["b_gate"]), a_qn=row(small["a_qn"]), a_kn=row(small["a_kn"]),
             a_sink=row(small["a_sink"]), b_qa=row(small["b_qa_norm"]), b_kva=row(small["b_kva_norm"]),
             b_qn=small["b_qn"][l, :B_NOPE].reshape(1, -1), b_qr=small["b_qn"][l, B_NOPE:].reshape(1, -1),
             b_kn=small["b_kn"][l, :B_NOPE].reshape(1, -1), b_kr=small["b_kn"][l, B_NOPE:].reshape(1, -1),
             m_g_mem=row(small["m_g_mem"]), m_qn=row(small["m_qn"]), m_kn=row(small["m_kn"]), g_mlp=row(small["g_mlp"]))
    for g in range(3):
        w[f"c{g}q"] = small["c_qn"][l, g].reshape(1, -1)
        w[f"c{g}k"] = small["c_kn"][l, g].reshape(1, -1)
    return w


def _rope_tables(positions):
    pos = positions.astype(F32)[:, None]
    tabs = []
    for dim in (A_HD, B_ROPE):
        inv = ROPE_THETA ** (-jnp.arange(0, dim, 2, dtype=F32) / dim)
        ang = pos * inv
        reps = LANE // (dim // 2)
        tabs += [jnp.tile(jnp.cos(ang), (1, reps)), jnp.tile(jnp.sin(ang), (1, reps))]
    return tuple(tabs)


def _local_step(x, mem, positions, big, small, loss_target):
    depth = small["g_mix"].shape[0]
    tabs = _rope_tables(positions)
    ws = [_layer_weights(big, small, l) for l in range(depth)]
    saved = []
    h = x
    for l in range(depth):
        h, sv = _layer_fwd(l, h, mem, ws[l], tabs)
        saved.append(sv)
    loss, dh = _loss_and_grad(h, loss_target)
    grads = [None] * depth
    for l in reversed(range(depth)):
        dh, grads[l] = _layer_bwd(l, dh, mem, ws[l], tabs, saved[l])
    return loss, dh, grads


def _grads_to_reference_layout(grads):
    st = lambda k: jnp.stack([g[k] for g in grads])
    flat = lambda k: jnp.stack([g[k].reshape(-1) for g in grads])
    out = dict(
        g_mix=flat("g_mix"), w_in=_in_from_ours(st("in")), b_gate=flat("b_gate"), a_qn=flat("a_qn"), a_kn=flat("a_kn"),
        a_sink=flat("a_sink"), b_qa_norm=flat("b_qa"), b_kva_norm=flat("b_kva"),
        b_w_uq=_heads_merge(st("uq"), B_HEADS, B_NOPE), b_w_ukv=_heads_merge(st("ukv"), B_HEADS, B_NOPE),
        b_qn=jnp.concatenate([flat("b_qn"), flat("b_qr")], axis=1), b_kn=jnp.concatenate([flat("b_kn"), flat("b_kr")], axis=1),
        c_qn=jnp.stack([jnp.stack([g[f"c{i}q"].reshape(-1) for i in range(3)]) for g in grads]),
        c_kn=jnp.stack([jnp.stack([g[f"c{i}k"].reshape(-1) for i in range(3)]) for g in grads]),
        m_g_mem=flat("m_g_mem"), m_w_kv=st("mkv"), m_qn=flat("m_qn"), m_kn=flat("m_kn"),
        w_branch=jnp.stack([jnp.stack(g["branch"]) for g in grads]), w_out=st("out"), g_mlp=flat("g_mlp"),
        w_up=st("up"), w_down=st("down"))
    return out


def _big_to_kernel_layout(full):
    c = lambda a: a.astype(BF16)
    return {"in": c(_in_to_ours(full["w_in"])), "uq": c(_heads_split(full["b_w_uq"], B_HEADS, B_NOPE)),
            "ukv": c(_heads_split(full["b_w_ukv"], B_HEADS, B_NOPE)), "mkv": c(full["m_w_kv"]),
            "branch": c(full["w_branch"]), "out": c(full["w_out"]), "up": c(full["w_up"]), "down": c(full["w_down"])}


MESH = pl.DeviceIdType.MESH
N_CHIPS = 4
N_DEV = 8
_ANY = pl.BlockSpec(memory_space=pl.ANY)


def _chip_exchange(arrays, *, gather, name):
    n = len(arrays)

    def body(*refs):
        ins, outs = refs[:n], refs[n:2 * n]
        send_sems, recv_sems, local_sems = refs[2 * n:]
        x, y, c = lax.axis_index("x"), lax.axis_index("y"), lax.axis_index("c")
        me = 2 * x + y
        chips = [(1 - x, y), (x, 1 - y), (1 - x, 1 - y)]
        locals_, sends = [], []
        for a in range(n):
            src_own = ins[a] if gather else ins[a].at[me]
            lc = pltpu.make_async_copy(src_own, outs[a].at[me], local_sems.at[a])
            lc.start()
            locals_.append(lc)
            for j, (px, py) in enumerate(chips):
                src = ins[a] if gather else ins[a].at[2 * px + py]
                cp = pltpu.make_async_remote_copy(src_ref=src, dst_ref=outs[a].at[me], send_sem=send_sems.at[3 * a + j],
                                                  recv_sem=recv_sems.at[3 * a + j], device_id=(px, py, c), device_id_type=MESH)
                cp.start()
                sends.append(cp)
        for a in range(n):
            for j, (px, py) in enumerate(chips):
                src = ins[a] if gather else ins[a].at[me]
                pltpu.make_async_remote_copy(src_ref=src, dst_ref=outs[a].at[2 * px + py], send_sem=send_sems.at[3 * a + j],
                                             recv_sem=recv_sems.at[3 * a + j], device_id=(px, py, c),
                                             device_id_type=MESH).wait_recv()
        for cp in sends:
            cp.wait_send()
        for lc in locals_:
            lc.wait()

    out_shape = [jax.ShapeDtypeStruct(((N_CHIPS,) + a.shape) if gather else a.shape, a.dtype) for a in arrays]
    return _pcall(
        body, name=name, in_specs=[_ANY] * n, out_specs=[_ANY] * n, out_shape=out_shape,
        scratch_shapes=[pltpu.SemaphoreType.DMA((3 * n,)), pltpu.SemaphoreType.DMA((3 * n,)), pltpu.SemaphoreType.DMA((n,))],
        compiler_params=pltpu.CompilerParams(has_side_effects=True),
    )(*arrays)


def _sibling_exchange(arrays, *, name):
    n = len(arrays)

    def body(*refs):
        ins, outs = refs[:n], refs[n:2 * n]
        send_sems, recv_sems = refs[2 * n:]
        x, y, c = lax.axis_index("x"), lax.axis_index("y"), lax.axis_index("c")
        cps = []
        for a in range(n):
            cp = pltpu.make_async_remote_copy(src_ref=ins[a], dst_ref=outs[a], send_sem=send_sems.at[a], recv_sem=recv_sems.at[a],
                                              device_id=(x, y, 1 - c), device_id_type=MESH)
            cp.start()
            cps.append(cp)
        for cp in cps:
            cp.wait()

    return _pcall(
        body, name=name, in_specs=[_ANY] * n, out_specs=[_ANY] * n,
        out_shape=[jax.ShapeDtypeStruct(a.shape, a.dtype) for a in arrays],
        scratch_shapes=[pltpu.SemaphoreType.DMA((n,)), pltpu.SemaphoreType.DMA((n,))],
        compiler_params=pltpu.CompilerParams(has_side_effects=True),
    )(*arrays)


def _allreduce_small(v, *, name):
    rows = v.shape[0]

    def body(v_ref, o_ref, slots, send_sems, recv_sems):
        x, y, c = lax.axis_index("x"), lax.axis_index("y"), lax.axis_index("c")
        me = 4 * x + 2 * y + c
        slots[me] = v_ref[...]
        cps = []
        for k in range(1, N_DEV):
            fx, fy, fc = (k >> 2) & 1, (k >> 1) & 1, k & 1
            peer = (x ^ fx, y ^ fy, c ^ fc)
            cp = pltpu.make_async_remote_copy(src_ref=v_ref, dst_ref=slots.at[me], send_sem=send_sems.at[k - 1],
                                              recv_sem=recv_sems.at[k - 1], device_id=peer, device_id_type=MESH)
            cp.start()
            cps.append((cp, peer))
        for k, (cp, (px, py, pc)) in enumerate(cps):
            pltpu.make_async_remote_copy(src_ref=v_ref, dst_ref=slots.at[4 * px + 2 * py + pc], send_sem=send_sems.at[k],
                                         recv_sem=recv_sems.at[k], device_id=(px, py, pc), device_id_type=MESH).wait_recv()
        for cp, _ in cps:
            cp.wait_send()
        tot = slots[0]
        for d in range(1, N_DEV):
            tot = tot + slots[d]
        o_ref[...] = tot

    vm = pl.BlockSpec(memory_space=pltpu.VMEM)
    return _pcall(
        body, name=name, in_specs=[vm], out_specs=vm, out_shape=jax.ShapeDtypeStruct(v.shape, F32),
        scratch_shapes=[pltpu.VMEM((N_DEV, rows, LANE), F32), pltpu.SemaphoreType.DMA((N_DEV - 1,)),
                        pltpu.SemaphoreType.DMA((N_DEV - 1,))],
        compiler_params=pltpu.CompilerParams(has_side_effects=True),
    )(v)


def _rows_block(rows, cols, itemsize=4, target_bytes=1 << 20):
    want = max(16, target_bytes // max(1, cols * itemsize))
    best = rows
    for t in range(16, rows, 16):
        if rows % t == 0 and t <= want:
            best = t
    return best if best <= want or rows <= want else rows


def _sum_slots(recv, *, name):
    shp = recv.shape[1:]
    r3 = recv.reshape(N_CHIPS, -1, shp[-1])
    rows, cols = r3.shape[1:]
    tb = _rows_block(rows, cols)

    def body(r0, r1, r2, r3_, o_ref):
        o_ref[...] = ((r0[...].astype(F32) + r1[...].astype(F32)) + r2[...].astype(F32)) + r3_[...].astype(F32)

    out = _pcall(
        body, name=name, grid=(rows // tb,),
        in_specs=[pl.BlockSpec((None, tb, cols), lambda i, k=k: (k, i, 0)) for k in range(N_CHIPS)],
        out_specs=pl.BlockSpec((tb, cols), lambda i: (i, 0)), out_shape=jax.ShapeDtypeStruct((rows, cols), F32),
        compiler_params=_cparams(("parallel",)),
    )(r3, r3, r3, r3)
    return out.reshape(shp)


def _adamw(w, g_parts, m, v, *, name):
    shp = w.shape
    two = lambda a: a.reshape(-1, shp[-1])
    rows, cols = two(w).shape
    tb = _rows_block(rows, cols, target_bytes=1 << 19)
    npart = len(g_parts)

    def body(*refs):
        w_ref = refs[0]
        gp = refs[1:1 + npart]
        m_ref, v_ref, g_out, d_out, m_out, v_out = refs[1 + npart:]
        g = gp[0][...]
        for r in gp[1:]:
            g = g + r[...]
        wv = w_ref[...]
        m2 = ADAM_B1 * m_ref[...] + (1.0 - ADAM_B1) * g
        v2 = ADAM_B2 * v_ref[...] + (1.0 - ADAM_B2) * jnp.square(g)
        m_hat = m2 / (1.0 - ADAM_B1 ** ADAM_STEP)
        v_hat = v2 / (1.0 - ADAM_B2 ** ADAM_STEP)
        g_out[...] = g
        d_out[...] = -ADAM_LR * (m_hat / (jnp.sqrt(v_hat) + ADAM_EPS) + ADAM_WD * wv)
        m_out[...] = m2
        v_out[...] = v2

    spec = pl.BlockSpec((tb, cols), lambda i: (i, 0))
    outs = _pcall(
        body, name=name, grid=(rows // tb,), in_specs=[spec] * (3 + npart), out_specs=[spec] * 4,
        out_shape=[jax.ShapeDtypeStruct((rows, cols), F32)] * 4, compiler_params=_cparams(("parallel",)),
    )(two(w), *[two(p) for p in g_parts], two(m), two(v))
    return [o.reshape(shp) for o in outs]


BIG = ("w_in", "b_w_uq", "b_w_ukv", "m_w_kv", "w_branch", "w_out", "w_up", "w_down")
_SHARD_AXIS = dict(w_in=2, b_w_uq=2, b_w_ukv=2, m_w_kv=1, w_branch=3, w_out=1, w_up=2, w_down=1)
SMALL = ("g_mix", "b_gate", "a_qn", "a_kn", "a_sink", "b_qa_norm", "b_kva_norm", "b_qn", "b_kn", "c_qn", "c_kn",
         "m_g_mem", "m_qn", "m_kn", "g_mlp")
WEIGHTS = ("g_mix", "w_in", "b_gate", "a_qn", "a_kn", "a_sink", "b_qa_norm", "b_kva_norm", "b_w_uq", "b_w_ukv", "b_qn", "b_kn",
           "c_qn", "c_kn", "m_g_mem", "m_w_kv", "m_qn", "m_kn", "w_branch", "w_out", "g_mlp", "w_up", "w_down")


def _unshard(gathered, axis):
    moved = jnp.moveaxis(gathered, 0, axis)
    shp = list(gathered.shape[1:])
    shp[axis] *= N_CHIPS
    return moved.reshape(shp)


def _shard_parts(full, axis):
    shp = list(full.shape)
    shp[axis:axis + 1] = [N_CHIPS, shp[axis] // N_CHIPS]
    return jnp.moveaxis(full.reshape(shp), axis, 0)


def _pack_small(d):
    flat = jnp.concatenate([d[k].reshape(-1).astype(F32) for k in SMALL])
    n = flat.shape[0]
    pad = (-n) % (8 * LANE)
    return jnp.pad(flat, (0, pad)).reshape(-1, LANE)


def _unpack_small(packed, like):
    flat = packed.reshape(-1)
    out, off = {}, 0
    for k in SMALL:
        n = int(np.prod(like[k].shape))
        out[k] = flat[off:off + n].reshape(like[k].shape)
        off += n
    return out


def _train_step(x, mem, positions, loss_target, w, m, v):
    shards = [w[k].astype(BF16) for k in BIG]
    gathered = _chip_exchange(shards, gather=True, name="gather_weights")
    full = {k: _unshard(g, _SHARD_AXIS[k]) for k, g in zip(BIG, gathered)}
    big = _big_to_kernel_layout(full)
    small = {k: w[k] for k in SMALL}
    loss, gx, grads = _local_step(x[0], mem[0], positions[0], big, small, loss_target[0])
    gref = _grads_to_reference_layout(grads)
    loss = lax.psum(loss, ("x", "y", "c"))
    parts = [_shard_parts(gref[k], _SHARD_AXIS[k]).astype(BF16) for k in BIG]
    recv = _chip_exchange(parts, gather=False, name="scatter_grads")
    mine = [_sum_slots(r, name=f"sum_{k}") for k, r in zip(BIG, recv)]
    theirs = _sibling_exchange(mine, name="sibling_grads")
    res = {}
    for k, p, q in zip(BIG, mine, theirs):
        res[k] = _adamw(w[k], [p, q], m[k], v[k], name=f"adamw_{k}")
    g_small = _allreduce_small(_pack_small({k: gref[k] for k in SMALL}), name="allreduce_small")
    packed = _adamw(_pack_small(small), [g_small], _pack_small({k: m[k] for k in SMALL}), _pack_small({k: v[k] for k in SMALL}),
                    name="adamw_small")
    unpacked = [_unpack_small(p, small) for p in packed]
    for k in SMALL:
        res[k] = [u[k] for u in unpacked]
    outs = [loss, gx[None]]
    for i in range(4):
        outs += [res[k][i] for k in WEIGHTS]
    return tuple(outs)


def kernel(x, mem, positions, g_mix, w_in, b_gate, a_qn, a_kn, a_sink, b_qa_norm, b_kva_norm, b_w_uq, b_w_ukv, b_qn, b_kn, c_qn, c_kn, m_g_mem, m_w_kv, m_qn, m_kn, w_branch, w_out, g_mlp, w_up, w_down, loss_target, m_g_mix, m_w_in, m_b_gate, m_a_qn, m_a_kn, m_a_sink, m_b_qa_norm, m_b_kva_norm, m_b_w_uq, m_b_w_ukv, m_b_qn, m_b_kn, m_c_qn, m_c_kn, m_m_g_mem, m_m_w_kv, m_m_qn, m_m_kn, m_w_branch, m_w_out, m_g_mlp, m_w_up, m_w_down, v_g_mix, v_w_in, v_b_gate, v_a_qn, v_a_kn, v_a_sink, v_b_qa_norm, v_b_kva_norm, v_b_w_uq, v_b_w_ukv, v_b_qn, v_b_kn, v_c_qn, v_c_kn, v_m_g_mem, v_m_w_kv, v_m_qn, v_m_kn, v_w_branch, v_w_out, v_g_mlp, v_w_up, v_w_down):
    args = dict(locals())
    w = {k: args[k] for k in WEIGHTS}
    m = {k: args["m_" + k] for k in WEIGHTS}
    v = {k: args["v_" + k] for k in WEIGHTS}
    return _train_step(x, mem, positions, loss_target, w, m, v)
```

```python
import functools
import math

import jax
import jax.numpy as jnp
import numpy as np
from jax import lax
from jax.experimental import pallas as pl
from jax.experimental.pallas import tpu as pltpu

F32 = jnp.float32
BF16 = jnp.bfloat16

DEPTH = 4
BLOCK = 128
ROPE_THETA = 10000.0
EPS = 1e-6
NEG = -1e30
A_HEADS, A_KV_HEADS, A_HD, A_WINDOW = 8, 2, 64, 128
B_HEADS, B_Q_LORA, B_KV_LORA, B_NOPE, B_ROPE, B_V = 8, 384, 256, 64, 32, 64
C_PATTERNS = ((128, 1), (512, 4), (2048, 16))
C_HEADS, C_HD = 8, 64
M_HEADS, M_HD = 4, 128
N_BRANCH, BRANCH_W = 4, 512
ADAM_LR, ADAM_B1, ADAM_B2, ADAM_EPS, ADAM_WD, ADAM_STEP = 0.001, 0.9, 0.999, 1e-08, 0.01, 10

LANE = 128
VMEM_LIMIT = 56 * 1024 * 1024


def _pcall(body, **kw):
    return pl.pallas_call(body, **kw)


def _cparams(sem):
    return pltpu.CompilerParams(dimension_semantics=sem, vmem_limit_bytes=VMEM_LIMIT)


def _tile(n, target):
    if n <= target:
        return n
    best = None
    for t in range(LANE, target + 1, LANE):
        if n % t == 0:
            best = t
    return best if best is not None else n


def _mm(a, b, *, ta=False, tb=False, out_dtype=F32, pro_a=None, epi=None, extra=None, name,
        tm=512, tn=1024, tk=1024):
    if ta:
        K, M = a.shape
    else:
        M, K = a.shape
    if tb:
        N, K2 = b.shape
    else:
        K2, N = b.shape
    assert K == K2, (a.shape, b.shape, ta, tb)
    tm, tn, tk = _tile(M, tm), _tile(N, tn), _tile(K, tk)
    nk = K // tk
    a_spec = pl.BlockSpec((tk, tm), lambda i, j, k: (k, i)) if ta else pl.BlockSpec((tm, tk), lambda i, j, k: (i, k))
    b_spec = pl.BlockSpec((tn, tk), lambda i, j, k: (j, k)) if tb else pl.BlockSpec((tk, tn), lambda i, j, k: (k, j))
    o_spec = pl.BlockSpec((tm, tn), lambda i, j, k: (i, j))
    dims = (((0,) if ta else (1,), (1,) if tb else (0,)), ((), ()))
    has_extra = extra is not None

    def body(*refs):
        if has_extra:
            a_ref, b_ref, e_ref, o_ref, acc_ref = refs
        else:
            a_ref, b_ref, o_ref, acc_ref = refs
            e_ref = None
        k = pl.program_id(2)
        av = a_ref[...]
        if pro_a is not None:
            av = pro_a(av.astype(F32))
        part = lax.dot_general(av.astype(BF16), b_ref[...].astype(BF16), dims, preferred_element_type=F32)

        @pl.when(k == 0)
        def _():
            acc_ref[...] = part

        @pl.when(k > 0)
        def _():
            acc_ref[...] += part

        @pl.when(k == nk - 1)
        def _():
            r = acc_ref[...]
            if epi is not None:
                r = epi(r, e_ref[...]) if has_extra else epi(r)
            o_ref[...] = r.astype(out_dtype)

    in_specs = [a_spec, b_spec] + ([o_spec] if has_extra else [])
    args = (a, b) + ((extra,) if has_extra else ())
    return _pcall(
        body, name=name, grid=(M // tm, N // tn, nk), in_specs=in_specs, out_specs=o_spec,
        out_shape=jax.ShapeDtypeStruct((M, N), out_dtype),
        scratch_shapes=[pltpu.VMEM((tm, tn), F32)],
        compiler_params=_cparams(("parallel", "parallel", "arbitrary")),
    )(*args)


def _piece(p):
    if isinstance(p, tuple):
        return p
    return (p, p.shape[1], 0)


def _row_spec(width, idx, tb):
    return pl.BlockSpec((tb, width), lambda i, idx=idx: (i, idx))


def _full_spec(arr):
    nd = arr.ndim
    return pl.BlockSpec(arr.shape, lambda i, nd=nd: (0,) * nd)


def _rowmap(f, rows, params, outs, *, tb, name):
    rows = [_piece(p) for p in rows]
    R = rows[0][0].shape[0]
    tb = min(tb, R)
    nr, npar, nout = len(rows), len(params), len(outs)

    def body(*refs):
        rv = [r[...] for r in refs[:nr]]
        pv = [r[...] for r in refs[nr:nr + npar]]
        res = f(*rv, *pv)
        for o_ref, val in zip(refs[nr + npar:], res):
            o_ref[...] = val.astype(o_ref.dtype)

    return _pcall(
        body, name=name, grid=(R // tb,),
        in_specs=[_row_spec(w, idx, tb) for (_, w, idx) in rows] + [_full_spec(p) for p in params],
        out_specs=[_row_spec(w, 0, tb) for (w, _) in outs],
        out_shape=[jax.ShapeDtypeStruct((R, w), dt) for (w, dt) in outs],
        compiler_params=_cparams(("parallel",)),
    )(*[r[0] for r in rows], *params)


def _rowmap_bwd(f, rows, params, couts, *, diff, out_dtypes, adds=None, tb, name):
    rows = [_piece(p) for p in rows]
    couts = [_piece(p) for p in couts]
    R = rows[0][0].shape[0]
    tb = min(tb, R)
    nr, npar, nc = len(rows), len(params), len(couts)
    didx = [i for i, d in enumerate(diff) if d]
    adds = [None] * len(didx) if adds is None else adds
    add_ops = [_piece(a) for a in adds if a is not None]
    na = len(add_ops)

    def body(*refs):
        rv = [r[...] for r in refs[:nr]]
        pv = [r[...] for r in refs[nr:nr + npar]]
        cv = [r[...] for r in refs[nr + npar:nr + npar + nc]]
        av = [r[...] for r in refs[nr + npar + nc:nr + npar + nc + na]]
        o_refs = refs[nr + npar + nc + na:]
        drow_refs, dpar_refs = o_refs[:len(didx)], o_refs[len(didx):]
        nondiff = {i: rv[i] for i in range(nr) if not diff[i]}

        def g(*dv):
            full = []
            it = iter(dv[:len(didx)])
            for i in range(nr):
                full.append(nondiff[i] if i in nondiff else next(it))
            return f(*full, *dv[len(didx):])

        res, vjp = jax.vjp(g, *[rv[i].astype(F32) for i in didx], *pv)
        cts = tuple(c.astype(r.dtype) for c, r in zip(cv, res))
        grads = vjp(cts)
        ai = 0
        for n, o_ref in enumerate(drow_refs):
            val = grads[n]
            if adds[n] is not None:
                val = val + av[ai].astype(F32)
                ai += 1
            o_ref[...] = val.astype(o_ref.dtype)
        first = pl.program_id(0) == 0
        for n, o_ref in enumerate(dpar_refs):
            gp = grads[len(didx) + n].astype(F32)

            @pl.when(first)
            def _(o_ref=o_ref, gp=gp):
                o_ref[...] = gp

            @pl.when(jnp.logical_not(first))
            def _(o_ref=o_ref, gp=gp):
                o_ref[...] += gp

    outs = _pcall(
        body, name=name, grid=(R // tb,),
        in_specs=([_row_spec(w, idx, tb) for (_, w, idx) in rows] + [_full_spec(p) for p in params]
                  + [_row_spec(w, idx, tb) for (_, w, idx) in couts] + [_row_spec(w, idx, tb) for (_, w, idx) in add_ops]),
        out_specs=[_row_spec(rows[i][1], 0, tb) for i in didx] + [_full_spec(p) for p in params],
        out_shape=([jax.ShapeDtypeStruct((R, rows[i][1]), dt) for i, dt in zip(didx, out_dtypes)]
                   + [jax.ShapeDtypeStruct(p.shape, F32) for p in params]),
        compiler_params=_cparams(("arbitrary",)),
    )(*[r[0] for r in rows], *params, *[c[0] for c in couts], *[a[0] for a in add_ops])
    return outs[:len(didx)], outs[len(didx):]


@functools.partial(jax.custom_vjp, nondiff_argnums=(1,))
def _lane_roll(x, shift):
    return pltpu.roll(x, shift % x.shape[-1], axis=x.ndim - 1)


def _lane_roll_fwd(x, shift):
    return _lane_roll(x, shift), None


def _lane_roll_bwd(shift, _, g):
    return (_lane_roll(g, -shift),)


_lane_roll.defvjp(_lane_roll_fwd, _lane_roll_bwd)


def _group_matrix(kind):
    r = lax.broadcasted_iota(jnp.int32, (LANE, LANE), 0)
    c = lax.broadcasted_iota(jnp.int32, (LANE, LANE), 1)
    if kind == "mla":
        gid = lambda l: jnp.where(l < B_NOPE, 0, jnp.where(l < B_NOPE + B_ROPE, 1, 2))
        inv = jnp.where(c < B_NOPE, 1.0 / B_NOPE, 1.0 / B_ROPE)
        return jnp.where(gid(r) == gid(c), inv, 0.0).astype(BF16)
    return jnp.where(r // kind == c // kind, 1.0 / kind, 0.0).astype(BF16)


@functools.partial(jax.custom_vjp, nondiff_argnums=(1,))
def _group_mean(xx, kind):
    gm = _group_matrix(kind)
    outs = []
    for b in range(xx.shape[-1] // LANE):
        t = xx[:, b * LANE:(b + 1) * LANE]
        hi = t.astype(BF16)
        lo = (t - hi.astype(F32)).astype(BF16)
        outs.append(jnp.dot(hi, gm, preferred_element_type=F32) + jnp.dot(lo, gm, preferred_element_type=F32))
    return jnp.concatenate(outs, axis=1) if len(outs) > 1 else outs[0]


def _group_mean_fwd(xx, kind):
    return _group_mean(xx, kind), None


def _group_mean_bwd(kind, _, g):
    return (_group_mean(g, kind),)


_group_mean.defvjp(_group_mean_fwd, _group_mean_bwd)


def _exact_dot(x, m):
    return jnp.dot(x, m, precision=lax.Precision.HIGHEST, preferred_element_type=F32)


def _head_norm(x, gain_tiled, group):
    return x * lax.rsqrt(_group_mean(x * x, group) + EPS) * gain_tiled


def _row_norm(x, gain):
    ms = jnp.mean(x * x, axis=-1, keepdims=True)
    return x * lax.rsqrt(ms + EPS) * gain


def _rope(x, cos, sin, hd):
    half = hd // 2
    lane = lax.broadcasted_iota(jnp.int32, x.shape, x.ndim - 1) % hd
    other = jnp.where(lane < half, -_lane_roll(x, -half), _lane_roll(x, half))
    return x * cos + other * sin


class _AttnGeom:
    def __init__(self, mode, lq, lk, max_dist=0):
        self.mode, self.lq, self.lk, self.max_dist = mode, lq, lk, max_dist
        if mode == "band":
            self.bq = self.bk = BLOCK
            self.nt_q = 2
            self.nt_k = 2
        elif mode == "causal":
            self.bq = self.bk = min(256, lq)
            self.nt_q = lk // self.bk
            self.nt_k = lq // self.bq
        else:
            self.bq = min(256, lq)
            self.bk = lk
            self.nt_q = 1
            self.nt_k = lq // self.bq
        self.nq, self.nk = lq // self.bq, lk // self.bk

    def kv_block(self, i, t):
        if self.mode == "band":
            return jnp.maximum(i - t, 0)
        if self.mode == "causal":
            return jnp.minimum(t, i)
        return 0 * i

    def kv_active(self, i, t):
        if self.mode == "band":
            return i - t >= 0
        if self.mode == "causal":
            return t <= i
        return None

    def q_block(self, j, t):
        if self.mode == "band":
            return jnp.minimum(j + t, self.nq - 1)
        if self.mode == "causal":
            return jnp.maximum(t, j)
        return t

    def q_active(self, j, t):
        if self.mode == "band":
            return j + t <= self.nq - 1
        if self.mode == "causal":
            return t >= j
        return None

    def mask(self, qb, kb):
        if self.mode == "full":
            return None
        qp = qb * self.bq + lax.broadcasted_iota(jnp.int32, (self.bq, self.bk), 0)
        kp = kb * self.bk + lax.broadcasted_iota(jnp.int32, (self.bq, self.bk), 1)
        d = qp - kp
        if self.mode == "band":
            return (d >= 0) & (d <= self.max_dist)
        return d >= 0


def _when(cond, fn):
    if cond is None:
        fn()
    else:
        pl.when(cond)(fn)


def _dil_view(p, dil):
    arr, w, idx = _piece(p)
    R, C = arr.shape
    assert C % w == 0, (C, w)
    return arr.reshape(R // dil, dil * C), w, idx, C // w


def _seq_spec(view, rows, blk_fn):
    _, w, idx, cpw = view
    return pl.BlockSpec((rows, w), lambda s, i, t: (blk_fn(i, t), s * cpw + idx))


_NT = (((1,), (1,)), ((), ()))
_TN = (((0,), (0,)), ((), ()))


def _scores(geom, scale, q, k, qp, kp, h, g, hd, rope, qb, kb):
    s = lax.dot_general(q[:, h * hd:(h + 1) * hd], k[:, g * hd:(g + 1) * hd], _NT, preferred_element_type=F32)
    if rope:
        s = s + lax.dot_general(qp[:, h * rope:(h + 1) * rope], kp[:, :rope], _NT, preferred_element_type=F32)
    s = s * scale
    m = geom.mask(qb, kb)
    return s, m


def _attn_fwd(geom, q, k, v, *, hq, hk, hd, hdv, scale, dil=1, qp=None, kp=None, rope=0, name):
    qv, kv, vv = _dil_view(q, dil), _dil_view(k, dil), _dil_view(v, dil)
    R = _piece(q)[0].shape[0]
    grp = hq // hk
    bq, bk, nt = geom.bq, geom.bk, geom.nt_q
    ops = [qv, kv, vv]
    specs = [_seq_spec(qv, bq, lambda i, t: i), _seq_spec(kv, bk, geom.kv_block), _seq_spec(vv, bk, geom.kv_block)]
    if rope:
        qpv, kpv = _dil_view(qp, dil), _dil_view(kp, dil)
        ops += [qpv, kpv]
        specs += [_seq_spec(qpv, bq, lambda i, t: i), _seq_spec(kpv, bk, geom.kv_block)]
    ow = hq * hdv
    o_view = (None, ow, 0, 1)
    o_spec = pl.BlockSpec((bq, ow), lambda s, i, t: (i, s))

    def body(*refs):
        if rope:
            q_ref, k_ref, v_ref, qp_ref, kp_ref, o_ref, lse_ref, m_sc, l_sc, acc_sc = refs
        else:
            q_ref, k_ref, v_ref, o_ref, lse_ref, m_sc, l_sc, acc_sc = refs
            qp_ref = kp_ref = None
        i, t = pl.program_id(1), pl.program_id(2)

        @pl.when(t == 0)
        def _():
            m_sc[...] = jnp.full(m_sc.shape, NEG, F32)
            l_sc[...] = jnp.zeros(l_sc.shape, F32)
            acc_sc[...] = jnp.zeros(acc_sc.shape, F32)

        def step():
            qa, ka, va = q_ref[...].astype(BF16), k_ref[...].astype(BF16), v_ref[...].astype(BF16)
            qpa = qp_ref[...].astype(BF16) if rope else None
            kpa = kp_ref[...].astype(BF16) if rope else None
            kb = geom.kv_block(i, t)
            for h in range(hq):
                g = h // grp
                s, msk = _scores(geom, scale, qa, ka, qpa, kpa, h, g, hd, rope, i, kb)
                if msk is not None:
                    s = jnp.where(msk, s, NEG)
                m_old = m_sc[h]
                m_new = jnp.maximum(m_old, jnp.max(s, axis=1, keepdims=True))
                p = jnp.exp(s - m_new)
                alpha = jnp.exp(m_old - m_new)
                l_sc[h] = alpha * l_sc[h] + jnp.sum(p, axis=1, keepdims=True)
                pv = jnp.dot(p.astype(BF16), va[:, g * hdv:(g + 1) * hdv], preferred_element_type=F32)
                acc_sc[h] = alpha * acc_sc[h] + pv
                m_sc[h] = m_new

        _when(geom.kv_active(i, t), step)

        @pl.when(t == nt - 1)
        def _():
            for h in range(hq):
                l = l_sc[h]
                o_ref[:, h * hdv:(h + 1) * hdv] = acc_sc[h] / l
                lse_ref[:, h * hdv:(h + 1) * hdv] = jnp.broadcast_to(m_sc[h] + jnp.log(l), (bq, hdv))

    o, lse = _pcall(
        body, name=name, grid=(dil, geom.nq, nt), in_specs=specs, out_specs=[o_spec, o_spec],
        out_shape=[jax.ShapeDtypeStruct((R // dil, dil * ow), F32)] * 2,
        scratch_shapes=[pltpu.VMEM((hq, bq, 1), F32), pltpu.VMEM((hq, bq, 1), F32), pltpu.VMEM((hq, bq, hdv), F32)],
        compiler_params=_cparams(("parallel", "parallel", "arbitrary")),
    )(*[o_[0] for o_ in ops])
    return o.reshape(R, ow), lse.reshape(R, ow)


def _attn_dq(geom, q, k, v, do, o, lse, dlse, *, hq, hk, hd, hdv, scale, dil=1, qp=None, kp=None, rope=0,
             out_dtype=F32, name):
    qv, kv, vv = _dil_view(q, dil), _dil_view(k, dil), _dil_view(v, dil)
    dov, ov, lv = _dil_view(do, dil), _dil_view(o, dil), _dil_view(lse, dil)
    R = _piece(q)[0].shape[0]
    grp = hq // hk
    bq, bk, nt = geom.bq, geom.bk, geom.nt_q
    qi = lambda i, t: i
    ops = [qv, kv, vv, dov, ov, lv]
    specs = [_seq_spec(qv, bq, qi), _seq_spec(kv, bk, geom.kv_block), _seq_spec(vv, bk, geom.kv_block),
             _seq_spec(dov, bq, qi), _seq_spec(ov, bq, qi), _seq_spec(lv, bq, qi)]
    has_dl = dlse is not None
    if has_dl:
        dlv = _dil_view(dlse, dil)
        ops.append(dlv)
        specs.append(_seq_spec(dlv, bq, qi))
    if rope:
        qpv, kpv = _dil_view(qp, dil), _dil_view(kp, dil)
        ops += [qpv, kpv]
        specs += [_seq_spec(qpv, bq, qi), _seq_spec(kpv, bk, geom.kv_block)]
    qw = hq * hd
    out_specs = [pl.BlockSpec((bq, qw), lambda s, i, t: (i, s))]
    out_shape = [jax.ShapeDtypeStruct((R // dil, dil * qw), out_dtype)]
    scratch = [pltpu.VMEM((hq, bq, 1), F32), pltpu.VMEM((hq, bq, hd), F32)]
    if rope:
        out_specs.append(pl.BlockSpec((bq, hq * rope), lambda s, i, t: (i, s)))
        out_shape.append(jax.ShapeDtypeStruct((R // dil, dil * hq * rope), out_dtype))
        scratch.append(pltpu.VMEM((hq, bq, rope), F32))

    def body(*refs):
        refs = list(refs)
        q_ref, k_ref, v_ref, do_ref, o_ref, l_ref = refs[:6]
        pos = 6
        dl_ref = None
        if has_dl:
            dl_ref = refs[pos]
            pos += 1
        qp_ref = kp_ref = None
        if rope:
            qp_ref, kp_ref = refs[pos:pos + 2]
            pos += 2
        dq_ref = refs[pos]
        pos += 1
        dqp_ref = None
        if rope:
            dqp_ref = refs[pos]
            pos += 1
        dl_sc, dq_sc = refs[pos:pos + 2]
        dqp_sc = refs[pos + 2] if rope else None
        i, t = pl.program_id(1), pl.program_id(2)

        @pl.when(t == 0)
        def _():
            dov_, ov_ = do_ref[...].astype(F32), o_ref[...].astype(F32)
            prod = dov_ * ov_
            for h in range(hq):
                d = jnp.sum(prod[:, h * hdv:(h + 1) * hdv], axis=1, keepdims=True)
                if has_dl:
                    d = d - jnp.sum(dl_ref[:, h * hdv:(h + 1) * hdv].astype(F32), axis=1, keepdims=True)
                dl_sc[h] = d
            dq_sc[...] = jnp.zeros(dq_sc.shape, F32)
            if rope:
                dqp_sc[...] = jnp.zeros(dqp_sc.shape, F32)

        def step():
            qa, ka, va = q_ref[...].astype(BF16), k_ref[...].astype(BF16), v_ref[...].astype(BF16)
            doa = do_ref[...].astype(BF16)
            qpa = qp_ref[...].astype(BF16) if rope else None
            kpa = kp_ref[...].astype(BF16) if rope else None
            kb = geom.kv_block(i, t)
            for h in range(hq):
                g = h // grp
                s, msk = _scores(geom, scale, qa, ka, qpa, kpa, h, g, hd, rope, i, kb)
                p = jnp.exp(s - l_ref[:, h * hdv:h * hdv + 1])
                if msk is not None:
                    p = jnp.where(msk, p, 0.0)
                dp = lax.dot_general(doa[:, h * hdv:(h + 1) * hdv], va[:, g * hdv:(g + 1) * hdv], _NT,
                                     preferred_element_type=F32)
                ds = (p * (dp - dl_sc[h]) * scale).astype(BF16)
                dq_sc[h] += jnp.dot(ds, ka[:, g * hd:(g + 1) * hd], preferred_element_type=F32)
                if rope:
                    dqp_sc[h] += jnp.dot(ds, kpa[:, :rope], preferred_element_type=F32)

        _when(geom.kv_active(i, t), step)

        @pl.when(t == nt - 1)
        def _():
            for h in range(hq):
                dq_ref[:, h * hd:(h + 1) * hd] = dq_sc[h].astype(dq_ref.dtype)
                if rope:
                    dqp_ref[:, h * rope:(h + 1) * rope] = dqp_sc[h].astype(dqp_ref.dtype)

    outs = _pcall(
        body, name=name, grid=(dil, geom.nq, nt), in_specs=specs, out_specs=out_specs, out_shape=out_shape,
        scratch_shapes=scratch, compiler_params=_cparams(("parallel", "parallel", "arbitrary")),
    )(*[o_[0] for o_ in ops])
    dq = outs[0].reshape(R, qw)
    if rope:
        return dq, outs[1].reshape(R, hq * rope)
    return dq


def _attn_dkv(geom, q, k, v, do, o, lse, dlse, *, hq, hk, hd, hdv, scale, dil=1, qp=None, kp=None, rope=0,
              out_dtype=F32, name):
    qv, kv, vv = _dil_view(q, dil), _dil_view(k, dil), _dil_view(v, dil)
    dov, ov, lv = _dil_view(do, dil), _dil_view(o, dil), _dil_view(lse, dil)
    Rk = _piece(k)[0].shape[0]
    grp = hq // hk
    bq, bk, nt = geom.bq, geom.bk, geom.nt_k
    kj = lambda j, t: j
    ops = [qv, kv, vv, dov, ov, lv]
    specs = [_seq_spec(qv, bq, geom.q_block), _seq_spec(kv, bk, kj), _seq_spec(vv, bk, kj),
             _seq_spec(dov, bq, geom.q_block), _seq_spec(ov, bq, geom.q_block), _seq_spec(lv, bq, geom.q_block)]
    has_dl = dlse is not None
    if has_dl:
        dlv = _dil_view(dlse, dil)
        ops.append(dlv)
        specs.append(_seq_spec(dlv, bq, geom.q_block))
    if rope:
        qpv, kpv = _dil_view(qp, dil), _dil_view(kp, dil)
        ops += [qpv, kpv]
        specs += [_seq_spec(qpv, bq, geom.q_block), _seq_spec(kpv, bk, kj)]
    kw, vw = hk * hd, hk * hdv
    out_specs = [pl.BlockSpec((bk, kw), lambda s, j, t: (j, s)), pl.BlockSpec((bk, vw), lambda s, j, t: (j, s))]
    out_shape = [jax.ShapeDtypeStruct((Rk // dil, dil * kw), out_dtype), jax.ShapeDtypeStruct((Rk // dil, dil * vw), out_dtype)]
    scratch = [pltpu.VMEM((hk, bk, hd), F32), pltpu.VMEM((hk, bk, hdv), F32)]
    if rope:
        out_specs.append(pl.BlockSpec((bk, LANE), lambda s, j, t: (j, s)))
        out_shape.append(jax.ShapeDtypeStruct((Rk // dil, dil * LANE), out_dtype))
        scratch.append(pltpu.VMEM((bk, rope), F32))

    def body(*refs):
        refs = list(refs)
        q_ref, k_ref, v_ref, do_ref, o_ref, l_ref = refs[:6]
        pos = 6
        dl_ref = None
        if has_dl:
            dl_ref = refs[pos]
            pos += 1
        qp_ref = kp_ref = None
        if rope:
            qp_ref, kp_ref = refs[pos:pos + 2]
            pos += 2
        dk_ref, dv_ref = refs[pos:pos + 2]
        pos += 2
        dkp_ref = None
        if rope:
            dkp_ref = refs[pos]
            pos += 1
        dk_sc, dv_sc = refs[pos:pos + 2]
        dkp_sc = refs[pos + 2] if rope else None
        j, t = pl.program_id(1), pl.program_id(2)

        @pl.when(t == 0)
        def _():
            dk_sc[...] = jnp.zeros(dk_sc.shape, F32)
            dv_sc[...] = jnp.zeros(dv_sc.shape, F32)
            if rope:
                dkp_sc[...] = jnp.zeros(dkp_sc.shape, F32)

        def step():
            qa, ka, va = q_ref[...].astype(BF16), k_ref[...].astype(BF16), v_ref[...].astype(BF16)
            dof = do_ref[...].astype(F32)
            doa = dof.astype(BF16)
            prod = dof * o_ref[...].astype(F32)
            qpa = qp_ref[...].astype(BF16) if rope else None
            kpa = kp_ref[...].astype(BF16) if rope else None
            qb = geom.q_block(j, t)
            for h in range(hq):
                g = h // grp
                s, msk = _scores(geom, scale, qa, ka, qpa, kpa, h, g, hd, rope, qb, j)
                p = jnp.exp(s - l_ref[:, h * hdv:h * hdv + 1])
                if msk is not None:
                    p = jnp.where(msk, p, 0.0)
                delta = jnp.sum(prod[:, h * hdv:(h + 1) * hdv], axis=1, keepdims=True)
                if has_dl:
                    delta = delta - jnp.sum(dl_ref[:, h * hdv:(h + 1) * hdv].astype(F32), axis=1, keepdims=True)
                do_h = doa[:, h * hdv:(h + 1) * hdv]
                dv_sc[g] += lax.dot_general(p.astype(BF16), do_h, _TN, preferred_element_type=F32)
                dp = lax.dot_general(do_h, va[:, g * hdv:(g + 1) * hdv], _NT, preferred_element_type=F32)
                ds = (p * (dp - delta) * scale).astype(BF16)
                dk_sc[g] += lax.dot_general(ds, qa[:, h * hd:(h + 1) * hd], _TN, preferred_element_type=F32)
                if rope:
                    dkp_sc[...] += lax.dot_general(ds, qpa[:, h * rope:(h + 1) * rope], _TN, preferred_element_type=F32)

        _when(geom.q_active(j, t), step)

        @pl.when(t == nt - 1)
        def _():
            for g in range(hk):
                dk_ref[:, g * hd:(g + 1) * hd] = dk_sc[g].astype(dk_ref.dtype)
                dv_ref[:, g * hdv:(g + 1) * hdv] = dv_sc[g].astype(dv_ref.dtype)
            if rope:
                dkp_ref[...] = jnp.zeros(dkp_ref.shape, dkp_ref.dtype)
                dkp_ref[:, :rope] = dkp_sc[...].astype(dkp_ref.dtype)

    outs = _pcall(
        body, name=name, grid=(dil, geom.nk, nt), in_specs=specs, out_specs=out_specs, out_shape=out_shape,
        scratch_shapes=scratch, compiler_params=_cparams(("parallel", "parallel", "arbitrary")),
    )(*[o_[0] for o_ in ops])
    dk, dv = outs[0].reshape(Rk, kw), outs[1].reshape(Rk, vw)
    if rope:
        return dk, dv, outs[2].reshape(Rk, LANE)
    return dk, dv


class _BandPlan:
    def __init__(self, S, dil, max_dist):
        self.L, self.dil, self.max_dist = S // dil, dil, max_dist
        self.nblk = self.L // BLOCK
        self.nb = min(4, self.nblk)
        self.ns = min(dil, max(1, 4 // self.nb))
        self.grid = (dil // self.ns, self.nblk // self.nb)
        self.rows = self.nb * BLOCK

    def view(self, p):
        arr, w, idx = _piece(p)
        R, C = arr.shape
        assert C % w == 0 and (self.ns == 1 or (C == w and idx == 0)), (C, w, idx, self.ns)
        return arr.reshape(R // self.dil, self.dil * C), w, idx, C // w

    def main(self, view):
        _, w, idx, cpw = view
        if self.ns == 1:
            return pl.BlockSpec((self.rows, w), lambda s, i: (i, s * cpw + idx))
        return pl.BlockSpec((self.rows, self.ns * w), lambda s, i: (i, s))

    def edge(self, view, nxt):
        _, w, idx, cpw = view
        nb, last = self.nb, self.nblk - 1
        rb = (lambda i: jnp.minimum((i + 1) * nb, last)) if nxt else (lambda i: jnp.maximum(i * nb - 1, 0))
        if self.ns == 1:
            return pl.BlockSpec((BLOCK, w), lambda s, i: (rb(i), s * cpw + idx))
        return pl.BlockSpec((BLOCK, self.ns * w), lambda s, i: (rb(i), s))

    def out(self, w):
        return pl.BlockSpec((self.rows, self.ns * w), lambda s, i: (i, s))

    def masks(self):
        qi = lax.broadcasted_iota(jnp.int32, (BLOCK, BLOCK), 0)
        kj = lax.broadcasted_iota(jnp.int32, (BLOCK, BLOCK), 1)
        return kj <= qi, (qi - kj + BLOCK) <= self.max_dist


def _half(e):
    lane = lax.broadcasted_iota(jnp.int32, (BLOCK, LANE), 1)
    return (lane < LANE // 2) if e == 0 else (lane >= LANE // 2)


def _swap_halves(t):
    return pltpu.roll(t, LANE // 2, axis=1)


def _kv_operand(nat, swapped, g, e):
    grp = g // 2
    return nat[grp] if g % 2 == e else swapped[grp]


def _load_groups(ref, edge_ref, b, col0, ngroups, from_edge):
    nat, swp = [], []
    for kg in range(ngroups):
        cols = slice(col0 + kg * LANE, col0 + (kg + 1) * LANE)
        t = edge_ref[:, cols] if from_edge else ref[b * BLOCK:(b + 1) * BLOCK, cols]
        t = t.astype(BF16)
        nat.append(t)
        swp.append(None)
    return nat, swp


def _band_fwd(q, k, v, *, S, dil, max_dist, hq, hk, scale, name):
    hd = 64
    plan = _BandPlan(S, dil, max_dist)
    qv, kv, vv = plan.view(q), plan.view(k), plan.view(v)
    wq, wk = hq * hd, hk * hd
    grp = hq // hk
    ns, nb = plan.ns, plan.nb
    need_swap = hk != hq

    def body(q_ref, k_ref, kp_ref, v_ref, vp_ref, o_ref, l_ref):
        i = pl.program_id(1)
        m_cur, m_band = plan.masks()
        has_prev = i > 0
        h0, h1 = _half(0), _half(1)
        for sg in range(ns):
            for b in range(nb):
                rows = slice(b * BLOCK, (b + 1) * BLOCK)
                kc, kcs = _load_groups(k_ref, None, b, sg * wk, wk // LANE, False)
                vc, vcs = _load_groups(v_ref, None, b, sg * wk, wk // LANE, False)
                if b > 0:
                    kp, kps = _load_groups(k_ref, None, b - 1, sg * wk, wk // LANE, False)
                    vp, vps = _load_groups(v_ref, None, b - 1, sg * wk, wk // LANE, False)
                    m_prev = m_band
                else:
                    kp, kps = _load_groups(None, kp_ref, 0, sg * wk, wk // LANE, True)
                    vp, vps = _load_groups(None, vp_ref, 0, sg * wk, wk // LANE, True)
                    m_prev = m_band & has_prev
                if need_swap:
                    kcs, vcs, kps, vps = ([_swap_halves(t) for t in ts] for ts in (kc, vc, kp, vp))
                for pr in range(wq // LANE):
                    cols = slice(sg * wq + pr * LANE, sg * wq + (pr + 1) * LANE)
                    q2 = q_ref[rows, cols].astype(BF16)
                    res = []
                    for e, hm in ((0, h0), (1, h1)):
                        g = (2 * pr + e) // grp
                        qm = jnp.where(hm, q2, jnp.zeros_like(q2))
                        s_c = lax.dot_general(qm, _kv_operand(kc, kcs, g, e), _NT, preferred_element_type=F32) * scale
                        s_p = lax.dot_general(qm, _kv_operand(kp, kps, g, e), _NT, preferred_element_type=F32) * scale
                        s_c = jnp.where(m_cur, s_c, NEG)
                        s_p = jnp.where(m_prev, s_p, NEG)
                        mx = jnp.maximum(jnp.max(s_c, axis=1, keepdims=True), jnp.max(s_p, axis=1, keepdims=True))
                        p_c, p_p = jnp.exp(s_c - mx), jnp.exp(s_p - mx)
                        l = jnp.sum(p_c, axis=1, keepdims=True) + jnp.sum(p_p, axis=1, keepdims=True)
                        acc = (jnp.dot(p_c.astype(BF16), _kv_operand(vc, vcs, g, e), preferred_element_type=F32)
                               + jnp.dot(p_p.astype(BF16), _kv_operand(vp, vps, g, e), preferred_element_type=F32))
                        res.append((acc / l, mx + jnp.log(l)))
                    o_ref[rows, cols] = jnp.where(h0, res[0][0], res[1][0])
                    l_ref[rows, cols] = jnp.where(h0, res[0][1], res[1][1])

    o, lse = _pcall(
        body, name=name, grid=plan.grid,
        in_specs=[plan.main(qv), plan.main(kv), plan.edge(kv, False), plan.main(vv), plan.edge(vv, False)],
        out_specs=[plan.out(wq)] * 2, out_shape=[jax.ShapeDtypeStruct((S // dil, dil * wq), F32)] * 2,
        compiler_params=_cparams(("parallel", "parallel")),
    )(qv[0], kv[0], kv[0], vv[0], vv[0])
    return o.reshape(S, wq), lse.reshape(S, wq)


def _band_dq(q, k, v, do, o, lse, dlse, *, S, dil, max_dist, hq, hk, scale, out_dtype=F32, name):
    hd = 64
    plan = _BandPlan(S, dil, max_dist)
    qv, kv, vv = plan.view(q), plan.view(k), plan.view(v)
    dov, ov, lv = plan.view(do), plan.view(o), plan.view(lse)
    has_dl = dlse is not None
    wq, wk = hq * hd, hk * hd
    grp = hq // hk
    ns, nb = plan.ns, plan.nb
    need_swap = hk != hq
    ops = [qv, kv, kv, vv, vv, dov, ov, lv]
    specs = [plan.main(qv), plan.main(kv), plan.edge(kv, False), plan.main(vv), plan.edge(vv, False), plan.main(dov),
             plan.main(ov), plan.main(lv)]
    if has_dl:
        dlv = plan.view(dlse)
        ops.append(dlv)
        specs.append(plan.main(dlv))

    def body(*refs):
        q_ref, k_ref, kp_ref, v_ref, vp_ref, do_ref, o_ref, l_ref = refs[:8]
        dl_ref = refs[8] if has_dl else None
        dq_ref = refs[-1]
        i = pl.program_id(1)
        m_cur, m_band = plan.masks()
        has_prev = i > 0
        h0, h1 = _half(0), _half(1)
        for sg in range(ns):
            for b in range(nb):
                rows = slice(b * BLOCK, (b + 1) * BLOCK)
                kc, kcs = _load_groups(k_ref, None, b, sg * wk, wk // LANE, False)
                vc, vcs = _load_groups(v_ref, None, b, sg * wk, wk // LANE, False)
                if b > 0:
                    kp, kps = _load_groups(k_ref, None, b - 1, sg * wk, wk // LANE, False)
                    vp, vps = _load_groups(v_ref, None, b - 1, sg * wk, wk // LANE, False)
                    m_prev = m_band
                else:
                    kp, kps = _load_groups(None, kp_ref, 0, sg * wk, wk // LANE, True)
                    vp, vps = _load_groups(None, vp_ref, 0, sg * wk, wk // LANE, True)
                    m_prev = m_band & has_prev
                if need_swap:
                    kcs, vcs, kps, vps = ([_swap_halves(t) for t in ts] for ts in (kc, vc, kp, vp))
                for pr in range(wq // LANE):
                    cols = slice(sg * wq + pr * LANE, sg * wq + (pr + 1) * LANE)
                    q2 = q_ref[rows, cols].astype(BF16)
                    do2 = do_ref[rows, cols].astype(F32)
                    prod = do2 * o_ref[rows, cols].astype(F32)
                    if has_dl:
                        prod = prod - dl_ref[rows, cols].astype(F32)
                    do2b = do2.astype(BF16)
                    l2 = l_ref[rows, cols]
                    res = []
                    for e, hm in ((0, h0), (1, h1)):
                        g = (2 * pr + e) // grp
                        qm = jnp.where(hm, q2, jnp.zeros_like(q2))
                        dom = jnp.where(hm, do2b, jnp.zeros_like(do2b))
                        delta = jnp.sum(jnp.where(hm, prod, 0.0), axis=1, keepdims=True)
                        lse_e = l2[:, e * 64:e * 64 + 1]
                        dq_e = None
                        for kop, vop, msk in ((_kv_operand(kc, kcs, g, e), _kv_operand(vc, vcs, g, e), m_cur),
                                              (_kv_operand(kp, kps, g, e), _kv_operand(vp, vps, g, e), m_prev)):
                            s = lax.dot_general(qm, kop, _NT, preferred_element_type=F32) * scale
                            p = jnp.where(msk, jnp.exp(s - lse_e), 0.0)
                            dp = lax.dot_general(dom, vop, _NT, preferred_element_type=F32)
                            ds = (p * (dp - delta) * scale).astype(BF16)
                            t = jnp.dot(ds, kop, preferred_element_type=F32)
                            dq_e = t if dq_e is None else dq_e + t
                        res.append(dq_e)
                    dq_ref[rows, cols] = jnp.where(h0, res[0], res[1]).astype(dq_ref.dtype)

    dq = _pcall(
        body, name=name, grid=plan.grid, in_specs=specs, out_specs=plan.out(wq),
        out_shape=jax.ShapeDtypeStruct((S // dil, dil * wq), out_dtype), compiler_params=_cparams(("parallel", "parallel")),
    )(*[o_[0] for o_ in ops])
    return dq.reshape(S, wq)


def _band_dkv(q, k, v, do, o, lse, dlse, *, S, dil, max_dist, hq, hk, scale, out_dtype=F32, name):
    hd = 64
    plan = _BandPlan(S, dil, max_dist)
    qv, kv, vv = plan.view(q), plan.view(k), plan.view(v)
    dov, ov, lv = plan.view(do), plan.view(o), plan.view(lse)
    has_dl = dlse is not None
    wq, wk = hq * hd, hk * hd
    grp = hq // hk
    ns, nb = plan.ns, plan.nb
    qlike = [qv, dov, ov, lv] + ([plan.view(dlse)] if has_dl else [])
    ops = [kv, vv] + qlike + qlike
    specs = [plan.main(kv), plan.main(vv)] + [plan.main(t) for t in qlike] + [plan.edge(t, True) for t in qlike]
    nql = len(qlike)
    nkg = wk // LANE

    def body(*refs):
        k_ref, v_ref = refs[:2]
        mains, edges = refs[2:2 + nql], refs[2 + nql:2 + 2 * nql]
        dk_ref, dv_ref = refs[2 + 2 * nql:]
        i = pl.program_id(1)
        m_cur, m_band = plan.masks()
        has_next = i < plan.grid[1] - 1
        h0, h1 = _half(0), _half(1)
        for sg in range(ns):
            for b in range(nb):
                rows = slice(b * BLOCK, (b + 1) * BLOCK)
                kc, _ = _load_groups(k_ref, None, b, sg * wk, nkg, False)
                vc, _ = _load_groups(v_ref, None, b, sg * wk, nkg, False)
                if hk != hq:
                    kcs, vcs = [_swap_halves(t) for t in kc], [_swap_halves(t) for t in vc]
                else:
                    kcs = vcs = None
                dk_nat, dk_swp = [None] * nkg, [None] * nkg
                dv_nat, dv_swp = [None] * nkg, [None] * nkg
                for rel in (0, 1):
                    if rel == 0:
                        src, qrows, msk = mains, rows, m_cur
                    elif b + 1 < nb:
                        src, qrows, msk = mains, slice((b + 1) * BLOCK, (b + 2) * BLOCK), m_band
                    else:
                        src, qrows, msk = edges, slice(0, BLOCK), m_band & has_next
                    for pr in range(wq // LANE):
                        cols = slice(sg * wq + pr * LANE, sg * wq + (pr + 1) * LANE)
                        q2 = src[0][qrows, cols].astype(BF16)
                        do2 = src[1][qrows, cols].astype(F32)
                        prod = do2 * src[2][qrows, cols].astype(F32)
                        if has_dl:
                            prod = prod - src[4][qrows, cols].astype(F32)
                        do2b = do2.astype(BF16)
                        l2 = src[3][qrows, cols]
                        for e, hm in ((0, h0), (1, h1)):
                            g = (2 * pr + e) // grp
                            kg = g // 2
                            qm = jnp.where(hm, q2, jnp.zeros_like(q2))
                            dom = jnp.where(hm, do2b, jnp.zeros_like(do2b))
                            delta = jnp.sum(jnp.where(hm, prod, 0.0), axis=1, keepdims=True)
                            kop, vop = _kv_operand(kc, kcs, g, e), _kv_operand(vc, vcs, g, e)
                            s = lax.dot_general(qm, kop, _NT, preferred_element_type=F32) * scale
                            p = jnp.where(msk, jnp.exp(s - l2[:, e * 64:e * 64 + 1]), 0.0)
                            dp = lax.dot_general(dom, vop, _NT, preferred_element_type=F32)
                            ds = (p * (dp - delta) * scale).astype(BF16)
                            tv = lax.dot_general(p.astype(BF16), dom, _TN, preferred_element_type=F32)
                            tk = lax.dot_general(ds, qm, _TN, preferred_element_type=F32)
                            if g % 2 == e:
                                dv_nat[kg] = tv if dv_nat[kg] is None else dv_nat[kg] + tv
                                dk_nat[kg] = tk if dk_nat[kg] is None else dk_nat[kg] + tk
                            else:
                                dv_swp[kg] = tv if dv_swp[kg] is None else dv_swp[kg] + tv
                                dk_swp[kg] = tk if dk_swp[kg] is None else dk_swp[kg] + tk
                for kg in range(nkg):
                    cols = slice(sg * wk + kg * LANE, sg * wk + (kg + 1) * LANE)
                    dkt, dvt = dk_nat[kg], dv_nat[kg]
                    if dk_swp[kg] is not None:
                        dkt = dkt + _swap_halves(dk_swp[kg])
                        dvt = dvt + _swap_halves(dv_swp[kg])
                    dk_ref[rows, cols] = dkt.astype(dk_ref.dtype)
                    dv_ref[rows, cols] = dvt.astype(dv_ref.dtype)

    dk, dv = _pcall(
        body, name=name, grid=plan.grid, in_specs=specs, out_specs=[plan.out(wk)] * 2,
        out_shape=[jax.ShapeDtypeStruct((S // dil, dil * wk), out_dtype)] * 2,
        compiler_params=_cparams(("parallel", "parallel")),
    )(*[o_[0] for o_ in ops])
    return dk.reshape(S, wk), dv.reshape(S, wk)


def _causal_block(S):
    return min(512, S)


def _causal_mask(bq):
    qi = lax.broadcasted_iota(jnp.int32, (bq, bq), 0)
    kj = lax.broadcasted_iota(jnp.int32, (bq, bq), 1)
    return kj <= qi


def _half_of(rows, e):
    lane = lax.broadcasted_iota(jnp.int32, (rows, LANE), 1)
    return (lane < LANE // 2) if e == 0 else (lane >= LANE // 2)


def _causal_fwd(q, k, v, *, heads, scale, name):
    S = q.shape[0]
    bq = _causal_block(S)
    nq = S // bq
    npair = heads // 2
    wv = heads * 64

    def body(q_ref, k_ref, v_ref, o_ref, l_ref, m_sc, l_sc, acc_sc):
        i, t = pl.program_id(0), pl.program_id(1)

        @pl.when(t == 0)
        def _():
            m_sc[...] = jnp.full(m_sc.shape, NEG, F32)
            l_sc[...] = jnp.zeros(l_sc.shape, F32)
            acc_sc[...] = jnp.zeros(acc_sc.shape, F32)

        def step(masked):
            msk = _causal_mask(bq) if masked else None
            h0 = _half_of(bq, 0)
            for pr in range(npair):
                v2 = v_ref[:, pr * LANE:(pr + 1) * LANE].astype(BF16)
                new = []
                for e in range(2):
                    h = 2 * pr + e
                    cols = slice(h * LANE, (h + 1) * LANE)
                    s = lax.dot_general(q_ref[:, cols], k_ref[:, cols], _NT, preferred_element_type=F32) * scale
                    if masked:
                        s = jnp.where(msk, s, NEG)
                    m_old = m_sc[h]
                    m_new = jnp.maximum(m_old, jnp.max(s, axis=1, keepdims=True))
                    p = jnp.exp(s - m_new)
                    alpha = jnp.exp(m_old - m_new)
                    l_sc[h] = alpha * l_sc[h] + jnp.sum(p, axis=1, keepdims=True)
                    m_sc[h] = m_new
                    new.append((alpha, jnp.dot(p.astype(BF16), v2, preferred_element_type=F32)))
                acc = acc_sc[pr]
                acc_sc[pr] = jnp.where(h0, new[0][0] * acc + new[0][1], new[1][0] * acc + new[1][1])

        pl.when(t < i)(lambda: step(False))
        pl.when(t == i)(lambda: step(True))

        @pl.when(t == nq - 1)
        def _():
            h0 = _half_of(bq, 0)
            for pr in range(npair):
                l0, l1 = l_sc[2 * pr], l_sc[2 * pr + 1]
                acc = acc_sc[pr]
                cols = slice(pr * LANE, (pr + 1) * LANE)
                o_ref[:, cols] = jnp.where(h0, acc / l0, acc / l1)
                l_ref[:, cols] = jnp.where(h0, m_sc[2 * pr] + jnp.log(l0), m_sc[2 * pr + 1] + jnp.log(l1))

    qs = pl.BlockSpec((bq, heads * LANE), lambda i, t: (i, 0))
    ks = pl.BlockSpec((bq, heads * LANE), lambda i, t: (jnp.minimum(t, i), 0))
    vs = pl.BlockSpec((bq, wv), lambda i, t: (jnp.minimum(t, i), 0))
    os_ = pl.BlockSpec((bq, wv), lambda i, t: (i, 0))
    return _pcall(
        body, name=name, grid=(nq, nq), in_specs=[qs, ks, vs], out_specs=[os_, os_],
        out_shape=[jax.ShapeDtypeStruct((S, wv), F32)] * 2,
        scratch_shapes=[pltpu.VMEM((heads, bq, 1), F32), pltpu.VMEM((heads, bq, 1), F32), pltpu.VMEM((npair, bq, LANE), F32)],
        compiler_params=_cparams(("parallel", "arbitrary")),
    )(q, k, v)


def _causal_bwd_tile(q_ref, k_ref, v2, do2, prod, l2, h, e, scale, msk, bq):
    cols = slice(h * LANE, (h + 1) * LANE)
    hm = _half_of(bq, e)
    s = lax.dot_general(q_ref[:, cols], k_ref[:, cols], _NT, preferred_element_type=F32) * scale
    p = jnp.exp(s - l2[:, e * 64:e * 64 + 1])
    if msk is not None:
        p = jnp.where(msk, p, 0.0)
    dom = jnp.where(hm, do2, jnp.zeros_like(do2))
    delta = jnp.sum(jnp.where(hm, prod, 0.0), axis=1, keepdims=True)
    dp = lax.dot_general(dom, v2, _NT, preferred_element_type=F32)
    ds = (p * (dp - delta) * scale).astype(BF16)
    return p, ds, dom


def _causal_dq(q, k, v, do, o, lse, *, heads, scale, out_dtype=BF16, name):
    S = q.shape[0]
    bq = _causal_block(S)
    nq = S // bq
    npair = heads // 2
    wv = heads * 64

    def body(q_ref, k_ref, v_ref, do_ref, o_ref, l_ref, dq_ref, dq_sc):
        i, t = pl.program_id(0), pl.program_id(1)

        @pl.when(t == 0)
        def _():
            dq_sc[...] = jnp.zeros(dq_sc.shape, F32)

        def step(masked):
            msk = _causal_mask(bq) if masked else None
            for pr in range(npair):
                pc = slice(pr * LANE, (pr + 1) * LANE)
                v2 = v_ref[:, pc].astype(BF16)
                dof = do_ref[:, pc].astype(F32)
                prod = dof * o_ref[:, pc]
                do2 = dof.astype(BF16)
                l2 = l_ref[:, pc]
                for e in range(2):
                    h = 2 * pr + e
                    _, ds, _ = _causal_bwd_tile(q_ref, k_ref, v2, do2, prod, l2, h, e, scale, msk, bq)
                    dq_sc[h] += jnp.dot(ds, k_ref[:, h * LANE:(h + 1) * LANE], preferred_element_type=F32)

        pl.when(t < i)(lambda: step(False))
        pl.when(t == i)(lambda: step(True))

        @pl.when(t == nq - 1)
        def _():
            for h in range(heads):
                dq_ref[:, h * LANE:(h + 1) * LANE] = dq_sc[h].astype(dq_ref.dtype)

    qs = pl.BlockSpec((bq, heads * LANE), lambda i, t: (i, 0))
    ks = pl.BlockSpec((bq, heads * LANE), lambda i, t: (jnp.minimum(t, i), 0))
    vs = pl.BlockSpec((bq, wv), lambda i, t: (jnp.minimum(t, i), 0))
    os_ = pl.BlockSpec((bq, wv), lambda i, t: (i, 0))
    return _pcall(
        body, name=name, grid=(nq, nq), in_specs=[qs, ks, vs, os_, os_, os_], out_specs=qs,
        out_shape=jax.ShapeDtypeStruct((S, heads * LANE), out_dtype),
        scratch_shapes=[pltpu.VMEM((heads, bq, LANE), F32)],
        compiler_params=_cparams(("parallel", "arbitrary")),
    )(q, k, v, do, o, lse)


def _causal_dkv(q, k, v, do, o, lse, *, heads, scale, out_dtype=BF16, name):
    S = q.shape[0]
    bq = _causal_block(S)
    nq = S // bq
    npair = heads // 2
    wv = heads * 64

    def body(q_ref, k_ref, v_ref, do_ref, o_ref, l_ref, dk_ref, dv_ref, dk_sc, dv_sc):
        j, t = pl.program_id(0), pl.program_id(1)

        @pl.when(t == 0)
        def _():
            dk_sc[...] = jnp.zeros(dk_sc.shape, F32)
            dv_sc[...] = jnp.zeros(dv_sc.shape, F32)

        def step(masked):
            msk = _causal_mask(bq) if masked else None
            for pr in range(npair):
                pc = slice(pr * LANE, (pr + 1) * LANE)
                v2 = v_ref[:, pc].astype(BF16)
                dof = do_ref[:, pc].astype(F32)
                prod = dof * o_ref[:, pc]
                do2 = dof.astype(BF16)
                l2 = l_ref[:, pc]
                dv_add = None
                for e in range(2):
                    h = 2 * pr + e
                    p, ds, dom = _causal_bwd_tile(q_ref, k_ref, v2, do2, prod, l2, h, e, scale, msk, bq)
                    tv = lax.dot_general(p.astype(BF16), dom, _TN, preferred_element_type=F32)
                    dv_add = tv if dv_add is None else dv_add + tv
                    dk_sc[h] += lax.dot_general(ds, q_ref[:, h * LANE:(h + 1) * LANE], _TN, preferred_element_type=F32)
                dv_sc[pr] += dv_add

        pl.when(t > j)(lambda: step(False))
        pl.when(t == j)(lambda: step(True))

        @pl.when(t == nq - 1)
        def _():
            for h in range(heads):
                dk_ref[:, h * LANE:(h + 1) * LANE] = dk_sc[h].astype(dk_ref.dtype)
            for pr in range(npair):
                dv_ref[:, pr * LANE:(pr + 1) * LANE] = dv_sc[pr].astype(dv_ref.dtype)

    qi = lambda j, t: (jnp.maximum(t, j), 0)
    qs = pl.BlockSpec((bq, heads * LANE), qi)
    os_ = pl.BlockSpec((bq, wv), qi)
    ks = pl.BlockSpec((bq, heads * LANE), lambda j, t: (j, 0))
    vs = pl.BlockSpec((bq, wv), lambda j, t: (j, 0))
    return _pcall(
        body, name=name, grid=(nq, nq), in_specs=[qs, ks, vs, os_, os_, os_], out_specs=[ks, vs],
        out_shape=[jax.ShapeDtypeStruct((S, heads * LANE), out_dtype), jax.ShapeDtypeStruct((S, wv), out_dtype)],
        scratch_shapes=[pltpu.VMEM((heads, bq, LANE), F32), pltpu.VMEM((npair, bq, LANE), F32)],
        compiler_params=_cparams(("parallel", "arbitrary")),
    )(q, k, v, do, o, lse)


@jax.custom_vjp
def _bdot(x, w):
    return jnp.dot(x.astype(BF16), w.astype(BF16), preferred_element_type=F32)


def _bdot_fwd(x, w):
    return _bdot(x, w), (x, w)


def _bdot_bwd(res, g):
    x, w = res
    gb = g.astype(BF16)
    dx = lax.dot_general(gb, w.astype(BF16), _NT, preferred_element_type=F32)
    dw = lax.dot_general(x.astype(BF16), gb, _TN, preferred_element_type=F32)
    return dx.astype(x.dtype), dw.astype(w.dtype)


_bdot.defvjp(_bdot_fwd, _bdot_bwd)


def _tile_matrix(hd, width):
    r = lax.broadcasted_iota(jnp.int32, (hd, width), 0)
    c = lax.broadcasted_iota(jnp.int32, (hd, width), 1) % hd
    return jnp.where(r == c, 1.0, 0.0).astype(F32)


def _spread_matrix(heads, width):
    per = width // heads
    r = lax.broadcasted_iota(jnp.int32, (heads, width), 0)
    c = lax.broadcasted_iota(jnp.int32, (heads, width), 1) // per
    return jnp.where(r == c, 1.0, 0.0).astype(F32)


def _wide(t, width):
    n = width // t.shape[-1]
    return jnp.concatenate([t] * n, axis=1) if n > 1 else t


def _norm_heads(x, gain, hd):
    return _head_norm(x, _exact_dot(gain, _tile_matrix(hd, x.shape[-1])), hd)


def _norm_rope(x, gain, cos, sin, hd):
    w = x.shape[-1]
    return _rope(_norm_heads(x, gain, hd), _wide(cos, w), _wide(sin, w), hd)


def _f_norm(x, g):
    return (_row_norm(x, g),)


def _f_prep_acm(aq, ak, c0q, c0k, c0v, c1q, c1k, c1v, c2q, c2k, c2v, mq, cos, sin, g_aq, g_ak, g0q, g0k, g1q, g1k, g2q, g2k, g_mq):
    outs = [_norm_rope(aq, g_aq, cos, sin, A_HD), _norm_rope(ak, g_ak, cos, sin, A_HD)]
    for cq, ck, cv, gq, gk in ((c0q, c0k, c0v, g0q, g0k), (c1q, c1k, c1v, g1q, g1k), (c2q, c2k, c2v, g2q, g2k)):
        outs += [_norm_rope(cq, gq, cos, sin, C_HD), _norm_rope(ck, gk, cos, sin, C_HD), cv]
    outs.append(_norm_heads(mq, g_mq, M_HD))
    return tuple(outs)


def _rope_mla_q(x, cos, sin):
    lane = lax.broadcasted_iota(jnp.int32, x.shape, x.ndim - 1) % LANE
    half = B_ROPE // 2
    first = (lane >= B_NOPE) & (lane < B_NOPE + half)
    other = jnp.where(first, -_lane_roll(x, -half), _lane_roll(x, half))
    return x * cos + other * sin


def _f_prep_b(ckv, cq, kr, cosq, sinq, cosr, sinr, g_qa, g_kva, w_uq, w_ukv, g_q, g_k, g_kr):
    wq = B_HEADS * LANE
    q_up = _bdot(_row_norm(cq, g_qa), w_uq)
    gq = _exact_dot(g_q, _tile_matrix(LANE, wq))
    qf = _rope_mla_q(_head_norm(q_up, gq, "mla"), _wide(cosq, wq), _wide(sinq, wq))
    kv_up = _bdot(_row_norm(ckv, g_kva), w_ukv)
    kn = _head_norm(kv_up[:, :wq], _exact_dot(g_k, _tile_matrix(LANE, wq)), B_NOPE)
    vb = kv_up[:, wq:]
    kp = _rope(_head_norm(kr, g_kr, B_ROPE), cosr, sinr, B_ROPE)
    kp = _lane_roll(kp, B_NOPE)
    return qf, kn + _wide(kp, wq), vb


def _f_mem_k(k, g):
    return (_norm_heads(k, g, M_HD),)


def _f_sink(o, lse, sink):
    sb = _exact_dot(sink, _spread_matrix(A_HEADS, o.shape[-1]))
    m = jnp.maximum(lse, sb)
    tot = m + jnp.log(jnp.exp(lse - m) + jnp.exp(sb - m))
    return (o * jnp.exp(lse - tot),)


def _f_combine(o0, o1, o2, l0, l1, l2):
    m = jnp.maximum(jnp.maximum(l0, l1), l2)
    e0, e1, e2 = jnp.exp(l0 - m), jnp.exp(l1 - m), jnp.exp(l2 - m)
    inv = 1.0 / (e0 + e1 + e2)
    return ((e0 * o0 + e1 * o1 + e2 * o2) * inv,)


def _f_gatemix(gp, y0, y1, y2, y3, bg):
    d = y0.shape[-1]
    gates = 1.0 / (1.0 + jnp.exp(-(gp + bg)))
    mix = gates[:, :d] * y0
    for n, y in enumerate((y1, y2, y3), start=1):
        mix = mix + gates[:, n * d:(n + 1) * d] * y
    return (mix,)


def _relu2(u):
    return jnp.square(jnp.maximum(u, 0.0))


def _add(r, e):
    return r + e.astype(F32)


def _relu2_grad(r, u):
    return r * (2.0 * jnp.maximum(u, 0.0))


def _loss_and_grad(y, target, *, tb=512):
    R, D = y.shape
    tb = min(tb, R)

    def body(y_ref, t_ref, dy_ref, l_ref):
        err = y_ref[...] - t_ref[...]
        dy_ref[...] = err * (1.0 / D)
        part = 0.5 * jnp.sum(jnp.sum(err * err, axis=1, keepdims=True) * (1.0 / D), axis=0, keepdims=True)
        first = pl.program_id(0) == 0

        @pl.when(first)
        def _():
            l_ref[...] = jnp.broadcast_to(part, l_ref.shape)

        @pl.when(jnp.logical_not(first))
        def _():
            l_ref[...] += jnp.broadcast_to(part, l_ref.shape)

    dy, l = _pcall(
        body, name="loss", grid=(R // tb,),
        in_specs=[pl.BlockSpec((tb, D), lambda i: (i, 0))] * 2,
        out_specs=[pl.BlockSpec((tb, D), lambda i: (i, 0)), pl.BlockSpec((8, LANE), lambda i: (0, 0))],
        out_shape=[jax.ShapeDtypeStruct((R, D), F32), jax.ShapeDtypeStruct((8, LANE), F32)],
        compiler_params=_cparams(("arbitrary",)),
    )(y, target)
    return l[0, 0], dy


def _z_layout(d):
    assert d == 1024, "the aligned layout below is laid out for D_MODEL = 1024"
    lay = {"gates": (4 * d, 0)}
    for g in range(3):
        for n, nm in enumerate("qkv"):
            lay[f"c{g}{nm}"] = (512, 8 + 3 * g + n)
    lay.update(aq=(512, 17), mq=(512, 18), ckv=(256, 38), cq=(384, 26), ak=(128, 81), av=(128, 82), kr=(128, 83))
    return lay, 10752


_KW_A = dict(hq=A_HEADS, hk=A_KV_HEADS, scale=A_HD ** -0.5)
_KW_B = dict(heads=B_HEADS, scale=(B_NOPE + B_ROPE) ** -0.5)
_KW_C = dict(hq=C_HEADS, hk=C_HEADS, scale=C_HD ** -0.5)
_KW_M = dict(hq=M_HEADS, hk=M_HEADS, hd=M_HD, hdv=M_HD, scale=M_HD ** -0.5)


def _layer_fwd(l, x, mem, w, tabs):
    S, D = x.shape
    lay, _ = _z_layout(D)
    cosA, sinA, cosB, sinB, cosQ, sinQ = tabs
    gM = _AttnGeom("full", S, mem.shape[0])
    nm = lambda s: f"l{l}_{s}"
    sv = {}
    hn = _rowmap(_f_norm, [x], [w["g_mix"]], [(D, BF16)], tb=512, name=nm("norm1"))[0]
    z = _mm(hn, w["in"], name=nm("in"), tn=768)
    zp = {k: (z, wd, idx) for k, (wd, idx) in lay.items()}
    acm_rows = [zp[k] for k in ("aq", "ak", "c0q", "c0k", "c0v", "c1q", "c1k", "c1v", "c2q", "c2k", "c2v", "mq")] + [cosA, sinA]
    acm_par = [w["a_qn"], w["a_kn"], w["c0q"], w["c0k"], w["c1q"], w["c1k"], w["c2q"], w["c2k"], w["m_qn"]]
    acm = _rowmap(_f_prep_acm, acm_rows, acm_par, [(p[1], BF16) for p in acm_rows[:12]], tb=256, name=nm("prep_acm"))
    qa, ka, qc0, kc0, vc0, qc1, kc1, vc1, qc2, kc2, vc2, mq = acm
    b_rows = [zp["ckv"], zp["cq"], zp["kr"], cosQ, sinQ, cosB, sinB]
    b_par = [w["b_qa"], w["b_kva"], w["uq"], w["ukv"], w["b_q"], w["b_k"], w["b_kr"]]
    qb, kb, vb = _rowmap(_f_prep_b, b_rows, b_par, [(B_HEADS * LANE, BF16), (B_HEADS * LANE, BF16), (512, BF16)], tb=256,
                         name=nm("prep_b"))
    memn = _rowmap(_f_norm, [mem], [w["m_g_mem"]], [(D, BF16)], tb=256, name=nm("mem_norm"))[0]
    mkv = _mm(memn, w["mkv"], name=nm("mem_kv"))
    mk = _rowmap(_f_mem_k, [(mkv, 512, 0)], [w["m_kn"]], [(512, BF16)], tb=256, name=nm("mem_k"))[0]
    mv = (mkv, 512, 1)

    oa_raw, lse_a = _band_fwd(qa, ka, zp["av"], S=S, dil=1, max_dist=A_WINDOW - 1, name=nm("attn_a"), **_KW_A)
    o_a = _rowmap(_f_sink, [oa_raw, lse_a], [w["a_sink"]], [(512, BF16)], tb=512, name=nm("sink"))[0]
    o_b, lse_b = _causal_fwd(qb, kb, vb, name=nm("attn_b"), **_KW_B)
    oc, lc = [], []
    for g, ((win, dil), qc, kc, vc) in enumerate(zip(C_PATTERNS, (qc0, qc1, qc2), (kc0, kc1, kc2), (vc0, vc1, vc2))):
        o_g, l_g = _band_fwd(qc, kc, vc, S=S, dil=dil, max_dist=win // dil, name=nm(f"attn_c{g}"), **_KW_C)
        oc.append(o_g)
        lc.append(l_g)
    o_c = _rowmap(_f_combine, oc + lc, [], [(512, BF16)], tb=512, name=nm("combine"))[0]
    o_m, lse_m = _attn_fwd(gM, mq, mk, mv, name=nm("attn_m"), **_KW_M)

    o_n = [o_a, o_b, o_c, o_m]
    ys = [_mm(o_n[n], w["branch"][n], name=nm(f"branch{n}")) for n in range(N_BRANCH)]
    mix = _rowmap(_f_gatemix, [zp["gates"]] + ys, [w["b_gate"]], [(D, BF16)], tb=256, name=nm("gatemix"))[0]
    x1 = _mm(mix, w["out"], extra=x, epi=_add, name=nm("out"))
    hn2 = _rowmap(_f_norm, [x1], [w["g_mlp"]], [(D, BF16)], tb=512, name=nm("norm2"))[0]
    u = _mm(hn2, w["up"], name=nm("up"))
    x2 = _mm(u, w["down"], pro_a=_relu2, extra=x1, epi=_add, name=nm("down"))
    sv.update(x=x, hn=hn, z=z, acm=acm, bq=(qb, kb, vb), memn=memn, mkv=mkv, mk=mk, oa_raw=oa_raw, lse_a=lse_a,
              o_b=o_b, lse_b=lse_b, oc=oc, lc=lc, o_m=o_m, lse_m=lse_m, o_n=o_n, ys=ys, mix=mix, x1=x1, hn2=hn2, u=u)
    return x2, sv


def _layer_bwd(l, dx2, mem, w, tabs, sv):
    x, z, x1, u = sv["x"], sv["z"], sv["x1"], sv["u"]
    S, D = x.shape
    lay, zw = _z_layout(D)
    cosA, sinA, cosB, sinB, cosQ, sinQ = tabs
    gM = _AttnGeom("full", S, mem.shape[0])
    nm = lambda s: f"l{l}_{s}"
    zp = {k: (z, wd, idx) for k, (wd, idx) in lay.items()}
    g = {}
    du = _mm(dx2, w["down"], tb=True, extra=u, epi=_relu2_grad, out_dtype=BF16, name=nm("d_down_x"))
    g["down"] = _mm(u, dx2, ta=True, pro_a=_relu2, name=nm("d_down_w"))
    dhn2 = _mm(du, w["up"], tb=True, name=nm("d_up_x"))
    g["up"] = _mm(sv["hn2"], du, ta=True, name=nm("d_up_w"))
    (dx1,), (g["g_mlp"],) = _rowmap_bwd(_f_norm, [x1], [w["g_mlp"]], [dhn2], diff=[True], out_dtypes=[F32], adds=[dx2],
                                        tb=256, name=nm("d_norm2"))
    dmix = _mm(dx1, w["out"], tb=True, name=nm("d_out_x"))
    g["out"] = _mm(sv["mix"], dx1, ta=True, name=nm("d_out_w"))
    (dgates, dy0, dy1, dy2, dy3), (g["b_gate"],) = _rowmap_bwd(
        _f_gatemix, [zp["gates"]] + sv["ys"], [w["b_gate"]], [dmix], diff=[True] * 5, out_dtypes=[BF16] * 5,
        tb=128, name=nm("d_gatemix"))
    dys = [dy0, dy1, dy2, dy3]
    do = [_mm(dys[n], w["branch"][n], tb=True, name=nm(f"d_branch{n}_x")) for n in range(N_BRANCH)]
    g["branch"] = [_mm(sv["o_n"][n], dys[n], ta=True, name=nm(f"d_branch{n}_w")) for n in range(N_BRANCH)]
    qa, ka, qc0, kc0, vc0, qc1, kc1, vc1, qc2, kc2, vc2, mq = sv["acm"]
    qb, kb, vb = sv["bq"]
    mkv, mk = sv["mkv"], sv["mk"]
    mv = (mkv, 512, 1)
    dmq = _attn_dq(gM, mq, mk, mv, do[3], sv["o_m"], sv["lse_m"], None, name=nm("attn_m_dq"), **_KW_M)
    dmk, dmv = _attn_dkv(gM, mq, mk, mv, do[3], sv["o_m"], sv["lse_m"], None, name=nm("attn_m_dkv"), **_KW_M)
    (doc0, doc1, doc2, dl0, dl1, dl2), _ = _rowmap_bwd(_f_combine, sv["oc"] + sv["lc"], [], [do[2]], diff=[True] * 6,
                                                      out_dtypes=[F32] * 6, tb=256, name=nm("d_combine"))
    dqc, dkc, dvc = [], [], []
    for gi, ((win, dil), qc, kc, vc, doc, dl) in enumerate(zip(C_PATTERNS, (qc0, qc1, qc2), (kc0, kc1, kc2), (vc0, vc1, vc2),
                                                              (doc0, doc1, doc2), (dl0, dl1, dl2))):
        args = (qc, kc, vc, doc, sv["oc"][gi], sv["lc"][gi], dl)
        kwc = dict(S=S, dil=dil, max_dist=win // dil, out_dtype=BF16, **_KW_C)
        dqc.append(_band_dq(*args, name=nm(f"attn_c{gi}_dq"), **kwc))
        dk_, dv_ = _band_dkv(*args, name=nm(f"attn_c{gi}_dkv"), **kwc)
        dkc.append(dk_)
        dvc.append(dv_)
    argsb = (qb, kb, vb, do[1], sv["o_b"], sv["lse_b"])
    dqb = _causal_dq(*argsb, name=nm("attn_b_dq"), **_KW_B)
    dkb, dvb = _causal_dkv(*argsb, name=nm("attn_b_dkv"), **_KW_B)
    (doa_raw, dlse_a), (g["a_sink"],) = _rowmap_bwd(_f_sink, [sv["oa_raw"], sv["lse_a"]], [w["a_sink"]], [do[0]],
                                                   diff=[True, True], out_dtypes=[F32, F32], tb=256, name=nm("d_sink"))
    argsa = (qa, ka, zp["av"], doa_raw, sv["oa_raw"], sv["lse_a"], dlse_a)
    kwa = dict(S=S, dil=1, max_dist=A_WINDOW - 1, out_dtype=BF16, **_KW_A)
    dqa = _band_dq(*argsa, name=nm("attn_a_dq"), **kwa)
    dka, dva = _band_dkv(*argsa, name=nm("attn_a_dkv"), **kwa)
    acm_rows = [zp[k] for k in ("aq", "ak", "c0q", "c0k", "c0v", "c1q", "c1k", "c1v", "c2q", "c2k", "c2v", "mq")] + [cosA, sinA]
    acm_par = [w["a_qn"], w["a_kn"], w["c0q"], w["c0k"], w["c1q"], w["c1k"], w["c2q"], w["c2k"], w["m_qn"]]
    acm_ct = [dqa, dka, dqc[0], dkc[0], dvc[0], dqc[1], dkc[1], dvc[1], dqc[2], dkc[2], dvc[2], dmq]
    dacm, (g["a_qn"], g["a_kn"], g["c0q"], g["c0k"], g["c1q"], g["c1k"], g["c2q"], g["c2k"], g["m_qn"]) = _rowmap_bwd(
        _f_prep_acm, acm_rows, acm_par, acm_ct, diff=[True] * 12 + [False, False], out_dtypes=[BF16] * 12, tb=128,
        name=nm("d_prep_acm"))
    d_aq, d_ak, d_c0q, d_c0k, d_c0v, d_c1q, d_c1k, d_c1v, d_c2q, d_c2k, d_c2v, d_mq = dacm
    b_rows = [zp["ckv"], zp["cq"], zp["kr"], cosQ, sinQ, cosB, sinB]
    b_par = [w["b_qa"], w["b_kva"], w["uq"], w["ukv"], w["b_q"], w["b_k"], w["b_kr"]]
    (d_ckv, d_cq, d_kr), gb = _rowmap_bwd(_f_prep_b, b_rows, b_par, [dqb, dkb, dvb], diff=[True] * 3 + [False] * 4,
                                          out_dtypes=[BF16] * 3, tb=256, name=nm("d_prep_b"))
    g["b_qa"], g["b_kva"], g["uq"], g["ukv"], g["b_q"], g["b_k"], g["b_kr"] = gb
    (dmkv_k,), (g["m_kn"],) = _rowmap_bwd(_f_mem_k, [(mkv, 512, 0)], [w["m_kn"]], [dmk], diff=[True], out_dtypes=[F32],
                                          tb=256, name=nm("d_mem_k"))
    dmkv = jnp.concatenate([dmkv_k, dmv], axis=1)
    dmemn = _mm(dmkv, w["mkv"], tb=True, name=nm("d_mem_kv_x"))
    g["mkv"] = _mm(sv["memn"], dmkv, ta=True, name=nm("d_mem_kv_w"))
    _, (g["m_g_mem"],) = _rowmap_bwd(_f_norm, [mem], [w["m_g_mem"]], [dmemn], diff=[True], out_dtypes=[F32], tb=256,
                                     name=nm("d_mem_norm"))
    dz = jnp.concatenate([dgates, d_c0q, d_c0k, d_c0v, d_c1q, d_c1k, d_c1v, d_c2q, d_c2k, d_c2v, d_aq, d_mq, d_ckv, d_cq,
                          d_ak, dva, d_kr], axis=1)
    assert dz.shape[1] == zw
    dhn = _mm(dz, w["in"], tb=True, name=nm("d_in_x"), tk=768)
    g["in"] = _mm(sv["hn"], dz, ta=True, name=nm("d_in_w"), tn=768)
    (dx,), (g["g_mix"],) = _rowmap_bwd(_f_norm, [x], [w["g_mix"]], [dhn], diff=[True], out_dtypes=[F32], adds=[dx1],
                                       tb=256, name=nm("d_norm1"))
    return dx, g


_IN_ORIG = dict(aq=(0, 512), ak=(512, 640), av=(640, 768), cq=(768, 1152), ckv=(1152, 1408), kr=(1408, 1440),
                c=(1440, 6048), mq=(6048, 6560), gates=(6560, 10656))
_IN_OURS = dict(gates=(0, 4096), c=(4096, 8704), aq=(8704, 9216), mq=(9216, 9728), ckv=(9728, 9984), cq=(9984, 10368),
                ak=(10368, 10496), av=(10496, 10624), kr=(10624, 10656))
_IN_ORDER_ORIG = ("aq", "ak", "av", "cq", "ckv", "kr", "c", "mq", "gates")


def _in_to_ours(w_in):
    pc = {k: w_in[..., a:b] for k, (a, b) in _IN_ORIG.items()}
    zeros = lambda n: jnp.zeros(w_in.shape[:-1] + (n,), w_in.dtype)
    return jnp.concatenate([pc["gates"], pc["c"], pc["aq"], pc["mq"], pc["ckv"], pc["cq"], pc["ak"], pc["av"], pc["kr"],
                            zeros(96)], axis=-1)


def _in_from_ours(g_in):
    return jnp.concatenate([g_in[..., _IN_OURS[k][0]:_IN_OURS[k][1]] for k in _IN_ORDER_ORIG], axis=-1)


def _uq_to_ours(w):
    per = B_NOPE + B_ROPE
    w4 = w.reshape(w.shape[:-1] + (B_HEADS, per))
    w4 = jnp.pad(w4, [(0, 0)] * (w4.ndim - 1) + [(0, LANE - per)])
    return w4.reshape(w.shape[:-1] + (B_HEADS * LANE,))


def _uq_from_ours(g):
    per = B_NOPE + B_ROPE
    g4 = g.reshape(g.shape[:-1] + (B_HEADS, LANE))[..., :per]
    return g4.reshape(g.shape[:-1] + (B_HEADS * per,))


def _ukv_to_ours(w):
    w4 = w.reshape(w.shape[:-1] + (B_HEADS, B_NOPE + B_V))
    keys = jnp.pad(w4[..., :B_NOPE], [(0, 0)] * (w4.ndim - 1) + [(0, LANE - B_NOPE)])
    vals = w4[..., B_NOPE:]
    return jnp.concatenate([keys.reshape(w.shape[:-1] + (B_HEADS * LANE,)), vals.reshape(w.shape[:-1] + (B_HEADS * B_V,))],
                           axis=-1)


def _ukv_from_ours(g):
    wq = B_HEADS * LANE
    keys = g[..., :wq].reshape(g.shape[:-1] + (B_HEADS, LANE))[..., :B_NOPE]
    vals = g[..., wq:].reshape(g.shape[:-1] + (B_HEADS, B_V))
    return jnp.concatenate([keys, vals], axis=-1).reshape(g.shape[:-1] + (B_HEADS * (B_NOPE + B_V),))


def _layer_weights(big, small, l):
    row = lambda a: a[l].reshape(1, -1)
    w = dict(big_l for big_l in ((k, v[l]) for k, v in big.items()))
    w.update(g_mix=row(small["g_mix"]), b_gate=row(small["b_gate"]), a_qn=row(small["a_qn"]), a_kn=row(small["a_kn"]),
             a_sink=row(small["a_sink"]), b_qa=row(small["b_qa_norm"]), b_kva=row(small["b_kva_norm"]),
             b_q=jnp.pad(small["b_qn"][l], (0, LANE - B_NOPE - B_ROPE)).reshape(1, -1),
             b_k=jnp.pad(small["b_kn"][l, :B_NOPE], (0, LANE - B_NOPE)).reshape(1, -1),
             b_kr=jnp.pad(small["b_kn"][l, B_NOPE:], (0, LANE - B_ROPE)).reshape(1, -1),
             m_g_mem=row(small["m_g_mem"]), m_qn=row(small["m_qn"]), m_kn=row(small["m_kn"]), g_mlp=row(small["g_mlp"]))
    for g in range(3):
        w[f"c{g}q"] = small["c_qn"][l, g].reshape(1, -1)
        w[f"c{g}k"] = small["c_kn"][l, g].reshape(1, -1)
    return w


def _rope_tables(positions):
    pos = positions.astype(F32)[:, None]
    tabs = []
    for dim in (A_HD, B_ROPE):
        inv = ROPE_THETA ** (-jnp.arange(0, dim, 2, dtype=F32) / dim)
        ang = pos * inv
        reps = LANE // (dim // 2)
        tabs += [jnp.tile(jnp.cos(ang), (1, reps)), jnp.tile(jnp.sin(ang), (1, reps))]
    half = B_ROPE // 2
    cb, sb = tabs[2][:, :half], tabs[3][:, :half]
    ones, zeros = jnp.ones((pos.shape[0], B_NOPE), F32), jnp.zeros((pos.shape[0], B_NOPE), F32)
    pad = LANE - B_NOPE - B_ROPE
    tabs.append(jnp.concatenate([ones, cb, cb, ones[:, :pad]], axis=1))
    tabs.append(jnp.concatenate([zeros, sb, sb, zeros[:, :pad]], axis=1))
    return tuple(tabs)


def _local_step(x, mem, positions, big, small, loss_target):
    depth = small["g_mix"].shape[0]
    tabs = _rope_tables(positions)
    ws = [_layer_weights(big, small, l) for l in range(depth)]
    saved = []
    h = x
    for l in range(depth):
        h, sv = _layer_fwd(l, h, mem, ws[l], tabs)
        saved.append(sv)
    loss, dh = _loss_and_grad(h, loss_target)
    grads = [None] * depth
    for l in reversed(range(depth)):
        dh, grads[l] = _layer_bwd(l, dh, mem, ws[l], tabs, saved[l])
    return loss, dh, grads


def _grads_to_reference_layout(grads):
    st = lambda k: jnp.stack([g[k] for g in grads])
    flat = lambda k: jnp.stack([g[k].reshape(-1) for g in grads])
    out = dict(
        g_mix=flat("g_mix"), w_in=_in_from_ours(st("in")), b_gate=flat("b_gate"), a_qn=flat("a_qn"), a_kn=flat("a_kn"),
        a_sink=flat("a_sink"), b_qa_norm=flat("b_qa"), b_kva_norm=flat("b_kva"),
        b_w_uq=_uq_from_ours(st("uq")), b_w_ukv=_ukv_from_ours(st("ukv")),
        b_qn=flat("b_q")[:, :B_NOPE + B_ROPE],
        b_kn=jnp.concatenate([flat("b_k")[:, :B_NOPE], flat("b_kr")[:, :B_ROPE]], axis=1),
        c_qn=jnp.stack([jnp.stack([g[f"c{i}q"].reshape(-1) for i in range(3)]) for g in grads]),
        c_kn=jnp.stack([jnp.stack([g[f"c{i}k"].reshape(-1) for i in range(3)]) for g in grads]),
        m_g_mem=flat("m_g_mem"), m_w_kv=st("mkv"), m_qn=flat("m_qn"), m_kn=flat("m_kn"),
        w_branch=jnp.stack([jnp.stack(g["branch"]) for g in grads]), w_out=st("out"), g_mlp=flat("g_mlp"),
        w_up=st("up"), w_down=st("down"))
    return out


def _big_to_kernel_layout(full):
    c = lambda a: a.astype(BF16)
    return {"in": c(_in_to_ours(full["w_in"])), "uq": c(_uq_to_ours(full["b_w_uq"])),
            "ukv": c(_ukv_to_ours(full["b_w_ukv"])), "mkv": c(full["m_w_kv"]),
            "branch": c(full["w_branch"]), "out": c(full["w_out"]), "up": c(full["w_up"]), "down": c(full["w_down"])}


MESH = pl.DeviceIdType.MESH
N_CHIPS = 4
N_DEV = 8
_ANY = pl.BlockSpec(memory_space=pl.ANY)


def _chip_exchange(arrays, *, gather, name):
    n = len(arrays)

    def body(*refs):
        ins, outs = refs[:n], refs[n:2 * n]
        send_sems, recv_sems, local_sems = refs[2 * n:]
        x, y, c = lax.axis_index("x"), lax.axis_index("y"), lax.axis_index("c")
        me = 2 * x + y
        chips = [(1 - x, y), (x, 1 - y), (1 - x, 1 - y)]
        locals_, sends = [], []
        for a in range(n):
            src_own = ins[a] if gather else ins[a].at[me]
            lc = pltpu.make_async_copy(src_own, outs[a].at[me], local_sems.at[a])
            lc.start()
            locals_.append(lc)
            for j, (px, py) in enumerate(chips):
                src = ins[a] if gather else ins[a].at[2 * px + py]
                cp = pltpu.make_async_remote_copy(src_ref=src, dst_ref=outs[a].at[me], send_sem=send_sems.at[3 * a + j],
                                                  recv_sem=recv_sems.at[3 * a + j], device_id=(px, py, c), device_id_type=MESH)
                cp.start()
                sends.append(cp)
        for a in range(n):
            for j, (px, py) in enumerate(chips):
                src = ins[a] if gather else ins[a].at[me]
                pltpu.make_async_remote_copy(src_ref=src, dst_ref=outs[a].at[2 * px + py], send_sem=send_sems.at[3 * a + j],
                                             recv_sem=recv_sems.at[3 * a + j], device_id=(px, py, c),
                                             device_id_type=MESH).wait_recv()
        for cp in sends:
            cp.wait_send()
        for lc in locals_:
            lc.wait()

    out_shape = [jax.ShapeDtypeStruct(((N_CHIPS,) + a.shape) if gather else a.shape, a.dtype) for a in arrays]
    return _pcall(
        body, name=name, in_specs=[_ANY] * n, out_specs=[_ANY] * n, out_shape=out_shape,
        scratch_shapes=[pltpu.SemaphoreType.DMA((3 * n,)), pltpu.SemaphoreType.DMA((3 * n,)), pltpu.SemaphoreType.DMA((n,))],
        compiler_params=pltpu.CompilerParams(has_side_effects=True),
    )(*arrays)


def _sibling_exchange(arrays, *, name):
    n = len(arrays)

    def body(*refs):
        ins, outs = refs[:n], refs[n:2 * n]
        send_sems, recv_sems = refs[2 * n:]
        x, y, c = lax.axis_index("x"), lax.axis_index("y"), lax.axis_index("c")
        cps = []
        for a in range(n):
            cp = pltpu.make_async_remote_copy(src_ref=ins[a], dst_ref=outs[a], send_sem=send_sems.at[a], recv_sem=recv_sems.at[a],
                                              device_id=(x, y, 1 - c), device_id_type=MESH)
            cp.start()
            cps.append(cp)
        for cp in cps:
            cp.wait()

    return _pcall(
        body, name=name, in_specs=[_ANY] * n, out_specs=[_ANY] * n,
        out_shape=[jax.ShapeDtypeStruct(a.shape, a.dtype) for a in arrays],
        scratch_shapes=[pltpu.SemaphoreType.DMA((n,)), pltpu.SemaphoreType.DMA((n,))],
        compiler_params=pltpu.CompilerParams(has_side_effects=True),
    )(*arrays)


def _allreduce_small(v, *, name):
    rows = v.shape[0]

    def body(v_ref, o_ref, slots, send_sems, recv_sems):
        x, y, c = lax.axis_index("x"), lax.axis_index("y"), lax.axis_index("c")
        me = 4 * x + 2 * y + c
        slots[me] = v_ref[...]
        cps = []
        for k in range(1, N_DEV):
            fx, fy, fc = (k >> 2) & 1, (k >> 1) & 1, k & 1
            peer = (x ^ fx, y ^ fy, c ^ fc)
            cp = pltpu.make_async_remote_copy(src_ref=v_ref, dst_ref=slots.at[me], send_sem=send_sems.at[k - 1],
                                              recv_sem=recv_sems.at[k - 1], device_id=peer, device_id_type=MESH)
            cp.start()
            cps.append((cp, peer))
        for k, (cp, (px, py, pc)) in enumerate(cps):
            pltpu.make_async_remote_copy(src_ref=v_ref, dst_ref=slots.at[4 * px + 2 * py + pc], send_sem=send_sems.at[k],
                                         recv_sem=recv_sems.at[k], device_id=(px, py, pc), device_id_type=MESH).wait_recv()
        for cp, _ in cps:
            cp.wait_send()
        tot = slots[0]
        for d in range(1, N_DEV):
            tot = tot + slots[d]
        o_ref[...] = tot

    vm = pl.BlockSpec(memory_space=pltpu.VMEM)
    return _pcall(
        body, name=name, in_specs=[vm], out_specs=vm, out_shape=jax.ShapeDtypeStruct(v.shape, F32),
        scratch_shapes=[pltpu.VMEM((N_DEV, rows, LANE), F32), pltpu.SemaphoreType.DMA((N_DEV - 1,)),
                        pltpu.SemaphoreType.DMA((N_DEV - 1,))],
        compiler_params=pltpu.CompilerParams(has_side_effects=True),
    )(v)


def _rows_block(rows, cols, itemsize=4, target_bytes=1 << 20):
    want = max(16, target_bytes // max(1, cols * itemsize))
    best = rows
    for t in range(16, rows, 16):
        if rows % t == 0 and t <= want:
            best = t
    return best if best <= want or rows <= want else rows


def _sum_slots(recv, *, name):
    shp = recv.shape[1:]
    r3 = recv.reshape(N_CHIPS, -1, shp[-1])
    rows, cols = r3.shape[1:]
    tb = _rows_block(rows, cols)

    def body(r0, r1, r2, r3_, o_ref):
        o_ref[...] = ((r0[...].astype(F32) + r1[...].astype(F32)) + r2[...].astype(F32)) + r3_[...].astype(F32)

    out = _pcall(
        body, name=name, grid=(rows // tb,),
        in_specs=[pl.BlockSpec((None, tb, cols), lambda i, k=k: (k, i, 0)) for k in range(N_CHIPS)],
        out_specs=pl.BlockSpec((tb, cols), lambda i: (i, 0)), out_shape=jax.ShapeDtypeStruct((rows, cols), F32),
        compiler_params=_cparams(("parallel",)),
    )(r3, r3, r3, r3)
    return out.reshape(shp)


def _adamw(w, g_parts, m, v, *, name):
    shp = w.shape
    two = lambda a: a.reshape(-1, shp[-1])
    rows, cols = two(w).shape
    tb = _rows_block(rows, cols, target_bytes=1 << 19)
    npart = len(g_parts)

    def body(*refs):
        w_ref = refs[0]
        gp = refs[1:1 + npart]
        m_ref, v_ref, g_out, d_out, m_out, v_out = refs[1 + npart:]
        g = gp[0][...]
        for r in gp[1:]:
            g = g + r[...]
        wv = w_ref[...]
        m2 = ADAM_B1 * m_ref[...] + (1.0 - ADAM_B1) * g
        v2 = ADAM_B2 * v_ref[...] + (1.0 - ADAM_B2) * jnp.square(g)
        m_hat = m2 / (1.0 - ADAM_B1 ** ADAM_STEP)
        v_hat = v2 / (1.0 - ADAM_B2 ** ADAM_STEP)
        g_out[...] = g
        d_out[...] = -ADAM_LR * (m_hat / (jnp.sqrt(v_hat) + ADAM_EPS) + ADAM_WD * wv)
        m_out[...] = m2
        v_out[...] = v2

    spec = pl.BlockSpec((tb, cols), lambda i: (i, 0))
    outs = _pcall(
        body, name=name, grid=(rows // tb,), in_specs=[spec] * (3 + npart), out_specs=[spec] * 4,
        out_shape=[jax.ShapeDtypeStruct((rows, cols), F32)] * 4, compiler_params=_cparams(("parallel",)),
    )(two(w), *[two(p) for p in g_parts], two(m), two(v))
    return [o.reshape(shp) for o in outs]


BIG = ("w_in", "b_w_uq", "b_w_ukv", "m_w_kv", "w_branch", "w_out", "w_up", "w_down")
_SHARD_AXIS = dict(w_in=2, b_w_uq=2, b_w_ukv=2, m_w_kv=1, w_branch=3, w_out=1, w_up=2, w_down=1)
SMALL = ("g_mix", "b_gate", "a_qn", "a_kn", "a_sink", "b_qa_norm", "b_kva_norm", "b_qn", "b_kn", "c_qn", "c_kn",
         "m_g_mem", "m_qn", "m_kn", "g_mlp")
WEIGHTS = ("g_mix", "w_in", "b_gate", "a_qn", "a_kn", "a_sink", "b_qa_norm", "b_kva_norm", "b_w_uq", "b_w_ukv", "b_qn", "b_kn",
           "c_qn", "c_kn", "m_g_mem", "m_w_kv", "m_qn", "m_kn", "w_branch", "w_out", "g_mlp", "w_up", "w_down")


def _unshard(gathered, axis):
    moved = jnp.moveaxis(gathered, 0, axis)
    shp = list(gathered.shape[1:])
    shp[axis] *= N_CHIPS
    return moved.reshape(shp)


def _shard_parts(full, axis):
    shp = list(full.shape)
    shp[axis:axis + 1] = [N_CHIPS, shp[axis] // N_CHIPS]
    return jnp.moveaxis(full.reshape(shp), axis, 0)


def _pack_small(d):
    flat = jnp.concatenate([d[k].reshape(-1).astype(F32) for k in SMALL])
    n = flat.shape[0]
    pad = (-n) % (8 * LANE)
    return jnp.pad(flat, (0, pad)).reshape(-1, LANE)


def _unpack_small(packed, like):
    flat = packed.reshape(-1)
    out, off = {}, 0
    for k in SMALL:
        n = int(np.prod(like[k].shape))
        out[k] = flat[off:off + n].reshape(like[k].shape)
        off += n
    return out


def _train_step(x, mem, positions, loss_target, w, m, v):
    shards = [w[k].astype(BF16) for k in BIG]
    gathered = _chip_exchange(shards, gather=True, name="gather_weights")
    full = {k: _unshard(g, _SHARD_AXIS[k]) for k, g in zip(BIG, gathered)}
    big = _big_to_kernel_layout(full)
    small = {k: w[k] for k in SMALL}
    loss, gx, grads = _local_step(x[0], mem[0], positions[0], big, small, loss_target[0])
    gref = _grads_to_reference_layout(grads)
    loss = lax.psum(loss, ("x", "y", "c"))
    parts = [_shard_parts(gref[k], _SHARD_AXIS[k]).astype(BF16) for k in BIG]
    recv = _chip_exchange(parts, gather=False, name="scatter_grads")
    mine = [_sum_slots(r, name=f"sum_{k}") for k, r in zip(BIG, recv)]
    theirs = _sibling_exchange(mine, name="sibling_grads")
    res = {}
    for k, p, q in zip(BIG, mine, theirs):
        res[k] = _adamw(w[k], [p, q], m[k], v[k], name=f"adamw_{k}")
    g_small = _allreduce_small(_pack_small({k: gref[k] for k in SMALL}), name="allreduce_small")
    packed = _adamw(_pack_small(small), [g_small], _pack_small({k: m[k] for k in SMALL}), _pack_small({k: v[k] for k in SMALL}),
                    name="adamw_small")
    unpacked = [_unpack_small(p, small) for p in packed]
    for k in SMALL:
        res[k] = [u[k] for u in unpacked]
    outs = [loss, gx[None]]
    for i in range(4):
        outs += [res[k][i] for k in WEIGHTS]
    return tuple(outs)


def kernel(x, mem, positions, g_mix, w_in, b_gate, a_qn, a_kn, a_sink, b_qa_norm, b_kva_norm, b_w_uq, b_w_ukv, b_qn, b_kn, c_qn, c_kn, m_g_mem, m_w_kv, m_qn, m_kn, w_branch, w_out, g_mlp, w_up, w_down, loss_target, m_g_mix, m_w_in, m_b_gate, m_a_qn, m_a_kn, m_a_sink, m_b_qa_norm, m_b_kva_norm, m_b_w_uq, m_b_w_ukv, m_b_qn, m_b_kn, m_c_qn, m_c_kn, m_m_g_mem, m_m_w_kv, m_m_qn, m_m_kn, m_w_branch, m_w_out, m_g_mlp, m_w_up, m_w_down, v_g_mix, v_w_in, v_b_gate, v_a_qn, v_a_kn, v_a_sink, v_b_qa_norm, v_b_kva_norm, v_b_w_uq, v_b_w_ukv, v_b_qn, v_b_kn, v_c_qn, v_c_kn, v_m_g_mem, v_m_w_kv, v_m_qn, v_m_kn, v_w_branch, v_w_out, v_g_mlp, v_w_up, v_w_down):
    args = dict(locals())
    w = {k: args[k] for k in WEIGHTS}
    m = {k: args["m_" + k] for k in WEIGHTS}
    v = {k: args["v_" + k] for k in WEIGHTS}
    return _train_step(x, mem, positions, loss_target, w, m, v)
```

```python
import functools
import math

import jax
import jax.numpy as jnp
import numpy as np
from jax import lax
from jax.experimental import pallas as pl
from jax.experimental.pallas import tpu as pltpu

F32 = jnp.float32
BF16 = jnp.bfloat16

DEPTH = 4
BLOCK = 128
ROPE_THETA = 10000.0
EPS = 1e-6
NEG = -1e30
A_HEADS, A_KV_HEADS, A_HD, A_WINDOW = 8, 2, 64, 128
B_HEADS, B_Q_LORA, B_KV_LORA, B_NOPE, B_ROPE, B_V = 8, 384, 256, 64, 32, 64
C_PATTERNS = ((128, 1), (512, 4), (2048, 16))
C_HEADS, C_HD = 8, 64
M_HEADS, M_HD = 4, 128
N_BRANCH, BRANCH_W = 4, 512
ADAM_LR, ADAM_B1, ADAM_B2, ADAM_EPS, ADAM_WD, ADAM_STEP = 0.001, 0.9, 0.999, 1e-08, 0.01, 10

LANE = 128
VMEM_LIMIT = 56 * 1024 * 1024


def _pcall(body, **kw):
    return pl.pallas_call(body, **kw)


def _cparams(sem):
    return pltpu.CompilerParams(dimension_semantics=sem, vmem_limit_bytes=VMEM_LIMIT)


def _tile(n, target):
    if n <= target:
        return n
    best = None
    for t in range(LANE, target + 1, LANE):
        if n % t == 0:
            best = t
    return best if best is not None else n


def _mm(a, b, *, ta=False, tb=False, out_dtype=F32, pro_a=None, epi=None, extra=None, name,
        tm=512, tn=1024, tk=1024):
    if ta:
        K, M = a.shape
    else:
        M, K = a.shape
    if tb:
        N, K2 = b.shape
    else:
        K2, N = b.shape
    assert K == K2, (a.shape, b.shape, ta, tb)
    tm, tn, tk = _tile(M, tm), _tile(N, tn), _tile(K, tk)
    nk = K // tk
    a_spec = pl.BlockSpec((tk, tm), lambda i, j, k: (k, i)) if ta else pl.BlockSpec((tm, tk), lambda i, j, k: (i, k))
    b_spec = pl.BlockSpec((tn, tk), lambda i, j, k: (j, k)) if tb else pl.BlockSpec((tk, tn), lambda i, j, k: (k, j))
    o_spec = pl.BlockSpec((tm, tn), lambda i, j, k: (i, j))
    dims = (((0,) if ta else (1,), (1,) if tb else (0,)), ((), ()))
    has_extra = extra is not None

    def body(*refs):
        if has_extra:
            a_ref, b_ref, e_ref, o_ref, acc_ref = refs
        else:
            a_ref, b_ref, o_ref, acc_ref = refs
            e_ref = None
        k = pl.program_id(2)
        av = a_ref[...]
        if pro_a is not None:
            av = pro_a(av.astype(F32))
        part = lax.dot_general(av.astype(BF16), b_ref[...].astype(BF16), dims, preferred_element_type=F32)

        @pl.when(k == 0)
        def _():
            acc_ref[...] = part

        @pl.when(k > 0)
        def _():
            acc_ref[...] += part

        @pl.when(k == nk - 1)
        def _():
            r = acc_ref[...]
            if epi is not None:
                r = epi(r, e_ref[...]) if has_extra else epi(r)
            o_ref[...] = r.astype(out_dtype)

    in_specs = [a_spec, b_spec] + ([o_spec] if has_extra else [])
    args = (a, b) + ((extra,) if has_extra else ())
    return _pcall(
        body, name=name, grid=(M // tm, N // tn, nk), in_specs=in_specs, out_specs=o_spec,
        out_shape=jax.ShapeDtypeStruct((M, N), out_dtype),
        scratch_shapes=[pltpu.VMEM((tm, tn), F32)],
        compiler_params=_cparams(("parallel", "parallel", "arbitrary")),
    )(*args)


def _piece(p):
    if isinstance(p, tuple):
        return p
    return (p, p.shape[1], 0)


def _row_spec(width, idx, tb):
    return pl.BlockSpec((tb, width), lambda i, idx=idx: (i, idx))


def _full_spec(arr):
    nd = arr.ndim
    return pl.BlockSpec(arr.shape, lambda i, nd=nd: (0,) * nd)


def _rowmap(f, rows, params, outs, *, tb, name):
    rows = [_piece(p) for p in rows]
    R = rows[0][0].shape[0]
    tb = min(tb, R)
    nr, npar, nout = len(rows), len(params), len(outs)

    def body(*refs):
        rv = [r[...] for r in refs[:nr]]
        pv = [r[...] for r in refs[nr:nr + npar]]
        res = f(*rv, *pv)
        for o_ref, val in zip(refs[nr + npar:], res):
            o_ref[...] = val.astype(o_ref.dtype)

    return _pcall(
        body, name=name, grid=(R // tb,),
        in_specs=[_row_spec(w, idx, tb) for (_, w, idx) in rows] + [_full_spec(p) for p in params],
        out_specs=[_row_spec(w, 0, tb) for (w, _) in outs],
        out_shape=[jax.ShapeDtypeStruct((R, w), dt) for (w, dt) in outs],
        compiler_params=_cparams(("parallel",)),
    )(*[r[0] for r in rows], *params)


def _rowmap_bwd(f, rows, params, couts, *, diff, out_dtypes, adds=None, tb, name):
    rows = [_piece(p) for p in rows]
    couts = [_piece(p) for p in couts]
    R = rows[0][0].shape[0]
    tb = min(tb, R)
    nr, npar, nc = len(rows), len(params), len(couts)
    didx = [i for i, d in enumerate(diff) if d]
    adds = [None] * len(didx) if adds is None else adds
    add_ops = [_piece(a) for a in adds if a is not None]
    na = len(add_ops)

    def body(*refs):
        rv = [r[...] for r in refs[:nr]]
        pv = [r[...] for r in refs[nr:nr + npar]]
        cv = [r[...] for r in refs[nr + npar:nr + npar + nc]]
        av = [r[...] for r in refs[nr + npar + nc:nr + npar + nc + na]]
        o_refs = refs[nr + npar + nc + na:]
        drow_refs, dpar_refs = o_refs[:len(didx)], o_refs[len(didx):]
        nondiff = {i: rv[i] for i in range(nr) if not diff[i]}

        def g(*dv):
            full = []
            it = iter(dv[:len(didx)])
            for i in range(nr):
                full.append(nondiff[i] if i in nondiff else next(it))
            return f(*full, *dv[len(didx):])

        res, vjp = jax.vjp(g, *[rv[i].astype(F32) for i in didx], *pv)
        cts = tuple(c.astype(r.dtype) for c, r in zip(cv, res))
        grads = vjp(cts)
        ai = 0
        for n, o_ref in enumerate(drow_refs):
            val = grads[n]
            if adds[n] is not None:
                val = val + av[ai].astype(F32)
                ai += 1
            o_ref[...] = val.astype(o_ref.dtype)
        first = pl.program_id(0) == 0
        for n, o_ref in enumerate(dpar_refs):
            gp = grads[len(didx) + n].astype(F32)

            @pl.when(first)
            def _(o_ref=o_ref, gp=gp):
                o_ref[...] = gp

            @pl.when(jnp.logical_not(first))
            def _(o_ref=o_ref, gp=gp):
                o_ref[...] += gp

    outs = _pcall(
        body, name=name, grid=(R // tb,),
        in_specs=([_row_spec(w, idx, tb) for (_, w, idx) in rows] + [_full_spec(p) for p in params]
                  + [_row_spec(w, idx, tb) for (_, w, idx) in couts] + [_row_spec(w, idx, tb) for (_, w, idx) in add_ops]),
        out_specs=[_row_spec(rows[i][1], 0, tb) for i in didx] + [_full_spec(p) for p in params],
        out_shape=([jax.ShapeDtypeStruct((R, rows[i][1]), dt) for i, dt in zip(didx, out_dtypes)]
                   + [jax.ShapeDtypeStruct(p.shape, F32) for p in params]),
        compiler_params=_cparams(("arbitrary",)),
    )(*[r[0] for r in rows], *params, *[c[0] for c in couts], *[a[0] for a in add_ops])
    return outs[:len(didx)], outs[len(didx):]


@functools.partial(jax.custom_vjp, nondiff_argnums=(1,))
def _lane_roll(x, shift):
    return pltpu.roll(x, shift % x.shape[-1], axis=x.ndim - 1)


def _lane_roll_fwd(x, shift):
    return _lane_roll(x, shift), None


def _lane_roll_bwd(shift, _, g):
    return (_lane_roll(g, -shift),)


_lane_roll.defvjp(_lane_roll_fwd, _lane_roll_bwd)


def _group_matrix(kind):
    r = lax.broadcasted_iota(jnp.int32, (LANE, LANE), 0)
    c = lax.broadcasted_iota(jnp.int32, (LANE, LANE), 1)
    if kind == "mla":
        gid = lambda l: jnp.where(l < B_NOPE, 0, jnp.where(l < B_NOPE + B_ROPE, 1, 2))
        inv = jnp.where(c < B_NOPE, 1.0 / B_NOPE, 1.0 / B_ROPE)
        return jnp.where(gid(r) == gid(c), inv, 0.0).astype(BF16)
    return jnp.where(r // kind == c // kind, 1.0 / kind, 0.0).astype(BF16)


@functools.partial(jax.custom_vjp, nondiff_argnums=(1,))
def _group_mean(xx, kind):
    gm = _group_matrix(kind)
    outs = []
    for b in range(xx.shape[-1] // LANE):
        t = xx[:, b * LANE:(b + 1) * LANE]
        hi = t.astype(BF16)
        lo = (t - hi.astype(F32)).astype(BF16)
        outs.append(jnp.dot(hi, gm, preferred_element_type=F32) + jnp.dot(lo, gm, preferred_element_type=F32))
    return jnp.concatenate(outs, axis=1) if len(outs) > 1 else outs[0]


def _group_mean_fwd(xx, kind):
    return _group_mean(xx, kind), None


def _group_mean_bwd(kind, _, g):
    return (_group_mean(g, kind),)


_group_mean.defvjp(_group_mean_fwd, _group_mean_bwd)


def _exact_dot(x, m):
    return jnp.dot(x, m, precision=lax.Precision.HIGHEST, preferred_element_type=F32)


def _head_norm(x, gain_tiled, group):
    return x * lax.rsqrt(_group_mean(x * x, group) + EPS) * gain_tiled


def _row_norm(x, gain):
    ms = jnp.mean(x * x, axis=-1, keepdims=True)
    return x * lax.rsqrt(ms + EPS) * gain


def _rope(x, cos, sin, hd):
    half = hd // 2
    lane = lax.broadcasted_iota(jnp.int32, x.shape, x.ndim - 1) % hd
    other = jnp.where(lane < half, -_lane_roll(x, -half), _lane_roll(x, half))
    return x * cos + other * sin


class _AttnGeom:
    def __init__(self, mode, lq, lk, max_dist=0):
        self.mode, self.lq, self.lk, self.max_dist = mode, lq, lk, max_dist
        if mode == "band":
            self.bq = self.bk = BLOCK
            self.nt_q = 2
            self.nt_k = 2
        elif mode == "causal":
            self.bq = self.bk = min(256, lq)
            self.nt_q = lk // self.bk
            self.nt_k = lq // self.bq
        else:
            self.bq = min(256, lq)
            self.bk = lk
            self.nt_q = 1
            self.nt_k = lq // self.bq
        self.nq, self.nk = lq // self.bq, lk // self.bk

    def kv_block(self, i, t):
        if self.mode == "band":
            return jnp.maximum(i - t, 0)
        if self.mode == "causal":
            return jnp.minimum(t, i)
        return 0 * i

    def kv_active(self, i, t):
        if self.mode == "band":
            return i - t >= 0
        if self.mode == "causal":
            return t <= i
        return None

    def q_block(self, j, t):
        if self.mode == "band":
            return jnp.minimum(j + t, self.nq - 1)
        if self.mode == "causal":
            return jnp.maximum(t, j)
        return t

    def q_active(self, j, t):
        if self.mode == "band":
            return j + t <= self.nq - 1
        if self.mode == "causal":
            return t >= j
        return None

    def mask(self, qb, kb):
        if self.mode == "full":
            return None
        qp = qb * self.bq + lax.broadcasted_iota(jnp.int32, (self.bq, self.bk), 0)
        kp = kb * self.bk + lax.broadcasted_iota(jnp.int32, (self.bq, self.bk), 1)
        d = qp - kp
        if self.mode == "band":
            return (d >= 0) & (d <= self.max_dist)
        return d >= 0


def _when(cond, fn):
    if cond is None:
        fn()
    else:
        pl.when(cond)(fn)


def _dil_view(p, dil):
    arr, w, idx = _piece(p)
    R, C = arr.shape
    assert C % w == 0, (C, w)
    return arr.reshape(R // dil, dil * C), w, idx, C // w


def _seq_spec(view, rows, blk_fn):
    _, w, idx, cpw = view
    return pl.BlockSpec((rows, w), lambda s, i, t: (blk_fn(i, t), s * cpw + idx))


_NT = (((1,), (1,)), ((), ()))
_TN = (((0,), (0,)), ((), ()))


def _scores(geom, scale, q, k, qp, kp, h, g, hd, rope, qb, kb):
    s = lax.dot_general(q[:, h * hd:(h + 1) * hd], k[:, g * hd:(g + 1) * hd], _NT, preferred_element_type=F32)
    if rope:
        s = s + lax.dot_general(qp[:, h * rope:(h + 1) * rope], kp[:, :rope], _NT, preferred_element_type=F32)
    s = s * scale
    m = geom.mask(qb, kb)
    return s, m


def _attn_fwd(geom, q, k, v, *, hq, hk, hd, hdv, scale, dil=1, qp=None, kp=None, rope=0, name):
    qv, kv, vv = _dil_view(q, dil), _dil_view(k, dil), _dil_view(v, dil)
    R = _piece(q)[0].shape[0]
    grp = hq // hk
    bq, bk, nt = geom.bq, geom.bk, geom.nt_q
    ops = [qv, kv, vv]
    specs = [_seq_spec(qv, bq, lambda i, t: i), _seq_spec(kv, bk, geom.kv_block), _seq_spec(vv, bk, geom.kv_block)]
    if rope:
        qpv, kpv = _dil_view(qp, dil), _dil_view(kp, dil)
        ops += [qpv, kpv]
        specs += [_seq_spec(qpv, bq, lambda i, t: i), _seq_spec(kpv, bk, geom.kv_block)]
    ow = hq * hdv
    o_view = (None, ow, 0, 1)
    o_spec = pl.BlockSpec((bq, ow), lambda s, i, t: (i, s))

    def body(*refs):
        if rope:
            q_ref, k_ref, v_ref, qp_ref, kp_ref, o_ref, lse_ref, m_sc, l_sc, acc_sc = refs
        else:
            q_ref, k_ref, v_ref, o_ref, lse_ref, m_sc, l_sc, acc_sc = refs
            qp_ref = kp_ref = None
        i, t = pl.program_id(1), pl.program_id(2)

        @pl.when(t == 0)
        def _():
            m_sc[...] = jnp.full(m_sc.shape, NEG, F32)
            l_sc[...] = jnp.zeros(l_sc.shape, F32)
            acc_sc[...] = jnp.zeros(acc_sc.shape, F32)

        def step():
            qa, ka, va = q_ref[...].astype(BF16), k_ref[...].astype(BF16), v_ref[...].astype(BF16)
            qpa = qp_ref[...].astype(BF16) if rope else None
            kpa = kp_ref[...].astype(BF16) if rope else None
            kb = geom.kv_block(i, t)
            for h in range(hq):
                g = h // grp
                s, msk = _scores(geom, scale, qa, ka, qpa, kpa, h, g, hd, rope, i, kb)
                if msk is not None:
                    s = jnp.where(msk, s, NEG)
                m_old = m_sc[h]
                m_new = jnp.maximum(m_old, jnp.max(s, axis=1, keepdims=True))
                p = jnp.exp(s - m_new)
                alpha = jnp.exp(m_old - m_new)
                l_sc[h] = alpha * l_sc[h] + jnp.sum(p, axis=1, keepdims=True)
                pv = jnp.dot(p.astype(BF16), va[:, g * hdv:(g + 1) * hdv], preferred_element_type=F32)
                acc_sc[h] = alpha * acc_sc[h] + pv
                m_sc[h] = m_new

        _when(geom.kv_active(i, t), step)

        @pl.when(t == nt - 1)
        def _():
            for h in range(hq):
                l = l_sc[h]
                o_ref[:, h * hdv:(h + 1) * hdv] = acc_sc[h] / l
                lse_ref[:, h * hdv:(h + 1) * hdv] = jnp.broadcast_to(m_sc[h] + jnp.log(l), (bq, hdv))

    o, lse = _pcall(
        body, name=name, grid=(dil, geom.nq, nt), in_specs=specs, out_specs=[o_spec, o_spec],
        out_shape=[jax.ShapeDtypeStruct((R // dil, dil * ow), F32)] * 2,
        scratch_shapes=[pltpu.VMEM((hq, bq, 1), F32), pltpu.VMEM((hq, bq, 1), F32), pltpu.VMEM((hq, bq, hdv), F32)],
        compiler_params=_cparams(("parallel", "parallel", "arbitrary")),
    )(*[o_[0] for o_ in ops])
    return o.reshape(R, ow), lse.reshape(R, ow)


def _attn_dq(geom, q, k, v, do, o, lse, dlse, *, hq, hk, hd, hdv, scale, dil=1, qp=None, kp=None, rope=0,
             out_dtype=F32, name):
    qv, kv, vv = _dil_view(q, dil), _dil_view(k, dil), _dil_view(v, dil)
    dov, ov, lv = _dil_view(do, dil), _dil_view(o, dil), _dil_view(lse, dil)
    R = _piece(q)[0].shape[0]
    grp = hq // hk
    bq, bk, nt = geom.bq, geom.bk, geom.nt_q
    qi = lambda i, t: i
    ops = [qv, kv, vv, dov, ov, lv]
    specs = [_seq_spec(qv, bq, qi), _seq_spec(kv, bk, geom.kv_block), _seq_spec(vv, bk, geom.kv_block),
             _seq_spec(dov, bq, qi), _seq_spec(ov, bq, qi), _seq_spec(lv, bq, qi)]
    has_dl = dlse is not None
    if has_dl:
        dlv = _dil_view(dlse, dil)
        ops.append(dlv)
        specs.append(_seq_spec(dlv, bq, qi))
    if rope:
        qpv, kpv = _dil_view(qp, dil), _dil_view(kp, dil)
        ops += [qpv, kpv]
        specs += [_seq_spec(qpv, bq, qi), _seq_spec(kpv, bk, geom.kv_block)]
    qw = hq * hd
    out_specs = [pl.BlockSpec((bq, qw), lambda s, i, t: (i, s))]
    out_shape = [jax.ShapeDtypeStruct((R // dil, dil * qw), out_dtype)]
    scratch = [pltpu.VMEM((hq, bq, 1), F32), pltpu.VMEM((hq, bq, hd), F32)]
    if rope:
        out_specs.append(pl.BlockSpec((bq, hq * rope), lambda s, i, t: (i, s)))
        out_shape.append(jax.ShapeDtypeStruct((R // dil, dil * hq * rope), out_dtype))
        scratch.append(pltpu.VMEM((hq, bq, rope), F32))

    def body(*refs):
        refs = list(refs)
        q_ref, k_ref, v_ref, do_ref, o_ref, l_ref = refs[:6]
        pos = 6
        dl_ref = None
        if has_dl:
            dl_ref = refs[pos]
            pos += 1
        qp_ref = kp_ref = None
        if rope:
            qp_ref, kp_ref = refs[pos:pos + 2]
            pos += 2
        dq_ref = refs[pos]
        pos += 1
        dqp_ref = None
        if rope:
            dqp_ref = refs[pos]
            pos += 1
        dl_sc, dq_sc = refs[pos:pos + 2]
        dqp_sc = refs[pos + 2] if rope else None
        i, t = pl.program_id(1), pl.program_id(2)

        @pl.when(t == 0)
        def _():
            dov_, ov_ = do_ref[...].astype(F32), o_ref[...].astype(F32)
            prod = dov_ * ov_
            for h in range(hq):
                d = jnp.sum(prod[:, h * hdv:(h + 1) * hdv], axis=1, keepdims=True)
                if has_dl:
                    d = d - jnp.sum(dl_ref[:, h * hdv:(h + 1) * hdv].astype(F32), axis=1, keepdims=True)
                dl_sc[h] = d
            dq_sc[...] = jnp.zeros(dq_sc.shape, F32)
            if rope:
                dqp_sc[...] = jnp.zeros(dqp_sc.shape, F32)

        def step():
            qa, ka, va = q_ref[...].astype(BF16), k_ref[...].astype(BF16), v_ref[...].astype(BF16)
            doa = do_ref[...].astype(BF16)
            qpa = qp_ref[...].astype(BF16) if rope else None
            kpa = kp_ref[...].astype(BF16) if rope else None
            kb = geom.kv_block(i, t)
            for h in range(hq):
                g = h // grp
                s, msk = _scores(geom, scale, qa, ka, qpa, kpa, h, g, hd, rope, i, kb)
                p = jnp.exp(s - l_ref[:, h * hdv:h * hdv + 1])
                if msk is not None:
                    p = jnp.where(msk, p, 0.0)
                dp = lax.dot_general(doa[:, h * hdv:(h + 1) * hdv], va[:, g * hdv:(g + 1) * hdv], _NT,
                                     preferred_element_type=F32)
                ds = (p * (dp - dl_sc[h]) * scale).astype(BF16)
                dq_sc[h] += jnp.dot(ds, ka[:, g * hd:(g + 1) * hd], preferred_element_type=F32)
                if rope:
                    dqp_sc[h] += jnp.dot(ds, kpa[:, :rope], preferred_element_type=F32)

        _when(geom.kv_active(i, t), step)

        @pl.when(t == nt - 1)
        def _():
            for h in range(hq):
                dq_ref[:, h * hd:(h + 1) * hd] = dq_sc[h].astype(dq_ref.dtype)
                if rope:
                    dqp_ref[:, h * rope:(h + 1) * rope] = dqp_sc[h].astype(dqp_ref.dtype)

    outs = _pcall(
        body, name=name, grid=(dil, geom.nq, nt), in_specs=specs, out_specs=out_specs, out_shape=out_shape,
        scratch_shapes=scratch, compiler_params=_cparams(("parallel", "parallel", "arbitrary")),
    )(*[o_[0] for o_ in ops])
    dq = outs[0].reshape(R, qw)
    if rope:
        return dq, outs[1].reshape(R, hq * rope)
    return dq


def _attn_dkv(geom, q, k, v, do, o, lse, dlse, *, hq, hk, hd, hdv, scale, dil=1, qp=None, kp=None, rope=0,
              out_dtype=F32, name):
    qv, kv, vv = _dil_view(q, dil), _dil_view(k, dil), _dil_view(v, dil)
    dov, ov, lv = _dil_view(do, dil), _dil_view(o, dil), _dil_view(lse, dil)
    Rk = _piece(k)[0].shape[0]
    grp = hq // hk
    bq, bk, nt = geom.bq, geom.bk, geom.nt_k
    kj = lambda j, t: j
    ops = [qv, kv, vv, dov, ov, lv]
    specs = [_seq_spec(qv, bq, geom.q_block), _seq_spec(kv, bk, kj), _seq_spec(vv, bk, kj),
             _seq_spec(dov, bq, geom.q_block), _seq_spec(ov, bq, geom.q_block), _seq_spec(lv, bq, geom.q_block)]
    has_dl = dlse is not None
    if has_dl:
        dlv = _dil_view(dlse, dil)
        ops.append(dlv)
        specs.append(_seq_spec(dlv, bq, geom.q_block))
    if rope:
        qpv, kpv = _dil_view(qp, dil), _dil_view(kp, dil)
        ops += [qpv, kpv]
        specs += [_seq_spec(qpv, bq, geom.q_block), _seq_spec(kpv, bk, kj)]
    kw, vw = hk * hd, hk * hdv
    out_specs = [pl.BlockSpec((bk, kw), lambda s, j, t: (j, s)), pl.BlockSpec((bk, vw), lambda s, j, t: (j, s))]
    out_shape = [jax.ShapeDtypeStruct((Rk // dil, dil * kw), out_dtype), jax.ShapeDtypeStruct((Rk // dil, dil * vw), out_dtype)]
    scratch = [pltpu.VMEM((hk, bk, hd), F32), pltpu.VMEM((hk, bk, hdv), F32)]
    if rope:
        out_specs.append(pl.BlockSpec((bk, LANE), lambda s, j, t: (j, s)))
        out_shape.append(jax.ShapeDtypeStruct((Rk // dil, dil * LANE), out_dtype))
        scratch.append(pltpu.VMEM((bk, rope), F32))

    def body(*refs):
        refs = list(refs)
        q_ref, k_ref, v_ref, do_ref, o_ref, l_ref = refs[:6]
        pos = 6
        dl_ref = None
        if has_dl:
            dl_ref = refs[pos]
            pos += 1
        qp_ref = kp_ref = None
        if rope:
            qp_ref, kp_ref = refs[pos:pos + 2]
            pos += 2
        dk_ref, dv_ref = refs[pos:pos + 2]
        pos += 2
        dkp_ref = None
        if rope:
            dkp_ref = refs[pos]
            pos += 1
        dk_sc, dv_sc = refs[pos:pos + 2]
        dkp_sc = refs[pos + 2] if rope else None
        j, t = pl.program_id(1), pl.program_id(2)

        @pl.when(t == 0)
        def _():
            dk_sc[...] = jnp.zeros(dk_sc.shape, F32)
            dv_sc[...] = jnp.zeros(dv_sc.shape, F32)
            if rope:
                dkp_sc[...] = jnp.zeros(dkp_sc.shape, F32)

        def step():
            qa, ka, va = q_ref[...].astype(BF16), k_ref[...].astype(BF16), v_ref[...].astype(BF16)
            dof = do_ref[...].astype(F32)
            doa = dof.astype(BF16)
            prod = dof * o_ref[...].astype(F32)
            qpa = qp_ref[...].astype(BF16) if rope else None
            kpa = kp_ref[...].astype(BF16) if rope else None
            qb = geom.q_block(j, t)
            for h in range(hq):
                g = h // grp
                s, msk = _scores(geom, scale, qa, ka, qpa, kpa, h, g, hd, rope, qb, j)
                p = jnp.exp(s - l_ref[:, h * hdv:h * hdv + 1])
                if msk is not None:
                    p = jnp.where(msk, p, 0.0)
                delta = jnp.sum(prod[:, h * hdv:(h + 1) * hdv], axis=1, keepdims=True)
                if has_dl:
                    delta = delta - jnp.sum(dl_ref[:, h * hdv:(h + 1) * hdv].astype(F32), axis=1, keepdims=True)
                do_h = doa[:, h * hdv:(h + 1) * hdv]
                dv_sc[g] += lax.dot_general(p.astype(BF16), do_h, _TN, preferred_element_type=F32)
                dp = lax.dot_general(do_h, va[:, g * hdv:(g + 1) * hdv], _NT, preferred_element_type=F32)
                ds = (p * (dp - delta) * scale).astype(BF16)
                dk_sc[g] += lax.dot_general(ds, qa[:, h * hd:(h + 1) * hd], _TN, preferred_element_type=F32)
                if rope:
                    dkp_sc[...] += lax.dot_general(ds, qpa[:, h * rope:(h + 1) * rope], _TN, preferred_element_type=F32)

        _when(geom.q_active(j, t), step)

        @pl.when(t == nt - 1)
        def _():
            for g in range(hk):
                dk_ref[:, g * hd:(g + 1) * hd] = dk_sc[g].astype(dk_ref.dtype)
                dv_ref[:, g * hdv:(g + 1) * hdv] = dv_sc[g].astype(dv_ref.dtype)
            if rope:
                dkp_ref[...] = jnp.zeros(dkp_ref.shape, dkp_ref.dtype)
                dkp_ref[:, :rope] = dkp_sc[...].astype(dkp_ref.dtype)

    outs = _pcall(
        body, name=name, grid=(dil, geom.nk, nt), in_specs=specs, out_specs=out_specs, out_shape=out_shape,
        scratch_shapes=scratch, compiler_params=_cparams(("parallel", "parallel", "arbitrary")),
    )(*[o_[0] for o_ in ops])
    dk, dv = outs[0].reshape(Rk, kw), outs[1].reshape(Rk, vw)
    if rope:
        return dk, dv, outs[2].reshape(Rk, LANE)
    return dk, dv


class _BandPlan:
    def __init__(self, S, dil, max_dist):
        self.L, self.dil, self.max_dist = S // dil, dil, max_dist
        self.nblk = self.L // BLOCK
        self.nb = min(4, self.nblk)
        self.ns = min(dil, max(1, 4 // self.nb))
        self.grid = (dil // self.ns, self.nblk // self.nb)
        self.rows = self.nb * BLOCK

    def view(self, p):
        arr, w, idx = _piece(p)
        R, C = arr.shape
        assert C % w == 0 and (self.ns == 1 or (C == w and idx == 0)), (C, w, idx, self.ns)
        return arr.reshape(R // self.dil, self.dil * C), w, idx, C // w

    def main(self, view):
        _, w, idx, cpw = view
        if self.ns == 1:
            return pl.BlockSpec((self.rows, w), lambda s, i: (i, s * cpw + idx))
        return pl.BlockSpec((self.rows, self.ns * w), lambda s, i: (i, s))

    def edge(self, view, nxt):
        _, w, idx, cpw = view
        nb, last = self.nb, self.nblk - 1
        rb = (lambda i: jnp.minimum((i + 1) * nb, last)) if nxt else (lambda i: jnp.maximum(i * nb - 1, 0))
        if self.ns == 1:
            return pl.BlockSpec((BLOCK, w), lambda s, i: (rb(i), s * cpw + idx))
        return pl.BlockSpec((BLOCK, self.ns * w), lambda s, i: (rb(i), s))

    def out(self, w):
        return pl.BlockSpec((self.rows, self.ns * w), lambda s, i: (i, s))

    def masks(self):
        qi = lax.broadcasted_iota(jnp.int32, (BLOCK, BLOCK), 0)
        kj = lax.broadcasted_iota(jnp.int32, (BLOCK, BLOCK), 1)
        return kj <= qi, (qi - kj + BLOCK) <= self.max_dist


def _half(e):
    lane = lax.broadcasted_iota(jnp.int32, (BLOCK, LANE), 1)
    return (lane < LANE // 2) if e == 0 else (lane >= LANE // 2)


def _swap_halves(t):
    return pltpu.roll(t, LANE // 2, axis=1)


def _kv_operand(nat, swapped, g, e):
    grp = g // 2
    return nat[grp] if g % 2 == e else swapped[grp]


def _load_groups(ref, edge_ref, b, col0, ngroups, from_edge):
    nat, swp = [], []
    for kg in range(ngroups):
        cols = slice(col0 + kg * LANE, col0 + (kg + 1) * LANE)
        t = edge_ref[:, cols] if from_edge else ref[b * BLOCK:(b + 1) * BLOCK, cols]
        t = t.astype(BF16)
        nat.append(t)
        swp.append(None)
    return nat, swp


def _band_fwd(q, k, v, *, S, dil, max_dist, hq, hk, scale, name):
    hd = 64
    plan = _BandPlan(S, dil, max_dist)
    qv, kv, vv = plan.view(q), plan.view(k), plan.view(v)
    wq, wk = hq * hd, hk * hd
    grp = hq // hk
    ns, nb = plan.ns, plan.nb
    need_swap = hk != hq

    def body(q_ref, k_ref, kp_ref, v_ref, vp_ref, o_ref, l_ref):
        i = pl.program_id(1)
        m_cur, m_band = plan.masks()
        has_prev = i > 0
        h0, h1 = _half(0), _half(1)
        for sg in range(ns):
            for b in range(nb):
                rows = slice(b * BLOCK, (b + 1) * BLOCK)
                kc, kcs = _load_groups(k_ref, None, b, sg * wk, wk // LANE, False)
                vc, vcs = _load_groups(v_ref, None, b, sg * wk, wk // LANE, False)
                if b > 0:
                    kp, kps = _load_groups(k_ref, None, b - 1, sg * wk, wk // LANE, False)
                    vp, vps = _load_groups(v_ref, None, b - 1, sg * wk, wk // LANE, False)
                    m_prev = m_band
                else:
                    kp, kps = _load_groups(None, kp_ref, 0, sg * wk, wk // LANE, True)
                    vp, vps = _load_groups(None, vp_ref, 0, sg * wk, wk // LANE, True)
                    m_prev = m_band & has_prev
                if need_swap:
                    kcs, vcs, kps, vps = ([_swap_halves(t) for t in ts] for ts in (kc, vc, kp, vp))
                for pr in range(wq // LANE):
                    cols = slice(sg * wq + pr * LANE, sg * wq + (pr + 1) * LANE)
                    q2 = q_ref[rows, cols].astype(BF16)
                    res = []
                    for e, hm in ((0, h0), (1, h1)):
                        g = (2 * pr + e) // grp
                        qm = jnp.where(hm, q2, jnp.zeros_like(q2))
                        s_c = lax.dot_general(qm, _kv_operand(kc, kcs, g, e), _NT, preferred_element_type=F32) * scale
                        s_p = lax.dot_general(qm, _kv_operand(kp, kps, g, e), _NT, preferred_element_type=F32) * scale
                        s_c = jnp.where(m_cur, s_c, NEG)
                        s_p = jnp.where(m_prev, s_p, NEG)
                        mx = jnp.maximum(jnp.max(s_c, axis=1, keepdims=True), jnp.max(s_p, axis=1, keepdims=True))
                        p_c, p_p = jnp.exp(s_c - mx), jnp.exp(s_p - mx)
                        l = jnp.sum(p_c, axis=1, keepdims=True) + jnp.sum(p_p, axis=1, keepdims=True)
                        acc = (jnp.dot(p_c.astype(BF16), _kv_operand(vc, vcs, g, e), preferred_element_type=F32)
                               + jnp.dot(p_p.astype(BF16), _kv_operand(vp, vps, g, e), preferred_element_type=F32))
                        res.append((acc / l, mx + jnp.log(l)))
                    o_ref[rows, cols] = jnp.where(h0, res[0][0], res[1][0])
                    l_ref[rows, cols] = jnp.where(h0, res[0][1], res[1][1])

    o, lse = _pcall(
        body, name=name, grid=plan.grid,
        in_specs=[plan.main(qv), plan.main(kv), plan.edge(kv, False), plan.main(vv), plan.edge(vv, False)],
        out_specs=[plan.out(wq)] * 2, out_shape=[jax.ShapeDtypeStruct((S // dil, dil * wq), F32)] * 2,
        compiler_params=_cparams(("parallel", "parallel")),
    )(qv[0], kv[0], kv[0], vv[0], vv[0])
    return o.reshape(S, wq), lse.reshape(S, wq)


def _band_dq(q, k, v, do, o, lse, dlse, *, S, dil, max_dist, hq, hk, scale, out_dtype=F32, name):
    hd = 64
    plan = _BandPlan(S, dil, max_dist)
    qv, kv, vv = plan.view(q), plan.view(k), plan.view(v)
    dov, ov, lv = plan.view(do), plan.view(o), plan.view(lse)
    has_dl = dlse is not None
    wq, wk = hq * hd, hk * hd
    grp = hq // hk
    ns, nb = plan.ns, plan.nb
    need_swap = hk != hq
    ops = [qv, kv, kv, vv, vv, dov, ov, lv]
    specs = [plan.main(qv), plan.main(kv), plan.edge(kv, False), plan.main(vv), plan.edge(vv, False), plan.main(dov),
             plan.main(ov), plan.main(lv)]
    if has_dl:
        dlv = plan.view(dlse)
        ops.append(dlv)
        specs.append(plan.main(dlv))

    def body(*refs):
        q_ref, k_ref, kp_ref, v_ref, vp_ref, do_ref, o_ref, l_ref = refs[:8]
        dl_ref = refs[8] if has_dl else None
        dq_ref = refs[-1]
        i = pl.program_id(1)
        m_cur, m_band = plan.masks()
        has_prev = i > 0
        h0, h1 = _half(0), _half(1)
        for sg in range(ns):
            for b in range(nb):
                rows = slice(b * BLOCK, (b + 1) * BLOCK)
                kc, kcs = _load_groups(k_ref, None, b, sg * wk, wk // LANE, False)
                vc, vcs = _load_groups(v_ref, None, b, sg * wk, wk // LANE, False)
                if b > 0:
                    kp, kps = _load_groups(k_ref, None, b - 1, sg * wk, wk // LANE, False)
                    vp, vps = _load_groups(v_ref, None, b - 1, sg * wk, wk // LANE, False)
                    m_prev = m_band
                else:
                    kp, kps = _load_groups(None, kp_ref, 0, sg * wk, wk // LANE, True)
                    vp, vps = _load_groups(None, vp_ref, 0, sg * wk, wk // LANE, True)
                    m_prev = m_band & has_prev
                if need_swap:
                    kcs, vcs, kps, vps = ([_swap_halves(t) for t in ts] for ts in (kc, vc, kp, vp))
                for pr in range(wq // LANE):
                    cols = slice(sg * wq + pr * LANE, sg * wq + (pr + 1) * LANE)
                    q2 = q_ref[rows, cols].astype(BF16)
                    do2 = do_ref[rows, cols].astype(F32)
                    prod = do2 * o_ref[rows, cols].astype(F32)
                    if has_dl:
                        prod = prod - dl_ref[rows, cols].astype(F32)
                    do2b = do2.astype(BF16)
                    l2 = l_ref[rows, cols]
                    res = []
                    for e, hm in ((0, h0), (1, h1)):
                        g = (2 * pr + e) // grp
                        qm = jnp.where(hm, q2, jnp.zeros_like(q2))
                        dom = jnp.where(hm, do2b, jnp.zeros_like(do2b))
                        delta = jnp.sum(jnp.where(hm, prod, 0.0), axis=1, keepdims=True)
                        lse_e = l2[:, e * 64:e * 64 + 1]
                        dq_e = None
                        for kop, vop, msk in ((_kv_operand(kc, kcs, g, e), _kv_operand(vc, vcs, g, e), m_cur),
                                              (_kv_operand(kp, kps, g, e), _kv_operand(vp, vps, g, e), m_prev)):
                            s = lax.dot_general(qm, kop, _NT, preferred_element_type=F32) * scale
                            p = jnp.where(msk, jnp.exp(s - lse_e), 0.0)
                            dp = lax.dot_general(dom, vop, _NT, preferred_element_type=F32)
                            ds = (p * (dp - delta) * scale).astype(BF16)
                            t = jnp.dot(ds, kop, preferred_element_type=F32)
                            dq_e = t if dq_e is None else dq_e + t
                        res.append(dq_e)
                    dq_ref[rows, cols] = jnp.where(h0, res[0], res[1]).astype(dq_ref.dtype)

    dq = _pcall(
        body, name=name, grid=plan.grid, in_specs=specs, out_specs=plan.out(wq),
        out_shape=jax.ShapeDtypeStruct((S // dil, dil * wq), out_dtype), compiler_params=_cparams(("parallel", "parallel")),
    )(*[o_[0] for o_ in ops])
    return dq.reshape(S, wq)


def _band_dkv(q, k, v, do, o, lse, dlse, *, S, dil, max_dist, hq, hk, scale, out_dtype=F32, name):
    hd = 64
    plan = _BandPlan(S, dil, max_dist)
    qv, kv, vv = plan.view(q), plan.view(k), plan.view(v)
    dov, ov, lv = plan.view(do), plan.view(o), plan.view(lse)
    has_dl = dlse is not None
    wq, wk = hq * hd, hk * hd
    grp = hq // hk
    ns, nb = plan.ns, plan.nb
    qlike = [qv, dov, ov, lv] + ([plan.view(dlse)] if has_dl else [])
    ops = [kv, vv] + qlike + qlike
    specs = [plan.main(kv), plan.main(vv)] + [plan.main(t) for t in qlike] + [plan.edge(t, True) for t in qlike]
    nql = len(qlike)
    nkg = wk // LANE

    def body(*refs):
        k_ref, v_ref = refs[:2]
        mains, edges = refs[2:2 + nql], refs[2 + nql:2 + 2 * nql]
        dk_ref, dv_ref = refs[2 + 2 * nql:]
        i = pl.program_id(1)
        m_cur, m_band = plan.masks()
        has_next = i < plan.grid[1] - 1
        h0, h1 = _half(0), _half(1)
        for sg in range(ns):
            for b in range(nb):
                rows = slice(b * BLOCK, (b + 1) * BLOCK)
                kc, _ = _load_groups(k_ref, None, b, sg * wk, nkg, False)
                vc, _ = _load_groups(v_ref, None, b, sg * wk, nkg, False)
                if hk != hq:
                    kcs, vcs = [_swap_halves(t) for t in kc], [_swap_halves(t) for t in vc]
                else:
                    kcs = vcs = None
                dk_nat, dk_swp = [None] * nkg, [None] * nkg
                dv_nat, dv_swp = [None] * nkg, [None] * nkg
                for rel in (0, 1):
                    if rel == 0:
                        src, qrows, msk = mains, rows, m_cur
                    elif b + 1 < nb:
                        src, qrows, msk = mains, slice((b + 1) * BLOCK, (b + 2) * BLOCK), m_band
                    else:
                        src, qrows, msk = edges, slice(0, BLOCK), m_band & has_next
                    for pr in range(wq // LANE):
                        cols = slice(sg * wq + pr * LANE, sg * wq + (pr + 1) * LANE)
                        q2 = src[0][qrows, cols].astype(BF16)
                        do2 = src[1][qrows, cols].astype(F32)
                        prod = do2 * src[2][qrows, cols].astype(F32)
                        if has_dl:
                            prod = prod - src[4][qrows, cols].astype(F32)
                        do2b = do2.astype(BF16)
                        l2 = src[3][qrows, cols]
                        for e, hm in ((0, h0), (1, h1)):
                            g = (2 * pr + e) // grp
                            kg = g // 2
                            qm = jnp.where(hm, q2, jnp.zeros_like(q2))
                            dom = jnp.where(hm, do2b, jnp.zeros_like(do2b))
                            delta = jnp.sum(jnp.where(hm, prod, 0.0), axis=1, keepdims=True)
                            kop, vop = _kv_operand(kc, kcs, g, e), _kv_operand(vc, vcs, g, e)
                            s = lax.dot_general(qm, kop, _NT, preferred_element_type=F32) * scale
                            p = jnp.where(msk, jnp.exp(s - l2[:, e * 64:e * 64 + 1]), 0.0)
                            dp = lax.dot_general(dom, vop, _NT, preferred_element_type=F32)
                            ds = (p * (dp - delta) * scale).astype(BF16)
                            tv = lax.dot_general(p.astype(BF16), dom, _TN, preferred_element_type=F32)
                            tk = lax.dot_general(ds, qm, _TN, preferred_element_type=F32)
                            if g % 2 == e:
                                dv_nat[kg] = tv if dv_nat[kg] is None else dv_nat[kg] + tv
                                dk_nat[kg] = tk if dk_nat[kg] is None else dk_nat[kg] + tk
                            else:
                                dv_swp[kg] = tv if dv_swp[kg] is None else dv_swp[kg] + tv
                                dk_swp[kg] = tk if dk_swp[kg] is None else dk_swp[kg] + tk
                for kg in range(nkg):
                    cols = slice(sg * wk + kg * LANE, sg * wk + (kg + 1) * LANE)
                    dkt, dvt = dk_nat[kg], dv_nat[kg]
                    if dk_swp[kg] is not None:
                        dkt = dkt + _swap_halves(dk_swp[kg])
                        dvt = dvt + _swap_halves(dv_swp[kg])
                    dk_ref[rows, cols] = dkt.astype(dk_ref.dtype)
                    dv_ref[rows, cols] = dvt.astype(dv_ref.dtype)

    dk, dv = _pcall(
        body, name=name, grid=plan.grid, in_specs=specs, out_specs=[plan.out(wk)] * 2,
        out_shape=[jax.ShapeDtypeStruct((S // dil, dil * wk), out_dtype)] * 2,
        compiler_params=_cparams(("parallel", "parallel")),
    )(*[o_[0] for o_ in ops])
    return dk.reshape(S, wk), dv.reshape(S, wk)


def _causal_block(S):
    return min(512, S)


def _causal_mask(bq):
    qi = lax.broadcasted_iota(jnp.int32, (bq, bq), 0)
    kj = lax.broadcasted_iota(jnp.int32, (bq, bq), 1)
    return kj <= qi


def _half_of(rows, e):
    lane = lax.broadcasted_iota(jnp.int32, (rows, LANE), 1)
    return (lane < LANE // 2) if e == 0 else (lane >= LANE // 2)


def _causal_fwd(q, k, v, *, heads, scale, name):
    S = q.shape[0]
    bq = _causal_block(S)
    nq = S // bq
    npair = heads // 2
    wv = heads * 64

    def body(q_ref, k_ref, v_ref, o_ref, l_ref, m_sc, l_sc, acc_sc):
        i, t = pl.program_id(0), pl.program_id(1)

        @pl.when(t == 0)
        def _():
            m_sc[...] = jnp.full(m_sc.shape, NEG, F32)
            l_sc[...] = jnp.zeros(l_sc.shape, F32)
            acc_sc[...] = jnp.zeros(acc_sc.shape, F32)

        def step(masked):
            msk = _causal_mask(bq) if masked else None
            h0 = _half_of(bq, 0)
            for pr in range(npair):
                v2 = v_ref[:, pr * LANE:(pr + 1) * LANE].astype(BF16)
                new = []
                for e in range(2):
                    h = 2 * pr + e
                    cols = slice(h * LANE, (h + 1) * LANE)
                    s = lax.dot_general(q_ref[:, cols], k_ref[:, cols], _NT, preferred_element_type=F32) * scale
                    if masked:
                        s = jnp.where(msk, s, NEG)
                    m_old = m_sc[h]
                    m_new = jnp.maximum(m_old, jnp.max(s, axis=1, keepdims=True))
                    p = jnp.exp(s - m_new)
                    alpha = jnp.exp(m_old - m_new)
                    l_sc[h] = alpha * l_sc[h] + jnp.sum(p, axis=1, keepdims=True)
                    m_sc[h] = m_new
                    new.append((alpha, jnp.dot(p.astype(BF16), v2, preferred_element_type=F32)))
                acc = acc_sc[pr]
                acc_sc[pr] = jnp.where(h0, new[0][0] * acc + new[0][1], new[1][0] * acc + new[1][1])

        pl.when(t < i)(lambda: step(False))
        pl.when(t == i)(lambda: step(True))

        @pl.when(t == nq - 1)
        def _():
            h0 = _half_of(bq, 0)
            for pr in range(npair):
                l0, l1 = l_sc[2 * pr], l_sc[2 * pr + 1]
                acc = acc_sc[pr]
                cols = slice(pr * LANE, (pr + 1) * LANE)
                o_ref[:, cols] = jnp.where(h0, acc / l0, acc / l1)
                l_ref[:, cols] = jnp.where(h0, m_sc[2 * pr] + jnp.log(l0), m_sc[2 * pr + 1] + jnp.log(l1))

    qs = pl.BlockSpec((bq, heads * LANE), lambda i, t: (i, 0))
    ks = pl.BlockSpec((bq, heads * LANE), lambda i, t: (jnp.minimum(t, i), 0))
    vs = pl.BlockSpec((bq, wv), lambda i, t: (jnp.minimum(t, i), 0))
    os_ = pl.BlockSpec((bq, wv), lambda i, t: (i, 0))
    return _pcall(
        body, name=name, grid=(nq, nq), in_specs=[qs, ks, vs], out_specs=[os_, os_],
        out_shape=[jax.ShapeDtypeStruct((S, wv), F32)] * 2,
        scratch_shapes=[pltpu.VMEM((heads, bq, 1), F32), pltpu.VMEM((heads, bq, 1), F32), pltpu.VMEM((npair, bq, LANE), F32)],
        compiler_params=_cparams(("parallel", "arbitrary")),
    )(q, k, v)


def _causal_bwd_tile(q_ref, k_ref, v2, do2, prod, l2, h, e, scale, msk, bq):
    cols = slice(h * LANE, (h + 1) * LANE)
    hm = _half_of(bq, e)
    s = lax.dot_general(q_ref[:, cols], k_ref[:, cols], _NT, preferred_element_type=F32) * scale
    p = jnp.exp(s - l2[:, e * 64:e * 64 + 1])
    if msk is not None:
        p = jnp.where(msk, p, 0.0)
    dom = jnp.where(hm, do2, jnp.zeros_like(do2))
    delta = jnp.sum(jnp.where(hm, prod, 0.0), axis=1, keepdims=True)
    dp = lax.dot_general(dom, v2, _NT, preferred_element_type=F32)
    ds = (p * (dp - delta) * scale).astype(BF16)
    return p, ds, dom


def _causal_dq(q, k, v, do, o, lse, *, heads, scale, out_dtype=BF16, name):
    S = q.shape[0]
    bq = _causal_block(S)
    nq = S // bq
    npair = heads // 2
    wv = heads * 64

    def body(q_ref, k_ref, v_ref, do_ref, o_ref, l_ref, dq_ref, dq_sc):
        i, t = pl.program_id(0), pl.program_id(1)

        @pl.when(t == 0)
        def _():
            dq_sc[...] = jnp.zeros(dq_sc.shape, F32)

        def step(masked):
            msk = _causal_mask(bq) if masked else None
            for pr in range(npair):
                pc = slice(pr * LANE, (pr + 1) * LANE)
                v2 = v_ref[:, pc].astype(BF16)
                dof = do_ref[:, pc].astype(F32)
                prod = dof * o_ref[:, pc]
                do2 = dof.astype(BF16)
                l2 = l_ref[:, pc]
                for e in range(2):
                    h = 2 * pr + e
                    _, ds, _ = _causal_bwd_tile(q_ref, k_ref, v2, do2, prod, l2, h, e, scale, msk, bq)
                    dq_sc[h] += jnp.dot(ds, k_ref[:, h * LANE:(h + 1) * LANE], preferred_element_type=F32)

        pl.when(t < i)(lambda: step(False))
        pl.when(t == i)(lambda: step(True))

        @pl.when(t == nq - 1)
        def _():
            for h in range(heads):
                dq_ref[:, h * LANE:(h + 1) * LANE] = dq_sc[h].astype(dq_ref.dtype)

    qs = pl.BlockSpec((bq, heads * LANE), lambda i, t: (i, 0))
    ks = pl.BlockSpec((bq, heads * LANE), lambda i, t: (jnp.minimum(t, i), 0))
    vs = pl.BlockSpec((bq, wv), lambda i, t: (jnp.minimum(t, i), 0))
    os_ = pl.BlockSpec((bq, wv), lambda i, t: (i, 0))
    return _pcall(
        body, name=name, grid=(nq, nq), in_specs=[qs, ks, vs, os_, os_, os_], out_specs=qs,
        out_shape=jax.ShapeDtypeStruct((S, heads * LANE), out_dtype),
        scratch_shapes=[pltpu.VMEM((heads, bq, LANE), F32)],
        compiler_params=_cparams(("parallel", "arbitrary")),
    )(q, k, v, do, o, lse)


def _causal_dkv(q, k, v, do, o, lse, *, heads, scale, out_dtype=BF16, name):
    S = q.shape[0]
    bq = _causal_block(S)
    nq = S // bq
    npair = heads // 2
    wv = heads * 64

    def body(q_ref, k_ref, v_ref, do_ref, o_ref, l_ref, dk_ref, dv_ref, dk_sc, dv_sc):
        j, t = pl.program_id(0), pl.program_id(1)

        @pl.when(t == 0)
        def _():
            dk_sc[...] = jnp.zeros(dk_sc.shape, F32)
            dv_sc[...] = jnp.zeros(dv_sc.shape, F32)

        def step(masked):
            msk = _causal_mask(bq) if masked else None
            for pr in range(npair):
                pc = slice(pr * LANE, (pr + 1) * LANE)
                v2 = v_ref[:, pc].astype(BF16)
                dof = do_ref[:, pc].astype(F32)
                prod = dof * o_ref[:, pc]
                do2 = dof.astype(BF16)
                l2 = l_ref[:, pc]
                dv_add = None
                for e in range(2):
                    h = 2 * pr + e
                    p, ds, dom = _causal_bwd_tile(q_ref, k_ref, v2, do2, prod, l2, h, e, scale, msk, bq)
                    tv = lax.dot_general(p.astype(BF16), dom, _TN, preferred_element_type=F32)
                    dv_add = tv if dv_add is None else dv_add + tv
                    dk_sc[h] += lax.dot_general(ds, q_ref[:, h * LANE:(h + 1) * LANE], _TN, preferred_element_type=F32)
                dv_sc[pr] += dv_add

        pl.when(t > j)(lambda: step(False))
        pl.when(t == j)(lambda: step(True))

        @pl.when(t == nq - 1)
        def _():
            for h in range(heads):
                dk_ref[:, h * LANE:(h + 1) * LANE] = dk_sc[h].astype(dk_ref.dtype)
            for pr in range(npair):
                dv_ref[:, pr * LANE:(pr + 1) * LANE] = dv_sc[pr].astype(dv_ref.dtype)

    qi = lambda j, t: (jnp.maximum(t, j), 0)
    qs = pl.BlockSpec((bq, heads * LANE), qi)
    os_ = pl.BlockSpec((bq, wv), qi)
    ks = pl.BlockSpec((bq, heads * LANE), lambda j, t: (j, 0))
    vs = pl.BlockSpec((bq, wv), lambda j, t: (j, 0))
    return _pcall(
        body, name=name, grid=(nq, nq), in_specs=[qs, ks, vs, os_, os_, os_], out_specs=[ks, vs],
        out_shape=[jax.ShapeDtypeStruct((S, heads * LANE), out_dtype), jax.ShapeDtypeStruct((S, wv), out_dtype)],
        scratch_shapes=[pltpu.VMEM((heads, bq, LANE), F32), pltpu.VMEM((npair, bq, LANE), F32)],
        compiler_params=_cparams(("parallel", "arbitrary")),
    )(q, k, v, do, o, lse)


@jax.custom_vjp
def _bdot(x, w):
    return jnp.dot(x.astype(BF16), w.astype(BF16), preferred_element_type=F32)


def _bdot_fwd(x, w):
    return _bdot(x, w), (x, w)


def _bdot_bwd(res, g):
    x, w = res
    gb = g.astype(BF16)
    dx = lax.dot_general(gb, w.astype(BF16), _NT, preferred_element_type=F32)
    dw = lax.dot_general(x.astype(BF16), gb, _TN, preferred_element_type=F32)
    return dx.astype(x.dtype), dw.astype(w.dtype)


_bdot.defvjp(_bdot_fwd, _bdot_bwd)


def _tile_matrix(hd, width):
    r = lax.broadcasted_iota(jnp.int32, (hd, width), 0)
    c = lax.broadcasted_iota(jnp.int32, (hd, width), 1) % hd
    return jnp.where(r == c, 1.0, 0.0).astype(F32)


def _spread_matrix(heads, width):
    per = width // heads
    r = lax.broadcasted_iota(jnp.int32, (heads, width), 0)
    c = lax.broadcasted_iota(jnp.int32, (heads, width), 1) // per
    return jnp.where(r == c, 1.0, 0.0).astype(F32)


def _wide(t, width):
    n = width // t.shape[-1]
    return jnp.concatenate([t] * n, axis=1) if n > 1 else t


def _norm_heads(x, gain, hd):
    return _head_norm(x, _exact_dot(gain, _tile_matrix(hd, x.shape[-1])), hd)


def _norm_rope(x, gain, cos, sin, hd):
    w = x.shape[-1]
    return _rope(_norm_heads(x, gain, hd), _wide(cos, w), _wide(sin, w), hd)


def _f_norm(x, g):
    return (_row_norm(x, g),)


def _f_prep_acm(aq, ak, c0q, c0k, c0v, c1q, c1k, c1v, c2q, c2k, c2v, mq, cos, sin, g_aq, g_ak, g0q, g0k, g1q, g1k, g2q, g2k, g_mq):
    outs = [_norm_rope(aq, g_aq, cos, sin, A_HD), _norm_rope(ak, g_ak, cos, sin, A_HD)]
    for cq, ck, cv, gq, gk in ((c0q, c0k, c0v, g0q, g0k), (c1q, c1k, c1v, g1q, g1k), (c2q, c2k, c2v, g2q, g2k)):
        outs += [_norm_rope(cq, gq, cos, sin, C_HD), _norm_rope(ck, gk, cos, sin, C_HD), cv]
    outs.append(_norm_heads(mq, g_mq, M_HD))
    return tuple(outs)


def _rope_mla_q(x, cos, sin):
    lane = lax.broadcasted_iota(jnp.int32, x.shape, x.ndim - 1) % LANE
    half = B_ROPE // 2
    first = (lane >= B_NOPE) & (lane < B_NOPE + half)
    other = jnp.where(first, -_lane_roll(x, -half), _lane_roll(x, half))
    return x * cos + other * sin


def _f_prep_b(ckv, cq, kr, cosq, sinq, cosr, sinr, g_qa, g_kva, w_uq, w_ukv, g_q, g_k, g_kr):
    wq = B_HEADS * LANE
    q_up = _bdot(_row_norm(cq, g_qa), w_uq)
    gq = _exact_dot(g_q, _tile_matrix(LANE, wq))
    qf = _rope_mla_q(_head_norm(q_up, gq, "mla"), _wide(cosq, wq), _wide(sinq, wq))
    kv_up = _bdot(_row_norm(ckv, g_kva), w_ukv)
    kn = _head_norm(kv_up[:, :wq], _exact_dot(g_k, _tile_matrix(LANE, wq)), B_NOPE)
    vb = kv_up[:, wq:]
    kp = _rope(_head_norm(kr, g_kr, B_ROPE), cosr, sinr, B_ROPE)
    kp = _lane_roll(kp, B_NOPE)
    return qf, kn + _wide(kp, wq), vb


def _f_mem_k(k, g):
    return (_norm_heads(k, g, M_HD),)


def _f_sink(o, lse, sink):
    sb = _exact_dot(sink, _spread_matrix(A_HEADS, o.shape[-1]))
    m = jnp.maximum(lse, sb)
    tot = m + jnp.log(jnp.exp(lse - m) + jnp.exp(sb - m))
    return (o * jnp.exp(lse - tot),)


def _f_combine(o0, o1, o2, l0, l1, l2):
    m = jnp.maximum(jnp.maximum(l0, l1), l2)
    e0, e1, e2 = jnp.exp(l0 - m), jnp.exp(l1 - m), jnp.exp(l2 - m)
    inv = 1.0 / (e0 + e1 + e2)
    return ((e0 * o0 + e1 * o1 + e2 * o2) * inv,)


def _f_gatemix(gp, y0, y1, y2, y3, bg):
    d = y0.shape[-1]
    gates = 1.0 / (1.0 + jnp.exp(-(gp + bg)))
    mix = gates[:, :d] * y0
    for n, y in enumerate((y1, y2, y3), start=1):
        mix = mix + gates[:, n * d:(n + 1) * d] * y
    return (mix,)


def _relu2(u):
    return jnp.square(jnp.maximum(u, 0.0))


def _add(r, e):
    return r + e.astype(F32)


def _relu2_grad(r, u):
    return r * (2.0 * jnp.maximum(u, 0.0))


def _loss_and_grad(y, target, *, tb=512):
    R, D = y.shape
    tb = min(tb, R)

    def body(y_ref, t_ref, dy_ref, l_ref):
        err = y_ref[...] - t_ref[...]
        dy_ref[...] = err * (1.0 / D)
        part = 0.5 * jnp.sum(jnp.sum(err * err, axis=1, keepdims=True) * (1.0 / D), axis=0, keepdims=True)
        first = pl.program_id(0) == 0

        @pl.when(first)
        def _():
            l_ref[...] = jnp.broadcast_to(part, l_ref.shape)

        @pl.when(jnp.logical_not(first))
        def _():
            l_ref[...] += jnp.broadcast_to(part, l_ref.shape)

    dy, l = _pcall(
        body, name="loss", grid=(R // tb,),
        in_specs=[pl.BlockSpec((tb, D), lambda i: (i, 0))] * 2,
        out_specs=[pl.BlockSpec((tb, D), lambda i: (i, 0)), pl.BlockSpec((8, LANE), lambda i: (0, 0))],
        out_shape=[jax.ShapeDtypeStruct((R, D), F32), jax.ShapeDtypeStruct((8, LANE), F32)],
        compiler_params=_cparams(("arbitrary",)),
    )(y, target)
    return l[0, 0], dy


def _z_layout(d):
    assert d == 1024, "the aligned layout below is laid out for D_MODEL = 1024"
    lay = {"gates": (4 * d, 0)}
    for g in range(3):
        for n, nm in enumerate("qkv"):
            lay[f"c{g}{nm}"] = (512, 8 + 3 * g + n)
    lay.update(aq=(512, 17), mq=(512, 18), ckv=(256, 38), cq=(384, 26), ak=(128, 81), av=(128, 82), kr=(128, 83))
    return lay, 10752


_KW_A = dict(hq=A_HEADS, hk=A_KV_HEADS, scale=A_HD ** -0.5)
_KW_B = dict(heads=B_HEADS, scale=(B_NOPE + B_ROPE) ** -0.5)
_KW_C = dict(hq=C_HEADS, hk=C_HEADS, scale=C_HD ** -0.5)
_KW_M = dict(hq=M_HEADS, hk=M_HEADS, hd=M_HD, hdv=M_HD, scale=M_HD ** -0.5)


def _layer_fwd(l, x, mem, w, tabs):
    S, D = x.shape
    lay, _ = _z_layout(D)
    cosA, sinA, cosB, sinB, cosQ, sinQ = tabs
    gM = _AttnGeom("full", S, mem.shape[0])
    nm = lambda s: f"l{l}_{s}"
    sv = {}
    hn = _rowmap(_f_norm, [x], [w["g_mix"]], [(D, BF16)], tb=512, name=nm("norm1"))[0]
    z = _mm(hn, w["in"], name=nm("in"), tn=768)
    zp = {k: (z, wd, idx) for k, (wd, idx) in lay.items()}
    acm_rows = [zp[k] for k in ("aq", "ak", "c0q", "c0k", "c0v", "c1q", "c1k", "c1v", "c2q", "c2k", "c2v", "mq")] + [cosA, sinA]
    acm_par = [w["a_qn"], w["a_kn"], w["c0q"], w["c0k"], w["c1q"], w["c1k"], w["c2q"], w["c2k"], w["m_qn"]]
    acm = _rowmap(_f_prep_acm, acm_rows, acm_par, [(p[1], BF16) for p in acm_rows[:12]], tb=256, name=nm("prep_acm"))
    qa, ka, qc0, kc0, vc0, qc1, kc1, vc1, qc2, kc2, vc2, mq = acm
    b_rows = [zp["ckv"], zp["cq"], zp["kr"], cosQ, sinQ, cosB, sinB]
    b_par = [w["b_qa"], w["b_kva"], w["uq"], w["ukv"], w["b_q"], w["b_k"], w["b_kr"]]
    qb, kb, vb = _rowmap(_f_prep_b, b_rows, b_par, [(B_HEADS * LANE, BF16), (B_HEADS * LANE, BF16), (512, BF16)], tb=256,
                         name=nm("prep_b"))
    memn = _rowmap(_f_norm, [mem], [w["m_g_mem"]], [(D, BF16)], tb=256, name=nm("mem_norm"))[0]
    mkv = _mm(memn, w["mkv"], name=nm("mem_kv"))
    mk = _rowmap(_f_mem_k, [(mkv, 512, 0)], [w["m_kn"]], [(512, BF16)], tb=256, name=nm("mem_k"))[0]
    mv = (mkv, 512, 1)

    oa_raw, lse_a = _band_fwd(qa, ka, zp["av"], S=S, dil=1, max_dist=A_WINDOW - 1, name=nm("attn_a"), **_KW_A)
    o_a = _rowmap(_f_sink, [oa_raw, lse_a], [w["a_sink"]], [(512, BF16)], tb=512, name=nm("sink"))[0]
    o_b, lse_b = _causal_fwd(qb, kb, vb, name=nm("attn_b"), **_KW_B)
    oc, lc = [], []
    for g, ((win, dil), qc, kc, vc) in enumerate(zip(C_PATTERNS, (qc0, qc1, qc2), (kc0, kc1, kc2), (vc0, vc1, vc2))):
        o_g, l_g = _band_fwd(qc, kc, vc, S=S, dil=dil, max_dist=win // dil, name=nm(f"attn_c{g}"), **_KW_C)
        oc.append(o_g)
        lc.append(l_g)
    o_c = _rowmap(_f_combine, oc + lc, [], [(512, BF16)], tb=512, name=nm("combine"))[0]
    o_m, lse_m = _attn_fwd(gM, mq, mk, mv, name=nm("attn_m"), **_KW_M)

    o_n = [o_a, o_b, o_c, o_m]
    ys = [_mm(o_n[n], w["branch"][n], name=nm(f"branch{n}")) for n in range(N_BRANCH)]
    mix = _rowmap(_f_gatemix, [zp["gates"]] + ys, [w["b_gate"]], [(D, BF16)], tb=256, name=nm("gatemix"))[0]
    x1 = _mm(mix, w["out"], extra=x, epi=_add, name=nm("out"))
    hn2 = _rowmap(_f_norm, [x1], [w["g_mlp"]], [(D, BF16)], tb=512, name=nm("norm2"))[0]
    u = _mm(hn2, w["up"], name=nm("up"))
    x2 = _mm(u, w["down"], pro_a=_relu2, extra=x1, epi=_add, name=nm("down"))
    sv.update(x=x, hn=hn, z=z, acm=acm, bq=(qb, kb, vb), memn=memn, mkv=mkv, mk=mk, oa_raw=oa_raw, lse_a=lse_a,
              o_b=o_b, lse_b=lse_b, oc=oc, lc=lc, o_m=o_m, lse_m=lse_m, o_n=o_n, ys=ys, mix=mix, x1=x1, hn2=hn2, u=u)
    return x2, sv


def _layer_bwd(l, dx2, mem, w, tabs, sv):
    x, z, x1, u = sv["x"], sv["z"], sv["x1"], sv["u"]
    S, D = x.shape
    lay, zw = _z_layout(D)
    cosA, sinA, cosB, sinB, cosQ, sinQ = tabs
    gM = _AttnGeom("full", S, mem.shape[0])
    nm = lambda s: f"l{l}_{s}"
    zp = {k: (z, wd, idx) for k, (wd, idx) in lay.items()}
    g = {}
    du = _mm(dx2, w["down"], tb=True, extra=u, epi=_relu2_grad, out_dtype=BF16, name=nm("d_down_x"))
    g["down"] = _mm(u, dx2, ta=True, pro_a=_relu2, name=nm("d_down_w"))
    dhn2 = _mm(du, w["up"], tb=True, name=nm("d_up_x"))
    g["up"] = _mm(sv["hn2"], du, ta=True, name=nm("d_up_w"))
    (dx1,), (g["g_mlp"],) = _rowmap_bwd(_f_norm, [x1], [w["g_mlp"]], [dhn2], diff=[True], out_dtypes=[F32], adds=[dx2],
                                        tb=256, name=nm("d_norm2"))
    dmix = _mm(dx1, w["out"], tb=True, name=nm("d_out_x"))
    g["out"] = _mm(sv["mix"], dx1, ta=True, name=nm("d_out_w"))
    (dgates, dy0, dy1, dy2, dy3), (g["b_gate"],) = _rowmap_bwd(
        _f_gatemix, [zp["gates"]] + sv["ys"], [w["b_gate"]], [dmix], diff=[True] * 5, out_dtypes=[BF16] * 5,
        tb=128, name=nm("d_gatemix"))
    dys = [dy0, dy1, dy2, dy3]
    do = [_mm(dys[n], w["branch"][n], tb=True, name=nm(f"d_branch{n}_x")) for n in range(N_BRANCH)]
    g["branch"] = [_mm(sv["o_n"][n], dys[n], ta=True, name=nm(f"d_branch{n}_w")) for n in range(N_BRANCH)]
    qa, ka, qc0, kc0, vc0, qc1, kc1, vc1, qc2, kc2, vc2, mq = sv["acm"]
    qb, kb, vb = sv["bq"]
    mkv, mk = sv["mkv"], sv["mk"]
    mv = (mkv, 512, 1)
    dmq = _attn_dq(gM, mq, mk, mv, do[3], sv["o_m"], sv["lse_m"], None, name=nm("attn_m_dq"), **_KW_M)
    dmk, dmv = _attn_dkv(gM, mq, mk, mv, do[3], sv["o_m"], sv["lse_m"], None, name=nm("attn_m_dkv"), **_KW_M)
    (doc0, doc1, doc2, dl0, dl1, dl2), _ = _rowmap_bwd(_f_combine, sv["oc"] + sv["lc"], [], [do[2]], diff=[True] * 6,
                                                      out_dtypes=[F32] * 6, tb=256, name=nm("d_combine"))
    dqc, dkc, dvc = [], [], []
    for gi, ((win, dil), qc, kc, vc, doc, dl) in enumerate(zip(C_PATTERNS, (qc0, qc1, qc2), (kc0, kc1, kc2), (vc0, vc1, vc2),
                                                              (doc0, doc1, doc2), (dl0, dl1, dl2))):
        args = (qc, kc, vc, doc, sv["oc"][gi], sv["lc"][gi], dl)
        kwc = dict(S=S, dil=dil, max_dist=win // dil, out_dtype=BF16, **_KW_C)
        dqc.append(_band_dq(*args, name=nm(f"attn_c{gi}_dq"), **kwc))
        dk_, dv_ = _band_dkv(*args, name=nm(f"attn_c{gi}_dkv"), **kwc)
        dkc.append(dk_)
        dvc.append(dv_)
    argsb = (qb, kb, vb, do[1], sv["o_b"], sv["lse_b"])
    dqb = _causal_dq(*argsb, name=nm("attn_b_dq"), **_KW_B)
    dkb, dvb = _causal_dkv(*argsb, name=nm("attn_b_dkv"), **_KW_B)
    (doa_raw, dlse_a), (g["a_sink"],) = _rowmap_bwd(_f_sink, [sv["oa_raw"], sv["lse_a"]], [w["a_sink"]], [do[0]],
                                                   diff=[True, True], out_dtypes=[F32, F32], tb=256, name=nm("d_sink"))
    argsa = (qa, ka, zp["av"], doa_raw, sv["oa_raw"], sv["lse_a"], dlse_a)
    kwa = dict(S=S, dil=1, max_dist=A_WINDOW - 1, out_dtype=BF16, **_KW_A)
    dqa = _band_dq(*argsa, name=nm("attn_a_dq"), **kwa)
    dka, dva = _band_dkv(*argsa, name=nm("attn_a_dkv"), **kwa)
    acm_rows = [zp[k] for k in ("aq", "ak", "c0q", "c0k", "c0v", "c1q", "c1k", "c1v", "c2q", "c2k", "c2v", "mq")] + [cosA, sinA]
    acm_par = [w["a_qn"], w["a_kn"], w["c0q"], w["c0k"], w["c1q"], w["c1k"], w["c2q"], w["c2k"], w["m_qn"]]
    acm_ct = [dqa, dka, dqc[0], dkc[0], dvc[0], dqc[1], dkc[1], dvc[1], dqc[2], dkc[2], dvc[2], dmq]
    dacm, (g["a_qn"], g["a_kn"], g["c0q"], g["c0k"], g["c1q"], g["c1k"], g["c2q"], g["c2k"], g["m_qn"]) = _rowmap_bwd(
        _f_prep_acm, acm_rows, acm_par, acm_ct, diff=[True] * 12 + [False, False], out_dtypes=[BF16] * 12, tb=128,
        name=nm("d_prep_acm"))
    d_aq, d_ak, d_c0q, d_c0k, d_c0v, d_c1q, d_c1k, d_c1v, d_c2q, d_c2k, d_c2v, d_mq = dacm
    b_rows = [zp["ckv"], zp["cq"], zp["kr"], cosQ, sinQ, cosB, sinB]
    b_par = [w["b_qa"], w["b_kva"], w["uq"], w["ukv"], w["b_q"], w["b_k"], w["b_kr"]]
    (d_ckv, d_cq, d_kr), gb = _rowmap_bwd(_f_prep_b, b_rows, b_par, [dqb, dkb, dvb], diff=[True] * 3 + [False] * 4,
                                          out_dtypes=[BF16] * 3, tb=256, name=nm("d_prep_b"))
    g["b_qa"], g["b_kva"], g["uq"], g["ukv"], g["b_q"], g["b_k"], g["b_kr"] = gb
    (dmkv_k,), (g["m_kn"],) = _rowmap_bwd(_f_mem_k, [(mkv, 512, 0)], [w["m_kn"]], [dmk], diff=[True], out_dtypes=[F32],
                                          tb=256, name=nm("d_mem_k"))
    dmkv = jnp.concatenate([dmkv_k, dmv], axis=1)
    dmemn = _mm(dmkv, w["mkv"], tb=True, name=nm("d_mem_kv_x"))
    g["mkv"] = _mm(sv["memn"], dmkv, ta=True, name=nm("d_mem_kv_w"))
    _, (g["m_g_mem"],) = _rowmap_bwd(_f_norm, [mem], [w["m_g_mem"]], [dmemn], diff=[True], out_dtypes=[F32], tb=256,
                                     name=nm("d_mem_norm"))
    dz = jnp.concatenate([dgates, d_c0q, d_c0k, d_c0v, d_c1q, d_c1k, d_c1v, d_c2q, d_c2k, d_c2v, d_aq, d_mq, d_ckv, d_cq,
                          d_ak, dva, d_kr], axis=1)
    assert dz.shape[1] == zw
    dhn = _mm(dz, w["in"], tb=True, name=nm("d_in_x"), tk=768)
    g["in"] = _mm(sv["hn"], dz, ta=True, name=nm("d_in_w"), tn=768)
    (dx,), (g["g_mix"],) = _rowmap_bwd(_f_norm, [x], [w["g_mix"]], [dhn], diff=[True], out_dtypes=[F32], adds=[dx1],
                                       tb=256, name=nm("d_norm1"))
    return dx, g


_IN_ORIG = dict(aq=(0, 512), ak=(512, 640), av=(640, 768), cq=(768, 1152), ckv=(1152, 1408), kr=(1408, 1440),
                c=(1440, 6048), mq=(6048, 6560), gates=(6560, 10656))
_IN_OURS = dict(gates=(0, 4096), c=(4096, 8704), aq=(8704, 9216), mq=(9216, 9728), ckv=(9728, 9984), cq=(9984, 10368),
                ak=(10368, 10496), av=(10496, 10624), kr=(10624, 10656))
_IN_ORDER_ORIG = ("aq", "ak", "av", "cq", "ckv", "kr", "c", "mq", "gates")


def _in_to_ours(w_in):
    pc = {k: w_in[..., a:b] for k, (a, b) in _IN_ORIG.items()}
    zeros = lambda n: jnp.zeros(w_in.shape[:-1] + (n,), w_in.dtype)
    return jnp.concatenate([pc["gates"], pc["c"], pc["aq"], pc["mq"], pc["ckv"], pc["cq"], pc["ak"], pc["av"], pc["kr"],
                            zeros(96)], axis=-1)


def _in_from_ours(g_in):
    return jnp.concatenate([g_in[..., _IN_OURS[k][0]:_IN_OURS[k][1]] for k in _IN_ORDER_ORIG], axis=-1)


def _uq_to_ours(w):
    per = B_NOPE + B_ROPE
    w4 = w.reshape(w.shape[:-1] + (B_HEADS, per))
    w4 = jnp.pad(w4, [(0, 0)] * (w4.ndim - 1) + [(0, LANE - per)])
    return w4.reshape(w.shape[:-1] + (B_HEADS * LANE,))


def _uq_from_ours(g):
    per = B_NOPE + B_ROPE
    g4 = g.reshape(g.shape[:-1] + (B_HEADS, LANE))[..., :per]
    return g4.reshape(g.shape[:-1] + (B_HEADS * per,))


def _ukv_to_ours(w):
    w4 = w.reshape(w.shape[:-1] + (B_HEADS, B_NOPE + B_V))
    keys = jnp.pad(w4[..., :B_NOPE], [(0, 0)] * (w4.ndim - 1) + [(0, LANE - B_NOPE)])
    vals = w4[..., B_NOPE:]
    return jnp.concatenate([keys.reshape(w.shape[:-1] + (B_HEADS * LANE,)), vals.reshape(w.shape[:-1] + (B_HEADS * B_V,))],
                           axis=-1)


def _ukv_from_ours(g):
    wq = B_HEADS * LANE
    keys = g[..., :wq].reshape(g.shape[:-1] + (B_HEADS, LANE))[..., :B_NOPE]
    vals = g[..., wq:].reshape(g.shape[:-1] + (B_HEADS, B_V))
    return jnp.concatenate([keys, vals], axis=-1).reshape(g.shape[:-1] + (B_HEADS * (B_NOPE + B_V),))


def _layer_weights(big_l, small, l):
    row = lambda a: a[l].reshape(1, -1)
    w = dict(big_l)
    w.update(g_mix=row(small["g_mix"]), b_gate=row(small["b_gate"]), a_qn=row(small["a_qn"]), a_kn=row(small["a_kn"]),
             a_sink=row(small["a_sink"]), b_qa=row(small["b_qa_norm"]), b_kva=row(small["b_kva_norm"]),
             b_q=jnp.pad(small["b_qn"][l], (0, LANE - B_NOPE - B_ROPE)).reshape(1, -1),
             b_k=jnp.pad(small["b_kn"][l, :B_NOPE], (0, LANE - B_NOPE)).reshape(1, -1),
             b_kr=jnp.pad(small["b_kn"][l, B_NOPE:], (0, LANE - B_ROPE)).reshape(1, -1),
             m_g_mem=row(small["m_g_mem"]), m_qn=row(small["m_qn"]), m_kn=row(small["m_kn"]), g_mlp=row(small["g_mlp"]))
    for g in range(3):
        w[f"c{g}q"] = small["c_qn"][l, g].reshape(1, -1)
        w[f"c{g}k"] = small["c_kn"][l, g].reshape(1, -1)
    return w


def _rope_tables(positions):
    pos = positions.astype(F32)[:, None]
    tabs = []
    for dim in (A_HD, B_ROPE):
        inv = ROPE_THETA ** (-jnp.arange(0, dim, 2, dtype=F32) / dim)
        ang = pos * inv
        reps = LANE // (dim // 2)
        tabs += [jnp.tile(jnp.cos(ang), (1, reps)), jnp.tile(jnp.sin(ang), (1, reps))]
    half = B_ROPE // 2
    cb, sb = tabs[2][:, :half], tabs[3][:, :half]
    ones, zeros = jnp.ones((pos.shape[0], B_NOPE), F32), jnp.zeros((pos.shape[0], B_NOPE), F32)
    pad = LANE - B_NOPE - B_ROPE
    tabs.append(jnp.concatenate([ones, cb, cb, ones[:, :pad]], axis=1))
    tabs.append(jnp.concatenate([zeros, sb, sb, zeros[:, :pad]], axis=1))
    return tuple(tabs)


def _local_step(x, mem, positions, small, loss_target, get_big, put_grads):
    depth = small["g_mix"].shape[0]
    tabs = _rope_tables(positions)
    ws, saved = [], []
    h = x
    for l in range(depth):
        ws.append(_layer_weights(get_big(l, h), small, l))
        h, sv = _layer_fwd(l, h, mem, ws[l], tabs)
        saved.append(sv)
    loss, dh = _loss_and_grad(h, loss_target)
    small_grads = [None] * depth
    for l in reversed(range(depth)):
        dh, g = _layer_bwd(l, dh, mem, ws[l], tabs, saved[l])
        zero = put_grads(l, g)
        if l > 0:
            ws[l - 1] = dict(ws[l - 1], g_mlp=ws[l - 1]["g_mlp"] + zero)
        small_grads[l] = g
    return loss, dh, small_grads


def _small_grads_to_reference_layout(grads):
    flat = lambda k: jnp.stack([g[k].reshape(-1) for g in grads])
    return dict(
        g_mix=flat("g_mix"), b_gate=flat("b_gate"), a_qn=flat("a_qn"), a_kn=flat("a_kn"), a_sink=flat("a_sink"),
        b_qa_norm=flat("b_qa"), b_kva_norm=flat("b_kva"), b_qn=flat("b_q")[:, :B_NOPE + B_ROPE],
        b_kn=jnp.concatenate([flat("b_k")[:, :B_NOPE], flat("b_kr")[:, :B_ROPE]], axis=1),
        c_qn=jnp.stack([jnp.stack([g[f"c{i}q"].reshape(-1) for i in range(3)]) for g in grads]),
        c_kn=jnp.stack([jnp.stack([g[f"c{i}k"].reshape(-1) for i in range(3)]) for g in grads]),
        m_g_mem=flat("m_g_mem"), m_qn=flat("m_qn"), m_kn=flat("m_kn"), g_mlp=flat("g_mlp"))


def _big_grads_to_reference_layout(g):
    return dict(w_in=_in_from_ours(g["in"]), b_w_uq=_uq_from_ours(g["uq"]), b_w_ukv=_ukv_from_ours(g["ukv"]), m_w_kv=g["mkv"],
                w_branch=jnp.stack(g["branch"]), w_out=g["out"], w_up=g["up"], w_down=g["down"])


def _big_to_kernel_layout(full):
    c = lambda a: a.astype(BF16)
    return {"in": c(_in_to_ours(full["w_in"])), "uq": c(_uq_to_ours(full["b_w_uq"])),
            "ukv": c(_ukv_to_ours(full["b_w_ukv"])), "mkv": c(full["m_w_kv"]),
            "branch": c(full["w_branch"]), "out": c(full["w_out"]), "up": c(full["w_up"]), "down": c(full["w_down"])}


MESH = pl.DeviceIdType.MESH
N_CHIPS = 4
N_DEV = 8
_ANY = pl.BlockSpec(memory_space=pl.ANY)


_HBM = pl.BlockSpec(memory_space=pltpu.HBM)
_SEM = pl.BlockSpec(memory_space=pltpu.SEMAPHORE)
_EFFECT = pltpu.SideEffectType.DATAFLOW_SIDE_EFFECTING


def _chip_peers():
    x, y, c = lax.axis_index("x"), lax.axis_index("y"), lax.axis_index("c")
    return 2 * x + y, [((1 - x, y, c), 2 * (1 - x) + y), ((x, 1 - y, c), 2 * x + 1 - y), ((1 - x, 1 - y, c), 2 * (1 - x) + 1 - y)]


def _exchange_start(srcs, lands, *, gather, name):
    n = len(srcs)

    def body(*refs):
        ins, land = refs[:n], refs[n:2 * n]
        send_sems, recv_sems = refs[2 * n], refs[2 * n + 1]
        token = refs[-1]
        me, peers = _chip_peers()
        for a in range(n):
            for j, (dev, chip) in enumerate(peers):
                src = ins[a] if gather else ins[a].at[chip]
                pltpu.make_async_remote_copy(src_ref=src, dst_ref=land[a].at[me], send_sem=send_sems.at[3 * a + j],
                                             recv_sem=recv_sems.at[3 * a + j], device_id=dev, device_id_type=MESH).start()
        token[...] = jnp.zeros(token.shape, token.dtype)

    hbm = lambda a: pltpu.HBM(a.shape, a.dtype)
    outs = _pcall(
        body, name=name,
        out_shape=(pltpu.SemaphoreType.DMA((3 * n,)), pltpu.SemaphoreType.DMA((3 * n,)), *[hbm(a) for a in srcs],
                   *[hbm(a) for a in lands], jax.ShapeDtypeStruct((8, LANE), F32)),
        in_specs=[_HBM] * (2 * n), out_specs=(_SEM, _SEM, *([_HBM] * (2 * n)), pl.BlockSpec(memory_space=pltpu.VMEM)),
        input_output_aliases={a: 2 + a for a in range(2 * n)},
        compiler_params=pltpu.CompilerParams(has_side_effects=_EFFECT),
    )(*[pltpu.with_memory_space_constraint(a, pltpu.HBM) for a in srcs],
      *[pltpu.with_memory_space_constraint(a, pltpu.HBM) for a in lands])
    return outs[0], outs[1], list(outs[2:2 + n]), list(outs[2 + n:2 + 2 * n]), outs[-1]


def _exchange_wait(state, after, *, gather, name):
    send_sems, recv_sems, srcs, lands, _ = state
    n = len(lands)

    def body(*refs):
        src_refs, land = refs[:n], refs[n:2 * n]
        ssem, rsem = refs[2 * n], refs[2 * n + 1]
        me, peers = _chip_peers()
        for a in range(n):
            for j, (dev, chip) in enumerate(peers):
                src = src_refs[a] if gather else src_refs[a].at[chip]
                cp = pltpu.make_async_remote_copy(src_ref=src, dst_ref=land[a].at[chip], send_sem=ssem.at[3 * a + j],
                                                  recv_sem=rsem.at[3 * a + j], device_id=dev, device_id_type=MESH)
                cp.wait_send()
                cp.wait_recv()

    outs = _pcall(
        body, name=name,
        out_shape=tuple(pltpu.HBM(a.shape, a.dtype) for a in list(srcs) + list(lands)),
        in_specs=[_HBM] * (2 * n) + [_SEM, _SEM, pl.BlockSpec(memory_space=pl.ANY)],
        out_specs=tuple([_HBM] * (2 * n)), input_output_aliases={a: a for a in range(2 * n)},
        compiler_params=pltpu.CompilerParams(has_side_effects=_EFFECT),
    )(*srcs, *lands, send_sems, recv_sems, after)
    return list(outs[n:])


def _sibling_exchange(arrays, *, name):
    n = len(arrays)

    def body(*refs):
        ins, outs = refs[:n], refs[n:2 * n]
        send_sems, recv_sems = refs[2 * n:]
        x, y, c = lax.axis_index("x"), lax.axis_index("y"), lax.axis_index("c")
        cps = []
        for a in range(n):
            cp = pltpu.make_async_remote_copy(src_ref=ins[a], dst_ref=outs[a], send_sem=send_sems.at[a], recv_sem=recv_sems.at[a],
                                              device_id=(x, y, 1 - c), device_id_type=MESH)
            cp.start()
            cps.append(cp)
        for cp in cps:
            cp.wait()

    return _pcall(
        body, name=name, in_specs=[_ANY] * n, out_specs=[_ANY] * n,
        out_shape=[jax.ShapeDtypeStruct(a.shape, a.dtype) for a in arrays],
        scratch_shapes=[pltpu.SemaphoreType.DMA((n,)), pltpu.SemaphoreType.DMA((n,))],
        compiler_params=pltpu.CompilerParams(has_side_effects=True),
    )(*arrays)


def _allreduce_small(v, *, name):
    rows = v.shape[0]

    def body(v_ref, o_ref, slots, send_sems, recv_sems):
        x, y, c = lax.axis_index("x"), lax.axis_index("y"), lax.axis_index("c")
        me = 4 * x + 2 * y + c
        slots[me] = v_ref[...]
        cps = []
        for k in range(1, N_DEV):
            fx, fy, fc = (k >> 2) & 1, (k >> 1) & 1, k & 1
            peer = (x ^ fx, y ^ fy, c ^ fc)
            cp = pltpu.make_async_remote_copy(src_ref=v_ref, dst_ref=slots.at[me], send_sem=send_sems.at[k - 1],
                                              recv_sem=recv_sems.at[k - 1], device_id=peer, device_id_type=MESH)
            cp.start()
            cps.append((cp, peer))
        for k, (cp, (px, py, pc)) in enumerate(cps):
            pltpu.make_async_remote_copy(src_ref=v_ref, dst_ref=slots.at[4 * px + 2 * py + pc], send_sem=send_sems.at[k],
                                         recv_sem=recv_sems.at[k], device_id=(px, py, pc), device_id_type=MESH).wait_recv()
        for cp, _ in cps:
            cp.wait_send()
        tot = slots[0]
        for d in range(1, N_DEV):
            tot = tot + slots[d]
        o_ref[...] = tot

    vm = pl.BlockSpec(memory_space=pltpu.VMEM)
    return _pcall(
        body, name=name, in_specs=[vm], out_specs=vm, out_shape=jax.ShapeDtypeStruct(v.shape, F32),
        scratch_shapes=[pltpu.VMEM((N_DEV, rows, LANE), F32), pltpu.SemaphoreType.DMA((N_DEV - 1,)),
                        pltpu.SemaphoreType.DMA((N_DEV - 1,))],
        compiler_params=pltpu.CompilerParams(has_side_effects=True),
    )(v)


def _rows_block(rows, cols, itemsize=4, target_bytes=1 << 20):
    want = max(16, target_bytes // max(1, cols * itemsize))
    best = rows
    for t in range(16, rows, 16):
        if rows % t == 0 and t <= want:
            best = t
    return best if best <= want or rows <= want else rows


def _sum_slots(recvs, parts, me, *, name):
    nl = len(recvs)
    shp = recvs[0].shape[1:]
    r3 = [r.reshape(N_CHIPS, -1, shp[-1]) for r in recvs]
    p3 = [q.reshape(N_CHIPS, -1, shp[-1]) for q in parts]
    rows, cols = r3[0].shape[1:]
    tb = _rows_block(rows, cols)
    nblk = rows // tb
    per = N_CHIPS + 1

    def body(me_ref, *refs):
        o_ref = refs[-1]
        lg = pl.program_id(0)
        for l in range(nl):
            r = refs[per * l:per * (l + 1)]

            @pl.when(lg == l)
            def _(r=r):
                tot = ((r[0][...].astype(F32) + r[1][...].astype(F32)) + r[2][...].astype(F32)) + r[3][...].astype(F32)
                o_ref[...] = tot + r[4][...].astype(F32)

    def row(l, lg, i):
        return jnp.where(lg < l, 0, jnp.where(lg > l, nblk - 1, i))

    in_specs, args = [], []
    for l in range(nl):
        for k in range(N_CHIPS):
            in_specs.append(pl.BlockSpec((None, tb, cols), lambda lg, i, me_ref, l=l, k=k: (k, row(l, lg, i), 0)))
            args.append(r3[l])
        in_specs.append(pl.BlockSpec((None, tb, cols), lambda lg, i, me_ref, l=l: (me_ref[0], row(l, lg, i), 0)))
        args.append(p3[l])
    out = _pcall(
        body, name=name,
        grid_spec=pltpu.PrefetchScalarGridSpec(
            num_scalar_prefetch=1, grid=(nl, nblk), in_specs=in_specs,
            out_specs=pl.BlockSpec((None, tb, cols), lambda lg, i, me_ref: (lg, i, 0))),
        out_shape=jax.ShapeDtypeStruct((nl, rows, cols), F32), compiler_params=_cparams(("arbitrary", "arbitrary")),
    )(me, *args)
    return out.reshape((nl,) + shp)


def _adamw(w, g_parts, m, v, *, name):
    shp = w.shape
    two = lambda a: a.reshape(-1, shp[-1])
    rows, cols = two(w).shape
    tb = _rows_block(rows, cols, target_bytes=1 << 19)
    npart = len(g_parts)

    def body(*refs):
        w_ref = refs[0]
        gp = refs[1:1 + npart]
        m_ref, v_ref, g_out, d_out, m_out, v_out = refs[1 + npart:]
        g = gp[0][...]
        for r in gp[1:]:
            g = g + r[...]
        wv = w_ref[...]
        m2 = ADAM_B1 * m_ref[...] + (1.0 - ADAM_B1) * g
        v2 = ADAM_B2 * v_ref[...] + (1.0 - ADAM_B2) * jnp.square(g)
        m_hat = m2 / (1.0 - ADAM_B1 ** ADAM_STEP)
        v_hat = v2 / (1.0 - ADAM_B2 ** ADAM_STEP)
        g_out[...] = g
        d_out[...] = -ADAM_LR * (m_hat / (jnp.sqrt(v_hat) + ADAM_EPS) + ADAM_WD * wv)
        m_out[...] = m2
        v_out[...] = v2

    spec = pl.BlockSpec((tb, cols), lambda i: (i, 0))
    outs = _pcall(
        body, name=name, grid=(rows // tb,), in_specs=[spec] * (3 + npart), out_specs=[spec] * 4,
        out_shape=[jax.ShapeDtypeStruct((rows, cols), F32)] * 4, compiler_params=_cparams(("parallel",)),
    )(two(w), *[two(p) for p in g_parts], two(m), two(v))
    return [o.reshape(shp) for o in outs]


BIG = ("w_in", "b_w_uq", "b_w_ukv", "m_w_kv", "w_branch", "w_out", "w_up", "w_down")
_SHARD_AXIS = dict(w_in=2, b_w_uq=2, b_w_ukv=2, m_w_kv=1, w_branch=3, w_out=1, w_up=2, w_down=1)
SMALL = ("g_mix", "b_gate", "a_qn", "a_kn", "a_sink", "b_qa_norm", "b_kva_norm", "b_qn", "b_kn", "c_qn", "c_kn",
         "m_g_mem", "m_qn", "m_kn", "g_mlp")
WEIGHTS = ("g_mix", "w_in", "b_gate", "a_qn", "a_kn", "a_sink", "b_qa_norm", "b_kva_norm", "b_w_uq", "b_w_ukv", "b_qn", "b_kn",
           "c_qn", "c_kn", "m_g_mem", "m_w_kv", "m_qn", "m_kn", "w_branch", "w_out", "g_mlp", "w_up", "w_down")


def _unshard(gathered, axis):
    moved = jnp.moveaxis(gathered, 0, axis)
    shp = list(gathered.shape[1:])
    shp[axis] *= N_CHIPS
    return moved.reshape(shp)


def _shard_parts(full, axis):
    shp = list(full.shape)
    shp[axis:axis + 1] = [N_CHIPS, shp[axis] // N_CHIPS]
    return jnp.moveaxis(full.reshape(shp), axis, 0)


def _pack_small(d):
    flat = jnp.concatenate([d[k].reshape(-1).astype(F32) for k in SMALL])
    n = flat.shape[0]
    pad = (-n) % (8 * LANE)
    return jnp.pad(flat, (0, pad)).reshape(-1, LANE)


def _unpack_small(packed, like):
    flat = packed.reshape(-1)
    out, off = {}, 0
    for k in SMALL:
        n = int(np.prod(like[k].shape))
        out[k] = flat[off:off + n].reshape(like[k].shape)
        off += n
    return out


def _train_step(x, mem, positions, loss_target, w, m, v):
    depth = w["g_mix"].shape[0]
    me = 2 * lax.axis_index("x") + lax.axis_index("y")
    landing = lambda a: jnp.zeros((N_CHIPS,) + a.shape, a.dtype)
    gathers, own = [], []
    for l in range(depth):
        own.append([w[k][l].astype(BF16) for k in BIG])
        gathers.append(_exchange_start(own[l], [landing(a) for a in own[l]], gather=True, name=f"gather_start{l}"))

    def with_own(land, mine):
        slot = lax.broadcasted_iota(jnp.int32, (N_CHIPS,) + (1,) * mine.ndim, 0)
        return jnp.where(slot == me, mine[None], land)

    def get_big(l, after):
        lands = _exchange_wait(gathers[l], after, gather=True, name=f"gather_wait{l}")
        return _big_to_kernel_layout({k: _unshard(with_own(g, o), _SHARD_AXIS[k] - 1) for k, g, o in zip(BIG, lands, own[l])})

    scatters, parts = [None] * depth, [None] * depth

    def put_grads(l, g):
        gref = _big_grads_to_reference_layout(g)
        parts[l] = [_shard_parts(gref[k], _SHARD_AXIS[k] - 1).astype(BF16) for k in BIG]
        lands = [landing(p[0]) for p in parts[l]]
        scatters[l] = _exchange_start(parts[l], lands, gather=False, name=f"scatter_start{l}")
        return scatters[l][4][:1, :1]

    small = {k: w[k] for k in SMALL}
    loss, gx, grads = _local_step(x[0], mem[0], positions[0], small, loss_target[0], get_big, put_grads)
    loss = lax.psum(loss, ("x", "y", "c"))
    recv = [_exchange_wait(scatters[l], gx, gather=False, name=f"scatter_wait{l}") for l in range(depth)]
    me1 = me.reshape(1).astype(jnp.int32)
    mine = [_sum_slots([recv[l][a] for l in range(depth)], [parts[l][a] for l in range(depth)], me1, name=f"sum_{k}")
            for a, k in enumerate(BIG)]
    theirs = _sibling_exchange(mine, name="sibling_grads")
    res = {}
    for k, p, q in zip(BIG, mine, theirs):
        res[k] = _adamw(w[k], [p, q], m[k], v[k], name=f"adamw_{k}")
    gsmall = _small_grads_to_reference_layout(grads)
    g_small = _allreduce_small(_pack_small(gsmall), name="allreduce_small")
    packed = _adamw(_pack_small(small), [g_small], _pack_small({k: m[k] for k in SMALL}), _pack_small({k: v[k] for k in SMALL}),
                    name="adamw_small")
    unpacked = [_unpack_small(p, small) for p in packed]
    for k in SMALL:
        res[k] = [u[k] for u in unpacked]
    outs = [loss, gx[None]]
    for i in range(4):
        outs += [res[k][i] for k in WEIGHTS]
    return tuple(outs)

def kernel(x, mem, positions, g_mix, w_in, b_gate, a_qn, a_kn, a_sink, b_qa_norm, b_kva_norm, b_w_uq, b_w_ukv, b_qn, b_kn, c_qn, c_kn, m_g_mem, m_w_kv, m_qn, m_kn, w_branch, w_out, g_mlp, w_up, w_down, loss_target, m_g_mix, m_w_in, m_b_gate, m_a_qn, m_a_kn, m_a_sink, m_b_qa_norm, m_b_kva_norm, m_b_w_uq, m_b_w_ukv, m_b_qn, m_b_kn, m_c_qn, m_c_kn, m_m_g_mem, m_m_w_kv, m_m_qn, m_m_kn, m_w_branch, m_w_out, m_g_mlp, m_w_up, m_w_down, v_g_mix, v_w_in, v_b_gate, v_a_qn, v_a_kn, v_a_sink, v_b_qa_norm, v_b_kva_norm, v_b_w_uq, v_b_w_ukv, v_b_qn, v_b_kn, v_c_qn, v_c_kn, v_m_g_mem, v_m_w_kv, v_m_qn, v_m_kn, v_w_branch, v_w_out, v_g_mlp, v_w_up, v_w_down):
    args = dict(locals())
    w = {k: args[k] for k in WEIGHTS}
    m = {k: args["m_" + k] for k in WEIGHTS}
    v = {k: args["v_" + k] for k in WEIGHTS}
    return _train_step(x, mem, positions, loss_target, w, m, v)
```

```python
import functools
import math

import jax
import jax.numpy as jnp
import numpy as np
from jax import lax
from jax.experimental import pallas as pl
from jax.experimental.pallas import tpu as pltpu

F32 = jnp.float32
BF16 = jnp.bfloat16

DEPTH = 4
BLOCK = 128
ROPE_THETA = 10000.0
EPS = 1e-6
NEG = -1e30
A_HEADS, A_KV_HEADS, A_HD, A_WINDOW = 8, 2, 64, 128
B_HEADS, B_Q_LORA, B_KV_LORA, B_NOPE, B_ROPE, B_V = 8, 384, 256, 64, 32, 64
C_PATTERNS = ((128, 1), (512, 4), (2048, 16))
C_HEADS, C_HD = 8, 64
M_HEADS, M_HD = 4, 128
N_BRANCH, BRANCH_W = 4, 512
ADAM_LR, ADAM_B1, ADAM_B2, ADAM_EPS, ADAM_WD, ADAM_STEP = 0.001, 0.9, 0.999, 1e-08, 0.01, 10

LANE = 128
VMEM_LIMIT = 56 * 1024 * 1024


def _pcall(body, **kw):
    return pl.pallas_call(body, **kw)


def _cparams(sem):
    return pltpu.CompilerParams(dimension_semantics=sem, vmem_limit_bytes=VMEM_LIMIT)


def _tile(n, target):
    if n <= target:
        return n
    best = None
    for t in range(LANE, target + 1, LANE):
        if n % t == 0:
            best = t
    return best if best is not None else n


def _mm(a, b, *, ta=False, tb=False, out_dtype=F32, pro_a=None, epi=None, extra=None, name,
        tm=1024, tn=1024, tk=1024):
    if ta:
        K, M = a.shape
    else:
        M, K = a.shape
    if tb:
        N, K2 = b.shape
    else:
        K2, N = b.shape
    assert K == K2, (a.shape, b.shape, ta, tb)
    tm, tn, tk = _tile(M, tm), _tile(N, tn), _tile(K, tk)
    nk = K // tk
    a_spec = pl.BlockSpec((tk, tm), lambda i, j, k: (k, i)) if ta else pl.BlockSpec((tm, tk), lambda i, j, k: (i, k))
    b_spec = pl.BlockSpec((tn, tk), lambda i, j, k: (j, k)) if tb else pl.BlockSpec((tk, tn), lambda i, j, k: (k, j))
    o_spec = pl.BlockSpec((tm, tn), lambda i, j, k: (i, j))
    dims = (((0,) if ta else (1,), (1,) if tb else (0,)), ((), ()))
    has_extra = extra is not None

    def body(*refs):
        if has_extra:
            a_ref, b_ref, e_ref, o_ref, acc_ref = refs
        else:
            a_ref, b_ref, o_ref, acc_ref = refs
            e_ref = None
        k = pl.program_id(2)
        av = a_ref[...]
        if pro_a is not None:
            av = pro_a(av.astype(F32))
        part = lax.dot_general(av.astype(BF16), b_ref[...].astype(BF16), dims, preferred_element_type=F32)

        @pl.when(k == 0)
        def _():
            acc_ref[...] = part

        @pl.when(k > 0)
        def _():
            acc_ref[...] += part

        @pl.when(k == nk - 1)
        def _():
            r = acc_ref[...]
            if epi is not None:
                r = epi(r, e_ref[...]) if has_extra else epi(r)
            o_ref[...] = r.astype(out_dtype)

    in_specs = [a_spec, b_spec] + ([o_spec] if has_extra else [])
    args = (a, b) + ((extra,) if has_extra else ())
    return _pcall(
        body, name=name, grid=(M // tm, N // tn, nk), in_specs=in_specs, out_specs=o_spec,
        out_shape=jax.ShapeDtypeStruct((M, N), out_dtype),
        scratch_shapes=[pltpu.VMEM((tm, tn), F32)],
        compiler_params=_cparams(("parallel", "parallel", "arbitrary")),
    )(*args)


def _piece(p):
    if isinstance(p, tuple):
        return p
    return (p, p.shape[1], 0)


def _row_spec(width, idx, tb):
    return pl.BlockSpec((tb, width), lambda i, idx=idx: (i, idx))


def _full_spec(arr):
    nd = arr.ndim
    return pl.BlockSpec(arr.shape, lambda i, nd=nd: (0,) * nd)


def _rowmap(f, rows, params, outs, *, tb, name):
    rows = [_piece(p) for p in rows]
    R = rows[0][0].shape[0]
    tb = min(tb, R)
    nr, npar, nout = len(rows), len(params), len(outs)

    def body(*refs):
        rv = [r[...] for r in refs[:nr]]
        pv = [r[...] for r in refs[nr:nr + npar]]
        res = f(*rv, *pv)
        for o_ref, val in zip(refs[nr + npar:], res):
            o_ref[...] = val.astype(o_ref.dtype)

    return _pcall(
        body, name=name, grid=(R // tb,),
        in_specs=[_row_spec(w, idx, tb) for (_, w, idx) in rows] + [_full_spec(p) for p in params],
        out_specs=[_row_spec(w, 0, tb) for (w, _) in outs],
        out_shape=[jax.ShapeDtypeStruct((R, w), dt) for (w, dt) in outs],
        compiler_params=_cparams(("parallel",)),
    )(*[r[0] for r in rows], *params)


def _rowmap_bwd(f, rows, params, couts, *, diff, out_dtypes, adds=None, tb, name):
    rows = [_piece(p) for p in rows]
    couts = [_piece(p) for p in couts]
    R = rows[0][0].shape[0]
    tb = min(tb, R)
    nr, npar, nc = len(rows), len(params), len(couts)
    didx = [i for i, d in enumerate(diff) if d]
    adds = [None] * len(didx) if adds is None else adds
    add_ops = [_piece(a) for a in adds if a is not None]
    na = len(add_ops)

    def body(*refs):
        rv = [r[...] for r in refs[:nr]]
        pv = [r[...] for r in refs[nr:nr + npar]]
        cv = [r[...] for r in refs[nr + npar:nr + npar + nc]]
        av = [r[...] for r in refs[nr + npar + nc:nr + npar + nc + na]]
        o_refs = refs[nr + npar + nc + na:]
        drow_refs, dpar_refs = o_refs[:len(didx)], o_refs[len(didx):]
        nondiff = {i: rv[i] for i in range(nr) if not diff[i]}

        def g(*dv):
            full = []
            it = iter(dv[:len(didx)])
            for i in range(nr):
                full.append(nondiff[i] if i in nondiff else next(it))
            return f(*full, *dv[len(didx):])

        res, vjp = jax.vjp(g, *[rv[i].astype(F32) for i in didx], *pv)
        cts = tuple(c.astype(r.dtype) for c, r in zip(cv, res))
        grads = vjp(cts)
        ai = 0
        for n, o_ref in enumerate(drow_refs):
            val = grads[n]
            if adds[n] is not None:
                val = val + av[ai].astype(F32)
                ai += 1
            o_ref[...] = val.astype(o_ref.dtype)
        first = pl.program_id(0) == 0
        for n, o_ref in enumerate(dpar_refs):
            gp = grads[len(didx) + n].astype(F32)

            @pl.when(first)
            def _(o_ref=o_ref, gp=gp):
                o_ref[...] = gp

            @pl.when(jnp.logical_not(first))
            def _(o_ref=o_ref, gp=gp):
                o_ref[...] += gp

    outs = _pcall(
        body, name=name, grid=(R // tb,),
        in_specs=([_row_spec(w, idx, tb) for (_, w, idx) in rows] + [_full_spec(p) for p in params]
                  + [_row_spec(w, idx, tb) for (_, w, idx) in couts] + [_row_spec(w, idx, tb) for (_, w, idx) in add_ops]),
        out_specs=[_row_spec(rows[i][1], 0, tb) for i in didx] + [_full_spec(p) for p in params],
        out_shape=([jax.ShapeDtypeStruct((R, rows[i][1]), dt) for i, dt in zip(didx, out_dtypes)]
                   + [jax.ShapeDtypeStruct(p.shape, F32) for p in params]),
        compiler_params=_cparams(("arbitrary",)),
    )(*[r[0] for r in rows], *params, *[c[0] for c in couts], *[a[0] for a in add_ops])
    return outs[:len(didx)], outs[len(didx):]


@functools.partial(jax.custom_vjp, nondiff_argnums=(1,))
def _lane_roll(x, shift):
    return pltpu.roll(x, shift % x.shape[-1], axis=x.ndim - 1)


def _lane_roll_fwd(x, shift):
    return _lane_roll(x, shift), None


def _lane_roll_bwd(shift, _, g):
    return (_lane_roll(g, -shift),)


_lane_roll.defvjp(_lane_roll_fwd, _lane_roll_bwd)


def _group_matrix(kind):
    r = lax.broadcasted_iota(jnp.int32, (LANE, LANE), 0)
    c = lax.broadcasted_iota(jnp.int32, (LANE, LANE), 1)
    if kind == "mla":
        gid = lambda l: jnp.where(l < B_NOPE, 0, jnp.where(l < B_NOPE + B_ROPE, 1, 2))
        inv = jnp.where(c < B_NOPE, 1.0 / B_NOPE, 1.0 / B_ROPE)
        return jnp.where(gid(r) == gid(c), inv, 0.0).astype(BF16)
    return jnp.where(r // kind == c // kind, 1.0 / kind, 0.0).astype(BF16)


@functools.partial(jax.custom_vjp, nondiff_argnums=(1,))
def _group_mean(xx, kind):
    gm = _group_matrix(kind)
    outs = []
    for b in range(xx.shape[-1] // LANE):
        t = xx[:, b * LANE:(b + 1) * LANE]
        hi = t.astype(BF16)
        lo = (t - hi.astype(F32)).astype(BF16)
        outs.append(jnp.dot(hi, gm, preferred_element_type=F32) + jnp.dot(lo, gm, preferred_element_type=F32))
    return jnp.concatenate(outs, axis=1) if len(outs) > 1 else outs[0]


def _group_mean_fwd(xx, kind):
    return _group_mean(xx, kind), None


def _group_mean_bwd(kind, _, g):
    return (_group_mean(g, kind),)


_group_mean.defvjp(_group_mean_fwd, _group_mean_bwd)


def _exact_dot(x, m):
    return jnp.dot(x, m, precision=lax.Precision.HIGHEST, preferred_element_type=F32)


def _head_norm(x, gain_tiled, group):
    return x * lax.rsqrt(_group_mean(x * x, group) + EPS) * gain_tiled


def _row_norm(x, gain):
    ms = jnp.mean(x * x, axis=-1, keepdims=True)
    return x * lax.rsqrt(ms + EPS) * gain


def _rope(x, cos, sin, hd):
    half = hd // 2
    lane = lax.broadcasted_iota(jnp.int32, x.shape, x.ndim - 1) % hd
    other = jnp.where(lane < half, -_lane_roll(x, -half), _lane_roll(x, half))
    return x * cos + other * sin


class _AttnGeom:
    def __init__(self, mode, lq, lk, max_dist=0):
        self.mode, self.lq, self.lk, self.max_dist = mode, lq, lk, max_dist
        if mode == "band":
            self.bq = self.bk = BLOCK
            self.nt_q = 2
            self.nt_k = 2
        elif mode == "causal":
            self.bq = self.bk = min(256, lq)
            self.nt_q = lk // self.bk
            self.nt_k = lq // self.bq
        else:
            self.bq = min(256, lq)
            self.bk = lk
            self.nt_q = 1
            self.nt_k = lq // self.bq
        self.nq, self.nk = lq // self.bq, lk // self.bk

    def kv_block(self, i, t):
        if self.mode == "band":
            return jnp.maximum(i - t, 0)
        if self.mode == "causal":
            return jnp.minimum(t, i)
        return 0 * i

    def kv_active(self, i, t):
        if self.mode == "band":
            return i - t >= 0
        if self.mode == "causal":
            return t <= i
        return None

    def q_block(self, j, t):
        if self.mode == "band":
            return jnp.minimum(j + t, self.nq - 1)
        if self.mode == "causal":
            return jnp.maximum(t, j)
        return t

    def q_active(self, j, t):
        if self.mode == "band":
            return j + t <= self.nq - 1
        if self.mode == "causal":
            return t >= j
        return None

    def mask(self, qb, kb):
        if self.mode == "full":
            return None
        qp = qb * self.bq + lax.broadcasted_iota(jnp.int32, (self.bq, self.bk), 0)
        kp = kb * self.bk + lax.broadcasted_iota(jnp.int32, (self.bq, self.bk), 1)
        d = qp - kp
        if self.mode == "band":
            return (d >= 0) & (d <= self.max_dist)
        return d >= 0


def _when(cond, fn):
    if cond is None:
        fn()
    else:
        pl.when(cond)(fn)


def _dil_view(p, dil):
    arr, w, idx = _piece(p)
    R, C = arr.shape
    assert C % w == 0, (C, w)
    return arr.reshape(R // dil, dil * C), w, idx, C // w


def _seq_spec(view, rows, blk_fn):
    _, w, idx, cpw = view
    return pl.BlockSpec((rows, w), lambda s, i, t: (blk_fn(i, t), s * cpw + idx))


_NT = (((1,), (1,)), ((), ()))
_TN = (((0,), (0,)), ((), ()))


def _scores(geom, scale, q, k, qp, kp, h, g, hd, rope, qb, kb):
    s = lax.dot_general(q[:, h * hd:(h + 1) * hd], k[:, g * hd:(g + 1) * hd], _NT, preferred_element_type=F32)
    if rope:
        s = s + lax.dot_general(qp[:, h * rope:(h + 1) * rope], kp[:, :rope], _NT, preferred_element_type=F32)
    s = s * scale
    m = geom.mask(qb, kb)
    return s, m


def _attn_fwd(geom, q, k, v, *, hq, hk, hd, hdv, scale, dil=1, qp=None, kp=None, rope=0, name):
    qv, kv, vv = _dil_view(q, dil), _dil_view(k, dil), _dil_view(v, dil)
    R = _piece(q)[0].shape[0]
    grp = hq // hk
    bq, bk, nt = geom.bq, geom.bk, geom.nt_q
    ops = [qv, kv, vv]
    specs = [_seq_spec(qv, bq, lambda i, t: i), _seq_spec(kv, bk, geom.kv_block), _seq_spec(vv, bk, geom.kv_block)]
    if rope:
        qpv, kpv = _dil_view(qp, dil), _dil_view(kp, dil)
        ops += [qpv, kpv]
        specs += [_seq_spec(qpv, bq, lambda i, t: i), _seq_spec(kpv, bk, geom.kv_block)]
    ow = hq * hdv
    o_view = (None, ow, 0, 1)
    o_spec = pl.BlockSpec((bq, ow), lambda s, i, t: (i, s))

    def body(*refs):
        if rope:
            q_ref, k_ref, v_ref, qp_ref, kp_ref, o_ref, lse_ref, m_sc, l_sc, acc_sc = refs
        else:
            q_ref, k_ref, v_ref, o_ref, lse_ref, m_sc, l_sc, acc_sc = refs
            qp_ref = kp_ref = None
        i, t = pl.program_id(1), pl.program_id(2)

        @pl.when(t == 0)
        def _():
            m_sc[...] = jnp.full(m_sc.shape, NEG, F32)
            l_sc[...] = jnp.zeros(l_sc.shape, F32)
            acc_sc[...] = jnp.zeros(acc_sc.shape, F32)

        def step():
            qa, ka, va = q_ref[...].astype(BF16), k_ref[...].astype(BF16), v_ref[...].astype(BF16)
            qpa = qp_ref[...].astype(BF16) if rope else None
            kpa = kp_ref[...].astype(BF16) if rope else None
            kb = geom.kv_block(i, t)
            for h in range(hq):
                g = h // grp
                s, msk = _scores(geom, scale, qa, ka, qpa, kpa, h, g, hd, rope, i, kb)
                if msk is not None:
                    s = jnp.where(msk, s, NEG)
                m_old = m_sc[h]
                m_new = jnp.maximum(m_old, jnp.max(s, axis=1, keepdims=True))
                p = jnp.exp(s - m_new)
                alpha = jnp.exp(m_old - m_new)
                l_sc[h] = alpha * l_sc[h] + jnp.sum(p, axis=1, keepdims=True)
                pv = jnp.dot(p.astype(BF16), va[:, g * hdv:(g + 1) * hdv], preferred_element_type=F32)
                acc_sc[h] = alpha * acc_sc[h] + pv
                m_sc[h] = m_new

        _when(geom.kv_active(i, t), step)

        @pl.when(t == nt - 1)
        def _():
            for h in range(hq):
                l = l_sc[h]
                o_ref[:, h * hdv:(h + 1) * hdv] = acc_sc[h] / l
                lse_ref[:, h * hdv:(h + 1) * hdv] = jnp.broadcast_to(m_sc[h] + jnp.log(l), (bq, hdv))

    o, lse = _pcall(
        body, name=name, grid=(dil, geom.nq, nt), in_specs=specs, out_specs=[o_spec, o_spec],
        out_shape=[jax.ShapeDtypeStruct((R // dil, dil * ow), F32)] * 2,
        scratch_shapes=[pltpu.VMEM((hq, bq, 1), F32), pltpu.VMEM((hq, bq, 1), F32), pltpu.VMEM((hq, bq, hdv), F32)],
        compiler_params=_cparams(("parallel", "parallel", "arbitrary")),
    )(*[o_[0] for o_ in ops])
    return o.reshape(R, ow), lse.reshape(R, ow)


def _attn_dq(geom, q, k, v, do, o, lse, dlse, *, hq, hk, hd, hdv, scale, dil=1, qp=None, kp=None, rope=0,
             out_dtype=F32, name):
    qv, kv, vv = _dil_view(q, dil), _dil_view(k, dil), _dil_view(v, dil)
    dov, ov, lv = _dil_view(do, dil), _dil_view(o, dil), _dil_view(lse, dil)
    R = _piece(q)[0].shape[0]
    grp = hq // hk
    bq, bk, nt = geom.bq, geom.bk, geom.nt_q
    qi = lambda i, t: i
    ops = [qv, kv, vv, dov, ov, lv]
    specs = [_seq_spec(qv, bq, qi), _seq_spec(kv, bk, geom.kv_block), _seq_spec(vv, bk, geom.kv_block),
             _seq_spec(dov, bq, qi), _seq_spec(ov, bq, qi), _seq_spec(lv, bq, qi)]
    has_dl = dlse is not None
    if has_dl:
        dlv = _dil_view(dlse, dil)
        ops.append(dlv)
        specs.append(_seq_spec(dlv, bq, qi))
    if rope:
        qpv, kpv = _dil_view(qp, dil), _dil_view(kp, dil)
        ops += [qpv, kpv]
        specs += [_seq_spec(qpv, bq, qi), _seq_spec(kpv, bk, geom.kv_block)]
    qw = hq * hd
    out_specs = [pl.BlockSpec((bq, qw), lambda s, i, t: (i, s))]
    out_shape = [jax.ShapeDtypeStruct((R // dil, dil * qw), out_dtype)]
    scratch = [pltpu.VMEM((hq, bq, 1), F32), pltpu.VMEM((hq, bq, hd), F32)]
    if rope:
        out_specs.append(pl.BlockSpec((bq, hq * rope), lambda s, i, t: (i, s)))
        out_shape.append(jax.ShapeDtypeStruct((R // dil, dil * hq * rope), out_dtype))
        scratch.append(pltpu.VMEM((hq, bq, rope), F32))

    def body(*refs):
        refs = list(refs)
        q_ref, k_ref, v_ref, do_ref, o_ref, l_ref = refs[:6]
        pos = 6
        dl_ref = None
        if has_dl:
            dl_ref = refs[pos]
            pos += 1
        qp_ref = kp_ref = None
        if rope:
            qp_ref, kp_ref = refs[pos:pos + 2]
            pos += 2
        dq_ref = refs[pos]
        pos += 1
        dqp_ref = None
        if rope:
            dqp_ref = refs[pos]
            pos += 1
        dl_sc, dq_sc = refs[pos:pos + 2]
        dqp_sc = refs[pos + 2] if rope else None
        i, t = pl.program_id(1), pl.program_id(2)

        @pl.when(t == 0)
        def _():
            dov_, ov_ = do_ref[...].astype(F32), o_ref[...].astype(F32)
            prod = dov_ * ov_
            for h in range(hq):
                d = jnp.sum(prod[:, h * hdv:(h + 1) * hdv], axis=1, keepdims=True)
                if has_dl:
                    d = d - jnp.sum(dl_ref[:, h * hdv:(h + 1) * hdv].astype(F32), axis=1, keepdims=True)
                dl_sc[h] = d
            dq_sc[...] = jnp.zeros(dq_sc.shape, F32)
            if rope:
                dqp_sc[...] = jnp.zeros(dqp_sc.shape, F32)

        def step():
            qa, ka, va = q_ref[...].astype(BF16), k_ref[...].astype(BF16), v_ref[...].astype(BF16)
            doa = do_ref[...].astype(BF16)
            qpa = qp_ref[...].astype(BF16) if rope else None
            kpa = kp_ref[...].astype(BF16) if rope else None
            kb = geom.kv_block(i, t)
            for h in range(hq):
                g = h // grp
                s, msk = _scores(geom, scale, qa, ka, qpa, kpa, h, g, hd, rope, i, kb)
                p = jnp.exp(s - l_ref[:, h * hdv:h * hdv + 1])
                if msk is not None:
                    p = jnp.where(msk, p, 0.0)
                dp = lax.dot_general(doa[:, h * hdv:(h + 1) * hdv], va[:, g * hdv:(g + 1) * hdv], _NT,
                                     preferred_element_type=F32)
                ds = (p * (dp - dl_sc[h]) * scale).astype(BF16)
                dq_sc[h] += jnp.dot(ds, ka[:, g * hd:(g + 1) * hd], preferred_element_type=F32)
                if rope:
                    dqp_sc[h] += jnp.dot(ds, kpa[:, :rope], preferred_element_type=F32)

        _when(geom.kv_active(i, t), step)

        @pl.when(t == nt - 1)
        def _():
            for h in range(hq):
                dq_ref[:, h * hd:(h + 1) * hd] = dq_sc[h].astype(dq_ref.dtype)
                if rope:
                    dqp_ref[:, h * rope:(h + 1) * rope] = dqp_sc[h].astype(dqp_ref.dtype)

    outs = _pcall(
        body, name=name, grid=(dil, geom.nq, nt), in_specs=specs, out_specs=out_specs, out_shape=out_shape,
        scratch_shapes=scratch, compiler_params=_cparams(("parallel", "parallel", "arbitrary")),
    )(*[o_[0] for o_ in ops])
    dq = outs[0].reshape(R, qw)
    if rope:
        return dq, outs[1].reshape(R, hq * rope)
    return dq


def _attn_dkv(geom, q, k, v, do, o, lse, dlse, *, hq, hk, hd, hdv, scale, dil=1, qp=None, kp=None, rope=0,
              out_dtype=F32, name):
    qv, kv, vv = _dil_view(q, dil), _dil_view(k, dil), _dil_view(v, dil)
    dov, ov, lv = _dil_view(do, dil), _dil_view(o, dil), _dil_view(lse, dil)
    Rk = _piece(k)[0].shape[0]
    grp = hq // hk
    bq, bk, nt = geom.bq, geom.bk, geom.nt_k
    kj = lambda j, t: j
    ops = [qv, kv, vv, dov, ov, lv]
    specs = [_seq_spec(qv, bq, geom.q_block), _seq_spec(kv, bk, kj), _seq_spec(vv, bk, kj),
             _seq_spec(dov, bq, geom.q_block), _seq_spec(ov, bq, geom.q_block), _seq_spec(lv, bq, geom.q_block)]
    has_dl = dlse is not None
    if has_dl:
        dlv = _dil_view(dlse, dil)
        ops.append(dlv)
        specs.append(_seq_spec(dlv, bq, geom.q_block))
    if rope:
        qpv, kpv = _dil_view(qp, dil), _dil_view(kp, dil)
        ops += [qpv, kpv]
        specs += [_seq_spec(qpv, bq, geom.q_block), _seq_spec(kpv, bk, kj)]
    kw, vw = hk * hd, hk * hdv
    out_specs = [pl.BlockSpec((bk, kw), lambda s, j, t: (j, s)), pl.BlockSpec((bk, vw), lambda s, j, t: (j, s))]
    out_shape = [jax.ShapeDtypeStruct((Rk // dil, dil * kw), out_dtype), jax.ShapeDtypeStruct((Rk // dil, dil * vw), out_dtype)]
    scratch = [pltpu.VMEM((hk, bk, hd), F32), pltpu.VMEM((hk, bk, hdv), F32)]
    if rope:
        out_specs.append(pl.BlockSpec((bk, LANE), lambda s, j, t: (j, s)))
        out_shape.append(jax.ShapeDtypeStruct((Rk // dil, dil * LANE), out_dtype))
        scratch.append(pltpu.VMEM((bk, rope), F32))

    def body(*refs):
        refs = list(refs)
        q_ref, k_ref, v_ref, do_ref, o_ref, l_ref = refs[:6]
        pos = 6
        dl_ref = None
        if has_dl:
            dl_ref = refs[pos]
            pos += 1
        qp_ref = kp_ref = None
        if rope:
            qp_ref, kp_ref = refs[pos:pos + 2]
            pos += 2
        dk_ref, dv_ref = refs[pos:pos + 2]
        pos += 2
        dkp_ref = None
        if rope:
            dkp_ref = refs[pos]
            pos += 1
        dk_sc, dv_sc = refs[pos:pos + 2]
        dkp_sc = refs[pos + 2] if rope else None
        j, t = pl.program_id(1), pl.program_id(2)

        @pl.when(t == 0)
        def _():
            dk_sc[...] = jnp.zeros(dk_sc.shape, F32)
            dv_sc[...] = jnp.zeros(dv_sc.shape, F32)
            if rope:
                dkp_sc[...] = jnp.zeros(dkp_sc.shape, F32)

        def step():
            qa, ka, va = q_ref[...].astype(BF16), k_ref[...].astype(BF16), v_ref[...].astype(BF16)
            dof = do_ref[...].astype(F32)
            doa = dof.astype(BF16)
            prod = dof * o_ref[...].astype(F32)
            qpa = qp_ref[...].astype(BF16) if rope else None
            kpa = kp_ref[...].astype(BF16) if rope else None
            qb = geom.q_block(j, t)
            for h in range(hq):
                g = h // grp
                s, msk = _scores(geom, scale, qa, ka, qpa, kpa, h, g, hd, rope, qb, j)
                p = jnp.exp(s - l_ref[:, h * hdv:h * hdv + 1])
                if msk is not None:
                    p = jnp.where(msk, p, 0.0)
                delta = jnp.sum(prod[:, h * hdv:(h + 1) * hdv], axis=1, keepdims=True)
                if has_dl:
                    delta = delta - jnp.sum(dl_ref[:, h * hdv:(h + 1) * hdv].astype(F32), axis=1, keepdims=True)
                do_h = doa[:, h * hdv:(h + 1) * hdv]
                dv_sc[g] += lax.dot_general(p.astype(BF16), do_h, _TN, preferred_element_type=F32)
                dp = lax.dot_general(do_h, va[:, g * hdv:(g + 1) * hdv], _NT, preferred_element_type=F32)
                ds = (p * (dp - delta) * scale).astype(BF16)
                dk_sc[g] += lax.dot_general(ds, qa[:, h * hd:(h + 1) * hd], _TN, preferred_element_type=F32)
                if rope:
                    dkp_sc[...] += lax.dot_general(ds, qpa[:, h * rope:(h + 1) * rope], _TN, preferred_element_type=F32)

        _when(geom.q_active(j, t), step)

        @pl.when(t == nt - 1)
        def _():
            for g in range(hk):
                dk_ref[:, g * hd:(g + 1) * hd] = dk_sc[g].astype(dk_ref.dtype)
                dv_ref[:, g * hdv:(g + 1) * hdv] = dv_sc[g].astype(dv_ref.dtype)
            if rope:
                dkp_ref[...] = jnp.zeros(dkp_ref.shape, dkp_ref.dtype)
                dkp_ref[:, :rope] = dkp_sc[...].astype(dkp_ref.dtype)

    outs = _pcall(
        body, name=name, grid=(dil, geom.nk, nt), in_specs=specs, out_specs=out_specs, out_shape=out_shape,
        scratch_shapes=scratch, compiler_params=_cparams(("parallel", "parallel", "arbitrary")),
    )(*[o_[0] for o_ in ops])
    dk, dv = outs[0].reshape(Rk, kw), outs[1].reshape(Rk, vw)
    if rope:
        return dk, dv, outs[2].reshape(Rk, LANE)
    return dk, dv


class _BandPlan:
    def __init__(self, S, dil, max_dist):
        self.L, self.dil, self.max_dist = S // dil, dil, max_dist
        self.nblk = self.L // BLOCK
        self.nb = min(4, self.nblk)
        self.ns = min(dil, max(1, 4 // self.nb))
        self.grid = (dil // self.ns, self.nblk // self.nb)
        self.rows = self.nb * BLOCK

    def view(self, p):
        arr, w, idx = _piece(p)
        R, C = arr.shape
        assert C % w == 0 and (self.ns == 1 or (C == w and idx == 0)), (C, w, idx, self.ns)
        return arr.reshape(R // self.dil, self.dil * C), w, idx, C // w

    def main(self, view):
        _, w, idx, cpw = view
        if self.ns == 1:
            return pl.BlockSpec((self.rows, w), lambda s, i: (i, s * cpw + idx))
        return pl.BlockSpec((self.rows, self.ns * w), lambda s, i: (i, s))

    def edge(self, view, nxt):
        _, w, idx, cpw = view
        nb, last = self.nb, self.nblk - 1
        rb = (lambda i: jnp.minimum((i + 1) * nb, last)) if nxt else (lambda i: jnp.maximum(i * nb - 1, 0))
        if self.ns == 1:
            return pl.BlockSpec((BLOCK, w), lambda s, i: (rb(i), s * cpw + idx))
        return pl.BlockSpec((BLOCK, self.ns * w), lambda s, i: (rb(i), s))

    def out(self, w):
        return pl.BlockSpec((self.rows, self.ns * w), lambda s, i: (i, s))

    def masks(self):
        qi = lax.broadcasted_iota(jnp.int32, (BLOCK, BLOCK), 0)
        kj = lax.broadcasted_iota(jnp.int32, (BLOCK, BLOCK), 1)
        return kj <= qi, (qi - kj + BLOCK) <= self.max_dist


def _half(e, rows=BLOCK):
    lane = lax.broadcasted_iota(jnp.int32, (rows, LANE), 1)
    return (lane < LANE // 2) if e == 0 else (lane >= LANE // 2)


def _swap_halves(t):
    return pltpu.roll(t, LANE // 2, axis=1)


def _kv_group(ref, rows, col0, pr, grp, hq, hk):
    if hk == hq:
        return ref[rows, col0 + pr * LANE:col0 + (pr + 1) * LANE].astype(BF16)
    g = (2 * pr) // grp
    t = ref[rows, col0 + (g // 2) * LANE:col0 + (g // 2 + 1) * LANE].astype(BF16)
    sw = _swap_halves(t)
    h0 = _half(0, t.shape[0])
    return jnp.where(h0, t, sw) if g % 2 == 0 else jnp.where(h0, sw, t)


def _stack_heads(t2):
    z = jnp.zeros_like(t2)
    h0 = _half(0, t2.shape[0])
    return jnp.concatenate([jnp.where(h0, t2, z), jnp.where(h0, z, t2)], axis=0)


def _unstack_heads(t, rows=BLOCK):
    return jnp.where(_half(0, rows), t[:rows], t[rows:])


def _per_head_col(t2):
    return jnp.concatenate([t2[:, :1], t2[:, LANE // 2:LANE // 2 + 1]], axis=0)


def _per_head_sum(t2):
    h0 = _half(0, t2.shape[0])
    return jnp.concatenate([jnp.sum(jnp.where(h0, t2, 0.0), axis=1, keepdims=True),
                            jnp.sum(jnp.where(h0, 0.0, t2), axis=1, keepdims=True)], axis=0)


def _band_fwd(q, k, v, *, S, dil, max_dist, hq, hk, scale, name):
    hd = 64
    plan = _BandPlan(S, dil, max_dist)
    qv, kv, vv = plan.view(q), plan.view(k), plan.view(v)
    wq, wk = hq * hd, hk * hd
    grp = hq // hk
    ns, nb = plan.ns, plan.nb

    def body(q_ref, k_ref, kp_ref, v_ref, vp_ref, o_ref, l_ref):
        i = pl.program_id(1)
        m_cur, m_band = plan.masks()
        has_prev = i > 0
        for sg in range(ns):
            for b in range(nb):
                rows = slice(b * BLOCK, (b + 1) * BLOCK)
                prows = slice((b - 1) * BLOCK, b * BLOCK) if b > 0 else slice(0, BLOCK)
                m_prev = m_band if b > 0 else (m_band & has_prev)
                msk = jnp.concatenate([m_prev, m_cur], axis=1)
                msk = jnp.concatenate([msk, msk], axis=0)
                for pr in range(wq // LANE):
                    cols = slice(sg * wq + pr * LANE, sg * wq + (pr + 1) * LANE)
                    kcat = jnp.concatenate([_kv_group(k_ref if b > 0 else kp_ref, prows, sg * wk, pr, grp, hq, hk),
                                            _kv_group(k_ref, rows, sg * wk, pr, grp, hq, hk)], axis=0)
                    vcat = jnp.concatenate([_kv_group(v_ref if b > 0 else vp_ref, prows, sg * wk, pr, grp, hq, hk),
                                            _kv_group(v_ref, rows, sg * wk, pr, grp, hq, hk)], axis=0)
                    qs = _stack_heads(q_ref[rows, cols].astype(BF16))
                    s = lax.dot_general(qs, kcat, _NT, preferred_element_type=F32) * scale
                    s = jnp.where(msk, s, NEG)
                    mx = jnp.max(s, axis=1, keepdims=True)
                    p = jnp.exp(s - mx)
                    l = jnp.sum(p, axis=1, keepdims=True)
                    acc = jnp.dot(p.astype(BF16), vcat, preferred_element_type=F32)
                    o_ref[rows, cols] = _unstack_heads(acc / l)
                    l_ref[rows, cols] = _unstack_heads(jnp.broadcast_to(mx + jnp.log(l), (2 * BLOCK, LANE)))

    o, lse = _pcall(
        body, name=name, grid=plan.grid,
        in_specs=[plan.main(qv), plan.main(kv), plan.edge(kv, False), plan.main(vv), plan.edge(vv, False)],
        out_specs=[plan.out(wq)] * 2, out_shape=[jax.ShapeDtypeStruct((S // dil, dil * wq), F32)] * 2,
        compiler_params=_cparams(("parallel", "parallel")),
    )(qv[0], kv[0], kv[0], vv[0], vv[0])
    return o.reshape(S, wq), lse.reshape(S, wq)


def _band_dq(q, k, v, do, o, lse, dlse, *, S, dil, max_dist, hq, hk, scale, out_dtype=F32, name):
    hd = 64
    plan = _BandPlan(S, dil, max_dist)
    qv, kv, vv = plan.view(q), plan.view(k), plan.view(v)
    dov, ov, lv = plan.view(do), plan.view(o), plan.view(lse)
    has_dl = dlse is not None
    wq, wk = hq * hd, hk * hd
    grp = hq // hk
    ns, nb = plan.ns, plan.nb
    ops = [qv, kv, kv, vv, vv, dov, ov, lv]
    specs = [plan.main(qv), plan.main(kv), plan.edge(kv, False), plan.main(vv), plan.edge(vv, False), plan.main(dov),
             plan.main(ov), plan.main(lv)]
    if has_dl:
        dlv = plan.view(dlse)
        ops.append(dlv)
        specs.append(plan.main(dlv))

    def body(*refs):
        q_ref, k_ref, kp_ref, v_ref, vp_ref, do_ref, o_ref, l_ref = refs[:8]
        dl_ref = refs[8] if has_dl else None
        dq_ref = refs[-1]
        i = pl.program_id(1)
        m_cur, m_band = plan.masks()
        has_prev = i > 0
        for sg in range(ns):
            for b in range(nb):
                rows = slice(b * BLOCK, (b + 1) * BLOCK)
                prows = slice((b - 1) * BLOCK, b * BLOCK) if b > 0 else slice(0, BLOCK)
                m_prev = m_band if b > 0 else (m_band & has_prev)
                msk = jnp.concatenate([m_prev, m_cur], axis=1)
                msk = jnp.concatenate([msk, msk], axis=0)
                for pr in range(wq // LANE):
                    cols = slice(sg * wq + pr * LANE, sg * wq + (pr + 1) * LANE)
                    kcat = jnp.concatenate([_kv_group(k_ref if b > 0 else kp_ref, prows, sg * wk, pr, grp, hq, hk),
                                            _kv_group(k_ref, rows, sg * wk, pr, grp, hq, hk)], axis=0)
                    vcat = jnp.concatenate([_kv_group(v_ref if b > 0 else vp_ref, prows, sg * wk, pr, grp, hq, hk),
                                            _kv_group(v_ref, rows, sg * wk, pr, grp, hq, hk)], axis=0)
                    qs = _stack_heads(q_ref[rows, cols].astype(BF16))
                    do2 = do_ref[rows, cols].astype(F32)
                    prod = do2 * o_ref[rows, cols].astype(F32)
                    if has_dl:
                        prod = prod - dl_ref[rows, cols].astype(F32)
                    delta = _per_head_sum(prod)
                    lse_rows = _per_head_col(l_ref[rows, cols])
                    dos = _stack_heads(do2.astype(BF16))
                    s = lax.dot_general(qs, kcat, _NT, preferred_element_type=F32) * scale
                    pm = jnp.where(msk, jnp.exp(s - lse_rows), 0.0)
                    dp = lax.dot_general(dos, vcat, _NT, preferred_element_type=F32)
                    ds = (pm * (dp - delta) * scale).astype(BF16)
                    dq_ref[rows, cols] = _unstack_heads(jnp.dot(ds, kcat, preferred_element_type=F32)).astype(dq_ref.dtype)

    dq = _pcall(
        body, name=name, grid=plan.grid, in_specs=specs, out_specs=plan.out(wq),
        out_shape=jax.ShapeDtypeStruct((S // dil, dil * wq), out_dtype), compiler_params=_cparams(("parallel", "parallel")),
    )(*[o_[0] for o_ in ops])
    return dq.reshape(S, wq)


def _band_dkv(q, k, v, do, o, lse, dlse, *, S, dil, max_dist, hq, hk, scale, out_dtype=F32, name):
    hd = 64
    plan = _BandPlan(S, dil, max_dist)
    qv, kv, vv = plan.view(q), plan.view(k), plan.view(v)
    dov, ov, lv = plan.view(do), plan.view(o), plan.view(lse)
    has_dl = dlse is not None
    wq, wk = hq * hd, hk * hd
    grp = hq // hk
    ns, nb = plan.ns, plan.nb
    qlike = [qv, dov, ov, lv] + ([plan.view(dlse)] if has_dl else [])
    ops = [kv, vv] + qlike + qlike
    specs = [plan.main(kv), plan.main(vv)] + [plan.main(t) for t in qlike] + [plan.edge(t, True) for t in qlike]
    nql = len(qlike)
    nkg = wk // LANE

    def body(*refs):
        k_ref, v_ref = refs[:2]
        mains, edges = refs[2:2 + nql], refs[2 + nql:2 + 2 * nql]
        dk_ref, dv_ref = refs[2 + 2 * nql:]
        i = pl.program_id(1)
        m_cur, m_band = plan.masks()
        has_next = i < plan.grid[1] - 1
        for sg in range(ns):
            for b in range(nb):
                rows = slice(b * BLOCK, (b + 1) * BLOCK)
                nxt_in_main = b + 1 < nb
                nrows = slice((b + 1) * BLOCK, (b + 2) * BLOCK) if nxt_in_main else slice(0, BLOCK)
                nsrc = mains if nxt_in_main else edges
                m_next = m_band if nxt_in_main else (m_band & has_next)
                msk = jnp.concatenate([m_cur, m_cur, m_next, m_next], axis=0)
                nacc = nkg if hk == hq else hk
                dk_acc, dv_acc = [None] * nacc, [None] * nacc
                for pr in range(wq // LANE):
                    cols = slice(sg * wq + pr * LANE, sg * wq + (pr + 1) * LANE)
                    kop = _kv_group(k_ref, rows, sg * wk, pr, grp, hq, hk)
                    vop = _kv_group(v_ref, rows, sg * wk, pr, grp, hq, hk)
                    qs, dos, deltas, lses = [], [], [], []
                    for src, r in ((mains, rows), (nsrc, nrows)):
                        qs.append(_stack_heads(src[0][r, cols].astype(BF16)))
                        do2 = src[1][r, cols].astype(F32)
                        prod = do2 * src[2][r, cols].astype(F32)
                        if has_dl:
                            prod = prod - src[4][r, cols].astype(F32)
                        deltas.append(_per_head_sum(prod))
                        lses.append(_per_head_col(src[3][r, cols]))
                        dos.append(_stack_heads(do2.astype(BF16)))
                    qs4, dos4 = jnp.concatenate(qs, axis=0), jnp.concatenate(dos, axis=0)
                    delta4, lse4 = jnp.concatenate(deltas, axis=0), jnp.concatenate(lses, axis=0)
                    s = lax.dot_general(qs4, kop, _NT, preferred_element_type=F32) * scale
                    pm = jnp.where(msk, jnp.exp(s - lse4), 0.0)
                    dp = lax.dot_general(dos4, vop, _NT, preferred_element_type=F32)
                    ds = (pm * (dp - delta4) * scale).astype(BF16)
                    tv = lax.dot_general(pm.astype(BF16), dos4, _TN, preferred_element_type=F32)
                    tk = lax.dot_general(ds, qs4, _TN, preferred_element_type=F32)
                    ai = pr if hk == hq else (2 * pr) // grp
                    dv_acc[ai] = tv if dv_acc[ai] is None else dv_acc[ai] + tv
                    dk_acc[ai] = tk if dk_acc[ai] is None else dk_acc[ai] + tk
                for kg in range(nkg):
                    cols = slice(sg * wk + kg * LANE, sg * wk + (kg + 1) * LANE)
                    if hk == hq:
                        dkt, dvt = dk_acc[kg], dv_acc[kg]
                    else:
                        both = lambda t: t + _swap_halves(t)
                        h0 = _half(0)
                        dkt = jnp.where(h0, both(dk_acc[2 * kg]), both(dk_acc[2 * kg + 1]))
                        dvt = jnp.where(h0, both(dv_acc[2 * kg]), both(dv_acc[2 * kg + 1]))
                    dk_ref[rows, cols] = dkt.astype(dk_ref.dtype)
                    dv_ref[rows, cols] = dvt.astype(dv_ref.dtype)

    dk, dv = _pcall(
        body, name=name, grid=plan.grid, in_specs=specs, out_specs=[plan.out(wk)] * 2,
        out_shape=[jax.ShapeDtypeStruct((S // dil, dil * wk), out_dtype)] * 2,
        compiler_params=_cparams(("parallel", "parallel")),
    )(*[o_[0] for o_ in ops])
    return dk.reshape(S, wk), dv.reshape(S, wk)


def _causal_block(S):
    return min(512, S)


def _causal_mask(bq):
    qi = lax.broadcasted_iota(jnp.int32, (bq, bq), 0)
    kj = lax.broadcasted_iota(jnp.int32, (bq, bq), 1)
    return kj <= qi


def _half_of(rows, e):
    lane = lax.broadcasted_iota(jnp.int32, (rows, LANE), 1)
    return (lane < LANE // 2) if e == 0 else (lane >= LANE // 2)


def _causal_fwd(q, k, v, *, heads, scale, name):
    S = q.shape[0]
    bq = _causal_block(S)
    nq = S // bq
    npair = heads // 2
    wv = heads * 64

    def body(q_ref, k_ref, v_ref, o_ref, l_ref, m_sc, l_sc, acc_sc):
        i, t = pl.program_id(0), pl.program_id(1)

        @pl.when(t == 0)
        def _():
            m_sc[...] = jnp.full(m_sc.shape, NEG, F32)
            l_sc[...] = jnp.zeros(l_sc.shape, F32)
            acc_sc[...] = jnp.zeros(acc_sc.shape, F32)

        def step(masked):
            msk = _causal_mask(bq) if masked else None
            h0 = _half_of(bq, 0)
            for pr in range(npair):
                v2 = v_ref[:, pr * LANE:(pr + 1) * LANE].astype(BF16)
                new = []
                for e in range(2):
                    h = 2 * pr + e
                    cols = slice(h * LANE, (h + 1) * LANE)
                    s = lax.dot_general(q_ref[:, cols], k_ref[:, cols], _NT, preferred_element_type=F32) * scale
                    if masked:
                        s = jnp.where(msk, s, NEG)
                    m_old = m_sc[h]
                    m_new = jnp.maximum(m_old, jnp.max(s, axis=1, keepdims=True))
                    p = jnp.exp(s - m_new)
                    alpha = jnp.exp(m_old - m_new)
                    l_sc[h] = alpha * l_sc[h] + jnp.sum(p, axis=1, keepdims=True)
                    m_sc[h] = m_new
                    new.append((alpha, jnp.dot(p.astype(BF16), v2, preferred_element_type=F32)))
                acc = acc_sc[pr]
                acc_sc[pr] = jnp.where(h0, new[0][0] * acc + new[0][1], new[1][0] * acc + new[1][1])

        pl.when(t < i)(lambda: step(False))
        pl.when(t == i)(lambda: step(True))

        @pl.when(t == nq - 1)
        def _():
            h0 = _half_of(bq, 0)
            for pr in range(npair):
                l0, l1 = l_sc[2 * pr], l_sc[2 * pr + 1]
                acc = acc_sc[pr]
                cols = slice(pr * LANE, (pr + 1) * LANE)
                o_ref[:, cols] = jnp.where(h0, acc / l0, acc / l1)
                l_ref[:, cols] = jnp.where(h0, m_sc[2 * pr] + jnp.log(l0), m_sc[2 * pr + 1] + jnp.log(l1))

    qs = pl.BlockSpec((bq, heads * LANE), lambda i, t: (i, 0))
    ks = pl.BlockSpec((bq, heads * LANE), lambda i, t: (jnp.minimum(t, i), 0))
    vs = pl.BlockSpec((bq, wv), lambda i, t: (jnp.minimum(t, i), 0))
    os_ = pl.BlockSpec((bq, wv), lambda i, t: (i, 0))
    return _pcall(
        body, name=name, grid=(nq, nq), in_specs=[qs, ks, vs], out_specs=[os_, os_],
        out_shape=[jax.ShapeDtypeStruct((S, wv), F32)] * 2,
        scratch_shapes=[pltpu.VMEM((heads, bq, 1), F32), pltpu.VMEM((heads, bq, 1), F32), pltpu.VMEM((npair, bq, LANE), F32)],
        compiler_params=_cparams(("parallel", "arbitrary")),
    )(q, k, v)


def _causal_bwd_tile(q_ref, k_ref, v2, do2, prod, l2, h, e, scale, msk, bq):
    cols = slice(h * LANE, (h + 1) * LANE)
    hm = _half_of(bq, e)
    s = lax.dot_general(q_ref[:, cols], k_ref[:, cols], _NT, preferred_element_type=F32) * scale
    p = jnp.exp(s - l2[:, e * 64:e * 64 + 1])
    if msk is not None:
        p = jnp.where(msk, p, 0.0)
    dom = jnp.where(hm, do2, jnp.zeros_like(do2))
    delta = jnp.sum(jnp.where(hm, prod, 0.0), axis=1, keepdims=True)
    dp = lax.dot_general(dom, v2, _NT, preferred_element_type=F32)
    ds = (p * (dp - delta) * scale).astype(BF16)
    return p, ds, dom


def _causal_dq(q, k, v, do, o, lse, *, heads, scale, out_dtype=BF16, name):
    S = q.shape[0]
    bq = _causal_block(S)
    nq = S // bq
    npair = heads // 2
    wv = heads * 64

    def body(q_ref, k_ref, v_ref, do_ref, o_ref, l_ref, dq_ref, dq_sc):
        i, t = pl.program_id(0), pl.program_id(1)

        @pl.when(t == 0)
        def _():
            dq_sc[...] = jnp.zeros(dq_sc.shape, F32)

        def step(masked):
            msk = _causal_mask(bq) if masked else None
            for pr in range(npair):
                pc = slice(pr * LANE, (pr + 1) * LANE)
                v2 = v_ref[:, pc].astype(BF16)
                dof = do_ref[:, pc].astype(F32)
                prod = dof * o_ref[:, pc]
                do2 = dof.astype(BF16)
                l2 = l_ref[:, pc]
                for e in range(2):
                    h = 2 * pr + e
                    _, ds, _ = _causal_bwd_tile(q_ref, k_ref, v2, do2, prod, l2, h, e, scale, msk, bq)
                    dq_sc[h] += jnp.dot(ds, k_ref[:, h * LANE:(h + 1) * LANE], preferred_element_type=F32)

        pl.when(t < i)(lambda: step(False))
        pl.when(t == i)(lambda: step(True))

        @pl.when(t == nq - 1)
        def _():
            for h in range(heads):
                dq_ref[:, h * LANE:(h + 1) * LANE] = dq_sc[h].astype(dq_ref.dtype)

    qs = pl.BlockSpec((bq, heads * LANE), lambda i, t: (i, 0))
    ks = pl.BlockSpec((bq, heads * LANE), lambda i, t: (jnp.minimum(t, i), 0))
    vs = pl.BlockSpec((bq, wv), lambda i, t: (jnp.minimum(t, i), 0))
    os_ = pl.BlockSpec((bq, wv), lambda i, t: (i, 0))
    return _pcall(
        body, name=name, grid=(nq, nq), in_specs=[qs, ks, vs, os_, os_, os_], out_specs=qs,
        out_shape=jax.ShapeDtypeStruct((S, heads * LANE), out_dtype),
        scratch_shapes=[pltpu.VMEM((heads, bq, LANE), F32)],
        compiler_params=_cparams(("parallel", "arbitrary")),
    )(q, k, v, do, o, lse)


def _causal_dkv(q, k, v, do, o, lse, *, heads, scale, out_dtype=BF16, name):
    S = q.shape[0]
    bq = _causal_block(S)
    nq = S // bq
    npair = heads // 2
    wv = heads * 64

    def body(q_ref, k_ref, v_ref, do_ref, o_ref, l_ref, dk_ref, dv_ref, dk_sc, dv_sc):
        j, t = pl.program_id(0), pl.program_id(1)

        @pl.when(t == 0)
        def _():
            dk_sc[...] = jnp.zeros(dk_sc.shape, F32)
            dv_sc[...] = jnp.zeros(dv_sc.shape, F32)

        def step(masked):
            msk = _causal_mask(bq) if masked else None
            for pr in range(npair):
                pc = slice(pr * LANE, (pr + 1) * LANE)
                v2 = v_ref[:, pc].astype(BF16)
                dof = do_ref[:, pc].astype(F32)
                prod = dof * o_ref[:, pc]
                do2 = dof.astype(BF16)
                l2 = l_ref[:, pc]
                dv_add = None
                for e in range(2):
                    h = 2 * pr + e
                    p, ds, dom = _causal_bwd_tile(q_ref, k_ref, v2, do2, prod, l2, h, e, scale, msk, bq)
                    tv = lax.dot_general(p.astype(BF16), dom, _TN, preferred_element_type=F32)
                    dv_add = tv if dv_add is None else dv_add + tv
                    dk_sc[h] += lax.dot_general(ds, q_ref[:, h * LANE:(h + 1) * LANE], _TN, preferred_element_type=F32)
                dv_sc[pr] += dv_add

        pl.when(t > j)(lambda: step(False))
        pl.when(t == j)(lambda: step(True))

        @pl.when(t == nq - 1)
        def _():
            for h in range(heads):
                dk_ref[:, h * LANE:(h + 1) * LANE] = dk_sc[h].astype(dk_ref.dtype)
            for pr in range(npair):
                dv_ref[:, pr * LANE:(pr + 1) * LANE] = dv_sc[pr].astype(dv_ref.dtype)

    qi = lambda j, t: (jnp.maximum(t, j), 0)
    qs = pl.BlockSpec((bq, heads * LANE), qi)
    os_ = pl.BlockSpec((bq, wv), qi)
    ks = pl.BlockSpec((bq, heads * LANE), lambda j, t: (j, 0))
    vs = pl.BlockSpec((bq, wv), lambda j, t: (j, 0))
    return _pcall(
        body, name=name, grid=(nq, nq), in_specs=[qs, ks, vs, os_, os_, os_], out_specs=[ks, vs],
        out_shape=[jax.ShapeDtypeStruct((S, heads * LANE), out_dtype), jax.ShapeDtypeStruct((S, wv), out_dtype)],
        scratch_shapes=[pltpu.VMEM((heads, bq, LANE), F32), pltpu.VMEM((npair, bq, LANE), F32)],
        compiler_params=_cparams(("parallel", "arbitrary")),
    )(q, k, v, do, o, lse)


@jax.custom_vjp
def _bdot(x, w):
    return jnp.dot(x.astype(BF16), w.astype(BF16), preferred_element_type=F32)


def _bdot_fwd(x, w):
    return _bdot(x, w), (x, w)


def _bdot_bwd(res, g):
    x, w = res
    gb = g.astype(BF16)
    dx = lax.dot_general(gb, w.astype(BF16), _NT, preferred_element_type=F32)
    dw = lax.dot_general(x.astype(BF16), gb, _TN, preferred_element_type=F32)
    return dx.astype(x.dtype), dw.astype(w.dtype)


_bdot.defvjp(_bdot_fwd, _bdot_bwd)


def _tile_matrix(hd, width):
    r = lax.broadcasted_iota(jnp.int32, (hd, width), 0)
    c = lax.broadcasted_iota(jnp.int32, (hd, width), 1) % hd
    return jnp.where(r == c, 1.0, 0.0).astype(F32)


def _spread_matrix(heads, width):
    per = width // heads
    r = lax.broadcasted_iota(jnp.int32, (heads, width), 0)
    c = lax.broadcasted_iota(jnp.int32, (heads, width), 1) // per
    return jnp.where(r == c, 1.0, 0.0).astype(F32)


def _wide(t, width):
    n = width // t.shape[-1]
    return jnp.concatenate([t] * n, axis=1) if n > 1 else t


def _norm_heads(x, gain, hd):
    return _head_norm(x, _exact_dot(gain, _tile_matrix(hd, x.shape[-1])), hd)


def _norm_rope(x, gain, cos, sin, hd):
    w = x.shape[-1]
    return _rope(_norm_heads(x, gain, hd), _wide(cos, w), _wide(sin, w), hd)


def _f_norm(x, g):
    return (_row_norm(x, g),)


def _f_prep_acm(aq, ak, c0q, c0k, c0v, c1q, c1k, c1v, c2q, c2k, c2v, mq, cos, sin, g_aq, g_ak, g0q, g0k, g1q, g1k, g2q, g2k, g_mq):
    outs = [_norm_rope(aq, g_aq, cos, sin, A_HD), _norm_rope(ak, g_ak, cos, sin, A_HD)]
    for cq, ck, cv, gq, gk in ((c0q, c0k, c0v, g0q, g0k), (c1q, c1k, c1v, g1q, g1k), (c2q, c2k, c2v, g2q, g2k)):
        outs += [_norm_rope(cq, gq, cos, sin, C_HD), _norm_rope(ck, gk, cos, sin, C_HD), cv]
    outs.append(_norm_heads(mq, g_mq, M_HD))
    return tuple(outs)


def _rope_mla_q(x, cos, sin):
    lane = lax.broadcasted_iota(jnp.int32, x.shape, x.ndim - 1) % LANE
    half = B_ROPE // 2
    first = (lane >= B_NOPE) & (lane < B_NOPE + half)
    other = jnp.where(first, -_lane_roll(x, -half), _lane_roll(x, half))
    return x * cos + other * sin


def _f_prep_b(ckv, cq, kr, cosq, sinq, cosr, sinr, g_qa, g_kva, w_uq, w_ukv, g_q, g_k, g_kr):
    wq = B_HEADS * LANE
    q_up = _bdot(_row_norm(cq, g_qa), w_uq)
    gq = _exact_dot(g_q, _tile_matrix(LANE, wq))
    qf = _rope_mla_q(_head_norm(q_up, gq, "mla"), _wide(cosq, wq), _wide(sinq, wq))
    kv_up = _bdot(_row_norm(ckv, g_kva), w_ukv)
    kn = _head_norm(kv_up[:, :wq], _exact_dot(g_k, _tile_matrix(LANE, wq)), B_NOPE)
    vb = kv_up[:, wq:]
    kp = _rope(_head_norm(kr, g_kr, B_ROPE), cosr, sinr, B_ROPE)
    kp = _lane_roll(kp, B_NOPE)
    return qf, kn + _wide(kp, wq), vb


def _f_mem_k(k, g):
    return (_norm_heads(k, g, M_HD),)


def _f_sink(o, lse, sink):
    sb = _exact_dot(sink, _spread_matrix(A_HEADS, o.shape[-1]))
    m = jnp.maximum(lse, sb)
    tot = m + jnp.log(jnp.exp(lse - m) + jnp.exp(sb - m))
    return (o * jnp.exp(lse - tot),)


def _f_combine(o0, o1, o2, l0, l1, l2):
    m = jnp.maximum(jnp.maximum(l0, l1), l2)
    e0, e1, e2 = jnp.exp(l0 - m), jnp.exp(l1 - m), jnp.exp(l2 - m)
    inv = 1.0 / (e0 + e1 + e2)
    return ((e0 * o0 + e1 * o1 + e2 * o2) * inv,)


def _f_gatemix(gp, y0, y1, y2, y3, bg):
    d = y0.shape[-1]
    gates = 1.0 / (1.0 + jnp.exp(-(gp + bg)))
    mix = gates[:, :d] * y0
    for n, y in enumerate((y1, y2, y3), start=1):
        mix = mix + gates[:, n * d:(n + 1) * d] * y
    return (mix,)


def _relu2(u):
    return jnp.square(jnp.maximum(u, 0.0))


def _add(r, e):
    return r + e.astype(F32)


def _relu2_grad(r, u):
    return r * (2.0 * jnp.maximum(u, 0.0))


def _loss_and_grad(y, target, *, tb=512):
    R, D = y.shape
    tb = min(tb, R)

    def body(y_ref, t_ref, dy_ref, l_ref):
        err = y_ref[...] - t_ref[...]
        dy_ref[...] = err * (1.0 / D)
        part = 0.5 * jnp.sum(jnp.sum(err * err, axis=1, keepdims=True) * (1.0 / D), axis=0, keepdims=True)
        first = pl.program_id(0) == 0

        @pl.when(first)
        def _():
            l_ref[...] = jnp.broadcast_to(part, l_ref.shape)

        @pl.when(jnp.logical_not(first))
        def _():
            l_ref[...] += jnp.broadcast_to(part, l_ref.shape)

    dy, l = _pcall(
        body, name="loss", grid=(R // tb,),
        in_specs=[pl.BlockSpec((tb, D), lambda i: (i, 0))] * 2,
        out_specs=[pl.BlockSpec((tb, D), lambda i: (i, 0)), pl.BlockSpec((8, LANE), lambda i: (0, 0))],
        out_shape=[jax.ShapeDtypeStruct((R, D), F32), jax.ShapeDtypeStruct((8, LANE), F32)],
        compiler_params=_cparams(("arbitrary",)),
    )(y, target)
    return l[0, 0], dy


def _z_layout(d):
    assert d == 1024, "the aligned layout below is laid out for D_MODEL = 1024"
    lay = {"gates": (4 * d, 0)}
    for g in range(3):
        for n, nm in enumerate("qkv"):
            lay[f"c{g}{nm}"] = (512, 8 + 3 * g + n)
    lay.update(aq=(512, 17), mq=(512, 18), ckv=(256, 38), cq=(384, 26), ak=(128, 81), av=(128, 82), kr=(128, 83))
    return lay, 10752


_KW_A = dict(hq=A_HEADS, hk=A_KV_HEADS, scale=A_HD ** -0.5)
_KW_B = dict(heads=B_HEADS, scale=(B_NOPE + B_ROPE) ** -0.5)
_KW_C = dict(hq=C_HEADS, hk=C_HEADS, scale=C_HD ** -0.5)
_KW_M = dict(hq=M_HEADS, hk=M_HEADS, hd=M_HD, hdv=M_HD, scale=M_HD ** -0.5)


def _layer_fwd(l, x, mem, w, tabs):
    S, D = x.shape
    lay, _ = _z_layout(D)
    cosA, sinA, cosB, sinB, cosQ, sinQ = tabs
    gM = _AttnGeom("full", S, mem.shape[0])
    nm = lambda s: f"l{l}_{s}"
    sv = {}
    hn = _rowmap(_f_norm, [x], [w["g_mix"]], [(D, BF16)], tb=512, name=nm("norm1"))[0]
    z = _mm(hn, w["in"], name=nm("in"), tn=1536)
    zp = {k: (z, wd, idx) for k, (wd, idx) in lay.items()}
    acm_rows = [zp[k] for k in ("aq", "ak", "c0q", "c0k", "c0v", "c1q", "c1k", "c1v", "c2q", "c2k", "c2v", "mq")] + [cosA, sinA]
    acm_par = [w["a_qn"], w["a_kn"], w["c0q"], w["c0k"], w["c1q"], w["c1k"], w["c2q"], w["c2k"], w["m_qn"]]
    acm = _rowmap(_f_prep_acm, acm_rows, acm_par, [(p[1], BF16) for p in acm_rows[:12]], tb=256, name=nm("prep_acm"))
    qa, ka, qc0, kc0, vc0, qc1, kc1, vc1, qc2, kc2, vc2, mq = acm
    b_rows = [zp["ckv"], zp["cq"], zp["kr"], cosQ, sinQ, cosB, sinB]
    b_par = [w["b_qa"], w["b_kva"], w["uq"], w["ukv"], w["b_q"], w["b_k"], w["b_kr"]]
    qb, kb, vb = _rowmap(_f_prep_b, b_rows, b_par, [(B_HEADS * LANE, BF16), (B_HEADS * LANE, BF16), (512, BF16)], tb=256,
                         name=nm("prep_b"))
    memn = _rowmap(_f_norm, [mem], [w["m_g_mem"]], [(D, BF16)], tb=256, name=nm("mem_norm"))[0]
    mkv = _mm(memn, w["mkv"], name=nm("mem_kv"))
    mk = _rowmap(_f_mem_k, [(mkv, 512, 0)], [w["m_kn"]], [(512, BF16)], tb=256, name=nm("mem_k"))[0]
    mv = (mkv, 512, 1)

    oa_raw, lse_a = _band_fwd(qa, ka, zp["av"], S=S, dil=1, max_dist=A_WINDOW - 1, name=nm("attn_a"), **_KW_A)
    o_a = _rowmap(_f_sink, [oa_raw, lse_a], [w["a_sink"]], [(512, BF16)], tb=512, name=nm("sink"))[0]
    o_b, lse_b = _causal_fwd(qb, kb, vb, name=nm("attn_b"), **_KW_B)
    oc, lc = [], []
    for g, ((win, dil), qc, kc, vc) in enumerate(zip(C_PATTERNS, (qc0, qc1, qc2), (kc0, kc1, kc2), (vc0, vc1, vc2))):
        o_g, l_g = _band_fwd(qc, kc, vc, S=S, dil=dil, max_dist=win // dil, name=nm(f"attn_c{g}"), **_KW_C)
        oc.append(o_g)
        lc.append(l_g)
    o_c = _rowmap(_f_combine, oc + lc, [], [(512, BF16)], tb=512, name=nm("combine"))[0]
    o_m, lse_m = _attn_fwd(gM, mq, mk, mv, name=nm("attn_m"), **_KW_M)

    o_n = [o_a, o_b, o_c, o_m]
    ys = [_mm(o_n[n], w["branch"][n], name=nm(f"branch{n}")) for n in range(N_BRANCH)]
    mix = _rowmap(_f_gatemix, [zp["gates"]] + ys, [w["b_gate"]], [(D, BF16)], tb=256, name=nm("gatemix"))[0]
    x1 = _mm(mix, w["out"], extra=x, epi=_add, name=nm("out"))
    hn2 = _rowmap(_f_norm, [x1], [w["g_mlp"]], [(D, BF16)], tb=512, name=nm("norm2"))[0]
    u = _mm(hn2, w["up"], name=nm("up"))
    x2 = _mm(u, w["down"], pro_a=_relu2, extra=x1, epi=_add, name=nm("down"))
    sv.update(x=x, hn=hn, z=z, acm=acm, bq=(qb, kb, vb), memn=memn, mkv=mkv, mk=mk, oa_raw=oa_raw, lse_a=lse_a,
              o_b=o_b, lse_b=lse_b, oc=oc, lc=lc, o_m=o_m, lse_m=lse_m, o_n=o_n, ys=ys, mix=mix, x1=x1, hn2=hn2, u=u)
    return x2, sv


def _layer_bwd(l, dx2, mem, w, tabs, sv):
    x, z, x1, u = sv["x"], sv["z"], sv["x1"], sv["u"]
    S, D = x.shape
    lay, zw = _z_layout(D)
    cosA, sinA, cosB, sinB, cosQ, sinQ = tabs
    gM = _AttnGeom("full", S, mem.shape[0])
    nm = lambda s: f"l{l}_{s}"
    zp = {k: (z, wd, idx) for k, (wd, idx) in lay.items()}
    g = {}
    du = _mm(dx2, w["down"], tb=True, extra=u, epi=_relu2_grad, out_dtype=BF16, name=nm("d_down_x"))
    g["down"] = _mm(u, dx2, ta=True, pro_a=_relu2, name=nm("d_down_w"))
    dhn2 = _mm(du, w["up"], tb=True, name=nm("d_up_x"))
    g["up"] = _mm(sv["hn2"], du, ta=True, name=nm("d_up_w"))
    (dx1,), (g["g_mlp"],) = _rowmap_bwd(_f_norm, [x1], [w["g_mlp"]], [dhn2], diff=[True], out_dtypes=[F32], adds=[dx2],
                                        tb=256, name=nm("d_norm2"))
    dmix = _mm(dx1, w["out"], tb=True, name=nm("d_out_x"))
    g["out"] = _mm(sv["mix"], dx1, ta=True, name=nm("d_out_w"))
    (dgates, dy0, dy1, dy2, dy3), (g["b_gate"],) = _rowmap_bwd(
        _f_gatemix, [zp["gates"]] + sv["ys"], [w["b_gate"]], [dmix], diff=[True] * 5, out_dtypes=[BF16] * 5,
        tb=128, name=nm("d_gatemix"))
    dys = [dy0, dy1, dy2, dy3]
    do = [_mm(dys[n], w["branch"][n], tb=True, name=nm(f"d_branch{n}_x")) for n in range(N_BRANCH)]
    g["branch"] = [_mm(sv["o_n"][n], dys[n], ta=True, name=nm(f"d_branch{n}_w")) for n in range(N_BRANCH)]
    qa, ka, qc0, kc0, vc0, qc1, kc1, vc1, qc2, kc2, vc2, mq = sv["acm"]
    qb, kb, vb = sv["bq"]
    mkv, mk = sv["mkv"], sv["mk"]
    mv = (mkv, 512, 1)
    dmq = _attn_dq(gM, mq, mk, mv, do[3], sv["o_m"], sv["lse_m"], None, name=nm("attn_m_dq"), **_KW_M)
    dmk, dmv = _attn_dkv(gM, mq, mk, mv, do[3], sv["o_m"], sv["lse_m"], None, name=nm("attn_m_dkv"), **_KW_M)
    (doc0, doc1, doc2, dl0, dl1, dl2), _ = _rowmap_bwd(_f_combine, sv["oc"] + sv["lc"], [], [do[2]], diff=[True] * 6,
                                                      out_dtypes=[F32] * 6, tb=256, name=nm("d_combine"))
    dqc, dkc, dvc = [], [], []
    for gi, ((win, dil), qc, kc, vc, doc, dl) in enumerate(zip(C_PATTERNS, (qc0, qc1, qc2), (kc0, kc1, kc2), (vc0, vc1, vc2),
                                                              (doc0, doc1, doc2), (dl0, dl1, dl2))):
        args = (qc, kc, vc, doc, sv["oc"][gi], sv["lc"][gi], dl)
        kwc = dict(S=S, dil=dil, max_dist=win // dil, out_dtype=BF16, **_KW_C)
        dqc.append(_band_dq(*args, name=nm(f"attn_c{gi}_dq"), **kwc))
        dk_, dv_ = _band_dkv(*args, name=nm(f"attn_c{gi}_dkv"), **kwc)
        dkc.append(dk_)
        dvc.append(dv_)
    argsb = (qb, kb, vb, do[1], sv["o_b"], sv["lse_b"])
    dqb = _causal_dq(*argsb, name=nm("attn_b_dq"), **_KW_B)
    dkb, dvb = _causal_dkv(*argsb, name=nm("attn_b_dkv"), **_KW_B)
    (doa_raw, dlse_a), (g["a_sink"],) = _rowmap_bwd(_f_sink, [sv["oa_raw"], sv["lse_a"]], [w["a_sink"]], [do[0]],
                                                   diff=[True, True], out_dtypes=[F32, F32], tb=256, name=nm("d_sink"))
    argsa = (qa, ka, zp["av"], doa_raw, sv["oa_raw"], sv["lse_a"], dlse_a)
    kwa = dict(S=S, dil=1, max_dist=A_WINDOW - 1, out_dtype=BF16, **_KW_A)
    dqa = _band_dq(*argsa, name=nm("attn_a_dq"), **kwa)
    dka, dva = _band_dkv(*argsa, name=nm("attn_a_dkv"), **kwa)
    acm_rows = [zp[k] for k in ("aq", "ak", "c0q", "c0k", "c0v", "c1q", "c1k", "c1v", "c2q", "c2k", "c2v", "mq")] + [cosA, sinA]
    acm_par = [w["a_qn"], w["a_kn"], w["c0q"], w["c0k"], w["c1q"], w["c1k"], w["c2q"], w["c2k"], w["m_qn"]]
    acm_ct = [dqa, dka, dqc[0], dkc[0], dvc[0], dqc[1], dkc[1], dvc[1], dqc[2], dkc[2], dvc[2], dmq]
    dacm, (g["a_qn"], g["a_kn"], g["c0q"], g["c0k"], g["c1q"], g["c1k"], g["c2q"], g["c2k"], g["m_qn"]) = _rowmap_bwd(
        _f_prep_acm, acm_rows, acm_par, acm_ct, diff=[True] * 12 + [False, False], out_dtypes=[BF16] * 12, tb=128,
        name=nm("d_prep_acm"))
    d_aq, d_ak, d_c0q, d_c0k, d_c0v, d_c1q, d_c1k, d_c1v, d_c2q, d_c2k, d_c2v, d_mq = dacm
    b_rows = [zp["ckv"], zp["cq"], zp["kr"], cosQ, sinQ, cosB, sinB]
    b_par = [w["b_qa"], w["b_kva"], w["uq"], w["ukv"], w["b_q"], w["b_k"], w["b_kr"]]
    (d_ckv, d_cq, d_kr), gb = _rowmap_bwd(_f_prep_b, b_rows, b_par, [dqb, dkb, dvb], diff=[True] * 3 + [False] * 4,
                                          out_dtypes=[BF16] * 3, tb=256, name=nm("d_prep_b"))
    g["b_qa"], g["b_kva"], g["uq"], g["ukv"], g["b_q"], g["b_k"], g["b_kr"] = gb
    (dmkv_k,), (g["m_kn"],) = _rowmap_bwd(_f_mem_k, [(mkv, 512, 0)], [w["m_kn"]], [dmk], diff=[True], out_dtypes=[F32],
                                          tb=256, name=nm("d_mem_k"))
    dmkv = jnp.concatenate([dmkv_k, dmv], axis=1)
    dmemn = _mm(dmkv, w["mkv"], tb=True, name=nm("d_mem_kv_x"))
    g["mkv"] = _mm(sv["memn"], dmkv, ta=True, name=nm("d_mem_kv_w"))
    _, (g["m_g_mem"],) = _rowmap_bwd(_f_norm, [mem], [w["m_g_mem"]], [dmemn], diff=[True], out_dtypes=[F32], tb=256,
                                     name=nm("d_mem_norm"))
    dz = jnp.concatenate([dgates, d_c0q, d_c0k, d_c0v, d_c1q, d_c1k, d_c1v, d_c2q, d_c2k, d_c2v, d_aq, d_mq, d_ckv, d_cq,
                          d_ak, dva, d_kr], axis=1)
    assert dz.shape[1] == zw
    dhn = _mm(dz, w["in"], tb=True, name=nm("d_in_x"), tk=1536)
    g["in"] = _mm(sv["hn"], dz, ta=True, name=nm("d_in_w"), tn=1536)
    (dx,), (g["g_mix"],) = _rowmap_bwd(_f_norm, [x], [w["g_mix"]], [dhn], diff=[True], out_dtypes=[F32], adds=[dx1],
                                       tb=256, name=nm("d_norm1"))
    return dx, g


_IN_ORIG = dict(aq=(0, 512), ak=(512, 640), av=(640, 768), cq=(768, 1152), ckv=(1152, 1408), kr=(1408, 1440),
                c=(1440, 6048), mq=(6048, 6560), gates=(6560, 10656))
_IN_OURS = dict(gates=(0, 4096), c=(4096, 8704), aq=(8704, 9216), mq=(9216, 9728), ckv=(9728, 9984), cq=(9984, 10368),
                ak=(10368, 10496), av=(10496, 10624), kr=(10624, 10656))
_IN_ORDER_ORIG = ("aq", "ak", "av", "cq", "ckv", "kr", "c", "mq", "gates")


def _in_to_ours(w_in):
    pc = {k: w_in[..., a:b] for k, (a, b) in _IN_ORIG.items()}
    zeros = lambda n: jnp.zeros(w_in.shape[:-1] + (n,), w_in.dtype)
    return jnp.concatenate([pc["gates"], pc["c"], pc["aq"], pc["mq"], pc["ckv"], pc["cq"], pc["ak"], pc["av"], pc["kr"],
                            zeros(96)], axis=-1)


def _in_from_ours(g_in):
    return jnp.concatenate([g_in[..., _IN_OURS[k][0]:_IN_OURS[k][1]] for k in _IN_ORDER_ORIG], axis=-1)


def _uq_to_ours(w):
    per = B_NOPE + B_ROPE
    w4 = w.reshape(w.shape[:-1] + (B_HEADS, per))
    w4 = jnp.pad(w4, [(0, 0)] * (w4.ndim - 1) + [(0, LANE - per)])
    return w4.reshape(w.shape[:-1] + (B_HEADS * LANE,))


def _uq_from_ours(g):
    per = B_NOPE + B_ROPE
    g4 = g.reshape(g.shape[:-1] + (B_HEADS, LANE))[..., :per]
    return g4.reshape(g.shape[:-1] + (B_HEADS * per,))


def _ukv_to_ours(w):
    w4 = w.reshape(w.shape[:-1] + (B_HEADS, B_NOPE + B_V))
    keys = jnp.pad(w4[..., :B_NOPE], [(0, 0)] * (w4.ndim - 1) + [(0, LANE - B_NOPE)])
    vals = w4[..., B_NOPE:]
    return jnp.concatenate([keys.reshape(w.shape[:-1] + (B_HEADS * LANE,)), vals.reshape(w.shape[:-1] + (B_HEADS * B_V,))],
                           axis=-1)


def _ukv_from_ours(g):
    wq = B_HEADS * LANE
    keys = g[..., :wq].reshape(g.shape[:-1] + (B_HEADS, LANE))[..., :B_NOPE]
    vals = g[..., wq:].reshape(g.shape[:-1] + (B_HEADS, B_V))
    return jnp.concatenate([keys, vals], axis=-1).reshape(g.shape[:-1] + (B_HEADS * (B_NOPE + B_V),))


def _layer_weights(big_l, small, l):
    row = lambda a: a[l].reshape(1, -1)
    w = dict(big_l)
    w.update(g_mix=row(small["g_mix"]), b_gate=row(small["b_gate"]), a_qn=row(small["a_qn"]), a_kn=row(small["a_kn"]),
             a_sink=row(small["a_sink"]), b_qa=row(small["b_qa_norm"]), b_kva=row(small["b_kva_norm"]),
             b_q=jnp.pad(small["b_qn"][l], (0, LANE - B_NOPE - B_ROPE)).reshape(1, -1),
             b_k=jnp.pad(small["b_kn"][l, :B_NOPE], (0, LANE - B_NOPE)).reshape(1, -1),
             b_kr=jnp.pad(small["b_kn"][l, B_NOPE:], (0, LANE - B_ROPE)).reshape(1, -1),
             m_g_mem=row(small["m_g_mem"]), m_qn=row(small["m_qn"]), m_kn=row(small["m_kn"]), g_mlp=row(small["g_mlp"]))
    for g in range(3):
        w[f"c{g}q"] = small["c_qn"][l, g].reshape(1, -1)
        w[f"c{g}k"] = small["c_kn"][l, g].reshape(1, -1)
    return w


def _rope_tables(positions):
    pos = positions.astype(F32)[:, None]
    tabs = []
    for dim in (A_HD, B_ROPE):
        inv = ROPE_THETA ** (-jnp.arange(0, dim, 2, dtype=F32) / dim)
        ang = pos * inv
        reps = LANE // (dim // 2)
        tabs += [jnp.tile(jnp.cos(ang), (1, reps)), jnp.tile(jnp.sin(ang), (1, reps))]
    half = B_ROPE // 2
    cb, sb = tabs[2][:, :half], tabs[3][:, :half]
    ones, zeros = jnp.ones((pos.shape[0], B_NOPE), F32), jnp.zeros((pos.shape[0], B_NOPE), F32)
    pad = LANE - B_NOPE - B_ROPE
    tabs.append(jnp.concatenate([ones, cb, cb, ones[:, :pad]], axis=1))
    tabs.append(jnp.concatenate([zeros, sb, sb, zeros[:, :pad]], axis=1))
    return tuple(tabs)


def _local_step(x, mem, positions, small, loss_target, get_big, put_grads):
    depth = small["g_mix"].shape[0]
    tabs = _rope_tables(positions)
    ws, saved = [], []
    h = x
    for l in range(depth):
        ws.append(_layer_weights(get_big(l, h), small, l))
        h, sv = _layer_fwd(l, h, mem, ws[l], tabs)
        saved.append(sv)
    loss, dh = _loss_and_grad(h, loss_target)
    small_grads = [None] * depth
    for l in reversed(range(depth)):
        dh, g = _layer_bwd(l, dh, mem, ws[l], tabs, saved[l])
        zero = put_grads(l, g)
        if l > 0:
            ws[l - 1] = dict(ws[l - 1], g_mlp=ws[l - 1]["g_mlp"] + zero)
        small_grads[l] = g
    return loss, dh, small_grads


def _small_grads_to_reference_layout(grads):
    flat = lambda k: jnp.stack([g[k].reshape(-1) for g in grads])
    return dict(
        g_mix=flat("g_mix"), b_gate=flat("b_gate"), a_qn=flat("a_qn"), a_kn=flat("a_kn"), a_sink=flat("a_sink"),
        b_qa_norm=flat("b_qa"), b_kva_norm=flat("b_kva"), b_qn=flat("b_q")[:, :B_NOPE + B_ROPE],
        b_kn=jnp.concatenate([flat("b_k")[:, :B_NOPE], flat("b_kr")[:, :B_ROPE]], axis=1),
        c_qn=jnp.stack([jnp.stack([g[f"c{i}q"].reshape(-1) for i in range(3)]) for g in grads]),
        c_kn=jnp.stack([jnp.stack([g[f"c{i}k"].reshape(-1) for i in range(3)]) for g in grads]),
        m_g_mem=flat("m_g_mem"), m_qn=flat("m_qn"), m_kn=flat("m_kn"), g_mlp=flat("g_mlp"))


def _big_grads_to_reference_layout(g):
    return dict(w_in=_in_from_ours(g["in"]), b_w_uq=_uq_from_ours(g["uq"]), b_w_ukv=_ukv_from_ours(g["ukv"]), m_w_kv=g["mkv"],
                w_branch=jnp.stack(g["branch"]), w_out=g["out"], w_up=g["up"], w_down=g["down"])


def _big_to_kernel_layout(full):
    c = lambda a: a.astype(BF16)
    return {"in": c(_in_to_ours(full["w_in"])), "uq": c(_uq_to_ours(full["b_w_uq"])),
            "ukv": c(_ukv_to_ours(full["b_w_ukv"])), "mkv": c(full["m_w_kv"]),
            "branch": c(full["w_branch"]), "out": c(full["w_out"]), "up": c(full["w_up"]), "down": c(full["w_down"])}


MESH = pl.DeviceIdType.MESH
N_CHIPS = 4
N_DEV = 8
_ANY = pl.BlockSpec(memory_space=pl.ANY)


_HBM = pl.BlockSpec(memory_space=pltpu.HBM)
_SEM = pl.BlockSpec(memory_space=pltpu.SEMAPHORE)
_EFFECT = pltpu.SideEffectType.DATAFLOW_SIDE_EFFECTING


def _chip_peers():
    x, y, c = lax.axis_index("x"), lax.axis_index("y"), lax.axis_index("c")
    return 2 * x + y, [((1 - x, y, c), 2 * (1 - x) + y), ((x, 1 - y, c), 2 * x + 1 - y), ((1 - x, 1 - y, c), 2 * (1 - x) + 1 - y)]


def _exchange_start(srcs, lands, *, gather, name):
    n = len(srcs)

    def body(*refs):
        ins, land = refs[:n], refs[n:2 * n]
        send_sems, recv_sems = refs[2 * n], refs[2 * n + 1]
        token = refs[-1]
        me, peers = _chip_peers()
        for a in range(n):
            for j, (dev, chip) in enumerate(peers):
                src = ins[a] if gather else ins[a].at[chip]
                pltpu.make_async_remote_copy(src_ref=src, dst_ref=land[a].at[me], send_sem=send_sems.at[3 * a + j],
                                             recv_sem=recv_sems.at[3 * a + j], device_id=dev, device_id_type=MESH).start()
        token[...] = jnp.zeros(token.shape, token.dtype)

    hbm = lambda a: pltpu.HBM(a.shape, a.dtype)
    outs = _pcall(
        body, name=name,
        out_shape=(pltpu.SemaphoreType.DMA((3 * n,)), pltpu.SemaphoreType.DMA((3 * n,)), *[hbm(a) for a in srcs],
                   *[hbm(a) for a in lands], jax.ShapeDtypeStruct((8, LANE), F32)),
        in_specs=[_HBM] * (2 * n), out_specs=(_SEM, _SEM, *([_HBM] * (2 * n)), pl.BlockSpec(memory_space=pltpu.VMEM)),
        input_output_aliases={a: 2 + a for a in range(2 * n)},
        compiler_params=pltpu.CompilerParams(has_side_effects=_EFFECT),
    )(*[pltpu.with_memory_space_constraint(a, pltpu.HBM) for a in srcs],
      *[pltpu.with_memory_space_constraint(a, pltpu.HBM) for a in lands])
    return outs[0], outs[1], list(outs[2:2 + n]), list(outs[2 + n:2 + 2 * n]), outs[-1]


def _exchange_wait(state, after, *, gather, name):
    send_sems, recv_sems, srcs, lands, _ = state
    n = len(lands)

    def body(*refs):
        src_refs, land = refs[:n], refs[n:2 * n]
        ssem, rsem = refs[2 * n], refs[2 * n + 1]
        me, peers = _chip_peers()
        for a in range(n):
            for j, (dev, chip) in enumerate(peers):
                src = src_refs[a] if gather else src_refs[a].at[chip]
                cp = pltpu.make_async_remote_copy(src_ref=src, dst_ref=land[a].at[chip], send_sem=ssem.at[3 * a + j],
                                                  recv_sem=rsem.at[3 * a + j], device_id=dev, device_id_type=MESH)
                cp.wait_send()
                cp.wait_recv()

    outs = _pcall(
        body, name=name,
        out_shape=tuple(pltpu.HBM(a.shape, a.dtype) for a in list(srcs) + list(lands)),
        in_specs=[_HBM] * (2 * n) + [_SEM, _SEM, pl.BlockSpec(memory_space=pl.ANY)],
        out_specs=tuple([_HBM] * (2 * n)), input_output_aliases={a: a for a in range(2 * n)},
        compiler_params=pltpu.CompilerParams(has_side_effects=_EFFECT),
    )(*srcs, *lands, send_sems, recv_sems, after)
    return list(outs[n:])


def _sibling_exchange(arrays, *, name):
    n = len(arrays)

    def body(*refs):
        ins, outs = refs[:n], refs[n:2 * n]
        send_sems, recv_sems = refs[2 * n:]
        x, y, c = lax.axis_index("x"), lax.axis_index("y"), lax.axis_index("c")
        cps = []
        for a in range(n):
            cp = pltpu.make_async_remote_copy(src_ref=ins[a], dst_ref=outs[a], send_sem=send_sems.at[a], recv_sem=recv_sems.at[a],
                                              device_id=(x, y, 1 - c), device_id_type=MESH)
            cp.start()
            cps.append(cp)
        for cp in cps:
            cp.wait()

    return _pcall(
        body, name=name, in_specs=[_ANY] * n, out_specs=[_ANY] * n,
        out_shape=[jax.ShapeDtypeStruct(a.shape, a.dtype) for a in arrays],
        scratch_shapes=[pltpu.SemaphoreType.DMA((n,)), pltpu.SemaphoreType.DMA((n,))],
        compiler_params=pltpu.CompilerParams(has_side_effects=True),
    )(*arrays)


def _allreduce_small(v, *, name):
    rows = v.shape[0]

    def body(v_ref, o_ref, slots, send_sems, recv_sems):
        x, y, c = lax.axis_index("x"), lax.axis_index("y"), lax.axis_index("c")
        me = 4 * x + 2 * y + c
        slots[me] = v_ref[...]
        cps = []
        for k in range(1, N_DEV):
            fx, fy, fc = (k >> 2) & 1, (k >> 1) & 1, k & 1
            peer = (x ^ fx, y ^ fy, c ^ fc)
            cp = pltpu.make_async_remote_copy(src_ref=v_ref, dst_ref=slots.at[me], send_sem=send_sems.at[k - 1],
                                              recv_sem=recv_sems.at[k - 1], device_id=peer, device_id_type=MESH)
            cp.start()
            cps.append((cp, peer))
        for k, (cp, (px, py, pc)) in enumerate(cps):
            pltpu.make_async_remote_copy(src_ref=v_ref, dst_ref=slots.at[4 * px + 2 * py + pc], send_sem=send_sems.at[k],
                                         recv_sem=recv_sems.at[k], device_id=(px, py, pc), device_id_type=MESH).wait_recv()
        for cp, _ in cps:
            cp.wait_send()
        tot = slots[0]
        for d in range(1, N_DEV):
            tot = tot + slots[d]
        o_ref[...] = tot

    vm = pl.BlockSpec(memory_space=pltpu.VMEM)
    return _pcall(
        body, name=name, in_specs=[vm], out_specs=vm, out_shape=jax.ShapeDtypeStruct(v.shape, F32),
        scratch_shapes=[pltpu.VMEM((N_DEV, rows, LANE), F32), pltpu.SemaphoreType.DMA((N_DEV - 1,)),
                        pltpu.SemaphoreType.DMA((N_DEV - 1,))],
        compiler_params=pltpu.CompilerParams(has_side_effects=True),
    )(v)


def _rows_block(rows, cols, itemsize=4, target_bytes=1 << 20):
    want = max(16, target_bytes // max(1, cols * itemsize))
    best = rows
    for t in range(16, rows, 16):
        if rows % t == 0 and t <= want:
            best = t
    return best if best <= want or rows <= want else rows


def _sum_slots(recvs, parts, me, *, name):
    nl = len(recvs)
    shp = recvs[0].shape[1:]
    r3 = [r.reshape(N_CHIPS, -1, shp[-1]) for r in recvs]
    p3 = [q.reshape(N_CHIPS, -1, shp[-1]) for q in parts]
    rows, cols = r3[0].shape[1:]
    tb = _rows_block(rows, cols)
    nblk = rows // tb
    per = N_CHIPS + 1

    def body(me_ref, *refs):
        o_ref = refs[-1]
        lg = pl.program_id(0)
        for l in range(nl):
            r = refs[per * l:per * (l + 1)]

            @pl.when(lg == l)
            def _(r=r):
                tot = ((r[0][...].astype(F32) + r[1][...].astype(F32)) + r[2][...].astype(F32)) + r[3][...].astype(F32)
                o_ref[...] = tot + r[4][...].astype(F32)

    def row(l, lg, i):
        return jnp.where(lg < l, 0, jnp.where(lg > l, nblk - 1, i))

    in_specs, args = [], []
    for l in range(nl):
        for k in range(N_CHIPS):
            in_specs.append(pl.BlockSpec((None, tb, cols), lambda lg, i, me_ref, l=l, k=k: (k, row(l, lg, i), 0)))
            args.append(r3[l])
        in_specs.append(pl.BlockSpec((None, tb, cols), lambda lg, i, me_ref, l=l: (me_ref[0], row(l, lg, i), 0)))
        args.append(p3[l])
    out = _pcall(
        body, name=name,
        grid_spec=pltpu.PrefetchScalarGridSpec(
            num_scalar_prefetch=1, grid=(nl, nblk), in_specs=in_specs,
            out_specs=pl.BlockSpec((None, tb, cols), lambda lg, i, me_ref: (lg, i, 0))),
        out_shape=jax.ShapeDtypeStruct((nl, rows, cols), F32), compiler_params=_cparams(("arbitrary", "arbitrary")),
    )(me, *args)
    return out.reshape((nl,) + shp)


def _adamw(w, g_parts, m, v, *, name):
    shp = w.shape
    two = lambda a: a.reshape(-1, shp[-1])
    rows, cols = two(w).shape
    tb = _rows_block(rows, cols, target_bytes=1 << 19)
    npart = len(g_parts)

    def body(*refs):
        w_ref = refs[0]
        gp = refs[1:1 + npart]
        m_ref, v_ref, g_out, d_out, m_out, v_out = refs[1 + npart:]
        g = gp[0][...]
        for r in gp[1:]:
            g = g + r[...]
        wv = w_ref[...]
        m2 = ADAM_B1 * m_ref[...] + (1.0 - ADAM_B1) * g
        v2 = ADAM_B2 * v_ref[...] + (1.0 - ADAM_B2) * jnp.square(g)
        m_hat = m2 / (1.0 - ADAM_B1 ** ADAM_STEP)
        v_hat = v2 / (1.0 - ADAM_B2 ** ADAM_STEP)
        g_out[...] = g
        d_out[...] = -ADAM_LR * (m_hat / (jnp.sqrt(v_hat) + ADAM_EPS) + ADAM_WD * wv)
        m_out[...] = m2
        v_out[...] = v2

    spec = pl.BlockSpec((tb, cols), lambda i: (i, 0))
    outs = _pcall(
        body, name=name, grid=(rows // tb,), in_specs=[spec] * (3 + npart), out_specs=[spec] * 4,
        out_shape=[jax.ShapeDtypeStruct((rows, cols), F32)] * 4, compiler_params=_cparams(("parallel",)),
    )(two(w), *[two(p) for p in g_parts], two(m), two(v))
    return [o.reshape(shp) for o in outs]


BIG = ("w_in", "b_w_uq", "b_w_ukv", "m_w_kv", "w_branch", "w_out", "w_up", "w_down")
_SHARD_AXIS = dict(w_in=2, b_w_uq=2, b_w_ukv=2, m_w_kv=1, w_branch=3, w_out=1, w_up=2, w_down=1)
SMALL = ("g_mix", "b_gate", "a_qn", "a_kn", "a_sink", "b_qa_norm", "b_kva_norm", "b_qn", "b_kn", "c_qn", "c_kn",
         "m_g_mem", "m_qn", "m_kn", "g_mlp")
WEIGHTS = ("g_mix", "w_in", "b_gate", "a_qn", "a_kn", "a_sink", "b_qa_norm", "b_kva_norm", "b_w_uq", "b_w_ukv", "b_qn", "b_kn",
           "c_qn", "c_kn", "m_g_mem", "m_w_kv", "m_qn", "m_kn", "w_branch", "w_out", "g_mlp", "w_up", "w_down")


def _unshard(gathered, axis):
    moved = jnp.moveaxis(gathered, 0, axis)
    shp = list(gathered.shape[1:])
    shp[axis] *= N_CHIPS
    return moved.reshape(shp)


def _shard_parts(full, axis):
    shp = list(full.shape)
    shp[axis:axis + 1] = [N_CHIPS, shp[axis] // N_CHIPS]
    return jnp.moveaxis(full.reshape(shp), axis, 0)


def _pack_small(d):
    flat = jnp.concatenate([d[k].reshape(-1).astype(F32) for k in SMALL])
    n = flat.shape[0]
    pad = (-n) % (8 * LANE)
    return jnp.pad(flat, (0, pad)).reshape(-1, LANE)


def _unpack_small(packed, like):
    flat = packed.reshape(-1)
    out, off = {}, 0
    for k in SMALL:
        n = int(np.prod(like[k].shape))
        out[k] = flat[off:off + n].reshape(like[k].shape)
        off += n
    return out


def _train_step(x, mem, positions, loss_target, w, m, v):
    depth = w["g_mix"].shape[0]
    me = 2 * lax.axis_index("x") + lax.axis_index("y")
    landing = lambda a: jnp.zeros((N_CHIPS,) + a.shape, a.dtype)
    gathers, own = [], []
    for l in range(depth):
        own.append([w[k][l].astype(BF16) for k in BIG])
        gathers.append(_exchange_start(own[l], [landing(a) for a in own[l]], gather=True, name=f"gather_start{l}"))

    def with_own(land, mine):
        slot = lax.broadcasted_iota(jnp.int32, (N_CHIPS,) + (1,) * mine.ndim, 0)
        return jnp.where(slot == me, mine[None], land)

    def get_big(l, after):
        lands = _exchange_wait(gathers[l], after, gather=True, name=f"gather_wait{l}")
        return _big_to_kernel_layout({k: _unshard(with_own(g, o), _SHARD_AXIS[k] - 1) for k, g, o in zip(BIG, lands, own[l])})

    scatters, parts = [None] * depth, [None] * depth

    def put_grads(l, g):
        gref = _big_grads_to_reference_layout(g)
        parts[l] = [_shard_parts(gref[k], _SHARD_AXIS[k] - 1).astype(BF16) for k in BIG]
        lands = [landing(p[0]) for p in parts[l]]
        scatters[l] = _exchange_start(parts[l], lands, gather=False, name=f"scatter_start{l}")
        return scatters[l][4][:1, :1]

    small = {k: w[k] for k in SMALL}
    loss, gx, grads = _local_step(x[0], mem[0], positions[0], small, loss_target[0], get_big, put_grads)
    loss = lax.psum(loss, ("x", "y", "c"))
    recv = [_exchange_wait(scatters[l], gx, gather=False, name=f"scatter_wait{l}") for l in range(depth)]
    me1 = me.reshape(1).astype(jnp.int32)
    mine = [_sum_slots([recv[l][a] for l in range(depth)], [parts[l][a] for l in range(depth)], me1, name=f"sum_{k}")
            for a, k in enumerate(BIG)]
    theirs = _sibling_exchange(mine, name="sibling_grads")
    res = {}
    for k, p, q in zip(BIG, mine, theirs):
        res[k] = _adamw(w[k], [p, q], m[k], v[k], name=f"adamw_{k}")
    gsmall = _small_grads_to_reference_layout(grads)
    g_small = _allreduce_small(_pack_small(gsmall), name="allreduce_small")
    packed = _adamw(_pack_small(small), [g_small], _pack_small({k: m[k] for k in SMALL}), _pack_small({k: v[k] for k in SMALL}),
                    name="adamw_small")
    unpacked = [_unpack_small(p, small) for p in packed]
    for k in SMALL:
        res[k] = [u[k] for u in unpacked]
    outs = [loss, gx[None]]
    for i in range(4):
        outs += [res[k][i] for k in WEIGHTS]
    return tuple(outs)

def kernel(x, mem, positions, g_mix, w_in, b_gate, a_qn, a_kn, a_sink, b_qa_norm, b_kva_norm, b_w_uq, b_w_ukv, b_qn, b_kn, c_qn, c_kn, m_g_mem, m_w_kv, m_qn, m_kn, w_branch, w_out, g_mlp, w_up, w_down, loss_target, m_g_mix, m_w_in, m_b_gate, m_a_qn, m_a_kn, m_a_sink, m_b_qa_norm, m_b_kva_norm, m_b_w_uq, m_b_w_ukv, m_b_qn, m_b_kn, m_c_qn, m_c_kn, m_m_g_mem, m_m_w_kv, m_m_qn, m_m_kn, m_w_branch, m_w_out, m_g_mlp, m_w_up, m_w_down, v_g_mix, v_w_in, v_b_gate, v_a_qn, v_a_kn, v_a_sink, v_b_qa_norm, v_b_kva_norm, v_b_w_uq, v_b_w_ukv, v_b_qn, v_b_kn, v_c_qn, v_c_kn, v_m_g_mem, v_m_w_kv, v_m_qn, v_m_kn, v_w_branch, v_w_out, v_g_mlp, v_w_up, v_w_down):
    args = dict(locals())
    w = {k: args[k] for k in WEIGHTS}
    m = {k: args["m_" + k] for k in WEIGHTS}
    v = {k: args["v_" + k] for k in WEIGHTS}
    return _train_step(x, mem, positions, loss_target, w, m, v)
```

```python
import functools
import math

import jax
import jax.numpy as jnp
import numpy as np
from jax import lax
from jax.experimental import pallas as pl
from jax.experimental.pallas import tpu as pltpu

F32 = jnp.float32
BF16 = jnp.bfloat16

DEPTH = 4
BLOCK = 128
ROPE_THETA = 10000.0
EPS = 1e-6
NEG = -1e30
A_HEADS, A_KV_HEADS, A_HD, A_WINDOW = 8, 2, 64, 128
B_HEADS, B_Q_LORA, B_KV_LORA, B_NOPE, B_ROPE, B_V = 8, 384, 256, 64, 32, 64
C_PATTERNS = ((128, 1), (512, 4), (2048, 16))
C_HEADS, C_HD = 8, 64
M_HEADS, M_HD = 4, 128
N_BRANCH, BRANCH_W = 4, 512
ADAM_LR, ADAM_B1, ADAM_B2, ADAM_EPS, ADAM_WD, ADAM_STEP = 0.001, 0.9, 0.999, 1e-08, 0.01, 10

LANE = 128
VMEM_LIMIT = 56 * 1024 * 1024


def _pcall(body, **kw):
    return pl.pallas_call(body, **kw)


def _cparams(sem):
    return pltpu.CompilerParams(dimension_semantics=sem, vmem_limit_bytes=VMEM_LIMIT)


def _tile(n, target):
    if n <= target:
        return n
    best = None
    for t in range(LANE, target + 1, LANE):
        if n % t == 0:
            best = t
    return best if best is not None else n


def _mm(a, b, *, ta=False, tb=False, out_dtype=F32, pro_a=None, epi=None, extra=None, name,
        tm=1024, tn=1024, tk=1024):
    if ta:
        K, M = a.shape
    else:
        M, K = a.shape
    if tb:
        N, K2 = b.shape
    else:
        K2, N = b.shape
    assert K == K2, (a.shape, b.shape, ta, tb)
    tm, tn, tk = _tile(M, tm), _tile(N, tn), _tile(K, tk)
    nk = K // tk
    a_spec = pl.BlockSpec((tk, tm), lambda i, j, k: (k, i)) if ta else pl.BlockSpec((tm, tk), lambda i, j, k: (i, k))
    b_spec = pl.BlockSpec((tn, tk), lambda i, j, k: (j, k)) if tb else pl.BlockSpec((tk, tn), lambda i, j, k: (k, j))
    o_spec = pl.BlockSpec((tm, tn), lambda i, j, k: (i, j))
    dims = (((0,) if ta else (1,), (1,) if tb else (0,)), ((), ()))
    has_extra = extra is not None

    def body(*refs):
        if has_extra:
            a_ref, b_ref, e_ref, o_ref, acc_ref = refs
        else:
            a_ref, b_ref, o_ref, acc_ref = refs
            e_ref = None
        k = pl.program_id(2)
        av = a_ref[...]
        if pro_a is not None:
            av = pro_a(av.astype(F32))
        part = lax.dot_general(av.astype(BF16), b_ref[...].astype(BF16), dims, preferred_element_type=F32)

        @pl.when(k == 0)
        def _():
            acc_ref[...] = part

        @pl.when(k > 0)
        def _():
            acc_ref[...] += part

        @pl.when(k == nk - 1)
        def _():
            r = acc_ref[...]
            if epi is not None:
                r = epi(r, e_ref[...]) if has_extra else epi(r)
            o_ref[...] = r.astype(out_dtype)

    in_specs = [a_spec, b_spec] + ([o_spec] if has_extra else [])
    args = (a, b) + ((extra,) if has_extra else ())
    return _pcall(
        body, name=name, grid=(M // tm, N // tn, nk), in_specs=in_specs, out_specs=o_spec,
        out_shape=jax.ShapeDtypeStruct((M, N), out_dtype),
        scratch_shapes=[pltpu.VMEM((tm, tn), F32)],
        compiler_params=_cparams(("parallel", "parallel", "arbitrary")),
    )(*args)


def _piece(p):
    if isinstance(p, tuple):
        return p
    return (p, p.shape[1], 0)


def _row_spec(width, idx, tb):
    return pl.BlockSpec((tb, width), lambda i, idx=idx: (i, idx))


def _full_spec(arr):
    nd = arr.ndim
    return pl.BlockSpec(arr.shape, lambda i, nd=nd: (0,) * nd)


def _rowmap(f, rows, params, outs, *, tb, name):
    rows = [_piece(p) for p in rows]
    R = rows[0][0].shape[0]
    tb = min(tb, R)
    nr, npar, nout = len(rows), len(params), len(outs)

    def body(*refs):
        rv = [r[...] for r in refs[:nr]]
        pv = [r[...] for r in refs[nr:nr + npar]]
        res = f(*rv, *pv)
        for o_ref, val in zip(refs[nr + npar:], res):
            o_ref[...] = val.astype(o_ref.dtype)

    return _pcall(
        body, name=name, grid=(R // tb,),
        in_specs=[_row_spec(w, idx, tb) for (_, w, idx) in rows] + [_full_spec(p) for p in params],
        out_specs=[_row_spec(w, 0, tb) for (w, _) in outs],
        out_shape=[jax.ShapeDtypeStruct((R, w), dt) for (w, dt) in outs],
        compiler_params=_cparams(("parallel",)),
    )(*[r[0] for r in rows], *params)


def _rowmap_bwd(f, rows, params, couts, *, diff, out_dtypes, adds=None, tb, name):
    rows = [_piece(p) for p in rows]
    couts = [_piece(p) for p in couts]
    R = rows[0][0].shape[0]
    tb = min(tb, R)
    nr, npar, nc = len(rows), len(params), len(couts)
    didx = [i for i, d in enumerate(diff) if d]
    adds = [None] * len(didx) if adds is None else adds
    add_ops = [_piece(a) for a in adds if a is not None]
    na = len(add_ops)

    def body(*refs):
        rv = [r[...] for r in refs[:nr]]
        pv = [r[...] for r in refs[nr:nr + npar]]
        cv = [r[...] for r in refs[nr + npar:nr + npar + nc]]
        av = [r[...] for r in refs[nr + npar + nc:nr + npar + nc + na]]
        o_refs = refs[nr + npar + nc + na:]
        drow_refs, dpar_refs = o_refs[:len(didx)], o_refs[len(didx):]
        nondiff = {i: rv[i] for i in range(nr) if not diff[i]}

        def g(*dv):
            full = []
            it = iter(dv[:len(didx)])
            for i in range(nr):
                full.append(nondiff[i] if i in nondiff else next(it))
            return f(*full, *dv[len(didx):])

        res, vjp = jax.vjp(g, *[rv[i].astype(F32) for i in didx], *pv)
        cts = tuple(c.astype(r.dtype) for c, r in zip(cv, res))
        grads = vjp(cts)
        ai = 0
        for n, o_ref in enumerate(drow_refs):
            val = grads[n]
            if adds[n] is not None:
                val = val + av[ai].astype(F32)
                ai += 1
            o_ref[...] = val.astype(o_ref.dtype)
        first = pl.program_id(0) == 0
        for n, o_ref in enumerate(dpar_refs):
            gp = grads[len(didx) + n].astype(F32)

            @pl.when(first)
            def _(o_ref=o_ref, gp=gp):
                o_ref[...] = gp

            @pl.when(jnp.logical_not(first))
            def _(o_ref=o_ref, gp=gp):
                o_ref[...] += gp

    outs = _pcall(
        body, name=name, grid=(R // tb,),
        in_specs=([_row_spec(w, idx, tb) for (_, w, idx) in rows] + [_full_spec(p) for p in params]
                  + [_row_spec(w, idx, tb) for (_, w, idx) in couts] + [_row_spec(w, idx, tb) for (_, w, idx) in add_ops]),
        out_specs=[_row_spec(rows[i][1], 0, tb) for i in didx] + [_full_spec(p) for p in params],
        out_shape=([jax.ShapeDtypeStruct((R, rows[i][1]), dt) for i, dt in zip(didx, out_dtypes)]
                   + [jax.ShapeDtypeStruct(p.shape, F32) for p in params]),
        compiler_params=_cparams(("arbitrary",)),
    )(*[r[0] for r in rows], *params, *[c[0] for c in couts], *[a[0] for a in add_ops])
    return outs[:len(didx)], outs[len(didx):]


@functools.partial(jax.custom_vjp, nondiff_argnums=(1,))
def _lane_roll(x, shift):
    return pltpu.roll(x, shift % x.shape[-1], axis=x.ndim - 1)


def _lane_roll_fwd(x, shift):
    return _lane_roll(x, shift), None


def _lane_roll_bwd(shift, _, g):
    return (_lane_roll(g, -shift),)


_lane_roll.defvjp(_lane_roll_fwd, _lane_roll_bwd)


def _group_matrix(kind):
    r = lax.broadcasted_iota(jnp.int32, (LANE, LANE), 0)
    c = lax.broadcasted_iota(jnp.int32, (LANE, LANE), 1)
    if kind == "mla":
        gid = lambda l: jnp.where(l < B_NOPE, 0, jnp.where(l < B_NOPE + B_ROPE, 1, 2))
        inv = jnp.where(c < B_NOPE, 1.0 / B_NOPE, 1.0 / B_ROPE)
        return jnp.where(gid(r) == gid(c), inv, 0.0).astype(BF16)
    return jnp.where(r // kind == c // kind, 1.0 / kind, 0.0).astype(BF16)


@functools.partial(jax.custom_vjp, nondiff_argnums=(1,))
def _group_mean(xx, kind):
    gm = _group_matrix(kind)
    outs = []
    for b in range(xx.shape[-1] // LANE):
        t = xx[:, b * LANE:(b + 1) * LANE]
        hi = t.astype(BF16)
        lo = (t - hi.astype(F32)).astype(BF16)
        outs.append(jnp.dot(hi, gm, preferred_element_type=F32) + jnp.dot(lo, gm, preferred_element_type=F32))
    return jnp.concatenate(outs, axis=1) if len(outs) > 1 else outs[0]


def _group_mean_fwd(xx, kind):
    return _group_mean(xx, kind), None


def _group_mean_bwd(kind, _, g):
    return (_group_mean(g, kind),)


_group_mean.defvjp(_group_mean_fwd, _group_mean_bwd)


def _exact_dot(x, m):
    return jnp.dot(x, m, precision=lax.Precision.HIGHEST, preferred_element_type=F32)


def _head_norm(x, gain_tiled, group):
    return x * lax.rsqrt(_group_mean(x * x, group) + EPS) * gain_tiled


def _row_norm(x, gain):
    ms = jnp.mean(x * x, axis=-1, keepdims=True)
    return x * lax.rsqrt(ms + EPS) * gain


def _rope(x, cos, sin, hd):
    half = hd // 2
    lane = lax.broadcasted_iota(jnp.int32, x.shape, x.ndim - 1) % hd
    other = jnp.where(lane < half, -_lane_roll(x, -half), _lane_roll(x, half))
    return x * cos + other * sin


class _AttnGeom:
    def __init__(self, mode, lq, lk, max_dist=0):
        self.mode, self.lq, self.lk, self.max_dist = mode, lq, lk, max_dist
        if mode == "band":
            self.bq = self.bk = BLOCK
            self.nt_q = 2
            self.nt_k = 2
        elif mode == "causal":
            self.bq = self.bk = min(256, lq)
            self.nt_q = lk // self.bk
            self.nt_k = lq // self.bq
        else:
            self.bq = min(256, lq)
            self.bk = lk
            self.nt_q = 1
            self.nt_k = lq // self.bq
        self.nq, self.nk = lq // self.bq, lk // self.bk

    def kv_block(self, i, t):
        if self.mode == "band":
            return jnp.maximum(i - t, 0)
        if self.mode == "causal":
            return jnp.minimum(t, i)
        return 0 * i

    def kv_active(self, i, t):
        if self.mode == "band":
            return i - t >= 0
        if self.mode == "causal":
            return t <= i
        return None

    def q_block(self, j, t):
        if self.mode == "band":
            return jnp.minimum(j + t, self.nq - 1)
        if self.mode == "causal":
            return jnp.maximum(t, j)
        return t

    def q_active(self, j, t):
        if self.mode == "band":
            return j + t <= self.nq - 1
        if self.mode == "causal":
            return t >= j
        return None

    def mask(self, qb, kb):
        if self.mode == "full":
            return None
        qp = qb * self.bq + lax.broadcasted_iota(jnp.int32, (self.bq, self.bk), 0)
        kp = kb * self.bk + lax.broadcasted_iota(jnp.int32, (self.bq, self.bk), 1)
        d = qp - kp
        if self.mode == "band":
            return (d >= 0) & (d <= self.max_dist)
        return d >= 0


def _when(cond, fn):
    if cond is None:
        fn()
    else:
        pl.when(cond)(fn)


def _dil_view(p, dil):
    arr, w, idx = _piece(p)
    R, C = arr.shape
    assert C % w == 0, (C, w)
    return arr.reshape(R // dil, dil * C), w, idx, C // w


def _seq_spec(view, rows, blk_fn):
    _, w, idx, cpw = view
    return pl.BlockSpec((rows, w), lambda s, i, t: (blk_fn(i, t), s * cpw + idx))


_NT = (((1,), (1,)), ((), ()))
_TN = (((0,), (0,)), ((), ()))


def _scores(geom, scale, q, k, qp, kp, h, g, hd, rope, qb, kb):
    s = lax.dot_general(q[:, h * hd:(h + 1) * hd], k[:, g * hd:(g + 1) * hd], _NT, preferred_element_type=F32)
    if rope:
        s = s + lax.dot_general(qp[:, h * rope:(h + 1) * rope], kp[:, :rope], _NT, preferred_element_type=F32)
    s = s * scale
    m = geom.mask(qb, kb)
    return s, m


def _attn_fwd(geom, q, k, v, *, hq, hk, hd, hdv, scale, dil=1, qp=None, kp=None, rope=0, name):
    qv, kv, vv = _dil_view(q, dil), _dil_view(k, dil), _dil_view(v, dil)
    R = _piece(q)[0].shape[0]
    grp = hq // hk
    bq, bk, nt = geom.bq, geom.bk, geom.nt_q
    ops = [qv, kv, vv]
    specs = [_seq_spec(qv, bq, lambda i, t: i), _seq_spec(kv, bk, geom.kv_block), _seq_spec(vv, bk, geom.kv_block)]
    if rope:
        qpv, kpv = _dil_view(qp, dil), _dil_view(kp, dil)
        ops += [qpv, kpv]
        specs += [_seq_spec(qpv, bq, lambda i, t: i), _seq_spec(kpv, bk, geom.kv_block)]
    ow = hq * hdv
    o_view = (None, ow, 0, 1)
    o_spec = pl.BlockSpec((bq, ow), lambda s, i, t: (i, s))

    def body(*refs):
        if rope:
            q_ref, k_ref, v_ref, qp_ref, kp_ref, o_ref, lse_ref, m_sc, l_sc, acc_sc = refs
        else:
            q_ref, k_ref, v_ref, o_ref, lse_ref, m_sc, l_sc, acc_sc = refs
            qp_ref = kp_ref = None
        i, t = pl.program_id(1), pl.program_id(2)

        @pl.when(t == 0)
        def _():
            m_sc[...] = jnp.full(m_sc.shape, NEG, F32)
            l_sc[...] = jnp.zeros(l_sc.shape, F32)
            acc_sc[...] = jnp.zeros(acc_sc.shape, F32)

        def step():
            qa, ka, va = q_ref[...].astype(BF16), k_ref[...].astype(BF16), v_ref[...].astype(BF16)
            qpa = qp_ref[...].astype(BF16) if rope else None
            kpa = kp_ref[...].astype(BF16) if rope else None
            kb = geom.kv_block(i, t)
            for h in range(hq):
                g = h // grp
                s, msk = _scores(geom, scale, qa, ka, qpa, kpa, h, g, hd, rope, i, kb)
                if msk is not None:
                    s = jnp.where(msk, s, NEG)
                m_old = m_sc[h]
                m_new = jnp.maximum(m_old, jnp.max(s, axis=1, keepdims=True))
                p = jnp.exp(s - m_new)
                alpha = jnp.exp(m_old - m_new)
                l_sc[h] = alpha * l_sc[h] + jnp.sum(p, axis=1, keepdims=True)
                pv = jnp.dot(p.astype(BF16), va[:, g * hdv:(g + 1) * hdv], preferred_element_type=F32)
                acc_sc[h] = alpha * acc_sc[h] + pv
                m_sc[h] = m_new

        _when(geom.kv_active(i, t), step)

        @pl.when(t == nt - 1)
        def _():
            for h in range(hq):
                l = l_sc[h]
                o_ref[:, h * hdv:(h + 1) * hdv] = acc_sc[h] / l
                lse_ref[:, h * hdv:(h + 1) * hdv] = jnp.broadcast_to(m_sc[h] + jnp.log(l), (bq, hdv))

    o, lse = _pcall(
        body, name=name, grid=(dil, geom.nq, nt), in_specs=specs, out_specs=[o_spec, o_spec],
        out_shape=[jax.ShapeDtypeStruct((R // dil, dil * ow), F32)] * 2,
        scratch_shapes=[pltpu.VMEM((hq, bq, 1), F32), pltpu.VMEM((hq, bq, 1), F32), pltpu.VMEM((hq, bq, hdv), F32)],
        compiler_params=_cparams(("parallel", "parallel", "arbitrary")),
    )(*[o_[0] for o_ in ops])
    return o.reshape(R, ow), lse.reshape(R, ow)


def _attn_dq(geom, q, k, v, do, o, lse, dlse, *, hq, hk, hd, hdv, scale, dil=1, qp=None, kp=None, rope=0,
             out_dtype=F32, name):
    qv, kv, vv = _dil_view(q, dil), _dil_view(k, dil), _dil_view(v, dil)
    dov, ov, lv = _dil_view(do, dil), _dil_view(o, dil), _dil_view(lse, dil)
    R = _piece(q)[0].shape[0]
    grp = hq // hk
    bq, bk, nt = geom.bq, geom.bk, geom.nt_q
    qi = lambda i, t: i
    ops = [qv, kv, vv, dov, ov, lv]
    specs = [_seq_spec(qv, bq, qi), _seq_spec(kv, bk, geom.kv_block), _seq_spec(vv, bk, geom.kv_block),
             _seq_spec(dov, bq, qi), _seq_spec(ov, bq, qi), _seq_spec(lv, bq, qi)]
    has_dl = dlse is not None
    if has_dl:
        dlv = _dil_view(dlse, dil)
        ops.append(dlv)
        specs.append(_seq_spec(dlv, bq, qi))
    if rope:
        qpv, kpv = _dil_view(qp, dil), _dil_view(kp, dil)
        ops += [qpv, kpv]
        specs += [_seq_spec(qpv, bq, qi), _seq_spec(kpv, bk, geom.kv_block)]
    qw = hq * hd
    out_specs = [pl.BlockSpec((bq, qw), lambda s, i, t: (i, s))]
    out_shape = [jax.ShapeDtypeStruct((R // dil, dil * qw), out_dtype)]
    scratch = [pltpu.VMEM((hq, bq, 1), F32), pltpu.VMEM((hq, bq, hd), F32)]
    if rope:
        out_specs.append(pl.BlockSpec((bq, hq * rope), lambda s, i, t: (i, s)))
        out_shape.append(jax.ShapeDtypeStruct((R // dil, dil * hq * rope), out_dtype))
        scratch.append(pltpu.VMEM((hq, bq, rope), F32))

    def body(*refs):
        refs = list(refs)
        q_ref, k_ref, v_ref, do_ref, o_ref, l_ref = refs[:6]
        pos = 6
        dl_ref = None
        if has_dl:
            dl_ref = refs[pos]
            pos += 1
        qp_ref = kp_ref = None
        if rope:
            qp_ref, kp_ref = refs[pos:pos + 2]
            pos += 2
        dq_ref = refs[pos]
        pos += 1
        dqp_ref = None
        if rope:
            dqp_ref = refs[pos]
            pos += 1
        dl_sc, dq_sc = refs[pos:pos + 2]
        dqp_sc = refs[pos + 2] if rope else None
        i, t = pl.program_id(1), pl.program_id(2)

        @pl.when(t == 0)
        def _():
            dov_, ov_ = do_ref[...].astype(F32), o_ref[...].astype(F32)
            prod = dov_ * ov_
            for h in range(hq):
                d = jnp.sum(prod[:, h * hdv:(h + 1) * hdv], axis=1, keepdims=True)
                if has_dl:
                    d = d - jnp.sum(dl_ref[:, h * hdv:(h + 1) * hdv].astype(F32), axis=1, keepdims=True)
                dl_sc[h] = d
            dq_sc[...] = jnp.zeros(dq_sc.shape, F32)
            if rope:
                dqp_sc[...] = jnp.zeros(dqp_sc.shape, F32)

        def step():
            qa, ka, va = q_ref[...].astype(BF16), k_ref[...].astype(BF16), v_ref[...].astype(BF16)
            doa = do_ref[...].astype(BF16)
            qpa = qp_ref[...].astype(BF16) if rope else None
            kpa = kp_ref[...].astype(BF16) if rope else None
            kb = geom.kv_block(i, t)
            for h in range(hq):
                g = h // grp
                s, msk = _scores(geom, scale, qa, ka, qpa, kpa, h, g, hd, rope, i, kb)
                p = jnp.exp(s - l_ref[:, h * hdv:h * hdv + 1])
                if msk is not None:
                    p = jnp.where(msk, p, 0.0)
                dp = lax.dot_general(doa[:, h * hdv:(h + 1) * hdv], va[:, g * hdv:(g + 1) * hdv], _NT,
                                     preferred_element_type=F32)
                ds = (p * (dp - dl_sc[h]) * scale).astype(BF16)
                dq_sc[h] += jnp.dot(ds, ka[:, g * hd:(g + 1) * hd], preferred_element_type=F32)
                if rope:
                    dqp_sc[h] += jnp.dot(ds, kpa[:, :rope], preferred_element_type=F32)

        _when(geom.kv_active(i, t), step)

        @pl.when(t == nt - 1)
        def _():
            for h in range(hq):
                dq_ref[:, h * hd:(h + 1) * hd] = dq_sc[h].astype(dq_ref.dtype)
                if rope:
                    dqp_ref[:, h * rope:(h + 1) * rope] = dqp_sc[h].astype(dqp_ref.dtype)

    outs = _pcall(
        body, name=name, grid=(dil, geom.nq, nt), in_specs=specs, out_specs=out_specs, out_shape=out_shape,
        scratch_shapes=scratch, compiler_params=_cparams(("parallel", "parallel", "arbitrary")),
    )(*[o_[0] for o_ in ops])
    dq = outs[0].reshape(R, qw)
    if rope:
        return dq, outs[1].reshape(R, hq * rope)
    return dq


def _attn_dkv(geom, q, k, v, do, o, lse, dlse, *, hq, hk, hd, hdv, scale, dil=1, qp=None, kp=None, rope=0,
              out_dtype=F32, name):
    qv, kv, vv = _dil_view(q, dil), _dil_view(k, dil), _dil_view(v, dil)
    dov, ov, lv = _dil_view(do, dil), _dil_view(o, dil), _dil_view(lse, dil)
    Rk = _piece(k)[0].shape[0]
    grp = hq // hk
    bq, bk, nt = geom.bq, geom.bk, geom.nt_k
    kj = lambda j, t: j
    ops = [qv, kv, vv, dov, ov, lv]
    specs = [_seq_spec(qv, bq, geom.q_block), _seq_spec(kv, bk, kj), _seq_spec(vv, bk, kj),
             _seq_spec(dov, bq, geom.q_block), _seq_spec(ov, bq, geom.q_block), _seq_spec(lv, bq, geom.q_block)]
    has_dl = dlse is not None
    if has_dl:
        dlv = _dil_view(dlse, dil)
        ops.append(dlv)
        specs.append(_seq_spec(dlv, bq, geom.q_block))
    if rope:
        qpv, kpv = _dil_view(qp, dil), _dil_view(kp, dil)
        ops += [qpv, kpv]
        specs += [_seq_spec(qpv, bq, geom.q_block), _seq_spec(kpv, bk, kj)]
    kw, vw = hk * hd, hk * hdv
    out_specs = [pl.BlockSpec((bk, kw), lambda s, j, t: (j, s)), pl.BlockSpec((bk, vw), lambda s, j, t: (j, s))]
    out_shape = [jax.ShapeDtypeStruct((Rk // dil, dil * kw), out_dtype), jax.ShapeDtypeStruct((Rk // dil, dil * vw), out_dtype)]
    scratch = [pltpu.VMEM((hk, bk, hd), F32), pltpu.VMEM((hk, bk, hdv), F32)]
    if rope:
        out_specs.append(pl.BlockSpec((bk, LANE), lambda s, j, t: (j, s)))
        out_shape.append(jax.ShapeDtypeStruct((Rk // dil, dil * LANE), out_dtype))
        scratch.append(pltpu.VMEM((bk, rope), F32))

    def body(*refs):
        refs = list(refs)
        q_ref, k_ref, v_ref, do_ref, o_ref, l_ref = refs[:6]
        pos = 6
        dl_ref = None
        if has_dl:
            dl_ref = refs[pos]
            pos += 1
        qp_ref = kp_ref = None
        if rope:
            qp_ref, kp_ref = refs[pos:pos + 2]
            pos += 2
        dk_ref, dv_ref = refs[pos:pos + 2]
        pos += 2
        dkp_ref = None
        if rope:
            dkp_ref = refs[pos]
            pos += 1
        dk_sc, dv_sc = refs[pos:pos + 2]
        dkp_sc = refs[pos + 2] if rope else None
        j, t = pl.program_id(1), pl.program_id(2)

        @pl.when(t == 0)
        def _():
            dk_sc[...] = jnp.zeros(dk_sc.shape, F32)
            dv_sc[...] = jnp.zeros(dv_sc.shape, F32)
            if rope:
                dkp_sc[...] = jnp.zeros(dkp_sc.shape, F32)

        def step():
            qa, ka, va = q_ref[...].astype(BF16), k_ref[...].astype(BF16), v_ref[...].astype(BF16)
            dof = do_ref[...].astype(F32)
            doa = dof.astype(BF16)
            prod = dof * o_ref[...].astype(F32)
            qpa = qp_ref[...].astype(BF16) if rope else None
            kpa = kp_ref[...].astype(BF16) if rope else None
            qb = geom.q_block(j, t)
            for h in range(hq):
                g = h // grp
                s, msk = _scores(geom, scale, qa, ka, qpa, kpa, h, g, hd, rope, qb, j)
                p = jnp.exp(s - l_ref[:, h * hdv:h * hdv + 1])
                if msk is not None:
                    p = jnp.where(msk, p, 0.0)
                delta = jnp.sum(prod[:, h * hdv:(h + 1) * hdv], axis=1, keepdims=True)
                if has_dl:
                    delta = delta - jnp.sum(dl_ref[:, h * hdv:(h + 1) * hdv].astype(F32), axis=1, keepdims=True)
                do_h = doa[:, h * hdv:(h + 1) * hdv]
                dv_sc[g] += lax.dot_general(p.astype(BF16), do_h, _TN, preferred_element_type=F32)
                dp = lax.dot_general(do_h, va[:, g * hdv:(g + 1) * hdv], _NT, preferred_element_type=F32)
                ds = (p * (dp - delta) * scale).astype(BF16)
                dk_sc[g] += lax.dot_general(ds, qa[:, h * hd:(h + 1) * hd], _TN, preferred_element_type=F32)
                if rope:
                    dkp_sc[...] += lax.dot_general(ds, qpa[:, h * rope:(h + 1) * rope], _TN, preferred_element_type=F32)

        _when(geom.q_active(j, t), step)

        @pl.when(t == nt - 1)
        def _():
            for g in range(hk):
                dk_ref[:, g * hd:(g + 1) * hd] = dk_sc[g].astype(dk_ref.dtype)
                dv_ref[:, g * hdv:(g + 1) * hdv] = dv_sc[g].astype(dv_ref.dtype)
            if rope:
                dkp_ref[...] = jnp.zeros(dkp_ref.shape, dkp_ref.dtype)
                dkp_ref[:, :rope] = dkp_sc[...].astype(dkp_ref.dtype)

    outs = _pcall(
        body, name=name, grid=(dil, geom.nk, nt), in_specs=specs, out_specs=out_specs, out_shape=out_shape,
        scratch_shapes=scratch, compiler_params=_cparams(("parallel", "parallel", "arbitrary")),
    )(*[o_[0] for o_ in ops])
    dk, dv = outs[0].reshape(Rk, kw), outs[1].reshape(Rk, vw)
    if rope:
        return dk, dv, outs[2].reshape(Rk, LANE)
    return dk, dv


class _BandPlan:
    def __init__(self, S, dil, max_dist):
        self.L, self.dil, self.max_dist = S // dil, dil, max_dist
        self.nblk = self.L // BLOCK
        self.nb = min(4, self.nblk)
        self.ns = min(dil, max(1, 4 // self.nb))
        self.grid = (dil // self.ns, self.nblk // self.nb)
        self.rows = self.nb * BLOCK

    def view(self, p):
        arr, w, idx = _piece(p)
        R, C = arr.shape
        assert C % w == 0 and (self.ns == 1 or (C == w and idx == 0)), (C, w, idx, self.ns)
        return arr.reshape(R // self.dil, self.dil * C), w, idx, C // w

    def main(self, view):
        _, w, idx, cpw = view
        if self.ns == 1:
            return pl.BlockSpec((self.rows, w), lambda s, i: (i, s * cpw + idx))
        return pl.BlockSpec((self.rows, self.ns * w), lambda s, i: (i, s))

    def edge(self, view, nxt):
        _, w, idx, cpw = view
        nb, last = self.nb, self.nblk - 1
        rb = (lambda i: jnp.minimum((i + 1) * nb, last)) if nxt else (lambda i: jnp.maximum(i * nb - 1, 0))
        if self.ns == 1:
            return pl.BlockSpec((BLOCK, w), lambda s, i: (rb(i), s * cpw + idx))
        return pl.BlockSpec((BLOCK, self.ns * w), lambda s, i: (rb(i), s))

    def out(self, w):
        return pl.BlockSpec((self.rows, self.ns * w), lambda s, i: (i, s))

    def masks(self):
        qi = lax.broadcasted_iota(jnp.int32, (BLOCK, BLOCK), 0)
        kj = lax.broadcasted_iota(jnp.int32, (BLOCK, BLOCK), 1)
        return kj <= qi, (qi - kj + BLOCK) <= self.max_dist


def _half(e, rows=BLOCK):
    lane = lax.broadcasted_iota(jnp.int32, (rows, LANE), 1)
    return (lane < LANE // 2) if e == 0 else (lane >= LANE // 2)


def _swap_halves(t):
    return pltpu.roll(t, LANE // 2, axis=1)


def _kv_group(ref, rows, col0, pr, grp, hq, hk):
    if hk == hq:
        return ref[rows, col0 + pr * LANE:col0 + (pr + 1) * LANE].astype(BF16)
    g = (2 * pr) // grp
    t = ref[rows, col0 + (g // 2) * LANE:col0 + (g // 2 + 1) * LANE].astype(BF16)
    sw = _swap_halves(t)
    h0 = _half(0, t.shape[0])
    return jnp.where(h0, t, sw) if g % 2 == 0 else jnp.where(h0, sw, t)


def _stack_heads(t2):
    z = jnp.zeros_like(t2)
    h0 = _half(0, t2.shape[0])
    return jnp.concatenate([jnp.where(h0, t2, z), jnp.where(h0, z, t2)], axis=0)


def _unstack_heads(t, rows=BLOCK):
    return jnp.where(_half(0, rows), t[:rows], t[rows:])


def _per_head_col(t2):
    return jnp.concatenate([t2[:, :1], t2[:, LANE // 2:LANE // 2 + 1]], axis=0)


def _per_head_sum(t2):
    h0 = _half(0, t2.shape[0])
    return jnp.concatenate([jnp.sum(jnp.where(h0, t2, 0.0), axis=1, keepdims=True),
                            jnp.sum(jnp.where(h0, 0.0, t2), axis=1, keepdims=True)], axis=0)


def _band_fwd(q, k, v, *, S, dil, max_dist, hq, hk, scale, name):
    hd = 64
    plan = _BandPlan(S, dil, max_dist)
    qv, kv, vv = plan.view(q), plan.view(k), plan.view(v)
    wq, wk = hq * hd, hk * hd
    grp = hq // hk
    ns, nb = plan.ns, plan.nb

    def body(q_ref, k_ref, kp_ref, v_ref, vp_ref, o_ref, l_ref):
        i = pl.program_id(1)
        m_cur, m_band = plan.masks()
        has_prev = i > 0
        for sg in range(ns):
            for b in range(nb):
                rows = slice(b * BLOCK, (b + 1) * BLOCK)
                prows = slice((b - 1) * BLOCK, b * BLOCK) if b > 0 else slice(0, BLOCK)
                m_prev = m_band if b > 0 else (m_band & has_prev)
                msk = jnp.concatenate([m_prev, m_cur], axis=1)
                msk = jnp.concatenate([msk, msk], axis=0)
                for pr in range(wq // LANE):
                    cols = slice(sg * wq + pr * LANE, sg * wq + (pr + 1) * LANE)
                    kcat = jnp.concatenate([_kv_group(k_ref if b > 0 else kp_ref, prows, sg * wk, pr, grp, hq, hk),
                                            _kv_group(k_ref, rows, sg * wk, pr, grp, hq, hk)], axis=0)
                    vcat = jnp.concatenate([_kv_group(v_ref if b > 0 else vp_ref, prows, sg * wk, pr, grp, hq, hk),
                                            _kv_group(v_ref, rows, sg * wk, pr, grp, hq, hk)], axis=0)
                    qs = _stack_heads(q_ref[rows, cols].astype(BF16))
                    s = lax.dot_general(qs, kcat, _NT, preferred_element_type=F32) * scale
                    s = jnp.where(msk, s, NEG)
                    mx = jnp.max(s, axis=1, keepdims=True)
                    p = jnp.exp(s - mx)
                    l = jnp.sum(p, axis=1, keepdims=True)
                    acc = jnp.dot(p.astype(BF16), vcat, preferred_element_type=F32)
                    o_ref[rows, cols] = _unstack_heads(acc / l)
                    l_ref[rows, cols] = _unstack_heads(jnp.broadcast_to(mx + jnp.log(l), (2 * BLOCK, LANE)))

    o, lse = _pcall(
        body, name=name, grid=plan.grid,
        in_specs=[plan.main(qv), plan.main(kv), plan.edge(kv, False), plan.main(vv), plan.edge(vv, False)],
        out_specs=[plan.out(wq)] * 2, out_shape=[jax.ShapeDtypeStruct((S // dil, dil * wq), F32)] * 2,
        compiler_params=_cparams(("parallel", "parallel")),
    )(qv[0], kv[0], kv[0], vv[0], vv[0])
    return o.reshape(S, wq), lse.reshape(S, wq)


def _band_dq(q, k, v, do, o, lse, dlse, *, S, dil, max_dist, hq, hk, scale, out_dtype=F32, name):
    hd = 64
    plan = _BandPlan(S, dil, max_dist)
    qv, kv, vv = plan.view(q), plan.view(k), plan.view(v)
    dov, ov, lv = plan.view(do), plan.view(o), plan.view(lse)
    has_dl = dlse is not None
    wq, wk = hq * hd, hk * hd
    grp = hq // hk
    ns, nb = plan.ns, plan.nb
    ops = [qv, kv, kv, vv, vv, dov, ov, lv]
    specs = [plan.main(qv), plan.main(kv), plan.edge(kv, False), plan.main(vv), plan.edge(vv, False), plan.main(dov),
             plan.main(ov), plan.main(lv)]
    if has_dl:
        dlv = plan.view(dlse)
        ops.append(dlv)
        specs.append(plan.main(dlv))

    def body(*refs):
        q_ref, k_ref, kp_ref, v_ref, vp_ref, do_ref, o_ref, l_ref = refs[:8]
        dl_ref = refs[8] if has_dl else None
        dq_ref = refs[-1]
        i = pl.program_id(1)
        m_cur, m_band = plan.masks()
        has_prev = i > 0
        for sg in range(ns):
            for b in range(nb):
                rows = slice(b * BLOCK, (b + 1) * BLOCK)
                prows = slice((b - 1) * BLOCK, b * BLOCK) if b > 0 else slice(0, BLOCK)
                m_prev = m_band if b > 0 else (m_band & has_prev)
                msk = jnp.concatenate([m_prev, m_cur], axis=1)
                msk = jnp.concatenate([msk, msk], axis=0)
                for pr in range(wq // LANE):
                    cols = slice(sg * wq + pr * LANE, sg * wq + (pr + 1) * LANE)
                    kcat = jnp.concatenate([_kv_group(k_ref if b > 0 else kp_ref, prows, sg * wk, pr, grp, hq, hk),
                                            _kv_group(k_ref, rows, sg * wk, pr, grp, hq, hk)], axis=0)
                    vcat = jnp.concatenate([_kv_group(v_ref if b > 0 else vp_ref, prows, sg * wk, pr, grp, hq, hk),
                                            _kv_group(v_ref, rows, sg * wk, pr, grp, hq, hk)], axis=0)
                    qs = _stack_heads(q_ref[rows, cols].astype(BF16))
                    do2 = do_ref[rows, cols].astype(F32)
                    prod = do2 * o_ref[rows, cols].astype(F32)
                    if has_dl:
                        prod = prod - dl_ref[rows, cols].astype(F32)
                    delta = _per_head_sum(prod)
                    lse_rows = _per_head_col(l_ref[rows, cols])
                    dos = _stack_heads(do2.astype(BF16))
                    s = lax.dot_general(qs, kcat, _NT, preferred_element_type=F32) * scale
                    pm = jnp.where(msk, jnp.exp(s - lse_rows), 0.0)
                    dp = lax.dot_general(dos, vcat, _NT, preferred_element_type=F32)
                    ds = (pm * (dp - delta) * scale).astype(BF16)
                    dq_ref[rows, cols] = _unstack_heads(jnp.dot(ds, kcat, preferred_element_type=F32)).astype(dq_ref.dtype)

    dq = _pcall(
        body, name=name, grid=plan.grid, in_specs=specs, out_specs=plan.out(wq),
        out_shape=jax.ShapeDtypeStruct((S // dil, dil * wq), out_dtype), compiler_params=_cparams(("parallel", "parallel")),
    )(*[o_[0] for o_ in ops])
    return dq.reshape(S, wq)


def _band_dkv(q, k, v, do, o, lse, dlse, *, S, dil, max_dist, hq, hk, scale, out_dtype=F32, name):
    hd = 64
    plan = _BandPlan(S, dil, max_dist)
    qv, kv, vv = plan.view(q), plan.view(k), plan.view(v)
    dov, ov, lv = plan.view(do), plan.view(o), plan.view(lse)
    has_dl = dlse is not None
    wq, wk = hq * hd, hk * hd
    grp = hq // hk
    ns, nb = plan.ns, plan.nb
    qlike = [qv, dov, ov, lv] + ([plan.view(dlse)] if has_dl else [])
    ops = [kv, vv] + qlike + qlike
    specs = [plan.main(kv), plan.main(vv)] + [plan.main(t) for t in qlike] + [plan.edge(t, True) for t in qlike]
    nql = len(qlike)
    nkg = wk // LANE

    def body(*refs):
        k_ref, v_ref = refs[:2]
        mains, edges = refs[2:2 + nql], refs[2 + nql:2 + 2 * nql]
        dk_ref, dv_ref = refs[2 + 2 * nql:]
        i = pl.program_id(1)
        m_cur, m_band = plan.masks()
        has_next = i < plan.grid[1] - 1
        for sg in range(ns):
            for b in range(nb):
                rows = slice(b * BLOCK, (b + 1) * BLOCK)
                nxt_in_main = b + 1 < nb
                nrows = slice((b + 1) * BLOCK, (b + 2) * BLOCK) if nxt_in_main else slice(0, BLOCK)
                nsrc = mains if nxt_in_main else edges
                m_next = m_band if nxt_in_main else (m_band & has_next)
                msk = jnp.concatenate([m_cur, m_cur, m_next, m_next], axis=0)
                nacc = nkg if hk == hq else hk
                dk_acc, dv_acc = [None] * nacc, [None] * nacc
                for pr in range(wq // LANE):
                    cols = slice(sg * wq + pr * LANE, sg * wq + (pr + 1) * LANE)
                    kop = _kv_group(k_ref, rows, sg * wk, pr, grp, hq, hk)
                    vop = _kv_group(v_ref, rows, sg * wk, pr, grp, hq, hk)
                    qs, dos, deltas, lses = [], [], [], []
                    for src, r in ((mains, rows), (nsrc, nrows)):
                        qs.append(_stack_heads(src[0][r, cols].astype(BF16)))
                        do2 = src[1][r, cols].astype(F32)
                        prod = do2 * src[2][r, cols].astype(F32)
                        if has_dl:
                            prod = prod - src[4][r, cols].astype(F32)
                        deltas.append(_per_head_sum(prod))
                        lses.append(_per_head_col(src[3][r, cols]))
                        dos.append(_stack_heads(do2.astype(BF16)))
                    qs4, dos4 = jnp.concatenate(qs, axis=0), jnp.concatenate(dos, axis=0)
                    delta4, lse4 = jnp.concatenate(deltas, axis=0), jnp.concatenate(lses, axis=0)
                    s = lax.dot_general(qs4, kop, _NT, preferred_element_type=F32) * scale
                    pm = jnp.where(msk, jnp.exp(s - lse4), 0.0)
                    dp = lax.dot_general(dos4, vop, _NT, preferred_element_type=F32)
                    ds = (pm * (dp - delta4) * scale).astype(BF16)
                    tv = lax.dot_general(pm.astype(BF16), dos4, _TN, preferred_element_type=F32)
                    tk = lax.dot_general(ds, qs4, _TN, preferred_element_type=F32)
                    ai = pr if hk == hq else (2 * pr) // grp
                    dv_acc[ai] = tv if dv_acc[ai] is None else dv_acc[ai] + tv
                    dk_acc[ai] = tk if dk_acc[ai] is None else dk_acc[ai] + tk
                for kg in range(nkg):
                    cols = slice(sg * wk + kg * LANE, sg * wk + (kg + 1) * LANE)
                    if hk == hq:
                        dkt, dvt = dk_acc[kg], dv_acc[kg]
                    else:
                        both = lambda t: t + _swap_halves(t)
                        h0 = _half(0)
                        dkt = jnp.where(h0, both(dk_acc[2 * kg]), both(dk_acc[2 * kg + 1]))
                        dvt = jnp.where(h0, both(dv_acc[2 * kg]), both(dv_acc[2 * kg + 1]))
                    dk_ref[rows, cols] = dkt.astype(dk_ref.dtype)
                    dv_ref[rows, cols] = dvt.astype(dv_ref.dtype)

    dk, dv = _pcall(
        body, name=name, grid=plan.grid, in_specs=specs, out_specs=[plan.out(wk)] * 2,
        out_shape=[jax.ShapeDtypeStruct((S // dil, dil * wk), out_dtype)] * 2,
        compiler_params=_cparams(("parallel", "parallel")),
    )(*[o_[0] for o_ in ops])
    return dk.reshape(S, wk), dv.reshape(S, wk)


def _causal_block(S):
    return min(512, S)


def _causal_mask(bq):
    qi = lax.broadcasted_iota(jnp.int32, (bq, bq), 0)
    kj = lax.broadcasted_iota(jnp.int32, (bq, bq), 1)
    return kj <= qi


def _half_of(rows, e):
    lane = lax.broadcasted_iota(jnp.int32, (rows, LANE), 1)
    return (lane < LANE // 2) if e == 0 else (lane >= LANE // 2)


def _causal_fwd(q, k, v, *, heads, scale, name):
    S = q.shape[0]
    bq = _causal_block(S)
    nq = S // bq
    npair = heads // 2
    wv = heads * 64

    c2 = scale * math.log2(math.e)

    def body(q_ref, k_ref, v_ref, o_ref, l_ref, m_sc, l_sc, acc_sc):
        i, t = pl.program_id(0), pl.program_id(1)

        @pl.when(t == 0)
        def _():
            m_sc[...] = jnp.full(m_sc.shape, NEG, F32)
            l_sc[...] = jnp.zeros(l_sc.shape, F32)
            acc_sc[...] = jnp.zeros(acc_sc.shape, F32)

        def step(masked):
            msk = _causal_mask(bq) if masked else None
            h0 = _half_of(bq, 0)
            for pr in range(npair):
                v2 = v_ref[:, pr * LANE:(pr + 1) * LANE].astype(BF16)
                new = []
                for e in range(2):
                    h = 2 * pr + e
                    cols = slice(h * LANE, (h + 1) * LANE)
                    s = lax.dot_general(q_ref[:, cols], k_ref[:, cols], _NT, preferred_element_type=F32)
                    if masked:
                        s = jnp.where(msk, s, NEG)
                    m_old = m_sc[h]
                    m_new = jnp.maximum(m_old, jnp.max(s, axis=1, keepdims=True))
                    p = jnp.exp2((s - m_new) * c2)
                    alpha = jnp.exp2((m_old - m_new) * c2)
                    l_sc[h] = alpha * l_sc[h] + jnp.sum(p, axis=1, keepdims=True)
                    m_sc[h] = m_new
                    new.append((alpha, jnp.dot(p.astype(BF16), v2, preferred_element_type=F32)))
                acc = acc_sc[pr]
                acc_sc[pr] = jnp.where(h0, new[0][0] * acc + new[0][1], new[1][0] * acc + new[1][1])

        pl.when(t < i)(lambda: step(False))
        pl.when(t == i)(lambda: step(True))

        @pl.when(t == nq - 1)
        def _():
            h0 = _half_of(bq, 0)
            for pr in range(npair):
                l0, l1 = l_sc[2 * pr], l_sc[2 * pr + 1]
                acc = acc_sc[pr]
                cols = slice(pr * LANE, (pr + 1) * LANE)
                o_ref[:, cols] = jnp.where(h0, acc / l0, acc / l1)
                l_ref[:, cols] = jnp.where(h0, m_sc[2 * pr] * scale + jnp.log(l0), m_sc[2 * pr + 1] * scale + jnp.log(l1))

    qs = pl.BlockSpec((bq, heads * LANE), lambda i, t: (i, 0))
    ks = pl.BlockSpec((bq, heads * LANE), lambda i, t: (jnp.minimum(t, i), 0))
    vs = pl.BlockSpec((bq, wv), lambda i, t: (jnp.minimum(t, i), 0))
    os_ = pl.BlockSpec((bq, wv), lambda i, t: (i, 0))
    return _pcall(
        body, name=name, grid=(nq, nq), in_specs=[qs, ks, vs], out_specs=[os_, os_],
        out_shape=[jax.ShapeDtypeStruct((S, wv), F32)] * 2,
        scratch_shapes=[pltpu.VMEM((heads, bq, 1), F32), pltpu.VMEM((heads, bq, 1), F32), pltpu.VMEM((npair, bq, LANE), F32)],
        compiler_params=_cparams(("parallel", "arbitrary")),
    )(q, k, v)


def _causal_bwd_tile(q_ref, k_ref, v2, do2, prod, l2, h, e, scale, msk, bq):
    cols = slice(h * LANE, (h + 1) * LANE)
    hm = _half_of(bq, e)
    s = lax.dot_general(q_ref[:, cols], k_ref[:, cols], _NT, preferred_element_type=F32)
    c2 = scale * math.log2(math.e)
    p = jnp.exp2((s - l2[:, e * 64:e * 64 + 1] * (1.0 / scale)) * c2)
    if msk is not None:
        p = jnp.where(msk, p, 0.0)
    dom = jnp.where(hm, do2, jnp.zeros_like(do2))
    delta = jnp.sum(jnp.where(hm, prod, 0.0), axis=1, keepdims=True)
    dp = lax.dot_general(dom, v2, _NT, preferred_element_type=F32)
    ds = (p * (dp - delta) * scale).astype(BF16)
    return p, ds, dom


def _causal_dq(q, k, v, do, o, lse, *, heads, scale, out_dtype=BF16, name):
    S = q.shape[0]
    bq = _causal_block(S)
    nq = S // bq
    npair = heads // 2
    wv = heads * 64

    def body(q_ref, k_ref, v_ref, do_ref, o_ref, l_ref, dq_ref, dq_sc):
        i, t = pl.program_id(0), pl.program_id(1)

        @pl.when(t == 0)
        def _():
            dq_sc[...] = jnp.zeros(dq_sc.shape, F32)

        def step(masked):
            msk = _causal_mask(bq) if masked else None
            for pr in range(npair):
                pc = slice(pr * LANE, (pr + 1) * LANE)
                v2 = v_ref[:, pc].astype(BF16)
                dof = do_ref[:, pc].astype(F32)
                prod = dof * o_ref[:, pc]
                do2 = dof.astype(BF16)
                l2 = l_ref[:, pc]
                for e in range(2):
                    h = 2 * pr + e
                    _, ds, _ = _causal_bwd_tile(q_ref, k_ref, v2, do2, prod, l2, h, e, scale, msk, bq)
                    dq_sc[h] += jnp.dot(ds, k_ref[:, h * LANE:(h + 1) * LANE], preferred_element_type=F32)

        pl.when(t < i)(lambda: step(False))
        pl.when(t == i)(lambda: step(True))

        @pl.when(t == nq - 1)
        def _():
            for h in range(heads):
                dq_ref[:, h * LANE:(h + 1) * LANE] = dq_sc[h].astype(dq_ref.dtype)

    qs = pl.BlockSpec((bq, heads * LANE), lambda i, t: (i, 0))
    ks = pl.BlockSpec((bq, heads * LANE), lambda i, t: (jnp.minimum(t, i), 0))
    vs = pl.BlockSpec((bq, wv), lambda i, t: (jnp.minimum(t, i), 0))
    os_ = pl.BlockSpec((bq, wv), lambda i, t: (i, 0))
    return _pcall(
        body, name=name, grid=(nq, nq), in_specs=[qs, ks, vs, os_, os_, os_], out_specs=qs,
        out_shape=jax.ShapeDtypeStruct((S, heads * LANE), out_dtype),
        scratch_shapes=[pltpu.VMEM((heads, bq, LANE), F32)],
        compiler_params=_cparams(("parallel", "arbitrary")),
    )(q, k, v, do, o, lse)


def _causal_dkv(q, k, v, do, o, lse, *, heads, scale, out_dtype=BF16, name):
    S = q.shape[0]
    bq = _causal_block(S)
    nq = S // bq
    npair = heads // 2
    wv = heads * 64

    def body(q_ref, k_ref, v_ref, do_ref, o_ref, l_ref, dk_ref, dv_ref, dk_sc, dv_sc):
        j, t = pl.program_id(0), pl.program_id(1)

        @pl.when(t == 0)
        def _():
            dk_sc[...] = jnp.zeros(dk_sc.shape, F32)
            dv_sc[...] = jnp.zeros(dv_sc.shape, F32)

        def step(masked):
            msk = _causal_mask(bq) if masked else None
            for pr in range(npair):
                pc = slice(pr * LANE, (pr + 1) * LANE)
                v2 = v_ref[:, pc].astype(BF16)
                dof = do_ref[:, pc].astype(F32)
                prod = dof * o_ref[:, pc]
                do2 = dof.astype(BF16)
                l2 = l_ref[:, pc]
                dv_add = None
                for e in range(2):
                    h = 2 * pr + e
                    p, ds, dom = _causal_bwd_tile(q_ref, k_ref, v2, do2, prod, l2, h, e, scale, msk, bq)
                    tv = lax.dot_general(p.astype(BF16), dom, _TN, preferred_element_type=F32)
                    dv_add = tv if dv_add is None else dv_add + tv
                    dk_sc[h] += lax.dot_general(ds, q_ref[:, h * LANE:(h + 1) * LANE], _TN, preferred_element_type=F32)
                dv_sc[pr] += dv_add

        pl.when(t > j)(lambda: step(False))
        pl.when(t == j)(lambda: step(True))

        @pl.when(t == nq - 1)
        def _():
            for h in range(heads):
                dk_ref[:, h * LANE:(h + 1) * LANE] = dk_sc[h].astype(dk_ref.dtype)
            for pr in range(npair):
                dv_ref[:, pr * LANE:(pr + 1) * LANE] = dv_sc[pr].astype(dv_ref.dtype)

    qi = lambda j, t: (jnp.maximum(t, j), 0)
    qs = pl.BlockSpec((bq, heads * LANE), qi)
    os_ = pl.BlockSpec((bq, wv), qi)
    ks = pl.BlockSpec((bq, heads * LANE), lambda j, t: (j, 0))
    vs = pl.BlockSpec((bq, wv), lambda j, t: (j, 0))
    return _pcall(
        body, name=name, grid=(nq, nq), in_specs=[qs, ks, vs, os_, os_, os_], out_specs=[ks, vs],
        out_shape=[jax.ShapeDtypeStruct((S, heads * LANE), out_dtype), jax.ShapeDtypeStruct((S, wv), out_dtype)],
        scratch_shapes=[pltpu.VMEM((heads, bq, LANE), F32), pltpu.VMEM((npair, bq, LANE), F32)],
        compiler_params=_cparams(("parallel", "arbitrary")),
    )(q, k, v, do, o, lse)


@jax.custom_vjp
def _bdot(x, w):
    return jnp.dot(x.astype(BF16), w.astype(BF16), preferred_element_type=F32)


def _bdot_fwd(x, w):
    return _bdot(x, w), (x, w)


def _bdot_bwd(res, g):
    x, w = res
    gb = g.astype(BF16)
    dx = lax.dot_general(gb, w.astype(BF16), _NT, preferred_element_type=F32)
    dw = lax.dot_general(x.astype(BF16), gb, _TN, preferred_element_type=F32)
    return dx.astype(x.dtype), dw.astype(w.dtype)


_bdot.defvjp(_bdot_fwd, _bdot_bwd)


def _tile_matrix(hd, width):
    r = lax.broadcasted_iota(jnp.int32, (hd, width), 0)
    c = lax.broadcasted_iota(jnp.int32, (hd, width), 1) % hd
    return jnp.where(r == c, 1.0, 0.0).astype(F32)


def _spread_matrix(heads, width):
    per = width // heads
    r = lax.broadcasted_iota(jnp.int32, (heads, width), 0)
    c = lax.broadcasted_iota(jnp.int32, (heads, width), 1) // per
    return jnp.where(r == c, 1.0, 0.0).astype(F32)


def _wide(t, width):
    n = width // t.shape[-1]
    return jnp.concatenate([t] * n, axis=1) if n > 1 else t


def _norm_heads(x, gain, hd):
    return _head_norm(x, _exact_dot(gain, _tile_matrix(hd, x.shape[-1])), hd)


def _norm_rope(x, gain, cos, sin, hd):
    w = x.shape[-1]
    return _rope(_norm_heads(x, gain, hd), _wide(cos, w), _wide(sin, w), hd)


def _f_norm(x, g):
    return (_row_norm(x, g),)


def _f_prep_acm(aq, ak, c0q, c0k, c0v, c1q, c1k, c1v, c2q, c2k, c2v, mq, cos, sin, g_aq, g_ak, g0q, g0k, g1q, g1k, g2q, g2k, g_mq):
    outs = [_norm_rope(aq, g_aq, cos, sin, A_HD), _norm_rope(ak, g_ak, cos, sin, A_HD)]
    for cq, ck, cv, gq, gk in ((c0q, c0k, c0v, g0q, g0k), (c1q, c1k, c1v, g1q, g1k), (c2q, c2k, c2v, g2q, g2k)):
        outs += [_norm_rope(cq, gq, cos, sin, C_HD), _norm_rope(ck, gk, cos, sin, C_HD), cv]
    outs.append(_norm_heads(mq, g_mq, M_HD))
    return tuple(outs)


def _rope_mla_q(x, cos, sin):
    lane = lax.broadcasted_iota(jnp.int32, x.shape, x.ndim - 1) % LANE
    half = B_ROPE // 2
    first = (lane >= B_NOPE) & (lane < B_NOPE + half)
    other = jnp.where(first, -_lane_roll(x, -half), _lane_roll(x, half))
    return x * cos + other * sin


def _f_prep_b(ckv, cq, kr, cosq, sinq, cosr, sinr, g_qa, g_kva, w_uq, w_ukv, g_q, g_k, g_kr):
    wq = B_HEADS * LANE
    q_up = _bdot(_row_norm(cq, g_qa), w_uq)
    gq = _exact_dot(g_q, _tile_matrix(LANE, wq))
    qf = _rope_mla_q(_head_norm(q_up, gq, "mla"), _wide(cosq, wq), _wide(sinq, wq))
    kv_up = _bdot(_row_norm(ckv, g_kva), w_ukv)
    kn = _head_norm(kv_up[:, :wq], _exact_dot(g_k, _tile_matrix(LANE, wq)), B_NOPE)
    vb = kv_up[:, wq:]
    kp = _rope(_head_norm(kr, g_kr, B_ROPE), cosr, sinr, B_ROPE)
    kp = _lane_roll(kp, B_NOPE)
    return qf, kn + _wide(kp, wq), vb


def _f_mem_k(k, g):
    return (_norm_heads(k, g, M_HD),)


def _f_sink(o, lse, sink):
    sb = _exact_dot(sink, _spread_matrix(A_HEADS, o.shape[-1]))
    m = jnp.maximum(lse, sb)
    tot = m + jnp.log(jnp.exp(lse - m) + jnp.exp(sb - m))
    return (o * jnp.exp(lse - tot),)


def _f_combine(o0, o1, o2, l0, l1, l2):
    m = jnp.maximum(jnp.maximum(l0, l1), l2)
    e0, e1, e2 = jnp.exp(l0 - m), jnp.exp(l1 - m), jnp.exp(l2 - m)
    inv = 1.0 / (e0 + e1 + e2)
    return ((e0 * o0 + e1 * o1 + e2 * o2) * inv,)


def _f_gatemix(gp, y0, y1, y2, y3, bg):
    d = y0.shape[-1]
    gates = 1.0 / (1.0 + jnp.exp(-(gp + bg)))
    mix = gates[:, :d] * y0
    for n, y in enumerate((y1, y2, y3), start=1):
        mix = mix + gates[:, n * d:(n + 1) * d] * y
    return (mix,)


def _relu2(u):
    return jnp.square(jnp.maximum(u, 0.0))


def _add(r, e):
    return r + e.astype(F32)


def _relu2_grad(r, u):
    return r * (2.0 * jnp.maximum(u, 0.0))


def _loss_and_grad(y, target, *, tb=512):
    R, D = y.shape
    tb = min(tb, R)

    def body(y_ref, t_ref, dy_ref, l_ref):
        err = y_ref[...] - t_ref[...]
        dy_ref[...] = err * (1.0 / D)
        part = 0.5 * jnp.sum(jnp.sum(err * err, axis=1, keepdims=True) * (1.0 / D), axis=0, keepdims=True)
        first = pl.program_id(0) == 0

        @pl.when(first)
        def _():
            l_ref[...] = jnp.broadcast_to(part, l_ref.shape)

        @pl.when(jnp.logical_not(first))
        def _():
            l_ref[...] += jnp.broadcast_to(part, l_ref.shape)

    dy, l = _pcall(
        body, name="loss", grid=(R // tb,),
        in_specs=[pl.BlockSpec((tb, D), lambda i: (i, 0))] * 2,
        out_specs=[pl.BlockSpec((tb, D), lambda i: (i, 0)), pl.BlockSpec((8, LANE), lambda i: (0, 0))],
        out_shape=[jax.ShapeDtypeStruct((R, D), F32), jax.ShapeDtypeStruct((8, LANE), F32)],
        compiler_params=_cparams(("arbitrary",)),
    )(y, target)
    return l[0, 0], dy


def _z_layout(d):
    assert d == 1024, "the aligned layout below is laid out for D_MODEL = 1024"
    lay = {"gates": (4 * d, 0)}
    for g in range(3):
        for n, nm in enumerate("qkv"):
            lay[f"c{g}{nm}"] = (512, 8 + 3 * g + n)
    lay.update(aq=(512, 17), mq=(512, 18), ckv=(256, 38), cq=(384, 26), ak=(128, 81), av=(128, 82), kr=(128, 83))
    return lay, 10752


_KW_A = dict(hq=A_HEADS, hk=A_KV_HEADS, scale=A_HD ** -0.5)
_KW_B = dict(heads=B_HEADS, scale=(B_NOPE + B_ROPE) ** -0.5)
_KW_C = dict(hq=C_HEADS, hk=C_HEADS, scale=C_HD ** -0.5)
_KW_M = dict(hq=M_HEADS, hk=M_HEADS, hd=M_HD, hdv=M_HD, scale=M_HD ** -0.5)


def _layer_fwd(l, x, mem, w, tabs):
    S, D = x.shape
    lay, _ = _z_layout(D)
    cosA, sinA, cosB, sinB, cosQ, sinQ = tabs
    gM = _AttnGeom("full", S, mem.shape[0])
    nm = lambda s: f"l{l}_{s}"
    sv = {}
    hn = _rowmap(_f_norm, [x], [w["g_mix"]], [(D, BF16)], tb=512, name=nm("norm1"))[0]
    z = _mm(hn, w["in"], name=nm("in"), tn=1536)
    zp = {k: (z, wd, idx) for k, (wd, idx) in lay.items()}
    acm_rows = [zp[k] for k in ("aq", "ak", "c0q", "c0k", "c0v", "c1q", "c1k", "c1v", "c2q", "c2k", "c2v", "mq")] + [cosA, sinA]
    acm_par = [w["a_qn"], w["a_kn"], w["c0q"], w["c0k"], w["c1q"], w["c1k"], w["c2q"], w["c2k"], w["m_qn"]]
    acm = _rowmap(_f_prep_acm, acm_rows, acm_par, [(p[1], BF16) for p in acm_rows[:12]], tb=256, name=nm("prep_acm"))
    qa, ka, qc0, kc0, vc0, qc1, kc1, vc1, qc2, kc2, vc2, mq = acm
    b_rows = [zp["ckv"], zp["cq"], zp["kr"], cosQ, sinQ, cosB, sinB]
    b_par = [w["b_qa"], w["b_kva"], w["uq"], w["ukv"], w["b_q"], w["b_k"], w["b_kr"]]
    qb, kb, vb = _rowmap(_f_prep_b, b_rows, b_par, [(B_HEADS * LANE, BF16), (B_HEADS * LANE, BF16), (512, BF16)], tb=256,
                         name=nm("prep_b"))
    memn = _rowmap(_f_norm, [mem], [w["m_g_mem"]], [(D, BF16)], tb=256, name=nm("mem_norm"))[0]
    mkv = _mm(memn, w["mkv"], name=nm("mem_kv"))
    mk = _rowmap(_f_mem_k, [(mkv, 512, 0)], [w["m_kn"]], [(512, BF16)], tb=256, name=nm("mem_k"))[0]
    mv = (mkv, 512, 1)

    oa_raw, lse_a = _band_fwd(qa, ka, zp["av"], S=S, dil=1, max_dist=A_WINDOW - 1, name=nm("attn_a"), **_KW_A)
    o_a = _rowmap(_f_sink, [oa_raw, lse_a], [w["a_sink"]], [(512, BF16)], tb=512, name=nm("sink"))[0]
    o_b, lse_b = _causal_fwd(qb, kb, vb, name=nm("attn_b"), **_KW_B)
    oc, lc = [], []
    for g, ((win, dil), qc, kc, vc) in enumerate(zip(C_PATTERNS, (qc0, qc1, qc2), (kc0, kc1, kc2), (vc0, vc1, vc2))):
        o_g, l_g = _band_fwd(qc, kc, vc, S=S, dil=dil, max_dist=win // dil, name=nm(f"attn_c{g}"), **_KW_C)
        oc.append(o_g)
        lc.append(l_g)
    o_c = _rowmap(_f_combine, oc + lc, [], [(512, BF16)], tb=512, name=nm("combine"))[0]
    o_m, lse_m = _attn_fwd(gM, mq, mk, mv, name=nm("attn_m"), **_KW_M)

    o_n = [o_a, o_b, o_c, o_m]
    ys = [_mm(o_n[n], w["branch"][n], name=nm(f"branch{n}")) for n in range(N_BRANCH)]
    mix = _rowmap(_f_gatemix, [zp["gates"]] + ys, [w["b_gate"]], [(D, BF16)], tb=256, name=nm("gatemix"))[0]
    x1 = _mm(mix, w["out"], extra=x, epi=_add, name=nm("out"))
    hn2 = _rowmap(_f_norm, [x1], [w["g_mlp"]], [(D, BF16)], tb=512, name=nm("norm2"))[0]
    u = _mm(hn2, w["up"], name=nm("up"))
    x2 = _mm(u, w["down"], pro_a=_relu2, extra=x1, epi=_add, name=nm("down"))
    sv.update(x=x, hn=hn, z=z, acm=acm, bq=(qb, kb, vb), memn=memn, mkv=mkv, mk=mk, oa_raw=oa_raw, lse_a=lse_a,
              o_b=o_b, lse_b=lse_b, oc=oc, lc=lc, o_m=o_m, lse_m=lse_m, o_n=o_n, ys=ys, mix=mix, x1=x1, hn2=hn2, u=u)
    return x2, sv


def _layer_bwd(l, dx2, mem, w, tabs, sv):
    x, z, x1, u = sv["x"], sv["z"], sv["x1"], sv["u"]
    S, D = x.shape
    lay, zw = _z_layout(D)
    cosA, sinA, cosB, sinB, cosQ, sinQ = tabs
    gM = _AttnGeom("full", S, mem.shape[0])
    nm = lambda s: f"l{l}_{s}"
    zp = {k: (z, wd, idx) for k, (wd, idx) in lay.items()}
    g = {}
    du = _mm(dx2, w["down"], tb=True, extra=u, epi=_relu2_grad, out_dtype=BF16, name=nm("d_down_x"))
    g["down"] = _mm(u, dx2, ta=True, pro_a=_relu2, name=nm("d_down_w"))
    dhn2 = _mm(du, w["up"], tb=True, name=nm("d_up_x"))
    g["up"] = _mm(sv["hn2"], du, ta=True, name=nm("d_up_w"))
    (dx1,), (g["g_mlp"],) = _rowmap_bwd(_f_norm, [x1], [w["g_mlp"]], [dhn2], diff=[True], out_dtypes=[F32], adds=[dx2],
                                        tb=256, name=nm("d_norm2"))
    dmix = _mm(dx1, w["out"], tb=True, name=nm("d_out_x"))
    g["out"] = _mm(sv["mix"], dx1, ta=True, name=nm("d_out_w"))
    (dgates, dy0, dy1, dy2, dy3), (g["b_gate"],) = _rowmap_bwd(
        _f_gatemix, [zp["gates"]] + sv["ys"], [w["b_gate"]], [dmix], diff=[True] * 5, out_dtypes=[BF16] * 5,
        tb=128, name=nm("d_gatemix"))
    dys = [dy0, dy1, dy2, dy3]
    do = [_mm(dys[n], w["branch"][n], tb=True, name=nm(f"d_branch{n}_x")) for n in range(N_BRANCH)]
    g["branch"] = [_mm(sv["o_n"][n], dys[n], ta=True, name=nm(f"d_branch{n}_w")) for n in range(N_BRANCH)]
    qa, ka, qc0, kc0, vc0, qc1, kc1, vc1, qc2, kc2, vc2, mq = sv["acm"]
    qb, kb, vb = sv["bq"]
    mkv, mk = sv["mkv"], sv["mk"]
    mv = (mkv, 512, 1)
    dmq = _attn_dq(gM, mq, mk, mv, do[3], sv["o_m"], sv["lse_m"], None, name=nm("attn_m_dq"), **_KW_M)
    dmk, dmv = _attn_dkv(gM, mq, mk, mv, do[3], sv["o_m"], sv["lse_m"], None, name=nm("attn_m_dkv"), **_KW_M)
    (doc0, doc1, doc2, dl0, dl1, dl2), _ = _rowmap_bwd(_f_combine, sv["oc"] + sv["lc"], [], [do[2]], diff=[True] * 6,
                                                      out_dtypes=[F32] * 6, tb=256, name=nm("d_combine"))
    dqc, dkc, dvc = [], [], []
    for gi, ((win, dil), qc, kc, vc, doc, dl) in enumerate(zip(C_PATTERNS, (qc0, qc1, qc2), (kc0, kc1, kc2), (vc0, vc1, vc2),
                                                              (doc0, doc1, doc2), (dl0, dl1, dl2))):
        args = (qc, kc, vc, doc, sv["oc"][gi], sv["lc"][gi], dl)
        kwc = dict(S=S, dil=dil, max_dist=win // dil, out_dtype=BF16, **_KW_C)
        dqc.append(_band_dq(*args, name=nm(f"attn_c{gi}_dq"), **kwc))
        dk_, dv_ = _band_dkv(*args, name=nm(f"attn_c{gi}_dkv"), **kwc)
        dkc.append(dk_)
        dvc.append(dv_)
    argsb = (qb, kb, vb, do[1], sv["o_b"], sv["lse_b"])
    dqb = _causal_dq(*argsb, name=nm("attn_b_dq"), **_KW_B)
    dkb, dvb = _causal_dkv(*argsb, name=nm("attn_b_dkv"), **_KW_B)
    (doa_raw, dlse_a), (g["a_sink"],) = _rowmap_bwd(_f_sink, [sv["oa_raw"], sv["lse_a"]], [w["a_sink"]], [do[0]],
                                                   diff=[True, True], out_dtypes=[F32, F32], tb=256, name=nm("d_sink"))
    argsa = (qa, ka, zp["av"], doa_raw, sv["oa_raw"], sv["lse_a"], dlse_a)
    kwa = dict(S=S, dil=1, max_dist=A_WINDOW - 1, out_dtype=BF16, **_KW_A)
    dqa = _band_dq(*argsa, name=nm("attn_a_dq"), **kwa)
    dka, dva = _band_dkv(*argsa, name=nm("attn_a_dkv"), **kwa)
    acm_rows = [zp[k] for k in ("aq", "ak", "c0q", "c0k", "c0v", "c1q", "c1k", "c1v", "c2q", "c2k", "c2v", "mq")] + [cosA, sinA]
    acm_par = [w["a_qn"], w["a_kn"], w["c0q"], w["c0k"], w["c1q"], w["c1k"], w["c2q"], w["c2k"], w["m_qn"]]
    acm_ct = [dqa, dka, dqc[0], dkc[0], dvc[0], dqc[1], dkc[1], dvc[1], dqc[2], dkc[2], dvc[2], dmq]
    dacm, (g["a_qn"], g["a_kn"], g["c0q"], g["c0k"], g["c1q"], g["c1k"], g["c2q"], g["c2k"], g["m_qn"]) = _rowmap_bwd(
        _f_prep_acm, acm_rows, acm_par, acm_ct, diff=[True] * 12 + [False, False], out_dtypes=[BF16] * 12, tb=256,
        name=nm("d_prep_acm"))
    d_aq, d_ak, d_c0q, d_c0k, d_c0v, d_c1q, d_c1k, d_c1v, d_c2q, d_c2k, d_c2v, d_mq = dacm
    b_rows = [zp["ckv"], zp["cq"], zp["kr"], cosQ, sinQ, cosB, sinB]
    b_par = [w["b_qa"], w["b_kva"], w["uq"], w["ukv"], w["b_q"], w["b_k"], w["b_kr"]]
    (d_ckv, d_cq, d_kr), gb = _rowmap_bwd(_f_prep_b, b_rows, b_par, [dqb, dkb, dvb], diff=[True] * 3 + [False] * 4,
                                          out_dtypes=[BF16] * 3, tb=256, name=nm("d_prep_b"))
    g["b_qa"], g["b_kva"], g["uq"], g["ukv"], g["b_q"], g["b_k"], g["b_kr"] = gb
    (dmkv_k,), (g["m_kn"],) = _rowmap_bwd(_f_mem_k, [(mkv, 512, 0)], [w["m_kn"]], [dmk], diff=[True], out_dtypes=[F32],
                                          tb=256, name=nm("d_mem_k"))
    dmkv = jnp.concatenate([dmkv_k, dmv], axis=1)
    dmemn = _mm(dmkv, w["mkv"], tb=True, name=nm("d_mem_kv_x"))
    g["mkv"] = _mm(sv["memn"], dmkv, ta=True, name=nm("d_mem_kv_w"))
    _, (g["m_g_mem"],) = _rowmap_bwd(_f_norm, [mem], [w["m_g_mem"]], [dmemn], diff=[True], out_dtypes=[F32], tb=256,
                                     name=nm("d_mem_norm"))
    dz = jnp.concatenate([dgates, d_c0q, d_c0k, d_c0v, d_c1q, d_c1k, d_c1v, d_c2q, d_c2k, d_c2v, d_aq, d_mq, d_ckv, d_cq,
                          d_ak, dva, d_kr], axis=1)
    assert dz.shape[1] == zw
    dhn = _mm(dz, w["in"], tb=True, name=nm("d_in_x"), tk=1536)
    g["in"] = _mm(sv["hn"], dz, ta=True, name=nm("d_in_w"), tn=1536)
    (dx,), (g["g_mix"],) = _rowmap_bwd(_f_norm, [x], [w["g_mix"]], [dhn], diff=[True], out_dtypes=[F32], adds=[dx1],
                                       tb=256, name=nm("d_norm1"))
    return dx, g


_IN_ORIG = dict(aq=(0, 512), ak=(512, 640), av=(640, 768), cq=(768, 1152), ckv=(1152, 1408), kr=(1408, 1440),
                c=(1440, 6048), mq=(6048, 6560), gates=(6560, 10656))
_IN_OURS = dict(gates=(0, 4096), c=(4096, 8704), aq=(8704, 9216), mq=(9216, 9728), ckv=(9728, 9984), cq=(9984, 10368),
                ak=(10368, 10496), av=(10496, 10624), kr=(10624, 10656))
_IN_ORDER_ORIG = ("aq", "ak", "av", "cq", "ckv", "kr", "c", "mq", "gates")


def _in_to_ours(w_in):
    pc = {k: w_in[..., a:b] for k, (a, b) in _IN_ORIG.items()}
    zeros = lambda n: jnp.zeros(w_in.shape[:-1] + (n,), w_in.dtype)
    return jnp.concatenate([pc["gates"], pc["c"], pc["aq"], pc["mq"], pc["ckv"], pc["cq"], pc["ak"], pc["av"], pc["kr"],
                            zeros(96)], axis=-1)


def _in_from_ours(g_in):
    return jnp.concatenate([g_in[..., _IN_OURS[k][0]:_IN_OURS[k][1]] for k in _IN_ORDER_ORIG], axis=-1)


def _uq_to_ours(w):
    per = B_NOPE + B_ROPE
    w4 = w.reshape(w.shape[:-1] + (B_HEADS, per))
    w4 = jnp.pad(w4, [(0, 0)] * (w4.ndim - 1) + [(0, LANE - per)])
    return w4.reshape(w.shape[:-1] + (B_HEADS * LANE,))


def _uq_from_ours(g):
    per = B_NOPE + B_ROPE
    g4 = g.reshape(g.shape[:-1] + (B_HEADS, LANE))[..., :per]
    return g4.reshape(g.shape[:-1] + (B_HEADS * per,))


def _ukv_to_ours(w):
    w4 = w.reshape(w.shape[:-1] + (B_HEADS, B_NOPE + B_V))
    keys = jnp.pad(w4[..., :B_NOPE], [(0, 0)] * (w4.ndim - 1) + [(0, LANE - B_NOPE)])
    vals = w4[..., B_NOPE:]
    return jnp.concatenate([keys.reshape(w.shape[:-1] + (B_HEADS * LANE,)), vals.reshape(w.shape[:-1] + (B_HEADS * B_V,))],
                           axis=-1)


def _ukv_from_ours(g):
    wq = B_HEADS * LANE
    keys = g[..., :wq].reshape(g.shape[:-1] + (B_HEADS, LANE))[..., :B_NOPE]
    vals = g[..., wq:].reshape(g.shape[:-1] + (B_HEADS, B_V))
    return jnp.concatenate([keys, vals], axis=-1).reshape(g.shape[:-1] + (B_HEADS * (B_NOPE + B_V),))


def _layer_weights(big_l, small, l):
    row = lambda a: a[l].reshape(1, -1)
    w = dict(big_l)
    w.update(g_mix=row(small["g_mix"]), b_gate=row(small["b_gate"]), a_qn=row(small["a_qn"]), a_kn=row(small["a_kn"]),
             a_sink=row(small["a_sink"]), b_qa=row(small["b_qa_norm"]), b_kva=row(small["b_kva_norm"]),
             b_q=jnp.pad(small["b_qn"][l], (0, LANE - B_NOPE - B_ROPE)).reshape(1, -1),
             b_k=jnp.pad(small["b_kn"][l, :B_NOPE], (0, LANE - B_NOPE)).reshape(1, -1),
             b_kr=jnp.pad(small["b_kn"][l, B_NOPE:], (0, LANE - B_ROPE)).reshape(1, -1),
             m_g_mem=row(small["m_g_mem"]), m_qn=row(small["m_qn"]), m_kn=row(small["m_kn"]), g_mlp=row(small["g_mlp"]))
    for g in range(3):
        w[f"c{g}q"] = small["c_qn"][l, g].reshape(1, -1)
        w[f"c{g}k"] = small["c_kn"][l, g].reshape(1, -1)
    return w


def _rope_tables(positions):
    pos = positions.astype(F32)[:, None]
    tabs = []
    for dim in (A_HD, B_ROPE):
        inv = ROPE_THETA ** (-jnp.arange(0, dim, 2, dtype=F32) / dim)
        ang = pos * inv
        reps = LANE // (dim // 2)
        tabs += [jnp.tile(jnp.cos(ang), (1, reps)), jnp.tile(jnp.sin(ang), (1, reps))]
    half = B_ROPE // 2
    cb, sb = tabs[2][:, :half], tabs[3][:, :half]
    ones, zeros = jnp.ones((pos.shape[0], B_NOPE), F32), jnp.zeros((pos.shape[0], B_NOPE), F32)
    pad = LANE - B_NOPE - B_ROPE
    tabs.append(jnp.concatenate([ones, cb, cb, ones[:, :pad]], axis=1))
    tabs.append(jnp.concatenate([zeros, sb, sb, zeros[:, :pad]], axis=1))
    return tuple(tabs)


def _local_step(x, mem, positions, small, loss_target, get_big, put_grads):
    depth = small["g_mix"].shape[0]
    tabs = _rope_tables(positions)
    ws, saved = [], []
    h = x
    for l in range(depth):
        ws.append(_layer_weights(get_big(l, h), small, l))
        h, sv = _layer_fwd(l, h, mem, ws[l], tabs)
        saved.append(sv)
    loss, dh = _loss_and_grad(h, loss_target)
    small_grads = [None] * depth
    for l in reversed(range(depth)):
        dh, g = _layer_bwd(l, dh, mem, ws[l], tabs, saved[l])
        zero = put_grads(l, g)
        if l > 0:
            ws[l - 1] = dict(ws[l - 1], g_mlp=ws[l - 1]["g_mlp"] + zero)
        small_grads[l] = g
    return loss, dh, small_grads


def _small_grads_to_reference_layout(grads):
    flat = lambda k: jnp.stack([g[k].reshape(-1) for g in grads])
    return dict(
        g_mix=flat("g_mix"), b_gate=flat("b_gate"), a_qn=flat("a_qn"), a_kn=flat("a_kn"), a_sink=flat("a_sink"),
        b_qa_norm=flat("b_qa"), b_kva_norm=flat("b_kva"), b_qn=flat("b_q")[:, :B_NOPE + B_ROPE],
        b_kn=jnp.concatenate([flat("b_k")[:, :B_NOPE], flat("b_kr")[:, :B_ROPE]], axis=1),
        c_qn=jnp.stack([jnp.stack([g[f"c{i}q"].reshape(-1) for i in range(3)]) for g in grads]),
        c_kn=jnp.stack([jnp.stack([g[f"c{i}k"].reshape(-1) for i in range(3)]) for g in grads]),
        m_g_mem=flat("m_g_mem"), m_qn=flat("m_qn"), m_kn=flat("m_kn"), g_mlp=flat("g_mlp"))


def _big_grads_to_reference_layout(g):
    return dict(w_in=_in_from_ours(g["in"]), b_w_uq=_uq_from_ours(g["uq"]), b_w_ukv=_ukv_from_ours(g["ukv"]), m_w_kv=g["mkv"],
                w_branch=jnp.stack(g["branch"]), w_out=g["out"], w_up=g["up"], w_down=g["down"])


def _big_to_kernel_layout(full):
    c = lambda a: a.astype(BF16)
    return {"in": c(_in_to_ours(full["w_in"])), "uq": c(_uq_to_ours(full["b_w_uq"])),
            "ukv": c(_ukv_to_ours(full["b_w_ukv"])), "mkv": c(full["m_w_kv"]),
            "branch": c(full["w_branch"]), "out": c(full["w_out"]), "up": c(full["w_up"]), "down": c(full["w_down"])}


MESH = pl.DeviceIdType.MESH
N_CHIPS = 4
N_DEV = 8
_ANY = pl.BlockSpec(memory_space=pl.ANY)


_HBM = pl.BlockSpec(memory_space=pltpu.HBM)
_SEM = pl.BlockSpec(memory_space=pltpu.SEMAPHORE)
_EFFECT = pltpu.SideEffectType.DATAFLOW_SIDE_EFFECTING


def _chip_peers():
    x, y, c = lax.axis_index("x"), lax.axis_index("y"), lax.axis_index("c")
    return 2 * x + y, [((1 - x, y, c), 2 * (1 - x) + y), ((x, 1 - y, c), 2 * x + 1 - y), ((1 - x, 1 - y, c), 2 * (1 - x) + 1 - y)]


def _exchange_start(srcs, lands, *, gather, name):
    n = len(srcs)

    def body(*refs):
        ins, land = refs[:n], refs[n:2 * n]
        send_sems, recv_sems = refs[2 * n], refs[2 * n + 1]
        token = refs[-1]
        me, peers = _chip_peers()
        for a in range(n):
            for j, (dev, chip) in enumerate(peers):
                src = ins[a] if gather else ins[a].at[chip]
                pltpu.make_async_remote_copy(src_ref=src, dst_ref=land[a].at[me], send_sem=send_sems.at[3 * a + j],
                                             recv_sem=recv_sems.at[3 * a + j], device_id=dev, device_id_type=MESH).start()
        token[...] = jnp.zeros(token.shape, token.dtype)

    hbm = lambda a: pltpu.HBM(a.shape, a.dtype)
    outs = _pcall(
        body, name=name,
        out_shape=(pltpu.SemaphoreType.DMA((3 * n,)), pltpu.SemaphoreType.DMA((3 * n,)), *[hbm(a) for a in srcs],
                   *[hbm(a) for a in lands], jax.ShapeDtypeStruct((8, LANE), F32)),
        in_specs=[_HBM] * (2 * n), out_specs=(_SEM, _SEM, *([_HBM] * (2 * n)), pl.BlockSpec(memory_space=pltpu.VMEM)),
        input_output_aliases={a: 2 + a for a in range(2 * n)},
        compiler_params=pltpu.CompilerParams(has_side_effects=_EFFECT),
    )(*[pltpu.with_memory_space_constraint(a, pltpu.HBM) for a in srcs],
      *[pltpu.with_memory_space_constraint(a, pltpu.HBM) for a in lands])
    return outs[0], outs[1], list(outs[2:2 + n]), list(outs[2 + n:2 + 2 * n]), outs[-1]


def _exchange_wait(state, after, *, gather, name):
    send_sems, recv_sems, srcs, lands, _ = state
    n = len(lands)

    def body(*refs):
        src_refs, land = refs[:n], refs[n:2 * n]
        ssem, rsem = refs[2 * n], refs[2 * n + 1]
        me, peers = _chip_peers()
        for a in range(n):
            for j, (dev, chip) in enumerate(peers):
                src = src_refs[a] if gather else src_refs[a].at[chip]
                cp = pltpu.make_async_remote_copy(src_ref=src, dst_ref=land[a].at[chip], send_sem=ssem.at[3 * a + j],
                                                  recv_sem=rsem.at[3 * a + j], device_id=dev, device_id_type=MESH)
                cp.wait_send()
                cp.wait_recv()

    outs = _pcall(
        body, name=name,
        out_shape=tuple(pltpu.HBM(a.shape, a.dtype) for a in list(srcs) + list(lands)),
        in_specs=[_HBM] * (2 * n) + [_SEM, _SEM, pl.BlockSpec(memory_space=pl.ANY)],
        out_specs=tuple([_HBM] * (2 * n)), input_output_aliases={a: a for a in range(2 * n)},
        compiler_params=pltpu.CompilerParams(has_side_effects=_EFFECT),
    )(*srcs, *lands, send_sems, recv_sems, after)
    return list(outs[:n]), list(outs[n:])


def _sibling_exchange(arrays, *, name):
    n = len(arrays)

    def body(*refs):
        ins, outs = refs[:n], refs[n:2 * n]
        send_sems, recv_sems = refs[2 * n:]
        x, y, c = lax.axis_index("x"), lax.axis_index("y"), lax.axis_index("c")
        cps = []
        for a in range(n):
            cp = pltpu.make_async_remote_copy(src_ref=ins[a], dst_ref=outs[a], send_sem=send_sems.at[a], recv_sem=recv_sems.at[a],
                                              device_id=(x, y, 1 - c), device_id_type=MESH)
            cp.start()
            cps.append(cp)
        for cp in cps:
            cp.wait()

    return _pcall(
        body, name=name, in_specs=[_ANY] * n, out_specs=[_ANY] * n,
        out_shape=[jax.ShapeDtypeStruct(a.shape, a.dtype) for a in arrays],
        scratch_shapes=[pltpu.SemaphoreType.DMA((n,)), pltpu.SemaphoreType.DMA((n,))],
        compiler_params=pltpu.CompilerParams(has_side_effects=True),
    )(*arrays)


def _allreduce_small(v, *, name):
    rows = v.shape[0]

    def body(v_ref, o_ref, slots, send_sems, recv_sems):
        x, y, c = lax.axis_index("x"), lax.axis_index("y"), lax.axis_index("c")
        me = 4 * x + 2 * y + c
        slots[me] = v_ref[...]
        cps = []
        for k in range(1, N_DEV):
            fx, fy, fc = (k >> 2) & 1, (k >> 1) & 1, k & 1
            peer = (x ^ fx, y ^ fy, c ^ fc)
            cp = pltpu.make_async_remote_copy(src_ref=v_ref, dst_ref=slots.at[me], send_sem=send_sems.at[k - 1],
                                              recv_sem=recv_sems.at[k - 1], device_id=peer, device_id_type=MESH)
            cp.start()
            cps.append((cp, peer))
        for k, (cp, (px, py, pc)) in enumerate(cps):
            pltpu.make_async_remote_copy(src_ref=v_ref, dst_ref=slots.at[4 * px + 2 * py + pc], send_sem=send_sems.at[k],
                                         recv_sem=recv_sems.at[k], device_id=(px, py, pc), device_id_type=MESH).wait_recv()
        for cp, _ in cps:
            cp.wait_send()
        tot = slots[0]
        for d in range(1, N_DEV):
            tot = tot + slots[d]
        o_ref[...] = tot

    vm = pl.BlockSpec(memory_space=pltpu.VMEM)
    return _pcall(
        body, name=name, in_specs=[vm], out_specs=vm, out_shape=jax.ShapeDtypeStruct(v.shape, F32),
        scratch_shapes=[pltpu.VMEM((N_DEV, rows, LANE), F32), pltpu.SemaphoreType.DMA((N_DEV - 1,)),
                        pltpu.SemaphoreType.DMA((N_DEV - 1,))],
        compiler_params=pltpu.CompilerParams(has_side_effects=True),
    )(v)


def _rows_block(rows, cols, itemsize=4, target_bytes=1 << 20):
    want = max(16, target_bytes // max(1, cols * itemsize))
    best = rows
    for t in range(16, rows, 16):
        if rows % t == 0 and t <= want:
            best = t
    return best if best <= want or rows <= want else rows


def _sum_slots(recvs, parts, me, *, name):
    nl = len(recvs)
    shp = recvs[0].shape[1:]
    r3 = [r.reshape(N_CHIPS, -1, shp[-1]) for r in recvs]
    p3 = [q.reshape(N_CHIPS, -1, shp[-1]) for q in parts]
    rows, cols = r3[0].shape[1:]
    tb = _rows_block(rows, cols)
    nblk = rows // tb
    per = N_CHIPS + 1

    def body(me_ref, *refs):
        o_ref = refs[-1]
        lg = pl.program_id(0)
        for l in range(nl):
            r = refs[per * l:per * (l + 1)]

            @pl.when(lg == l)
            def _(r=r):
                tot = ((r[0][...].astype(F32) + r[1][...].astype(F32)) + r[2][...].astype(F32)) + r[3][...].astype(F32)
                o_ref[...] = tot + r[4][...].astype(F32)

    def row(l, lg, i):
        return jnp.where(lg < l, 0, jnp.where(lg > l, nblk - 1, i))

    in_specs, args = [], []
    for l in range(nl):
        for k in range(N_CHIPS):
            in_specs.append(pl.BlockSpec((None, tb, cols), lambda lg, i, me_ref, l=l, k=k: (k, row(l, lg, i), 0)))
            args.append(r3[l])
        in_specs.append(pl.BlockSpec((None, tb, cols), lambda lg, i, me_ref, l=l: (me_ref[0], row(l, lg, i), 0)))
        args.append(p3[l])
    out = _pcall(
        body, name=name,
        grid_spec=pltpu.PrefetchScalarGridSpec(
            num_scalar_prefetch=1, grid=(nl, nblk), in_specs=in_specs,
            out_specs=pl.BlockSpec((None, tb, cols), lambda lg, i, me_ref: (lg, i, 0))),
        out_shape=jax.ShapeDtypeStruct((nl, rows, cols), F32), compiler_params=_cparams(("arbitrary", "arbitrary")),
    )(me, *args)
    return out.reshape((nl,) + shp)


def _adamw(w, g_parts, m, v, *, layer0=0, prev=None, name):
    shp = w.shape
    two = lambda a: a.reshape(-1, shp[-1])
    rows, cols = two(w).shape
    grows = two(g_parts[0]).shape[0]
    per_layer = rows // shp[0] if layer0 or prev is not None or grows != rows else rows
    tb = _rows_block(per_layer, cols, target_bytes=1 << 19)
    off = (layer0 * per_layer) // tb if per_layer != rows else 0
    npart = len(g_parts)
    nprev = 0 if prev is None else 4

    def body(*refs):
        w_ref = refs[0]
        gp = refs[1:1 + npart]
        m_ref, v_ref = refs[1 + npart:3 + npart]
        g_out, d_out, m_out, v_out = refs[3 + npart + nprev:]
        g = gp[0][...]
        for r in gp[1:]:
            g = g + r[...]
        wv = w_ref[...]
        m2 = ADAM_B1 * m_ref[...] + (1.0 - ADAM_B1) * g
        v2 = ADAM_B2 * v_ref[...] + (1.0 - ADAM_B2) * jnp.square(g)
        m_hat = m2 / (1.0 - ADAM_B1 ** ADAM_STEP)
        v_hat = v2 / (1.0 - ADAM_B2 ** ADAM_STEP)
        g_out[...] = g
        d_out[...] = -ADAM_LR * (m_hat / (jnp.sqrt(v_hat) + ADAM_EPS) + ADAM_WD * wv)
        m_out[...] = m2
        v_out[...] = v2

    wspec = pl.BlockSpec((tb, cols), lambda i: (i + off, 0))
    gspec = pl.BlockSpec((tb, cols), lambda i: (i, 0))
    in_specs = [wspec] + [gspec] * npart + [wspec, wspec] + [_ANY] * nprev
    args = [two(w)] + [two(p) for p in g_parts] + [two(m), two(v)] + ([two(a) for a in prev] if prev is not None else [])
    outs = _pcall(
        body, name=name, grid=(grows // tb,), in_specs=in_specs, out_specs=[wspec] * 4,
        out_shape=[jax.ShapeDtypeStruct((rows, cols), F32)] * 4,
        input_output_aliases={3 + npart + k: k for k in range(nprev)}, compiler_params=_cparams(("parallel",)),
    )(*args)
    return [o.reshape(shp) for o in outs]


BIG = ("w_in", "b_w_uq", "b_w_ukv", "m_w_kv", "w_branch", "w_out", "w_up", "w_down")
_SHARD_AXIS = dict(w_in=2, b_w_uq=2, b_w_ukv=2, m_w_kv=1, w_branch=3, w_out=1, w_up=2, w_down=1)
SMALL = ("g_mix", "b_gate", "a_qn", "a_kn", "a_sink", "b_qa_norm", "b_kva_norm", "b_qn", "b_kn", "c_qn", "c_kn",
         "m_g_mem", "m_qn", "m_kn", "g_mlp")
WEIGHTS = ("g_mix", "w_in", "b_gate", "a_qn", "a_kn", "a_sink", "b_qa_norm", "b_kva_norm", "b_w_uq", "b_w_ukv", "b_qn", "b_kn",
           "c_qn", "c_kn", "m_g_mem", "m_w_kv", "m_qn", "m_kn", "w_branch", "w_out", "g_mlp", "w_up", "w_down")


def _unshard(gathered, axis):
    moved = jnp.moveaxis(gathered, 0, axis)
    shp = list(gathered.shape[1:])
    shp[axis] *= N_CHIPS
    return moved.reshape(shp)


def _shard_parts(full, axis):
    shp = list(full.shape)
    shp[axis:axis + 1] = [N_CHIPS, shp[axis] // N_CHIPS]
    return jnp.moveaxis(full.reshape(shp), axis, 0)


def _pack_small(d):
    flat = jnp.concatenate([d[k].reshape(-1).astype(F32) for k in SMALL])
    n = flat.shape[0]
    pad = (-n) % (8 * LANE)
    return jnp.pad(flat, (0, pad)).reshape(-1, LANE)


def _unpack_small(packed, like):
    flat = packed.reshape(-1)
    out, off = {}, 0
    for k in SMALL:
        n = int(np.prod(like[k].shape))
        out[k] = flat[off:off + n].reshape(like[k].shape)
        off += n
    return out


def _train_step(x, mem, positions, loss_target, w, m, v):
    depth = w["g_mix"].shape[0]
    me = 2 * lax.axis_index("x") + lax.axis_index("y")
    landing = lambda a: jnp.zeros((N_CHIPS,) + a.shape, a.dtype)
    gathers = []
    for l in range(depth):
        own = [w[k][l].astype(BF16) for k in BIG]
        gathers.append(_exchange_start(own, [landing(a) for a in own], gather=True, name=f"gather_start{l}"))

    def with_own(land, mine):
        slot = lax.broadcasted_iota(jnp.int32, (N_CHIPS,) + (1,) * mine.ndim, 0)
        return jnp.where(slot == me, mine[None], land)

    def get_big(l, after):
        mine, lands = _exchange_wait(gathers[l], after, gather=True, name=f"gather_wait{l}")
        return _big_to_kernel_layout({k: _unshard(with_own(g, o), _SHARD_AXIS[k] - 1) for k, g, o in zip(BIG, lands, mine)})

    scatters = [None] * depth

    def put_grads(l, g):
        gref = _big_grads_to_reference_layout(g)
        parts = [_shard_parts(gref[k], _SHARD_AXIS[k] - 1).astype(BF16) for k in BIG]
        scatters[l] = _exchange_start(parts, [landing(p[0]) for p in parts], gather=False, name=f"scatter_start{l}")
        return scatters[l][4][:1, :1]

    small = {k: w[k] for k in SMALL}
    loss, gx, grads = _local_step(x[0], mem[0], positions[0], small, loss_target[0], get_big, put_grads)
    loss = lax.psum(loss, ("x", "y", "c"))
    me1 = me.reshape(1).astype(jnp.int32)
    res = {k: None for k in BIG}
    after = scatters[0][4]
    for lo, n in ([(1, depth - 1), (0, 1)] if depth > 1 else [(0, 1)]):
        waited = [_exchange_wait(scatters[l], after, gather=False, name=f"scatter_wait{l}") for l in range(lo, lo + n)]
        parts, recv = [p for p, _ in waited], [r for _, r in waited]
        mine = [_sum_slots([recv[l][a] for l in range(n)], [parts[l][a] for l in range(n)], me1, name=f"sum_{k}_{lo}")
                for a, k in enumerate(BIG)]
        theirs = _sibling_exchange(mine, name=f"sibling_grads{lo}")
        for k, p, q in zip(BIG, mine, theirs):
            res[k] = _adamw(w[k], [p, q], m[k], v[k], layer0=lo, prev=res[k], name=f"adamw_{k}_{lo}")
        after = res[BIG[-1]][0]
    gsmall = _small_grads_to_reference_layout(grads)
    g_small = _allreduce_small(_pack_small(gsmall), name="allreduce_small")
    packed = _adamw(_pack_small(small), [g_small], _pack_small({k: m[k] for k in SMALL}), _pack_small({k: v[k] for k in SMALL}),
                    name="adamw_small")
    unpacked = [_unpack_small(p, small) for p in packed]
    for k in SMALL:
        res[k] = [u[k] for u in unpacked]
    outs = [loss, gx[None]]
    for i in range(4):
        outs += [res[k][i] for k in WEIGHTS]
    return tuple(outs)

def kernel(x, mem, positions, g_mix, w_in, b_gate, a_qn, a_kn, a_sink, b_qa_norm, b_kva_norm, b_w_uq, b_w_ukv, b_qn, b_kn, c_qn, c_kn, m_g_mem, m_w_kv, m_qn, m_kn, w_branch, w_out, g_mlp, w_up, w_down, loss_target, m_g_mix, m_w_in, m_b_gate, m_a_qn, m_a_kn, m_a_sink, m_b_qa_norm, m_b_kva_norm, m_b_w_uq, m_b_w_ukv, m_b_qn, m_b_kn, m_c_qn, m_c_kn, m_m_g_mem, m_m_w_kv, m_m_qn, m_m_kn, m_w_branch, m_w_out, m_g_mlp, m_w_up, m_w_down, v_g_mix, v_w_in, v_b_gate, v_a_qn, v_a_kn, v_a_sink, v_b_qa_norm, v_b_kva_norm, v_b_w_uq, v_b_w_ukv, v_b_qn, v_b_kn, v_c_qn, v_c_kn, v_m_g_mem, v_m_w_kv, v_m_qn, v_m_kn, v_w_branch, v_w_out, v_g_mlp, v_w_up, v_w_down):
    args = dict(locals())
    w = {k: args[k] for k in WEIGHTS}
    m = {k: args["m_" + k] for k in WEIGHTS}
    v = {k: args["v_" + k] for k in WEIGHTS}
    return _train_step(x, mem, positions, loss_target, w, m, v)
```

```python
import functools
import math

import jax
import jax.numpy as jnp
import numpy as np
from jax import lax
from jax.experimental import pallas as pl
from jax.experimental.pallas import tpu as pltpu

F32 = jnp.float32
BF16 = jnp.bfloat16

DEPTH = 4
BLOCK = 128
ROPE_THETA = 10000.0
EPS = 1e-6
NEG = -1e30
A_HEADS, A_KV_HEADS, A_HD, A_WINDOW = 8, 2, 64, 128
B_HEADS, B_Q_LORA, B_KV_LORA, B_NOPE, B_ROPE, B_V = 8, 384, 256, 64, 32, 64
C_PATTERNS = ((128, 1), (512, 4), (2048, 16))
C_HEADS, C_HD = 8, 64
M_HEADS, M_HD = 4, 128
N_BRANCH, BRANCH_W = 4, 512
ADAM_LR, ADAM_B1, ADAM_B2, ADAM_EPS, ADAM_WD, ADAM_STEP = 0.001, 0.9, 0.999, 1e-08, 0.01, 10

LANE = 128
VMEM_LIMIT = 56 * 1024 * 1024


def _pcall(body, **kw):
    return pl.pallas_call(body, **kw)


def _cparams(sem):
    return pltpu.CompilerParams(dimension_semantics=sem, vmem_limit_bytes=VMEM_LIMIT)


def _tile(n, target):
    if n <= target:
        return n
    best = None
    for t in range(LANE, target + 1, LANE):
        if n % t == 0:
            best = t
    return best if best is not None else n


def _mm(a, b, *, ta=False, tb=False, out_dtype=F32, pro_a=None, epi=None, extra=None, name,
        tm=1024, tn=1024, tk=1024):
    if ta:
        K, M = a.shape
    else:
        M, K = a.shape
    if tb:
        N, K2 = b.shape
    else:
        K2, N = b.shape
    assert K == K2, (a.shape, b.shape, ta, tb)
    tm, tn, tk = _tile(M, tm), _tile(N, tn), _tile(K, tk)
    nk = K // tk
    a_spec = pl.BlockSpec((tk, tm), lambda i, j, k: (k, i)) if ta else pl.BlockSpec((tm, tk), lambda i, j, k: (i, k))
    b_spec = pl.BlockSpec((tn, tk), lambda i, j, k: (j, k)) if tb else pl.BlockSpec((tk, tn), lambda i, j, k: (k, j))
    o_spec = pl.BlockSpec((tm, tn), lambda i, j, k: (i, j))
    dims = (((0,) if ta else (1,), (1,) if tb else (0,)), ((), ()))
    has_extra = extra is not None

    def body(*refs):
        if has_extra:
            a_ref, b_ref, e_ref, o_ref, acc_ref = refs
        else:
            a_ref, b_ref, o_ref, acc_ref = refs
            e_ref = None
        k = pl.program_id(2)
        av = a_ref[...]
        if pro_a is not None:
            av = pro_a(av.astype(F32))
        part = lax.dot_general(av.astype(BF16), b_ref[...].astype(BF16), dims, preferred_element_type=F32)

        @pl.when(k == 0)
        def _():
            acc_ref[...] = part

        @pl.when(k > 0)
        def _():
            acc_ref[...] += part

        @pl.when(k == nk - 1)
        def _():
            r = acc_ref[...]
            if epi is not None:
                r = epi(r, e_ref[...]) if has_extra else epi(r)
            o_ref[...] = r.astype(out_dtype)

    in_specs = [a_spec, b_spec] + ([o_spec] if has_extra else [])
    args = (a, b) + ((extra,) if has_extra else ())
    return _pcall(
        body, name=name, grid=(M // tm, N // tn, nk), in_specs=in_specs, out_specs=o_spec,
        out_shape=jax.ShapeDtypeStruct((M, N), out_dtype),
        scratch_shapes=[pltpu.VMEM((tm, tn), F32)],
        compiler_params=_cparams(("parallel", "parallel", "arbitrary")),
    )(*args)


def _piece(p):
    if isinstance(p, tuple):
        return p
    return (p, p.shape[1], 0)


def _row_spec(width, idx, tb):
    return pl.BlockSpec((tb, width), lambda i, idx=idx: (i, idx))


def _full_spec(arr):
    nd = arr.ndim
    return pl.BlockSpec(arr.shape, lambda i, nd=nd: (0,) * nd)


def _rowmap(f, rows, params, outs, *, tb, name):
    rows = [_piece(p) for p in rows]
    R = rows[0][0].shape[0]
    tb = min(tb, R)
    nr, npar, nout = len(rows), len(params), len(outs)

    def body(*refs):
        rv = [r[...] for r in refs[:nr]]
        pv = [r[...] for r in refs[nr:nr + npar]]
        res = f(*rv, *pv)
        for o_ref, val in zip(refs[nr + npar:], res):
            o_ref[...] = val.astype(o_ref.dtype)

    return _pcall(
        body, name=name, grid=(R // tb,),
        in_specs=[_row_spec(w, idx, tb) for (_, w, idx) in rows] + [_full_spec(p) for p in params],
        out_specs=[_row_spec(w, 0, tb) for (w, _) in outs],
        out_shape=[jax.ShapeDtypeStruct((R, w), dt) for (w, dt) in outs],
        compiler_params=_cparams(("parallel",)),
    )(*[r[0] for r in rows], *params)


def _rowmap_bwd(f, rows, params, couts, *, diff, out_dtypes, adds=None, tb, name):
    rows = [_piece(p) for p in rows]
    couts = [_piece(p) for p in couts]
    R = rows[0][0].shape[0]
    tb = min(tb, R)
    nr, npar, nc = len(rows), len(params), len(couts)
    didx = [i for i, d in enumerate(diff) if d]
    adds = [None] * len(didx) if adds is None else adds
    add_ops = [_piece(a) for a in adds if a is not None]
    na = len(add_ops)

    def body(*refs):
        rv = [r[...] for r in refs[:nr]]
        pv = [r[...] for r in refs[nr:nr + npar]]
        cv = [r[...] for r in refs[nr + npar:nr + npar + nc]]
        av = [r[...] for r in refs[nr + npar + nc:nr + npar + nc + na]]
        o_refs = refs[nr + npar + nc + na:]
        drow_refs, dpar_refs = o_refs[:len(didx)], o_refs[len(didx):]
        nondiff = {i: rv[i] for i in range(nr) if not diff[i]}

        def g(*dv):
            full = []
            it = iter(dv[:len(didx)])
            for i in range(nr):
                full.append(nondiff[i] if i in nondiff else next(it))
            return f(*full, *dv[len(didx):])

        res, vjp = jax.vjp(g, *[rv[i].astype(F32) for i in didx], *pv)
        cts = tuple(c.astype(r.dtype) for c, r in zip(cv, res))
        grads = vjp(cts)
        ai = 0
        for n, o_ref in enumerate(drow_refs):
            val = grads[n]
            if adds[n] is not None:
                val = val + av[ai].astype(F32)
                ai += 1
            o_ref[...] = val.astype(o_ref.dtype)
        first = pl.program_id(0) == 0
        for n, o_ref in enumerate(dpar_refs):
            gp = grads[len(didx) + n].astype(F32)

            @pl.when(first)
            def _(o_ref=o_ref, gp=gp):
                o_ref[...] = gp

            @pl.when(jnp.logical_not(first))
            def _(o_ref=o_ref, gp=gp):
                o_ref[...] += gp

    outs = _pcall(
        body, name=name, grid=(R // tb,),
        in_specs=([_row_spec(w, idx, tb) for (_, w, idx) in rows] + [_full_spec(p) for p in params]
                  + [_row_spec(w, idx, tb) for (_, w, idx) in couts] + [_row_spec(w, idx, tb) for (_, w, idx) in add_ops]),
        out_specs=[_row_spec(rows[i][1], 0, tb) for i in didx] + [_full_spec(p) for p in params],
        out_shape=([jax.ShapeDtypeStruct((R, rows[i][1]), dt) for i, dt in zip(didx, out_dtypes)]
                   + [jax.ShapeDtypeStruct(p.shape, F32) for p in params]),
        compiler_params=_cparams(("arbitrary",)),
    )(*[r[0] for r in rows], *params, *[c[0] for c in couts], *[a[0] for a in add_ops])
    return outs[:len(didx)], outs[len(didx):]


@functools.partial(jax.custom_vjp, nondiff_argnums=(1,))
def _lane_roll(x, shift):
    return pltpu.roll(x, shift % x.shape[-1], axis=x.ndim - 1)


def _lane_roll_fwd(x, shift):
    return _lane_roll(x, shift), None


def _lane_roll_bwd(shift, _, g):
    return (_lane_roll(g, -shift),)


_lane_roll.defvjp(_lane_roll_fwd, _lane_roll_bwd)


def _group_matrix(kind):
    r = lax.broadcasted_iota(jnp.int32, (LANE, LANE), 0)
    c = lax.broadcasted_iota(jnp.int32, (LANE, LANE), 1)
    if kind == "mla":
        gid = lambda l: jnp.where(l < B_NOPE, 0, jnp.where(l < B_NOPE + B_ROPE, 1, 2))
        inv = jnp.where(c < B_NOPE, 1.0 / B_NOPE, 1.0 / B_ROPE)
        return jnp.where(gid(r) == gid(c), inv, 0.0).astype(BF16)
    return jnp.where(r // kind == c // kind, 1.0 / kind, 0.0).astype(BF16)


@functools.partial(jax.custom_vjp, nondiff_argnums=(1,))
def _group_mean(xx, kind):
    gm = _group_matrix(kind)
    outs = []
    for b in range(xx.shape[-1] // LANE):
        t = xx[:, b * LANE:(b + 1) * LANE]
        hi = t.astype(BF16)
        lo = (t - hi.astype(F32)).astype(BF16)
        outs.append(jnp.dot(hi, gm, preferred_element_type=F32) + jnp.dot(lo, gm, preferred_element_type=F32))
    return jnp.concatenate(outs, axis=1) if len(outs) > 1 else outs[0]


def _group_mean_fwd(xx, kind):
    return _group_mean(xx, kind), None


def _group_mean_bwd(kind, _, g):
    return (_group_mean(g, kind),)


_group_mean.defvjp(_group_mean_fwd, _group_mean_bwd)


def _exact_dot(x, m):
    return jnp.dot(x, m, precision=lax.Precision.HIGHEST, preferred_element_type=F32)


def _head_norm(x, gain_tiled, group):
    return x * lax.rsqrt(_group_mean(x * x, group) + EPS) * gain_tiled


def _row_norm(x, gain):
    ms = jnp.mean(x * x, axis=-1, keepdims=True)
    return x * lax.rsqrt(ms + EPS) * gain


def _rope(x, cos, sin, hd):
    half = hd // 2
    lane = lax.broadcasted_iota(jnp.int32, x.shape, x.ndim - 1) % hd
    other = jnp.where(lane < half, -_lane_roll(x, -half), _lane_roll(x, half))
    return x * cos + other * sin


class _AttnGeom:
    def __init__(self, mode, lq, lk, max_dist=0):
        self.mode, self.lq, self.lk, self.max_dist = mode, lq, lk, max_dist
        if mode == "band":
            self.bq = self.bk = BLOCK
            self.nt_q = 2
            self.nt_k = 2
        elif mode == "causal":
            self.bq = self.bk = min(256, lq)
            self.nt_q = lk // self.bk
            self.nt_k = lq // self.bq
        else:
            self.bq = min(256, lq)
            self.bk = lk
            self.nt_q = 1
            self.nt_k = lq // self.bq
        self.nq, self.nk = lq // self.bq, lk // self.bk

    def kv_block(self, i, t):
        if self.mode == "band":
            return jnp.maximum(i - t, 0)
        if self.mode == "causal":
            return jnp.minimum(t, i)
        return 0 * i

    def kv_active(self, i, t):
        if self.mode == "band":
            return i - t >= 0
        if self.mode == "causal":
            return t <= i
        return None

    def q_block(self, j, t):
        if self.mode == "band":
            return jnp.minimum(j + t, self.nq - 1)
        if self.mode == "causal":
            return jnp.maximum(t, j)
        return t

    def q_active(self, j, t):
        if self.mode == "band":
            return j + t <= self.nq - 1
        if self.mode == "causal":
            return t >= j
        return None

    def mask(self, qb, kb):
        if self.mode == "full":
            return None
        qp = qb * self.bq + lax.broadcasted_iota(jnp.int32, (self.bq, self.bk), 0)
        kp = kb * self.bk + lax.broadcasted_iota(jnp.int32, (self.bq, self.bk), 1)
        d = qp - kp
        if self.mode == "band":
            return (d >= 0) & (d <= self.max_dist)
        return d >= 0


def _when(cond, fn):
    if cond is None:
        fn()
    else:
        pl.when(cond)(fn)


def _dil_view(p, dil):
    arr, w, idx = _piece(p)
    R, C = arr.shape
    assert C % w == 0, (C, w)
    return arr.reshape(R // dil, dil * C), w, idx, C // w


def _seq_spec(view, rows, blk_fn):
    _, w, idx, cpw = view
    return pl.BlockSpec((rows, w), lambda s, i, t: (blk_fn(i, t), s * cpw + idx))


_NT = (((1,), (1,)), ((), ()))
_TN = (((0,), (0,)), ((), ()))


def _scores(geom, scale, q, k, qp, kp, h, g, hd, rope, qb, kb):
    s = lax.dot_general(q[:, h * hd:(h + 1) * hd], k[:, g * hd:(g + 1) * hd], _NT, preferred_element_type=F32)
    if rope:
        s = s + lax.dot_general(qp[:, h * rope:(h + 1) * rope], kp[:, :rope], _NT, preferred_element_type=F32)
    s = s * scale
    m = geom.mask(qb, kb)
    return s, m


def _attn_fwd(geom, q, k, v, *, hq, hk, hd, hdv, scale, dil=1, qp=None, kp=None, rope=0, name):
    qv, kv, vv = _dil_view(q, dil), _dil_view(k, dil), _dil_view(v, dil)
    R = _piece(q)[0].shape[0]
    grp = hq // hk
    bq, bk, nt = geom.bq, geom.bk, geom.nt_q
    ops = [qv, kv, vv]
    specs = [_seq_spec(qv, bq, lambda i, t: i), _seq_spec(kv, bk, geom.kv_block), _seq_spec(vv, bk, geom.kv_block)]
    if rope:
        qpv, kpv = _dil_view(qp, dil), _dil_view(kp, dil)
        ops += [qpv, kpv]
        specs += [_seq_spec(qpv, bq, lambda i, t: i), _seq_spec(kpv, bk, geom.kv_block)]
    ow = hq * hdv
    o_view = (None, ow, 0, 1)
    o_spec = pl.BlockSpec((bq, ow), lambda s, i, t: (i, s))

    def body(*refs):
        if rope:
            q_ref, k_ref, v_ref, qp_ref, kp_ref, o_ref, lse_ref, m_sc, l_sc, acc_sc = refs
        else:
            q_ref, k_ref, v_ref, o_ref, lse_ref, m_sc, l_sc, acc_sc = refs
            qp_ref = kp_ref = None
        i, t = pl.program_id(1), pl.program_id(2)

        @pl.when(t == 0)
        def _():
            m_sc[...] = jnp.full(m_sc.shape, NEG, F32)
            l_sc[...] = jnp.zeros(l_sc.shape, F32)
            acc_sc[...] = jnp.zeros(acc_sc.shape, F32)

        def step():
            qa, ka, va = q_ref[...].astype(BF16), k_ref[...].astype(BF16), v_ref[...].astype(BF16)
            qpa = qp_ref[...].astype(BF16) if rope else None
            kpa = kp_ref[...].astype(BF16) if rope else None
            kb = geom.kv_block(i, t)
            for h in range(hq):
                g = h // grp
                s, msk = _scores(geom, scale, qa, ka, qpa, kpa, h, g, hd, rope, i, kb)
                if msk is not None:
                    s = jnp.where(msk, s, NEG)
                m_old = m_sc[h]
                m_new = jnp.maximum(m_old, jnp.max(s, axis=1, keepdims=True))
                p = jnp.exp(s - m_new)
                alpha = jnp.exp(m_old - m_new)
                l_sc[h] = alpha * l_sc[h] + jnp.sum(p, axis=1, keepdims=True)
                pv = jnp.dot(p.astype(BF16), va[:, g * hdv:(g + 1) * hdv], preferred_element_type=F32)
                acc_sc[h] = alpha * acc_sc[h] + pv
                m_sc[h] = m_new

        _when(geom.kv_active(i, t), step)

        @pl.when(t == nt - 1)
        def _():
            for h in range(hq):
                l = l_sc[h]
                o_ref[:, h * hdv:(h + 1) * hdv] = acc_sc[h] / l
                lse_ref[:, h * hdv:(h + 1) * hdv] = jnp.broadcast_to(m_sc[h] + jnp.log(l), (bq, hdv))

    o, lse = _pcall(
        body, name=name, grid=(dil, geom.nq, nt), in_specs=specs, out_specs=[o_spec, o_spec],
        out_shape=[jax.ShapeDtypeStruct((R // dil, dil * ow), F32)] * 2,
        scratch_shapes=[pltpu.VMEM((hq, bq, 1), F32), pltpu.VMEM((hq, bq, 1), F32), pltpu.VMEM((hq, bq, hdv), F32)],
        compiler_params=_cparams(("parallel", "parallel", "arbitrary")),
    )(*[o_[0] for o_ in ops])
    return o.reshape(R, ow), lse.reshape(R, ow)


def _attn_dq(geom, q, k, v, do, o, lse, dlse, *, hq, hk, hd, hdv, scale, dil=1, qp=None, kp=None, rope=0,
             out_dtype=F32, name):
    qv, kv, vv = _dil_view(q, dil), _dil_view(k, dil), _dil_view(v, dil)
    dov, ov, lv = _dil_view(do, dil), _dil_view(o, dil), _dil_view(lse, dil)
    R = _piece(q)[0].shape[0]
    grp = hq // hk
    bq, bk, nt = geom.bq, geom.bk, geom.nt_q
    qi = lambda i, t: i
    ops = [qv, kv, vv, dov, ov, lv]
    specs = [_seq_spec(qv, bq, qi), _seq_spec(kv, bk, geom.kv_block), _seq_spec(vv, bk, geom.kv_block),
             _seq_spec(dov, bq, qi), _seq_spec(ov, bq, qi), _seq_spec(lv, bq, qi)]
    has_dl = dlse is not None
    if has_dl:
        dlv = _dil_view(dlse, dil)
        ops.append(dlv)
        specs.append(_seq_spec(dlv, bq, qi))
    if rope:
        qpv, kpv = _dil_view(qp, dil), _dil_view(kp, dil)
        ops += [qpv, kpv]
        specs += [_seq_spec(qpv, bq, qi), _seq_spec(kpv, bk, geom.kv_block)]
    qw = hq * hd
    out_specs = [pl.BlockSpec((bq, qw), lambda s, i, t: (i, s))]
    out_shape = [jax.ShapeDtypeStruct((R // dil, dil * qw), out_dtype)]
    scratch = [pltpu.VMEM((hq, bq, 1), F32), pltpu.VMEM((hq, bq, hd), F32)]
    if rope:
        out_specs.append(pl.BlockSpec((bq, hq * rope), lambda s, i, t: (i, s)))
        out_shape.append(jax.ShapeDtypeStruct((R // dil, dil * hq * rope), out_dtype))
        scratch.append(pltpu.VMEM((hq, bq, rope), F32))

    def body(*refs):
        refs = list(refs)
        q_ref, k_ref, v_ref, do_ref, o_ref, l_ref = refs[:6]
        pos = 6
        dl_ref = None
        if has_dl:
            dl_ref = refs[pos]
            pos += 1
        qp_ref = kp_ref = None
        if rope:
            qp_ref, kp_ref = refs[pos:pos + 2]
            pos += 2
        dq_ref = refs[pos]
        pos += 1
        dqp_ref = None
        if rope:
            dqp_ref = refs[pos]
            pos += 1
        dl_sc, dq_sc = refs[pos:pos + 2]
        dqp_sc = refs[pos + 2] if rope else None
        i, t = pl.program_id(1), pl.program_id(2)

        @pl.when(t == 0)
        def _():
            dov_, ov_ = do_ref[...].astype(F32), o_ref[...].astype(F32)
            prod = dov_ * ov_
            for h in range(hq):
                d = jnp.sum(prod[:, h * hdv:(h + 1) * hdv], axis=1, keepdims=True)
                if has_dl:
                    d = d - jnp.sum(dl_ref[:, h * hdv:(h + 1) * hdv].astype(F32), axis=1, keepdims=True)
                dl_sc[h] = d
            dq_sc[...] = jnp.zeros(dq_sc.shape, F32)
            if rope:
                dqp_sc[...] = jnp.zeros(dqp_sc.shape, F32)

        def step():
            qa, ka, va = q_ref[...].astype(BF16), k_ref[...].astype(BF16), v_ref[...].astype(BF16)
            doa = do_ref[...].astype(BF16)
            qpa = qp_ref[...].astype(BF16) if rope else None
            kpa = kp_ref[...].astype(BF16) if rope else None
            kb = geom.kv_block(i, t)
            for h in range(hq):
                g = h // grp
                s, msk = _scores(geom, scale, qa, ka, qpa, kpa, h, g, hd, rope, i, kb)
                p = jnp.exp(s - l_ref[:, h * hdv:h * hdv + 1])
                if msk is not None:
                    p = jnp.where(msk, p, 0.0)
                dp = lax.dot_general(doa[:, h * hdv:(h + 1) * hdv], va[:, g * hdv:(g + 1) * hdv], _NT,
                                     preferred_element_type=F32)
                ds = (p * (dp - dl_sc[h]) * scale).astype(BF16)
                dq_sc[h] += jnp.dot(ds, ka[:, g * hd:(g + 1) * hd], preferred_element_type=F32)
                if rope:
                    dqp_sc[h] += jnp.dot(ds, kpa[:, :rope], preferred_element_type=F32)

        _when(geom.kv_active(i, t), step)

        @pl.when(t == nt - 1)
        def _():
            for h in range(hq):
                dq_ref[:, h * hd:(h + 1) * hd] = dq_sc[h].astype(dq_ref.dtype)
                if rope:
                    dqp_ref[:, h * rope:(h + 1) * rope] = dqp_sc[h].astype(dqp_ref.dtype)

    outs = _pcall(
        body, name=name, grid=(dil, geom.nq, nt), in_specs=specs, out_specs=out_specs, out_shape=out_shape,
        scratch_shapes=scratch, compiler_params=_cparams(("parallel", "parallel", "arbitrary")),
    )(*[o_[0] for o_ in ops])
    dq = outs[0].reshape(R, qw)
    if rope:
        return dq, outs[1].reshape(R, hq * rope)
    return dq


def _attn_dkv(geom, q, k, v, do, o, lse, dlse, *, hq, hk, hd, hdv, scale, dil=1, qp=None, kp=None, rope=0,
              out_dtype=F32, name):
    qv, kv, vv = _dil_view(q, dil), _dil_view(k, dil), _dil_view(v, dil)
    dov, ov, lv = _dil_view(do, dil), _dil_view(o, dil), _dil_view(lse, dil)
    Rk = _piece(k)[0].shape[0]
    grp = hq // hk
    bq, bk, nt = geom.bq, geom.bk, geom.nt_k
    kj = lambda j, t: j
    ops = [qv, kv, vv, dov, ov, lv]
    specs = [_seq_spec(qv, bq, geom.q_block), _seq_spec(kv, bk, kj), _seq_spec(vv, bk, kj),
             _seq_spec(dov, bq, geom.q_block), _seq_spec(ov, bq, geom.q_block), _seq_spec(lv, bq, geom.q_block)]
    has_dl = dlse is not None
    if has_dl:
        dlv = _dil_view(dlse, dil)
        ops.append(dlv)
        specs.append(_seq_spec(dlv, bq, geom.q_block))
    if rope:
        qpv, kpv = _dil_view(qp, dil), _dil_view(kp, dil)
        ops += [qpv, kpv]
        specs += [_seq_spec(qpv, bq, geom.q_block), _seq_spec(kpv, bk, kj)]
    kw, vw = hk * hd, hk * hdv
    out_specs = [pl.BlockSpec((bk, kw), lambda s, j, t: (j, s)), pl.BlockSpec((bk, vw), lambda s, j, t: (j, s))]
    out_shape = [jax.ShapeDtypeStruct((Rk // dil, dil * kw), out_dtype), jax.ShapeDtypeStruct((Rk // dil, dil * vw), out_dtype)]
    scratch = [pltpu.VMEM((hk, bk, hd), F32), pltpu.VMEM((hk, bk, hdv), F32)]
    if rope:
        out_specs.append(pl.BlockSpec((bk, LANE), lambda s, j, t: (j, s)))
        out_shape.append(jax.ShapeDtypeStruct((Rk // dil, dil * LANE), out_dtype))
        scratch.append(pltpu.VMEM((bk, rope), F32))

    def body(*refs):
        refs = list(refs)
        q_ref, k_ref, v_ref, do_ref, o_ref, l_ref = refs[:6]
        pos = 6
        dl_ref = None
        if has_dl:
            dl_ref = refs[pos]
            pos += 1
        qp_ref = kp_ref = None
        if rope:
            qp_ref, kp_ref = refs[pos:pos + 2]
            pos += 2
        dk_ref, dv_ref = refs[pos:pos + 2]
        pos += 2
        dkp_ref = None
        if rope:
            dkp_ref = refs[pos]
            pos += 1
        dk_sc, dv_sc = refs[pos:pos + 2]
        dkp_sc = refs[pos + 2] if rope else None
        j, t = pl.program_id(1), pl.program_id(2)

        @pl.when(t == 0)
        def _():
            dk_sc[...] = jnp.zeros(dk_sc.shape, F32)
            dv_sc[...] = jnp.zeros(dv_sc.shape, F32)
            if rope:
                dkp_sc[...] = jnp.zeros(dkp_sc.shape, F32)

        def step():
            qa, ka, va = q_ref[...].astype(BF16), k_ref[...].astype(BF16), v_ref[...].astype(BF16)
            dof = do_ref[...].astype(F32)
            doa = dof.astype(BF16)
            prod = dof * o_ref[...].astype(F32)
            qpa = qp_ref[...].astype(BF16) if rope else None
            kpa = kp_ref[...].astype(BF16) if rope else None
            qb = geom.q_block(j, t)
            for h in range(hq):
                g = h // grp
                s, msk = _scores(geom, scale, qa, ka, qpa, kpa, h, g, hd, rope, qb, j)
                p = jnp.exp(s - l_ref[:, h * hdv:h * hdv + 1])
                if msk is not None:
                    p = jnp.where(msk, p, 0.0)
                delta = jnp.sum(prod[:, h * hdv:(h + 1) * hdv], axis=1, keepdims=True)
                if has_dl:
                    delta = delta - jnp.sum(dl_ref[:, h * hdv:(h + 1) * hdv].astype(F32), axis=1, keepdims=True)
                do_h = doa[:, h * hdv:(h + 1) * hdv]
                dv_sc[g] += lax.dot_general(p.astype(BF16), do_h, _TN, preferred_element_type=F32)
                dp = lax.dot_general(do_h, va[:, g * hdv:(g + 1) * hdv], _NT, preferred_element_type=F32)
                ds = (p * (dp - delta) * scale).astype(BF16)
                dk_sc[g] += lax.dot_general(ds, qa[:, h * hd:(h + 1) * hd], _TN, preferred_element_type=F32)
                if rope:
                    dkp_sc[...] += lax.dot_general(ds, qpa[:, h * rope:(h + 1) * rope], _TN, preferred_element_type=F32)

        _when(geom.q_active(j, t), step)

        @pl.when(t == nt - 1)
        def _():
            for g in range(hk):
                dk_ref[:, g * hd:(g + 1) * hd] = dk_sc[g].astype(dk_ref.dtype)
                dv_ref[:, g * hdv:(g + 1) * hdv] = dv_sc[g].astype(dv_ref.dtype)
            if rope:
                dkp_ref[...] = jnp.zeros(dkp_ref.shape, dkp_ref.dtype)
                dkp_ref[:, :rope] = dkp_sc[...].astype(dkp_ref.dtype)

    outs = _pcall(
        body, name=name, grid=(dil, geom.nk, nt), in_specs=specs, out_specs=out_specs, out_shape=out_shape,
        scratch_shapes=scratch, compiler_params=_cparams(("parallel", "parallel", "arbitrary")),
    )(*[o_[0] for o_ in ops])
    dk, dv = outs[0].reshape(Rk, kw), outs[1].reshape(Rk, vw)
    if rope:
        return dk, dv, outs[2].reshape(Rk, LANE)
    return dk, dv


class _BandPlan:
    def __init__(self, S, dil, max_dist):
        self.L, self.dil, self.max_dist = S // dil, dil, max_dist
        self.nblk = self.L // BLOCK
        self.nb = min(4, self.nblk)
        self.ns = min(dil, max(1, 4 // self.nb))
        self.grid = (dil // self.ns, self.nblk // self.nb)
        self.rows = self.nb * BLOCK

    def view(self, p):
        arr, w, idx = _piece(p)
        R, C = arr.shape
        assert C % w == 0 and (self.ns == 1 or (C == w and idx == 0)), (C, w, idx, self.ns)
        return arr.reshape(R // self.dil, self.dil * C), w, idx, C // w

    def main(self, view):
        _, w, idx, cpw = view
        if self.ns == 1:
            return pl.BlockSpec((self.rows, w), lambda s, i: (i, s * cpw + idx))
        return pl.BlockSpec((self.rows, self.ns * w), lambda s, i: (i, s))

    def edge(self, view, nxt):
        _, w, idx, cpw = view
        nb, last = self.nb, self.nblk - 1
        rb = (lambda i: jnp.minimum((i + 1) * nb, last)) if nxt else (lambda i: jnp.maximum(i * nb - 1, 0))
        if self.ns == 1:
            return pl.BlockSpec((BLOCK, w), lambda s, i: (rb(i), s * cpw + idx))
        return pl.BlockSpec((BLOCK, self.ns * w), lambda s, i: (rb(i), s))

    def out(self, w):
        return pl.BlockSpec((self.rows, self.ns * w), lambda s, i: (i, s))

    def masks(self):
        qi = lax.broadcasted_iota(jnp.int32, (BLOCK, BLOCK), 0)
        kj = lax.broadcasted_iota(jnp.int32, (BLOCK, BLOCK), 1)
        return kj <= qi, (qi - kj + BLOCK) <= self.max_dist


def _half(e, rows=BLOCK):
    lane = lax.broadcasted_iota(jnp.int32, (rows, LANE), 1)
    return (lane < LANE // 2) if e == 0 else (lane >= LANE // 2)


def _swap_halves(t):
    return pltpu.roll(t, LANE // 2, axis=1)


def _kv_group(ref, rows, col0, pr, grp, hq, hk):
    if hk == hq:
        return ref[rows, col0 + pr * LANE:col0 + (pr + 1) * LANE].astype(BF16)
    g = (2 * pr) // grp
    t = ref[rows, col0 + (g // 2) * LANE:col0 + (g // 2 + 1) * LANE].astype(BF16)
    sw = _swap_halves(t)
    h0 = _half(0, t.shape[0])
    return jnp.where(h0, t, sw) if g % 2 == 0 else jnp.where(h0, sw, t)


def _stack_heads(t2):
    z = jnp.zeros_like(t2)
    h0 = _half(0, t2.shape[0])
    return jnp.concatenate([jnp.where(h0, t2, z), jnp.where(h0, z, t2)], axis=0)


def _unstack_heads(t, rows=BLOCK):
    return jnp.where(_half(0, rows), t[:rows], t[rows:])


def _per_head_col(t2):
    return jnp.concatenate([t2[:, :1], t2[:, LANE // 2:LANE // 2 + 1]], axis=0)


def _per_head_sum(t2):
    h0 = _half(0, t2.shape[0])
    return jnp.concatenate([jnp.sum(jnp.where(h0, t2, 0.0), axis=1, keepdims=True),
                            jnp.sum(jnp.where(h0, 0.0, t2), axis=1, keepdims=True)], axis=0)


def _band_fwd(q, k, v, *, S, dil, max_dist, hq, hk, scale, name):
    hd = 64
    plan = _BandPlan(S, dil, max_dist)
    qv, kv, vv = plan.view(q), plan.view(k), plan.view(v)
    wq, wk = hq * hd, hk * hd
    grp = hq // hk
    ns, nb = plan.ns, plan.nb

    def body(q_ref, k_ref, kp_ref, v_ref, vp_ref, o_ref, l_ref):
        i = pl.program_id(1)
        m_cur, m_band = plan.masks()
        has_prev = i > 0
        for sg in range(ns):
            for b in range(nb):
                rows = slice(b * BLOCK, (b + 1) * BLOCK)
                prows = slice((b - 1) * BLOCK, b * BLOCK) if b > 0 else slice(0, BLOCK)
                m_prev = m_band if b > 0 else (m_band & has_prev)
                msk = jnp.concatenate([m_prev, m_cur], axis=1)
                msk = jnp.concatenate([msk, msk], axis=0)
                for pr in range(wq // LANE):
                    cols = slice(sg * wq + pr * LANE, sg * wq + (pr + 1) * LANE)
                    kcat = jnp.concatenate([_kv_group(k_ref if b > 0 else kp_ref, prows, sg * wk, pr, grp, hq, hk),
                                            _kv_group(k_ref, rows, sg * wk, pr, grp, hq, hk)], axis=0)
                    vcat = jnp.concatenate([_kv_group(v_ref if b > 0 else vp_ref, prows, sg * wk, pr, grp, hq, hk),
                                            _kv_group(v_ref, rows, sg * wk, pr, grp, hq, hk)], axis=0)
                    qs = _stack_heads(q_ref[rows, cols].astype(BF16))
                    s = lax.dot_general(qs, kcat, _NT, preferred_element_type=F32) * scale
                    s = jnp.where(msk, s, NEG)
                    mx = jnp.max(s, axis=1, keepdims=True)
                    p = jnp.exp(s - mx)
                    l = jnp.sum(p, axis=1, keepdims=True)
                    acc = jnp.dot(p.astype(BF16), vcat, preferred_element_type=F32)
                    o_ref[rows, cols] = _unstack_heads(acc / l)
                    l_ref[rows, cols] = _unstack_heads(jnp.broadcast_to(mx + jnp.log(l), (2 * BLOCK, LANE)))

    o, lse = _pcall(
        body, name=name, grid=plan.grid,
        in_specs=[plan.main(qv), plan.main(kv), plan.edge(kv, False), plan.main(vv), plan.edge(vv, False)],
        out_specs=[plan.out(wq)] * 2, out_shape=[jax.ShapeDtypeStruct((S // dil, dil * wq), F32)] * 2,
        compiler_params=_cparams(("parallel", "parallel")),
    )(qv[0], kv[0], kv[0], vv[0], vv[0])
    return o.reshape(S, wq), lse.reshape(S, wq)


def _band_dq(q, k, v, do, o, lse, dlse, *, S, dil, max_dist, hq, hk, scale, out_dtype=F32, name):
    hd = 64
    plan = _BandPlan(S, dil, max_dist)
    qv, kv, vv = plan.view(q), plan.view(k), plan.view(v)
    dov, ov, lv = plan.view(do), plan.view(o), plan.view(lse)
    has_dl = dlse is not None
    wq, wk = hq * hd, hk * hd
    grp = hq // hk
    ns, nb = plan.ns, plan.nb
    ops = [qv, kv, kv, vv, vv, dov, ov, lv]
    specs = [plan.main(qv), plan.main(kv), plan.edge(kv, False), plan.main(vv), plan.edge(vv, False), plan.main(dov),
             plan.main(ov), plan.main(lv)]
    if has_dl:
        dlv = plan.view(dlse)
        ops.append(dlv)
        specs.append(plan.main(dlv))

    def body(*refs):
        q_ref, k_ref, kp_ref, v_ref, vp_ref, do_ref, o_ref, l_ref = refs[:8]
        dl_ref = refs[8] if has_dl else None
        dq_ref = refs[-1]
        i = pl.program_id(1)
        m_cur, m_band = plan.masks()
        has_prev = i > 0
        for sg in range(ns):
            for b in range(nb):
                rows = slice(b * BLOCK, (b + 1) * BLOCK)
                prows = slice((b - 1) * BLOCK, b * BLOCK) if b > 0 else slice(0, BLOCK)
                m_prev = m_band if b > 0 else (m_band & has_prev)
                msk = jnp.concatenate([m_prev, m_cur], axis=1)
                msk = jnp.concatenate([msk, msk], axis=0)
                for pr in range(wq // LANE):
                    cols = slice(sg * wq + pr * LANE, sg * wq + (pr + 1) * LANE)
                    kcat = jnp.concatenate([_kv_group(k_ref if b > 0 else kp_ref, prows, sg * wk, pr, grp, hq, hk),
                                            _kv_group(k_ref, rows, sg * wk, pr, grp, hq, hk)], axis=0)
                    vcat = jnp.concatenate([_kv_group(v_ref if b > 0 else vp_ref, prows, sg * wk, pr, grp, hq, hk),
                                            _kv_group(v_ref, rows, sg * wk, pr, grp, hq, hk)], axis=0)
                    qs = _stack_heads(q_ref[rows, cols].astype(BF16))
                    do2 = do_ref[rows, cols].astype(F32)
                    prod = do2 * o_ref[rows, cols].astype(F32)
                    if has_dl:
                        prod = prod - dl_ref[rows, cols].astype(F32)
                    delta = _per_head_sum(prod)
                    lse_rows = _per_head_col(l_ref[rows, cols])
                    dos = _stack_heads(do2.astype(BF16))
                    s = lax.dot_general(qs, kcat, _NT, preferred_element_type=F32) * scale
                    pm = jnp.where(msk, jnp.exp(s - lse_rows), 0.0)
                    dp = lax.dot_general(dos, vcat, _NT, preferred_element_type=F32)
                    ds = (pm * (dp - delta) * scale).astype(BF16)
                    dq_ref[rows, cols] = _unstack_heads(jnp.dot(ds, kcat, preferred_element_type=F32)).astype(dq_ref.dtype)

    dq = _pcall(
        body, name=name, grid=plan.grid, in_specs=specs, out_specs=plan.out(wq),
        out_shape=jax.ShapeDtypeStruct((S // dil, dil * wq), out_dtype), compiler_params=_cparams(("parallel", "parallel")),
    )(*[o_[0] for o_ in ops])
    return dq.reshape(S, wq)


def _band_dkv(q, k, v, do, o, lse, dlse, *, S, dil, max_dist, hq, hk, scale, out_dtype=F32, name):
    hd = 64
    plan = _BandPlan(S, dil, max_dist)
    qv, kv, vv = plan.view(q), plan.view(k), plan.view(v)
    dov, ov, lv = plan.view(do), plan.view(o), plan.view(lse)
    has_dl = dlse is not None
    wq, wk = hq * hd, hk * hd
    grp = hq // hk
    ns, nb = plan.ns, plan.nb
    qlike = [qv, dov, ov, lv] + ([plan.view(dlse)] if has_dl else [])
    ops = [kv, vv] + qlike + qlike
    specs = [plan.main(kv), plan.main(vv)] + [plan.main(t) for t in qlike] + [plan.edge(t, True) for t in qlike]
    nql = len(qlike)
    nkg = wk // LANE

    def body(*refs):
        k_ref, v_ref = refs[:2]
        mains, edges = refs[2:2 + nql], refs[2 + nql:2 + 2 * nql]
        dk_ref, dv_ref = refs[2 + 2 * nql:]
        i = pl.program_id(1)
        m_cur, m_band = plan.masks()
        has_next = i < plan.grid[1] - 1
        for sg in range(ns):
            for b in range(nb):
                rows = slice(b * BLOCK, (b + 1) * BLOCK)
                nxt_in_main = b + 1 < nb
                nrows = slice((b + 1) * BLOCK, (b + 2) * BLOCK) if nxt_in_main else slice(0, BLOCK)
                nsrc = mains if nxt_in_main else edges
                m_next = m_band if nxt_in_main else (m_band & has_next)
                msk = jnp.concatenate([m_cur, m_cur, m_next, m_next], axis=0)
                nacc = nkg if hk == hq else hk
                dk_acc, dv_acc = [None] * nacc, [None] * nacc
                for pr in range(wq // LANE):
                    cols = slice(sg * wq + pr * LANE, sg * wq + (pr + 1) * LANE)
                    kop = _kv_group(k_ref, rows, sg * wk, pr, grp, hq, hk)
                    vop = _kv_group(v_ref, rows, sg * wk, pr, grp, hq, hk)
                    qs, dos, deltas, lses = [], [], [], []
                    for src, r in ((mains, rows), (nsrc, nrows)):
                        qs.append(_stack_heads(src[0][r, cols].astype(BF16)))
                        do2 = src[1][r, cols].astype(F32)
                        prod = do2 * src[2][r, cols].astype(F32)
                        if has_dl:
                            prod = prod - src[4][r, cols].astype(F32)
                        deltas.append(_per_head_sum(prod))
                        lses.append(_per_head_col(src[3][r, cols]))
                        dos.append(_stack_heads(do2.astype(BF16)))
                    qs4, dos4 = jnp.concatenate(qs, axis=0), jnp.concatenate(dos, axis=0)
                    delta4, lse4 = jnp.concatenate(deltas, axis=0), jnp.concatenate(lses, axis=0)
                    s = lax.dot_general(qs4, kop, _NT, preferred_element_type=F32) * scale
                    pm = jnp.where(msk, jnp.exp(s - lse4), 0.0)
                    dp = lax.dot_general(dos4, vop, _NT, preferred_element_type=F32)
                    ds = (pm * (dp - delta4) * scale).astype(BF16)
                    tv = lax.dot_general(pm.astype(BF16), dos4, _TN, preferred_element_type=F32)
                    tk = lax.dot_general(ds, qs4, _TN, preferred_element_type=F32)
                    ai = pr if hk == hq else (2 * pr) // grp
                    dv_acc[ai] = tv if dv_acc[ai] is None else dv_acc[ai] + tv
                    dk_acc[ai] = tk if dk_acc[ai] is None else dk_acc[ai] + tk
                for kg in range(nkg):
                    cols = slice(sg * wk + kg * LANE, sg * wk + (kg + 1) * LANE)
                    if hk == hq:
                        dkt, dvt = dk_acc[kg], dv_acc[kg]
                    else:
                        both = lambda t: t + _swap_halves(t)
                        h0 = _half(0)
                        dkt = jnp.where(h0, both(dk_acc[2 * kg]), both(dk_acc[2 * kg + 1]))
                        dvt = jnp.where(h0, both(dv_acc[2 * kg]), both(dv_acc[2 * kg + 1]))
                    dk_ref[rows, cols] = dkt.astype(dk_ref.dtype)
                    dv_ref[rows, cols] = dvt.astype(dv_ref.dtype)

    dk, dv = _pcall(
        body, name=name, grid=plan.grid, in_specs=specs, out_specs=[plan.out(wk)] * 2,
        out_shape=[jax.ShapeDtypeStruct((S // dil, dil * wk), out_dtype)] * 2,
        compiler_params=_cparams(("parallel", "parallel")),
    )(*[o_[0] for o_ in ops])
    return dk.reshape(S, wk), dv.reshape(S, wk)


def _causal_block(S):
    return min(512, S)


def _causal_mask(bq):
    qi = lax.broadcasted_iota(jnp.int32, (bq, bq), 0)
    kj = lax.broadcasted_iota(jnp.int32, (bq, bq), 1)
    return kj <= qi


def _half_of(rows, e):
    lane = lax.broadcasted_iota(jnp.int32, (rows, LANE), 1)
    return (lane < LANE // 2) if e == 0 else (lane >= LANE // 2)


def _causal_fwd(q, k, v, *, heads, scale, name, bk=None):
    S = q.shape[0]
    bq = _causal_block(S)
    bk = bq if bk is None else min(bk, bq)
    r = bq // bk
    nq, nk = S // bq, S // bk
    npair = heads // 2
    wv = heads * 64
    c2 = scale * math.log2(math.e)

    def body(q_ref, k_ref, v_ref, o_ref, l_ref, m_sc, l_sc, acc_sc):
        i, t = pl.program_id(0), pl.program_id(1)

        @pl.when(t == 0)
        def _():
            m_sc[...] = jnp.full(m_sc.shape, NEG, F32)
            l_sc[...] = jnp.zeros(l_sc.shape, F32)
            acc_sc[...] = jnp.zeros(acc_sc.shape, F32)

        def step(masked):
            if masked:
                qi = lax.broadcasted_iota(jnp.int32, (bq, bk), 0)
                kj = lax.broadcasted_iota(jnp.int32, (bq, bk), 1) + (t - r * i) * bk
                msk = kj <= qi
            h0 = _half_of(bq, 0)
            for pr in range(npair):
                v2 = v_ref[:, pr * LANE:(pr + 1) * LANE].astype(BF16)
                new = []
                for e in range(2):
                    h = 2 * pr + e
                    cols = slice(h * LANE, (h + 1) * LANE)
                    s = lax.dot_general(q_ref[:, cols], k_ref[:, cols], _NT, preferred_element_type=F32)
                    if masked:
                        s = jnp.where(msk, s, NEG)
                    m_old = m_sc[h]
                    m_new = jnp.maximum(m_old, jnp.max(s, axis=1, keepdims=True))
                    p = jnp.exp2((s - m_new) * c2)
                    alpha = jnp.exp2((m_old - m_new) * c2)
                    l_sc[h] = alpha * l_sc[h] + jnp.sum(p, axis=1, keepdims=True)
                    m_sc[h] = m_new
                    new.append((alpha, jnp.dot(p.astype(BF16), v2, preferred_element_type=F32)))
                acc = acc_sc[pr]
                acc_sc[pr] = jnp.where(h0, new[0][0] * acc + new[0][1], new[1][0] * acc + new[1][1])

        pl.when(t < r * i)(lambda: step(False))
        pl.when((t >= r * i) & (t < r * (i + 1)))(lambda: step(True))

        @pl.when(t == nk - 1)
        def _():
            h0 = _half_of(bq, 0)
            for pr in range(npair):
                l0, l1 = l_sc[2 * pr], l_sc[2 * pr + 1]
                acc = acc_sc[pr]
                cols = slice(pr * LANE, (pr + 1) * LANE)
                o_ref[:, cols] = jnp.where(h0, acc / l0, acc / l1)
                l_ref[:, cols] = jnp.where(h0, m_sc[2 * pr] * scale + jnp.log(l0), m_sc[2 * pr + 1] * scale + jnp.log(l1))

    kvi = lambda i, t: (jnp.minimum(t, r * (i + 1) - 1), 0)
    qs = pl.BlockSpec((bq, heads * LANE), lambda i, t: (i, 0))
    ks = pl.BlockSpec((bk, heads * LANE), kvi)
    vs = pl.BlockSpec((bk, wv), kvi)
    os_ = pl.BlockSpec((bq, wv), lambda i, t: (i, 0))
    return _pcall(
        body, name=name, grid=(nq, nk), in_specs=[qs, ks, vs], out_specs=[os_, os_],
        out_shape=[jax.ShapeDtypeStruct((S, wv), F32)] * 2,
        scratch_shapes=[pltpu.VMEM((heads, bq, 1), F32), pltpu.VMEM((heads, bq, 1), F32), pltpu.VMEM((npair, bq, LANE), F32)],
        compiler_params=_cparams(("parallel", "arbitrary")),
    )(q, k, v)


def _causal_bwd_tile(q_ref, k_ref, v2, do2, prod, l2, h, e, scale, msk, bq):
    cols = slice(h * LANE, (h + 1) * LANE)
    hm = _half_of(bq, e)
    s = lax.dot_general(q_ref[:, cols], k_ref[:, cols], _NT, preferred_element_type=F32)
    c2 = scale * math.log2(math.e)
    p = jnp.exp2((s - l2[:, e * 64:e * 64 + 1] * (1.0 / scale)) * c2)
    if msk is not None:
        p = jnp.where(msk, p, 0.0)
    dom = jnp.where(hm, do2, jnp.zeros_like(do2))
    delta = jnp.sum(jnp.where(hm, prod, 0.0), axis=1, keepdims=True)
    dp = lax.dot_general(dom, v2, _NT, preferred_element_type=F32)
    ds = (p * (dp - delta) * scale).astype(BF16)
    return p, ds, dom


def _causal_dq(q, k, v, do, o, lse, *, heads, scale, out_dtype=BF16, name):
    S = q.shape[0]
    bq = _causal_block(S)
    nq = S // bq
    npair = heads // 2
    wv = heads * 64

    def body(q_ref, k_ref, v_ref, do_ref, o_ref, l_ref, dq_ref, dq_sc):
        i, t = pl.program_id(0), pl.program_id(1)

        @pl.when(t == 0)
        def _():
            dq_sc[...] = jnp.zeros(dq_sc.shape, F32)

        def step(masked):
            msk = _causal_mask(bq) if masked else None
            for pr in range(npair):
                pc = slice(pr * LANE, (pr + 1) * LANE)
                v2 = v_ref[:, pc].astype(BF16)
                dof = do_ref[:, pc].astype(F32)
                prod = dof * o_ref[:, pc]
                do2 = dof.astype(BF16)
                l2 = l_ref[:, pc]
                for e in range(2):
                    h = 2 * pr + e
                    _, ds, _ = _causal_bwd_tile(q_ref, k_ref, v2, do2, prod, l2, h, e, scale, msk, bq)
                    dq_sc[h] += jnp.dot(ds, k_ref[:, h * LANE:(h + 1) * LANE], preferred_element_type=F32)

        pl.when(t < i)(lambda: step(False))
        pl.when(t == i)(lambda: step(True))

        @pl.when(t == nq - 1)
        def _():
            for h in range(heads):
                dq_ref[:, h * LANE:(h + 1) * LANE] = dq_sc[h].astype(dq_ref.dtype)

    qs = pl.BlockSpec((bq, heads * LANE), lambda i, t: (i, 0))
    ks = pl.BlockSpec((bq, heads * LANE), lambda i, t: (jnp.minimum(t, i), 0))
    vs = pl.BlockSpec((bq, wv), lambda i, t: (jnp.minimum(t, i), 0))
    os_ = pl.BlockSpec((bq, wv), lambda i, t: (i, 0))
    return _pcall(
        body, name=name, grid=(nq, nq), in_specs=[qs, ks, vs, os_, os_, os_], out_specs=qs,
        out_shape=jax.ShapeDtypeStruct((S, heads * LANE), out_dtype),
        scratch_shapes=[pltpu.VMEM((heads, bq, LANE), F32)],
        compiler_params=_cparams(("parallel", "arbitrary")),
    )(q, k, v, do, o, lse)


def _causal_dkv(q, k, v, do, o, lse, *, heads, scale, out_dtype=BF16, name):
    S = q.shape[0]
    bq = _causal_block(S)
    nq = S // bq
    npair = heads // 2
    wv = heads * 64

    def body(q_ref, k_ref, v_ref, do_ref, o_ref, l_ref, dk_ref, dv_ref, dk_sc, dv_sc):
        j, t = pl.program_id(0), pl.program_id(1)

        @pl.when(t == 0)
        def _():
            dk_sc[...] = jnp.zeros(dk_sc.shape, F32)
            dv_sc[...] = jnp.zeros(dv_sc.shape, F32)

        def step(masked):
            msk = _causal_mask(bq) if masked else None
            for pr in range(npair):
                pc = slice(pr * LANE, (pr + 1) * LANE)
                v2 = v_ref[:, pc].astype(BF16)
                dof = do_ref[:, pc].astype(F32)
                prod = dof * o_ref[:, pc]
                do2 = dof.astype(BF16)
                l2 = l_ref[:, pc]
                dv_add = None
                for e in range(2):
                    h = 2 * pr + e
                    p, ds, dom = _causal_bwd_tile(q_ref, k_ref, v2, do2, prod, l2, h, e, scale, msk, bq)
                    tv = lax.dot_general(p.astype(BF16), dom, _TN, preferred_element_type=F32)
                    dv_add = tv if dv_add is None else dv_add + tv
                    dk_sc[h] += lax.dot_general(ds, q_ref[:, h * LANE:(h + 1) * LANE], _TN, preferred_element_type=F32)
                dv_sc[pr] += dv_add

        pl.when(t > j)(lambda: step(False))
        pl.when(t == j)(lambda: step(True))

        @pl.when(t == nq - 1)
        def _():
            for h in range(heads):
                dk_ref[:, h * LANE:(h + 1) * LANE] = dk_sc[h].astype(dk_ref.dtype)
            for pr in range(npair):
                dv_ref[:, pr * LANE:(pr + 1) * LANE] = dv_sc[pr].astype(dv_ref.dtype)

    qi = lambda j, t: (jnp.maximum(t, j), 0)
    qs = pl.BlockSpec((bq, heads * LANE), qi)
    os_ = pl.BlockSpec((bq, wv), qi)
    ks = pl.BlockSpec((bq, heads * LANE), lambda j, t: (j, 0))
    vs = pl.BlockSpec((bq, wv), lambda j, t: (j, 0))
    return _pcall(
        body, name=name, grid=(nq, nq), in_specs=[qs, ks, vs, os_, os_, os_], out_specs=[ks, vs],
        out_shape=[jax.ShapeDtypeStruct((S, heads * LANE), out_dtype), jax.ShapeDtypeStruct((S, wv), out_dtype)],
        scratch_shapes=[pltpu.VMEM((heads, bq, LANE), F32), pltpu.VMEM((npair, bq, LANE), F32)],
        compiler_params=_cparams(("parallel", "arbitrary")),
    )(q, k, v, do, o, lse)


@jax.custom_vjp
def _bdot(x, w):
    return jnp.dot(x.astype(BF16), w.astype(BF16), preferred_element_type=F32)


def _bdot_fwd(x, w):
    return _bdot(x, w), (x, w)


def _bdot_bwd(res, g):
    x, w = res
    gb = g.astype(BF16)
    dx = lax.dot_general(gb, w.astype(BF16), _NT, preferred_element_type=F32)
    dw = lax.dot_general(x.astype(BF16), gb, _TN, preferred_element_type=F32)
    return dx.astype(x.dtype), dw.astype(w.dtype)


_bdot.defvjp(_bdot_fwd, _bdot_bwd)


def _tile_matrix(hd, width):
    r = lax.broadcasted_iota(jnp.int32, (hd, width), 0)
    c = lax.broadcasted_iota(jnp.int32, (hd, width), 1) % hd
    return jnp.where(r == c, 1.0, 0.0).astype(F32)


def _spread_matrix(heads, width):
    per = width // heads
    r = lax.broadcasted_iota(jnp.int32, (heads, width), 0)
    c = lax.broadcasted_iota(jnp.int32, (heads, width), 1) // per
    return jnp.where(r == c, 1.0, 0.0).astype(F32)


def _wide(t, width):
    n = width // t.shape[-1]
    return jnp.concatenate([t] * n, axis=1) if n > 1 else t


def _norm_heads(x, gain, hd):
    return _head_norm(x, _exact_dot(gain, _tile_matrix(hd, x.shape[-1])), hd)


def _norm_rope(x, gain, cos, sin, hd):
    w = x.shape[-1]
    return _rope(_norm_heads(x, gain, hd), _wide(cos, w), _wide(sin, w), hd)


def _f_norm(x, g):
    return (_row_norm(x, g),)


def _f_prep_acm(aq, ak, c0q, c0k, c0v, c1q, c1k, c1v, c2q, c2k, c2v, mq, cos, sin, g_aq, g_ak, g0q, g0k, g1q, g1k, g2q, g2k, g_mq):
    outs = [_norm_rope(aq, g_aq, cos, sin, A_HD), _norm_rope(ak, g_ak, cos, sin, A_HD)]
    for cq, ck, cv, gq, gk in ((c0q, c0k, c0v, g0q, g0k), (c1q, c1k, c1v, g1q, g1k), (c2q, c2k, c2v, g2q, g2k)):
        outs += [_norm_rope(cq, gq, cos, sin, C_HD), _norm_rope(ck, gk, cos, sin, C_HD), cv]
    outs.append(_norm_heads(mq, g_mq, M_HD))
    return tuple(outs)


def _rope_mla_q(x, cos, sin):
    lane = lax.broadcasted_iota(jnp.int32, x.shape, x.ndim - 1) % LANE
    half = B_ROPE // 2
    first = (lane >= B_NOPE) & (lane < B_NOPE + half)
    other = jnp.where(first, -_lane_roll(x, -half), _lane_roll(x, half))
    return x * cos + other * sin


def _f_prep_b(ckv, cq, kr, cosq, sinq, cosr, sinr, g_qa, g_kva, w_uq, w_ukv, g_q, g_k, g_kr):
    wq = B_HEADS * LANE
    q_up = _bdot(_row_norm(cq, g_qa), w_uq)
    gq = _exact_dot(g_q, _tile_matrix(LANE, wq))
    qf = _rope_mla_q(_head_norm(q_up, gq, "mla"), _wide(cosq, wq), _wide(sinq, wq))
    kv_up = _bdot(_row_norm(ckv, g_kva), w_ukv)
    kn = _head_norm(kv_up[:, :wq], _exact_dot(g_k, _tile_matrix(LANE, wq)), B_NOPE)
    vb = kv_up[:, wq:]
    kp = _rope(_head_norm(kr, g_kr, B_ROPE), cosr, sinr, B_ROPE)
    kp = _lane_roll(kp, B_NOPE)
    return qf, kn + _wide(kp, wq), vb


def _f_mem_k(k, g):
    return (_norm_heads(k, g, M_HD),)


def _f_sink(o, lse, sink):
    sb = _exact_dot(sink, _spread_matrix(A_HEADS, o.shape[-1]))
    m = jnp.maximum(lse, sb)
    tot = m + jnp.log(jnp.exp(lse - m) + jnp.exp(sb - m))
    return (o * jnp.exp(lse - tot),)


def _f_combine(o0, o1, o2, l0, l1, l2):
    m = jnp.maximum(jnp.maximum(l0, l1), l2)
    e0, e1, e2 = jnp.exp(l0 - m), jnp.exp(l1 - m), jnp.exp(l2 - m)
    inv = 1.0 / (e0 + e1 + e2)
    return ((e0 * o0 + e1 * o1 + e2 * o2) * inv,)


def _f_gatemix(gp, y0, y1, y2, y3, bg):
    d = y0.shape[-1]
    gates = 1.0 / (1.0 + jnp.exp(-(gp + bg)))
    mix = gates[:, :d] * y0
    for n, y in enumerate((y1, y2, y3), start=1):
        mix = mix + gates[:, n * d:(n + 1) * d] * y
    return (mix,)


def _relu2(u):
    return jnp.square(jnp.maximum(u, 0.0))


def _add(r, e):
    return r + e.astype(F32)


def _relu2_grad(r, u):
    return r * (2.0 * jnp.maximum(u, 0.0))


def _loss_and_grad(y, target, *, tb=512):
    R, D = y.shape
    tb = min(tb, R)

    def body(y_ref, t_ref, dy_ref, l_ref):
        err = y_ref[...] - t_ref[...]
        dy_ref[...] = err * (1.0 / D)
        part = 0.5 * jnp.sum(jnp.sum(err * err, axis=1, keepdims=True) * (1.0 / D), axis=0, keepdims=True)
        first = pl.program_id(0) == 0

        @pl.when(first)
        def _():
            l_ref[...] = jnp.broadcast_to(part, l_ref.shape)

        @pl.when(jnp.logical_not(first))
        def _():
            l_ref[...] += jnp.broadcast_to(part, l_ref.shape)

    dy, l = _pcall(
        body, name="loss", grid=(R // tb,),
        in_specs=[pl.BlockSpec((tb, D), lambda i: (i, 0))] * 2,
        out_specs=[pl.BlockSpec((tb, D), lambda i: (i, 0)), pl.BlockSpec((8, LANE), lambda i: (0, 0))],
        out_shape=[jax.ShapeDtypeStruct((R, D), F32), jax.ShapeDtypeStruct((8, LANE), F32)],
        compiler_params=_cparams(("arbitrary",)),
    )(y, target)
    return l[0, 0], dy


def _z_layout(d):
    assert d == 1024, "the aligned layout below is laid out for D_MODEL = 1024"
    lay = {"gates": (4 * d, 0)}
    for g in range(3):
        for n, nm in enumerate("qkv"):
            lay[f"c{g}{nm}"] = (512, 8 + 3 * g + n)
    lay.update(aq=(512, 17), mq=(512, 18), ckv=(256, 38), cq=(384, 26), ak=(128, 81), av=(128, 82), kr=(128, 83))
    return lay, 10752


_KW_A = dict(hq=A_HEADS, hk=A_KV_HEADS, scale=A_HD ** -0.5)
_KW_B = dict(heads=B_HEADS, scale=(B_NOPE + B_ROPE) ** -0.5)
_KW_C = dict(hq=C_HEADS, hk=C_HEADS, scale=C_HD ** -0.5)
_KW_M = dict(hq=M_HEADS, hk=M_HEADS, hd=M_HD, hdv=M_HD, scale=M_HD ** -0.5)


def _layer_fwd(l, x, mem, w, tabs, late=None):
    S, D = x.shape
    lay, _ = _z_layout(D)
    cosA, sinA, cosB, sinB, cosQ, sinQ = tabs
    gM = _AttnGeom("full", S, mem.shape[0])
    nm = lambda s: f"l{l}_{s}"
    sv = {}
    hn = _rowmap(_f_norm, [x], [w["g_mix"]], [(D, BF16)], tb=512, name=nm("norm1"))[0]
    z = _mm(hn, w["in"], name=nm("in"), tn=1536)
    zp = {k: (z, wd, idx) for k, (wd, idx) in lay.items()}
    acm_rows = [zp[k] for k in ("aq", "ak", "c0q", "c0k", "c0v", "c1q", "c1k", "c1v", "c2q", "c2k", "c2v", "mq")] + [cosA, sinA]
    acm_par = [w["a_qn"], w["a_kn"], w["c0q"], w["c0k"], w["c1q"], w["c1k"], w["c2q"], w["c2k"], w["m_qn"]]
    acm = _rowmap(_f_prep_acm, acm_rows, acm_par, [(p[1], BF16) for p in acm_rows[:12]], tb=256, name=nm("prep_acm"))
    qa, ka, qc0, kc0, vc0, qc1, kc1, vc1, qc2, kc2, vc2, mq = acm
    oa_raw, lse_a = _band_fwd(qa, ka, zp["av"], S=S, dil=1, max_dist=A_WINDOW - 1, name=nm("attn_a"), **_KW_A)
    o_a = _rowmap(_f_sink, [oa_raw, lse_a], [w["a_sink"]], [(512, BF16)], tb=512, name=nm("sink"))[0]
    oc, lc = [], []
    for g, ((win, dil), qc, kc, vc) in enumerate(zip(C_PATTERNS, (qc0, qc1, qc2), (kc0, kc1, kc2), (vc0, vc1, vc2))):
        o_g, l_g = _band_fwd(qc, kc, vc, S=S, dil=dil, max_dist=win // dil, name=nm(f"attn_c{g}"), **_KW_C)
        oc.append(o_g)
        lc.append(l_g)
    o_c = _rowmap(_f_combine, oc + lc, [], [(512, BF16)], tb=512, name=nm("combine"))[0]
    if late is not None:
        w = dict(w, **late(o_c))

    b_rows = [zp["ckv"], zp["cq"], zp["kr"], cosQ, sinQ, cosB, sinB]
    b_par = [w["b_qa"], w["b_kva"], w["uq"], w["ukv"], w["b_q"], w["b_k"], w["b_kr"]]
    qb, kb, vb = _rowmap(_f_prep_b, b_rows, b_par, [(B_HEADS * LANE, BF16), (B_HEADS * LANE, BF16), (512, BF16)], tb=256,
                         name=nm("prep_b"))
    memn = _rowmap(_f_norm, [mem], [w["m_g_mem"]], [(D, BF16)], tb=256, name=nm("mem_norm"))[0]
    mkv = _mm(memn, w["mkv"], name=nm("mem_kv"))
    mk = _rowmap(_f_mem_k, [(mkv, 512, 0)], [w["m_kn"]], [(512, BF16)], tb=256, name=nm("mem_k"))[0]
    mv = (mkv, 512, 1)
    o_b, lse_b = _causal_fwd(qb, kb, vb, name=nm("attn_b"), **_KW_B)
    o_m, lse_m = _attn_fwd(gM, mq, mk, mv, name=nm("attn_m"), **_KW_M)

    o_n = [o_a, o_b, o_c, o_m]
    ys = [_mm(o_n[n], w["branch"][n], name=nm(f"branch{n}")) for n in range(N_BRANCH)]
    mix = _rowmap(_f_gatemix, [zp["gates"]] + ys, [w["b_gate"]], [(D, BF16)], tb=256, name=nm("gatemix"))[0]
    x1 = _mm(mix, w["out"], extra=x, epi=_add, name=nm("out"))
    hn2 = _rowmap(_f_norm, [x1], [w["g_mlp"]], [(D, BF16)], tb=512, name=nm("norm2"))[0]
    u = _mm(hn2, w["up"], name=nm("up"))
    x2 = _mm(u, w["down"], pro_a=_relu2, extra=x1, epi=_add, name=nm("down"))
    sv.update(x=x, hn=hn, z=z, acm=acm, bq=(qb, kb, vb), memn=memn, mkv=mkv, mk=mk, oa_raw=oa_raw, lse_a=lse_a,
              o_b=o_b, lse_b=lse_b, oc=oc, lc=lc, o_m=o_m, lse_m=lse_m, o_n=o_n, ys=ys, mix=mix, x1=x1, hn2=hn2, u=u)
    return x2, sv, w


def _layer_bwd(l, dx2, mem, w, tabs, sv):
    x, z, x1, u = sv["x"], sv["z"], sv["x1"], sv["u"]
    S, D = x.shape
    lay, zw = _z_layout(D)
    cosA, sinA, cosB, sinB, cosQ, sinQ = tabs
    gM = _AttnGeom("full", S, mem.shape[0])
    nm = lambda s: f"l{l}_{s}"
    zp = {k: (z, wd, idx) for k, (wd, idx) in lay.items()}
    g = {}
    du = _mm(dx2, w["down"], tb=True, extra=u, epi=_relu2_grad, out_dtype=BF16, name=nm("d_down_x"))
    g["down"] = _mm(u, dx2, ta=True, pro_a=_relu2, name=nm("d_down_w"))
    dhn2 = _mm(du, w["up"], tb=True, name=nm("d_up_x"))
    g["up"] = _mm(sv["hn2"], du, ta=True, name=nm("d_up_w"))
    (dx1,), (g["g_mlp"],) = _rowmap_bwd(_f_norm, [x1], [w["g_mlp"]], [dhn2], diff=[True], out_dtypes=[F32], adds=[dx2],
                                        tb=256, name=nm("d_norm2"))
    dmix = _mm(dx1, w["out"], tb=True, name=nm("d_out_x"))
    g["out"] = _mm(sv["mix"], dx1, ta=True, name=nm("d_out_w"))
    (dgates, dy0, dy1, dy2, dy3), (g["b_gate"],) = _rowmap_bwd(
        _f_gatemix, [zp["gates"]] + sv["ys"], [w["b_gate"]], [dmix], diff=[True] * 5, out_dtypes=[BF16] * 5,
        tb=128, name=nm("d_gatemix"))
    dys = [dy0, dy1, dy2, dy3]
    do = [_mm(dys[n], w["branch"][n], tb=True, name=nm(f"d_branch{n}_x")) for n in range(N_BRANCH)]
    g["branch"] = [_mm(sv["o_n"][n], dys[n], ta=True, name=nm(f"d_branch{n}_w")) for n in range(N_BRANCH)]
    qa, ka, qc0, kc0, vc0, qc1, kc1, vc1, qc2, kc2, vc2, mq = sv["acm"]
    qb, kb, vb = sv["bq"]
    mkv, mk = sv["mkv"], sv["mk"]
    mv = (mkv, 512, 1)
    dmq = _attn_dq(gM, mq, mk, mv, do[3], sv["o_m"], sv["lse_m"], None, name=nm("attn_m_dq"), **_KW_M)
    dmk, dmv = _attn_dkv(gM, mq, mk, mv, do[3], sv["o_m"], sv["lse_m"], None, name=nm("attn_m_dkv"), **_KW_M)
    (doc0, doc1, doc2, dl0, dl1, dl2), _ = _rowmap_bwd(_f_combine, sv["oc"] + sv["lc"], [], [do[2]], diff=[True] * 6,
                                                      out_dtypes=[F32] * 6, tb=256, name=nm("d_combine"))
    dqc, dkc, dvc = [], [], []
    for gi, ((win, dil), qc, kc, vc, doc, dl) in enumerate(zip(C_PATTERNS, (qc0, qc1, qc2), (kc0, kc1, kc2), (vc0, vc1, vc2),
                                                              (doc0, doc1, doc2), (dl0, dl1, dl2))):
        args = (qc, kc, vc, doc, sv["oc"][gi], sv["lc"][gi], dl)
        kwc = dict(S=S, dil=dil, max_dist=win // dil, out_dtype=BF16, **_KW_C)
        dqc.append(_band_dq(*args, name=nm(f"attn_c{gi}_dq"), **kwc))
        dk_, dv_ = _band_dkv(*args, name=nm(f"attn_c{gi}_dkv"), **kwc)
        dkc.append(dk_)
        dvc.append(dv_)
    argsb = (qb, kb, vb, do[1], sv["o_b"], sv["lse_b"])
    dqb = _causal_dq(*argsb, name=nm("attn_b_dq"), **_KW_B)
    dkb, dvb = _causal_dkv(*argsb, name=nm("attn_b_dkv"), **_KW_B)
    (doa_raw, dlse_a), (g["a_sink"],) = _rowmap_bwd(_f_sink, [sv["oa_raw"], sv["lse_a"]], [w["a_sink"]], [do[0]],
                                                   diff=[True, True], out_dtypes=[F32, F32], tb=256, name=nm("d_sink"))
    argsa = (qa, ka, zp["av"], doa_raw, sv["oa_raw"], sv["lse_a"], dlse_a)
    kwa = dict(S=S, dil=1, max_dist=A_WINDOW - 1, out_dtype=BF16, **_KW_A)
    dqa = _band_dq(*argsa, name=nm("attn_a_dq"), **kwa)
    dka, dva = _band_dkv(*argsa, name=nm("attn_a_dkv"), **kwa)
    acm_rows = [zp[k] for k in ("aq", "ak", "c0q", "c0k", "c0v", "c1q", "c1k", "c1v", "c2q", "c2k", "c2v", "mq")] + [cosA, sinA]
    acm_par = [w["a_qn"], w["a_kn"], w["c0q"], w["c0k"], w["c1q"], w["c1k"], w["c2q"], w["c2k"], w["m_qn"]]
    acm_ct = [dqa, dka, dqc[0], dkc[0], dvc[0], dqc[1], dkc[1], dvc[1], dqc[2], dkc[2], dvc[2], dmq]
    dacm, (g["a_qn"], g["a_kn"], g["c0q"], g["c0k"], g["c1q"], g["c1k"], g["c2q"], g["c2k"], g["m_qn"]) = _rowmap_bwd(
        _f_prep_acm, acm_rows, acm_par, acm_ct, diff=[True] * 12 + [False, False], out_dtypes=[BF16] * 12, tb=256,
        name=nm("d_prep_acm"))
    d_aq, d_ak, d_c0q, d_c0k, d_c0v, d_c1q, d_c1k, d_c1v, d_c2q, d_c2k, d_c2v, d_mq = dacm
    b_rows = [zp["ckv"], zp["cq"], zp["kr"], cosQ, sinQ, cosB, sinB]
    b_par = [w["b_qa"], w["b_kva"], w["uq"], w["ukv"], w["b_q"], w["b_k"], w["b_kr"]]
    (d_ckv, d_cq, d_kr), gb = _rowmap_bwd(_f_prep_b, b_rows, b_par, [dqb, dkb, dvb], diff=[True] * 3 + [False] * 4,
                                          out_dtypes=[BF16] * 3, tb=256, name=nm("d_prep_b"))
    g["b_qa"], g["b_kva"], g["uq"], g["ukv"], g["b_q"], g["b_k"], g["b_kr"] = gb
    (dmkv_k,), (g["m_kn"],) = _rowmap_bwd(_f_mem_k, [(mkv, 512, 0)], [w["m_kn"]], [dmk], diff=[True], out_dtypes=[F32],
                                          tb=256, name=nm("d_mem_k"))
    dmkv = jnp.concatenate([dmkv_k, dmv], axis=1)
    dmemn = _mm(dmkv, w["mkv"], tb=True, name=nm("d_mem_kv_x"))
    g["mkv"] = _mm(sv["memn"], dmkv, ta=True, name=nm("d_mem_kv_w"))
    _, (g["m_g_mem"],) = _rowmap_bwd(_f_norm, [mem], [w["m_g_mem"]], [dmemn], diff=[True], out_dtypes=[F32], tb=256,
                                     name=nm("d_mem_norm"))
    dz = jnp.concatenate([dgates, d_c0q, d_c0k, d_c0v, d_c1q, d_c1k, d_c1v, d_c2q, d_c2k, d_c2v, d_aq, d_mq, d_ckv, d_cq,
                          d_ak, dva, d_kr], axis=1)
    assert dz.shape[1] == zw
    dhn = _mm(dz, w["in"], tb=True, name=nm("d_in_x"), tk=1536)
    g["in"] = _mm(sv["hn"], dz, ta=True, name=nm("d_in_w"), tn=1536)
    (dx,), (g["g_mix"],) = _rowmap_bwd(_f_norm, [x], [w["g_mix"]], [dhn], diff=[True], out_dtypes=[F32], adds=[dx1],
                                       tb=256, name=nm("d_norm1"))
    return dx, g


_IN_ORIG = dict(aq=(0, 512), ak=(512, 640), av=(640, 768), cq=(768, 1152), ckv=(1152, 1408), kr=(1408, 1440),
                c=(1440, 6048), mq=(6048, 6560), gates=(6560, 10656))
_IN_OURS = dict(gates=(0, 4096), c=(4096, 8704), aq=(8704, 9216), mq=(9216, 9728), ckv=(9728, 9984), cq=(9984, 10368),
                ak=(10368, 10496), av=(10496, 10624), kr=(10624, 10656))
_IN_ORDER_ORIG = ("aq", "ak", "av", "cq", "ckv", "kr", "c", "mq", "gates")


def _in_to_ours(w_in):
    pc = {k: w_in[..., a:b] for k, (a, b) in _IN_ORIG.items()}
    zeros = lambda n: jnp.zeros(w_in.shape[:-1] + (n,), w_in.dtype)
    return jnp.concatenate([pc["gates"], pc["c"], pc["aq"], pc["mq"], pc["ckv"], pc["cq"], pc["ak"], pc["av"], pc["kr"],
                            zeros(96)], axis=-1)


def _in_from_ours(g_in):
    return jnp.concatenate([g_in[..., _IN_OURS[k][0]:_IN_OURS[k][1]] for k in _IN_ORDER_ORIG], axis=-1)


def _uq_to_ours(w):
    per = B_NOPE + B_ROPE
    w4 = w.reshape(w.shape[:-1] + (B_HEADS, per))
    w4 = jnp.pad(w4, [(0, 0)] * (w4.ndim - 1) + [(0, LANE - per)])
    return w4.reshape(w.shape[:-1] + (B_HEADS * LANE,))


def _uq_from_ours(g):
    per = B_NOPE + B_ROPE
    g4 = g.reshape(g.shape[:-1] + (B_HEADS, LANE))[..., :per]
    return g4.reshape(g.shape[:-1] + (B_HEADS * per,))


def _ukv_to_ours(w):
    w4 = w.reshape(w.shape[:-1] + (B_HEADS, B_NOPE + B_V))
    keys = jnp.pad(w4[..., :B_NOPE], [(0, 0)] * (w4.ndim - 1) + [(0, LANE - B_NOPE)])
    vals = w4[..., B_NOPE:]
    return jnp.concatenate([keys.reshape(w.shape[:-1] + (B_HEADS * LANE,)), vals.reshape(w.shape[:-1] + (B_HEADS * B_V,))],
                           axis=-1)


def _ukv_from_ours(g):
    wq = B_HEADS * LANE
    keys = g[..., :wq].reshape(g.shape[:-1] + (B_HEADS, LANE))[..., :B_NOPE]
    vals = g[..., wq:].reshape(g.shape[:-1] + (B_HEADS, B_V))
    return jnp.concatenate([keys, vals], axis=-1).reshape(g.shape[:-1] + (B_HEADS * (B_NOPE + B_V),))


def _layer_weights(big_l, small, l):
    row = lambda a: a[l].reshape(1, -1)
    w = dict(big_l)
    w.update(g_mix=row(small["g_mix"]), b_gate=row(small["b_gate"]), a_qn=row(small["a_qn"]), a_kn=row(small["a_kn"]),
             a_sink=row(small["a_sink"]), b_qa=row(small["b_qa_norm"]), b_kva=row(small["b_kva_norm"]),
             b_q=jnp.pad(small["b_qn"][l], (0, LANE - B_NOPE - B_ROPE)).reshape(1, -1),
             b_k=jnp.pad(small["b_kn"][l, :B_NOPE], (0, LANE - B_NOPE)).reshape(1, -1),
             b_kr=jnp.pad(small["b_kn"][l, B_NOPE:], (0, LANE - B_ROPE)).reshape(1, -1),
             m_g_mem=row(small["m_g_mem"]), m_qn=row(small["m_qn"]), m_kn=row(small["m_kn"]), g_mlp=row(small["g_mlp"]))
    for g in range(3):
        w[f"c{g}q"] = small["c_qn"][l, g].reshape(1, -1)
        w[f"c{g}k"] = small["c_kn"][l, g].reshape(1, -1)
    return w


def _rope_tables(positions):
    pos = positions.astype(F32)[:, None]
    tabs = []
    for dim in (A_HD, B_ROPE):
        inv = ROPE_THETA ** (-jnp.arange(0, dim, 2, dtype=F32) / dim)
        ang = pos * inv
        reps = LANE // (dim // 2)
        tabs += [jnp.tile(jnp.cos(ang), (1, reps)), jnp.tile(jnp.sin(ang), (1, reps))]
    half = B_ROPE // 2
    cb, sb = tabs[2][:, :half], tabs[3][:, :half]
    ones, zeros = jnp.ones((pos.shape[0], B_NOPE), F32), jnp.zeros((pos.shape[0], B_NOPE), F32)
    pad = LANE - B_NOPE - B_ROPE
    tabs.append(jnp.concatenate([ones, cb, cb, ones[:, :pad]], axis=1))
    tabs.append(jnp.concatenate([zeros, sb, sb, zeros[:, :pad]], axis=1))
    return tuple(tabs)


def _local_step(x, mem, positions, small, loss_target, get_big, put_grads):
    depth = small["g_mix"].shape[0]
    tabs = _rope_tables(positions)
    ws, saved = [], []
    h = x
    for l in range(depth):
        first, late = get_big(l, h)
        h, sv, wl = _layer_fwd(l, h, mem, _layer_weights(first, small, l), tabs, late)
        ws.append(wl)
        saved.append(sv)
    loss, dh = _loss_and_grad(h, loss_target)
    small_grads = [None] * depth
    for l in reversed(range(depth)):
        dh, g = _layer_bwd(l, dh, mem, ws[l], tabs, saved[l])
        zero = put_grads(l, g)
        if l > 0:
            ws[l - 1] = dict(ws[l - 1], g_mlp=ws[l - 1]["g_mlp"] + zero)
        small_grads[l] = g
    return loss, dh, small_grads


def _small_grads_to_reference_layout(grads):
    flat = lambda k: jnp.stack([g[k].reshape(-1) for g in grads])
    return dict(
        g_mix=flat("g_mix"), b_gate=flat("b_gate"), a_qn=flat("a_qn"), a_kn=flat("a_kn"), a_sink=flat("a_sink"),
        b_qa_norm=flat("b_qa"), b_kva_norm=flat("b_kva"), b_qn=flat("b_q")[:, :B_NOPE + B_ROPE],
        b_kn=jnp.concatenate([flat("b_k")[:, :B_NOPE], flat("b_kr")[:, :B_ROPE]], axis=1),
        c_qn=jnp.stack([jnp.stack([g[f"c{i}q"].reshape(-1) for i in range(3)]) for g in grads]),
        c_kn=jnp.stack([jnp.stack([g[f"c{i}k"].reshape(-1) for i in range(3)]) for g in grads]),
        m_g_mem=flat("m_g_mem"), m_qn=flat("m_qn"), m_kn=flat("m_kn"), g_mlp=flat("g_mlp"))


def _big_grads_to_reference_layout(g):
    return dict(w_in=_in_from_ours(g["in"]), b_w_uq=_uq_from_ours(g["uq"]), b_w_ukv=_ukv_from_ours(g["ukv"]), m_w_kv=g["mkv"],
                w_branch=jnp.stack(g["branch"]), w_out=g["out"], w_up=g["up"], w_down=g["down"])


def _big_to_kernel_layout(full):
    conv = {"w_in": ("in", _in_to_ours), "b_w_uq": ("uq", _uq_to_ours), "b_w_ukv": ("ukv", _ukv_to_ours),
            "m_w_kv": ("mkv", None), "w_branch": ("branch", None), "w_out": ("out", None), "w_up": ("up", None),
            "w_down": ("down", None)}
    out = {}
    for k, a in full.items():
        name, fn = conv[k]
        out[name] = (a if fn is None else fn(a)).astype(BF16)
    return out


MESH = pl.DeviceIdType.MESH
N_CHIPS = 4
N_DEV = 8
_ANY = pl.BlockSpec(memory_space=pl.ANY)


_HBM = pl.BlockSpec(memory_space=pltpu.HBM)
_SEM = pl.BlockSpec(memory_space=pltpu.SEMAPHORE)
_EFFECT = pltpu.SideEffectType.DATAFLOW_SIDE_EFFECTING


def _chip_peers():
    x, y, c = lax.axis_index("x"), lax.axis_index("y"), lax.axis_index("c")
    return 2 * x + y, [((1 - x, y, c), 2 * (1 - x) + y), ((x, 1 - y, c), 2 * x + 1 - y), ((1 - x, 1 - y, c), 2 * (1 - x) + 1 - y)]


def _exchange_start(srcs, lands, *, gather, name):
    n = len(srcs)

    def body(*refs):
        ins, land = refs[:n], refs[n:2 * n]
        send_sems, recv_sems = refs[2 * n], refs[2 * n + 1]
        token = refs[-1]
        me, peers = _chip_peers()
        for a in range(n):
            for j, (dev, chip) in enumerate(peers):
                src = ins[a] if gather else ins[a].at[chip]
                pltpu.make_async_remote_copy(src_ref=src, dst_ref=land[a].at[me], send_sem=send_sems.at[3 * a + j],
                                             recv_sem=recv_sems.at[3 * a + j], device_id=dev, device_id_type=MESH).start()
        token[...] = jnp.zeros(token.shape, token.dtype)

    hbm = lambda a: pltpu.HBM(a.shape, a.dtype)
    outs = _pcall(
        body, name=name,
        out_shape=(pltpu.SemaphoreType.DMA((3 * n,)), pltpu.SemaphoreType.DMA((3 * n,)), *[hbm(a) for a in srcs],
                   *[hbm(a) for a in lands], jax.ShapeDtypeStruct((8, LANE), F32)),
        in_specs=[_HBM] * (2 * n), out_specs=(_SEM, _SEM, *([_HBM] * (2 * n)), pl.BlockSpec(memory_space=pltpu.VMEM)),
        input_output_aliases={a: 2 + a for a in range(2 * n)},
        compiler_params=pltpu.CompilerParams(has_side_effects=_EFFECT),
    )(*[pltpu.with_memory_space_constraint(a, pltpu.HBM) for a in srcs],
      *[pltpu.with_memory_space_constraint(a, pltpu.HBM) for a in lands])
    return outs[0], outs[1], list(outs[2:2 + n]), list(outs[2 + n:2 + 2 * n]), outs[-1]


def _exchange_wait(state, after, *, gather, name):
    send_sems, recv_sems, srcs, lands, _ = state
    n = len(lands)

    def body(*refs):
        src_refs, land = refs[:n], refs[n:2 * n]
        ssem, rsem = refs[2 * n], refs[2 * n + 1]
        me, peers = _chip_peers()
        for a in range(n):
            for j, (dev, chip) in enumerate(peers):
                src = src_refs[a] if gather else src_refs[a].at[chip]
                cp = pltpu.make_async_remote_copy(src_ref=src, dst_ref=land[a].at[chip], send_sem=ssem.at[3 * a + j],
                                                  recv_sem=rsem.at[3 * a + j], device_id=dev, device_id_type=MESH)
                cp.wait_send()
                cp.wait_recv()

    outs = _pcall(
        body, name=name,
        out_shape=tuple(pltpu.HBM(a.shape, a.dtype) for a in list(srcs) + list(lands)),
        in_specs=[_HBM] * (2 * n) + [_SEM, _SEM, pl.BlockSpec(memory_space=pl.ANY)],
        out_specs=tuple([_HBM] * (2 * n)), input_output_aliases={a: a for a in range(2 * n)},
        compiler_params=pltpu.CompilerParams(has_side_effects=_EFFECT),
    )(*srcs, *lands, send_sems, recv_sems, after)
    return list(outs[:n]), list(outs[n:])


def _sibling_exchange(arrays, *, name):
    n = len(arrays)

    def body(*refs):
        ins, outs = refs[:n], refs[n:2 * n]
        send_sems, recv_sems = refs[2 * n:]
        x, y, c = lax.axis_index("x"), lax.axis_index("y"), lax.axis_index("c")
        cps = []
        for a in range(n):
            cp = pltpu.make_async_remote_copy(src_ref=ins[a], dst_ref=outs[a], send_sem=send_sems.at[a], recv_sem=recv_sems.at[a],
                                              device_id=(x, y, 1 - c), device_id_type=MESH)
            cp.start()
            cps.append(cp)
        for cp in cps:
            cp.wait()

    return _pcall(
        body, name=name, in_specs=[_ANY] * n, out_specs=[_ANY] * n,
        out_shape=[jax.ShapeDtypeStruct(a.shape, a.dtype) for a in arrays],
        scratch_shapes=[pltpu.SemaphoreType.DMA((n,)), pltpu.SemaphoreType.DMA((n,))],
        compiler_params=pltpu.CompilerParams(has_side_effects=True),
    )(*arrays)


def _allreduce_small(v, *, name):
    rows = v.shape[0]

    def body(v_ref, o_ref, slots, send_sems, recv_sems):
        x, y, c = lax.axis_index("x"), lax.axis_index("y"), lax.axis_index("c")
        me = 4 * x + 2 * y + c
        slots[me] = v_ref[...]
        cps = []
        for k in range(1, N_DEV):
            fx, fy, fc = (k >> 2) & 1, (k >> 1) & 1, k & 1
            peer = (x ^ fx, y ^ fy, c ^ fc)
            cp = pltpu.make_async_remote_copy(src_ref=v_ref, dst_ref=slots.at[me], send_sem=send_sems.at[k - 1],
                                              recv_sem=recv_sems.at[k - 1], device_id=peer, device_id_type=MESH)
            cp.start()
            cps.append((cp, peer))
        for k, (cp, (px, py, pc)) in enumerate(cps):
            pltpu.make_async_remote_copy(src_ref=v_ref, dst_ref=slots.at[4 * px + 2 * py + pc], send_sem=send_sems.at[k],
                                         recv_sem=recv_sems.at[k], device_id=(px, py, pc), device_id_type=MESH).wait_recv()
        for cp, _ in cps:
            cp.wait_send()
        tot = slots[0]
        for d in range(1, N_DEV):
            tot = tot + slots[d]
        o_ref[...] = tot

    vm = pl.BlockSpec(memory_space=pltpu.VMEM)
    return _pcall(
        body, name=name, in_specs=[vm], out_specs=vm, out_shape=jax.ShapeDtypeStruct(v.shape, F32),
        scratch_shapes=[pltpu.VMEM((N_DEV, rows, LANE), F32), pltpu.SemaphoreType.DMA((N_DEV - 1,)),
                        pltpu.SemaphoreType.DMA((N_DEV - 1,))],
        compiler_params=pltpu.CompilerParams(has_side_effects=True),
    )(v)


def _rows_block(rows, cols, itemsize=4, target_bytes=1 << 20):
    want = max(16, target_bytes // max(1, cols * itemsize))
    best = rows
    for t in range(16, rows, 16):
        if rows % t == 0 and t <= want:
            best = t
    return best if best <= want or rows <= want else rows


def _sum_slots(recvs, parts, me, *, name):
    nl = len(recvs)
    shp = recvs[0].shape[1:]
    r3 = [r.reshape(N_CHIPS, -1, shp[-1]) for r in recvs]
    p3 = [q.reshape(N_CHIPS, -1, shp[-1]) for q in parts]
    rows, cols = r3[0].shape[1:]
    tb = _rows_block(rows, cols)
    nblk = rows // tb
    per = N_CHIPS

    def body(me_ref, *refs):
        o_ref = refs[-1]
        lg = pl.program_id(0)
        for l in range(nl):
            r = refs[per * l:per * (l + 1)]

            @pl.when(lg == l)
            def _(r=r):
                tot = (r[0][...].astype(F32) + r[1][...].astype(F32)) + r[2][...].astype(F32)
                o_ref[...] = tot + r[3][...].astype(F32)

    def row(l, lg, i):
        return jnp.where(lg < l, 0, jnp.where(lg > l, nblk - 1, i))

    in_specs, args = [], []
    for l in range(nl):
        for k in range(1, N_CHIPS):
            in_specs.append(pl.BlockSpec((None, tb, cols), lambda lg, i, me_ref, l=l, k=k: (me_ref[0] ^ k, row(l, lg, i), 0)))
            args.append(r3[l])
        in_specs.append(pl.BlockSpec((None, tb, cols), lambda lg, i, me_ref, l=l: (me_ref[0], row(l, lg, i), 0)))
        args.append(p3[l])
    out = _pcall(
        body, name=name,
        grid_spec=pltpu.PrefetchScalarGridSpec(
            num_scalar_prefetch=1, grid=(nl, nblk), in_specs=in_specs,
            out_specs=pl.BlockSpec((None, tb, cols), lambda lg, i, me_ref: (lg, i, 0))),
        out_shape=jax.ShapeDtypeStruct((nl, rows, cols), F32), compiler_params=_cparams(("arbitrary", "arbitrary")),
    )(me, *args)
    return out.reshape((nl,) + shp)


def _adamw(w, g_parts, m, v, *, layer0=0, prev=None, name):
    shp = w.shape
    two = lambda a: a.reshape(-1, shp[-1])
    rows, cols = two(w).shape
    grows = two(g_parts[0]).shape[0]
    per_layer = rows // shp[0] if layer0 or prev is not None or grows != rows else rows
    tb = _rows_block(per_layer, cols, target_bytes=1 << 19)
    off = (layer0 * per_layer) // tb if per_layer != rows else 0
    npart = len(g_parts)
    nprev = 0 if prev is None else 4

    def body(*refs):
        w_ref = refs[0]
        gp = refs[1:1 + npart]
        m_ref, v_ref = refs[1 + npart:3 + npart]
        g_out, d_out, m_out, v_out = refs[3 + npart + nprev:]
        g = gp[0][...]
        for r in gp[1:]:
            g = g + r[...]
        wv = w_ref[...]
        m2 = ADAM_B1 * m_ref[...] + (1.0 - ADAM_B1) * g
        v2 = ADAM_B2 * v_ref[...] + (1.0 - ADAM_B2) * jnp.square(g)
        m_hat = m2 / (1.0 - ADAM_B1 ** ADAM_STEP)
        v_hat = v2 / (1.0 - ADAM_B2 ** ADAM_STEP)
        g_out[...] = g
        d_out[...] = -ADAM_LR * (m_hat / (jnp.sqrt(v_hat) + ADAM_EPS) + ADAM_WD * wv)
        m_out[...] = m2
        v_out[...] = v2

    wspec = pl.BlockSpec((tb, cols), lambda i: (i + off, 0))
    gspec = pl.BlockSpec((tb, cols), lambda i: (i, 0))
    in_specs = [wspec] + [gspec] * npart + [wspec, wspec] + [_ANY] * nprev
    args = [two(w)] + [two(p) for p in g_parts] + [two(m), two(v)] + ([two(a) for a in prev] if prev is not None else [])
    outs = _pcall(
        body, name=name, grid=(grows // tb,), in_specs=in_specs, out_specs=[wspec] * 4,
        out_shape=[jax.ShapeDtypeStruct((rows, cols), F32)] * 4,
        input_output_aliases={3 + npart + k: k for k in range(nprev)}, compiler_params=_cparams(("parallel",)),
    )(*args)
    return [o.reshape(shp) for o in outs]


BIG = ("w_in", "b_w_uq", "b_w_ukv", "m_w_kv", "w_branch", "w_out", "w_up", "w_down")
_SHARD_AXIS = dict(w_in=2, b_w_uq=2, b_w_ukv=2, m_w_kv=1, w_branch=3, w_out=1, w_up=2, w_down=1)
SMALL = ("g_mix", "b_gate", "a_qn", "a_kn", "a_sink", "b_qa_norm", "b_kva_norm", "b_qn", "b_kn", "c_qn", "c_kn",
         "m_g_mem", "m_qn", "m_kn", "g_mlp")
WEIGHTS = ("g_mix", "w_in", "b_gate", "a_qn", "a_kn", "a_sink", "b_qa_norm", "b_kva_norm", "b_w_uq", "b_w_ukv", "b_qn", "b_kn",
           "c_qn", "c_kn", "m_g_mem", "m_w_kv", "m_qn", "m_kn", "w_branch", "w_out", "g_mlp", "w_up", "w_down")


def _unshard(gathered, axis):
    moved = jnp.moveaxis(gathered, 0, axis)
    shp = list(gathered.shape[1:])
    shp[axis] *= N_CHIPS
    return moved.reshape(shp)


def _shard_parts(full, axis):
    shp = list(full.shape)
    shp[axis:axis + 1] = [N_CHIPS, shp[axis] // N_CHIPS]
    return jnp.moveaxis(full.reshape(shp), axis, 0)


def _pack_small(d):
    flat = jnp.concatenate([d[k].reshape(-1).astype(F32) for k in SMALL])
    n = flat.shape[0]
    pad = (-n) % (8 * LANE)
    return jnp.pad(flat, (0, pad)).reshape(-1, LANE)


def _unpack_small(packed, like):
    flat = packed.reshape(-1)
    out, off = {}, 0
    for k in SMALL:
        n = int(np.prod(like[k].shape))
        out[k] = flat[off:off + n].reshape(like[k].shape)
        off += n
    return out


def _train_step(x, mem, positions, loss_target, w, m, v):
    depth = w["g_mix"].shape[0]
    me = 2 * lax.axis_index("x") + lax.axis_index("y")
    landing = lambda a: lax.empty((N_CHIPS,) + a.shape, a.dtype)
    def with_own(land, mine):
        slot = lax.broadcasted_iota(jnp.int32, (N_CHIPS,) + (1,) * mine.ndim, 0)
        return jnp.where(slot == me, mine[None], land)

    gathers = {}
    for l in range(depth):
        for names in ((BIG[:1], BIG[1:]) if l == 0 else (BIG,)):
            own = [w[k][l].astype(BF16) for k in names]
            tag = f"{l}" if len(names) == len(BIG) else f"{l}_{names[0]}"
            gathers[(l, names)] = _exchange_start(own, [landing(a) for a in own], gather=True, name=f"gather_start{tag}")

    def gathered(l, names, after):
        tag = f"{l}" if len(names) == len(BIG) else f"{l}_{names[0]}"
        mine, lands = _exchange_wait(gathers[(l, names)], after, gather=True, name=f"gather_wait{tag}")
        full = {k: _unshard(with_own(g, o), _SHARD_AXIS[k] - 1) for k, g, o in zip(names, lands, mine)}
        return {kk: vv for kk, vv in _big_to_kernel_layout(full).items()}

    def get_big(l, after):
        if l == 0:
            return gathered(0, BIG[:1], after), lambda later: gathered(0, BIG[1:], later)
        return gathered(l, BIG, after), None

    scatters = [None] * depth

    def put_grads(l, g):
        gref = _big_grads_to_reference_layout(g)
        parts = [_shard_parts(gref[k], _SHARD_AXIS[k] - 1).astype(BF16) for k in BIG]
        scatters[l] = _exchange_start(parts, [landing(p[0]) for p in parts], gather=False, name=f"scatter_start{l}")
        return scatters[l][4][:1, :1]

    small = {k: w[k] for k in SMALL}
    loss, gx, grads = _local_step(x[0], mem[0], positions[0], small, loss_target[0], get_big, put_grads)
    loss = lax.psum(loss, ("x", "y", "c"))
    me1 = me.reshape(1).astype(jnp.int32)
    res = {k: None for k in BIG}
    after = scatters[0][4]
    for lo, n in ([(1, depth - 1), (0, 1)] if depth > 1 else [(0, 1)]):
        waited = [_exchange_wait(scatters[l], after, gather=False, name=f"scatter_wait{l}") for l in range(lo, lo + n)]
        parts, recv = [p for p, _ in waited], [r for _, r in waited]
        mine = [_sum_slots([recv[l][a] for l in range(n)], [parts[l][a] for l in range(n)], me1, name=f"sum_{k}_{lo}")
                for a, k in enumerate(BIG)]
        theirs = _sibling_exchange(mine, name=f"sibling_grads{lo}")
        for k, p, q in zip(BIG, mine, theirs):
            res[k] = _adamw(w[k], [p, q], m[k], v[k], layer0=lo, prev=res[k], name=f"adamw_{k}_{lo}")
        after = res[BIG[-1]][0]
    gsmall = _small_grads_to_reference_layout(grads)
    g_small = _allreduce_small(_pack_small(gsmall), name="allreduce_small")
    packed = _adamw(_pack_small(small), [g_small], _pack_small({k: m[k] for k in SMALL}), _pack_small({k: v[k] for k in SMALL}),
                    name="adamw_small")
    unpacked = [_unpack_small(p, small) for p in packed]
    for k in SMALL:
        res[k] = [u[k] for u in unpacked]
    outs = [loss, gx[None]]
    for i in range(4):
        outs += [res[k][i] for k in WEIGHTS]
    return tuple(outs)

def kernel(x, mem, positions, g_mix, w_in, b_gate, a_qn, a_kn, a_sink, b_qa_norm, b_kva_norm, b_w_uq, b_w_ukv, b_qn, b_kn, c_qn, c_kn, m_g_mem, m_w_kv, m_qn, m_kn, w_branch, w_out, g_mlp, w_up, w_down, loss_target, m_g_mix, m_w_in, m_b_gate, m_a_qn, m_a_kn, m_a_sink, m_b_qa_norm, m_b_kva_norm, m_b_w_uq, m_b_w_ukv, m_b_qn, m_b_kn, m_c_qn, m_c_kn, m_m_g_mem, m_m_w_kv, m_m_qn, m_m_kn, m_w_branch, m_w_out, m_g_mlp, m_w_up, m_w_down, v_g_mix, v_w_in, v_b_gate, v_a_qn, v_a_kn, v_a_sink, v_b_qa_norm, v_b_kva_norm, v_b_w_uq, v_b_w_ukv, v_b_qn, v_b_kn, v_c_qn, v_c_kn, v_m_g_mem, v_m_w_kv, v_m_qn, v_m_kn, v_w_branch, v_w_out, v_g_mlp, v_w_up, v_w_down):
    args = dict(locals())
    w = {k: args[k] for k in WEIGHTS}
    m = {k: args["m_" + k] for k in WEIGHTS}
    v = {k: args["v_" + k] for k in WEIGHTS}
    return _train_step(x, mem, positions, loss_target, w, m, v)
```

```python
import functools
import math

import jax
import jax.numpy as jnp
import numpy as np
from jax import lax
from jax.experimental import pallas as pl
from jax.experimental.pallas import tpu as pltpu

F32 = jnp.float32
BF16 = jnp.bfloat16

DEPTH = 4
BLOCK = 128
ROPE_THETA = 10000.0
EPS = 1e-6
NEG = -1e30
A_HEADS, A_KV_HEADS, A_HD, A_WINDOW = 8, 2, 64, 128
B_HEADS, B_Q_LORA, B_KV_LORA, B_NOPE, B_ROPE, B_V = 8, 384, 256, 64, 32, 64
C_PATTERNS = ((128, 1), (512, 4), (2048, 16))
C_HEADS, C_HD = 8, 64
M_HEADS, M_HD = 4, 128
N_BRANCH, BRANCH_W = 4, 512
ADAM_LR, ADAM_B1, ADAM_B2, ADAM_EPS, ADAM_WD, ADAM_STEP = 0.001, 0.9, 0.999, 1e-08, 0.01, 10

LANE = 128
VMEM_LIMIT = 56 * 1024 * 1024


def _pcall(body, **kw):
    return pl.pallas_call(body, **kw)


def _cparams(sem):
    return pltpu.CompilerParams(dimension_semantics=sem, vmem_limit_bytes=VMEM_LIMIT)


def _tile(n, target):
    if n <= target:
        return n
    best = None
    for t in range(LANE, target + 1, LANE):
        if n % t == 0:
            best = t
    return best if best is not None else n


def _mm(a, b, *, ta=False, tb=False, out_dtype=F32, pro_a=None, epi=None, extra=None, name,
        tm=1024, tn=1024, tk=1024):
    if ta:
        K, M = a.shape
    else:
        M, K = a.shape
    if tb:
        N, K2 = b.shape
    else:
        K2, N = b.shape
    assert K == K2, (a.shape, b.shape, ta, tb)
    tm, tn, tk = _tile(M, tm), _tile(N, tn), _tile(K, tk)
    nk = K // tk
    a_spec = pl.BlockSpec((tk, tm), lambda i, j, k: (k, i)) if ta else pl.BlockSpec((tm, tk), lambda i, j, k: (i, k))
    b_spec = pl.BlockSpec((tn, tk), lambda i, j, k: (j, k)) if tb else pl.BlockSpec((tk, tn), lambda i, j, k: (k, j))
    o_spec = pl.BlockSpec((tm, tn), lambda i, j, k: (i, j))
    dims = (((0,) if ta else (1,), (1,) if tb else (0,)), ((), ()))
    has_extra = extra is not None

    def body(*refs):
        if has_extra:
            a_ref, b_ref, e_ref, o_ref, acc_ref = refs
        else:
            a_ref, b_ref, o_ref, acc_ref = refs
            e_ref = None
        k = pl.program_id(2)
        av = a_ref[...]
        if pro_a is not None:
            av = pro_a(av.astype(F32))
        part = lax.dot_general(av.astype(BF16), b_ref[...].astype(BF16), dims, preferred_element_type=F32)

        @pl.when(k == 0)
        def _():
            acc_ref[...] = part

        @pl.when(k > 0)
        def _():
            acc_ref[...] += part

        @pl.when(k == nk - 1)
        def _():
            r = acc_ref[...]
            if epi is not None:
                r = epi(r, e_ref[...]) if has_extra else epi(r)
            o_ref[...] = r.astype(out_dtype)

    in_specs = [a_spec, b_spec] + ([o_spec] if has_extra else [])
    args = (a, b) + ((extra,) if has_extra else ())
    return _pcall(
        body, name=name, grid=(M // tm, N // tn, nk), in_specs=in_specs, out_specs=o_spec,
        out_shape=jax.ShapeDtypeStruct((M, N), out_dtype),
        scratch_shapes=[pltpu.VMEM((tm, tn), F32)],
        compiler_params=_cparams(("parallel", "parallel", "arbitrary")),
    )(*args)


def _piece(p):
    if isinstance(p, tuple):
        return p
    return (p, p.shape[1], 0)


def _row_spec(width, idx, tb):
    return pl.BlockSpec((tb, width), lambda i, idx=idx: (i, idx))


def _full_spec(arr):
    nd = arr.ndim
    return pl.BlockSpec(arr.shape, lambda i, nd=nd: (0,) * nd)


def _rowmap(f, rows, params, outs, *, tb, name):
    rows = [_piece(p) for p in rows]
    R = rows[0][0].shape[0]
    tb = min(tb, R)
    nr, npar, nout = len(rows), len(params), len(outs)

    def body(*refs):
        rv = [r[...] for r in refs[:nr]]
        pv = [r[...] for r in refs[nr:nr + npar]]
        res = f(*rv, *pv)
        for o_ref, val in zip(refs[nr + npar:], res):
            o_ref[...] = val.astype(o_ref.dtype)

    return _pcall(
        body, name=name, grid=(R // tb,),
        in_specs=[_row_spec(w, idx, tb) for (_, w, idx) in rows] + [_full_spec(p) for p in params],
        out_specs=[_row_spec(w, 0, tb) for (w, _) in outs],
        out_shape=[jax.ShapeDtypeStruct((R, w), dt) for (w, dt) in outs],
        compiler_params=_cparams(("parallel",)),
    )(*[r[0] for r in rows], *params)


def _rowmap_bwd(f, rows, params, couts, *, diff, out_dtypes, adds=None, tb, name):
    rows = [_piece(p) for p in rows]
    couts = [_piece(p) for p in couts]
    R = rows[0][0].shape[0]
    tb = min(tb, R)
    nr, npar, nc = len(rows), len(params), len(couts)
    didx = [i for i, d in enumerate(diff) if d]
    adds = [None] * len(didx) if adds is None else adds
    add_ops = [_piece(a) for a in adds if a is not None]
    na = len(add_ops)

    def body(*refs):
        rv = [r[...] for r in refs[:nr]]
        pv = [r[...] for r in refs[nr:nr + npar]]
        cv = [r[...] for r in refs[nr + npar:nr + npar + nc]]
        av = [r[...] for r in refs[nr + npar + nc:nr + npar + nc + na]]
        o_refs = refs[nr + npar + nc + na:]
        drow_refs, dpar_refs = o_refs[:len(didx)], o_refs[len(didx):]
        nondiff = {i: rv[i] for i in range(nr) if not diff[i]}

        def g(*dv):
            full = []
            it = iter(dv[:len(didx)])
            for i in range(nr):
                full.append(nondiff[i] if i in nondiff else next(it))
            return f(*full, *dv[len(didx):])

        res, vjp = jax.vjp(g, *[rv[i].astype(F32) for i in didx], *pv)
        cts = tuple(c.astype(r.dtype) for c, r in zip(cv, res))
        grads = vjp(cts)
        ai = 0
        for n, o_ref in enumerate(drow_refs):
            val = grads[n]
            if adds[n] is not None:
                val = val + av[ai].astype(F32)
                ai += 1
            o_ref[...] = val.astype(o_ref.dtype)
        first = pl.program_id(0) == 0
        for n, o_ref in enumerate(dpar_refs):
            gp = grads[len(didx) + n].astype(F32)

            @pl.when(first)
            def _(o_ref=o_ref, gp=gp):
                o_ref[...] = gp

            @pl.when(jnp.logical_not(first))
            def _(o_ref=o_ref, gp=gp):
                o_ref[...] += gp

    outs = _pcall(
        body, name=name, grid=(R // tb,),
        in_specs=([_row_spec(w, idx, tb) for (_, w, idx) in rows] + [_full_spec(p) for p in params]
                  + [_row_spec(w, idx, tb) for (_, w, idx) in couts] + [_row_spec(w, idx, tb) for (_, w, idx) in add_ops]),
        out_specs=[_row_spec(rows[i][1], 0, tb) for i in didx] + [_full_spec(p) for p in params],
        out_shape=([jax.ShapeDtypeStruct((R, rows[i][1]), dt) for i, dt in zip(didx, out_dtypes)]
                   + [jax.ShapeDtypeStruct(p.shape, F32) for p in params]),
        compiler_params=_cparams(("arbitrary",)),
    )(*[r[0] for r in rows], *params, *[c[0] for c in couts], *[a[0] for a in add_ops])
    return outs[:len(didx)], outs[len(didx):]


@functools.partial(jax.custom_vjp, nondiff_argnums=(1,))
def _lane_roll(x, shift):
    return pltpu.roll(x, shift % x.shape[-1], axis=x.ndim - 1)


def _lane_roll_fwd(x, shift):
    return _lane_roll(x, shift), None


def _lane_roll_bwd(shift, _, g):
    return (_lane_roll(g, -shift),)


_lane_roll.defvjp(_lane_roll_fwd, _lane_roll_bwd)


def _group_matrix(kind):
    r = lax.broadcasted_iota(jnp.int32, (LANE, LANE), 0)
    c = lax.broadcasted_iota(jnp.int32, (LANE, LANE), 1)
    if kind == "mla":
        gid = lambda l: jnp.where(l < B_NOPE, 0, jnp.where(l < B_NOPE + B_ROPE, 1, 2))
        inv = jnp.where(c < B_NOPE, 1.0 / B_NOPE, 1.0 / B_ROPE)
        return jnp.where(gid(r) == gid(c), inv, 0.0).astype(BF16)
    return jnp.where(r // kind == c // kind, 1.0 / kind, 0.0).astype(BF16)


@functools.partial(jax.custom_vjp, nondiff_argnums=(1,))
def _group_mean(xx, kind):
    gm = _group_matrix(kind)
    outs = []
    for b in range(xx.shape[-1] // LANE):
        t = xx[:, b * LANE:(b + 1) * LANE]
        hi = t.astype(BF16)
        lo = (t - hi.astype(F32)).astype(BF16)
        outs.append(jnp.dot(hi, gm, preferred_element_type=F32) + jnp.dot(lo, gm, preferred_element_type=F32))
    return jnp.concatenate(outs, axis=1) if len(outs) > 1 else outs[0]


def _group_mean_fwd(xx, kind):
    return _group_mean(xx, kind), None


def _group_mean_bwd(kind, _, g):
    return (_group_mean(g, kind),)


_group_mean.defvjp(_group_mean_fwd, _group_mean_bwd)


def _exact_dot(x, m):
    return jnp.dot(x, m, precision=lax.Precision.HIGHEST, preferred_element_type=F32)


def _head_norm(x, gain_tiled, group):
    return x * lax.rsqrt(_group_mean(x * x, group) + EPS) * gain_tiled


def _row_norm(x, gain):
    ms = jnp.mean(x * x, axis=-1, keepdims=True)
    return x * lax.rsqrt(ms + EPS) * gain


def _rope(x, cos, sin, hd):
    half = hd // 2
    lane = lax.broadcasted_iota(jnp.int32, x.shape, x.ndim - 1) % hd
    other = jnp.where(lane < half, -_lane_roll(x, -half), _lane_roll(x, half))
    return x * cos + other * sin


class _AttnGeom:
    def __init__(self, mode, lq, lk, max_dist=0):
        self.mode, self.lq, self.lk, self.max_dist = mode, lq, lk, max_dist
        if mode == "band":
            self.bq = self.bk = BLOCK
            self.nt_q = 2
            self.nt_k = 2
        elif mode == "causal":
            self.bq = self.bk = min(256, lq)
            self.nt_q = lk // self.bk
            self.nt_k = lq // self.bq
        else:
            self.bq = min(512, lq)
            self.bk = lk
            self.nt_q = 1
            self.nt_k = lq // self.bq
        self.nq, self.nk = lq // self.bq, lk // self.bk

    def kv_block(self, i, t):
        if self.mode == "band":
            return jnp.maximum(i - t, 0)
        if self.mode == "causal":
            return jnp.minimum(t, i)
        return 0 * i

    def kv_active(self, i, t):
        if self.mode == "band":
            return i - t >= 0
        if self.mode == "causal":
            return t <= i
        return None

    def q_block(self, j, t):
        if self.mode == "band":
            return jnp.minimum(j + t, self.nq - 1)
        if self.mode == "causal":
            return jnp.maximum(t, j)
        return t

    def q_active(self, j, t):
        if self.mode == "band":
            return j + t <= self.nq - 1
        if self.mode == "causal":
            return t >= j
        return None

    def mask(self, qb, kb):
        if self.mode == "full":
            return None
        qp = qb * self.bq + lax.broadcasted_iota(jnp.int32, (self.bq, self.bk), 0)
        kp = kb * self.bk + lax.broadcasted_iota(jnp.int32, (self.bq, self.bk), 1)
        d = qp - kp
        if self.mode == "band":
            return (d >= 0) & (d <= self.max_dist)
        return d >= 0


def _when(cond, fn):
    if cond is None:
        fn()
    else:
        pl.when(cond)(fn)


def _dil_view(p, dil):
    arr, w, idx = _piece(p)
    R, C = arr.shape
    assert C % w == 0, (C, w)
    return arr.reshape(R // dil, dil * C), w, idx, C // w


def _seq_spec(view, rows, blk_fn):
    _, w, idx, cpw = view
    return pl.BlockSpec((rows, w), lambda s, i, t: (blk_fn(i, t), s * cpw + idx))


_NT = (((1,), (1,)), ((), ()))
_TN = (((0,), (0,)), ((), ()))


def _scores(geom, scale, q, k, qp, kp, h, g, hd, rope, qb, kb):
    s = lax.dot_general(q[:, h * hd:(h + 1) * hd], k[:, g * hd:(g + 1) * hd], _NT, preferred_element_type=F32)
    if rope:
        s = s + lax.dot_general(qp[:, h * rope:(h + 1) * rope], kp[:, :rope], _NT, preferred_element_type=F32)
    s = s * scale
    m = geom.mask(qb, kb)
    return s, m


def _attn_fwd(geom, q, k, v, *, hq, hk, hd, hdv, scale, dil=1, qp=None, kp=None, rope=0, name):
    qv, kv, vv = _dil_view(q, dil), _dil_view(k, dil), _dil_view(v, dil)
    R = _piece(q)[0].shape[0]
    grp = hq // hk
    bq, bk, nt = geom.bq, geom.bk, geom.nt_q
    ops = [qv, kv, vv]
    specs = [_seq_spec(qv, bq, lambda i, t: i), _seq_spec(kv, bk, geom.kv_block), _seq_spec(vv, bk, geom.kv_block)]
    if rope:
        qpv, kpv = _dil_view(qp, dil), _dil_view(kp, dil)
        ops += [qpv, kpv]
        specs += [_seq_spec(qpv, bq, lambda i, t: i), _seq_spec(kpv, bk, geom.kv_block)]
    ow = hq * hdv
    o_view = (None, ow, 0, 1)
    o_spec = pl.BlockSpec((bq, ow), lambda s, i, t: (i, s))

    def body(*refs):
        if rope:
            q_ref, k_ref, v_ref, qp_ref, kp_ref, o_ref, lse_ref, m_sc, l_sc, acc_sc = refs
        else:
            q_ref, k_ref, v_ref, o_ref, lse_ref, m_sc, l_sc, acc_sc = refs
            qp_ref = kp_ref = None
        i, t = pl.program_id(1), pl.program_id(2)

        @pl.when(t == 0)
        def _():
            m_sc[...] = jnp.full(m_sc.shape, NEG, F32)
            l_sc[...] = jnp.zeros(l_sc.shape, F32)
            acc_sc[...] = jnp.zeros(acc_sc.shape, F32)

        def step():
            qa, ka, va = q_ref[...].astype(BF16), k_ref[...].astype(BF16), v_ref[...].astype(BF16)
            qpa = qp_ref[...].astype(BF16) if rope else None
            kpa = kp_ref[...].astype(BF16) if rope else None
            kb = geom.kv_block(i, t)
            for h in range(hq):
                g = h // grp
                s, msk = _scores(geom, scale, qa, ka, qpa, kpa, h, g, hd, rope, i, kb)
                if msk is not None:
                    s = jnp.where(msk, s, NEG)
                m_old = m_sc[h]
                m_new = jnp.maximum(m_old, jnp.max(s, axis=1, keepdims=True))
                p = jnp.exp(s - m_new)
                alpha = jnp.exp(m_old - m_new)
                l_sc[h] = alpha * l_sc[h] + jnp.sum(p, axis=1, keepdims=True)
                pv = jnp.dot(p.astype(BF16), va[:, g * hdv:(g + 1) * hdv], preferred_element_type=F32)
                acc_sc[h] = alpha * acc_sc[h] + pv
                m_sc[h] = m_new

        _when(geom.kv_active(i, t), step)

        @pl.when(t == nt - 1)
        def _():
            for h in range(hq):
                l = l_sc[h]
                o_ref[:, h * hdv:(h + 1) * hdv] = acc_sc[h] / l
                lse_ref[:, h * hdv:(h + 1) * hdv] = jnp.broadcast_to(m_sc[h] + jnp.log(l), (bq, hdv))

    o, lse = _pcall(
        body, name=name, grid=(dil, geom.nq, nt), in_specs=specs, out_specs=[o_spec, o_spec],
        out_shape=[jax.ShapeDtypeStruct((R // dil, dil * ow), F32)] * 2,
        scratch_shapes=[pltpu.VMEM((hq, bq, 1), F32), pltpu.VMEM((hq, bq, 1), F32), pltpu.VMEM((hq, bq, hdv), F32)],
        compiler_params=_cparams(("parallel", "parallel", "arbitrary")),
    )(*[o_[0] for o_ in ops])
    return o.reshape(R, ow), lse.reshape(R, ow)


def _attn_dq(geom, q, k, v, do, o, lse, dlse, *, hq, hk, hd, hdv, scale, dil=1, qp=None, kp=None, rope=0,
             out_dtype=F32, name):
    qv, kv, vv = _dil_view(q, dil), _dil_view(k, dil), _dil_view(v, dil)
    dov, ov, lv = _dil_view(do, dil), _dil_view(o, dil), _dil_view(lse, dil)
    R = _piece(q)[0].shape[0]
    grp = hq // hk
    bq, bk, nt = geom.bq, geom.bk, geom.nt_q
    qi = lambda i, t: i
    ops = [qv, kv, vv, dov, ov, lv]
    specs = [_seq_spec(qv, bq, qi), _seq_spec(kv, bk, geom.kv_block), _seq_spec(vv, bk, geom.kv_block),
             _seq_spec(dov, bq, qi), _seq_spec(ov, bq, qi), _seq_spec(lv, bq, qi)]
    has_dl = dlse is not None
    if has_dl:
        dlv = _dil_view(dlse, dil)
        ops.append(dlv)
        specs.append(_seq_spec(dlv, bq, qi))
    if rope:
        qpv, kpv = _dil_view(qp, dil), _dil_view(kp, dil)
        ops += [qpv, kpv]
        specs += [_seq_spec(qpv, bq, qi), _seq_spec(kpv, bk, geom.kv_block)]
    qw = hq * hd
    out_specs = [pl.BlockSpec((bq, qw), lambda s, i, t: (i, s))]
    out_shape = [jax.ShapeDtypeStruct((R // dil, dil * qw), out_dtype)]
    scratch = [pltpu.VMEM((hq, bq, 1), F32), pltpu.VMEM((hq, bq, hd), F32)]
    if rope:
        out_specs.append(pl.BlockSpec((bq, hq * rope), lambda s, i, t: (i, s)))
        out_shape.append(jax.ShapeDtypeStruct((R // dil, dil * hq * rope), out_dtype))
        scratch.append(pltpu.VMEM((hq, bq, rope), F32))

    def body(*refs):
        refs = list(refs)
        q_ref, k_ref, v_ref, do_ref, o_ref, l_ref = refs[:6]
        pos = 6
        dl_ref = None
        if has_dl:
            dl_ref = refs[pos]
            pos += 1
        qp_ref = kp_ref = None
        if rope:
            qp_ref, kp_ref = refs[pos:pos + 2]
            pos += 2
        dq_ref = refs[pos]
        pos += 1
        dqp_ref = None
        if rope:
            dqp_ref = refs[pos]
            pos += 1
        dl_sc, dq_sc = refs[pos:pos + 2]
        dqp_sc = refs[pos + 2] if rope else None
        i, t = pl.program_id(1), pl.program_id(2)

        @pl.when(t == 0)
        def _():
            dov_, ov_ = do_ref[...].astype(F32), o_ref[...].astype(F32)
            prod = dov_ * ov_
            for h in range(hq):
                d = jnp.sum(prod[:, h * hdv:(h + 1) * hdv], axis=1, keepdims=True)
                if has_dl:
                    d = d - jnp.sum(dl_ref[:, h * hdv:(h + 1) * hdv].astype(F32), axis=1, keepdims=True)
                dl_sc[h] = d
            dq_sc[...] = jnp.zeros(dq_sc.shape, F32)
            if rope:
                dqp_sc[...] = jnp.zeros(dqp_sc.shape, F32)

        def step():
            qa, ka, va = q_ref[...].astype(BF16), k_ref[...].astype(BF16), v_ref[...].astype(BF16)
            doa = do_ref[...].astype(BF16)
            qpa = qp_ref[...].astype(BF16) if rope else None
            kpa = kp_ref[...].astype(BF16) if rope else None
            kb = geom.kv_block(i, t)
            for h in range(hq):
                g = h // grp
                s, msk = _scores(geom, scale, qa, ka, qpa, kpa, h, g, hd, rope, i, kb)
                p = jnp.exp(s - l_ref[:, h * hdv:h * hdv + 1])
                if msk is not None:
                    p = jnp.where(msk, p, 0.0)
                dp = lax.dot_general(doa[:, h * hdv:(h + 1) * hdv], va[:, g * hdv:(g + 1) * hdv], _NT,
                                     preferred_element_type=F32)
                ds = (p * (dp - dl_sc[h]) * scale).astype(BF16)
                dq_sc[h] += jnp.dot(ds, ka[:, g * hd:(g + 1) * hd], preferred_element_type=F32)
                if rope:
                    dqp_sc[h] += jnp.dot(ds, kpa[:, :rope], preferred_element_type=F32)

        _when(geom.kv_active(i, t), step)

        @pl.when(t == nt - 1)
        def _():
            for h in range(hq):
                dq_ref[:, h * hd:(h + 1) * hd] = dq_sc[h].astype(dq_ref.dtype)
                if rope:
                    dqp_ref[:, h * rope:(h + 1) * rope] = dqp_sc[h].astype(dqp_ref.dtype)

    outs = _pcall(
        body, name=name, grid=(dil, geom.nq, nt), in_specs=specs, out_specs=out_specs, out_shape=out_shape,
        scratch_shapes=scratch, compiler_params=_cparams(("parallel", "parallel", "arbitrary")),
    )(*[o_[0] for o_ in ops])
    dq = outs[0].reshape(R, qw)
    if rope:
        return dq, outs[1].reshape(R, hq * rope)
    return dq


def _attn_dkv(geom, q, k, v, do, o, lse, dlse, *, hq, hk, hd, hdv, scale, dil=1, qp=None, kp=None, rope=0,
              out_dtype=F32, name):
    qv, kv, vv = _dil_view(q, dil), _dil_view(k, dil), _dil_view(v, dil)
    dov, ov, lv = _dil_view(do, dil), _dil_view(o, dil), _dil_view(lse, dil)
    Rk = _piece(k)[0].shape[0]
    grp = hq // hk
    bq, bk, nt = geom.bq, geom.bk, geom.nt_k
    kj = lambda j, t: j
    ops = [qv, kv, vv, dov, ov, lv]
    specs = [_seq_spec(qv, bq, geom.q_block), _seq_spec(kv, bk, kj), _seq_spec(vv, bk, kj),
             _seq_spec(dov, bq, geom.q_block), _seq_spec(ov, bq, geom.q_block), _seq_spec(lv, bq, geom.q_block)]
    has_dl = dlse is not None
    if has_dl:
        dlv = _dil_view(dlse, dil)
        ops.append(dlv)
        specs.append(_seq_spec(dlv, bq, geom.q_block))
    if rope:
        qpv, kpv = _dil_view(qp, dil), _dil_view(kp, dil)
        ops += [qpv, kpv]
        specs += [_seq_spec(qpv, bq, geom.q_block), _seq_spec(kpv, bk, kj)]
    kw, vw = hk * hd, hk * hdv
    out_specs = [pl.BlockSpec((bk, kw), lambda s, j, t: (j, s)), pl.BlockSpec((bk, vw), lambda s, j, t: (j, s))]
    out_shape = [jax.ShapeDtypeStruct((Rk // dil, dil * kw), out_dtype), jax.ShapeDtypeStruct((Rk // dil, dil * vw), out_dtype)]
    scratch = [pltpu.VMEM((hk, bk, hd), F32), pltpu.VMEM((hk, bk, hdv), F32)]
    if rope:
        out_specs.append(pl.BlockSpec((bk, LANE), lambda s, j, t: (j, s)))
        out_shape.append(jax.ShapeDtypeStruct((Rk // dil, dil * LANE), out_dtype))
        scratch.append(pltpu.VMEM((bk, rope), F32))

    def body(*refs):
        refs = list(refs)
        q_ref, k_ref, v_ref, do_ref, o_ref, l_ref = refs[:6]
        pos = 6
        dl_ref = None
        if has_dl:
            dl_ref = refs[pos]
            pos += 1
        qp_ref = kp_ref = None
        if rope:
            qp_ref, kp_ref = refs[pos:pos + 2]
            pos += 2
        dk_ref, dv_ref = refs[pos:pos + 2]
        pos += 2
        dkp_ref = None
        if rope:
            dkp_ref = refs[pos]
            pos += 1
        dk_sc, dv_sc = refs[pos:pos + 2]
        dkp_sc = refs[pos + 2] if rope else None
        j, t = pl.program_id(1), pl.program_id(2)

        @pl.when(t == 0)
        def _():
            dk_sc[...] = jnp.zeros(dk_sc.shape, F32)
            dv_sc[...] = jnp.zeros(dv_sc.shape, F32)
            if rope:
                dkp_sc[...] = jnp.zeros(dkp_sc.shape, F32)

        def step():
            qa, ka, va = q_ref[...].astype(BF16), k_ref[...].astype(BF16), v_ref[...].astype(BF16)
            dof = do_ref[...].astype(F32)
            doa = dof.astype(BF16)
            prod = dof * o_ref[...].astype(F32)
            qpa = qp_ref[...].astype(BF16) if rope else None
            kpa = kp_ref[...].astype(BF16) if rope else None
            qb = geom.q_block(j, t)
            for h in range(hq):
                g = h // grp
                s, msk = _scores(geom, scale, qa, ka, qpa, kpa, h, g, hd, rope, qb, j)
                p = jnp.exp(s - l_ref[:, h * hdv:h * hdv + 1])
                if msk is not None:
                    p = jnp.where(msk, p, 0.0)
                delta = jnp.sum(prod[:, h * hdv:(h + 1) * hdv], axis=1, keepdims=True)
                if has_dl:
                    delta = delta - jnp.sum(dl_ref[:, h * hdv:(h + 1) * hdv].astype(F32), axis=1, keepdims=True)
                do_h = doa[:, h * hdv:(h + 1) * hdv]
                dv_sc[g] += lax.dot_general(p.astype(BF16), do_h, _TN, preferred_element_type=F32)
                dp = lax.dot_general(do_h, va[:, g * hdv:(g + 1) * hdv], _NT, preferred_element_type=F32)
                ds = (p * (dp - delta) * scale).astype(BF16)
                dk_sc[g] += lax.dot_general(ds, qa[:, h * hd:(h + 1) * hd], _TN, preferred_element_type=F32)
                if rope:
                    dkp_sc[...] += lax.dot_general(ds, qpa[:, h * rope:(h + 1) * rope], _TN, preferred_element_type=F32)

        _when(geom.q_active(j, t), step)

        @pl.when(t == nt - 1)
        def _():
            for g in range(hk):
                dk_ref[:, g * hd:(g + 1) * hd] = dk_sc[g].astype(dk_ref.dtype)
                dv_ref[:, g * hdv:(g + 1) * hdv] = dv_sc[g].astype(dv_ref.dtype)
            if rope:
                dkp_ref[...] = jnp.zeros(dkp_ref.shape, dkp_ref.dtype)
                dkp_ref[:, :rope] = dkp_sc[...].astype(dkp_ref.dtype)

    outs = _pcall(
        body, name=name, grid=(dil, geom.nk, nt), in_specs=specs, out_specs=out_specs, out_shape=out_shape,
        scratch_shapes=scratch, compiler_params=_cparams(("parallel", "parallel", "arbitrary")),
    )(*[o_[0] for o_ in ops])
    dk, dv = outs[0].reshape(Rk, kw), outs[1].reshape(Rk, vw)
    if rope:
        return dk, dv, outs[2].reshape(Rk, LANE)
    return dk, dv


class _BandPlan:
    def __init__(self, S, dil, max_dist):
        self.L, self.dil, self.max_dist = S // dil, dil, max_dist
        self.nblk = self.L // BLOCK
        self.nb = min(4, self.nblk)
        self.ns = min(dil, max(1, 4 // self.nb))
        self.grid = (dil // self.ns, self.nblk // self.nb)
        self.rows = self.nb * BLOCK

    def view(self, p):
        arr, w, idx = _piece(p)
        R, C = arr.shape
        assert C % w == 0 and (self.ns == 1 or (C == w and idx == 0)), (C, w, idx, self.ns)
        return arr.reshape(R // self.dil, self.dil * C), w, idx, C // w

    def main(self, view):
        _, w, idx, cpw = view
        if self.ns == 1:
            return pl.BlockSpec((self.rows, w), lambda s, i: (i, s * cpw + idx))
        return pl.BlockSpec((self.rows, self.ns * w), lambda s, i: (i, s))

    def edge(self, view, nxt):
        _, w, idx, cpw = view
        nb, last = self.nb, self.nblk - 1
        rb = (lambda i: jnp.minimum((i + 1) * nb, last)) if nxt else (lambda i: jnp.maximum(i * nb - 1, 0))
        if self.ns == 1:
            return pl.BlockSpec((BLOCK, w), lambda s, i: (rb(i), s * cpw + idx))
        return pl.BlockSpec((BLOCK, self.ns * w), lambda s, i: (rb(i), s))

    def out(self, w):
        return pl.BlockSpec((self.rows, self.ns * w), lambda s, i: (i, s))

    def masks(self):
        qi = lax.broadcasted_iota(jnp.int32, (BLOCK, BLOCK), 0)
        kj = lax.broadcasted_iota(jnp.int32, (BLOCK, BLOCK), 1)
        return kj <= qi, (qi - kj + BLOCK) <= self.max_dist


def _half(e, rows=BLOCK):
    lane = lax.broadcasted_iota(jnp.int32, (rows, LANE), 1)
    return (lane < LANE // 2) if e == 0 else (lane >= LANE // 2)


def _swap_halves(t):
    return pltpu.roll(t, LANE // 2, axis=1)


def _kv_group(ref, rows, col0, pr, grp, hq, hk):
    if hk == hq:
        return ref[rows, col0 + pr * LANE:col0 + (pr + 1) * LANE].astype(BF16)
    g = (2 * pr) // grp
    t = ref[rows, col0 + (g // 2) * LANE:col0 + (g // 2 + 1) * LANE].astype(BF16)
    sw = _swap_halves(t)
    h0 = _half(0, t.shape[0])
    return jnp.where(h0, t, sw) if g % 2 == 0 else jnp.where(h0, sw, t)


def _stack_heads(t2):
    z = jnp.zeros_like(t2)
    h0 = _half(0, t2.shape[0])
    return jnp.concatenate([jnp.where(h0, t2, z), jnp.where(h0, z, t2)], axis=0)


def _unstack_heads(t, rows=BLOCK):
    return jnp.where(_half(0, rows), t[:rows], t[rows:])


def _per_head_col(t2):
    return jnp.concatenate([t2[:, :1], t2[:, LANE // 2:LANE // 2 + 1]], axis=0)


def _per_head_sum(t2):
    h0 = _half(0, t2.shape[0])
    return jnp.concatenate([jnp.sum(jnp.where(h0, t2, 0.0), axis=1, keepdims=True),
                            jnp.sum(jnp.where(h0, 0.0, t2), axis=1, keepdims=True)], axis=0)


def _band_fwd(q, k, v, *, S, dil, max_dist, hq, hk, scale, name):
    hd = 64
    plan = _BandPlan(S, dil, max_dist)
    qv, kv, vv = plan.view(q), plan.view(k), plan.view(v)
    wq, wk = hq * hd, hk * hd
    grp = hq // hk
    ns, nb = plan.ns, plan.nb

    def body(q_ref, k_ref, kp_ref, v_ref, vp_ref, o_ref, l_ref):
        i = pl.program_id(1)
        m_cur, m_band = plan.masks()
        has_prev = i > 0
        for sg in range(ns):
            for b in range(nb):
                rows = slice(b * BLOCK, (b + 1) * BLOCK)
                prows = slice((b - 1) * BLOCK, b * BLOCK) if b > 0 else slice(0, BLOCK)
                m_prev = m_band if b > 0 else (m_band & has_prev)
                msk = jnp.concatenate([m_prev, m_cur], axis=1)
                msk = jnp.concatenate([msk, msk], axis=0)
                for pr in range(wq // LANE):
                    cols = slice(sg * wq + pr * LANE, sg * wq + (pr + 1) * LANE)
                    kcat = jnp.concatenate([_kv_group(k_ref if b > 0 else kp_ref, prows, sg * wk, pr, grp, hq, hk),
                                            _kv_group(k_ref, rows, sg * wk, pr, grp, hq, hk)], axis=0)
                    vcat = jnp.concatenate([_kv_group(v_ref if b > 0 else vp_ref, prows, sg * wk, pr, grp, hq, hk),
                                            _kv_group(v_ref, rows, sg * wk, pr, grp, hq, hk)], axis=0)
                    qs = _stack_heads(q_ref[rows, cols].astype(BF16))
                    s = lax.dot_general(qs, kcat, _NT, preferred_element_type=F32) * scale
                    s = jnp.where(msk, s, NEG)
                    mx = jnp.max(s, axis=1, keepdims=True)
                    p = jnp.exp(s - mx)
                    l = jnp.sum(p, axis=1, keepdims=True)
                    acc = jnp.dot(p.astype(BF16), vcat, preferred_element_type=F32)
                    o_ref[rows, cols] = _unstack_heads(acc / l)
                    l_ref[rows, cols] = _unstack_heads(jnp.broadcast_to(mx + jnp.log(l), (2 * BLOCK, LANE)))

    o, lse = _pcall(
        body, name=name, grid=plan.grid,
        in_specs=[plan.main(qv), plan.main(kv), plan.edge(kv, False), plan.main(vv), plan.edge(vv, False)],
        out_specs=[plan.out(wq)] * 2, out_shape=[jax.ShapeDtypeStruct((S // dil, dil * wq), F32)] * 2,
        compiler_params=_cparams(("parallel", "parallel")),
    )(qv[0], kv[0], kv[0], vv[0], vv[0])
    return o.reshape(S, wq), lse.reshape(S, wq)


def _band_dq(q, k, v, do, o, lse, dlse, *, S, dil, max_dist, hq, hk, scale, out_dtype=F32, name):
    hd = 64
    plan = _BandPlan(S, dil, max_dist)
    qv, kv, vv = plan.view(q), plan.view(k), plan.view(v)
    dov, ov, lv = plan.view(do), plan.view(o), plan.view(lse)
    has_dl = dlse is not None
    wq, wk = hq * hd, hk * hd
    grp = hq // hk
    ns, nb = plan.ns, plan.nb
    ops = [qv, kv, kv, vv, vv, dov, ov, lv]
    specs = [plan.main(qv), plan.main(kv), plan.edge(kv, False), plan.main(vv), plan.edge(vv, False), plan.main(dov),
             plan.main(ov), plan.main(lv)]
    if has_dl:
        dlv = plan.view(dlse)
        ops.append(dlv)
        specs.append(plan.main(dlv))

    def body(*refs):
        q_ref, k_ref, kp_ref, v_ref, vp_ref, do_ref, o_ref, l_ref = refs[:8]
        dl_ref = refs[8] if has_dl else None
        dq_ref = refs[-1]
        i = pl.program_id(1)
        m_cur, m_band = plan.masks()
        has_prev = i > 0
        for sg in range(ns):
            for b in range(nb):
                rows = slice(b * BLOCK, (b + 1) * BLOCK)
                prows = slice((b - 1) * BLOCK, b * BLOCK) if b > 0 else slice(0, BLOCK)
                m_prev = m_band if b > 0 else (m_band & has_prev)
                msk = jnp.concatenate([m_prev, m_cur], axis=1)
                msk = jnp.concatenate([msk, msk], axis=0)
                for pr in range(wq // LANE):
                    cols = slice(sg * wq + pr * LANE, sg * wq + (pr + 1) * LANE)
                    kcat = jnp.concatenate([_kv_group(k_ref if b > 0 else kp_ref, prows, sg * wk, pr, grp, hq, hk),
                                            _kv_group(k_ref, rows, sg * wk, pr, grp, hq, hk)], axis=0)
                    vcat = jnp.concatenate([_kv_group(v_ref if b > 0 else vp_ref, prows, sg * wk, pr, grp, hq, hk),
                                            _kv_group(v_ref, rows, sg * wk, pr, grp, hq, hk)], axis=0)
                    qs = _stack_heads(q_ref[rows, cols].astype(BF16))
                    do2 = do_ref[rows, cols].astype(F32)
                    prod = do2 * o_ref[rows, cols].astype(F32)
                    if has_dl:
                        prod = prod - dl_ref[rows, cols].astype(F32)
                    delta = _per_head_sum(prod)
                    lse_rows = _per_head_col(l_ref[rows, cols])
                    dos = _stack_heads(do2.astype(BF16))
                    s = lax.dot_general(qs, kcat, _NT, preferred_element_type=F32) * scale
                    pm = jnp.where(msk, jnp.exp(s - lse_rows), 0.0)
                    dp = lax.dot_general(dos, vcat, _NT, preferred_element_type=F32)
                    ds = (pm * (dp - delta) * scale).astype(BF16)
                    dq_ref[rows, cols] = _unstack_heads(jnp.dot(ds, kcat, preferred_element_type=F32)).astype(dq_ref.dtype)

    dq = _pcall(
        body, name=name, grid=plan.grid, in_specs=specs, out_specs=plan.out(wq),
        out_shape=jax.ShapeDtypeStruct((S // dil, dil * wq), out_dtype), compiler_params=_cparams(("parallel", "parallel")),
    )(*[o_[0] for o_ in ops])
    return dq.reshape(S, wq)


def _band_dkv(q, k, v, do, o, lse, dlse, *, S, dil, max_dist, hq, hk, scale, out_dtype=F32, name):
    hd = 64
    plan = _BandPlan(S, dil, max_dist)
    qv, kv, vv = plan.view(q), plan.view(k), plan.view(v)
    dov, ov, lv = plan.view(do), plan.view(o), plan.view(lse)
    has_dl = dlse is not None
    wq, wk = hq * hd, hk * hd
    grp = hq // hk
    ns, nb = plan.ns, plan.nb
    qlike = [qv, dov, ov, lv] + ([plan.view(dlse)] if has_dl else [])
    ops = [kv, vv] + qlike + qlike
    specs = [plan.main(kv), plan.main(vv)] + [plan.main(t) for t in qlike] + [plan.edge(t, True) for t in qlike]
    nql = len(qlike)
    nkg = wk // LANE

    def body(*refs):
        k_ref, v_ref = refs[:2]
        mains, edges = refs[2:2 + nql], refs[2 + nql:2 + 2 * nql]
        dk_ref, dv_ref = refs[2 + 2 * nql:]
        i = pl.program_id(1)
        m_cur, m_band = plan.masks()
        has_next = i < plan.grid[1] - 1
        for sg in range(ns):
            for b in range(nb):
                rows = slice(b * BLOCK, (b + 1) * BLOCK)
                nxt_in_main = b + 1 < nb
                nrows = slice((b + 1) * BLOCK, (b + 2) * BLOCK) if nxt_in_main else slice(0, BLOCK)
                nsrc = mains if nxt_in_main else edges
                m_next = m_band if nxt_in_main else (m_band & has_next)
                msk = jnp.concatenate([m_cur, m_cur, m_next, m_next], axis=0)
                nacc = nkg if hk == hq else hk
                dk_acc, dv_acc = [None] * nacc, [None] * nacc
                for pr in range(wq // LANE):
                    cols = slice(sg * wq + pr * LANE, sg * wq + (pr + 1) * LANE)
                    kop = _kv_group(k_ref, rows, sg * wk, pr, grp, hq, hk)
                    vop = _kv_group(v_ref, rows, sg * wk, pr, grp, hq, hk)
                    qs, dos, deltas, lses = [], [], [], []
                    for src, r in ((mains, rows), (nsrc, nrows)):
                        qs.append(_stack_heads(src[0][r, cols].astype(BF16)))
                        do2 = src[1][r, cols].astype(F32)
                        prod = do2 * src[2][r, cols].astype(F32)
                        if has_dl:
                            prod = prod - src[4][r, cols].astype(F32)
                        deltas.append(_per_head_sum(prod))
                        lses.append(_per_head_col(src[3][r, cols]))
                        dos.append(_stack_heads(do2.astype(BF16)))
                    qs4, dos4 = jnp.concatenate(qs, axis=0), jnp.concatenate(dos, axis=0)
                    delta4, lse4 = jnp.concatenate(deltas, axis=0), jnp.concatenate(lses, axis=0)
                    s = lax.dot_general(qs4, kop, _NT, preferred_element_type=F32) * scale
                    pm = jnp.where(msk, jnp.exp(s - lse4), 0.0)
                    dp = lax.dot_general(dos4, vop, _NT, preferred_element_type=F32)
                    ds = (pm * (dp - delta4) * scale).astype(BF16)
                    tv = lax.dot_general(pm.astype(BF16), dos4, _TN, preferred_element_type=F32)
                    tk = lax.dot_general(ds, qs4, _TN, preferred_element_type=F32)
                    ai = pr if hk == hq else (2 * pr) // grp
                    dv_acc[ai] = tv if dv_acc[ai] is None else dv_acc[ai] + tv
                    dk_acc[ai] = tk if dk_acc[ai] is None else dk_acc[ai] + tk
                for kg in range(nkg):
                    cols = slice(sg * wk + kg * LANE, sg * wk + (kg + 1) * LANE)
                    if hk == hq:
                        dkt, dvt = dk_acc[kg], dv_acc[kg]
                    else:
                        both = lambda t: t + _swap_halves(t)
                        h0 = _half(0)
                        dkt = jnp.where(h0, both(dk_acc[2 * kg]), both(dk_acc[2 * kg + 1]))
                        dvt = jnp.where(h0, both(dv_acc[2 * kg]), both(dv_acc[2 * kg + 1]))
                    dk_ref[rows, cols] = dkt.astype(dk_ref.dtype)
                    dv_ref[rows, cols] = dvt.astype(dv_ref.dtype)

    dk, dv = _pcall(
        body, name=name, grid=plan.grid, in_specs=specs, out_specs=[plan.out(wk)] * 2,
        out_shape=[jax.ShapeDtypeStruct((S // dil, dil * wk), out_dtype)] * 2,
        compiler_params=_cparams(("parallel", "parallel")),
    )(*[o_[0] for o_ in ops])
    return dk.reshape(S, wk), dv.reshape(S, wk)


def _causal_block(S):
    return min(512, S)


def _causal_mask(bq):
    qi = lax.broadcasted_iota(jnp.int32, (bq, bq), 0)
    kj = lax.broadcasted_iota(jnp.int32, (bq, bq), 1)
    return kj <= qi


def _half_of(rows, e):
    lane = lax.broadcasted_iota(jnp.int32, (rows, LANE), 1)
    return (lane < LANE // 2) if e == 0 else (lane >= LANE // 2)


def _causal_fwd(q, k, v, *, heads, scale, name, bk=None):
    S = q.shape[0]
    bq = _causal_block(S)
    bk = bq if bk is None else min(bk, bq)
    r = bq // bk
    nq, nk = S // bq, S // bk
    npair = heads // 2
    wv = heads * 64
    c2 = scale * math.log2(math.e)

    def body(q_ref, k_ref, v_ref, o_ref, l_ref, m_sc, l_sc, acc_sc):
        i, t = pl.program_id(0), pl.program_id(1)

        @pl.when(t == 0)
        def _():
            m_sc[...] = jnp.full(m_sc.shape, NEG, F32)
            l_sc[...] = jnp.zeros(l_sc.shape, F32)
            acc_sc[...] = jnp.zeros(acc_sc.shape, F32)

        def step(masked):
            if masked:
                qi = lax.broadcasted_iota(jnp.int32, (bq, bk), 0)
                kj = lax.broadcasted_iota(jnp.int32, (bq, bk), 1) + (t - r * i) * bk
                msk = kj <= qi
            h0 = _half_of(bq, 0)
            ss = []
            for h in range(heads):
                cols = slice(h * LANE, (h + 1) * LANE)
                sh = lax.dot_general(q_ref[:, cols], k_ref[:, cols], _NT, preferred_element_type=F32)
                ss.append(jnp.where(msk, sh, NEG) if masked else sh)
            m_old = [m_sc[h] for h in range(heads)]
            m_new = [jnp.maximum(m_old[h], jnp.max(ss[h], axis=1, keepdims=True)) for h in range(heads)]
            ps = [jnp.exp2((ss[h] - m_new[h]) * c2) for h in range(heads)]
            alpha = [jnp.exp2((m_old[h] - m_new[h]) * c2) for h in range(heads)]
            for h in range(heads):
                l_sc[h] = alpha[h] * l_sc[h] + jnp.sum(ps[h], axis=1, keepdims=True)
                m_sc[h] = m_new[h]
            for pr in range(npair):
                v2 = v_ref[:, pr * LANE:(pr + 1) * LANE].astype(BF16)
                pv = [jnp.dot(ps[2 * pr + e].astype(BF16), v2, preferred_element_type=F32) for e in range(2)]
                acc = acc_sc[pr]
                acc_sc[pr] = jnp.where(h0, alpha[2 * pr] * acc + pv[0], alpha[2 * pr + 1] * acc + pv[1])

        pl.when(t < r * i)(lambda: step(False))
        pl.when((t >= r * i) & (t < r * (i + 1)))(lambda: step(True))

        @pl.when(t == nk - 1)
        def _():
            h0 = _half_of(bq, 0)
            for pr in range(npair):
                l0, l1 = l_sc[2 * pr], l_sc[2 * pr + 1]
                acc = acc_sc[pr]
                cols = slice(pr * LANE, (pr + 1) * LANE)
                o_ref[:, cols] = jnp.where(h0, acc / l0, acc / l1)
                l_ref[:, cols] = jnp.where(h0, m_sc[2 * pr] * scale + jnp.log(l0), m_sc[2 * pr + 1] * scale + jnp.log(l1))

    kvi = lambda i, t: (jnp.minimum(t, r * (i + 1) - 1), 0)
    qs = pl.BlockSpec((bq, heads * LANE), lambda i, t: (i, 0))
    ks = pl.BlockSpec((bk, heads * LANE), kvi)
    vs = pl.BlockSpec((bk, wv), kvi)
    os_ = pl.BlockSpec((bq, wv), lambda i, t: (i, 0))
    return _pcall(
        body, name=name, grid=(nq, nk), in_specs=[qs, ks, vs], out_specs=[os_, os_],
        out_shape=[jax.ShapeDtypeStruct((S, wv), F32)] * 2,
        scratch_shapes=[pltpu.VMEM((heads, bq, 1), F32), pltpu.VMEM((heads, bq, 1), F32), pltpu.VMEM((npair, bq, LANE), F32)],
        compiler_params=_cparams(("parallel", "arbitrary")),
    )(q, k, v)


def _causal_bwd_tiles(q_ref, k_ref, v_ref, do_ref, o_ref, l_ref, heads, scale, msk, bq):
    c2 = scale * math.log2(math.e)
    ss, dps, doms, deltas, lses = [], [], [], [], []
    for pr in range(heads // 2):
        pc = slice(pr * LANE, (pr + 1) * LANE)
        v2 = v_ref[:, pc].astype(BF16)
        dof = do_ref[:, pc].astype(F32)
        prod = dof * o_ref[:, pc]
        do2 = dof.astype(BF16)
        l2 = l_ref[:, pc]
        for e in range(2):
            h = 2 * pr + e
            cols = slice(h * LANE, (h + 1) * LANE)
            hm = _half_of(bq, e)
            dom = jnp.where(hm, do2, jnp.zeros_like(do2))
            ss.append(lax.dot_general(q_ref[:, cols], k_ref[:, cols], _NT, preferred_element_type=F32))
            dps.append(lax.dot_general(dom, v2, _NT, preferred_element_type=F32))
            doms.append(dom)
            deltas.append(jnp.sum(jnp.where(hm, prod, 0.0), axis=1, keepdims=True))
            lses.append(l2[:, e * 64:e * 64 + 1] * (1.0 / scale))
    ps = [jnp.exp2((ss[h] - lses[h]) * c2) for h in range(heads)]
    if msk is not None:
        ps = [jnp.where(msk, t, 0.0) for t in ps]
    dss = [(ps[h] * (dps[h] - deltas[h]) * scale).astype(BF16) for h in range(heads)]
    return ps, dss, doms


def _causal_bwd_tile(q_ref, k_ref, v2, do2, prod, l2, h, e, scale, msk, bq):
    cols = slice(h * LANE, (h + 1) * LANE)
    hm = _half_of(bq, e)
    s = lax.dot_general(q_ref[:, cols], k_ref[:, cols], _NT, preferred_element_type=F32)
    c2 = scale * math.log2(math.e)
    p = jnp.exp2((s - l2[:, e * 64:e * 64 + 1] * (1.0 / scale)) * c2)
    if msk is not None:
        p = jnp.where(msk, p, 0.0)
    dom = jnp.where(hm, do2, jnp.zeros_like(do2))
    delta = jnp.sum(jnp.where(hm, prod, 0.0), axis=1, keepdims=True)
    dp = lax.dot_general(dom, v2, _NT, preferred_element_type=F32)
    ds = (p * (dp - delta) * scale).astype(BF16)
    return p, ds, dom


def _causal_dq(q, k, v, do, o, lse, *, heads, scale, out_dtype=BF16, name):
    S = q.shape[0]
    bq = _causal_block(S)
    nq = S // bq
    npair = heads // 2
    wv = heads * 64

    def body(q_ref, k_ref, v_ref, do_ref, o_ref, l_ref, dq_ref, dq_sc):
        i, t = pl.program_id(0), pl.program_id(1)

        @pl.when(t == 0)
        def _():
            dq_sc[...] = jnp.zeros(dq_sc.shape, F32)

        def step(masked):
            msk = _causal_mask(bq) if masked else None
            for pr in range(npair):
                pc = slice(pr * LANE, (pr + 1) * LANE)
                v2 = v_ref[:, pc].astype(BF16)
                dof = do_ref[:, pc].astype(F32)
                prod = dof * o_ref[:, pc]
                do2 = dof.astype(BF16)
                l2 = l_ref[:, pc]
                for e in range(2):
                    h = 2 * pr + e
                    _, ds, _ = _causal_bwd_tile(q_ref, k_ref, v2, do2, prod, l2, h, e, scale, msk, bq)
                    dq_sc[h] += jnp.dot(ds, k_ref[:, h * LANE:(h + 1) * LANE], preferred_element_type=F32)

        pl.when(t < i)(lambda: step(False))
        pl.when(t == i)(lambda: step(True))

        @pl.when(t == nq - 1)
        def _():
            for h in range(heads):
                dq_ref[:, h * LANE:(h + 1) * LANE] = dq_sc[h].astype(dq_ref.dtype)

    qs = pl.BlockSpec((bq, heads * LANE), lambda i, t: (i, 0))
    ks = pl.BlockSpec((bq, heads * LANE), lambda i, t: (jnp.minimum(t, i), 0))
    vs = pl.BlockSpec((bq, wv), lambda i, t: (jnp.minimum(t, i), 0))
    os_ = pl.BlockSpec((bq, wv), lambda i, t: (i, 0))
    return _pcall(
        body, name=name, grid=(nq, nq), in_specs=[qs, ks, vs, os_, os_, os_], out_specs=qs,
        out_shape=jax.ShapeDtypeStruct((S, heads * LANE), out_dtype),
        scratch_shapes=[pltpu.VMEM((heads, bq, LANE), F32)],
        compiler_params=_cparams(("parallel", "arbitrary")),
    )(q, k, v, do, o, lse)


def _causal_dkv(q, k, v, do, o, lse, *, heads, scale, out_dtype=BF16, name):
    S = q.shape[0]
    bq = _causal_block(S)
    nq = S // bq
    npair = heads // 2
    wv = heads * 64

    def body(q_ref, k_ref, v_ref, do_ref, o_ref, l_ref, dk_ref, dv_ref, dk_sc, dv_sc):
        j, t = pl.program_id(0), pl.program_id(1)

        @pl.when(t == 0)
        def _():
            dk_sc[...] = jnp.zeros(dk_sc.shape, F32)
            dv_sc[...] = jnp.zeros(dv_sc.shape, F32)

        def step(masked):
            msk = _causal_mask(bq) if masked else None
            ps, dss, doms = _causal_bwd_tiles(q_ref, k_ref, v_ref, do_ref, o_ref, l_ref, heads, scale, msk, bq)
            for pr in range(npair):
                tv = [lax.dot_general(ps[2 * pr + e].astype(BF16), doms[2 * pr + e], _TN, preferred_element_type=F32)
                      for e in range(2)]
                dv_sc[pr] += tv[0] + tv[1]
            for h in range(heads):
                dk_sc[h] += lax.dot_general(dss[h], q_ref[:, h * LANE:(h + 1) * LANE], _TN, preferred_element_type=F32)

        pl.when(t > j)(lambda: step(False))
        pl.when(t == j)(lambda: step(True))

        @pl.when(t == nq - 1)
        def _():
            for h in range(heads):
                dk_ref[:, h * LANE:(h + 1) * LANE] = dk_sc[h].astype(dk_ref.dtype)
            for pr in range(npair):
                dv_ref[:, pr * LANE:(pr + 1) * LANE] = dv_sc[pr].astype(dv_ref.dtype)

    qi = lambda j, t: (jnp.maximum(t, j), 0)
    qs = pl.BlockSpec((bq, heads * LANE), qi)
    os_ = pl.BlockSpec((bq, wv), qi)
    ks = pl.BlockSpec((bq, heads * LANE), lambda j, t: (j, 0))
    vs = pl.BlockSpec((bq, wv), lambda j, t: (j, 0))
    return _pcall(
        body, name=name, grid=(nq, nq), in_specs=[qs, ks, vs, os_, os_, os_], out_specs=[ks, vs],
        out_shape=[jax.ShapeDtypeStruct((S, heads * LANE), out_dtype), jax.ShapeDtypeStruct((S, wv), out_dtype)],
        scratch_shapes=[pltpu.VMEM((heads, bq, LANE), F32), pltpu.VMEM((npair, bq, LANE), F32)],
        compiler_params=_cparams(("parallel", "arbitrary")),
    )(q, k, v, do, o, lse)


@jax.custom_vjp
def _bdot(x, w):
    return jnp.dot(x.astype(BF16), w.astype(BF16), preferred_element_type=F32)


def _bdot_fwd(x, w):
    return _bdot(x, w), (x, w)


def _bdot_bwd(res, g):
    x, w = res
    gb = g.astype(BF16)
    dx = lax.dot_general(gb, w.astype(BF16), _NT, preferred_element_type=F32)
    dw = lax.dot_general(x.astype(BF16), gb, _TN, preferred_element_type=F32)
    return dx.astype(x.dtype), dw.astype(w.dtype)


_bdot.defvjp(_bdot_fwd, _bdot_bwd)


def _tile_matrix(hd, width):
    r = lax.broadcasted_iota(jnp.int32, (hd, width), 0)
    c = lax.broadcasted_iota(jnp.int32, (hd, width), 1) % hd
    return jnp.where(r == c, 1.0, 0.0).astype(F32)


def _spread_matrix(heads, width):
    per = width // heads
    r = lax.broadcasted_iota(jnp.int32, (heads, width), 0)
    c = lax.broadcasted_iota(jnp.int32, (heads, width), 1) // per
    return jnp.where(r == c, 1.0, 0.0).astype(F32)


def _wide(t, width):
    n = width // t.shape[-1]
    return jnp.concatenate([t] * n, axis=1) if n > 1 else t


def _norm_heads(x, gain, hd):
    return _head_norm(x, _exact_dot(gain, _tile_matrix(hd, x.shape[-1])), hd)


def _norm_rope(x, gain, cos, sin, hd):
    w = x.shape[-1]
    return _rope(_norm_heads(x, gain, hd), _wide(cos, w), _wide(sin, w), hd)


def _f_norm(x, g):
    return (_row_norm(x, g),)


def _f_prep_acm(aq, ak, c0q, c0k, c0v, c1q, c1k, c1v, c2q, c2k, c2v, mq, cos, sin, g_aq, g_ak, g0q, g0k, g1q, g1k, g2q, g2k, g_mq):
    outs = [_norm_rope(aq, g_aq, cos, sin, A_HD), _norm_rope(ak, g_ak, cos, sin, A_HD)]
    for cq, ck, cv, gq, gk in ((c0q, c0k, c0v, g0q, g0k), (c1q, c1k, c1v, g1q, g1k), (c2q, c2k, c2v, g2q, g2k)):
        outs += [_norm_rope(cq, gq, cos, sin, C_HD), _norm_rope(ck, gk, cos, sin, C_HD), cv]
    outs.append(_norm_heads(mq, g_mq, M_HD))
    return tuple(outs)


def _rope_mla_q(x, cos, sin):
    lane = lax.broadcasted_iota(jnp.int32, x.shape, x.ndim - 1) % LANE
    half = B_ROPE // 2
    first = (lane >= B_NOPE) & (lane < B_NOPE + half)
    other = jnp.where(first, -_lane_roll(x, -half), _lane_roll(x, half))
    return x * cos + other * sin


def _f_prep_b(ckv, cq, kr, cosq, sinq, cosr, sinr, g_qa, g_kva, w_uq, w_ukv, g_q, g_k, g_kr):
    wq = B_HEADS * LANE
    q_up = _bdot(_row_norm(cq, g_qa), w_uq)
    gq = _exact_dot(g_q, _tile_matrix(LANE, wq))
    qf = _rope_mla_q(_head_norm(q_up, gq, "mla"), _wide(cosq, wq), _wide(sinq, wq))
    kv_up = _bdot(_row_norm(ckv, g_kva), w_ukv)
    kn = _head_norm(kv_up[:, :wq], _exact_dot(g_k, _tile_matrix(LANE, wq)), B_NOPE)
    vb = kv_up[:, wq:]
    kp = _rope(_head_norm(kr, g_kr, B_ROPE), cosr, sinr, B_ROPE)
    kp = _lane_roll(kp, B_NOPE)
    return qf, kn + _wide(kp, wq), vb


def _f_mem_k(k, g):
    return (_norm_heads(k, g, M_HD),)


def _f_sink(o, lse, sink):
    sb = _exact_dot(sink, _spread_matrix(A_HEADS, o.shape[-1]))
    m = jnp.maximum(lse, sb)
    tot = m + jnp.log(jnp.exp(lse - m) + jnp.exp(sb - m))
    return (o * jnp.exp(lse - tot),)


def _f_combine(o0, o1, o2, l0, l1, l2):
    m = jnp.maximum(jnp.maximum(l0, l1), l2)
    e0, e1, e2 = jnp.exp(l0 - m), jnp.exp(l1 - m), jnp.exp(l2 - m)
    inv = 1.0 / (e0 + e1 + e2)
    return ((e0 * o0 + e1 * o1 + e2 * o2) * inv,)


def _f_gatemix(gp, y0, y1, y2, y3, bg):
    d = y0.shape[-1]
    gates = 1.0 / (1.0 + jnp.exp(-(gp + bg)))
    mix = gates[:, :d] * y0
    for n, y in enumerate((y1, y2, y3), start=1):
        mix = mix + gates[:, n * d:(n + 1) * d] * y
    return (mix,)


def _relu2(u):
    return jnp.square(jnp.maximum(u, 0.0))


def _add(r, e):
    return r + e.astype(F32)


def _relu2_grad(r, u):
    return r * (2.0 * jnp.maximum(u, 0.0))


def _loss_and_grad(y, target, *, tb=512):
    R, D = y.shape
    tb = min(tb, R)

    def body(y_ref, t_ref, dy_ref, l_ref):
        err = y_ref[...] - t_ref[...]
        dy_ref[...] = err * (1.0 / D)
        part = 0.5 * jnp.sum(jnp.sum(err * err, axis=1, keepdims=True) * (1.0 / D), axis=0, keepdims=True)
        first = pl.program_id(0) == 0

        @pl.when(first)
        def _():
            l_ref[...] = jnp.broadcast_to(part, l_ref.shape)

        @pl.when(jnp.logical_not(first))
        def _():
            l_ref[...] += jnp.broadcast_to(part, l_ref.shape)

    dy, l = _pcall(
        body, name="loss", grid=(R // tb,),
        in_specs=[pl.BlockSpec((tb, D), lambda i: (i, 0))] * 2,
        out_specs=[pl.BlockSpec((tb, D), lambda i: (i, 0)), pl.BlockSpec((8, LANE), lambda i: (0, 0))],
        out_shape=[jax.ShapeDtypeStruct((R, D), F32), jax.ShapeDtypeStruct((8, LANE), F32)],
        compiler_params=_cparams(("arbitrary",)),
    )(y, target)
    return l[0, 0], dy


def _z_layout(d):
    assert d == 1024, "the aligned layout below is laid out for D_MODEL = 1024"
    lay = {"gates": (4 * d, 0)}
    for g in range(3):
        for n, nm in enumerate("qkv"):
            lay[f"c{g}{nm}"] = (512, 8 + 3 * g + n)
    lay.update(aq=(512, 17), mq=(512, 18), ckv=(256, 38), cq=(384, 26), ak=(128, 81), av=(128, 82), kr=(128, 83))
    return lay, 10752


_KW_A = dict(hq=A_HEADS, hk=A_KV_HEADS, scale=A_HD ** -0.5)
_KW_B = dict(heads=B_HEADS, scale=(B_NOPE + B_ROPE) ** -0.5)
_KW_C = dict(hq=C_HEADS, hk=C_HEADS, scale=C_HD ** -0.5)
_KW_M = dict(hq=M_HEADS, hk=M_HEADS, hd=M_HD, hdv=M_HD, scale=M_HD ** -0.5)


def _layer_fwd(l, x, mem, w, tabs, late=None):
    S, D = x.shape
    lay, _ = _z_layout(D)
    cosA, sinA, cosB, sinB, cosQ, sinQ = tabs
    gM = _AttnGeom("full", S, mem.shape[0])
    nm = lambda s: f"l{l}_{s}"
    sv = {}
    hn = _rowmap(_f_norm, [x], [w["g_mix"]], [(D, BF16)], tb=512, name=nm("norm1"))[0]
    z = _mm(hn, w["in"], name=nm("in"), tn=1536)
    zp = {k: (z, wd, idx) for k, (wd, idx) in lay.items()}
    acm_rows = [zp[k] for k in ("aq", "ak", "c0q", "c0k", "c0v", "c1q", "c1k", "c1v", "c2q", "c2k", "c2v", "mq")] + [cosA, sinA]
    acm_par = [w["a_qn"], w["a_kn"], w["c0q"], w["c0k"], w["c1q"], w["c1k"], w["c2q"], w["c2k"], w["m_qn"]]
    acm = _rowmap(_f_prep_acm, acm_rows, acm_par, [(p[1], BF16) for p in acm_rows[:12]], tb=256, name=nm("prep_acm"))
    qa, ka, qc0, kc0, vc0, qc1, kc1, vc1, qc2, kc2, vc2, mq = acm
    oa_raw, lse_a = _band_fwd(qa, ka, zp["av"], S=S, dil=1, max_dist=A_WINDOW - 1, name=nm("attn_a"), **_KW_A)
    o_a = _rowmap(_f_sink, [oa_raw, lse_a], [w["a_sink"]], [(512, BF16)], tb=512, name=nm("sink"))[0]
    oc, lc = [], []
    for g, ((win, dil), qc, kc, vc) in enumerate(zip(C_PATTERNS, (qc0, qc1, qc2), (kc0, kc1, kc2), (vc0, vc1, vc2))):
        o_g, l_g = _band_fwd(qc, kc, vc, S=S, dil=dil, max_dist=win // dil, name=nm(f"attn_c{g}"), **_KW_C)
        oc.append(o_g)
        lc.append(l_g)
    o_c = _rowmap(_f_combine, oc + lc, [], [(512, BF16)], tb=512, name=nm("combine"))[0]
    if late is not None:
        w = dict(w, **late(o_c))

    b_rows = [zp["ckv"], zp["cq"], zp["kr"], cosQ, sinQ, cosB, sinB]
    b_par = [w["b_qa"], w["b_kva"], w["uq"], w["ukv"], w["b_q"], w["b_k"], w["b_kr"]]
    qb, kb, vb = _rowmap(_f_prep_b, b_rows, b_par, [(B_HEADS * LANE, BF16), (B_HEADS * LANE, BF16), (512, BF16)], tb=256,
                         name=nm("prep_b"))
    memn = _rowmap(_f_norm, [mem], [w["m_g_mem"]], [(D, BF16)], tb=256, name=nm("mem_norm"))[0]
    mkv = _mm(memn, w["mkv"], name=nm("mem_kv"))
    mk = _rowmap(_f_mem_k, [(mkv, 512, 0)], [w["m_kn"]], [(512, BF16)], tb=256, name=nm("mem_k"))[0]
    mv = (mkv, 512, 1)
    o_b, lse_b = _causal_fwd(qb, kb, vb, name=nm("attn_b"), **_KW_B)
    o_m, lse_m = _attn_fwd(gM, mq, mk, mv, name=nm("attn_m"), **_KW_M)

    o_n = [o_a, o_b, o_c, o_m]
    ys = [_mm(o_n[n], w["branch"][n], name=nm(f"branch{n}")) for n in range(N_BRANCH)]
    mix = _rowmap(_f_gatemix, [zp["gates"]] + ys, [w["b_gate"]], [(D, BF16)], tb=256, name=nm("gatemix"))[0]
    x1 = _mm(mix, w["out"], extra=x, epi=_add, name=nm("out"))
    hn2 = _rowmap(_f_norm, [x1], [w["g_mlp"]], [(D, BF16)], tb=512, name=nm("norm2"))[0]
    u = _mm(hn2, w["up"], name=nm("up"))
    x2 = _mm(u, w["down"], pro_a=_relu2, extra=x1, epi=_add, name=nm("down"))
    sv.update(x=x, hn=hn, z=z, acm=acm, bq=(qb, kb, vb), memn=memn, mkv=mkv, mk=mk, oa_raw=oa_raw, lse_a=lse_a,
              o_b=o_b, lse_b=lse_b, oc=oc, lc=lc, o_m=o_m, lse_m=lse_m, o_n=o_n, ys=ys, mix=mix, x1=x1, hn2=hn2, u=u)
    return x2, sv, w


def _layer_bwd(l, dx2, mem, w, tabs, sv, early=None):
    x, z, x1, u = sv["x"], sv["z"], sv["x1"], sv["u"]
    S, D = x.shape
    lay, zw = _z_layout(D)
    cosA, sinA, cosB, sinB, cosQ, sinQ = tabs
    gM = _AttnGeom("full", S, mem.shape[0])
    nm = lambda s: f"l{l}_{s}"
    zp = {k: (z, wd, idx) for k, (wd, idx) in lay.items()}
    g = {}
    du = _mm(dx2, w["down"], tb=True, extra=u, epi=_relu2_grad, out_dtype=BF16, name=nm("d_down_x"))
    g["down"] = _mm(u, dx2, ta=True, pro_a=_relu2, name=nm("d_down_w"))
    dhn2 = _mm(du, w["up"], tb=True, name=nm("d_up_x"))
    g["up"] = _mm(sv["hn2"], du, ta=True, name=nm("d_up_w"))
    (dx1,), (g["g_mlp"],) = _rowmap_bwd(_f_norm, [x1], [w["g_mlp"]], [dhn2], diff=[True], out_dtypes=[F32], adds=[dx2],
                                        tb=256, name=nm("d_norm2"))
    dmix = _mm(dx1, w["out"], tb=True, name=nm("d_out_x"))
    g["out"] = _mm(sv["mix"], dx1, ta=True, name=nm("d_out_w"))
    (dgates, dy0, dy1, dy2, dy3), (g["b_gate"],) = _rowmap_bwd(
        _f_gatemix, [zp["gates"]] + sv["ys"], [w["b_gate"]], [dmix], diff=[True] * 5, out_dtypes=[BF16] * 5,
        tb=128, name=nm("d_gatemix"))
    dys = [dy0, dy1, dy2, dy3]
    do = [_mm(dys[n], w["branch"][n], tb=True, name=nm(f"d_branch{n}_x")) for n in range(N_BRANCH)]
    g["branch"] = [_mm(sv["o_n"][n], dys[n], ta=True, name=nm(f"d_branch{n}_w")) for n in range(N_BRANCH)]
    qa, ka, qc0, kc0, vc0, qc1, kc1, vc1, qc2, kc2, vc2, mq = sv["acm"]
    qb, kb, vb = sv["bq"]
    mkv, mk = sv["mkv"], sv["mk"]
    mv = (mkv, 512, 1)
    dmq = _attn_dq(gM, mq, mk, mv, do[3], sv["o_m"], sv["lse_m"], None, name=nm("attn_m_dq"), **_KW_M)
    dmk, dmv = _attn_dkv(gM, mq, mk, mv, do[3], sv["o_m"], sv["lse_m"], None, name=nm("attn_m_dkv"), **_KW_M)
    (doc0, doc1, doc2, dl0, dl1, dl2), _ = _rowmap_bwd(_f_combine, sv["oc"] + sv["lc"], [], [do[2]], diff=[True] * 6,
                                                      out_dtypes=[F32] * 6, tb=256, name=nm("d_combine"))
    dqc, dkc, dvc = [], [], []
    for gi, ((win, dil), qc, kc, vc, doc, dl) in enumerate(zip(C_PATTERNS, (qc0, qc1, qc2), (kc0, kc1, kc2), (vc0, vc1, vc2),
                                                              (doc0, doc1, doc2), (dl0, dl1, dl2))):
        args = (qc, kc, vc, doc, sv["oc"][gi], sv["lc"][gi], dl)
        kwc = dict(S=S, dil=dil, max_dist=win // dil, out_dtype=BF16, **_KW_C)
        dqc.append(_band_dq(*args, name=nm(f"attn_c{gi}_dq"), **kwc))
        dk_, dv_ = _band_dkv(*args, name=nm(f"attn_c{gi}_dkv"), **kwc)
        dkc.append(dk_)
        dvc.append(dv_)
    argsb = (qb, kb, vb, do[1], sv["o_b"], sv["lse_b"])
    dqb = _causal_dq(*argsb, name=nm("attn_b_dq"), **_KW_B)
    dkb, dvb = _causal_dkv(*argsb, name=nm("attn_b_dkv"), **_KW_B)
    (doa_raw, dlse_a), (g["a_sink"],) = _rowmap_bwd(_f_sink, [sv["oa_raw"], sv["lse_a"]], [w["a_sink"]], [do[0]],
                                                   diff=[True, True], out_dtypes=[F32, F32], tb=256, name=nm("d_sink"))
    argsa = (qa, ka, zp["av"], doa_raw, sv["oa_raw"], sv["lse_a"], dlse_a)
    kwa = dict(S=S, dil=1, max_dist=A_WINDOW - 1, out_dtype=BF16, **_KW_A)
    dqa = _band_dq(*argsa, name=nm("attn_a_dq"), **kwa)
    dka, dva = _band_dkv(*argsa, name=nm("attn_a_dkv"), **kwa)
    acm_rows = [zp[k] for k in ("aq", "ak", "c0q", "c0k", "c0v", "c1q", "c1k", "c1v", "c2q", "c2k", "c2v", "mq")] + [cosA, sinA]
    acm_par = [w["a_qn"], w["a_kn"], w["c0q"], w["c0k"], w["c1q"], w["c1k"], w["c2q"], w["c2k"], w["m_qn"]]
    acm_ct = [dqa, dka, dqc[0], dkc[0], dvc[0], dqc[1], dkc[1], dvc[1], dqc[2], dkc[2], dvc[2], dmq]
    dacm, (g["a_qn"], g["a_kn"], g["c0q"], g["c0k"], g["c1q"], g["c1k"], g["c2q"], g["c2k"], g["m_qn"]) = _rowmap_bwd(
        _f_prep_acm, acm_rows, acm_par, acm_ct, diff=[True] * 12 + [False, False], out_dtypes=[BF16] * 12, tb=256,
        name=nm("d_prep_acm"))
    d_aq, d_ak, d_c0q, d_c0k, d_c0v, d_c1q, d_c1k, d_c1v, d_c2q, d_c2k, d_c2v, d_mq = dacm
    b_rows = [zp["ckv"], zp["cq"], zp["kr"], cosQ, sinQ, cosB, sinB]
    b_par = [w["b_qa"], w["b_kva"], w["uq"], w["ukv"], w["b_q"], w["b_k"], w["b_kr"]]
    (d_ckv, d_cq, d_kr), gb = _rowmap_bwd(_f_prep_b, b_rows, b_par, [dqb, dkb, dvb], diff=[True] * 3 + [False] * 4,
                                          out_dtypes=[BF16] * 3, tb=256, name=nm("d_prep_b"))
    g["b_qa"], g["b_kva"], g["uq"], g["ukv"], g["b_q"], g["b_k"], g["b_kr"] = gb
    (dmkv_k,), (g["m_kn"],) = _rowmap_bwd(_f_mem_k, [(mkv, 512, 0)], [w["m_kn"]], [dmk], diff=[True], out_dtypes=[F32],
                                          tb=256, name=nm("d_mem_k"))
    dmkv = jnp.concatenate([dmkv_k, dmv], axis=1)
    dmemn = _mm(dmkv, w["mkv"], tb=True, name=nm("d_mem_kv_x"))
    g["mkv"] = _mm(sv["memn"], dmkv, ta=True, name=nm("d_mem_kv_w"))
    _, (g["m_g_mem"],) = _rowmap_bwd(_f_norm, [mem], [w["m_g_mem"]], [dmemn], diff=[True], out_dtypes=[F32], tb=256,
                                     name=nm("d_mem_norm"))
    if early is not None:
        early(g)
    dz = jnp.concatenate([dgates, d_c0q, d_c0k, d_c0v, d_c1q, d_c1k, d_c1v, d_c2q, d_c2k, d_c2v, d_aq, d_mq, d_ckv, d_cq,
                          d_ak, dva, d_kr], axis=1)
    assert dz.shape[1] == zw
    dhn = _mm(dz, w["in"], tb=True, name=nm("d_in_x"), tk=1536)
    g["in"] = _mm(sv["hn"], dz, ta=True, name=nm("d_in_w"), tn=1536)
    (dx,), (g["g_mix"],) = _rowmap_bwd(_f_norm, [x], [w["g_mix"]], [dhn], diff=[True], out_dtypes=[F32], adds=[dx1],
                                       tb=256, name=nm("d_norm1"))
    return dx, g


_IN_ORIG = dict(aq=(0, 512), ak=(512, 640), av=(640, 768), cq=(768, 1152), ckv=(1152, 1408), kr=(1408, 1440),
                c=(1440, 6048), mq=(6048, 6560), gates=(6560, 10656))
_IN_OURS = dict(gates=(0, 4096), c=(4096, 8704), aq=(8704, 9216), mq=(9216, 9728), ckv=(9728, 9984), cq=(9984, 10368),
                ak=(10368, 10496), av=(10496, 10624), kr=(10624, 10656))
_IN_ORDER_ORIG = ("aq", "ak", "av", "cq", "ckv", "kr", "c", "mq", "gates")


def _in_to_ours(w_in):
    pc = {k: w_in[..., a:b] for k, (a, b) in _IN_ORIG.items()}
    zeros = lambda n: jnp.zeros(w_in.shape[:-1] + (n,), w_in.dtype)
    return jnp.concatenate([pc["gates"], pc["c"], pc["aq"], pc["mq"], pc["ckv"], pc["cq"], pc["ak"], pc["av"], pc["kr"],
                            zeros(96)], axis=-1)


def _in_from_ours(g_in):
    return jnp.concatenate([g_in[..., _IN_OURS[k][0]:_IN_OURS[k][1]] for k in _IN_ORDER_ORIG], axis=-1)


def _uq_to_ours(w):
    per = B_NOPE + B_ROPE
    w4 = w.reshape(w.shape[:-1] + (B_HEADS, per))
    w4 = jnp.pad(w4, [(0, 0)] * (w4.ndim - 1) + [(0, LANE - per)])
    return w4.reshape(w.shape[:-1] + (B_HEADS * LANE,))


def _uq_from_ours(g):
    per = B_NOPE + B_ROPE
    g4 = g.reshape(g.shape[:-1] + (B_HEADS, LANE))[..., :per]
    return g4.reshape(g.shape[:-1] + (B_HEADS * per,))


def _ukv_to_ours(w):
    w4 = w.reshape(w.shape[:-1] + (B_HEADS, B_NOPE + B_V))
    keys = jnp.pad(w4[..., :B_NOPE], [(0, 0)] * (w4.ndim - 1) + [(0, LANE - B_NOPE)])
    vals = w4[..., B_NOPE:]
    return jnp.concatenate([keys.reshape(w.shape[:-1] + (B_HEADS * LANE,)), vals.reshape(w.shape[:-1] + (B_HEADS * B_V,))],
                           axis=-1)


def _ukv_from_ours(g):
    wq = B_HEADS * LANE
    keys = g[..., :wq].reshape(g.shape[:-1] + (B_HEADS, LANE))[..., :B_NOPE]
    vals = g[..., wq:].reshape(g.shape[:-1] + (B_HEADS, B_V))
    return jnp.concatenate([keys, vals], axis=-1).reshape(g.shape[:-1] + (B_HEADS * (B_NOPE + B_V),))


def _layer_weights(big_l, small, l):
    row = lambda a: a[l].reshape(1, -1)
    w = dict(big_l)
    w.update(g_mix=row(small["g_mix"]), b_gate=row(small["b_gate"]), a_qn=row(small["a_qn"]), a_kn=row(small["a_kn"]),
             a_sink=row(small["a_sink"]), b_qa=row(small["b_qa_norm"]), b_kva=row(small["b_kva_norm"]),
             b_q=jnp.pad(small["b_qn"][l], (0, LANE - B_NOPE - B_ROPE)).reshape(1, -1),
             b_k=jnp.pad(small["b_kn"][l, :B_NOPE], (0, LANE - B_NOPE)).reshape(1, -1),
             b_kr=jnp.pad(small["b_kn"][l, B_NOPE:], (0, LANE - B_ROPE)).reshape(1, -1),
             m_g_mem=row(small["m_g_mem"]), m_qn=row(small["m_qn"]), m_kn=row(small["m_kn"]), g_mlp=row(small["g_mlp"]))
    for g in range(3):
        w[f"c{g}q"] = small["c_qn"][l, g].reshape(1, -1)
        w[f"c{g}k"] = small["c_kn"][l, g].reshape(1, -1)
    return w


def _rope_tables(positions):
    pos = positions.astype(F32)[:, None]
    tabs = []
    for dim in (A_HD, B_ROPE):
        inv = ROPE_THETA ** (-jnp.arange(0, dim, 2, dtype=F32) / dim)
        ang = pos * inv
        reps = LANE // (dim // 2)
        tabs += [jnp.tile(jnp.cos(ang), (1, reps)), jnp.tile(jnp.sin(ang), (1, reps))]
    half = B_ROPE // 2
    cb, sb = tabs[2][:, :half], tabs[3][:, :half]
    ones, zeros = jnp.ones((pos.shape[0], B_NOPE), F32), jnp.zeros((pos.shape[0], B_NOPE), F32)
    pad = LANE - B_NOPE - B_ROPE
    tabs.append(jnp.concatenate([ones, cb, cb, ones[:, :pad]], axis=1))
    tabs.append(jnp.concatenate([zeros, sb, sb, zeros[:, :pad]], axis=1))
    return tuple(tabs)


def _local_step(x, mem, positions, small, loss_target, get_big, put_grads, early_grads=None):
    depth = small["g_mix"].shape[0]
    tabs = _rope_tables(positions)
    ws, saved = [], []
    h = x
    for l in range(depth):
        first, late = get_big(l, h)
        h, sv, wl = _layer_fwd(l, h, mem, _layer_weights(first, small, l), tabs, late)
        ws.append(wl)
        saved.append(sv)
    loss, dh = _loss_and_grad(h, loss_target)
    small_grads = [None] * depth
    for l in reversed(range(depth)):
        dh, g = _layer_bwd(l, dh, mem, ws[l], tabs, saved[l], early_grads if l == 0 else None)
        zero = put_grads(l, g)
        if l > 0:
            ws[l - 1] = dict(ws[l - 1], g_mlp=ws[l - 1]["g_mlp"] + zero)
        small_grads[l] = g
    return loss, dh, small_grads


def _small_grads_to_reference_layout(grads):
    flat = lambda k: jnp.stack([g[k].reshape(-1) for g in grads])
    return dict(
        g_mix=flat("g_mix"), b_gate=flat("b_gate"), a_qn=flat("a_qn"), a_kn=flat("a_kn"), a_sink=flat("a_sink"),
        b_qa_norm=flat("b_qa"), b_kva_norm=flat("b_kva"), b_qn=flat("b_q")[:, :B_NOPE + B_ROPE],
        b_kn=jnp.concatenate([flat("b_k")[:, :B_NOPE], flat("b_kr")[:, :B_ROPE]], axis=1),
        c_qn=jnp.stack([jnp.stack([g[f"c{i}q"].reshape(-1) for i in range(3)]) for g in grads]),
        c_kn=jnp.stack([jnp.stack([g[f"c{i}k"].reshape(-1) for i in range(3)]) for g in grads]),
        m_g_mem=flat("m_g_mem"), m_qn=flat("m_qn"), m_kn=flat("m_kn"), g_mlp=flat("g_mlp"))


_BIG_GRAD = {"w_in": lambda g: _in_from_ours(g["in"]), "b_w_uq": lambda g: _uq_from_ours(g["uq"]),
             "b_w_ukv": lambda g: _ukv_from_ours(g["ukv"]), "m_w_kv": lambda g: g["mkv"], "w_branch": lambda g: jnp.stack(g["branch"]),
             "w_out": lambda g: g["out"], "w_up": lambda g: g["up"], "w_down": lambda g: g["down"]}


def _big_grads_to_reference_layout(g, names=None):
    return {k: _BIG_GRAD[k](g) for k in (_BIG_GRAD if names is None else names)}


def _big_to_kernel_layout(full):
    conv = {"w_in": ("in", _in_to_ours), "b_w_uq": ("uq", _uq_to_ours), "b_w_ukv": ("ukv", _ukv_to_ours),
            "m_w_kv": ("mkv", None), "w_branch": ("branch", None), "w_out": ("out", None), "w_up": ("up", None),
            "w_down": ("down", None)}
    out = {}
    for k, a in full.items():
        name, fn = conv[k]
        out[name] = (a if fn is None else fn(a)).astype(BF16)
    return out


MESH = pl.DeviceIdType.MESH
N_CHIPS = 4
N_DEV = 8
_ANY = pl.BlockSpec(memory_space=pl.ANY)


_HBM = pl.BlockSpec(memory_space=pltpu.HBM)
_SEM = pl.BlockSpec(memory_space=pltpu.SEMAPHORE)
_EFFECT = pltpu.SideEffectType.DATAFLOW_SIDE_EFFECTING


def _chip_peers():
    x, y, c = lax.axis_index("x"), lax.axis_index("y"), lax.axis_index("c")
    return 2 * x + y, [((1 - x, y, c), 2 * (1 - x) + y), ((x, 1 - y, c), 2 * x + 1 - y), ((1 - x, 1 - y, c), 2 * (1 - x) + 1 - y)]


def _exchange_start(srcs, lands, *, gather, name):
    n = len(srcs)

    def body(*refs):
        ins, land = refs[:n], refs[n:2 * n]
        send_sems, recv_sems = refs[2 * n], refs[2 * n + 1]
        token = refs[-1]
        me, peers = _chip_peers()
        for a in range(n):
            for j, (dev, chip) in enumerate(peers):
                src = ins[a] if gather else ins[a].at[chip]
                pltpu.make_async_remote_copy(src_ref=src, dst_ref=land[a].at[me], send_sem=send_sems.at[3 * a + j],
                                             recv_sem=recv_sems.at[3 * a + j], device_id=dev, device_id_type=MESH).start()
        token[...] = jnp.zeros(token.shape, token.dtype)

    hbm = lambda a: pltpu.HBM(a.shape, a.dtype)
    outs = _pcall(
        body, name=name,
        out_shape=(pltpu.SemaphoreType.DMA((3 * n,)), pltpu.SemaphoreType.DMA((3 * n,)), *[hbm(a) for a in srcs],
                   *[hbm(a) for a in lands], jax.ShapeDtypeStruct((8, LANE), F32)),
        in_specs=[_HBM] * (2 * n), out_specs=(_SEM, _SEM, *([_HBM] * (2 * n)), pl.BlockSpec(memory_space=pltpu.VMEM)),
        input_output_aliases={a: 2 + a for a in range(2 * n)},
        compiler_params=pltpu.CompilerParams(has_side_effects=_EFFECT),
    )(*[pltpu.with_memory_space_constraint(a, pltpu.HBM) for a in srcs],
      *[pltpu.with_memory_space_constraint(a, pltpu.HBM) for a in lands])
    return outs[0], outs[1], list(outs[2:2 + n]), list(outs[2 + n:2 + 2 * n]), outs[-1]


def _exchange_wait(state, after, *, gather, name):
    send_sems, recv_sems, srcs, lands, _ = state
    n = len(lands)

    def body(*refs):
        src_refs, land = refs[:n], refs[n:2 * n]
        ssem, rsem = refs[2 * n], refs[2 * n + 1]
        me, peers = _chip_peers()
        for a in range(n):
            for j, (dev, chip) in enumerate(peers):
                src = src_refs[a] if gather else src_refs[a].at[chip]
                cp = pltpu.make_async_remote_copy(src_ref=src, dst_ref=land[a].at[chip], send_sem=ssem.at[3 * a + j],
                                                  recv_sem=rsem.at[3 * a + j], device_id=dev, device_id_type=MESH)
                cp.wait_send()
                cp.wait_recv()

    outs = _pcall(
        body, name=name,
        out_shape=tuple(pltpu.HBM(a.shape, a.dtype) for a in list(srcs) + list(lands)),
        in_specs=[_HBM] * (2 * n) + [_SEM, _SEM, pl.BlockSpec(memory_space=pl.ANY)],
        out_specs=tuple([_HBM] * (2 * n)), input_output_aliases={a: a for a in range(2 * n)},
        compiler_params=pltpu.CompilerParams(has_side_effects=_EFFECT),
    )(*srcs, *lands, send_sems, recv_sems, after)
    return list(outs[:n]), list(outs[n:])


def _sibling_exchange(arrays, *, name):
    n = len(arrays)

    def body(*refs):
        ins, outs = refs[:n], refs[n:2 * n]
        send_sems, recv_sems = refs[2 * n:]
        x, y, c = lax.axis_index("x"), lax.axis_index("y"), lax.axis_index("c")
        cps = []
        for a in range(n):
            cp = pltpu.make_async_remote_copy(src_ref=ins[a], dst_ref=outs[a], send_sem=send_sems.at[a], recv_sem=recv_sems.at[a],
                                              device_id=(x, y, 1 - c), device_id_type=MESH)
            cp.start()
            cps.append(cp)
        for cp in cps:
            cp.wait()

    return _pcall(
        body, name=name, in_specs=[_ANY] * n, out_specs=[_ANY] * n,
        out_shape=[jax.ShapeDtypeStruct(a.shape, a.dtype) for a in arrays],
        scratch_shapes=[pltpu.SemaphoreType.DMA((n,)), pltpu.SemaphoreType.DMA((n,))],
        compiler_params=pltpu.CompilerParams(has_side_effects=True),
    )(*arrays)


def _allreduce_small(v, *, name):
    rows = v.shape[0]

    def body(v_ref, o_ref, slots, send_sems, recv_sems):
        x, y, c = lax.axis_index("x"), lax.axis_index("y"), lax.axis_index("c")
        me = 4 * x + 2 * y + c
        slots[me] = v_ref[...]
        cps = []
        for k in range(1, N_DEV):
            fx, fy, fc = (k >> 2) & 1, (k >> 1) & 1, k & 1
            peer = (x ^ fx, y ^ fy, c ^ fc)
            cp = pltpu.make_async_remote_copy(src_ref=v_ref, dst_ref=slots.at[me], send_sem=send_sems.at[k - 1],
                                              recv_sem=recv_sems.at[k - 1], device_id=peer, device_id_type=MESH)
            cp.start()
            cps.append((cp, peer))
        for k, (cp, (px, py, pc)) in enumerate(cps):
            pltpu.make_async_remote_copy(src_ref=v_ref, dst_ref=slots.at[4 * px + 2 * py + pc], send_sem=send_sems.at[k],
                                         recv_sem=recv_sems.at[k], device_id=(px, py, pc), device_id_type=MESH).wait_recv()
        for cp, _ in cps:
            cp.wait_send()
        tot = slots[0]
        for d in range(1, N_DEV):
            tot = tot + slots[d]
        o_ref[...] = tot

    vm = pl.BlockSpec(memory_space=pltpu.VMEM)
    return _pcall(
        body, name=name, in_specs=[vm], out_specs=vm, out_shape=jax.ShapeDtypeStruct(v.shape, F32),
        scratch_shapes=[pltpu.VMEM((N_DEV, rows, LANE), F32), pltpu.SemaphoreType.DMA((N_DEV - 1,)),
                        pltpu.SemaphoreType.DMA((N_DEV - 1,))],
        compiler_params=pltpu.CompilerParams(has_side_effects=True),
    )(v)


def _rows_block(rows, cols, itemsize=4, target_bytes=1 << 20):
    want = max(16, target_bytes // max(1, cols * itemsize))
    best = rows
    for t in range(16, rows, 16):
        if rows % t == 0 and t <= want:
            best = t
    return best if best <= want or rows <= want else rows


def _sum_slots(recvs, parts, me, *, name):
    nl = len(recvs)
    shp = recvs[0].shape[1:]
    r3 = [r.reshape(N_CHIPS, -1, shp[-1]) for r in recvs]
    p3 = [q.reshape(N_CHIPS, -1, shp[-1]) for q in parts]
    rows, cols = r3[0].shape[1:]
    tb = _rows_block(rows, cols)
    nblk = rows // tb
    per = N_CHIPS

    def body(me_ref, *refs):
        o_ref = refs[-1]
        lg = pl.program_id(0)
        for l in range(nl):
            r = refs[per * l:per * (l + 1)]

            @pl.when(lg == l)
            def _(r=r):
                tot = (r[0][...].astype(F32) + r[1][...].astype(F32)) + r[2][...].astype(F32)
                o_ref[...] = tot + r[3][...].astype(F32)

    def row(l, lg, i):
        return jnp.where(lg < l, 0, jnp.where(lg > l, nblk - 1, i))

    in_specs, args = [], []
    for l in range(nl):
        for k in range(1, N_CHIPS):
            in_specs.append(pl.BlockSpec((None, tb, cols), lambda lg, i, me_ref, l=l, k=k: (me_ref[0] ^ k, row(l, lg, i), 0)))
            args.append(r3[l])
        in_specs.append(pl.BlockSpec((None, tb, cols), lambda lg, i, me_ref, l=l: (me_ref[0], row(l, lg, i), 0)))
        args.append(p3[l])
    out = _pcall(
        body, name=name,
        grid_spec=pltpu.PrefetchScalarGridSpec(
            num_scalar_prefetch=1, grid=(nl, nblk), in_specs=in_specs,
            out_specs=pl.BlockSpec((None, tb, cols), lambda lg, i, me_ref: (lg, i, 0))),
        out_shape=jax.ShapeDtypeStruct((nl, rows, cols), F32), compiler_params=_cparams(("arbitrary", "arbitrary")),
    )(me, *args)
    return out.reshape((nl,) + shp)


def _adamw(w, g_parts, m, v, *, layer0=0, prev=None, name):
    shp = w.shape
    two = lambda a: a.reshape(-1, shp[-1])
    rows, cols = two(w).shape
    grows = two(g_parts[0]).shape[0]
    per_layer = rows // shp[0] if layer0 or prev is not None or grows != rows else rows
    tb = _rows_block(per_layer, cols, target_bytes=1 << 19)
    off = (layer0 * per_layer) // tb if per_layer != rows else 0
    npart = len(g_parts)
    nprev = 0 if prev is None else 4

    def body(*refs):
        w_ref = refs[0]
        gp = refs[1:1 + npart]
        m_ref, v_ref = refs[1 + npart:3 + npart]
        g_out, d_out, m_out, v_out = refs[3 + npart + nprev:]
        g = gp[0][...]
        for r in gp[1:]:
            g = g + r[...]
        wv = w_ref[...]
        m2 = ADAM_B1 * m_ref[...] + (1.0 - ADAM_B1) * g
        v2 = ADAM_B2 * v_ref[...] + (1.0 - ADAM_B2) * jnp.square(g)
        m_hat = m2 / (1.0 - ADAM_B1 ** ADAM_STEP)
        v_hat = v2 / (1.0 - ADAM_B2 ** ADAM_STEP)
        g_out[...] = g
        d_out[...] = -ADAM_LR * (m_hat / (jnp.sqrt(v_hat) + ADAM_EPS) + ADAM_WD * wv)
        m_out[...] = m2
        v_out[...] = v2

    wspec = pl.BlockSpec((tb, cols), lambda i: (i + off, 0))
    gspec = pl.BlockSpec((tb, cols), lambda i: (i, 0))
    in_specs = [wspec] + [gspec] * npart + [wspec, wspec] + [_ANY] * nprev
    args = [two(w)] + [two(p) for p in g_parts] + [two(m), two(v)] + ([two(a) for a in prev] if prev is not None else [])
    outs = _pcall(
        body, name=name, grid=(grows // tb,), in_specs=in_specs, out_specs=[wspec] * 4,
        out_shape=[jax.ShapeDtypeStruct((rows, cols), F32)] * 4,
        input_output_aliases={3 + npart + k: k for k in range(nprev)}, compiler_params=_cparams(("parallel",)),
    )(*args)
    return [o.reshape(shp) for o in outs]


BIG = ("w_in", "b_w_uq", "b_w_ukv", "m_w_kv", "w_branch", "w_out", "w_up", "w_down")
_SHARD_AXIS = dict(w_in=2, b_w_uq=2, b_w_ukv=2, m_w_kv=1, w_branch=3, w_out=1, w_up=2, w_down=1)
SMALL = ("g_mix", "b_gate", "a_qn", "a_kn", "a_sink", "b_qa_norm", "b_kva_norm", "b_qn", "b_kn", "c_qn", "c_kn",
         "m_g_mem", "m_qn", "m_kn", "g_mlp")
WEIGHTS = ("g_mix", "w_in", "b_gate", "a_qn", "a_kn", "a_sink", "b_qa_norm", "b_kva_norm", "b_w_uq", "b_w_ukv", "b_qn", "b_kn",
           "c_qn", "c_kn", "m_g_mem", "m_w_kv", "m_qn", "m_kn", "w_branch", "w_out", "g_mlp", "w_up", "w_down")


def _unshard(gathered, axis):
    moved = jnp.moveaxis(gathered, 0, axis)
    shp = list(gathered.shape[1:])
    shp[axis] *= N_CHIPS
    return moved.reshape(shp)


def _shard_parts(full, axis):
    shp = list(full.shape)
    shp[axis:axis + 1] = [N_CHIPS, shp[axis] // N_CHIPS]
    return jnp.moveaxis(full.reshape(shp), axis, 0)


def _pack_small(d):
    flat = jnp.concatenate([d[k].reshape(-1).astype(F32) for k in SMALL])
    n = flat.shape[0]
    pad = (-n) % (8 * LANE)
    return jnp.pad(flat, (0, pad)).reshape(-1, LANE)


def _unpack_small(packed, like):
    flat = packed.reshape(-1)
    out, off = {}, 0
    for k in SMALL:
        n = int(np.prod(like[k].shape))
        out[k] = flat[off:off + n].reshape(like[k].shape)
        off += n
    return out


def _train_step(x, mem, positions, loss_target, w, m, v):
    depth = w["g_mix"].shape[0]
    me = 2 * lax.axis_index("x") + lax.axis_index("y")
    landing = lambda a: lax.empty((N_CHIPS,) + a.shape, a.dtype)
    def with_own(land, mine):
        slot = lax.broadcasted_iota(jnp.int32, (N_CHIPS,) + (1,) * mine.ndim, 0)
        return jnp.where(slot == me, mine[None], land)

    gathers = {}
    for l in range(depth):
        for names in ((BIG[:1], BIG[1:]) if l == 0 else (BIG,)):
            own = [w[k][l].astype(BF16) for k in names]
            tag = f"{l}" if len(names) == len(BIG) else f"{l}_{names[0]}"
            gathers[(l, names)] = _exchange_start(own, [landing(a) for a in own], gather=True, name=f"gather_start{tag}")

    def gathered(l, names, after):
        tag = f"{l}" if len(names) == len(BIG) else f"{l}_{names[0]}"
        mine, lands = _exchange_wait(gathers[(l, names)], after, gather=True, name=f"gather_wait{tag}")
        full = {k: _unshard(with_own(g, o), _SHARD_AXIS[k] - 1) for k, g, o in zip(names, lands, mine)}
        return {kk: vv for kk, vv in _big_to_kernel_layout(full).items()}

    def get_big(l, after):
        if l == 0:
            return gathered(0, BIG[:1], after), lambda later: gathered(0, BIG[1:], later)
        return gathered(l, BIG, after), None

    scatters = {}

    def scatter(l, g, names, tag):
        gref = _big_grads_to_reference_layout(g, names)
        parts = [_shard_parts(gref[k], _SHARD_AXIS[k] - 1).astype(BF16) for k in names]
        scatters[(l, names)] = _exchange_start(parts, [landing(p[0]) for p in parts], gather=False, name=f"scatter_start{tag}")
        return scatters[(l, names)][4]

    last_names = (BIG[1:], BIG[:1]) if depth > 1 else (BIG,)

    def early_grads(g):
        if len(last_names) > 1:
            scatter(0, g, last_names[0], "0_rest")

    def put_grads(l, g):
        names = last_names[-1] if l == 0 else BIG
        return scatter(l, g, names, f"{l}")[:1, :1]

    small = {k: w[k] for k in SMALL}
    loss, gx, grads = _local_step(x[0], mem[0], positions[0], small, loss_target[0], get_big, put_grads, early_grads)
    loss = lax.psum(loss, ("x", "y", "c"))
    me1 = me.reshape(1).astype(jnp.int32)
    res = {k: None for k in BIG}
    after = scatters[(0, last_names[-1])][4]
    for lo, n in ([(1, depth - 1), (0, 1)] if depth > 1 else [(0, 1)]):
        parts, recv = [], []
        for l in range(lo, lo + n):
            pl_, rl_ = {}, {}
            for names in ((BIG,) if l > 0 else last_names):
                tag = f"{l}" if (l > 0 or names == last_names[-1]) else "0_rest"
                ps_, rs_ = _exchange_wait(scatters[(l, names)], after, gather=False, name=f"scatter_wait{tag}")
                pl_.update(zip(names, ps_))
                rl_.update(zip(names, rs_))
            parts.append([pl_[k] for k in BIG])
            recv.append([rl_[k] for k in BIG])
        mine = [_sum_slots([recv[l][a] for l in range(n)], [parts[l][a] for l in range(n)], me1, name=f"sum_{k}_{lo}")
                for a, k in enumerate(BIG)]
        theirs = _sibling_exchange(mine, name=f"sibling_grads{lo}")
        for k, p, q in zip(BIG, mine, theirs):
            res[k] = _adamw(w[k], [p, q], m[k], v[k], layer0=lo, prev=res[k], name=f"adamw_{k}_{lo}")
        after = res[BIG[-1]][0]
    gsmall = _small_grads_to_reference_layout(grads)
    g_small = _allreduce_small(_pack_small(gsmall), name="allreduce_small")
    packed = _adamw(_pack_small(small), [g_small], _pack_small({k: m[k] for k in SMALL}), _pack_small({k: v[k] for k in SMALL}),
                    name="adamw_small")
    unpacked = [_unpack_small(p, small) for p in packed]
    for k in SMALL:
        res[k] = [u[k] for u in unpacked]
    outs = [loss, gx[None]]
    for i in range(4):
        outs += [res[k][i] for k in WEIGHTS]
    return tuple(outs)

def kernel(x, mem, positions, g_mix, w_in, b_gate, a_qn, a_kn, a_sink, b_qa_norm, b_kva_norm, b_w_uq, b_w_ukv, b_qn, b_kn, c_qn, c_kn, m_g_mem, m_w_kv, m_qn, m_kn, w_branch, w_out, g_mlp, w_up, w_down, loss_target, m_g_mix, m_w_in, m_b_gate, m_a_qn, m_a_kn, m_a_sink, m_b_qa_norm, m_b_kva_norm, m_b_w_uq, m_b_w_ukv, m_b_qn, m_b_kn, m_c_qn, m_c_kn, m_m_g_mem, m_m_w_kv, m_m_qn, m_m_kn, m_w_branch, m_w_out, m_g_mlp, m_w_up, m_w_down, v_g_mix, v_w_in, v_b_gate, v_a_qn, v_a_kn, v_a_sink, v_b_qa_norm, v_b_kva_norm, v_b_w_uq, v_b_w_ukv, v_b_qn, v_b_kn, v_c_qn, v_c_kn, v_m_g_mem, v_m_w_kv, v_m_qn, v_m_kn, v_w_branch, v_w_out, v_g_mlp, v_w_up, v_w_down):
    args = dict(locals())
    w = {k: args[k] for k in WEIGHTS}
    m = {k: args["m_" + k] for k in WEIGHTS}
    v = {k: args["v_" + k] for k in WEIGHTS}
    return _train_step(x, mem, positions, loss_target, w, m, v)
```

```python
import functools
import math

import jax
import jax.numpy as jnp
import numpy as np
from jax import lax
from jax.experimental import pallas as pl
from jax.experimental.pallas import tpu as pltpu

F32 = jnp.float32
BF16 = jnp.bfloat16

DEPTH = 4
BLOCK = 128
ROPE_THETA = 10000.0
EPS = 1e-6
NEG = -1e30
A_HEADS, A_KV_HEADS, A_HD, A_WINDOW = 8, 2, 64, 128
B_HEADS, B_Q_LORA, B_KV_LORA, B_NOPE, B_ROPE, B_V = 8, 384, 256, 64, 32, 64
C_PATTERNS = ((128, 1), (512, 4), (2048, 16))
C_HEADS, C_HD = 8, 64
M_HEADS, M_HD = 4, 128
N_BRANCH, BRANCH_W = 4, 512
ADAM_LR, ADAM_B1, ADAM_B2, ADAM_EPS, ADAM_WD, ADAM_STEP = 0.001, 0.9, 0.999, 1e-08, 0.01, 10

LANE = 128
VMEM_LIMIT = 56 * 1024 * 1024


def _pcall(body, **kw):
    return pl.pallas_call(body, **kw)


def _cparams(sem):
    return pltpu.CompilerParams(dimension_semantics=sem, vmem_limit_bytes=VMEM_LIMIT)


def _tile(n, target):
    if n <= target:
        return n
    best = None
    for t in range(LANE, target + 1, LANE):
        if n % t == 0:
            best = t
    return best if best is not None else n


def _mm(a, b, *, ta=False, tb=False, out_dtype=F32, pro_a=None, epi=None, extra=None, name,
        tm=1024, tn=1024, tk=1024):
    if ta:
        K, M = a.shape
    else:
        M, K = a.shape
    if tb:
        N, K2 = b.shape
    else:
        K2, N = b.shape
    assert K == K2, (a.shape, b.shape, ta, tb)
    tm, tn, tk = _tile(M, tm), _tile(N, tn), _tile(K, tk)
    nk = K // tk
    a_spec = pl.BlockSpec((tk, tm), lambda i, j, k: (k, i)) if ta else pl.BlockSpec((tm, tk), lambda i, j, k: (i, k))
    b_spec = pl.BlockSpec((tn, tk), lambda i, j, k: (j, k)) if tb else pl.BlockSpec((tk, tn), lambda i, j, k: (k, j))
    o_spec = pl.BlockSpec((tm, tn), lambda i, j, k: (i, j))
    dims = (((0,) if ta else (1,), (1,) if tb else (0,)), ((), ()))
    has_extra = extra is not None

    def body(*refs):
        if has_extra:
            a_ref, b_ref, e_ref, o_ref, acc_ref = refs
        else:
            a_ref, b_ref, o_ref, acc_ref = refs
            e_ref = None
        k = pl.program_id(2)
        av = a_ref[...]
        if pro_a is not None:
            av = pro_a(av.astype(F32))
        part = lax.dot_general(av.astype(BF16), b_ref[...].astype(BF16), dims, preferred_element_type=F32)

        @pl.when(k == 0)
        def _():
            acc_ref[...] = part

        @pl.when(k > 0)
        def _():
            acc_ref[...] += part

        @pl.when(k == nk - 1)
        def _():
            r = acc_ref[...]
            if epi is not None:
                r = epi(r, e_ref[...]) if has_extra else epi(r)
            o_ref[...] = r.astype(out_dtype)

    in_specs = [a_spec, b_spec] + ([o_spec] if has_extra else [])
    args = (a, b) + ((extra,) if has_extra else ())
    return _pcall(
        body, name=name, grid=(M // tm, N // tn, nk), in_specs=in_specs, out_specs=o_spec,
        out_shape=jax.ShapeDtypeStruct((M, N), out_dtype),
        scratch_shapes=[pltpu.VMEM((tm, tn), F32)],
        compiler_params=_cparams(("parallel", "parallel", "arbitrary")),
    )(*args)


def _piece(p):
    if isinstance(p, tuple):
        return p
    return (p, p.shape[1], 0)


def _row_spec(width, idx, tb):
    return pl.BlockSpec((tb, width), lambda i, idx=idx: (i, idx))


def _full_spec(arr):
    nd = arr.ndim
    return pl.BlockSpec(arr.shape, lambda i, nd=nd: (0,) * nd)


def _rowmap(f, rows, params, outs, *, tb, name):
    rows = [_piece(p) for p in rows]
    R = rows[0][0].shape[0]
    tb = min(tb, R)
    nr, npar, nout = len(rows), len(params), len(outs)

    def body(*refs):
        rv = [r[...] for r in refs[:nr]]
        pv = [r[...] for r in refs[nr:nr + npar]]
        res = f(*rv, *pv)
        for o_ref, val in zip(refs[nr + npar:], res):
            o_ref[...] = val.astype(o_ref.dtype)

    return _pcall(
        body, name=name, grid=(R // tb,),
        in_specs=[_row_spec(w, idx, tb) for (_, w, idx) in rows] + [_full_spec(p) for p in params],
        out_specs=[_row_spec(w, 0, tb) for (w, _) in outs],
        out_shape=[jax.ShapeDtypeStruct((R, w), dt) for (w, dt) in outs],
        compiler_params=_cparams(("parallel",)),
    )(*[r[0] for r in rows], *params)


def _rowmap_bwd(f, rows, params, couts, *, diff, out_dtypes, adds=None, tb, name):
    rows = [_piece(p) for p in rows]
    couts = [_piece(p) for p in couts]
    R = rows[0][0].shape[0]
    tb = min(tb, R)
    nr, npar, nc = len(rows), len(params), len(couts)
    didx = [i for i, d in enumerate(diff) if d]
    adds = [None] * len(didx) if adds is None else adds
    add_ops = [_piece(a) for a in adds if a is not None]
    na = len(add_ops)

    def body(*refs):
        rv = [r[...] for r in refs[:nr]]
        pv = [r[...] for r in refs[nr:nr + npar]]
        cv = [r[...] for r in refs[nr + npar:nr + npar + nc]]
        av = [r[...] for r in refs[nr + npar + nc:nr + npar + nc + na]]
        o_refs = refs[nr + npar + nc + na:]
        drow_refs, dpar_refs = o_refs[:len(didx)], o_refs[len(didx):]
        nondiff = {i: rv[i] for i in range(nr) if not diff[i]}

        def g(*dv):
            full = []
            it = iter(dv[:len(didx)])
            for i in range(nr):
                full.append(nondiff[i] if i in nondiff else next(it))
            return f(*full, *dv[len(didx):])

        res, vjp = jax.vjp(g, *[rv[i].astype(F32) for i in didx], *pv)
        cts = tuple(c.astype(r.dtype) for c, r in zip(cv, res))
        grads = vjp(cts)
        ai = 0
        for n, o_ref in enumerate(drow_refs):
            val = grads[n]
            if adds[n] is not None:
                val = val + av[ai].astype(F32)
                ai += 1
            o_ref[...] = val.astype(o_ref.dtype)
        first = pl.program_id(0) == 0
        for n, o_ref in enumerate(dpar_refs):
            gp = grads[len(didx) + n].astype(F32)

            @pl.when(first)
            def _(o_ref=o_ref, gp=gp):
                o_ref[...] = gp

            @pl.when(jnp.logical_not(first))
            def _(o_ref=o_ref, gp=gp):
                o_ref[...] += gp

    outs = _pcall(
        body, name=name, grid=(R // tb,),
        in_specs=([_row_spec(w, idx, tb) for (_, w, idx) in rows] + [_full_spec(p) for p in params]
                  + [_row_spec(w, idx, tb) for (_, w, idx) in couts] + [_row_spec(w, idx, tb) for (_, w, idx) in add_ops]),
        out_specs=[_row_spec(rows[i][1], 0, tb) for i in didx] + [_full_spec(p) for p in params],
        out_shape=([jax.ShapeDtypeStruct((R, rows[i][1]), dt) for i, dt in zip(didx, out_dtypes)]
                   + [jax.ShapeDtypeStruct(p.shape, F32) for p in params]),
        compiler_params=_cparams(("arbitrary",)),
    )(*[r[0] for r in rows], *params, *[c[0] for c in couts], *[a[0] for a in add_ops])
    return outs[:len(didx)], outs[len(didx):]


@functools.partial(jax.custom_vjp, nondiff_argnums=(1,))
def _lane_roll(x, shift):
    return pltpu.roll(x, shift % x.shape[-1], axis=x.ndim - 1)


def _lane_roll_fwd(x, shift):
    return _lane_roll(x, shift), None


def _lane_roll_bwd(shift, _, g):
    return (_lane_roll(g, -shift),)


_lane_roll.defvjp(_lane_roll_fwd, _lane_roll_bwd)


def _group_matrix(kind):
    r = lax.broadcasted_iota(jnp.int32, (LANE, LANE), 0)
    c = lax.broadcasted_iota(jnp.int32, (LANE, LANE), 1)
    if kind == "mla":
        gid = lambda l: jnp.where(l < B_NOPE, 0, jnp.where(l < B_NOPE + B_ROPE, 1, 2))
        inv = jnp.where(c < B_NOPE, 1.0 / B_NOPE, 1.0 / B_ROPE)
        return jnp.where(gid(r) == gid(c), inv, 0.0).astype(BF16)
    return jnp.where(r // kind == c // kind, 1.0 / kind, 0.0).astype(BF16)


@functools.partial(jax.custom_vjp, nondiff_argnums=(1,))
def _group_mean(xx, kind):
    gm = _group_matrix(kind)
    outs = []
    for b in range(xx.shape[-1] // LANE):
        t = xx[:, b * LANE:(b + 1) * LANE]
        hi = t.astype(BF16)
        lo = (t - hi.astype(F32)).astype(BF16)
        outs.append(jnp.dot(hi, gm, preferred_element_type=F32) + jnp.dot(lo, gm, preferred_element_type=F32))
    return jnp.concatenate(outs, axis=1) if len(outs) > 1 else outs[0]


def _group_mean_fwd(xx, kind):
    return _group_mean(xx, kind), None


def _group_mean_bwd(kind, _, g):
    return (_group_mean(g, kind),)


_group_mean.defvjp(_group_mean_fwd, _group_mean_bwd)


def _exact_dot(x, m):
    return jnp.dot(x, m, precision=lax.Precision.HIGHEST, preferred_element_type=F32)


def _head_norm(x, gain_tiled, group):
    return x * lax.rsqrt(_group_mean(x * x, group) + EPS) * gain_tiled


def _row_norm(x, gain):
    ms = jnp.mean(x * x, axis=-1, keepdims=True)
    return x * lax.rsqrt(ms + EPS) * gain


def _rope(x, cos, sin, hd):
    half = hd // 2
    lane = lax.broadcasted_iota(jnp.int32, x.shape, x.ndim - 1) % hd
    other = jnp.where(lane < half, -_lane_roll(x, -half), _lane_roll(x, half))
    return x * cos + other * sin


class _AttnGeom:
    def __init__(self, mode, lq, lk, max_dist=0):
        self.mode, self.lq, self.lk, self.max_dist = mode, lq, lk, max_dist
        if mode == "band":
            self.bq = self.bk = BLOCK
            self.nt_q = 2
            self.nt_k = 2
        elif mode == "causal":
            self.bq = self.bk = min(256, lq)
            self.nt_q = lk // self.bk
            self.nt_k = lq // self.bq
        else:
            self.bq = min(512, lq)
            self.bk = lk
            self.nt_q = 1
            self.nt_k = lq // self.bq
        self.nq, self.nk = lq // self.bq, lk // self.bk

    def kv_block(self, i, t):
        if self.mode == "band":
            return jnp.maximum(i - t, 0)
        if self.mode == "causal":
            return jnp.minimum(t, i)
        return 0 * i

    def kv_active(self, i, t):
        if self.mode == "band":
            return i - t >= 0
        if self.mode == "causal":
            return t <= i
        return None

    def q_block(self, j, t):
        if self.mode == "band":
            return jnp.minimum(j + t, self.nq - 1)
        if self.mode == "causal":
            return jnp.maximum(t, j)
        return t

    def q_active(self, j, t):
        if self.mode == "band":
            return j + t <= self.nq - 1
        if self.mode == "causal":
            return t >= j
        return None

    def mask(self, qb, kb):
        if self.mode == "full":
            return None
        qp = qb * self.bq + lax.broadcasted_iota(jnp.int32, (self.bq, self.bk), 0)
        kp = kb * self.bk + lax.broadcasted_iota(jnp.int32, (self.bq, self.bk), 1)
        d = qp - kp
        if self.mode == "band":
            return (d >= 0) & (d <= self.max_dist)
        return d >= 0


def _when(cond, fn):
    if cond is None:
        fn()
    else:
        pl.when(cond)(fn)


def _dil_view(p, dil):
    arr, w, idx = _piece(p)
    R, C = arr.shape
    assert C % w == 0, (C, w)
    return arr.reshape(R // dil, dil * C), w, idx, C // w


def _seq_spec(view, rows, blk_fn):
    _, w, idx, cpw = view
    return pl.BlockSpec((rows, w), lambda s, i, t: (blk_fn(i, t), s * cpw + idx))


_NT = (((1,), (1,)), ((), ()))
_TN = (((0,), (0,)), ((), ()))


def _scores(geom, scale, q, k, qp, kp, h, g, hd, rope, qb, kb):
    s = lax.dot_general(q[:, h * hd:(h + 1) * hd], k[:, g * hd:(g + 1) * hd], _NT, preferred_element_type=F32)
    if rope:
        s = s + lax.dot_general(qp[:, h * rope:(h + 1) * rope], kp[:, :rope], _NT, preferred_element_type=F32)
    s = s * scale
    m = geom.mask(qb, kb)
    return s, m


def _attn_fwd(geom, q, k, v, *, hq, hk, hd, hdv, scale, dil=1, qp=None, kp=None, rope=0, name):
    qv, kv, vv = _dil_view(q, dil), _dil_view(k, dil), _dil_view(v, dil)
    R = _piece(q)[0].shape[0]
    grp = hq // hk
    bq, bk, nt = geom.bq, geom.bk, geom.nt_q
    ops = [qv, kv, vv]
    specs = [_seq_spec(qv, bq, lambda i, t: i), _seq_spec(kv, bk, geom.kv_block), _seq_spec(vv, bk, geom.kv_block)]
    if rope:
        qpv, kpv = _dil_view(qp, dil), _dil_view(kp, dil)
        ops += [qpv, kpv]
        specs += [_seq_spec(qpv, bq, lambda i, t: i), _seq_spec(kpv, bk, geom.kv_block)]
    ow = hq * hdv
    o_view = (None, ow, 0, 1)
    o_spec = pl.BlockSpec((bq, ow), lambda s, i, t: (i, s))

    def body(*refs):
        if rope:
            q_ref, k_ref, v_ref, qp_ref, kp_ref, o_ref, lse_ref, m_sc, l_sc, acc_sc = refs
        else:
            q_ref, k_ref, v_ref, o_ref, lse_ref, m_sc, l_sc, acc_sc = refs
            qp_ref = kp_ref = None
        i, t = pl.program_id(1), pl.program_id(2)

        @pl.when(t == 0)
        def _():
            m_sc[...] = jnp.full(m_sc.shape, NEG, F32)
            l_sc[...] = jnp.zeros(l_sc.shape, F32)
            acc_sc[...] = jnp.zeros(acc_sc.shape, F32)

        def step():
            qa, ka, va = q_ref[...].astype(BF16), k_ref[...].astype(BF16), v_ref[...].astype(BF16)
            qpa = qp_ref[...].astype(BF16) if rope else None
            kpa = kp_ref[...].astype(BF16) if rope else None
            kb = geom.kv_block(i, t)
            for h in range(hq):
                g = h // grp
                s, msk = _scores(geom, scale, qa, ka, qpa, kpa, h, g, hd, rope, i, kb)
                if msk is not None:
                    s = jnp.where(msk, s, NEG)
                m_old = m_sc[h]
                m_new = jnp.maximum(m_old, jnp.max(s, axis=1, keepdims=True))
                p = jnp.exp(s - m_new)
                alpha = jnp.exp(m_old - m_new)
                l_sc[h] = alpha * l_sc[h] + jnp.sum(p, axis=1, keepdims=True)
                pv = jnp.dot(p.astype(BF16), va[:, g * hdv:(g + 1) * hdv], preferred_element_type=F32)
                acc_sc[h] = alpha * acc_sc[h] + pv
                m_sc[h] = m_new

        _when(geom.kv_active(i, t), step)

        @pl.when(t == nt - 1)
        def _():
            for h in range(hq):
                l = l_sc[h]
                o_ref[:, h * hdv:(h + 1) * hdv] = acc_sc[h] / l
                lse_ref[:, h * hdv:(h + 1) * hdv] = jnp.broadcast_to(m_sc[h] + jnp.log(l), (bq, hdv))

    o, lse = _pcall(
        body, name=name, grid=(dil, geom.nq, nt), in_specs=specs, out_specs=[o_spec, o_spec],
        out_shape=[jax.ShapeDtypeStruct((R // dil, dil * ow), F32)] * 2,
        scratch_shapes=[pltpu.VMEM((hq, bq, 1), F32), pltpu.VMEM((hq, bq, 1), F32), pltpu.VMEM((hq, bq, hdv), F32)],
        compiler_params=_cparams(("parallel", "parallel", "arbitrary")),
    )(*[o_[0] for o_ in ops])
    return o.reshape(R, ow), lse.reshape(R, ow)


def _attn_dq(geom, q, k, v, do, o, lse, dlse, *, hq, hk, hd, hdv, scale, dil=1, qp=None, kp=None, rope=0,
             out_dtype=F32, name):
    qv, kv, vv = _dil_view(q, dil), _dil_view(k, dil), _dil_view(v, dil)
    dov, ov, lv = _dil_view(do, dil), _dil_view(o, dil), _dil_view(lse, dil)
    R = _piece(q)[0].shape[0]
    grp = hq // hk
    bq, bk, nt = geom.bq, geom.bk, geom.nt_q
    qi = lambda i, t: i
    ops = [qv, kv, vv, dov, ov, lv]
    specs = [_seq_spec(qv, bq, qi), _seq_spec(kv, bk, geom.kv_block), _seq_spec(vv, bk, geom.kv_block),
             _seq_spec(dov, bq, qi), _seq_spec(ov, bq, qi), _seq_spec(lv, bq, qi)]
    has_dl = dlse is not None
    if has_dl:
        dlv = _dil_view(dlse, dil)
        ops.append(dlv)
        specs.append(_seq_spec(dlv, bq, qi))
    if rope:
        qpv, kpv = _dil_view(qp, dil), _dil_view(kp, dil)
        ops += [qpv, kpv]
        specs += [_seq_spec(qpv, bq, qi), _seq_spec(kpv, bk, geom.kv_block)]
    qw = hq * hd
    out_specs = [pl.BlockSpec((bq, qw), lambda s, i, t: (i, s))]
    out_shape = [jax.ShapeDtypeStruct((R // dil, dil * qw), out_dtype)]
    scratch = [pltpu.VMEM((hq, bq, 1), F32), pltpu.VMEM((hq, bq, hd), F32)]
    if rope:
        out_specs.append(pl.BlockSpec((bq, hq * rope), lambda s, i, t: (i, s)))
        out_shape.append(jax.ShapeDtypeStruct((R // dil, dil * hq * rope), out_dtype))
        scratch.append(pltpu.VMEM((hq, bq, rope), F32))

    def body(*refs):
        refs = list(refs)
        q_ref, k_ref, v_ref, do_ref, o_ref, l_ref = refs[:6]
        pos = 6
        dl_ref = None
        if has_dl:
            dl_ref = refs[pos]
            pos += 1
        qp_ref = kp_ref = None
        if rope:
            qp_ref, kp_ref = refs[pos:pos + 2]
            pos += 2
        dq_ref = refs[pos]
        pos += 1
        dqp_ref = None
        if rope:
            dqp_ref = refs[pos]
            pos += 1
        dl_sc, dq_sc = refs[pos:pos + 2]
        dqp_sc = refs[pos + 2] if rope else None
        i, t = pl.program_id(1), pl.program_id(2)

        @pl.when(t == 0)
        def _():
            dov_, ov_ = do_ref[...].astype(F32), o_ref[...].astype(F32)
            prod = dov_ * ov_
            for h in range(hq):
                d = jnp.sum(prod[:, h * hdv:(h + 1) * hdv], axis=1, keepdims=True)
                if has_dl:
                    d = d - jnp.sum(dl_ref[:, h * hdv:(h + 1) * hdv].astype(F32), axis=1, keepdims=True)
                dl_sc[h] = d
            dq_sc[...] = jnp.zeros(dq_sc.shape, F32)
            if rope:
                dqp_sc[...] = jnp.zeros(dqp_sc.shape, F32)

        def step():
            qa, ka, va = q_ref[...].astype(BF16), k_ref[...].astype(BF16), v_ref[...].astype(BF16)
            doa = do_ref[...].astype(BF16)
            qpa = qp_ref[...].astype(BF16) if rope else None
            kpa = kp_ref[...].astype(BF16) if rope else None
            kb = geom.kv_block(i, t)
            for h in range(hq):
                g = h // grp
                s, msk = _scores(geom, scale, qa, ka, qpa, kpa, h, g, hd, rope, i, kb)
                p = jnp.exp(s - l_ref[:, h * hdv:h * hdv + 1])
                if msk is not None:
                    p = jnp.where(msk, p, 0.0)
                dp = lax.dot_general(doa[:, h * hdv:(h + 1) * hdv], va[:, g * hdv:(g + 1) * hdv], _NT,
                                     preferred_element_type=F32)
                ds = (p * (dp - dl_sc[h]) * scale).astype(BF16)
                dq_sc[h] += jnp.dot(ds, ka[:, g * hd:(g + 1) * hd], preferred_element_type=F32)
                if rope:
                    dqp_sc[h] += jnp.dot(ds, kpa[:, :rope], preferred_element_type=F32)

        _when(geom.kv_active(i, t), step)

        @pl.when(t == nt - 1)
        def _():
            for h in range(hq):
                dq_ref[:, h * hd:(h + 1) * hd] = dq_sc[h].astype(dq_ref.dtype)
                if rope:
                    dqp_ref[:, h * rope:(h + 1) * rope] = dqp_sc[h].astype(dqp_ref.dtype)

    outs = _pcall(
        body, name=name, grid=(dil, geom.nq, nt), in_specs=specs, out_specs=out_specs, out_shape=out_shape,
        scratch_shapes=scratch, compiler_params=_cparams(("parallel", "parallel", "arbitrary")),
    )(*[o_[0] for o_ in ops])
    dq = outs[0].reshape(R, qw)
    if rope:
        return dq, outs[1].reshape(R, hq * rope)
    return dq


def _attn_dkv(geom, q, k, v, do, o, lse, dlse, *, hq, hk, hd, hdv, scale, dil=1, qp=None, kp=None, rope=0,
              out_dtype=F32, name):
    qv, kv, vv = _dil_view(q, dil), _dil_view(k, dil), _dil_view(v, dil)
    dov, ov, lv = _dil_view(do, dil), _dil_view(o, dil), _dil_view(lse, dil)
    Rk = _piece(k)[0].shape[0]
    grp = hq // hk
    bq, bk, nt = geom.bq, geom.bk, geom.nt_k
    kj = lambda j, t: j
    ops = [qv, kv, vv, dov, ov, lv]
    specs = [_seq_spec(qv, bq, geom.q_block), _seq_spec(kv, bk, kj), _seq_spec(vv, bk, kj),
             _seq_spec(dov, bq, geom.q_block), _seq_spec(ov, bq, geom.q_block), _seq_spec(lv, bq, geom.q_block)]
    has_dl = dlse is not None
    if has_dl:
        dlv = _dil_view(dlse, dil)
        ops.append(dlv)
        specs.append(_seq_spec(dlv, bq, geom.q_block))
    if rope:
        qpv, kpv = _dil_view(qp, dil), _dil_view(kp, dil)
        ops += [qpv, kpv]
        specs += [_seq_spec(qpv, bq, geom.q_block), _seq_spec(kpv, bk, kj)]
    kw, vw = hk * hd, hk * hdv
    out_specs = [pl.BlockSpec((bk, kw), lambda s, j, t: (j, s)), pl.BlockSpec((bk, vw), lambda s, j, t: (j, s))]
    out_shape = [jax.ShapeDtypeStruct((Rk // dil, dil * kw), out_dtype), jax.ShapeDtypeStruct((Rk // dil, dil * vw), out_dtype)]
    scratch = [pltpu.VMEM((hk, bk, hd), F32), pltpu.VMEM((hk, bk, hdv), F32)]
    if rope:
        out_specs.append(pl.BlockSpec((bk, LANE), lambda s, j, t: (j, s)))
        out_shape.append(jax.ShapeDtypeStruct((Rk // dil, dil * LANE), out_dtype))
        scratch.append(pltpu.VMEM((bk, rope), F32))

    def body(*refs):
        refs = list(refs)
        q_ref, k_ref, v_ref, do_ref, o_ref, l_ref = refs[:6]
        pos = 6
        dl_ref = None
        if has_dl:
            dl_ref = refs[pos]
            pos += 1
        qp_ref = kp_ref = None
        if rope:
            qp_ref, kp_ref = refs[pos:pos + 2]
            pos += 2
        dk_ref, dv_ref = refs[pos:pos + 2]
        pos += 2
        dkp_ref = None
        if rope:
            dkp_ref = refs[pos]
            pos += 1
        dk_sc, dv_sc = refs[pos:pos + 2]
        dkp_sc = refs[pos + 2] if rope else None
        j, t = pl.program_id(1), pl.program_id(2)

        @pl.when(t == 0)
        def _():
            dk_sc[...] = jnp.zeros(dk_sc.shape, F32)
            dv_sc[...] = jnp.zeros(dv_sc.shape, F32)
            if rope:
                dkp_sc[...] = jnp.zeros(dkp_sc.shape, F32)

        def step():
            qa, ka, va = q_ref[...].astype(BF16), k_ref[...].astype(BF16), v_ref[...].astype(BF16)
            dof = do_ref[...].astype(F32)
            doa = dof.astype(BF16)
            prod = dof * o_ref[...].astype(F32)
            qpa = qp_ref[...].astype(BF16) if rope else None
            kpa = kp_ref[...].astype(BF16) if rope else None
            qb = geom.q_block(j, t)
            for h in range(hq):
                g = h // grp
                s, msk = _scores(geom, scale, qa, ka, qpa, kpa, h, g, hd, rope, qb, j)
                p = jnp.exp(s - l_ref[:, h * hdv:h * hdv + 1])
                if msk is not None:
                    p = jnp.where(msk, p, 0.0)
                delta = jnp.sum(prod[:, h * hdv:(h + 1) * hdv], axis=1, keepdims=True)
                if has_dl:
                    delta = delta - jnp.sum(dl_ref[:, h * hdv:(h + 1) * hdv].astype(F32), axis=1, keepdims=True)
                do_h = doa[:, h * hdv:(h + 1) * hdv]
                dv_sc[g] += lax.dot_general(p.astype(BF16), do_h, _TN, preferred_element_type=F32)
                dp = lax.dot_general(do_h, va[:, g * hdv:(g + 1) * hdv], _NT, preferred_element_type=F32)
                ds = (p * (dp - delta) * scale).astype(BF16)
                dk_sc[g] += lax.dot_general(ds, qa[:, h * hd:(h + 1) * hd], _TN, preferred_element_type=F32)
                if rope:
                    dkp_sc[...] += lax.dot_general(ds, qpa[:, h * rope:(h + 1) * rope], _TN, preferred_element_type=F32)

        _when(geom.q_active(j, t), step)

        @pl.when(t == nt - 1)
        def _():
            for g in range(hk):
                dk_ref[:, g * hd:(g + 1) * hd] = dk_sc[g].astype(dk_ref.dtype)
                dv_ref[:, g * hdv:(g + 1) * hdv] = dv_sc[g].astype(dv_ref.dtype)
            if rope:
                dkp_ref[...] = jnp.zeros(dkp_ref.shape, dkp_ref.dtype)
                dkp_ref[:, :rope] = dkp_sc[...].astype(dkp_ref.dtype)

    outs = _pcall(
        body, name=name, grid=(dil, geom.nk, nt), in_specs=specs, out_specs=out_specs, out_shape=out_shape,
        scratch_shapes=scratch, compiler_params=_cparams(("parallel", "parallel", "arbitrary")),
    )(*[o_[0] for o_ in ops])
    dk, dv = outs[0].reshape(Rk, kw), outs[1].reshape(Rk, vw)
    if rope:
        return dk, dv, outs[2].reshape(Rk, LANE)
    return dk, dv


class _Viewed(tuple):
    pass


class _BandPlan:
    def __init__(self, S, dil, max_dist):
        self.L, self.dil, self.max_dist = S // dil, dil, max_dist
        self.nblk = self.L // BLOCK
        self.nb = min(4, self.nblk)
        self.ns = min(dil, max(1, 4 // self.nb))
        self.grid = (dil // self.ns, self.nblk // self.nb)
        self.rows = self.nb * BLOCK

    def view(self, p):
        if isinstance(p, _Viewed):
            return p
        arr, w, idx = _piece(p)
        R, C = arr.shape
        assert C % w == 0 and (self.ns == 1 or (C == w and idx == 0)), (C, w, idx, self.ns)
        return _Viewed((arr.reshape(R // self.dil, self.dil * C), w, idx, C // w))

    def main(self, view):
        _, w, idx, cpw = view
        if self.ns == 1:
            return pl.BlockSpec((self.rows, w), lambda s, i: (i, s * cpw + idx))
        return pl.BlockSpec((self.rows, self.ns * w), lambda s, i: (i, s))

    def edge(self, view, nxt):
        _, w, idx, cpw = view
        nb, last = self.nb, self.nblk - 1
        rb = (lambda i: jnp.minimum((i + 1) * nb, last)) if nxt else (lambda i: jnp.maximum(i * nb - 1, 0))
        if self.ns == 1:
            return pl.BlockSpec((BLOCK, w), lambda s, i: (rb(i), s * cpw + idx))
        return pl.BlockSpec((BLOCK, self.ns * w), lambda s, i: (rb(i), s))

    def out(self, w):
        return pl.BlockSpec((self.rows, self.ns * w), lambda s, i: (i, s))

    def masks(self):
        qi = lax.broadcasted_iota(jnp.int32, (BLOCK, BLOCK), 0)
        kj = lax.broadcasted_iota(jnp.int32, (BLOCK, BLOCK), 1)
        return kj <= qi, (qi - kj + BLOCK) <= self.max_dist


def _half(e, rows=BLOCK):
    lane = lax.broadcasted_iota(jnp.int32, (rows, LANE), 1)
    return (lane < LANE // 2) if e == 0 else (lane >= LANE // 2)


def _swap_halves(t):
    return pltpu.roll(t, LANE // 2, axis=1)


def _kv_group(ref, rows, col0, pr, grp, hq, hk):
    if hk == hq:
        return ref[rows, col0 + pr * LANE:col0 + (pr + 1) * LANE].astype(BF16)
    g = (2 * pr) // grp
    t = ref[rows, col0 + (g // 2) * LANE:col0 + (g // 2 + 1) * LANE].astype(BF16)
    sw = _swap_halves(t)
    h0 = _half(0, t.shape[0])
    return jnp.where(h0, t, sw) if g % 2 == 0 else jnp.where(h0, sw, t)


def _stack_heads(t2):
    z = jnp.zeros_like(t2)
    h0 = _half(0, t2.shape[0])
    return jnp.concatenate([jnp.where(h0, t2, z), jnp.where(h0, z, t2)], axis=0)


def _unstack_heads(t, rows=BLOCK):
    return jnp.where(_half(0, rows), t[:rows], t[rows:])


def _per_head_col(t2):
    return jnp.concatenate([t2[:, :1], t2[:, LANE // 2:LANE // 2 + 1]], axis=0)


def _per_head_sum(t2):
    h0 = _half(0, t2.shape[0])
    return jnp.concatenate([jnp.sum(jnp.where(h0, t2, 0.0), axis=1, keepdims=True),
                            jnp.sum(jnp.where(h0, 0.0, t2), axis=1, keepdims=True)], axis=0)


def _band_fwd(q, k, v, *, S, dil, max_dist, hq, hk, scale, name):
    hd = 64
    plan = _BandPlan(S, dil, max_dist)
    qv, kv, vv = plan.view(q), plan.view(k), plan.view(v)
    wq, wk = hq * hd, hk * hd
    grp = hq // hk
    ns, nb = plan.ns, plan.nb

    def body(q_ref, k_ref, kp_ref, v_ref, vp_ref, o_ref, l_ref):
        i = pl.program_id(1)
        m_cur, m_band = plan.masks()
        has_prev = i > 0
        for sg in range(ns):
            for b in range(nb):
                rows = slice(b * BLOCK, (b + 1) * BLOCK)
                prows = slice((b - 1) * BLOCK, b * BLOCK) if b > 0 else slice(0, BLOCK)
                m_prev = m_band if b > 0 else (m_band & has_prev)
                msk = jnp.concatenate([m_prev, m_cur], axis=1)
                msk = jnp.concatenate([msk, msk], axis=0)
                for pr in range(wq // LANE):
                    cols = slice(sg * wq + pr * LANE, sg * wq + (pr + 1) * LANE)
                    kcat = jnp.concatenate([_kv_group(k_ref if b > 0 else kp_ref, prows, sg * wk, pr, grp, hq, hk),
                                            _kv_group(k_ref, rows, sg * wk, pr, grp, hq, hk)], axis=0)
                    vcat = jnp.concatenate([_kv_group(v_ref if b > 0 else vp_ref, prows, sg * wk, pr, grp, hq, hk),
                                            _kv_group(v_ref, rows, sg * wk, pr, grp, hq, hk)], axis=0)
                    qs = _stack_heads(q_ref[rows, cols].astype(BF16))
                    s = lax.dot_general(qs, kcat, _NT, preferred_element_type=F32) * scale
                    s = jnp.where(msk, s, NEG)
                    mx = jnp.max(s, axis=1, keepdims=True)
                    p = jnp.exp(s - mx)
                    l = jnp.sum(p, axis=1, keepdims=True)
                    acc = jnp.dot(p.astype(BF16), vcat, preferred_element_type=F32)
                    o_ref[rows, cols] = _unstack_heads(acc / l)
                    l_ref[rows, cols] = _unstack_heads(jnp.broadcast_to(mx + jnp.log(l), (2 * BLOCK, LANE)))

    o, lse = _pcall(
        body, name=name, grid=plan.grid,
        in_specs=[plan.main(qv), plan.main(kv), plan.edge(kv, False), plan.main(vv), plan.edge(vv, False)],
        out_specs=[plan.out(wq)] * 2, out_shape=[jax.ShapeDtypeStruct((S // dil, dil * wq), F32)] * 2,
        compiler_params=_cparams(("parallel", "parallel")),
    )(qv[0], kv[0], kv[0], vv[0], vv[0])
    return o.reshape(S, wq), lse.reshape(S, wq), (_Viewed((o, wq, 0, 1)), _Viewed((lse, wq, 0, 1)))


def _band_views(arrays, *, S, dil, max_dist):
    plan = _BandPlan(S, dil, max_dist)
    return [None if a is None else plan.view(a) for a in arrays]


def _band_dq(q, k, v, do, o, lse, dlse, *, S, dil, max_dist, hq, hk, scale, out_dtype=F32, name):
    hd = 64
    plan = _BandPlan(S, dil, max_dist)
    qv, kv, vv = plan.view(q), plan.view(k), plan.view(v)
    dov, ov, lv = plan.view(do), plan.view(o), plan.view(lse)
    has_dl = dlse is not None
    wq, wk = hq * hd, hk * hd
    grp = hq // hk
    ns, nb = plan.ns, plan.nb
    ops = [qv, kv, kv, vv, vv, dov, ov, lv]
    specs = [plan.main(qv), plan.main(kv), plan.edge(kv, False), plan.main(vv), plan.edge(vv, False), plan.main(dov),
             plan.main(ov), plan.main(lv)]
    if has_dl:
        dlv = plan.view(dlse)
        ops.append(dlv)
        specs.append(plan.main(dlv))

    def body(*refs):
        q_ref, k_ref, kp_ref, v_ref, vp_ref, do_ref, o_ref, l_ref = refs[:8]
        dl_ref = refs[8] if has_dl else None
        dq_ref = refs[-1]
        i = pl.program_id(1)
        m_cur, m_band = plan.masks()
        has_prev = i > 0
        for sg in range(ns):
            for b in range(nb):
                rows = slice(b * BLOCK, (b + 1) * BLOCK)
                prows = slice((b - 1) * BLOCK, b * BLOCK) if b > 0 else slice(0, BLOCK)
                m_prev = m_band if b > 0 else (m_band & has_prev)
                msk = jnp.concatenate([m_prev, m_cur], axis=1)
                msk = jnp.concatenate([msk, msk], axis=0)
                for pr in range(wq // LANE):
                    cols = slice(sg * wq + pr * LANE, sg * wq + (pr + 1) * LANE)
                    kcat = jnp.concatenate([_kv_group(k_ref if b > 0 else kp_ref, prows, sg * wk, pr, grp, hq, hk),
                                            _kv_group(k_ref, rows, sg * wk, pr, grp, hq, hk)], axis=0)
                    vcat = jnp.concatenate([_kv_group(v_ref if b > 0 else vp_ref, prows, sg * wk, pr, grp, hq, hk),
                                            _kv_group(v_ref, rows, sg * wk, pr, grp, hq, hk)], axis=0)
                    qs = _stack_heads(q_ref[rows, cols].astype(BF16))
                    do2 = do_ref[rows, cols].astype(F32)
                    prod = do2 * o_ref[rows, cols].astype(F32)
                    if has_dl:
                        prod = prod - dl_ref[rows, cols].astype(F32)
                    delta = _per_head_sum(prod)
                    lse_rows = _per_head_col(l_ref[rows, cols])
                    dos = _stack_heads(do2.astype(BF16))
                    s = lax.dot_general(qs, kcat, _NT, preferred_element_type=F32) * scale
                    pm = jnp.where(msk, jnp.exp(s - lse_rows), 0.0)
                    dp = lax.dot_general(dos, vcat, _NT, preferred_element_type=F32)
                    ds = (pm * (dp - delta) * scale).astype(BF16)
                    dq_ref[rows, cols] = _unstack_heads(jnp.dot(ds, kcat, preferred_element_type=F32)).astype(dq_ref.dtype)

    dq = _pcall(
        body, name=name, grid=plan.grid, in_specs=specs, out_specs=plan.out(wq),
        out_shape=jax.ShapeDtypeStruct((S // dil, dil * wq), out_dtype), compiler_params=_cparams(("parallel", "parallel")),
    )(*[o_[0] for o_ in ops])
    return dq.reshape(S, wq)


def _band_dkv(q, k, v, do, o, lse, dlse, *, S, dil, max_dist, hq, hk, scale, out_dtype=F32, name):
    hd = 64
    plan = _BandPlan(S, dil, max_dist)
    qv, kv, vv = plan.view(q), plan.view(k), plan.view(v)
    dov, ov, lv = plan.view(do), plan.view(o), plan.view(lse)
    has_dl = dlse is not None
    wq, wk = hq * hd, hk * hd
    grp = hq // hk
    ns, nb = plan.ns, plan.nb
    qlike = [qv, dov, ov, lv] + ([plan.view(dlse)] if has_dl else [])
    ops = [kv, vv] + qlike + qlike
    specs = [plan.main(kv), plan.main(vv)] + [plan.main(t) for t in qlike] + [plan.edge(t, True) for t in qlike]
    nql = len(qlike)
    nkg = wk // LANE

    def body(*refs):
        k_ref, v_ref = refs[:2]
        mains, edges = refs[2:2 + nql], refs[2 + nql:2 + 2 * nql]
        dk_ref, dv_ref = refs[2 + 2 * nql:]
        i = pl.program_id(1)
        m_cur, m_band = plan.masks()
        has_next = i < plan.grid[1] - 1
        for sg in range(ns):
            for b in range(nb):
                rows = slice(b * BLOCK, (b + 1) * BLOCK)
                nxt_in_main = b + 1 < nb
                nrows = slice((b + 1) * BLOCK, (b + 2) * BLOCK) if nxt_in_main else slice(0, BLOCK)
                nsrc = mains if nxt_in_main else edges
                m_next = m_band if nxt_in_main else (m_band & has_next)
                mt_cur, mt_next = m_cur.T, m_next.T
                msk = jnp.concatenate([mt_cur, mt_cur, mt_next, mt_next], axis=1)
                nacc = nkg if hk == hq else hk
                dk_acc, dv_acc = [None] * nacc, [None] * nacc
                for pr in range(wq // LANE):
                    cols = slice(sg * wq + pr * LANE, sg * wq + (pr + 1) * LANE)
                    kop = _kv_group(k_ref, rows, sg * wk, pr, grp, hq, hk)
                    vop = _kv_group(v_ref, rows, sg * wk, pr, grp, hq, hk)
                    qs, dos, deltas, lses = [], [], [], []
                    for src, r in ((mains, rows), (nsrc, nrows)):
                        qs.append(_stack_heads(src[0][r, cols].astype(BF16)))
                        do2 = src[1][r, cols].astype(F32)
                        prod = do2 * src[2][r, cols].astype(F32)
                        if has_dl:
                            prod = prod - src[4][r, cols].astype(F32)
                        prod_t, lse_t = prod.T, src[3][r, cols].T
                        for e in range(2):
                            deltas.append(jnp.sum(prod_t[e * 64:(e + 1) * 64], axis=0, keepdims=True))
                            lses.append(lse_t[e * 64:e * 64 + 1])
                        dos.append(_stack_heads(do2.astype(BF16)))
                    qs4, dos4 = jnp.concatenate(qs, axis=0), jnp.concatenate(dos, axis=0)
                    delta4, lse4 = jnp.concatenate(deltas, axis=1), jnp.concatenate(lses, axis=1)
                    st = lax.dot_general(kop, qs4, _NT, preferred_element_type=F32) * scale
                    pm = jnp.where(msk, jnp.exp(st - lse4), 0.0)
                    dpt = lax.dot_general(vop, dos4, _NT, preferred_element_type=F32)
                    ds = (pm * (dpt - delta4) * scale).astype(BF16)
                    tv = jnp.dot(pm.astype(BF16), dos4, preferred_element_type=F32)
                    tk = jnp.dot(ds, qs4, preferred_element_type=F32)
                    ai = pr if hk == hq else (2 * pr) // grp
                    dv_acc[ai] = tv if dv_acc[ai] is None else dv_acc[ai] + tv
                    dk_acc[ai] = tk if dk_acc[ai] is None else dk_acc[ai] + tk
                for kg in range(nkg):
                    cols = slice(sg * wk + kg * LANE, sg * wk + (kg + 1) * LANE)
                    if hk == hq:
                        dkt, dvt = dk_acc[kg], dv_acc[kg]
                    else:
                        both = lambda t: t + _swap_halves(t)
                        h0 = _half(0)
                        dkt = jnp.where(h0, both(dk_acc[2 * kg]), both(dk_acc[2 * kg + 1]))
                        dvt = jnp.where(h0, both(dv_acc[2 * kg]), both(dv_acc[2 * kg + 1]))
                    dk_ref[rows, cols] = dkt.astype(dk_ref.dtype)
                    dv_ref[rows, cols] = dvt.astype(dv_ref.dtype)

    dk, dv = _pcall(
        body, name=name, grid=plan.grid, in_specs=specs, out_specs=[plan.out(wk)] * 2,
        out_shape=[jax.ShapeDtypeStruct((S // dil, dil * wk), out_dtype)] * 2,
        compiler_params=_cparams(("parallel", "parallel")),
    )(*[o_[0] for o_ in ops])
    return dk.reshape(S, wk), dv.reshape(S, wk)


def _causal_block(S):
    return min(512, S)


def _causal_mask(bq):
    qi = lax.broadcasted_iota(jnp.int32, (bq, bq), 0)
    kj = lax.broadcasted_iota(jnp.int32, (bq, bq), 1)
    return kj <= qi


def _half_of(rows, e):
    lane = lax.broadcasted_iota(jnp.int32, (rows, LANE), 1)
    return (lane < LANE // 2) if e == 0 else (lane >= LANE // 2)


def _causal_fwd(q, k, v, *, heads, scale, name, bk=None):
    S = q.shape[0]
    bq = _causal_block(S)
    bk = bq if bk is None else min(bk, bq)
    r = bq // bk
    nq, nk = S // bq, S // bk
    npair = heads // 2
    wv = heads * 64
    c2 = scale * math.log2(math.e)

    def body(q_ref, k_ref, v_ref, o_ref, l_ref, m_sc, l_sc, acc_sc):
        i, t = pl.program_id(0), pl.program_id(1)

        @pl.when(t == 0)
        def _():
            m_sc[...] = jnp.full(m_sc.shape, NEG, F32)
            l_sc[...] = jnp.zeros(l_sc.shape, F32)
            acc_sc[...] = jnp.zeros(acc_sc.shape, F32)

        def step(masked):
            if masked:
                qi = lax.broadcasted_iota(jnp.int32, (bq, bk), 0)
                kj = lax.broadcasted_iota(jnp.int32, (bq, bk), 1) + (t - r * i) * bk
                msk = kj <= qi
            h0 = _half_of(bq, 0)
            ss = []
            for h in range(heads):
                cols = slice(h * LANE, (h + 1) * LANE)
                sh = lax.dot_general(q_ref[:, cols], k_ref[:, cols], _NT, preferred_element_type=F32)
                ss.append(jnp.where(msk, sh, NEG) if masked else sh)
            m_old = [m_sc[h] for h in range(heads)]
            m_new = [jnp.maximum(m_old[h], jnp.max(ss[h], axis=1, keepdims=True)) for h in range(heads)]
            ps = [jnp.exp2((ss[h] - m_new[h]) * c2) for h in range(heads)]
            alpha = [jnp.exp2((m_old[h] - m_new[h]) * c2) for h in range(heads)]
            for h in range(heads):
                l_sc[h] = alpha[h] * l_sc[h] + jnp.sum(ps[h], axis=1, keepdims=True)
                m_sc[h] = m_new[h]
            for pr in range(npair):
                v2 = v_ref[:, pr * LANE:(pr + 1) * LANE].astype(BF16)
                pv = [jnp.dot(ps[2 * pr + e].astype(BF16), v2, preferred_element_type=F32) for e in range(2)]
                acc = acc_sc[pr]
                acc_sc[pr] = jnp.where(h0, alpha[2 * pr] * acc + pv[0], alpha[2 * pr + 1] * acc + pv[1])

        pl.when(t < r * i)(lambda: step(False))
        pl.when((t >= r * i) & (t < r * (i + 1)))(lambda: step(True))

        @pl.when(t == nk - 1)
        def _():
            h0 = _half_of(bq, 0)
            for pr in range(npair):
                l0, l1 = l_sc[2 * pr], l_sc[2 * pr + 1]
                acc = acc_sc[pr]
                cols = slice(pr * LANE, (pr + 1) * LANE)
                o_ref[:, cols] = jnp.where(h0, acc / l0, acc / l1)
                l_ref[:, cols] = jnp.where(h0, m_sc[2 * pr] * scale + jnp.log(l0), m_sc[2 * pr + 1] * scale + jnp.log(l1))

    kvi = lambda i, t: (jnp.minimum(t, r * (i + 1) - 1), 0)
    qs = pl.BlockSpec((bq, heads * LANE), lambda i, t: (i, 0))
    ks = pl.BlockSpec((bk, heads * LANE), kvi)
    vs = pl.BlockSpec((bk, wv), kvi)
    os_ = pl.BlockSpec((bq, wv), lambda i, t: (i, 0))
    return _pcall(
        body, name=name, grid=(nq, nk), in_specs=[qs, ks, vs], out_specs=[os_, os_],
        out_shape=[jax.ShapeDtypeStruct((S, wv), F32)] * 2,
        scratch_shapes=[pltpu.VMEM((heads, bq, 1), F32), pltpu.VMEM((heads, bq, 1), F32), pltpu.VMEM((npair, bq, LANE), F32)],
        compiler_params=_cparams(("parallel", "arbitrary")),
    )(q, k, v)


def _causal_bwd_tile(q_ref, k_ref, v2, do2, prod, l2, h, e, scale, msk, bq):
    cols = slice(h * LANE, (h + 1) * LANE)
    hm = _half_of(bq, e)
    s = lax.dot_general(q_ref[:, cols], k_ref[:, cols], _NT, preferred_element_type=F32)
    c2 = scale * math.log2(math.e)
    p = jnp.exp2((s - l2[:, e * 64:e * 64 + 1] * (1.0 / scale)) * c2)
    if msk is not None:
        p = jnp.where(msk, p, 0.0)
    dom = jnp.where(hm, do2, jnp.zeros_like(do2))
    delta = jnp.sum(jnp.where(hm, prod, 0.0), axis=1, keepdims=True)
    dp = lax.dot_general(dom, v2, _NT, preferred_element_type=F32)
    ds = (p * (dp - delta) * scale).astype(BF16)
    return p, ds, dom


def _causal_dq(q, k, v, do, o, lse, *, heads, scale, out_dtype=BF16, name):
    S = q.shape[0]
    bq = _causal_block(S)
    nq = S // bq
    npair = heads // 2
    wv = heads * 64

    def body(q_ref, k_ref, v_ref, do_ref, o_ref, l_ref, dq_ref, dq_sc):
        i, t = pl.program_id(0), pl.program_id(1)

        @pl.when(t == 0)
        def _():
            dq_sc[...] = jnp.zeros(dq_sc.shape, F32)

        def step(masked):
            msk = _causal_mask(bq) if masked else None
            for pr in range(npair):
                pc = slice(pr * LANE, (pr + 1) * LANE)
                v2 = v_ref[:, pc].astype(BF16)
                dof = do_ref[:, pc].astype(F32)
                prod = dof * o_ref[:, pc]
                do2 = dof.astype(BF16)
                l2 = l_ref[:, pc]
                for e in range(2):
                    h = 2 * pr + e
                    _, ds, _ = _causal_bwd_tile(q_ref, k_ref, v2, do2, prod, l2, h, e, scale, msk, bq)
                    dq_sc[h] += jnp.dot(ds, k_ref[:, h * LANE:(h + 1) * LANE], preferred_element_type=F32)

        pl.when(t < i)(lambda: step(False))
        pl.when(t == i)(lambda: step(True))

        @pl.when(t == nq - 1)
        def _():
            for h in range(heads):
                dq_ref[:, h * LANE:(h + 1) * LANE] = dq_sc[h].astype(dq_ref.dtype)

    qs = pl.BlockSpec((bq, heads * LANE), lambda i, t: (i, 0))
    ks = pl.BlockSpec((bq, heads * LANE), lambda i, t: (jnp.minimum(t, i), 0))
    vs = pl.BlockSpec((bq, wv), lambda i, t: (jnp.minimum(t, i), 0))
    os_ = pl.BlockSpec((bq, wv), lambda i, t: (i, 0))
    return _pcall(
        body, name=name, grid=(nq, nq), in_specs=[qs, ks, vs, os_, os_, os_], out_specs=qs,
        out_shape=jax.ShapeDtypeStruct((S, heads * LANE), out_dtype),
        scratch_shapes=[pltpu.VMEM((heads, bq, LANE), F32)],
        compiler_params=_cparams(("parallel", "arbitrary")),
    )(q, k, v, do, o, lse)


def _causal_dkv(q, k, v, do, o, lse, *, heads, scale, out_dtype=BF16, name):
    S = q.shape[0]
    bq = _causal_block(S)
    nq = S // bq
    npair = heads // 2
    wv = heads * 64

    def body(q_ref, k_ref, v_ref, do_ref, o_ref, l_ref, dk_ref, dv_ref, dk_sc, dv_sc):
        j, t = pl.program_id(0), pl.program_id(1)

        @pl.when(t == 0)
        def _():
            dk_sc[...] = jnp.zeros(dk_sc.shape, F32)
            dv_sc[...] = jnp.zeros(dv_sc.shape, F32)

        def step(masked):
            c2 = scale * math.log2(math.e)
            if masked:
                msk = lax.broadcasted_iota(jnp.int32, (bq, bq), 0) <= lax.broadcasted_iota(jnp.int32, (bq, bq), 1)
            sts, dpts, doms, deltas, lses = [], [], [], [], []
            for pr in range(npair):
                pc = slice(pr * LANE, (pr + 1) * LANE)
                v2 = v_ref[:, pc].astype(BF16)
                dof = do_ref[:, pc].astype(F32)
                prod_t = (dof * o_ref[:, pc]).T
                lse_t = l_ref[:, pc].T
                do2 = dof.astype(BF16)
                for e in range(2):
                    h = 2 * pr + e
                    cols = slice(h * LANE, (h + 1) * LANE)
                    dom = jnp.where(_half_of(bq, e), do2, jnp.zeros_like(do2))
                    sts.append(lax.dot_general(k_ref[:, cols], q_ref[:, cols], _NT, preferred_element_type=F32))
                    dpts.append(lax.dot_general(v2, dom, _NT, preferred_element_type=F32))
                    doms.append(dom)
                    deltas.append(jnp.sum(prod_t[e * 64:(e + 1) * 64], axis=0, keepdims=True))
                    lses.append(lse_t[e * 64:e * 64 + 1] * (1.0 / scale))
            pts = [jnp.exp2((sts[h] - lses[h]) * c2) for h in range(heads)]
            if masked:
                pts = [jnp.where(msk, t_, 0.0) for t_ in pts]
            dsts = [(pts[h] * (dpts[h] - deltas[h]) * scale).astype(BF16) for h in range(heads)]
            for pr in range(npair):
                tv = [jnp.dot(pts[2 * pr + e].astype(BF16), doms[2 * pr + e], preferred_element_type=F32) for e in range(2)]
                dv_sc[pr] += tv[0] + tv[1]
            for h in range(heads):
                dk_sc[h] += jnp.dot(dsts[h], q_ref[:, h * LANE:(h + 1) * LANE], preferred_element_type=F32)

        pl.when(t > j)(lambda: step(False))
        pl.when(t == j)(lambda: step(True))

        @pl.when(t == nq - 1)
        def _():
            for h in range(heads):
                dk_ref[:, h * LANE:(h + 1) * LANE] = dk_sc[h].astype(dk_ref.dtype)
            for pr in range(npair):
                dv_ref[:, pr * LANE:(pr + 1) * LANE] = dv_sc[pr].astype(dv_ref.dtype)

    qi = lambda j, t: (jnp.maximum(t, j), 0)
    qs = pl.BlockSpec((bq, heads * LANE), qi)
    os_ = pl.BlockSpec((bq, wv), qi)
    ks = pl.BlockSpec((bq, heads * LANE), lambda j, t: (j, 0))
    vs = pl.BlockSpec((bq, wv), lambda j, t: (j, 0))
    return _pcall(
        body, name=name, grid=(nq, nq), in_specs=[qs, ks, vs, os_, os_, os_], out_specs=[ks, vs],
        out_shape=[jax.ShapeDtypeStruct((S, heads * LANE), out_dtype), jax.ShapeDtypeStruct((S, wv), out_dtype)],
        scratch_shapes=[pltpu.VMEM((heads, bq, LANE), F32), pltpu.VMEM((npair, bq, LANE), F32)],
        compiler_params=_cparams(("parallel", "arbitrary")),
    )(q, k, v, do, o, lse)


@jax.custom_vjp
def _bdot(x, w):
    return jnp.dot(x.astype(BF16), w.astype(BF16), preferred_element_type=F32)


def _bdot_fwd(x, w):
    return _bdot(x, w), (x, w)


def _bdot_bwd(res, g):
    x, w = res
    gb = g.astype(BF16)
    dx = lax.dot_general(gb, w.astype(BF16), _NT, preferred_element_type=F32)
    dw = lax.dot_general(x.astype(BF16), gb, _TN, preferred_element_type=F32)
    return dx.astype(x.dtype), dw.astype(w.dtype)


_bdot.defvjp(_bdot_fwd, _bdot_bwd)


def _tile_matrix(hd, width):
    r = lax.broadcasted_iota(jnp.int32, (hd, width), 0)
    c = lax.broadcasted_iota(jnp.int32, (hd, width), 1) % hd
    return jnp.where(r == c, 1.0, 0.0).astype(F32)


def _spread_matrix(heads, width):
    per = width // heads
    r = lax.broadcasted_iota(jnp.int32, (heads, width), 0)
    c = lax.broadcasted_iota(jnp.int32, (heads, width), 1) // per
    return jnp.where(r == c, 1.0, 0.0).astype(F32)


def _wide(t, width):
    n = width // t.shape[-1]
    return jnp.concatenate([t] * n, axis=1) if n > 1 else t


def _norm_heads(x, gain, hd):
    return _head_norm(x, _exact_dot(gain, _tile_matrix(hd, x.shape[-1])), hd)


def _norm_rope(x, gain, cos, sin, hd):
    w = x.shape[-1]
    return _rope(_norm_heads(x, gain, hd), _wide(cos, w), _wide(sin, w), hd)


def _f_norm(x, g):
    return (_row_norm(x, g),)


def _f_prep_acm(aq, ak, c0q, c0k, c0v, c1q, c1k, c1v, c2q, c2k, c2v, mq, cos, sin, g_aq, g_ak, g0q, g0k, g1q, g1k, g2q, g2k, g_mq):
    outs = [_norm_rope(aq, g_aq, cos, sin, A_HD), _norm_rope(ak, g_ak, cos, sin, A_HD)]
    for cq, ck, cv, gq, gk in ((c0q, c0k, c0v, g0q, g0k), (c1q, c1k, c1v, g1q, g1k), (c2q, c2k, c2v, g2q, g2k)):
        outs += [_norm_rope(cq, gq, cos, sin, C_HD), _norm_rope(ck, gk, cos, sin, C_HD), cv]
    outs.append(_norm_heads(mq, g_mq, M_HD))
    return tuple(outs)


def _rope_mla_q(x, cos, sin):
    lane = lax.broadcasted_iota(jnp.int32, x.shape, x.ndim - 1) % LANE
    half = B_ROPE // 2
    first = (lane >= B_NOPE) & (lane < B_NOPE + half)
    other = jnp.where(first, -_lane_roll(x, -half), _lane_roll(x, half))
    return x * cos + other * sin


def _f_prep_b(ckv, cq, kr, cosq, sinq, cosr, sinr, g_qa, g_kva, w_uq, w_ukv, g_q, g_k, g_kr):
    wq = B_HEADS * LANE
    q_up = _bdot(_row_norm(cq, g_qa), w_uq)
    gq = _exact_dot(g_q, _tile_matrix(LANE, wq))
    qf = _rope_mla_q(_head_norm(q_up, gq, "mla"), _wide(cosq, wq), _wide(sinq, wq))
    kv_up = _bdot(_row_norm(ckv, g_kva), w_ukv)
    kn = _head_norm(kv_up[:, :wq], _exact_dot(g_k, _tile_matrix(LANE, wq)), B_NOPE)
    vb = kv_up[:, wq:]
    kp = _rope(_head_norm(kr, g_kr, B_ROPE), cosr, sinr, B_ROPE)
    kp = _lane_roll(kp, B_NOPE)
    return qf, kn + _wide(kp, wq), vb


def _f_mem_k(k, g):
    return (_norm_heads(k, g, M_HD),)


def _f_sink(o, lse, sink):
    sb = _exact_dot(sink, _spread_matrix(A_HEADS, o.shape[-1]))
    m = jnp.maximum(lse, sb)
    tot = m + jnp.log(jnp.exp(lse - m) + jnp.exp(sb - m))
    return (o * jnp.exp(lse - tot),)


def _f_combine(o0, o1, o2, l0, l1, l2):
    m = jnp.maximum(jnp.maximum(l0, l1), l2)
    e0, e1, e2 = jnp.exp(l0 - m), jnp.exp(l1 - m), jnp.exp(l2 - m)
    inv = 1.0 / (e0 + e1 + e2)
    return ((e0 * o0 + e1 * o1 + e2 * o2) * inv,)


def _f_gatemix(gp, y0, y1, y2, y3, bg):
    d = y0.shape[-1]
    gates = 1.0 / (1.0 + jnp.exp(-(gp + bg)))
    mix = gates[:, :d] * y0
    for n, y in enumerate((y1, y2, y3), start=1):
        mix = mix + gates[:, n * d:(n + 1) * d] * y
    return (mix,)


def _relu2(u):
    return jnp.square(jnp.maximum(u, 0.0))


def _add(r, e):
    return r + e.astype(F32)


def _relu2_grad(r, u):
    return r * (2.0 * jnp.maximum(u, 0.0))


def _loss_and_grad(y, target, *, tb=512):
    R, D = y.shape
    tb = min(tb, R)

    def body(y_ref, t_ref, dy_ref, l_ref):
        err = y_ref[...] - t_ref[...]
        dy_ref[...] = err * (1.0 / D)
        part = 0.5 * jnp.sum(jnp.sum(err * err, axis=1, keepdims=True) * (1.0 / D), axis=0, keepdims=True)
        first = pl.program_id(0) == 0

        @pl.when(first)
        def _():
            l_ref[...] = jnp.broadcast_to(part, l_ref.shape)

        @pl.when(jnp.logical_not(first))
        def _():
            l_ref[...] += jnp.broadcast_to(part, l_ref.shape)

    dy, l = _pcall(
        body, name="loss", grid=(R // tb,),
        in_specs=[pl.BlockSpec((tb, D), lambda i: (i, 0))] * 2,
        out_specs=[pl.BlockSpec((tb, D), lambda i: (i, 0)), pl.BlockSpec((8, LANE), lambda i: (0, 0))],
        out_shape=[jax.ShapeDtypeStruct((R, D), F32), jax.ShapeDtypeStruct((8, LANE), F32)],
        compiler_params=_cparams(("arbitrary",)),
    )(y, target)
    return l[0, 0], dy


def _z_layout(d):
    assert d == 1024, "the aligned layout below is laid out for D_MODEL = 1024"
    lay = {"gates": (4 * d, 0)}
    for g in range(3):
        for n, nm in enumerate("qkv"):
            lay[f"c{g}{nm}"] = (512, 8 + 3 * g + n)
    lay.update(aq=(512, 17), mq=(512, 18), ckv=(256, 38), cq=(384, 26), ak=(128, 81), av=(128, 82), kr=(128, 83))
    return lay, 10752


_KW_A = dict(hq=A_HEADS, hk=A_KV_HEADS, scale=A_HD ** -0.5)
_KW_B = dict(heads=B_HEADS, scale=(B_NOPE + B_ROPE) ** -0.5)
_KW_C = dict(hq=C_HEADS, hk=C_HEADS, scale=C_HD ** -0.5)
_KW_M = dict(hq=M_HEADS, hk=M_HEADS, hd=M_HD, hdv=M_HD, scale=M_HD ** -0.5)


def _layer_fwd(l, x, mem, w, tabs, late=None):
    S, D = x.shape
    lay, _ = _z_layout(D)
    cosA, sinA, cosB, sinB, cosQ, sinQ = tabs
    gM = _AttnGeom("full", S, mem.shape[0])
    nm = lambda s: f"l{l}_{s}"
    sv = {}
    hn = _rowmap(_f_norm, [x], [w["g_mix"]], [(D, BF16)], tb=512, name=nm("norm1"))[0]
    z = _mm(hn, w["in"], name=nm("in"), tn=1536)
    zp = {k: (z, wd, idx) for k, (wd, idx) in lay.items()}
    acm_rows = [zp[k] for k in ("aq", "ak", "c0q", "c0k", "c0v", "c1q", "c1k", "c1v", "c2q", "c2k", "c2v", "mq")] + [cosA, sinA]
    acm_par = [w["a_qn"], w["a_kn"], w["c0q"], w["c0k"], w["c1q"], w["c1k"], w["c2q"], w["c2k"], w["m_qn"]]
    acm = _rowmap(_f_prep_acm, acm_rows, acm_par, [(p[1], BF16) for p in acm_rows[:12]], tb=256, name=nm("prep_acm"))
    qa, ka, qc0, kc0, vc0, qc1, kc1, vc1, qc2, kc2, vc2, mq = acm
    oa_raw, lse_a, _ = _band_fwd(qa, ka, zp["av"], S=S, dil=1, max_dist=A_WINDOW - 1, name=nm("attn_a"), **_KW_A)
    o_a = _rowmap(_f_sink, [oa_raw, lse_a], [w["a_sink"]], [(512, BF16)], tb=512, name=nm("sink"))[0]
    oc, lc, cviews = [], [], []
    for g, ((win, dil), qc, kc, vc) in enumerate(zip(C_PATTERNS, (qc0, qc1, qc2), (kc0, kc1, kc2), (vc0, vc1, vc2))):
        qkv = _band_views([qc, kc, vc], S=S, dil=dil, max_dist=win // dil)
        o_g, l_g, ol_views = _band_fwd(*qkv, S=S, dil=dil, max_dist=win // dil, name=nm(f"attn_c{g}"), **_KW_C)
        oc.append(o_g)
        lc.append(l_g)
        cviews.append(tuple(qkv) + tuple(ol_views))
    o_c = _rowmap(_f_combine, oc + lc, [], [(512, BF16)], tb=512, name=nm("combine"))[0]
    if late is not None:
        w = dict(w, **late(o_c))

    b_rows = [zp["ckv"], zp["cq"], zp["kr"], cosQ, sinQ, cosB, sinB]
    b_par = [w["b_qa"], w["b_kva"], w["uq"], w["ukv"], w["b_q"], w["b_k"], w["b_kr"]]
    qb, kb, vb = _rowmap(_f_prep_b, b_rows, b_par, [(B_HEADS * LANE, BF16), (B_HEADS * LANE, BF16), (512, BF16)], tb=256,
                         name=nm("prep_b"))
    memn = _rowmap(_f_norm, [mem], [w["m_g_mem"]], [(D, BF16)], tb=256, name=nm("mem_norm"))[0]
    mkv = _mm(memn, w["mkv"], name=nm("mem_kv"))
    mk = _rowmap(_f_mem_k, [(mkv, 512, 0)], [w["m_kn"]], [(512, BF16)], tb=256, name=nm("mem_k"))[0]
    mv = (mkv, 512, 1)
    o_b, lse_b = _causal_fwd(qb, kb, vb, name=nm("attn_b"), **_KW_B)
    o_m, lse_m = _attn_fwd(gM, mq, mk, mv, name=nm("attn_m"), **_KW_M)

    o_n = [o_a, o_b, o_c, o_m]
    ys = [_mm(o_n[n], w["branch"][n], name=nm(f"branch{n}")) for n in range(N_BRANCH)]
    mix = _rowmap(_f_gatemix, [zp["gates"]] + ys, [w["b_gate"]], [(D, BF16)], tb=256, name=nm("gatemix"))[0]
    x1 = _mm(mix, w["out"], extra=x, epi=_add, name=nm("out"))
    hn2 = _rowmap(_f_norm, [x1], [w["g_mlp"]], [(D, BF16)], tb=512, name=nm("norm2"))[0]
    u = _mm(hn2, w["up"], name=nm("up"))
    x2 = _mm(u, w["down"], pro_a=_relu2, extra=x1, epi=_add, name=nm("down"))
    sv.update(x=x, hn=hn, z=z, acm=acm, bq=(qb, kb, vb), memn=memn, mkv=mkv, mk=mk, oa_raw=oa_raw, lse_a=lse_a,
              o_b=o_b, lse_b=lse_b, oc=oc, lc=lc, cviews=cviews, o_m=o_m, lse_m=lse_m, o_n=o_n, ys=ys, mix=mix, x1=x1, hn2=hn2, u=u)
    return x2, sv, w


def _layer_bwd(l, dx2, mem, w, tabs, sv, early=None):
    x, z, x1, u = sv["x"], sv["z"], sv["x1"], sv["u"]
    S, D = x.shape
    lay, zw = _z_layout(D)
    cosA, sinA, cosB, sinB, cosQ, sinQ = tabs
    gM = _AttnGeom("full", S, mem.shape[0])
    nm = lambda s: f"l{l}_{s}"
    zp = {k: (z, wd, idx) for k, (wd, idx) in lay.items()}
    g = {}
    du = _mm(dx2, w["down"], tb=True, extra=u, epi=_relu2_grad, out_dtype=BF16, name=nm("d_down_x"))
    g["down"] = _mm(u, dx2, ta=True, pro_a=_relu2, name=nm("d_down_w"))
    dhn2 = _mm(du, w["up"], tb=True, name=nm("d_up_x"))
    g["up"] = _mm(sv["hn2"], du, ta=True, name=nm("d_up_w"))
    (dx1,), (g["g_mlp"],) = _rowmap_bwd(_f_norm, [x1], [w["g_mlp"]], [dhn2], diff=[True], out_dtypes=[F32], adds=[dx2],
                                        tb=256, name=nm("d_norm2"))
    dmix = _mm(dx1, w["out"], tb=True, name=nm("d_out_x"))
    g["out"] = _mm(sv["mix"], dx1, ta=True, name=nm("d_out_w"))
    (dgates, dy0, dy1, dy2, dy3), (g["b_gate"],) = _rowmap_bwd(
        _f_gatemix, [zp["gates"]] + sv["ys"], [w["b_gate"]], [dmix], diff=[True] * 5, out_dtypes=[BF16] * 5,
        tb=128, name=nm("d_gatemix"))
    dys = [dy0, dy1, dy2, dy3]
    do = [_mm(dys[n], w["branch"][n], tb=True, name=nm(f"d_branch{n}_x")) for n in range(N_BRANCH)]
    g["branch"] = [_mm(sv["o_n"][n], dys[n], ta=True, name=nm(f"d_branch{n}_w")) for n in range(N_BRANCH)]
    qa, ka, qc0, kc0, vc0, qc1, kc1, vc1, qc2, kc2, vc2, mq = sv["acm"]
    qb, kb, vb = sv["bq"]
    mkv, mk = sv["mkv"], sv["mk"]
    mv = (mkv, 512, 1)
    dmq = _attn_dq(gM, mq, mk, mv, do[3], sv["o_m"], sv["lse_m"], None, name=nm("attn_m_dq"), **_KW_M)
    dmk, dmv = _attn_dkv(gM, mq, mk, mv, do[3], sv["o_m"], sv["lse_m"], None, name=nm("attn_m_dkv"), **_KW_M)
    (doc0, doc1, doc2, dl0, dl1, dl2), _ = _rowmap_bwd(_f_combine, sv["oc"] + sv["lc"], [], [do[2]], diff=[True] * 6,
                                                      out_dtypes=[F32] * 6, tb=256, name=nm("d_combine"))
    dqc, dkc, dvc = [], [], []
    for gi, ((win, dil), qc, kc, vc, doc, dl) in enumerate(zip(C_PATTERNS, (qc0, qc1, qc2), (kc0, kc1, kc2), (vc0, vc1, vc2),
                                                              (doc0, doc1, doc2), (dl0, dl1, dl2))):
        qv_, kv_, vv_, ov_, lv_ = sv["cviews"][gi]
        dov_, dlv_ = _band_views([doc, dl], S=S, dil=dil, max_dist=win // dil)
        args = (qv_, kv_, vv_, dov_, ov_, lv_, dlv_)
        kwc = dict(S=S, dil=dil, max_dist=win // dil, out_dtype=BF16, **_KW_C)
        dqc.append(_band_dq(*args, name=nm(f"attn_c{gi}_dq"), **kwc))
        dk_, dv_ = _band_dkv(*args, name=nm(f"attn_c{gi}_dkv"), **kwc)
        dkc.append(dk_)
        dvc.append(dv_)
    argsb = (qb, kb, vb, do[1], sv["o_b"], sv["lse_b"])
    dqb = _causal_dq(*argsb, name=nm("attn_b_dq"), **_KW_B)
    dkb, dvb = _causal_dkv(*argsb, name=nm("attn_b_dkv"), **_KW_B)
    (doa_raw, dlse_a), (g["a_sink"],) = _rowmap_bwd(_f_sink, [sv["oa_raw"], sv["lse_a"]], [w["a_sink"]], [do[0]],
                                                   diff=[True, True], out_dtypes=[F32, F32], tb=256, name=nm("d_sink"))
    argsa = (qa, ka, zp["av"], doa_raw, sv["oa_raw"], sv["lse_a"], dlse_a)
    kwa = dict(S=S, dil=1, max_dist=A_WINDOW - 1, out_dtype=BF16, **_KW_A)
    dqa = _band_dq(*argsa, name=nm("attn_a_dq"), **kwa)
    dka, dva = _band_dkv(*argsa, name=nm("attn_a_dkv"), **kwa)
    acm_rows = [zp[k] for k in ("aq", "ak", "c0q", "c0k", "c0v", "c1q", "c1k", "c1v", "c2q", "c2k", "c2v", "mq")] + [cosA, sinA]
    acm_par = [w["a_qn"], w["a_kn"], w["c0q"], w["c0k"], w["c1q"], w["c1k"], w["c2q"], w["c2k"], w["m_qn"]]
    acm_ct = [dqa, dka, dqc[0], dkc[0], dvc[0], dqc[1], dkc[1], dvc[1], dqc[2], dkc[2], dvc[2], dmq]
    dacm, (g["a_qn"], g["a_kn"], g["c0q"], g["c0k"], g["c1q"], g["c1k"], g["c2q"], g["c2k"], g["m_qn"]) = _rowmap_bwd(
        _f_prep_acm, acm_rows, acm_par, acm_ct, diff=[True] * 12 + [False, False], out_dtypes=[BF16] * 12, tb=256,
        name=nm("d_prep_acm"))
    d_aq, d_ak, d_c0q, d_c0k, d_c0v, d_c1q, d_c1k, d_c1v, d_c2q, d_c2k, d_c2v, d_mq = dacm
    b_rows = [zp["ckv"], zp["cq"], zp["kr"], cosQ, sinQ, cosB, sinB]
    b_par = [w["b_qa"], w["b_kva"], w["uq"], w["ukv"], w["b_q"], w["b_k"], w["b_kr"]]
    (d_ckv, d_cq, d_kr), gb = _rowmap_bwd(_f_prep_b, b_rows, b_par, [dqb, dkb, dvb], diff=[True] * 3 + [False] * 4,
                                          out_dtypes=[BF16] * 3, tb=256, name=nm("d_prep_b"))
    g["b_qa"], g["b_kva"], g["uq"], g["ukv"], g["b_q"], g["b_k"], g["b_kr"] = gb
    (dmkv_k,), (g["m_kn"],) = _rowmap_bwd(_f_mem_k, [(mkv, 512, 0)], [w["m_kn"]], [dmk], diff=[True], out_dtypes=[F32],
                                          tb=256, name=nm("d_mem_k"))
    dmkv = jnp.concatenate([dmkv_k, dmv], axis=1)
    dmemn = _mm(dmkv, w["mkv"], tb=True, name=nm("d_mem_kv_x"))
    g["mkv"] = _mm(sv["memn"], dmkv, ta=True, name=nm("d_mem_kv_w"))
    _, (g["m_g_mem"],) = _rowmap_bwd(_f_norm, [mem], [w["m_g_mem"]], [dmemn], diff=[True], out_dtypes=[F32], tb=256,
                                     name=nm("d_mem_norm"))
    if early is not None:
        early(g)
    dz = jnp.concatenate([dgates, d_c0q, d_c0k, d_c0v, d_c1q, d_c1k, d_c1v, d_c2q, d_c2k, d_c2v, d_aq, d_mq, d_ckv, d_cq,
                          d_ak, dva, d_kr], axis=1)
    assert dz.shape[1] == zw
    dhn = _mm(dz, w["in"], tb=True, name=nm("d_in_x"), tk=1536)
    g["in"] = _mm(sv["hn"], dz, ta=True, name=nm("d_in_w"), tn=1536)
    (dx,), (g["g_mix"],) = _rowmap_bwd(_f_norm, [x], [w["g_mix"]], [dhn], diff=[True], out_dtypes=[F32], adds=[dx1],
                                       tb=256, name=nm("d_norm1"))
    return dx, g


_IN_ORIG = dict(aq=(0, 512), ak=(512, 640), av=(640, 768), cq=(768, 1152), ckv=(1152, 1408), kr=(1408, 1440),
                c=(1440, 6048), mq=(6048, 6560), gates=(6560, 10656))
_IN_OURS = dict(gates=(0, 4096), c=(4096, 8704), aq=(8704, 9216), mq=(9216, 9728), ckv=(9728, 9984), cq=(9984, 10368),
                ak=(10368, 10496), av=(10496, 10624), kr=(10624, 10656))
_IN_ORDER_ORIG = ("aq", "ak", "av", "cq", "ckv", "kr", "c", "mq", "gates")


def _in_to_ours(w_in):
    pc = {k: w_in[..., a:b] for k, (a, b) in _IN_ORIG.items()}
    zeros = lambda n: jnp.zeros(w_in.shape[:-1] + (n,), w_in.dtype)
    return jnp.concatenate([pc["gates"], pc["c"], pc["aq"], pc["mq"], pc["ckv"], pc["cq"], pc["ak"], pc["av"], pc["kr"],
                            zeros(96)], axis=-1)


def _in_from_ours(g_in):
    return jnp.concatenate([g_in[..., _IN_OURS[k][0]:_IN_OURS[k][1]] for k in _IN_ORDER_ORIG], axis=-1)


def _uq_to_ours(w):
    per = B_NOPE + B_ROPE
    w4 = w.reshape(w.shape[:-1] + (B_HEADS, per))
    w4 = jnp.pad(w4, [(0, 0)] * (w4.ndim - 1) + [(0, LANE - per)])
    return w4.reshape(w.shape[:-1] + (B_HEADS * LANE,))


def _uq_from_ours(g):
    per = B_NOPE + B_ROPE
    g4 = g.reshape(g.shape[:-1] + (B_HEADS, LANE))[..., :per]
    return g4.reshape(g.shape[:-1] + (B_HEADS * per,))


def _ukv_to_ours(w):
    w4 = w.reshape(w.shape[:-1] + (B_HEADS, B_NOPE + B_V))
    keys = jnp.pad(w4[..., :B_NOPE], [(0, 0)] * (w4.ndim - 1) + [(0, LANE - B_NOPE)])
    vals = w4[..., B_NOPE:]
    return jnp.concatenate([keys.reshape(w.shape[:-1] + (B_HEADS * LANE,)), vals.reshape(w.shape[:-1] + (B_HEADS * B_V,))],
                           axis=-1)


def _ukv_from_ours(g):
    wq = B_HEADS * LANE
    keys = g[..., :wq].reshape(g.shape[:-1] + (B_HEADS, LANE))[..., :B_NOPE]
    vals = g[..., wq:].reshape(g.shape[:-1] + (B_HEADS, B_V))
    return jnp.concatenate([keys, vals], axis=-1).reshape(g.shape[:-1] + (B_HEADS * (B_NOPE + B_V),))


def _layer_weights(big_l, small, l):
    row = lambda a: a[l].reshape(1, -1)
    w = dict(big_l)
    w.update(g_mix=row(small["g_mix"]), b_gate=row(small["b_gate"]), a_qn=row(small["a_qn"]), a_kn=row(small["a_kn"]),
             a_sink=row(small["a_sink"]), b_qa=row(small["b_qa_norm"]), b_kva=row(small["b_kva_norm"]),
             b_q=jnp.pad(small["b_qn"][l], (0, LANE - B_NOPE - B_ROPE)).reshape(1, -1),
             b_k=jnp.pad(small["b_kn"][l, :B_NOPE], (0, LANE - B_NOPE)).reshape(1, -1),
             b_kr=jnp.pad(small["b_kn"][l, B_NOPE:], (0, LANE - B_ROPE)).reshape(1, -1),
             m_g_mem=row(small["m_g_mem"]), m_qn=row(small["m_qn"]), m_kn=row(small["m_kn"]), g_mlp=row(small["g_mlp"]))
    for g in range(3):
        w[f"c{g}q"] = small["c_qn"][l, g].reshape(1, -1)
        w[f"c{g}k"] = small["c_kn"][l, g].reshape(1, -1)
    return w


def _rope_tables(positions):
    pos = positions.astype(F32)[:, None]
    tabs = []
    for dim in (A_HD, B_ROPE):
        inv = ROPE_THETA ** (-jnp.arange(0, dim, 2, dtype=F32) / dim)
        ang = pos * inv
        reps = LANE // (dim // 2)
        tabs += [jnp.tile(jnp.cos(ang), (1, reps)), jnp.tile(jnp.sin(ang), (1, reps))]
    half = B_ROPE // 2
    cb, sb = tabs[2][:, :half], tabs[3][:, :half]
    ones, zeros = jnp.ones((pos.shape[0], B_NOPE), F32), jnp.zeros((pos.shape[0], B_NOPE), F32)
    pad = LANE - B_NOPE - B_ROPE
    tabs.append(jnp.concatenate([ones, cb, cb, ones[:, :pad]], axis=1))
    tabs.append(jnp.concatenate([zeros, sb, sb, zeros[:, :pad]], axis=1))
    return tuple(tabs)


def _local_step(x, mem, positions, small, loss_target, get_big, put_grads, early_grads=None):
    depth = small["g_mix"].shape[0]
    tabs = _rope_tables(positions)
    ws, saved = [], []
    h = x
    for l in range(depth):
        first, late = get_big(l, h)
        h, sv, wl = _layer_fwd(l, h, mem, _layer_weights(first, small, l), tabs, late)
        ws.append(wl)
        saved.append(sv)
    loss, dh = _loss_and_grad(h, loss_target)
    small_grads = [None] * depth
    for l in reversed(range(depth)):
        dh, g = _layer_bwd(l, dh, mem, ws[l], tabs, saved[l], early_grads if l == 0 else None)
        zero = put_grads(l, g)
        if l > 0:
            ws[l - 1] = dict(ws[l - 1], g_mlp=ws[l - 1]["g_mlp"] + zero)
        small_grads[l] = g
    return loss, dh, small_grads


def _small_grads_to_reference_layout(grads):
    flat = lambda k: jnp.stack([g[k].reshape(-1) for g in grads])
    return dict(
        g_mix=flat("g_mix"), b_gate=flat("b_gate"), a_qn=flat("a_qn"), a_kn=flat("a_kn"), a_sink=flat("a_sink"),
        b_qa_norm=flat("b_qa"), b_kva_norm=flat("b_kva"), b_qn=flat("b_q")[:, :B_NOPE + B_ROPE],
        b_kn=jnp.concatenate([flat("b_k")[:, :B_NOPE], flat("b_kr")[:, :B_ROPE]], axis=1),
        c_qn=jnp.stack([jnp.stack([g[f"c{i}q"].reshape(-1) for i in range(3)]) for g in grads]),
        c_kn=jnp.stack([jnp.stack([g[f"c{i}k"].reshape(-1) for i in range(3)]) for g in grads]),
        m_g_mem=flat("m_g_mem"), m_qn=flat("m_qn"), m_kn=flat("m_kn"), g_mlp=flat("g_mlp"))


_BIG_GRAD = {"w_in": lambda g: _in_from_ours(g["in"]), "b_w_uq": lambda g: _uq_from_ours(g["uq"]),
             "b_w_ukv": lambda g: _ukv_from_ours(g["ukv"]), "m_w_kv": lambda g: g["mkv"], "w_branch": lambda g: jnp.stack(g["branch"]),
             "w_out": lambda g: g["out"], "w_up": lambda g: g["up"], "w_down": lambda g: g["down"]}


def _big_grads_to_reference_layout(g, names=None):
    return {k: _BIG_GRAD[k](g) for k in (_BIG_GRAD if names is None else names)}


def _big_to_kernel_layout(full):
    conv = {"w_in": ("in", _in_to_ours), "b_w_uq": ("uq", _uq_to_ours), "b_w_ukv": ("ukv", _ukv_to_ours),
            "m_w_kv": ("mkv", None), "w_branch": ("branch", None), "w_out": ("out", None), "w_up": ("up", None),
            "w_down": ("down", None)}
    out = {}
    for k, a in full.items():
        name, fn = conv[k]
        out[name] = (a if fn is None else fn(a)).astype(BF16)
    return out


MESH = pl.DeviceIdType.MESH
N_CHIPS = 4
N_DEV = 8
_ANY = pl.BlockSpec(memory_space=pl.ANY)


_HBM = pl.BlockSpec(memory_space=pltpu.HBM)
_SEM = pl.BlockSpec(memory_space=pltpu.SEMAPHORE)
_EFFECT = pltpu.SideEffectType.DATAFLOW_SIDE_EFFECTING


def _chip_peers():
    x, y, c = lax.axis_index("x"), lax.axis_index("y"), lax.axis_index("c")
    return 2 * x + y, [((1 - x, y, c), 2 * (1 - x) + y), ((x, 1 - y, c), 2 * x + 1 - y), ((1 - x, 1 - y, c), 2 * (1 - x) + 1 - y)]


def _exchange_start(srcs, lands, *, gather, name):
    n = len(srcs)

    def body(*refs):
        ins, land = refs[:n], refs[n:2 * n]
        send_sems, recv_sems = refs[2 * n], refs[2 * n + 1]
        token = refs[-1]
        me, peers = _chip_peers()
        for a in range(n):
            for j, (dev, chip) in enumerate(peers):
                src = ins[a] if gather else ins[a].at[chip]
                pltpu.make_async_remote_copy(src_ref=src, dst_ref=land[a].at[me], send_sem=send_sems.at[3 * a + j],
                                             recv_sem=recv_sems.at[3 * a + j], device_id=dev, device_id_type=MESH).start()
        token[...] = jnp.zeros(token.shape, token.dtype)

    hbm = lambda a: pltpu.HBM(a.shape, a.dtype)
    outs = _pcall(
        body, name=name,
        out_shape=(pltpu.SemaphoreType.DMA((3 * n,)), pltpu.SemaphoreType.DMA((3 * n,)), *[hbm(a) for a in srcs],
                   *[hbm(a) for a in lands], jax.ShapeDtypeStruct((8, LANE), F32)),
        in_specs=[_HBM] * (2 * n), out_specs=(_SEM, _SEM, *([_HBM] * (2 * n)), pl.BlockSpec(memory_space=pltpu.VMEM)),
        input_output_aliases={a: 2 + a for a in range(2 * n)},
        compiler_params=pltpu.CompilerParams(has_side_effects=_EFFECT),
    )(*[pltpu.with_memory_space_constraint(a, pltpu.HBM) for a in srcs],
      *[pltpu.with_memory_space_constraint(a, pltpu.HBM) for a in lands])
    return outs[0], outs[1], list(outs[2:2 + n]), list(outs[2 + n:2 + 2 * n]), outs[-1]


def _exchange_wait(state, after, *, gather, name):
    send_sems, recv_sems, srcs, lands, _ = state
    n = len(lands)

    def body(*refs):
        src_refs, land = refs[:n], refs[n:2 * n]
        ssem, rsem = refs[2 * n], refs[2 * n + 1]
        me, peers = _chip_peers()
        for a in range(n):
            for j, (dev, chip) in enumerate(peers):
                src = src_refs[a] if gather else src_refs[a].at[chip]
                cp = pltpu.make_async_remote_copy(src_ref=src, dst_ref=land[a].at[chip], send_sem=ssem.at[3 * a + j],
                                                  recv_sem=rsem.at[3 * a + j], device_id=dev, device_id_type=MESH)
                cp.wait_send()
                cp.wait_recv()

    outs = _pcall(
        body, name=name,
        out_shape=tuple(pltpu.HBM(a.shape, a.dtype) for a in list(srcs) + list(lands)),
        in_specs=[_HBM] * (2 * n) + [_SEM, _SEM, pl.BlockSpec(memory_space=pl.ANY)],
        out_specs=tuple([_HBM] * (2 * n)), input_output_aliases={a: a for a in range(2 * n)},
        compiler_params=pltpu.CompilerParams(has_side_effects=_EFFECT),
    )(*srcs, *lands, send_sems, recv_sems, after)
    return list(outs[:n]), list(outs[n:])


def _sibling_exchange(arrays, *, name):
    n = len(arrays)

    def body(*refs):
        ins, outs = refs[:n], refs[n:2 * n]
        send_sems, recv_sems = refs[2 * n:]
        x, y, c = lax.axis_index("x"), lax.axis_index("y"), lax.axis_index("c")
        cps = []
        for a in range(n):
            cp = pltpu.make_async_remote_copy(src_ref=ins[a], dst_ref=outs[a], send_sem=send_sems.at[a], recv_sem=recv_sems.at[a],
                                              device_id=(x, y, 1 - c), device_id_type=MESH)
            cp.start()
            cps.append(cp)
        for cp in cps:
            cp.wait()

    return _pcall(
        body, name=name, in_specs=[_ANY] * n, out_specs=[_ANY] * n,
        out_shape=[jax.ShapeDtypeStruct(a.shape, a.dtype) for a in arrays],
        scratch_shapes=[pltpu.SemaphoreType.DMA((n,)), pltpu.SemaphoreType.DMA((n,))],
        compiler_params=pltpu.CompilerParams(has_side_effects=True),
    )(*arrays)


def _allreduce_small(v, *, name):
    rows = v.shape[0]

    def body(v_ref, o_ref, slots, send_sems, recv_sems):
        x, y, c = lax.axis_index("x"), lax.axis_index("y"), lax.axis_index("c")
        me = 4 * x + 2 * y + c
        slots[me] = v_ref[...]
        cps = []
        for k in range(1, N_DEV):
            fx, fy, fc = (k >> 2) & 1, (k >> 1) & 1, k & 1
            peer = (x ^ fx, y ^ fy, c ^ fc)
            cp = pltpu.make_async_remote_copy(src_ref=v_ref, dst_ref=slots.at[me], send_sem=send_sems.at[k - 1],
                                              recv_sem=recv_sems.at[k - 1], device_id=peer, device_id_type=MESH)
            cp.start()
            cps.append((cp, peer))
        for k, (cp, (px, py, pc)) in enumerate(cps):
            pltpu.make_async_remote_copy(src_ref=v_ref, dst_ref=slots.at[4 * px + 2 * py + pc], send_sem=send_sems.at[k],
                                         recv_sem=recv_sems.at[k], device_id=(px, py, pc), device_id_type=MESH).wait_recv()
        for cp, _ in cps:
            cp.wait_send()
        tot = slots[0]
        for d in range(1, N_DEV):
            tot = tot + slots[d]
        o_ref[...] = tot

    vm = pl.BlockSpec(memory_space=pltpu.VMEM)
    return _pcall(
        body, name=name, in_specs=[vm], out_specs=vm, out_shape=jax.ShapeDtypeStruct(v.shape, F32),
        scratch_shapes=[pltpu.VMEM((N_DEV, rows, LANE), F32), pltpu.SemaphoreType.DMA((N_DEV - 1,)),
                        pltpu.SemaphoreType.DMA((N_DEV - 1,))],
        compiler_params=pltpu.CompilerParams(has_side_effects=True),
    )(v)


def _rows_block(rows, cols, itemsize=4, target_bytes=1 << 20):
    want = max(16, target_bytes // max(1, cols * itemsize))
    best = rows
    for t in range(16, rows, 16):
        if rows % t == 0 and t <= want:
            best = t
    return best if best <= want or rows <= want else rows


def _sum_slots(recvs, parts, me, *, name):
    nl = len(recvs)
    shp = recvs[0].shape[1:]
    r3 = [r.reshape(N_CHIPS, -1, shp[-1]) for r in recvs]
    p3 = [q.reshape(N_CHIPS, -1, shp[-1]) for q in parts]
    rows, cols = r3[0].shape[1:]
    tb = _rows_block(rows, cols)
    nblk = rows // tb
    per = N_CHIPS

    def body(me_ref, *refs):
        o_ref = refs[-1]
        lg = pl.program_id(0)
        for l in range(nl):
            r = refs[per * l:per * (l + 1)]

            @pl.when(lg == l)
            def _(r=r):
                tot = (r[0][...].astype(F32) + r[1][...].astype(F32)) + r[2][...].astype(F32)
                o_ref[...] = tot + r[3][...].astype(F32)

    def row(l, lg, i):
        return jnp.where(lg < l, 0, jnp.where(lg > l, nblk - 1, i))

    in_specs, args = [], []
    for l in range(nl):
        for k in range(1, N_CHIPS):
            in_specs.append(pl.BlockSpec((None, tb, cols), lambda lg, i, me_ref, l=l, k=k: (me_ref[0] ^ k, row(l, lg, i), 0)))
            args.append(r3[l])
        in_specs.append(pl.BlockSpec((None, tb, cols), lambda lg, i, me_ref, l=l: (me_ref[0], row(l, lg, i), 0)))
        args.append(p3[l])
    out = _pcall(
        body, name=name,
        grid_spec=pltpu.PrefetchScalarGridSpec(
            num_scalar_prefetch=1, grid=(nl, nblk), in_specs=in_specs,
            out_specs=pl.BlockSpec((None, tb, cols), lambda lg, i, me_ref: (lg, i, 0))),
        out_shape=jax.ShapeDtypeStruct((nl, rows, cols), F32), compiler_params=_cparams(("arbitrary", "arbitrary")),
    )(me, *args)
    return out.reshape((nl,) + shp)


def _adamw(w, g_parts, m, v, *, layer0=0, prev=None, name):
    shp = w.shape
    two = lambda a: a.reshape(-1, shp[-1])
    rows, cols = two(w).shape
    grows = two(g_parts[0]).shape[0]
    per_layer = rows // shp[0] if layer0 or prev is not None or grows != rows else rows
    tb = _rows_block(per_layer, cols, target_bytes=1 << 19)
    off = (layer0 * per_layer) // tb if per_layer != rows else 0
    npart = len(g_parts)
    nprev = 0 if prev is None else 4

    def body(*refs):
        w_ref = refs[0]
        gp = refs[1:1 + npart]
        m_ref, v_ref = refs[1 + npart:3 + npart]
        g_out, d_out, m_out, v_out = refs[3 + npart + nprev:]
        g = gp[0][...]
        for r in gp[1:]:
            g = g + r[...]
        wv = w_ref[...]
        m2 = ADAM_B1 * m_ref[...] + (1.0 - ADAM_B1) * g
        v2 = ADAM_B2 * v_ref[...] + (1.0 - ADAM_B2) * jnp.square(g)
        m_hat = m2 / (1.0 - ADAM_B1 ** ADAM_STEP)
        v_hat = v2 / (1.0 - ADAM_B2 ** ADAM_STEP)
        g_out[...] = g
        d_out[...] = -ADAM_LR * (m_hat / (jnp.sqrt(v_hat) + ADAM_EPS) + ADAM_WD * wv)
        m_out[...] = m2
        v_out[...] = v2

    wspec = pl.BlockSpec((tb, cols), lambda i: (i + off, 0))
    gspec = pl.BlockSpec((tb, cols), lambda i: (i, 0))
    in_specs = [wspec] + [gspec] * npart + [wspec, wspec] + [_ANY] * nprev
    args = [two(w)] + [two(p) for p in g_parts] + [two(m), two(v)] + ([two(a) for a in prev] if prev is not None else [])
    outs = _pcall(
        body, name=name, grid=(grows // tb,), in_specs=in_specs, out_specs=[wspec] * 4,
        out_shape=[jax.ShapeDtypeStruct((rows, cols), F32)] * 4,
        input_output_aliases={3 + npart + k: k for k in range(nprev)}, compiler_params=_cparams(("parallel",)),
    )(*args)
    return [o.reshape(shp) for o in outs]


BIG = ("w_in", "b_w_uq", "b_w_ukv", "m_w_kv", "w_branch", "w_out", "w_up", "w_down")
_SHARD_AXIS = dict(w_in=2, b_w_uq=2, b_w_ukv=2, m_w_kv=1, w_branch=3, w_out=1, w_up=2, w_down=1)
SMALL = ("g_mix", "b_gate", "a_qn", "a_kn", "a_sink", "b_qa_norm", "b_kva_norm", "b_qn", "b_kn", "c_qn", "c_kn",
         "m_g_mem", "m_qn", "m_kn", "g_mlp")
WEIGHTS = ("g_mix", "w_in", "b_gate", "a_qn", "a_kn", "a_sink", "b_qa_norm", "b_kva_norm", "b_w_uq", "b_w_ukv", "b_qn", "b_kn",
           "c_qn", "c_kn", "m_g_mem", "m_w_kv", "m_qn", "m_kn", "w_branch", "w_out", "g_mlp", "w_up", "w_down")


def _unshard(gathered, axis):
    moved = jnp.moveaxis(gathered, 0, axis)
    shp = list(gathered.shape[1:])
    shp[axis] *= N_CHIPS
    return moved.reshape(shp)


def _shard_parts(full, axis):
    shp = list(full.shape)
    shp[axis:axis + 1] = [N_CHIPS, shp[axis] // N_CHIPS]
    return jnp.moveaxis(full.reshape(shp), axis, 0)


def _pack_small(d):
    flat = jnp.concatenate([d[k].reshape(-1).astype(F32) for k in SMALL])
    n = flat.shape[0]
    pad = (-n) % (8 * LANE)
    return jnp.pad(flat, (0, pad)).reshape(-1, LANE)


def _unpack_small(packed, like):
    flat = packed.reshape(-1)
    out, off = {}, 0
    for k in SMALL:
        n = int(np.prod(like[k].shape))
        out[k] = flat[off:off + n].reshape(like[k].shape)
        off += n
    return out


def _train_step(x, mem, positions, loss_target, w, m, v):
    depth = w["g_mix"].shape[0]
    me = 2 * lax.axis_index("x") + lax.axis_index("y")
    landing = lambda a: lax.empty((N_CHIPS,) + a.shape, a.dtype)
    def with_own(land, mine):
        slot = lax.broadcasted_iota(jnp.int32, (N_CHIPS,) + (1,) * mine.ndim, 0)
        return jnp.where(slot == me, mine[None], land)

    gathers = {}
    for l in range(depth):
        for names in ((BIG[:1], BIG[1:]) if l == 0 else (BIG,)):
            own = [w[k][l].astype(BF16) for k in names]
            tag = f"{l}" if len(names) == len(BIG) else f"{l}_{names[0]}"
            gathers[(l, names)] = _exchange_start(own, [landing(a) for a in own], gather=True, name=f"gather_start{tag}")

    def gathered(l, names, after):
        tag = f"{l}" if len(names) == len(BIG) else f"{l}_{names[0]}"
        mine, lands = _exchange_wait(gathers[(l, names)], after, gather=True, name=f"gather_wait{tag}")
        full = {k: _unshard(with_own(g, o), _SHARD_AXIS[k] - 1) for k, g, o in zip(names, lands, mine)}
        return {kk: vv for kk, vv in _big_to_kernel_layout(full).items()}

    def get_big(l, after):
        if l == 0:
            return gathered(0, BIG[:1], after), lambda later: gathered(0, BIG[1:], later)
        return gathered(l, BIG, after), None

    scatters = {}

    def scatter(l, g, names, tag):
        gref = _big_grads_to_reference_layout(g, names)
        parts = [_shard_parts(gref[k], _SHARD_AXIS[k] - 1).astype(BF16) for k in names]
        scatters[(l, names)] = _exchange_start(parts, [landing(p[0]) for p in parts], gather=False, name=f"scatter_start{tag}")
        return scatters[(l, names)][4]

    last_names = (BIG[1:], BIG[:1]) if depth > 1 else (BIG,)

    def early_grads(g):
        if len(last_names) > 1:
            scatter(0, g, last_names[0], "0_rest")

    def put_grads(l, g):
        names = last_names[-1] if l == 0 else BIG
        return scatter(l, g, names, f"{l}")[:1, :1]

    small = {k: w[k] for k in SMALL}
    loss, gx, grads = _local_step(x[0], mem[0], positions[0], small, loss_target[0], get_big, put_grads, early_grads)
    loss = lax.psum(loss, ("x", "y", "c"))
    me1 = me.reshape(1).astype(jnp.int32)
    res = {k: None for k in BIG}
    after = scatters[(0, last_names[-1])][4]
    for lo, n in ([(1, depth - 1), (0, 1)] if depth > 1 else [(0, 1)]):
        parts, recv = [], []
        for l in range(lo, lo + n):
            pl_, rl_ = {}, {}
            for names in ((BIG,) if l > 0 else last_names):
                tag = f"{l}" if (l > 0 or names == last_names[-1]) else "0_rest"
                ps_, rs_ = _exchange_wait(scatters[(l, names)], after, gather=False, name=f"scatter_wait{tag}")
                pl_.update(zip(names, ps_))
                rl_.update(zip(names, rs_))
            parts.append([pl_[k] for k in BIG])
            recv.append([rl_[k] for k in BIG])
        mine = [_sum_slots([recv[l][a] for l in range(n)], [parts[l][a] for l in range(n)], me1, name=f"sum_{k}_{lo}")
                for a, k in enumerate(BIG)]
        theirs = _sibling_exchange(mine, name=f"sibling_grads{lo}")
        for k, p, q in zip(BIG, mine, theirs):
            res[k] = _adamw(w[k], [p, q], m[k], v[k], layer0=lo, prev=res[k], name=f"adamw_{k}_{lo}")
        after = res[BIG[-1]][0]
    gsmall = _small_grads_to_reference_layout(grads)
    g_small = _allreduce_small(_pack_small(gsmall), name="allreduce_small")
    packed = _adamw(_pack_small(small), [g_small], _pack_small({k: m[k] for k in SMALL}), _pack_small({k: v[k] for k in SMALL}),
                    name="adamw_small")
    unpacked = [_unpack_small(p, small) for p in packed]
    for k in SMALL:
        res[k] = [u[k] for u in unpacked]
    outs = [loss, gx[None]]
    for i in range(4):
        outs += [res[k][i] for k in WEIGHTS]
    return tuple(outs)

def kernel(x, mem, positions, g_mix, w_in, b_gate, a_qn, a_kn, a_sink, b_qa_norm, b_kva_norm, b_w_uq, b_w_ukv, b_qn, b_kn, c_qn, c_kn, m_g_mem, m_w_kv, m_qn, m_kn, w_branch, w_out, g_mlp, w_up, w_down, loss_target, m_g_mix, m_w_in, m_b_gate, m_a_qn, m_a_kn, m_a_sink, m_b_qa_norm, m_b_kva_norm, m_b_w_uq, m_b_w_ukv, m_b_qn, m_b_kn, m_c_qn, m_c_kn, m_m_g_mem, m_m_w_kv, m_m_qn, m_m_kn, m_w_branch, m_w_out, m_g_mlp, m_w_up, m_w_down, v_g_mix, v_w_in, v_b_gate, v_a_qn, v_a_kn, v_a_sink, v_b_qa_norm, v_b_kva_norm, v_b_w_uq, v_b_w_ukv, v_b_qn, v_b_kn, v_c_qn, v_c_kn, v_m_g_mem, v_m_w_kv, v_m_qn, v_m_kn, v_w_branch, v_w_out, v_g_mlp, v_w_up, v_w_down):
    args = dict(locals())
    w = {k: args[k] for k in WEIGHTS}
    m = {k: args["m_" + k] for k in WEIGHTS}
    v = {k: args["v_" + k] for k in WEIGHTS}
    return _train_step(x, mem, positions, loss_target, w, m, v)
```

```python
import functools
import math

import jax
import jax.numpy as jnp
import numpy as np
from jax import lax
from jax.experimental import pallas as pl
from jax.experimental.pallas import tpu as pltpu

F32 = jnp.float32
BF16 = jnp.bfloat16

DEPTH = 4
BLOCK = 128
ROPE_THETA = 10000.0
EPS = 1e-6
NEG = -1e30
A_HEADS, A_KV_HEADS, A_HD, A_WINDOW = 8, 2, 64, 128
B_HEADS, B_Q_LORA, B_KV_LORA, B_NOPE, B_ROPE, B_V = 8, 384, 256, 64, 32, 64
C_PATTERNS = ((128, 1), (512, 4), (2048, 16))
C_HEADS, C_HD = 8, 64
M_HEADS, M_HD = 4, 128
N_BRANCH, BRANCH_W = 4, 512
ADAM_LR, ADAM_B1, ADAM_B2, ADAM_EPS, ADAM_WD, ADAM_STEP = 0.001, 0.9, 0.999, 1e-08, 0.01, 10

LANE = 128
VMEM_LIMIT = 56 * 1024 * 1024


def _pcall(body, **kw):
    return pl.pallas_call(body, **kw)


def _cparams(sem):
    return pltpu.CompilerParams(dimension_semantics=sem, vmem_limit_bytes=VMEM_LIMIT)


def _tile(n, target):
    if n <= target:
        return n
    best = None
    for t in range(LANE, target + 1, LANE):
        if n % t == 0:
            best = t
    return best if best is not None else n


def _mm(a, b, *, ta=False, tb=False, out_dtype=F32, pro_a=None, epi=None, extra=None, name,
        tm=1024, tn=1024, tk=1024):
    if ta:
        K, M = a.shape
    else:
        M, K = a.shape
    if tb:
        N, K2 = b.shape
    else:
        K2, N = b.shape
    assert K == K2, (a.shape, b.shape, ta, tb)
    tm, tn, tk = _tile(M, tm), _tile(N, tn), _tile(K, tk)
    nk = K // tk
    a_spec = pl.BlockSpec((tk, tm), lambda i, j, k: (k, i)) if ta else pl.BlockSpec((tm, tk), lambda i, j, k: (i, k))
    b_spec = pl.BlockSpec((tn, tk), lambda i, j, k: (j, k)) if tb else pl.BlockSpec((tk, tn), lambda i, j, k: (k, j))
    o_spec = pl.BlockSpec((tm, tn), lambda i, j, k: (i, j))
    dims = (((0,) if ta else (1,), (1,) if tb else (0,)), ((), ()))
    has_extra = extra is not None

    def body(*refs):
        if has_extra:
            a_ref, b_ref, e_ref, o_ref, acc_ref = refs
        else:
            a_ref, b_ref, o_ref, acc_ref = refs
            e_ref = None
        k = pl.program_id(2)
        av = a_ref[...]
        if pro_a is not None:
            av = pro_a(av.astype(F32))
        part = lax.dot_general(av.astype(BF16), b_ref[...].astype(BF16), dims, preferred_element_type=F32)

        @pl.when(k == 0)
        def _():
            acc_ref[...] = part

        @pl.when(k > 0)
        def _():
            acc_ref[...] += part

        @pl.when(k == nk - 1)
        def _():
            r = acc_ref[...]
            if epi is not None:
                r = epi(r, e_ref[...]) if has_extra else epi(r)
            o_ref[...] = r.astype(out_dtype)

    in_specs = [a_spec, b_spec] + ([o_spec] if has_extra else [])
    args = (a, b) + ((extra,) if has_extra else ())
    return _pcall(
        body, name=name, grid=(M // tm, N // tn, nk), in_specs=in_specs, out_specs=o_spec,
        out_shape=jax.ShapeDtypeStruct((M, N), out_dtype),
        scratch_shapes=[pltpu.VMEM((tm, tn), F32)],
        compiler_params=_cparams(("parallel", "parallel", "arbitrary")),
    )(*args)


def _piece(p):
    if isinstance(p, tuple):
        return p
    return (p, p.shape[1], 0)


def _row_spec(width, idx, tb):
    return pl.BlockSpec((tb, width), lambda i, idx=idx: (i, idx))


def _full_spec(arr):
    nd = arr.ndim
    return pl.BlockSpec(arr.shape, lambda i, nd=nd: (0,) * nd)


def _rowmap(f, rows, params, outs, *, tb, name):
    rows = [_piece(p) for p in rows]
    R = rows[0][0].shape[0]
    tb = min(tb, R)
    nr, npar, nout = len(rows), len(params), len(outs)

    def body(*refs):
        rv = [r[...] for r in refs[:nr]]
        pv = [r[...] for r in refs[nr:nr + npar]]
        res = f(*rv, *pv)
        for o_ref, val in zip(refs[nr + npar:], res):
            o_ref[...] = val.astype(o_ref.dtype)

    return _pcall(
        body, name=name, grid=(R // tb,),
        in_specs=[_row_spec(w, idx, tb) for (_, w, idx) in rows] + [_full_spec(p) for p in params],
        out_specs=[_row_spec(w, 0, tb) for (w, _) in outs],
        out_shape=[jax.ShapeDtypeStruct((R, w), dt) for (w, dt) in outs],
        compiler_params=_cparams(("parallel",)),
    )(*[r[0] for r in rows], *params)


def _rowmap_bwd(f, rows, params, couts, *, diff, out_dtypes, adds=None, tb, name):
    rows = [_piece(p) for p in rows]
    couts = [_piece(p) for p in couts]
    R = rows[0][0].shape[0]
    tb = min(tb, R)
    nr, npar, nc = len(rows), len(params), len(couts)
    didx = [i for i, d in enumerate(diff) if d]
    adds = [None] * len(didx) if adds is None else adds
    add_ops = [_piece(a) for a in adds if a is not None]
    na = len(add_ops)

    def body(*refs):
        rv = [r[...] for r in refs[:nr]]
        pv = [r[...] for r in refs[nr:nr + npar]]
        cv = [r[...] for r in refs[nr + npar:nr + npar + nc]]
        av = [r[...] for r in refs[nr + npar + nc:nr + npar + nc + na]]
        o_refs = refs[nr + npar + nc + na:]
        drow_refs, dpar_refs = o_refs[:len(didx)], o_refs[len(didx):]
        nondiff = {i: rv[i] for i in range(nr) if not diff[i]}

        def g(*dv):
            full = []
            it = iter(dv[:len(didx)])
            for i in range(nr):
                full.append(nondiff[i] if i in nondiff else next(it))
            return f(*full, *dv[len(didx):])

        res, vjp = jax.vjp(g, *[rv[i].astype(F32) for i in didx], *pv)
        cts = tuple(c.astype(r.dtype) for c, r in zip(cv, res))
        grads = vjp(cts)
        ai = 0
        for n, o_ref in enumerate(drow_refs):
            val = grads[n]
            if adds[n] is not None:
                val = val + av[ai].astype(F32)
                ai += 1
            o_ref[...] = val.astype(o_ref.dtype)
        first = pl.program_id(0) == 0
        for n, o_ref in enumerate(dpar_refs):
            gp = grads[len(didx) + n].astype(F32)

            @pl.when(first)
            def _(o_ref=o_ref, gp=gp):
                o_ref[...] = gp

            @pl.when(jnp.logical_not(first))
            def _(o_ref=o_ref, gp=gp):
                o_ref[...] += gp

    outs = _pcall(
        body, name=name, grid=(R // tb,),
        in_specs=([_row_spec(w, idx, tb) for (_, w, idx) in rows] + [_full_spec(p) for p in params]
                  + [_row_spec(w, idx, tb) for (_, w, idx) in couts] + [_row_spec(w, idx, tb) for (_, w, idx) in add_ops]),
        out_specs=[_row_spec(rows[i][1], 0, tb) for i in didx] + [_full_spec(p) for p in params],
        out_shape=([jax.ShapeDtypeStruct((R, rows[i][1]), dt) for i, dt in zip(didx, out_dtypes)]
                   + [jax.ShapeDtypeStruct(p.shape, F32) for p in params]),
        compiler_params=_cparams(("arbitrary",)),
    )(*[r[0] for r in rows], *params, *[c[0] for c in couts], *[a[0] for a in add_ops])
    return outs[:len(didx)], outs[len(didx):]


@functools.partial(jax.custom_vjp, nondiff_argnums=(1,))
def _lane_roll(x, shift):
    return pltpu.roll(x, shift % x.shape[-1], axis=x.ndim - 1)


def _lane_roll_fwd(x, shift):
    return _lane_roll(x, shift), None


def _lane_roll_bwd(shift, _, g):
    return (_lane_roll(g, -shift),)


_lane_roll.defvjp(_lane_roll_fwd, _lane_roll_bwd)


def _group_matrix(kind):
    r = lax.broadcasted_iota(jnp.int32, (LANE, LANE), 0)
    c = lax.broadcasted_iota(jnp.int32, (LANE, LANE), 1)
    if kind == "mla":
        gid = lambda l: jnp.where(l < B_NOPE, 0, jnp.where(l < B_NOPE + B_ROPE, 1, 2))
        inv = jnp.where(c < B_NOPE, 1.0 / B_NOPE, 1.0 / B_ROPE)
        return jnp.where(gid(r) == gid(c), inv, 0.0).astype(BF16)
    return jnp.where(r // kind == c // kind, 1.0 / kind, 0.0).astype(BF16)


@functools.partial(jax.custom_vjp, nondiff_argnums=(1,))
def _group_mean(xx, kind):
    gm = _group_matrix(kind)
    outs = []
    for b in range(xx.shape[-1] // LANE):
        t = xx[:, b * LANE:(b + 1) * LANE]
        hi = t.astype(BF16)
        lo = (t - hi.astype(F32)).astype(BF16)
        outs.append(jnp.dot(hi, gm, preferred_element_type=F32) + jnp.dot(lo, gm, preferred_element_type=F32))
    return jnp.concatenate(outs, axis=1) if len(outs) > 1 else outs[0]


def _group_mean_fwd(xx, kind):
    return _group_mean(xx, kind), None


def _group_mean_bwd(kind, _, g):
    return (_group_mean(g, kind),)


_group_mean.defvjp(_group_mean_fwd, _group_mean_bwd)


def _exact_dot(x, m):
    return jnp.dot(x, m, precision=lax.Precision.HIGHEST, preferred_element_type=F32)


def _head_norm(x, gain_tiled, group):
    return x * lax.rsqrt(_group_mean(x * x, group) + EPS) * gain_tiled


def _row_norm(x, gain):
    ms = jnp.mean(x * x, axis=-1, keepdims=True)
    return x * lax.rsqrt(ms + EPS) * gain


def _rope(x, cos, sin, hd):
    half = hd // 2
    lane = lax.broadcasted_iota(jnp.int32, x.shape, x.ndim - 1) % hd
    other = jnp.where(lane < half, -_lane_roll(x, -half), _lane_roll(x, half))
    return x * cos + other * sin


class _AttnGeom:
    def __init__(self, mode, lq, lk, max_dist=0):
        self.mode, self.lq, self.lk, self.max_dist = mode, lq, lk, max_dist
        if mode == "band":
            self.bq = self.bk = BLOCK
            self.nt_q = 2
            self.nt_k = 2
        elif mode == "causal":
            self.bq = self.bk = min(256, lq)
            self.nt_q = lk // self.bk
            self.nt_k = lq // self.bq
        else:
            self.bq = min(512, lq)
            self.bk = lk
            self.nt_q = 1
            self.nt_k = lq // self.bq
        self.nq, self.nk = lq // self.bq, lk // self.bk

    def kv_block(self, i, t):
        if self.mode == "band":
            return jnp.maximum(i - t, 0)
        if self.mode == "causal":
            return jnp.minimum(t, i)
        return 0 * i

    def kv_active(self, i, t):
        if self.mode == "band":
            return i - t >= 0
        if self.mode == "causal":
            return t <= i
        return None

    def q_block(self, j, t):
        if self.mode == "band":
            return jnp.minimum(j + t, self.nq - 1)
        if self.mode == "causal":
            return jnp.maximum(t, j)
        return t

    def q_active(self, j, t):
        if self.mode == "band":
            return j + t <= self.nq - 1
        if self.mode == "causal":
            return t >= j
        return None

    def mask(self, qb, kb):
        if self.mode == "full":
            return None
        qp = qb * self.bq + lax.broadcasted_iota(jnp.int32, (self.bq, self.bk), 0)
        kp = kb * self.bk + lax.broadcasted_iota(jnp.int32, (self.bq, self.bk), 1)
        d = qp - kp
        if self.mode == "band":
            return (d >= 0) & (d <= self.max_dist)
        return d >= 0


def _when(cond, fn):
    if cond is None:
        fn()
    else:
        pl.when(cond)(fn)


def _dil_view(p, dil):
    arr, w, idx = _piece(p)
    R, C = arr.shape
    assert C % w == 0, (C, w)
    return arr.reshape(R // dil, dil * C), w, idx, C // w


def _seq_spec(view, rows, blk_fn):
    _, w, idx, cpw = view
    return pl.BlockSpec((rows, w), lambda s, i, t: (blk_fn(i, t), s * cpw + idx))


_NT = (((1,), (1,)), ((), ()))
_TN = (((0,), (0,)), ((), ()))


def _scores(geom, scale, q, k, qp, kp, h, g, hd, rope, qb, kb):
    s = lax.dot_general(q[:, h * hd:(h + 1) * hd], k[:, g * hd:(g + 1) * hd], _NT, preferred_element_type=F32)
    if rope:
        s = s + lax.dot_general(qp[:, h * rope:(h + 1) * rope], kp[:, :rope], _NT, preferred_element_type=F32)
    s = s * scale
    m = geom.mask(qb, kb)
    return s, m


def _attn_fwd(geom, q, k, v, *, hq, hk, hd, hdv, scale, dil=1, qp=None, kp=None, rope=0, name):
    qv, kv, vv = _dil_view(q, dil), _dil_view(k, dil), _dil_view(v, dil)
    R = _piece(q)[0].shape[0]
    grp = hq // hk
    bq, bk, nt = geom.bq, geom.bk, geom.nt_q
    ops = [qv, kv, vv]
    specs = [_seq_spec(qv, bq, lambda i, t: i), _seq_spec(kv, bk, geom.kv_block), _seq_spec(vv, bk, geom.kv_block)]
    if rope:
        qpv, kpv = _dil_view(qp, dil), _dil_view(kp, dil)
        ops += [qpv, kpv]
        specs += [_seq_spec(qpv, bq, lambda i, t: i), _seq_spec(kpv, bk, geom.kv_block)]
    ow = hq * hdv
    o_view = (None, ow, 0, 1)
    o_spec = pl.BlockSpec((bq, ow), lambda s, i, t: (i, s))

    def body(*refs):
        if rope:
            q_ref, k_ref, v_ref, qp_ref, kp_ref, o_ref, lse_ref, m_sc, l_sc, acc_sc = refs
        else:
            q_ref, k_ref, v_ref, o_ref, lse_ref, m_sc, l_sc, acc_sc = refs
            qp_ref = kp_ref = None
        i, t = pl.program_id(1), pl.program_id(2)

        @pl.when(t == 0)
        def _():
            m_sc[...] = jnp.full(m_sc.shape, NEG, F32)
            l_sc[...] = jnp.zeros(l_sc.shape, F32)
            acc_sc[...] = jnp.zeros(acc_sc.shape, F32)

        def step():
            qa, ka, va = q_ref[...].astype(BF16), k_ref[...].astype(BF16), v_ref[...].astype(BF16)
            qpa = qp_ref[...].astype(BF16) if rope else None
            kpa = kp_ref[...].astype(BF16) if rope else None
            kb = geom.kv_block(i, t)
            for h in range(hq):
                g = h // grp
                s, msk = _scores(geom, scale, qa, ka, qpa, kpa, h, g, hd, rope, i, kb)
                if msk is not None:
                    s = jnp.where(msk, s, NEG)
                m_old = m_sc[h]
                m_new = jnp.maximum(m_old, jnp.max(s, axis=1, keepdims=True))
                p = jnp.exp(s - m_new)
                alpha = jnp.exp(m_old - m_new)
                l_sc[h] = alpha * l_sc[h] + jnp.sum(p, axis=1, keepdims=True)
                pv = jnp.dot(p.astype(BF16), va[:, g * hdv:(g + 1) * hdv], preferred_element_type=F32)
                acc_sc[h] = alpha * acc_sc[h] + pv
                m_sc[h] = m_new

        _when(geom.kv_active(i, t), step)

        @pl.when(t == nt - 1)
        def _():
            for h in range(hq):
                l = l_sc[h]
                o_ref[:, h * hdv:(h + 1) * hdv] = acc_sc[h] / l
                lse_ref[:, h * hdv:(h + 1) * hdv] = jnp.broadcast_to(m_sc[h] + jnp.log(l), (bq, hdv))

    o, lse = _pcall(
        body, name=name, grid=(dil, geom.nq, nt), in_specs=specs, out_specs=[o_spec, o_spec],
        out_shape=[jax.ShapeDtypeStruct((R // dil, dil * ow), F32)] * 2,
        scratch_shapes=[pltpu.VMEM((hq, bq, 1), F32), pltpu.VMEM((hq, bq, 1), F32), pltpu.VMEM((hq, bq, hdv), F32)],
        compiler_params=_cparams(("parallel", "parallel", "arbitrary")),
    )(*[o_[0] for o_ in ops])
    return o.reshape(R, ow), lse.reshape(R, ow)


def _attn_dq(geom, q, k, v, do, o, lse, dlse, *, hq, hk, hd, hdv, scale, dil=1, qp=None, kp=None, rope=0,
             out_dtype=F32, name):
    qv, kv, vv = _dil_view(q, dil), _dil_view(k, dil), _dil_view(v, dil)
    dov, ov, lv = _dil_view(do, dil), _dil_view(o, dil), _dil_view(lse, dil)
    R = _piece(q)[0].shape[0]
    grp = hq // hk
    bq, bk, nt = geom.bq, geom.bk, geom.nt_q
    qi = lambda i, t: i
    ops = [qv, kv, vv, dov, ov, lv]
    specs = [_seq_spec(qv, bq, qi), _seq_spec(kv, bk, geom.kv_block), _seq_spec(vv, bk, geom.kv_block),
             _seq_spec(dov, bq, qi), _seq_spec(ov, bq, qi), _seq_spec(lv, bq, qi)]
    has_dl = dlse is not None
    if has_dl:
        dlv = _dil_view(dlse, dil)
        ops.append(dlv)
        specs.append(_seq_spec(dlv, bq, qi))
    if rope:
        qpv, kpv = _dil_view(qp, dil), _dil_view(kp, dil)
        ops += [qpv, kpv]
        specs += [_seq_spec(qpv, bq, qi), _seq_spec(kpv, bk, geom.kv_block)]
    qw = hq * hd
    out_specs = [pl.BlockSpec((bq, qw), lambda s, i, t: (i, s))]
    out_shape = [jax.ShapeDtypeStruct((R // dil, dil * qw), out_dtype)]
    scratch = [pltpu.VMEM((hq, bq, 1), F32), pltpu.VMEM((hq, bq, hd), F32)]
    if rope:
        out_specs.append(pl.BlockSpec((bq, hq * rope), lambda s, i, t: (i, s)))
        out_shape.append(jax.ShapeDtypeStruct((R // dil, dil * hq * rope), out_dtype))
        scratch.append(pltpu.VMEM((hq, bq, rope), F32))

    def body(*refs):
        refs = list(refs)
        q_ref, k_ref, v_ref, do_ref, o_ref, l_ref = refs[:6]
        pos = 6
        dl_ref = None
        if has_dl:
            dl_ref = refs[pos]
            pos += 1
        qp_ref = kp_ref = None
        if rope:
            qp_ref, kp_ref = refs[pos:pos + 2]
            pos += 2
        dq_ref = refs[pos]
        pos += 1
        dqp_ref = None
        if rope:
            dqp_ref = refs[pos]
            pos += 1
        dl_sc, dq_sc = refs[pos:pos + 2]
        dqp_sc = refs[pos + 2] if rope else None
        i, t = pl.program_id(1), pl.program_id(2)

        @pl.when(t == 0)
        def _():
            dov_, ov_ = do_ref[...].astype(F32), o_ref[...].astype(F32)
            prod = dov_ * ov_
            for h in range(hq):
                d = jnp.sum(prod[:, h * hdv:(h + 1) * hdv], axis=1, keepdims=True)
                if has_dl:
                    d = d - jnp.sum(dl_ref[:, h * hdv:(h + 1) * hdv].astype(F32), axis=1, keepdims=True)
                dl_sc[h] = d
            dq_sc[...] = jnp.zeros(dq_sc.shape, F32)
            if rope:
                dqp_sc[...] = jnp.zeros(dqp_sc.shape, F32)

        def step():
            qa, ka, va = q_ref[...].astype(BF16), k_ref[...].astype(BF16), v_ref[...].astype(BF16)
            doa = do_ref[...].astype(BF16)
            qpa = qp_ref[...].astype(BF16) if rope else None
            kpa = kp_ref[...].astype(BF16) if rope else None
            kb = geom.kv_block(i, t)
            for h in range(hq):
                g = h // grp
                s, msk = _scores(geom, scale, qa, ka, qpa, kpa, h, g, hd, rope, i, kb)
                p = jnp.exp(s - l_ref[:, h * hdv:h * hdv + 1])
                if msk is not None:
                    p = jnp.where(msk, p, 0.0)
                dp = lax.dot_general(doa[:, h * hdv:(h + 1) * hdv], va[:, g * hdv:(g + 1) * hdv], _NT,
                                     preferred_element_type=F32)
                ds = (p * (dp - dl_sc[h]) * scale).astype(BF16)
                dq_sc[h] += jnp.dot(ds, ka[:, g * hd:(g + 1) * hd], preferred_element_type=F32)
                if rope:
                    dqp_sc[h] += jnp.dot(ds, kpa[:, :rope], preferred_element_type=F32)

        _when(geom.kv_active(i, t), step)

        @pl.when(t == nt - 1)
        def _():
            for h in range(hq):
                dq_ref[:, h * hd:(h + 1) * hd] = dq_sc[h].astype(dq_ref.dtype)
                if rope:
                    dqp_ref[:, h * rope:(h + 1) * rope] = dqp_sc[h].astype(dqp_ref.dtype)

    outs = _pcall(
        body, name=name, grid=(dil, geom.nq, nt), in_specs=specs, out_specs=out_specs, out_shape=out_shape,
        scratch_shapes=scratch, compiler_params=_cparams(("parallel", "parallel", "arbitrary")),
    )(*[o_[0] for o_ in ops])
    dq = outs[0].reshape(R, qw)
    if rope:
        return dq, outs[1].reshape(R, hq * rope)
    return dq


def _attn_dkv(geom, q, k, v, do, o, lse, dlse, *, hq, hk, hd, hdv, scale, dil=1, qp=None, kp=None, rope=0,
              out_dtype=F32, name):
    qv, kv, vv = _dil_view(q, dil), _dil_view(k, dil), _dil_view(v, dil)
    dov, ov, lv = _dil_view(do, dil), _dil_view(o, dil), _dil_view(lse, dil)
    Rk = _piece(k)[0].shape[0]
    grp = hq // hk
    bq, bk, nt = geom.bq, geom.bk, geom.nt_k
    kj = lambda j, t: j
    ops = [qv, kv, vv, dov, ov, lv]
    specs = [_seq_spec(qv, bq, geom.q_block), _seq_spec(kv, bk, kj), _seq_spec(vv, bk, kj),
             _seq_spec(dov, bq, geom.q_block), _seq_spec(ov, bq, geom.q_block), _seq_spec(lv, bq, geom.q_block)]
    has_dl = dlse is not None
    if has_dl:
        dlv = _dil_view(dlse, dil)
        ops.append(dlv)
        specs.append(_seq_spec(dlv, bq, geom.q_block))
    if rope:
        qpv, kpv = _dil_view(qp, dil), _dil_view(kp, dil)
        ops += [qpv, kpv]
        specs += [_seq_spec(qpv, bq, geom.q_block), _seq_spec(kpv, bk, kj)]
    kw, vw = hk * hd, hk * hdv
    out_specs = [pl.BlockSpec((bk, kw), lambda s, j, t: (j, s)), pl.BlockSpec((bk, vw), lambda s, j, t: (j, s))]
    out_shape = [jax.ShapeDtypeStruct((Rk // dil, dil * kw), out_dtype), jax.ShapeDtypeStruct((Rk // dil, dil * vw), out_dtype)]
    scratch = [pltpu.VMEM((hk, bk, hd), F32), pltpu.VMEM((hk, bk, hdv), F32)]
    if rope:
        out_specs.append(pl.BlockSpec((bk, LANE), lambda s, j, t: (j, s)))
        out_shape.append(jax.ShapeDtypeStruct((Rk // dil, dil * LANE), out_dtype))
        scratch.append(pltpu.VMEM((bk, rope), F32))

    def body(*refs):
        refs = list(refs)
        q_ref, k_ref, v_ref, do_ref, o_ref, l_ref = refs[:6]
        pos = 6
        dl_ref = None
        if has_dl:
            dl_ref = refs[pos]
            pos += 1
        qp_ref = kp_ref = None
        if rope:
            qp_ref, kp_ref = refs[pos:pos + 2]
            pos += 2
        dk_ref, dv_ref = refs[pos:pos + 2]
        pos += 2
        dkp_ref = None
        if rope:
            dkp_ref = refs[pos]
            pos += 1
        dk_sc, dv_sc = refs[pos:pos + 2]
        dkp_sc = refs[pos + 2] if rope else None
        j, t = pl.program_id(1), pl.program_id(2)

        @pl.when(t == 0)
        def _():
            dk_sc[...] = jnp.zeros(dk_sc.shape, F32)
            dv_sc[...] = jnp.zeros(dv_sc.shape, F32)
            if rope:
                dkp_sc[...] = jnp.zeros(dkp_sc.shape, F32)

        def step():
            qa, ka, va = q_ref[...].astype(BF16), k_ref[...].astype(BF16), v_ref[...].astype(BF16)
            dof = do_ref[...].astype(F32)
            doa = dof.astype(BF16)
            prod = dof * o_ref[...].astype(F32)
            qpa = qp_ref[...].astype(BF16) if rope else None
            kpa = kp_ref[...].astype(BF16) if rope else None
            qb = geom.q_block(j, t)
            for h in range(hq):
                g = h // grp
                s, msk = _scores(geom, scale, qa, ka, qpa, kpa, h, g, hd, rope, qb, j)
                p = jnp.exp(s - l_ref[:, h * hdv:h * hdv + 1])
                if msk is not None:
                    p = jnp.where(msk, p, 0.0)
                delta = jnp.sum(prod[:, h * hdv:(h + 1) * hdv], axis=1, keepdims=True)
                if has_dl:
                    delta = delta - jnp.sum(dl_ref[:, h * hdv:(h + 1) * hdv].astype(F32), axis=1, keepdims=True)
                do_h = doa[:, h * hdv:(h + 1) * hdv]
                dv_sc[g] += lax.dot_general(p.astype(BF16), do_h, _TN, preferred_element_type=F32)
                dp = lax.dot_general(do_h, va[:, g * hdv:(g + 1) * hdv], _NT, preferred_element_type=F32)
                ds = (p * (dp - delta) * scale).astype(BF16)
                dk_sc[g] += lax.dot_general(ds, qa[:, h * hd:(h + 1) * hd], _TN, preferred_element_type=F32)
                if rope:
                    dkp_sc[...] += lax.dot_general(ds, qpa[:, h * rope:(h + 1) * rope], _TN, preferred_element_type=F32)

        _when(geom.q_active(j, t), step)

        @pl.when(t == nt - 1)
        def _():
            for g in range(hk):
                dk_ref[:, g * hd:(g + 1) * hd] = dk_sc[g].astype(dk_ref.dtype)
                dv_ref[:, g * hdv:(g + 1) * hdv] = dv_sc[g].astype(dv_ref.dtype)
            if rope:
                dkp_ref[...] = jnp.zeros(dkp_ref.shape, dkp_ref.dtype)
                dkp_ref[:, :rope] = dkp_sc[...].astype(dkp_ref.dtype)

    outs = _pcall(
        body, name=name, grid=(dil, geom.nk, nt), in_specs=specs, out_specs=out_specs, out_shape=out_shape,
        scratch_shapes=scratch, compiler_params=_cparams(("parallel", "parallel", "arbitrary")),
    )(*[o_[0] for o_ in ops])
    dk, dv = outs[0].reshape(Rk, kw), outs[1].reshape(Rk, vw)
    if rope:
        return dk, dv, outs[2].reshape(Rk, LANE)
    return dk, dv


class _Viewed(tuple):
    pass


class _BandPlan:
    def __init__(self, S, dil, max_dist):
        self.L, self.dil, self.max_dist = S // dil, dil, max_dist
        self.nblk = self.L // BLOCK
        self.nb = min(4, self.nblk)
        self.ns = min(dil, max(1, 4 // self.nb))
        self.grid = (dil // self.ns, self.nblk // self.nb)
        self.rows = self.nb * BLOCK

    def view(self, p):
        if isinstance(p, _Viewed):
            return p
        arr, w, idx = _piece(p)
        R, C = arr.shape
        assert C % w == 0 and (self.ns == 1 or (C == w and idx == 0)), (C, w, idx, self.ns)
        return _Viewed((arr.reshape(R // self.dil, self.dil * C), w, idx, C // w))

    def main(self, view):
        _, w, idx, cpw = view
        if self.ns == 1:
            return pl.BlockSpec((self.rows, w), lambda s, i: (i, s * cpw + idx))
        return pl.BlockSpec((self.rows, self.ns * w), lambda s, i: (i, s))

    def edge(self, view, nxt):
        _, w, idx, cpw = view
        nb, last = self.nb, self.nblk - 1
        rb = (lambda i: jnp.minimum((i + 1) * nb, last)) if nxt else (lambda i: jnp.maximum(i * nb - 1, 0))
        if self.ns == 1:
            return pl.BlockSpec((BLOCK, w), lambda s, i: (rb(i), s * cpw + idx))
        return pl.BlockSpec((BLOCK, self.ns * w), lambda s, i: (rb(i), s))

    def out(self, w):
        return pl.BlockSpec((self.rows, self.ns * w), lambda s, i: (i, s))

    def masks(self):
        qi = lax.broadcasted_iota(jnp.int32, (BLOCK, BLOCK), 0)
        kj = lax.broadcasted_iota(jnp.int32, (BLOCK, BLOCK), 1)
        return kj <= qi, (qi - kj + BLOCK) <= self.max_dist


def _half(e, rows=BLOCK):
    lane = lax.broadcasted_iota(jnp.int32, (rows, LANE), 1)
    return (lane < LANE // 2) if e == 0 else (lane >= LANE // 2)


def _swap_halves(t):
    return pltpu.roll(t, LANE // 2, axis=1)


def _kv_group(ref, rows, col0, pr, grp, hq, hk):
    if hk == hq:
        return ref[rows, col0 + pr * LANE:col0 + (pr + 1) * LANE].astype(BF16)
    g = (2 * pr) // grp
    t = ref[rows, col0 + (g // 2) * LANE:col0 + (g // 2 + 1) * LANE].astype(BF16)
    sw = _swap_halves(t)
    h0 = _half(0, t.shape[0])
    return jnp.where(h0, t, sw) if g % 2 == 0 else jnp.where(h0, sw, t)


def _stack_heads(t2):
    z = jnp.zeros_like(t2)
    h0 = _half(0, t2.shape[0])
    return jnp.concatenate([jnp.where(h0, t2, z), jnp.where(h0, z, t2)], axis=0)


def _unstack_heads(t, rows=BLOCK):
    return jnp.where(_half(0, rows), t[:rows], t[rows:])


def _per_head_col(t2):
    return jnp.concatenate([t2[:, :1], t2[:, LANE // 2:LANE // 2 + 1]], axis=0)


def _per_head_sum(t2):
    h0 = _half(0, t2.shape[0])
    return jnp.concatenate([jnp.sum(jnp.where(h0, t2, 0.0), axis=1, keepdims=True),
                            jnp.sum(jnp.where(h0, 0.0, t2), axis=1, keepdims=True)], axis=0)


def _band_fwd(q, k, v, *, S, dil, max_dist, hq, hk, scale, name):
    hd = 64
    plan = _BandPlan(S, dil, max_dist)
    qv, kv, vv = plan.view(q), plan.view(k), plan.view(v)
    wq, wk = hq * hd, hk * hd
    grp = hq // hk
    ns, nb = plan.ns, plan.nb

    def body(q_ref, k_ref, kp_ref, v_ref, vp_ref, o_ref, l_ref):
        i = pl.program_id(1)
        m_cur, m_band = plan.masks()
        has_prev = i > 0
        for sg in range(ns):
            for b in range(nb):
                rows = slice(b * BLOCK, (b + 1) * BLOCK)
                prows = slice((b - 1) * BLOCK, b * BLOCK) if b > 0 else slice(0, BLOCK)
                m_prev = m_band if b > 0 else (m_band & has_prev)
                msk = jnp.concatenate([m_prev, m_cur], axis=1)
                msk = jnp.concatenate([msk, msk], axis=0)
                for pr in range(wq // LANE):
                    cols = slice(sg * wq + pr * LANE, sg * wq + (pr + 1) * LANE)
                    kcat = jnp.concatenate([_kv_group(k_ref if b > 0 else kp_ref, prows, sg * wk, pr, grp, hq, hk),
                                            _kv_group(k_ref, rows, sg * wk, pr, grp, hq, hk)], axis=0)
                    vcat = jnp.concatenate([_kv_group(v_ref if b > 0 else vp_ref, prows, sg * wk, pr, grp, hq, hk),
                                            _kv_group(v_ref, rows, sg * wk, pr, grp, hq, hk)], axis=0)
                    qs = _stack_heads(q_ref[rows, cols].astype(BF16))
                    s = lax.dot_general(qs, kcat, _NT, preferred_element_type=F32) * scale
                    s = jnp.where(msk, s, NEG)
                    mx = jnp.max(s, axis=1, keepdims=True)
                    p = jnp.exp(s - mx)
                    l = jnp.sum(p, axis=1, keepdims=True)
                    acc = jnp.dot(p.astype(BF16), vcat, preferred_element_type=F32)
                    o_ref[rows, cols] = _unstack_heads(acc / l)
                    l_ref[rows, cols] = _unstack_heads(jnp.broadcast_to(mx + jnp.log(l), (2 * BLOCK, LANE)))

    o, lse = _pcall(
        body, name=name, grid=plan.grid,
        in_specs=[plan.main(qv), plan.main(kv), plan.edge(kv, False), plan.main(vv), plan.edge(vv, False)],
        out_specs=[plan.out(wq)] * 2, out_shape=[jax.ShapeDtypeStruct((S // dil, dil * wq), F32)] * 2,
        compiler_params=_cparams(("parallel", "parallel")),
    )(qv[0], kv[0], kv[0], vv[0], vv[0])
    return o.reshape(S, wq), lse.reshape(S, wq), (_Viewed((o, wq, 0, 1)), _Viewed((lse, wq, 0, 1)))


def _band_views(arrays, *, S, dil, max_dist):
    plan = _BandPlan(S, dil, max_dist)
    return [None if a is None else plan.view(a) for a in arrays]


def _band_dq(q, k, v, do, o, lse, dlse, *, S, dil, max_dist, hq, hk, scale, out_dtype=F32, name):
    hd = 64
    plan = _BandPlan(S, dil, max_dist)
    qv, kv, vv = plan.view(q), plan.view(k), plan.view(v)
    dov, ov, lv = plan.view(do), plan.view(o), plan.view(lse)
    has_dl = dlse is not None
    wq, wk = hq * hd, hk * hd
    grp = hq // hk
    ns, nb = plan.ns, plan.nb
    ops = [qv, kv, kv, vv, vv, dov, ov, lv]
    specs = [plan.main(qv), plan.main(kv), plan.edge(kv, False), plan.main(vv), plan.edge(vv, False), plan.main(dov),
             plan.main(ov), plan.main(lv)]
    if has_dl:
        dlv = plan.view(dlse)
        ops.append(dlv)
        specs.append(plan.main(dlv))

    def body(*refs):
        q_ref, k_ref, kp_ref, v_ref, vp_ref, do_ref, o_ref, l_ref = refs[:8]
        dl_ref = refs[8] if has_dl else None
        dq_ref = refs[-1]
        i = pl.program_id(1)
        m_cur, m_band = plan.masks()
        has_prev = i > 0
        for sg in range(ns):
            for b in range(nb):
                rows = slice(b * BLOCK, (b + 1) * BLOCK)
                prows = slice((b - 1) * BLOCK, b * BLOCK) if b > 0 else slice(0, BLOCK)
                m_prev = m_band if b > 0 else (m_band & has_prev)
                msk = jnp.concatenate([m_prev, m_cur], axis=1)
                msk = jnp.concatenate([msk, msk], axis=0)
                for pr in range(wq // LANE):
                    cols = slice(sg * wq + pr * LANE, sg * wq + (pr + 1) * LANE)
                    kcat = jnp.concatenate([_kv_group(k_ref if b > 0 else kp_ref, prows, sg * wk, pr, grp, hq, hk),
                                            _kv_group(k_ref, rows, sg * wk, pr, grp, hq, hk)], axis=0)
                    vcat = jnp.concatenate([_kv_group(v_ref if b > 0 else vp_ref, prows, sg * wk, pr, grp, hq, hk),
                                            _kv_group(v_ref, rows, sg * wk, pr, grp, hq, hk)], axis=0)
                    qs = _stack_heads(q_ref[rows, cols].astype(BF16))
                    do2 = do_ref[rows, cols].astype(F32)
                    prod = do2 * o_ref[rows, cols].astype(F32)
                    if has_dl:
                        prod = prod - dl_ref[rows, cols].astype(F32)
                    delta = _per_head_sum(prod)
                    lse_rows = _per_head_col(l_ref[rows, cols])
                    dos = _stack_heads(do2.astype(BF16))
                    s = lax.dot_general(qs, kcat, _NT, preferred_element_type=F32) * scale
                    pm = jnp.where(msk, jnp.exp(s - lse_rows), 0.0)
                    dp = lax.dot_general(dos, vcat, _NT, preferred_element_type=F32)
                    ds = (pm * (dp - delta) * scale).astype(BF16)
                    dq_ref[rows, cols] = _unstack_heads(jnp.dot(ds, kcat, preferred_element_type=F32)).astype(dq_ref.dtype)

    dq = _pcall(
        body, name=name, grid=plan.grid, in_specs=specs, out_specs=plan.out(wq),
        out_shape=jax.ShapeDtypeStruct((S // dil, dil * wq), out_dtype), compiler_params=_cparams(("parallel", "parallel")),
    )(*[o_[0] for o_ in ops])
    return dq.reshape(S, wq)


def _band_dkv(q, k, v, do, o, lse, dlse, *, S, dil, max_dist, hq, hk, scale, out_dtype=F32, name):
    hd = 64
    plan = _BandPlan(S, dil, max_dist)
    qv, kv, vv = plan.view(q), plan.view(k), plan.view(v)
    dov, ov, lv = plan.view(do), plan.view(o), plan.view(lse)
    has_dl = dlse is not None
    wq, wk = hq * hd, hk * hd
    grp = hq // hk
    ns, nb = plan.ns, plan.nb
    qlike = [qv, dov, ov, lv] + ([plan.view(dlse)] if has_dl else [])
    ops = [kv, vv] + qlike + qlike
    specs = [plan.main(kv), plan.main(vv)] + [plan.main(t) for t in qlike] + [plan.edge(t, True) for t in qlike]
    nql = len(qlike)
    nkg = wk // LANE

    def body(*refs):
        k_ref, v_ref = refs[:2]
        mains, edges = refs[2:2 + nql], refs[2 + nql:2 + 2 * nql]
        dk_ref, dv_ref = refs[2 + 2 * nql:]
        i = pl.program_id(1)
        m_cur, m_band = plan.masks()
        has_next = i < plan.grid[1] - 1
        for sg in range(ns):
            for b in range(nb):
                rows = slice(b * BLOCK, (b + 1) * BLOCK)
                nxt_in_main = b + 1 < nb
                nrows = slice((b + 1) * BLOCK, (b + 2) * BLOCK) if nxt_in_main else slice(0, BLOCK)
                nsrc = mains if nxt_in_main else edges
                m_next = m_band if nxt_in_main else (m_band & has_next)
                mt_cur, mt_next = m_cur.T, m_next.T
                msk = jnp.concatenate([mt_cur, mt_cur, mt_next, mt_next], axis=1)
                nacc = nkg if hk == hq else hk
                dk_acc, dv_acc = [None] * nacc, [None] * nacc
                for pr in range(wq // LANE):
                    cols = slice(sg * wq + pr * LANE, sg * wq + (pr + 1) * LANE)
                    kop = _kv_group(k_ref, rows, sg * wk, pr, grp, hq, hk)
                    vop = _kv_group(v_ref, rows, sg * wk, pr, grp, hq, hk)
                    qs, dos, deltas, lses = [], [], [], []
                    for src, r in ((mains, rows), (nsrc, nrows)):
                        qs.append(_stack_heads(src[0][r, cols].astype(BF16)))
                        do2 = src[1][r, cols].astype(F32)
                        prod = do2 * src[2][r, cols].astype(F32)
                        if has_dl:
                            prod = prod - src[4][r, cols].astype(F32)
                        prod_t, lse_t = prod.T, src[3][r, cols].T
                        for e in range(2):
                            deltas.append(jnp.sum(prod_t[e * 64:(e + 1) * 64], axis=0, keepdims=True))
                            lses.append(lse_t[e * 64:e * 64 + 1])
                        dos.append(_stack_heads(do2.astype(BF16)))
                    qs4, dos4 = jnp.concatenate(qs, axis=0), jnp.concatenate(dos, axis=0)
                    delta4, lse4 = jnp.concatenate(deltas, axis=1), jnp.concatenate(lses, axis=1)
                    st = lax.dot_general(kop, qs4, _NT, preferred_element_type=F32) * scale
                    pm = jnp.where(msk, jnp.exp(st - lse4), 0.0)
                    dpt = lax.dot_general(vop, dos4, _NT, preferred_element_type=F32)
                    ds = (pm * (dpt - delta4) * scale).astype(BF16)
                    tv = jnp.dot(pm.astype(BF16), dos4, preferred_element_type=F32)
                    tk = jnp.dot(ds, qs4, preferred_element_type=F32)
                    ai = pr if hk == hq else (2 * pr) // grp
                    dv_acc[ai] = tv if dv_acc[ai] is None else dv_acc[ai] + tv
                    dk_acc[ai] = tk if dk_acc[ai] is None else dk_acc[ai] + tk
                for kg in range(nkg):
                    cols = slice(sg * wk + kg * LANE, sg * wk + (kg + 1) * LANE)
                    if hk == hq:
                        dkt, dvt = dk_acc[kg], dv_acc[kg]
                    else:
                        both = lambda t: t + _swap_halves(t)
                        h0 = _half(0)
                        dkt = jnp.where(h0, both(dk_acc[2 * kg]), both(dk_acc[2 * kg + 1]))
                        dvt = jnp.where(h0, both(dv_acc[2 * kg]), both(dv_acc[2 * kg + 1]))
                    dk_ref[rows, cols] = dkt.astype(dk_ref.dtype)
                    dv_ref[rows, cols] = dvt.astype(dv_ref.dtype)

    dk, dv = _pcall(
        body, name=name, grid=plan.grid, in_specs=specs, out_specs=[plan.out(wk)] * 2,
        out_shape=[jax.ShapeDtypeStruct((S // dil, dil * wk), out_dtype)] * 2,
        compiler_params=_cparams(("parallel", "parallel")),
    )(*[o_[0] for o_ in ops])
    return dk.reshape(S, wk), dv.reshape(S, wk)


def _causal_block(S):
    return min(512, S)


def _causal_mask(bq):
    qi = lax.broadcasted_iota(jnp.int32, (bq, bq), 0)
    kj = lax.broadcasted_iota(jnp.int32, (bq, bq), 1)
    return kj <= qi


def _half_of(rows, e):
    lane = lax.broadcasted_iota(jnp.int32, (rows, LANE), 1)
    return (lane < LANE // 2) if e == 0 else (lane >= LANE // 2)


def _causal_fwd(q, k, v, *, heads, scale, name, bk=None):
    S = q.shape[0]
    bq = _causal_block(S)
    bk = bq if bk is None else min(bk, bq)
    r = bq // bk
    nq, nk = S // bq, S // bk
    npair = heads // 2
    wv = heads * 64
    c2 = scale * math.log2(math.e)

    def body(q_ref, k_ref, v_ref, o_ref, l_ref, m_sc, l_sc, acc_sc):
        i, t = pl.program_id(0), pl.program_id(1)

        @pl.when(t == 0)
        def _():
            m_sc[...] = jnp.full(m_sc.shape, NEG, F32)
            l_sc[...] = jnp.zeros(l_sc.shape, F32)
            acc_sc[...] = jnp.zeros(acc_sc.shape, F32)

        def step(masked):
            if masked:
                qi = lax.broadcasted_iota(jnp.int32, (bq, bk), 0)
                kj = lax.broadcasted_iota(jnp.int32, (bq, bk), 1) + (t - r * i) * bk
                msk = kj <= qi
            h0 = _half_of(bq, 0)
            ss = []
            for h in range(heads):
                cols = slice(h * LANE, (h + 1) * LANE)
                sh = lax.dot_general(q_ref[:, cols], k_ref[:, cols], _NT, preferred_element_type=F32)
                ss.append(jnp.where(msk, sh, NEG) if masked else sh)
            m_old = [m_sc[h] for h in range(heads)]
            m_new = [jnp.maximum(m_old[h], jnp.max(ss[h], axis=1, keepdims=True)) for h in range(heads)]
            ps = [jnp.exp2((ss[h] - m_new[h]) * c2) for h in range(heads)]
            alpha = [jnp.exp2((m_old[h] - m_new[h]) * c2) for h in range(heads)]
            for h in range(heads):
                l_sc[h] = alpha[h] * l_sc[h] + jnp.sum(ps[h], axis=1, keepdims=True)
                m_sc[h] = m_new[h]
            for pr in range(npair):
                v2 = v_ref[:, pr * LANE:(pr + 1) * LANE].astype(BF16)
                pv = [jnp.dot(ps[2 * pr + e].astype(BF16), v2, preferred_element_type=F32) for e in range(2)]
                acc = acc_sc[pr]
                acc_sc[pr] = jnp.where(h0, alpha[2 * pr] * acc + pv[0], alpha[2 * pr + 1] * acc + pv[1])

        pl.when(t < r * i)(lambda: step(False))
        pl.when((t >= r * i) & (t < r * (i + 1)))(lambda: step(True))

        @pl.when(t == nk - 1)
        def _():
            h0 = _half_of(bq, 0)
            for pr in range(npair):
                l0, l1 = l_sc[2 * pr], l_sc[2 * pr + 1]
                acc = acc_sc[pr]
                cols = slice(pr * LANE, (pr + 1) * LANE)
                o_ref[:, cols] = jnp.where(h0, acc / l0, acc / l1)
                l_ref[:, cols] = jnp.where(h0, m_sc[2 * pr] * scale + jnp.log(l0), m_sc[2 * pr + 1] * scale + jnp.log(l1))

    kvi = lambda i, t: (jnp.minimum(t, r * (i + 1) - 1), 0)
    qs = pl.BlockSpec((bq, heads * LANE), lambda i, t: (i, 0))
    ks = pl.BlockSpec((bk, heads * LANE), kvi)
    vs = pl.BlockSpec((bk, wv), kvi)
    os_ = pl.BlockSpec((bq, wv), lambda i, t: (i, 0))
    return _pcall(
        body, name=name, grid=(nq, nk), in_specs=[qs, ks, vs], out_specs=[os_, os_],
        out_shape=[jax.ShapeDtypeStruct((S, wv), F32)] * 2,
        scratch_shapes=[pltpu.VMEM((heads, bq, 1), F32), pltpu.VMEM((heads, bq, 1), F32), pltpu.VMEM((npair, bq, LANE), F32)],
        compiler_params=_cparams(("parallel", "arbitrary")),
    )(q, k, v)


def _causal_bwd_tile(q_ref, k_ref, v2, do2, prod, l2, h, e, scale, msk, bq):
    cols = slice(h * LANE, (h + 1) * LANE)
    hm = _half_of(bq, e)
    s = lax.dot_general(q_ref[:, cols], k_ref[:, cols], _NT, preferred_element_type=F32)
    c2 = scale * math.log2(math.e)
    p = jnp.exp2((s - l2[:, e * 64:e * 64 + 1] * (1.0 / scale)) * c2)
    if msk is not None:
        p = jnp.where(msk, p, 0.0)
    dom = jnp.where(hm, do2, jnp.zeros_like(do2))
    delta = jnp.sum(jnp.where(hm, prod, 0.0), axis=1, keepdims=True)
    dp = lax.dot_general(dom, v2, _NT, preferred_element_type=F32)
    ds = (p * (dp - delta) * scale).astype(BF16)
    return p, ds, dom


def _causal_dq(q, k, v, do, o, lse, *, heads, scale, out_dtype=BF16, name):
    S = q.shape[0]
    bq = _causal_block(S)
    nq = S // bq
    npair = heads // 2
    wv = heads * 64

    def body(q_ref, k_ref, v_ref, do_ref, o_ref, l_ref, dq_ref, dq_sc):
        i, t = pl.program_id(0), pl.program_id(1)

        @pl.when(t == 0)
        def _():
            dq_sc[...] = jnp.zeros(dq_sc.shape, F32)

        def step(masked):
            msk = _causal_mask(bq) if masked else None
            for pr in range(npair):
                pc = slice(pr * LANE, (pr + 1) * LANE)
                v2 = v_ref[:, pc].astype(BF16)
                dof = do_ref[:, pc].astype(F32)
                prod = dof * o_ref[:, pc]
                do2 = dof.astype(BF16)
                l2 = l_ref[:, pc]
                for e in range(2):
                    h = 2 * pr + e
                    _, ds, _ = _causal_bwd_tile(q_ref, k_ref, v2, do2, prod, l2, h, e, scale, msk, bq)
                    dq_sc[h] += jnp.dot(ds, k_ref[:, h * LANE:(h + 1) * LANE], preferred_element_type=F32)

        pl.when(t < i)(lambda: step(False))
        pl.when(t == i)(lambda: step(True))

        @pl.when(t == nq - 1)
        def _():
            for h in range(heads):
                dq_ref[:, h * LANE:(h + 1) * LANE] = dq_sc[h].astype(dq_ref.dtype)

    qs = pl.BlockSpec((bq, heads * LANE), lambda i, t: (i, 0))
    ks = pl.BlockSpec((bq, heads * LANE), lambda i, t: (jnp.minimum(t, i), 0))
    vs = pl.BlockSpec((bq, wv), lambda i, t: (jnp.minimum(t, i), 0))
    os_ = pl.BlockSpec((bq, wv), lambda i, t: (i, 0))
    return _pcall(
        body, name=name, grid=(nq, nq), in_specs=[qs, ks, vs, os_, os_, os_], out_specs=qs,
        out_shape=jax.ShapeDtypeStruct((S, heads * LANE), out_dtype),
        scratch_shapes=[pltpu.VMEM((heads, bq, LANE), F32)],
        compiler_params=_cparams(("parallel", "arbitrary")),
    )(q, k, v, do, o, lse)


def _causal_dkv(q, k, v, do, o, lse, *, heads, scale, out_dtype=BF16, name):
    S = q.shape[0]
    bq = _causal_block(S)
    nq = S // bq
    npair = heads // 2
    wv = heads * 64

    def body(q_ref, k_ref, v_ref, do_ref, o_ref, l_ref, dk_ref, dv_ref, dk_sc, dv_sc):
        j, t = pl.program_id(0), pl.program_id(1)

        @pl.when(t == 0)
        def _():
            dk_sc[...] = jnp.zeros(dk_sc.shape, F32)
            dv_sc[...] = jnp.zeros(dv_sc.shape, F32)

        def step(masked):
            c2 = scale * math.log2(math.e)
            if masked:
                msk = lax.broadcasted_iota(jnp.int32, (bq, bq), 0) <= lax.broadcasted_iota(jnp.int32, (bq, bq), 1)
            sts, dpts, doms, deltas, lses = [], [], [], [], []
            for pr in range(npair):
                pc = slice(pr * LANE, (pr + 1) * LANE)
                v2 = v_ref[:, pc].astype(BF16)
                dof = do_ref[:, pc].astype(F32)
                prod_t = (dof * o_ref[:, pc]).T
                lse_t = l_ref[:, pc].T
                do2 = dof.astype(BF16)
                for e in range(2):
                    h = 2 * pr + e
                    cols = slice(h * LANE, (h + 1) * LANE)
                    dom = jnp.where(_half_of(bq, e), do2, jnp.zeros_like(do2))
                    sts.append(lax.dot_general(k_ref[:, cols], q_ref[:, cols], _NT, preferred_element_type=F32))
                    dpts.append(lax.dot_general(v2, dom, _NT, preferred_element_type=F32))
                    doms.append(dom)
                    deltas.append(jnp.sum(prod_t[e * 64:(e + 1) * 64], axis=0, keepdims=True))
                    lses.append(lse_t[e * 64:e * 64 + 1] * (1.0 / scale))
            pts = [jnp.exp2((sts[h] - lses[h]) * c2) for h in range(heads)]
            if masked:
                pts = [jnp.where(msk, t_, 0.0) for t_ in pts]
            dsts = [(pts[h] * (dpts[h] - deltas[h]) * scale).astype(BF16) for h in range(heads)]
            for pr in range(npair):
                tv = [jnp.dot(pts[2 * pr + e].astype(BF16), doms[2 * pr + e], preferred_element_type=F32) for e in range(2)]
                dv_sc[pr] += tv[0] + tv[1]
            for h in range(heads):
                dk_sc[h] += jnp.dot(dsts[h], q_ref[:, h * LANE:(h + 1) * LANE], preferred_element_type=F32)

        pl.when(t > j)(lambda: step(False))
        pl.when(t == j)(lambda: step(True))

        @pl.when(t == nq - 1)
        def _():
            for h in range(heads):
                dk_ref[:, h * LANE:(h + 1) * LANE] = dk_sc[h].astype(dk_ref.dtype)
            for pr in range(npair):
                dv_ref[:, pr * LANE:(pr + 1) * LANE] = dv_sc[pr].astype(dv_ref.dtype)

    qi = lambda j, t: (jnp.maximum(t, j), 0)
    qs = pl.BlockSpec((bq, heads * LANE), qi)
    os_ = pl.BlockSpec((bq, wv), qi)
    ks = pl.BlockSpec((bq, heads * LANE), lambda j, t: (j, 0))
    vs = pl.BlockSpec((bq, wv), lambda j, t: (j, 0))
    return _pcall(
        body, name=name, grid=(nq, nq), in_specs=[qs, ks, vs, os_, os_, os_], out_specs=[ks, vs],
        out_shape=[jax.ShapeDtypeStruct((S, heads * LANE), out_dtype), jax.ShapeDtypeStruct((S, wv), out_dtype)],
        scratch_shapes=[pltpu.VMEM((heads, bq, LANE), F32), pltpu.VMEM((npair, bq, LANE), F32)],
        compiler_params=_cparams(("parallel", "arbitrary")),
    )(q, k, v, do, o, lse)


@jax.custom_vjp
def _bdot(x, w):
    return jnp.dot(x.astype(BF16), w.astype(BF16), preferred_element_type=F32)


def _bdot_fwd(x, w):
    return _bdot(x, w), (x, w)


def _bdot_bwd(res, g):
    x, w = res
    gb = g.astype(BF16)
    dx = lax.dot_general(gb, w.astype(BF16), _NT, preferred_element_type=F32)
    dw = lax.dot_general(x.astype(BF16), gb, _TN, preferred_element_type=F32)
    return dx.astype(x.dtype), dw.astype(w.dtype)


_bdot.defvjp(_bdot_fwd, _bdot_bwd)


def _tile_matrix(hd, width):
    r = lax.broadcasted_iota(jnp.int32, (hd, width), 0)
    c = lax.broadcasted_iota(jnp.int32, (hd, width), 1) % hd
    return jnp.where(r == c, 1.0, 0.0).astype(F32)


def _spread_matrix(heads, width):
    per = width // heads
    r = lax.broadcasted_iota(jnp.int32, (heads, width), 0)
    c = lax.broadcasted_iota(jnp.int32, (heads, width), 1) // per
    return jnp.where(r == c, 1.0, 0.0).astype(F32)


def _wide(t, width):
    n = width // t.shape[-1]
    return jnp.concatenate([t] * n, axis=1) if n > 1 else t


def _norm_heads(x, gain, hd):
    return _head_norm(x, _exact_dot(gain, _tile_matrix(hd, x.shape[-1])), hd)


def _norm_rope(x, gain, cos, sin, hd):
    w = x.shape[-1]
    return _rope(_norm_heads(x, gain, hd), _wide(cos, w), _wide(sin, w), hd)


def _f_norm(x, g):
    return (_row_norm(x, g),)


def _f_prep_acm(aq, ak, c0q, c0k, c0v, c1q, c1k, c1v, c2q, c2k, c2v, mq, cos, sin, g_aq, g_ak, g0q, g0k, g1q, g1k, g2q, g2k, g_mq):
    outs = [_norm_rope(aq, g_aq, cos, sin, A_HD), _norm_rope(ak, g_ak, cos, sin, A_HD)]
    for cq, ck, cv, gq, gk in ((c0q, c0k, c0v, g0q, g0k), (c1q, c1k, c1v, g1q, g1k), (c2q, c2k, c2v, g2q, g2k)):
        outs += [_norm_rope(cq, gq, cos, sin, C_HD), _norm_rope(ck, gk, cos, sin, C_HD), cv]
    outs.append(_norm_heads(mq, g_mq, M_HD))
    return tuple(outs)


def _rope_mla_q(x, cos, sin):
    lane = lax.broadcasted_iota(jnp.int32, x.shape, x.ndim - 1) % LANE
    half = B_ROPE // 2
    first = (lane >= B_NOPE) & (lane < B_NOPE + half)
    other = jnp.where(first, -_lane_roll(x, -half), _lane_roll(x, half))
    return x * cos + other * sin


def _f_prep_b(ckv, cq, kr, cosq, sinq, cosr, sinr, g_qa, g_kva, w_uq, w_ukv, g_q, g_k, g_kr):
    wq = B_HEADS * LANE
    q_up = _bdot(_row_norm(cq, g_qa), w_uq)
    gq = _exact_dot(g_q, _tile_matrix(LANE, wq))
    qf = _rope_mla_q(_head_norm(q_up, gq, "mla"), _wide(cosq, wq), _wide(sinq, wq))
    kv_up = _bdot(_row_norm(ckv, g_kva), w_ukv)
    kn = _head_norm(kv_up[:, :wq], _exact_dot(g_k, _tile_matrix(LANE, wq)), B_NOPE)
    vb = kv_up[:, wq:]
    kp = _rope(_head_norm(kr, g_kr, B_ROPE), cosr, sinr, B_ROPE)
    kp = _lane_roll(kp, B_NOPE)
    return qf, kn + _wide(kp, wq), vb


def _f_mem_k(k, g):
    return (_norm_heads(k, g, M_HD),)


def _f_sink(o, lse, sink):
    sb = _exact_dot(sink, _spread_matrix(A_HEADS, o.shape[-1]))
    m = jnp.maximum(lse, sb)
    tot = m + jnp.log(jnp.exp(lse - m) + jnp.exp(sb - m))
    return (o * jnp.exp(lse - tot),)


def _f_combine(o0, o1, o2, l0, l1, l2):
    m = jnp.maximum(jnp.maximum(l0, l1), l2)
    e0, e1, e2 = jnp.exp(l0 - m), jnp.exp(l1 - m), jnp.exp(l2 - m)
    inv = 1.0 / (e0 + e1 + e2)
    return ((e0 * o0 + e1 * o1 + e2 * o2) * inv,)


def _f_gatemix(gp, y0, y1, y2, y3, bg):
    d = y0.shape[-1]
    gates = 1.0 / (1.0 + jnp.exp(-(gp + bg)))
    mix = gates[:, :d] * y0
    for n, y in enumerate((y1, y2, y3), start=1):
        mix = mix + gates[:, n * d:(n + 1) * d] * y
    return (mix,)


def _relu2(u):
    return jnp.square(jnp.maximum(u, 0.0))


def _add(r, e):
    return r + e.astype(F32)


def _relu2_grad(r, u):
    return r * (2.0 * jnp.maximum(u, 0.0))


def _loss_and_grad(y, target, *, tb=512):
    R, D = y.shape
    tb = min(tb, R)

    def body(y_ref, t_ref, dy_ref, l_ref):
        err = y_ref[...] - t_ref[...]
        dy_ref[...] = err * (1.0 / D)
        part = 0.5 * jnp.sum(jnp.sum(err * err, axis=1, keepdims=True) * (1.0 / D), axis=0, keepdims=True)
        first = pl.program_id(0) == 0

        @pl.when(first)
        def _():
            l_ref[...] = jnp.broadcast_to(part, l_ref.shape)

        @pl.when(jnp.logical_not(first))
        def _():
            l_ref[...] += jnp.broadcast_to(part, l_ref.shape)

    dy, l = _pcall(
        body, name="loss", grid=(R // tb,),
        in_specs=[pl.BlockSpec((tb, D), lambda i: (i, 0))] * 2,
        out_specs=[pl.BlockSpec((tb, D), lambda i: (i, 0)), pl.BlockSpec((8, LANE), lambda i: (0, 0))],
        out_shape=[jax.ShapeDtypeStruct((R, D), F32), jax.ShapeDtypeStruct((8, LANE), F32)],
        compiler_params=_cparams(("arbitrary",)),
    )(y, target)
    return l[0, 0], dy


def _z_layout(d):
    assert d == 1024, "the aligned layout below is laid out for D_MODEL = 1024"
    lay = {"gates": (4 * d, 0)}
    for g in range(3):
        for n, nm in enumerate("qkv"):
            lay[f"c{g}{nm}"] = (512, 8 + 3 * g + n)
    lay.update(aq=(512, 17), mq=(512, 18), ckv=(256, 38), cq=(384, 26), ak=(128, 81), av=(128, 82), kr=(128, 83))
    return lay, 10752


_KW_A = dict(hq=A_HEADS, hk=A_KV_HEADS, scale=A_HD ** -0.5)
_KW_B = dict(heads=B_HEADS, scale=(B_NOPE + B_ROPE) ** -0.5)
_KW_C = dict(hq=C_HEADS, hk=C_HEADS, scale=C_HD ** -0.5)
_KW_M = dict(hq=M_HEADS, hk=M_HEADS, hd=M_HD, hdv=M_HD, scale=M_HD ** -0.5)


def _layer_fwd(l, x, mem, w, tabs, late=None):
    S, D = x.shape
    lay, _ = _z_layout(D)
    cosA, sinA, cosB, sinB, cosQ, sinQ = tabs
    gM = _AttnGeom("full", S, mem.shape[0])
    nm = lambda s: f"l{l}_{s}"
    sv = {}
    hn = _rowmap(_f_norm, [x], [w["g_mix"]], [(D, BF16)], tb=512, name=nm("norm1"))[0]
    z = _mm(hn, w["in"], name=nm("in"), tn=1536)
    zp = {k: (z, wd, idx) for k, (wd, idx) in lay.items()}
    acm_rows = [zp[k] for k in ("aq", "ak", "c0q", "c0k", "c0v", "c1q", "c1k", "c1v", "c2q", "c2k", "c2v", "mq")] + [cosA, sinA]
    acm_par = [w["a_qn"], w["a_kn"], w["c0q"], w["c0k"], w["c1q"], w["c1k"], w["c2q"], w["c2k"], w["m_qn"]]
    acm = _rowmap(_f_prep_acm, acm_rows, acm_par, [(p[1], BF16) for p in acm_rows[:12]], tb=256, name=nm("prep_acm"))
    qa, ka, qc0, kc0, vc0, qc1, kc1, vc1, qc2, kc2, vc2, mq = acm
    oa_raw, lse_a, _ = _band_fwd(qa, ka, zp["av"], S=S, dil=1, max_dist=A_WINDOW - 1, name=nm("attn_a"), **_KW_A)
    o_a = _rowmap(_f_sink, [oa_raw, lse_a], [w["a_sink"]], [(512, BF16)], tb=512, name=nm("sink"))[0]
    oc, lc, cviews = [], [], []
    for g, ((win, dil), qc, kc, vc) in enumerate(zip(C_PATTERNS, (qc0, qc1, qc2), (kc0, kc1, kc2), (vc0, vc1, vc2))):
        qkv = _band_views([qc, kc, vc], S=S, dil=dil, max_dist=win // dil)
        o_g, l_g, ol_views = _band_fwd(*qkv, S=S, dil=dil, max_dist=win // dil, name=nm(f"attn_c{g}"), **_KW_C)
        oc.append(o_g)
        lc.append(l_g)
        cviews.append(tuple(qkv) + tuple(ol_views))
    o_c = _rowmap(_f_combine, oc + lc, [], [(512, BF16)], tb=512, name=nm("combine"))[0]
    if late is not None:
        w = dict(w, **late(o_c))

    b_rows = [zp["ckv"], zp["cq"], zp["kr"], cosQ, sinQ, cosB, sinB]
    b_par = [w["b_qa"], w["b_kva"], w["uq"], w["ukv"], w["b_q"], w["b_k"], w["b_kr"]]
    qb, kb, vb = _rowmap(_f_prep_b, b_rows, b_par, [(B_HEADS * LANE, BF16), (B_HEADS * LANE, BF16), (512, BF16)], tb=256,
                         name=nm("prep_b"))
    memn = _rowmap(_f_norm, [mem], [w["m_g_mem"]], [(D, BF16)], tb=256, name=nm("mem_norm"))[0]
    mkv = _mm(memn, w["mkv"], name=nm("mem_kv"))
    mk = _rowmap(_f_mem_k, [(mkv, 512, 0)], [w["m_kn"]], [(512, BF16)], tb=256, name=nm("mem_k"))[0]
    mv = (mkv, 512, 1)
    o_b, lse_b = _causal_fwd(qb, kb, vb, name=nm("attn_b"), **_KW_B)
    o_m, lse_m = _attn_fwd(gM, mq, mk, mv, name=nm("attn_m"), **_KW_M)

    o_n = [o_a, o_b, o_c, o_m]
    ys = [_mm(o_n[n], w["branch"][n], name=nm(f"branch{n}")) for n in range(N_BRANCH)]
    mix = _rowmap(_f_gatemix, [zp["gates"]] + ys, [w["b_gate"]], [(D, BF16)], tb=256, name=nm("gatemix"))[0]
    x1 = _mm(mix, w["out"], extra=x, epi=_add, name=nm("out"))
    hn2 = _rowmap(_f_norm, [x1], [w["g_mlp"]], [(D, BF16)], tb=512, name=nm("norm2"))[0]
    u = _mm(hn2, w["up"], name=nm("up"))
    x2 = _mm(u, w["down"], pro_a=_relu2, extra=x1, epi=_add, name=nm("down"))
    sv.update(x=x, hn=hn, z=z, acm=acm, bq=(qb, kb, vb), memn=memn, mkv=mkv, mk=mk, oa_raw=oa_raw, lse_a=lse_a,
              o_b=o_b, lse_b=lse_b, oc=oc, lc=lc, cviews=cviews, o_m=o_m, lse_m=lse_m, o_n=o_n, ys=ys, mix=mix, x1=x1, hn2=hn2, u=u)
    return x2, sv, w


def _layer_bwd(l, dx2, mem, w, tabs, sv, early=None):
    x, z, x1, u = sv["x"], sv["z"], sv["x1"], sv["u"]
    S, D = x.shape
    lay, zw = _z_layout(D)
    cosA, sinA, cosB, sinB, cosQ, sinQ = tabs
    gM = _AttnGeom("full", S, mem.shape[0])
    nm = lambda s: f"l{l}_{s}"
    zp = {k: (z, wd, idx) for k, (wd, idx) in lay.items()}
    g = {}
    du = _mm(dx2, w["down"], tb=True, extra=u, epi=_relu2_grad, out_dtype=BF16, name=nm("d_down_x"))
    g["down"] = _mm(u, dx2, ta=True, pro_a=_relu2, name=nm("d_down_w"))
    dhn2 = _mm(du, w["up"], tb=True, name=nm("d_up_x"))
    g["up"] = _mm(sv["hn2"], du, ta=True, name=nm("d_up_w"))
    (dx1,), (g["g_mlp"],) = _rowmap_bwd(_f_norm, [x1], [w["g_mlp"]], [dhn2], diff=[True], out_dtypes=[F32], adds=[dx2],
                                        tb=256, name=nm("d_norm2"))
    dmix = _mm(dx1, w["out"], tb=True, name=nm("d_out_x"))
    g["out"] = _mm(sv["mix"], dx1, ta=True, name=nm("d_out_w"))
    (dgates, dy0, dy1, dy2, dy3), (g["b_gate"],) = _rowmap_bwd(
        _f_gatemix, [zp["gates"]] + sv["ys"], [w["b_gate"]], [dmix], diff=[True] * 5, out_dtypes=[BF16] * 5,
        tb=128, name=nm("d_gatemix"))
    dys = [dy0, dy1, dy2, dy3]
    do = [_mm(dys[n], w["branch"][n], tb=True, name=nm(f"d_branch{n}_x")) for n in range(N_BRANCH)]
    g["branch"] = [_mm(sv["o_n"][n], dys[n], ta=True, name=nm(f"d_branch{n}_w")) for n in range(N_BRANCH)]
    qa, ka, qc0, kc0, vc0, qc1, kc1, vc1, qc2, kc2, vc2, mq = sv["acm"]
    qb, kb, vb = sv["bq"]
    mkv, mk = sv["mkv"], sv["mk"]
    mv = (mkv, 512, 1)
    dmq = _attn_dq(gM, mq, mk, mv, do[3], sv["o_m"], sv["lse_m"], None, name=nm("attn_m_dq"), **_KW_M)
    dmk, dmv = _attn_dkv(gM, mq, mk, mv, do[3], sv["o_m"], sv["lse_m"], None, name=nm("attn_m_dkv"), **_KW_M)
    (doc0, doc1, doc2, dl0, dl1, dl2), _ = _rowmap_bwd(_f_combine, sv["oc"] + sv["lc"], [], [do[2]], diff=[True] * 6,
                                                      out_dtypes=[F32] * 6, tb=256, name=nm("d_combine"))
    dqc, dkc, dvc = [], [], []
    for gi, ((win, dil), qc, kc, vc, doc, dl) in enumerate(zip(C_PATTERNS, (qc0, qc1, qc2), (kc0, kc1, kc2), (vc0, vc1, vc2),
                                                              (doc0, doc1, doc2), (dl0, dl1, dl2))):
        qv_, kv_, vv_, ov_, lv_ = sv["cviews"][gi]
        dov_, dlv_ = _band_views([doc, dl], S=S, dil=dil, max_dist=win // dil)
        args = (qv_, kv_, vv_, dov_, ov_, lv_, dlv_)
        kwc = dict(S=S, dil=dil, max_dist=win // dil, out_dtype=BF16, **_KW_C)
        dqc.append(_band_dq(*args, name=nm(f"attn_c{gi}_dq"), **kwc))
        dk_, dv_ = _band_dkv(*args, name=nm(f"attn_c{gi}_dkv"), **kwc)
        dkc.append(dk_)
        dvc.append(dv_)
    argsb = (qb, kb, vb, do[1], sv["o_b"], sv["lse_b"])
    dqb = _causal_dq(*argsb, name=nm("attn_b_dq"), **_KW_B)
    dkb, dvb = _causal_dkv(*argsb, name=nm("attn_b_dkv"), **_KW_B)
    (doa_raw, dlse_a), (g["a_sink"],) = _rowmap_bwd(_f_sink, [sv["oa_raw"], sv["lse_a"]], [w["a_sink"]], [do[0]],
                                                   diff=[True, True], out_dtypes=[F32, F32], tb=256, name=nm("d_sink"))
    argsa = (qa, ka, zp["av"], doa_raw, sv["oa_raw"], sv["lse_a"], dlse_a)
    kwa = dict(S=S, dil=1, max_dist=A_WINDOW - 1, out_dtype=BF16, **_KW_A)
    dqa = _band_dq(*argsa, name=nm("attn_a_dq"), **kwa)
    dka, dva = _band_dkv(*argsa, name=nm("attn_a_dkv"), **kwa)
    acm_rows = [zp[k] for k in ("aq", "ak", "c0q", "c0k", "c0v", "c1q", "c1k", "c1v", "c2q", "c2k", "c2v", "mq")] + [cosA, sinA]
    acm_par = [w["a_qn"], w["a_kn"], w["c0q"], w["c0k"], w["c1q"], w["c1k"], w["c2q"], w["c2k"], w["m_qn"]]
    acm_ct = [dqa, dka, dqc[0], dkc[0], dvc[0], dqc[1], dkc[1], dvc[1], dqc[2], dkc[2], dvc[2], dmq]
    dacm, (g["a_qn"], g["a_kn"], g["c0q"], g["c0k"], g["c1q"], g["c1k"], g["c2q"], g["c2k"], g["m_qn"]) = _rowmap_bwd(
        _f_prep_acm, acm_rows, acm_par, acm_ct, diff=[True] * 12 + [False, False], out_dtypes=[BF16] * 12, tb=256,
        name=nm("d_prep_acm"))
    d_aq, d_ak, d_c0q, d_c0k, d_c0v, d_c1q, d_c1k, d_c1v, d_c2q, d_c2k, d_c2v, d_mq = dacm
    b_rows = [zp["ckv"], zp["cq"], zp["kr"], cosQ, sinQ, cosB, sinB]
    b_par = [w["b_qa"], w["b_kva"], w["uq"], w["ukv"], w["b_q"], w["b_k"], w["b_kr"]]
    (d_ckv, d_cq, d_kr), gb = _rowmap_bwd(_f_prep_b, b_rows, b_par, [dqb, dkb, dvb], diff=[True] * 3 + [False] * 4,
                                          out_dtypes=[BF16] * 3, tb=256, name=nm("d_prep_b"))
    g["b_qa"], g["b_kva"], g["uq"], g["ukv"], g["b_q"], g["b_k"], g["b_kr"] = gb
    (dmkv_k,), (g["m_kn"],) = _rowmap_bwd(_f_mem_k, [(mkv, 512, 0)], [w["m_kn"]], [dmk], diff=[True], out_dtypes=[F32],
                                          tb=256, name=nm("d_mem_k"))
    dmkv = jnp.concatenate([dmkv_k, dmv], axis=1)
    dmemn = _mm(dmkv, w["mkv"], tb=True, name=nm("d_mem_kv_x"))
    g["mkv"] = _mm(sv["memn"], dmkv, ta=True, name=nm("d_mem_kv_w"))
    _, (g["m_g_mem"],) = _rowmap_bwd(_f_norm, [mem], [w["m_g_mem"]], [dmemn], diff=[True], out_dtypes=[F32], tb=256,
                                     name=nm("d_mem_norm"))
    if early is not None:
        early(g)
    dz = jnp.concatenate([dgates, d_c0q, d_c0k, d_c0v, d_c1q, d_c1k, d_c1v, d_c2q, d_c2k, d_c2v, d_aq, d_mq, d_ckv, d_cq,
                          d_ak, dva, d_kr], axis=1)
    assert dz.shape[1] == zw
    dhn = _mm(dz, w["in"], tb=True, name=nm("d_in_x"), tk=1536)
    g["in"] = _mm(sv["hn"], dz, ta=True, name=nm("d_in_w"), tn=1536)
    (dx,), (g["g_mix"],) = _rowmap_bwd(_f_norm, [x], [w["g_mix"]], [dhn], diff=[True], out_dtypes=[F32], adds=[dx1],
                                       tb=256, name=nm("d_norm1"))
    return dx, g


_IN_ORIG = dict(aq=(0, 512), ak=(512, 640), av=(640, 768), cq=(768, 1152), ckv=(1152, 1408), kr=(1408, 1440),
                c=(1440, 6048), mq=(6048, 6560), gates=(6560, 10656))
_IN_OURS = dict(gates=(0, 4096), c=(4096, 8704), aq=(8704, 9216), mq=(9216, 9728), ckv=(9728, 9984), cq=(9984, 10368),
                ak=(10368, 10496), av=(10496, 10624), kr=(10624, 10656))
_IN_ORDER_ORIG = ("aq", "ak", "av", "cq", "ckv", "kr", "c", "mq", "gates")


def _in_to_ours(w_in):
    pc = {k: w_in[..., a:b] for k, (a, b) in _IN_ORIG.items()}
    zeros = lambda n: jnp.zeros(w_in.shape[:-1] + (n,), w_in.dtype)
    return jnp.concatenate([pc["gates"], pc["c"], pc["aq"], pc["mq"], pc["ckv"], pc["cq"], pc["ak"], pc["av"], pc["kr"],
                            zeros(96)], axis=-1)


def _in_from_ours(g_in):
    return jnp.concatenate([g_in[..., _IN_OURS[k][0]:_IN_OURS[k][1]] for k in _IN_ORDER_ORIG], axis=-1)


def _uq_to_ours(w):
    per = B_NOPE + B_ROPE
    w4 = w.reshape(w.shape[:-1] + (B_HEADS, per))
    w4 = jnp.pad(w4, [(0, 0)] * (w4.ndim - 1) + [(0, LANE - per)])
    return w4.reshape(w.shape[:-1] + (B_HEADS * LANE,))


def _uq_from_ours(g):
    per = B_NOPE + B_ROPE
    g4 = g.reshape(g.shape[:-1] + (B_HEADS, LANE))[..., :per]
    return g4.reshape(g.shape[:-1] + (B_HEADS * per,))


def _ukv_to_ours(w):
    w4 = w.reshape(w.shape[:-1] + (B_HEADS, B_NOPE + B_V))
    keys = jnp.pad(w4[..., :B_NOPE], [(0, 0)] * (w4.ndim - 1) + [(0, LANE - B_NOPE)])
    vals = w4[..., B_NOPE:]
    return jnp.concatenate([keys.reshape(w.shape[:-1] + (B_HEADS * LANE,)), vals.reshape(w.shape[:-1] + (B_HEADS * B_V,))],
                           axis=-1)


def _ukv_from_ours(g):
    wq = B_HEADS * LANE
    keys = g[..., :wq].reshape(g.shape[:-1] + (B_HEADS, LANE))[..., :B_NOPE]
    vals = g[..., wq:].reshape(g.shape[:-1] + (B_HEADS, B_V))
    return jnp.concatenate([keys, vals], axis=-1).reshape(g.shape[:-1] + (B_HEADS * (B_NOPE + B_V),))


def _layer_weights(big_l, small, l):
    row = lambda a: a[l].reshape(1, -1)
    w = dict(big_l)
    w.update(g_mix=row(small["g_mix"]), b_gate=row(small["b_gate"]), a_qn=row(small["a_qn"]), a_kn=row(small["a_kn"]),
             a_sink=row(small["a_sink"]), b_qa=row(small["b_qa_norm"]), b_kva=row(small["b_kva_norm"]),
             b_q=jnp.pad(small["b_qn"][l], (0, LANE - B_NOPE - B_ROPE)).reshape(1, -1),
             b_k=jnp.pad(small["b_kn"][l, :B_NOPE], (0, LANE - B_NOPE)).reshape(1, -1),
             b_kr=jnp.pad(small["b_kn"][l, B_NOPE:], (0, LANE - B_ROPE)).reshape(1, -1),
             m_g_mem=row(small["m_g_mem"]), m_qn=row(small["m_qn"]), m_kn=row(small["m_kn"]), g_mlp=row(small["g_mlp"]))
    for g in range(3):
        w[f"c{g}q"] = small["c_qn"][l, g].reshape(1, -1)
        w[f"c{g}k"] = small["c_kn"][l, g].reshape(1, -1)
    return w


def _rope_tables(positions):
    pos = positions.astype(F32)[:, None]
    tabs = []
    for dim in (A_HD, B_ROPE):
        inv = ROPE_THETA ** (-jnp.arange(0, dim, 2, dtype=F32) / dim)
        ang = pos * inv
        reps = LANE // (dim // 2)
        tabs += [jnp.tile(jnp.cos(ang), (1, reps)), jnp.tile(jnp.sin(ang), (1, reps))]
    half = B_ROPE // 2
    cb, sb = tabs[2][:, :half], tabs[3][:, :half]
    ones, zeros = jnp.ones((pos.shape[0], B_NOPE), F32), jnp.zeros((pos.shape[0], B_NOPE), F32)
    pad = LANE - B_NOPE - B_ROPE
    tabs.append(jnp.concatenate([ones, cb, cb, ones[:, :pad]], axis=1))
    tabs.append(jnp.concatenate([zeros, sb, sb, zeros[:, :pad]], axis=1))
    return tuple(tabs)


def _local_step(x, mem, positions, small, loss_target, get_big, put_grads, early_grads=None):
    depth = small["g_mix"].shape[0]
    tabs = _rope_tables(positions)
    ws, saved = [], []
    h = x
    for l in range(depth):
        first, late = get_big(l, h)
        h, sv, wl = _layer_fwd(l, h, mem, _layer_weights(first, small, l), tabs, late)
        ws.append(wl)
        saved.append(sv)
    loss, dh = _loss_and_grad(h, loss_target)
    small_grads = [None] * depth
    for l in reversed(range(depth)):
        dh, g = _layer_bwd(l, dh, mem, ws[l], tabs, saved[l], early_grads if l == 0 else None)
        zero = put_grads(l, g)
        if l > 0:
            ws[l - 1] = dict(ws[l - 1], g_mlp=ws[l - 1]["g_mlp"] + zero)
        small_grads[l] = g
    return loss, dh, small_grads


def _small_grads_to_reference_layout(grads):
    flat = lambda k: jnp.stack([g[k].reshape(-1) for g in grads])
    return dict(
        g_mix=flat("g_mix"), b_gate=flat("b_gate"), a_qn=flat("a_qn"), a_kn=flat("a_kn"), a_sink=flat("a_sink"),
        b_qa_norm=flat("b_qa"), b_kva_norm=flat("b_kva"), b_qn=flat("b_q")[:, :B_NOPE + B_ROPE],
        b_kn=jnp.concatenate([flat("b_k")[:, :B_NOPE], flat("b_kr")[:, :B_ROPE]], axis=1),
        c_qn=jnp.stack([jnp.stack([g[f"c{i}q"].reshape(-1) for i in range(3)]) for g in grads]),
        c_kn=jnp.stack([jnp.stack([g[f"c{i}k"].reshape(-1) for i in range(3)]) for g in grads]),
        m_g_mem=flat("m_g_mem"), m_qn=flat("m_qn"), m_kn=flat("m_kn"), g_mlp=flat("g_mlp"))


_BIG_GRAD = {"w_in": lambda g: _in_from_ours(g["in"]), "b_w_uq": lambda g: _uq_from_ours(g["uq"]),
             "b_w_ukv": lambda g: _ukv_from_ours(g["ukv"]), "m_w_kv": lambda g: g["mkv"], "w_branch": lambda g: jnp.stack(g["branch"]),
             "w_out": lambda g: g["out"], "w_up": lambda g: g["up"], "w_down": lambda g: g["down"]}


def _big_grads_to_reference_layout(g, names=None):
    return {k: _BIG_GRAD[k](g) for k in (_BIG_GRAD if names is None else names)}


def _big_to_kernel_layout(full):
    conv = {"w_in": ("in", _in_to_ours), "b_w_uq": ("uq", _uq_to_ours), "b_w_ukv": ("ukv", _ukv_to_ours),
            "m_w_kv": ("mkv", None), "w_branch": ("branch", None), "w_out": ("out", None), "w_up": ("up", None),
            "w_down": ("down", None)}
    out = {}
    for k, a in full.items():
        name, fn = conv[k]
        out[name] = (a if fn is None else fn(a)).astype(BF16)
    return out


MESH = pl.DeviceIdType.MESH
N_CHIPS = 4
N_DEV = 8
_ANY = pl.BlockSpec(memory_space=pl.ANY)


_HBM = pl.BlockSpec(memory_space=pltpu.HBM)
_SEM = pl.BlockSpec(memory_space=pltpu.SEMAPHORE)
_EFFECT = pltpu.SideEffectType.DATAFLOW_SIDE_EFFECTING


def _chip_peers():
    x, y, c = lax.axis_index("x"), lax.axis_index("y"), lax.axis_index("c")
    return 2 * x + y, [((1 - x, y, c), 2 * (1 - x) + y), ((x, 1 - y, c), 2 * x + 1 - y), ((1 - x, 1 - y, c), 2 * (1 - x) + 1 - y)]


def _exchange_start(srcs, lands, *, gather, name):
    n = len(srcs)

    def body(*refs):
        ins, land = refs[:n], refs[n:2 * n]
        send_sems, recv_sems = refs[2 * n], refs[2 * n + 1]
        token = refs[-1]
        me, peers = _chip_peers()
        for a in range(n):
            for j, (dev, chip) in enumerate(peers):
                src = ins[a] if gather else ins[a].at[chip]
                pltpu.make_async_remote_copy(src_ref=src, dst_ref=land[a].at[me], send_sem=send_sems.at[3 * a + j],
                                             recv_sem=recv_sems.at[3 * a + j], device_id=dev, device_id_type=MESH).start()
        token[...] = jnp.zeros(token.shape, token.dtype)

    hbm = lambda a: pltpu.HBM(a.shape, a.dtype)
    outs = _pcall(
        body, name=name,
        out_shape=(pltpu.SemaphoreType.DMA((3 * n,)), pltpu.SemaphoreType.DMA((3 * n,)), *[hbm(a) for a in srcs],
                   *[hbm(a) for a in lands], jax.ShapeDtypeStruct((8, LANE), F32)),
        in_specs=[_HBM] * (2 * n), out_specs=(_SEM, _SEM, *([_HBM] * (2 * n)), pl.BlockSpec(memory_space=pltpu.VMEM)),
        input_output_aliases={a: 2 + a for a in range(2 * n)},
        compiler_params=pltpu.CompilerParams(has_side_effects=_EFFECT),
    )(*[pltpu.with_memory_space_constraint(a, pltpu.HBM) for a in srcs],
      *[pltpu.with_memory_space_constraint(a, pltpu.HBM) for a in lands])
    return outs[0], outs[1], list(outs[2:2 + n]), list(outs[2 + n:2 + 2 * n]), outs[-1]


def _exchange_wait(state, after, *, gather, name):
    send_sems, recv_sems, srcs, lands, _ = state
    n = len(lands)

    def body(*refs):
        src_refs, land = refs[:n], refs[n:2 * n]
        ssem, rsem = refs[2 * n], refs[2 * n + 1]
        me, peers = _chip_peers()
        for a in range(n):
            for j, (dev, chip) in enumerate(peers):
                src = src_refs[a] if gather else src_refs[a].at[chip]
                cp = pltpu.make_async_remote_copy(src_ref=src, dst_ref=land[a].at[chip], send_sem=ssem.at[3 * a + j],
                                                  recv_sem=rsem.at[3 * a + j], device_id=dev, device_id_type=MESH)
                cp.wait_send()
                cp.wait_recv()

    outs = _pcall(
        body, name=name,
        out_shape=tuple(pltpu.HBM(a.shape, a.dtype) for a in list(srcs) + list(lands)),
        in_specs=[_HBM] * (2 * n) + [_SEM, _SEM, pl.BlockSpec(memory_space=pl.ANY)],
        out_specs=tuple([_HBM] * (2 * n)), input_output_aliases={a: a for a in range(2 * n)},
        compiler_params=pltpu.CompilerParams(has_side_effects=_EFFECT),
    )(*srcs, *lands, send_sems, recv_sems, after)
    return list(outs[:n]), list(outs[n:])


def _sibling_exchange(arrays, *, name):
    n = len(arrays)

    def body(*refs):
        ins, outs = refs[:n], refs[n:2 * n]
        send_sems, recv_sems = refs[2 * n:]
        x, y, c = lax.axis_index("x"), lax.axis_index("y"), lax.axis_index("c")
        cps = []
        for a in range(n):
            cp = pltpu.make_async_remote_copy(src_ref=ins[a], dst_ref=outs[a], send_sem=send_sems.at[a], recv_sem=recv_sems.at[a],
                                              device_id=(x, y, 1 - c), device_id_type=MESH)
            cp.start()
            cps.append(cp)
        for cp in cps:
            cp.wait()

    return _pcall(
        body, name=name, in_specs=[_ANY] * n, out_specs=[_ANY] * n,
        out_shape=[jax.ShapeDtypeStruct(a.shape, a.dtype) for a in arrays],
        scratch_shapes=[pltpu.SemaphoreType.DMA((n,)), pltpu.SemaphoreType.DMA((n,))],
        compiler_params=pltpu.CompilerParams(has_side_effects=True),
    )(*arrays)


def _allreduce_small(v, *, name):
    rows = v.shape[0]

    def body(v_ref, o_ref, slots, send_sems, recv_sems):
        x, y, c = lax.axis_index("x"), lax.axis_index("y"), lax.axis_index("c")
        me = 4 * x + 2 * y + c
        slots[me] = v_ref[...]
        cps = []
        for k in range(1, N_DEV):
            fx, fy, fc = (k >> 2) & 1, (k >> 1) & 1, k & 1
            peer = (x ^ fx, y ^ fy, c ^ fc)
            cp = pltpu.make_async_remote_copy(src_ref=v_ref, dst_ref=slots.at[me], send_sem=send_sems.at[k - 1],
                                              recv_sem=recv_sems.at[k - 1], device_id=peer, device_id_type=MESH)
            cp.start()
            cps.append((cp, peer))
        for k, (cp, (px, py, pc)) in enumerate(cps):
            pltpu.make_async_remote_copy(src_ref=v_ref, dst_ref=slots.at[4 * px + 2 * py + pc], send_sem=send_sems.at[k],
                                         recv_sem=recv_sems.at[k], device_id=(px, py, pc), device_id_type=MESH).wait_recv()
        for cp, _ in cps:
            cp.wait_send()
        tot = slots[0]
        for d in range(1, N_DEV):
            tot = tot + slots[d]
        o_ref[...] = tot

    vm = pl.BlockSpec(memory_space=pltpu.VMEM)
    return _pcall(
        body, name=name, in_specs=[vm], out_specs=vm, out_shape=jax.ShapeDtypeStruct(v.shape, F32),
        scratch_shapes=[pltpu.VMEM((N_DEV, rows, LANE), F32), pltpu.SemaphoreType.DMA((N_DEV - 1,)),
                        pltpu.SemaphoreType.DMA((N_DEV - 1,))],
        compiler_params=pltpu.CompilerParams(has_side_effects=True),
    )(v)


def _rows_block(rows, cols, itemsize=4, target_bytes=1 << 20, step=16):
    want = max(16, target_bytes // max(1, cols * itemsize))
    best = rows
    for t in range(step, rows, step):
        if rows % t == 0 and t <= want:
            best = t
    return best if best <= want or rows <= want else rows


def _sum_slots(recvs, parts, me, *, name):
    nl = len(recvs)
    shp = recvs[0].shape[1:]
    r3 = [r.reshape(N_CHIPS, -1, shp[-1]) for r in recvs]
    p3 = [q.reshape(N_CHIPS, -1, shp[-1]) for q in parts]
    rows, cols = r3[0].shape[1:]
    if rows % 16 == 0:
        rb, cb = _rows_block(rows, cols), cols
    else:
        rb, cb = rows, _tile(cols, LANE)
    ncb = cols // cb
    nblk = (rows // rb) * ncb
    per = N_CHIPS

    def body(me_ref, *refs):
        o_ref = refs[-1]
        lg = pl.program_id(0)
        for l in range(nl):
            r = refs[per * l:per * (l + 1)]

            @pl.when(lg == l)
            def _(r=r):
                tot = (r[0][...].astype(F32) + r[1][...].astype(F32)) + r[2][...].astype(F32)
                o_ref[...] = tot + r[3][...].astype(F32)

    def blk(l, lg, i):
        j = jnp.where(lg < l, 0, jnp.where(lg > l, nblk - 1, i))
        return j // ncb, j % ncb

    in_specs, args = [], []
    for l in range(nl):
        for k in range(1, N_CHIPS):
            in_specs.append(pl.BlockSpec((None, rb, cb), lambda lg, i, me_ref, l=l, k=k: (me_ref[0] ^ k, *blk(l, lg, i))))
            args.append(r3[l])
        in_specs.append(pl.BlockSpec((None, rb, cb), lambda lg, i, me_ref, l=l: (me_ref[0], *blk(l, lg, i))))
        args.append(p3[l])
    out = _pcall(
        body, name=name,
        grid_spec=pltpu.PrefetchScalarGridSpec(
            num_scalar_prefetch=1, grid=(nl, nblk), in_specs=in_specs,
            out_specs=pl.BlockSpec((None, rb, cb), lambda lg, i, me_ref: (lg, i // ncb, i % ncb))),
        out_shape=jax.ShapeDtypeStruct((nl, rows, cols), F32), compiler_params=_cparams(("arbitrary", "arbitrary")),
    )(me, *args)
    return out.reshape((nl,) + shp)


def _adamw(w, g_parts, m, v, *, layer0=0, prev=None, name):
    shp = w.shape
    two = lambda a: a.reshape(-1, shp[-1])
    rows, cols = two(w).shape
    grows = two(g_parts[0]).shape[0]
    per_layer = rows // shp[0] if layer0 or prev is not None or grows != rows else rows
    tb = _rows_block(per_layer, cols, target_bytes=1 << 19, step=8)
    off = (layer0 * per_layer) // tb if per_layer != rows else 0
    npart = len(g_parts)
    nprev = 0 if prev is None else 4

    def body(*refs):
        w_ref = refs[0]
        gp = refs[1:1 + npart]
        m_ref, v_ref = refs[1 + npart:3 + npart]
        g_out, d_out, m_out, v_out = refs[3 + npart + nprev:]
        g = gp[0][...]
        for r in gp[1:]:
            g = g + r[...]
        wv = w_ref[...]
        m2 = ADAM_B1 * m_ref[...] + (1.0 - ADAM_B1) * g
        v2 = ADAM_B2 * v_ref[...] + (1.0 - ADAM_B2) * jnp.square(g)
        m_hat = m2 / (1.0 - ADAM_B1 ** ADAM_STEP)
        v_hat = v2 / (1.0 - ADAM_B2 ** ADAM_STEP)
        g_out[...] = g
        d_out[...] = -ADAM_LR * (m_hat / (jnp.sqrt(v_hat) + ADAM_EPS) + ADAM_WD * wv)
        m_out[...] = m2
        v_out[...] = v2

    wspec = pl.BlockSpec((tb, cols), lambda i: (i + off, 0))
    gspec = pl.BlockSpec((tb, cols), lambda i: (i, 0))
    in_specs = [wspec] + [gspec] * npart + [wspec, wspec] + [_ANY] * nprev
    args = [two(w)] + [two(p) for p in g_parts] + [two(m), two(v)] + ([two(a) for a in prev] if prev is not None else [])
    outs = _pcall(
        body, name=name, grid=(grows // tb,), in_specs=in_specs, out_specs=[wspec] * 4,
        out_shape=[jax.ShapeDtypeStruct((rows, cols), F32)] * 4,
        input_output_aliases={3 + npart + k: k for k in range(nprev)}, compiler_params=_cparams(("parallel",)),
    )(*args)
    return [o.reshape(shp) for o in outs]


BIG = ("w_in", "b_w_uq", "b_w_ukv", "m_w_kv", "w_branch", "w_out", "w_up", "w_down")
_SHARD_AXIS = dict(w_in=2, b_w_uq=2, b_w_ukv=2, m_w_kv=1, w_branch=3, w_out=1, w_up=2, w_down=1)
SMALL = ("g_mix", "b_gate", "a_qn", "a_kn", "a_sink", "b_qa_norm", "b_kva_norm", "b_qn", "b_kn", "c_qn", "c_kn",
         "m_g_mem", "m_qn", "m_kn", "g_mlp")
WEIGHTS = ("g_mix", "w_in", "b_gate", "a_qn", "a_kn", "a_sink", "b_qa_norm", "b_kva_norm", "b_w_uq", "b_w_ukv", "b_qn", "b_kn",
           "c_qn", "c_kn", "m_g_mem", "m_w_kv", "m_qn", "m_kn", "w_branch", "w_out", "g_mlp", "w_up", "w_down")


def _to_entry(name, a):
    return jnp.swapaxes(a, -1, -2) if name == "w_in" else a


def _unshard(gathered, axis):
    moved = jnp.moveaxis(gathered, 0, axis)
    shp = list(gathered.shape[1:])
    shp[axis] *= N_CHIPS
    return moved.reshape(shp)


def _shard_parts(full, axis):
    shp = list(full.shape)
    shp[axis:axis + 1] = [N_CHIPS, shp[axis] // N_CHIPS]
    return jnp.moveaxis(full.reshape(shp), axis, 0)


def _pack_small(d):
    flat = jnp.concatenate([d[k].reshape(-1).astype(F32) for k in SMALL])
    n = flat.shape[0]
    pad = (-n) % (8 * LANE)
    return jnp.pad(flat, (0, pad)).reshape(-1, LANE)


def _unpack_small(packed, like):
    flat = packed.reshape(-1)
    out, off = {}, 0
    for k in SMALL:
        n = int(np.prod(like[k].shape))
        out[k] = flat[off:off + n].reshape(like[k].shape)
        off += n
    return out


def _train_step(x, mem, positions, loss_target, w, m, v):
    depth = w["g_mix"].shape[0]
    me = 2 * lax.axis_index("x") + lax.axis_index("y")
    landing = lambda a: lax.empty((N_CHIPS,) + a.shape, a.dtype)
    def with_own(land, mine):
        slot = lax.broadcasted_iota(jnp.int32, (N_CHIPS,) + (1,) * mine.ndim, 0)
        return jnp.where(slot == me, mine[None], land)

    gathers = {}
    for l in range(depth):
        for names in ((BIG[:1], BIG[1:]) if l == 0 else (BIG,)):
            own = [w[k][l].astype(BF16) for k in names]
            tag = f"{l}" if len(names) == len(BIG) else f"{l}_{names[0]}"
            gathers[(l, names)] = _exchange_start(own, [landing(a) for a in own], gather=True, name=f"gather_start{tag}")

    def gathered(l, names, after):
        tag = f"{l}" if len(names) == len(BIG) else f"{l}_{names[0]}"
        mine, lands = _exchange_wait(gathers[(l, names)], after, gather=True, name=f"gather_wait{tag}")
        full = {k: _unshard(with_own(g, o), _SHARD_AXIS[k] - 1) for k, g, o in zip(names, lands, mine)}
        return {kk: vv for kk, vv in _big_to_kernel_layout(full).items()}

    def get_big(l, after):
        if l == 0:
            return gathered(0, BIG[:1], after), lambda later: gathered(0, BIG[1:], later)
        return gathered(l, BIG, after), None

    scatters = {}

    def scatter(l, g, names, tag):
        gref = _big_grads_to_reference_layout(g, names)
        parts = [_to_entry(k, _shard_parts(gref[k], _SHARD_AXIS[k] - 1)).astype(BF16) for k in names]
        scatters[(l, names)] = _exchange_start(parts, [landing(p[0]) for p in parts], gather=False, name=f"scatter_start{tag}")
        return scatters[(l, names)][4]

    last_names = (BIG[1:], BIG[:1]) if depth > 1 else (BIG,)

    def early_grads(g):
        if len(last_names) > 1:
            scatter(0, g, last_names[0], "0_rest")

    def put_grads(l, g):
        names = last_names[-1] if l == 0 else BIG
        return scatter(l, g, names, f"{l}")[:1, :1]

    small = {k: w[k] for k in SMALL}
    loss, gx, grads = _local_step(x[0], mem[0], positions[0], small, loss_target[0], get_big, put_grads, early_grads)
    loss = lax.psum(loss, ("x", "y", "c"))
    me1 = me.reshape(1).astype(jnp.int32)
    res = {k: None for k in BIG}
    after = scatters[(0, last_names[-1])][4]
    for lo, n in ([(1, depth - 1), (0, 1)] if depth > 1 else [(0, 1)]):
        parts, recv = [], []
        for l in range(lo, lo + n):
            pl_, rl_ = {}, {}
            for names in ((BIG,) if l > 0 else last_names):
                tag = f"{l}" if (l > 0 or names == last_names[-1]) else "0_rest"
                ps_, rs_ = _exchange_wait(scatters[(l, names)], after, gather=False, name=f"scatter_wait{tag}")
                pl_.update(zip(names, ps_))
                rl_.update(zip(names, rs_))
            parts.append([pl_[k] for k in BIG])
            recv.append([rl_[k] for k in BIG])
        mine = [_sum_slots([recv[l][a] for l in range(n)], [parts[l][a] for l in range(n)], me1, name=f"sum_{k}_{lo}")
                for a, k in enumerate(BIG)]
        theirs = _sibling_exchange(mine, name=f"sibling_grads{lo}")
        for k, p, q in zip(BIG, mine, theirs):
            res[k] = _adamw(_to_entry(k, w[k]), [p, q], _to_entry(k, m[k]), _to_entry(k, v[k]), layer0=lo, prev=res[k],
                            name=f"adamw_{k}_{lo}")
        after = res[BIG[-1]][0]
    res = {k: [_to_entry(k, a) for a in r] for k, r in res.items()}
    gsmall = _small_grads_to_reference_layout(grads)
    g_small = _allreduce_small(_pack_small(gsmall), name="allreduce_small")
    packed = _adamw(_pack_small(small), [g_small], _pack_small({k: m[k] for k in SMALL}), _pack_small({k: v[k] for k in SMALL}),
                    name="adamw_small")
    unpacked = [_unpack_small(p, small) for p in packed]
    for k in SMALL:
        res[k] = [u[k] for u in unpacked]
    outs = [loss, gx[None]]
    for i in range(4):
        outs += [res[k][i] for k in WEIGHTS]
    return tuple(outs)

def kernel(x, mem, positions, g_mix, w_in, b_gate, a_qn, a_kn, a_sink, b_qa_norm, b_kva_norm, b_w_uq, b_w_ukv, b_qn, b_kn, c_qn, c_kn, m_g_mem, m_w_kv, m_qn, m_kn, w_branch, w_out, g_mlp, w_up, w_down, loss_target, m_g_mix, m_w_in, m_b_gate, m_a_qn, m_a_kn, m_a_sink, m_b_qa_norm, m_b_kva_norm, m_b_w_uq, m_b_w_ukv, m_b_qn, m_b_kn, m_c_qn, m_c_kn, m_m_g_mem, m_m_w_kv, m_m_qn, m_m_kn, m_w_branch, m_w_out, m_g_mlp, m_w_up, m_w_down, v_g_mix, v_w_in, v_b_gate, v_a_qn, v_a_kn, v_a_sink, v_b_qa_norm, v_b_kva_norm, v_b_w_uq, v_b_w_ukv, v_b_qn, v_b_kn, v_c_qn, v_c_kn, v_m_g_mem, v_m_w_kv, v_m_qn, v_m_kn, v_w_branch, v_w_out, v_g_mlp, v_w_up, v_w_down):
    args = dict(locals())
    w = {k: args[k] for k in WEIGHTS}
    m = {k: args["m_" + k] for k in WEIGHTS}
    v = {k: args["v_" + k] for k in WEIGHTS}
    return _train_step(x, mem, positions, loss_target, w, m, v)
```

```python
import functools
import math

import jax
import jax.numpy as jnp
import numpy as np
from jax import lax
from jax.experimental import pallas as pl
from jax.experimental.pallas import tpu as pltpu

F32 = jnp.float32
BF16 = jnp.bfloat16

DEPTH = 4
BLOCK = 128
ROPE_THETA = 10000.0
EPS = 1e-6
NEG = -1e30
A_HEADS, A_KV_HEADS, A_HD, A_WINDOW = 8, 2, 64, 128
B_HEADS, B_Q_LORA, B_KV_LORA, B_NOPE, B_ROPE, B_V = 8, 384, 256, 64, 32, 64
C_PATTERNS = ((128, 1), (512, 4), (2048, 16))
C_HEADS, C_HD = 8, 64
M_HEADS, M_HD = 4, 128
N_BRANCH, BRANCH_W = 4, 512
ADAM_LR, ADAM_B1, ADAM_B2, ADAM_EPS, ADAM_WD, ADAM_STEP = 0.001, 0.9, 0.999, 1e-08, 0.01, 10

LANE = 128
VMEM_LIMIT = 56 * 1024 * 1024


def _pcall(body, **kw):
    return pl.pallas_call(body, **kw)


def _cparams(sem):
    return pltpu.CompilerParams(dimension_semantics=sem, vmem_limit_bytes=VMEM_LIMIT)


def _tile(n, target):
    if n <= target:
        return n
    best = None
    for t in range(LANE, target + 1, LANE):
        if n % t == 0:
            best = t
    return best if best is not None else n


def _mm(a, b, *, ta=False, tb=False, out_dtype=F32, pro_a=None, epi=None, extra=None, name,
        tm=1024, tn=1024, tk=1024):
    if ta:
        K, M = a.shape
    else:
        M, K = a.shape
    if tb:
        N, K2 = b.shape
    else:
        K2, N = b.shape
    assert K == K2, (a.shape, b.shape, ta, tb)
    tm, tn, tk = _tile(M, tm), _tile(N, tn), _tile(K, tk)
    nk = K // tk
    a_spec = pl.BlockSpec((tk, tm), lambda i, j, k: (k, i)) if ta else pl.BlockSpec((tm, tk), lambda i, j, k: (i, k))
    b_spec = pl.BlockSpec((tn, tk), lambda i, j, k: (j, k)) if tb else pl.BlockSpec((tk, tn), lambda i, j, k: (k, j))
    o_spec = pl.BlockSpec((tm, tn), lambda i, j, k: (i, j))
    dims = (((0,) if ta else (1,), (1,) if tb else (0,)), ((), ()))
    has_extra = extra is not None

    def body(*refs):
        if has_extra:
            a_ref, b_ref, e_ref, o_ref, acc_ref = refs
        else:
            a_ref, b_ref, o_ref, acc_ref = refs
            e_ref = None
        k = pl.program_id(2)
        av = a_ref[...]
        if pro_a is not None:
            av = pro_a(av.astype(F32))
        part = lax.dot_general(av.astype(BF16), b_ref[...].astype(BF16), dims, preferred_element_type=F32)

        @pl.when(k == 0)
        def _():
            acc_ref[...] = part

        @pl.when(k > 0)
        def _():
            acc_ref[...] += part

        @pl.when(k == nk - 1)
        def _():
            r = acc_ref[...]
            if epi is not None:
                r = epi(r, e_ref[...]) if has_extra else epi(r)
            o_ref[...] = r.astype(out_dtype)

    in_specs = [a_spec, b_spec] + ([o_spec] if has_extra else [])
    args = (a, b) + ((extra,) if has_extra else ())
    return _pcall(
        body, name=name, grid=(M // tm, N // tn, nk), in_specs=in_specs, out_specs=o_spec,
        out_shape=jax.ShapeDtypeStruct((M, N), out_dtype),
        scratch_shapes=[pltpu.VMEM((tm, tn), F32)],
        compiler_params=_cparams(("parallel", "parallel", "arbitrary")),
    )(*args)


def _piece(p):
    if isinstance(p, tuple):
        return p
    return (p, p.shape[1], 0)


def _row_spec(width, idx, tb):
    return pl.BlockSpec((tb, width), lambda i, idx=idx: (i, idx))


def _full_spec(arr):
    nd = arr.ndim
    return pl.BlockSpec(arr.shape, lambda i, nd=nd: (0,) * nd)


def _rowmap(f, rows, params, outs, *, tb, name):
    rows = [_piece(p) for p in rows]
    R = rows[0][0].shape[0]
    tb = min(tb, R)
    nr, npar, nout = len(rows), len(params), len(outs)

    def body(*refs):
        rv = [r[...] for r in refs[:nr]]
        pv = [r[...] for r in refs[nr:nr + npar]]
        res = f(*rv, *pv)
        for o_ref, val in zip(refs[nr + npar:], res):
            o_ref[...] = val.astype(o_ref.dtype)

    return _pcall(
        body, name=name, grid=(R // tb,),
        in_specs=[_row_spec(w, idx, tb) for (_, w, idx) in rows] + [_full_spec(p) for p in params],
        out_specs=[_row_spec(w, 0, tb) for (w, _) in outs],
        out_shape=[jax.ShapeDtypeStruct((R, w), dt) for (w, dt) in outs],
        compiler_params=_cparams(("parallel",)),
    )(*[r[0] for r in rows], *params)


def _rowmap_bwd(f, rows, params, couts, *, diff, out_dtypes, adds=None, tb, name):
    rows = [_piece(p) for p in rows]
    couts = [_piece(p) for p in couts]
    R = rows[0][0].shape[0]
    tb = min(tb, R)
    nr, npar, nc = len(rows), len(params), len(couts)
    didx = [i for i, d in enumerate(diff) if d]
    adds = [None] * len(didx) if adds is None else adds
    add_ops = [_piece(a) for a in adds if a is not None]
    na = len(add_ops)

    def body(*refs):
        rv = [r[...] for r in refs[:nr]]
        pv = [r[...] for r in refs[nr:nr + npar]]
        cv = [r[...] for r in refs[nr + npar:nr + npar + nc]]
        av = [r[...] for r in refs[nr + npar + nc:nr + npar + nc + na]]
        o_refs = refs[nr + npar + nc + na:]
        drow_refs, dpar_refs = o_refs[:len(didx)], o_refs[len(didx):]
        nondiff = {i: rv[i] for i in range(nr) if not diff[i]}

        def g(*dv):
            full = []
            it = iter(dv[:len(didx)])
            for i in range(nr):
                full.append(nondiff[i] if i in nondiff else next(it))
            return f(*full, *dv[len(didx):])

        res, vjp = jax.vjp(g, *[rv[i].astype(F32) for i in didx], *pv)
        cts = tuple(c.astype(r.dtype) for c, r in zip(cv, res))
        grads = vjp(cts)
        ai = 0
        for n, o_ref in enumerate(drow_refs):
            val = grads[n]
            if adds[n] is not None:
                val = val + av[ai].astype(F32)
                ai += 1
            o_ref[...] = val.astype(o_ref.dtype)
        first = pl.program_id(0) == 0
        for n, o_ref in enumerate(dpar_refs):
            gp = grads[len(didx) + n].astype(F32)

            @pl.when(first)
            def _(o_ref=o_ref, gp=gp):
                o_ref[...] = gp

            @pl.when(jnp.logical_not(first))
            def _(o_ref=o_ref, gp=gp):
                o_ref[...] += gp

    outs = _pcall(
        body, name=name, grid=(R // tb,),
        in_specs=([_row_spec(w, idx, tb) for (_, w, idx) in rows] + [_full_spec(p) for p in params]
                  + [_row_spec(w, idx, tb) for (_, w, idx) in couts] + [_row_spec(w, idx, tb) for (_, w, idx) in add_ops]),
        out_specs=[_row_spec(rows[i][1], 0, tb) for i in didx] + [_full_spec(p) for p in params],
        out_shape=([jax.ShapeDtypeStruct((R, rows[i][1]), dt) for i, dt in zip(didx, out_dtypes)]
                   + [jax.ShapeDtypeStruct(p.shape, F32) for p in params]),
        compiler_params=_cparams(("arbitrary",)),
    )(*[r[0] for r in rows], *params, *[c[0] for c in couts], *[a[0] for a in add_ops])
    return outs[:len(didx)], outs[len(didx):]


@functools.partial(jax.custom_vjp, nondiff_argnums=(1,))
def _lane_roll(x, shift):
    return pltpu.roll(x, shift % x.shape[-1], axis=x.ndim - 1)


def _lane_roll_fwd(x, shift):
    return _lane_roll(x, shift), None


def _lane_roll_bwd(shift, _, g):
    return (_lane_roll(g, -shift),)


_lane_roll.defvjp(_lane_roll_fwd, _lane_roll_bwd)


def _group_matrix(kind):
    r = lax.broadcasted_iota(jnp.int32, (LANE, LANE), 0)
    c = lax.broadcasted_iota(jnp.int32, (LANE, LANE), 1)
    if kind == "mla":
        gid = lambda l: jnp.where(l < B_NOPE, 0, jnp.where(l < B_NOPE + B_ROPE, 1, 2))
        inv = jnp.where(c < B_NOPE, 1.0 / B_NOPE, 1.0 / B_ROPE)
        return jnp.where(gid(r) == gid(c), inv, 0.0).astype(BF16)
    return jnp.where(r // kind == c // kind, 1.0 / kind, 0.0).astype(BF16)


@functools.partial(jax.custom_vjp, nondiff_argnums=(1,))
def _group_mean(xx, kind):
    gm = _group_matrix(kind)
    outs = []
    for b in range(xx.shape[-1] // LANE):
        t = xx[:, b * LANE:(b + 1) * LANE]
        hi = t.astype(BF16)
        lo = (t - hi.astype(F32)).astype(BF16)
        outs.append(jnp.dot(hi, gm, preferred_element_type=F32) + jnp.dot(lo, gm, preferred_element_type=F32))
    return jnp.concatenate(outs, axis=1) if len(outs) > 1 else outs[0]


def _group_mean_fwd(xx, kind):
    return _group_mean(xx, kind), None


def _group_mean_bwd(kind, _, g):
    return (_group_mean(g, kind),)


_group_mean.defvjp(_group_mean_fwd, _group_mean_bwd)


def _exact_dot(x, m):
    return jnp.dot(x, m, precision=lax.Precision.HIGHEST, preferred_element_type=F32)


def _head_norm(x, gain_tiled, group):
    return x * lax.rsqrt(_group_mean(x * x, group) + EPS) * gain_tiled


def _row_norm(x, gain):
    ms = jnp.mean(x * x, axis=-1, keepdims=True)
    return x * lax.rsqrt(ms + EPS) * gain


def _rope(x, cos, sin, hd):
    half = hd // 2
    lane = lax.broadcasted_iota(jnp.int32, x.shape, x.ndim - 1) % hd
    other = jnp.where(lane < half, -_lane_roll(x, -half), _lane_roll(x, half))
    return x * cos + other * sin


class _AttnGeom:
    def __init__(self, mode, lq, lk, max_dist=0):
        self.mode, self.lq, self.lk, self.max_dist = mode, lq, lk, max_dist
        if mode == "band":
            self.bq = self.bk = BLOCK
            self.nt_q = 2
            self.nt_k = 2
        elif mode == "causal":
            self.bq = self.bk = min(256, lq)
            self.nt_q = lk // self.bk
            self.nt_k = lq // self.bq
        else:
            self.bq = min(512, lq)
            self.bk = lk
            self.nt_q = 1
            self.nt_k = lq // self.bq
        self.nq, self.nk = lq // self.bq, lk // self.bk

    def kv_block(self, i, t):
        if self.mode == "band":
            return jnp.maximum(i - t, 0)
        if self.mode == "causal":
            return jnp.minimum(t, i)
        return 0 * i

    def kv_active(self, i, t):
        if self.mode == "band":
            return i - t >= 0
        if self.mode == "causal":
            return t <= i
        return None

    def q_block(self, j, t):
        if self.mode == "band":
            return jnp.minimum(j + t, self.nq - 1)
        if self.mode == "causal":
            return jnp.maximum(t, j)
        return t

    def q_active(self, j, t):
        if self.mode == "band":
            return j + t <= self.nq - 1
        if self.mode == "causal":
            return t >= j
        return None

    def mask(self, qb, kb):
        if self.mode == "full":
            return None
        qp = qb * self.bq + lax.broadcasted_iota(jnp.int32, (self.bq, self.bk), 0)
        kp = kb * self.bk + lax.broadcasted_iota(jnp.int32, (self.bq, self.bk), 1)
        d = qp - kp
        if self.mode == "band":
            return (d >= 0) & (d <= self.max_dist)
        return d >= 0


def _when(cond, fn):
    if cond is None:
        fn()
    else:
        pl.when(cond)(fn)


def _dil_view(p, dil):
    arr, w, idx = _piece(p)
    R, C = arr.shape
    assert C % w == 0, (C, w)
    return arr.reshape(R // dil, dil * C), w, idx, C // w


def _seq_spec(view, rows, blk_fn):
    _, w, idx, cpw = view
    return pl.BlockSpec((rows, w), lambda s, i, t: (blk_fn(i, t), s * cpw + idx))


_NT = (((1,), (1,)), ((), ()))
_TN = (((0,), (0,)), ((), ()))


def _scores(geom, scale, q, k, qp, kp, h, g, hd, rope, qb, kb):
    s = lax.dot_general(q[:, h * hd:(h + 1) * hd], k[:, g * hd:(g + 1) * hd], _NT, preferred_element_type=F32)
    if rope:
        s = s + lax.dot_general(qp[:, h * rope:(h + 1) * rope], kp[:, :rope], _NT, preferred_element_type=F32)
    s = s * scale
    m = geom.mask(qb, kb)
    return s, m


def _attn_fwd(geom, q, k, v, *, hq, hk, hd, hdv, scale, dil=1, qp=None, kp=None, rope=0, name):
    qv, kv, vv = _dil_view(q, dil), _dil_view(k, dil), _dil_view(v, dil)
    R = _piece(q)[0].shape[0]
    grp = hq // hk
    bq, bk, nt = geom.bq, geom.bk, geom.nt_q
    ops = [qv, kv, vv]
    specs = [_seq_spec(qv, bq, lambda i, t: i), _seq_spec(kv, bk, geom.kv_block), _seq_spec(vv, bk, geom.kv_block)]
    if rope:
        qpv, kpv = _dil_view(qp, dil), _dil_view(kp, dil)
        ops += [qpv, kpv]
        specs += [_seq_spec(qpv, bq, lambda i, t: i), _seq_spec(kpv, bk, geom.kv_block)]
    ow = hq * hdv
    o_view = (None, ow, 0, 1)
    o_spec = pl.BlockSpec((bq, ow), lambda s, i, t: (i, s))

    def body(*refs):
        if rope:
            q_ref, k_ref, v_ref, qp_ref, kp_ref, o_ref, lse_ref, m_sc, l_sc, acc_sc = refs
        else:
            q_ref, k_ref, v_ref, o_ref, lse_ref, m_sc, l_sc, acc_sc = refs
            qp_ref = kp_ref = None
        i, t = pl.program_id(1), pl.program_id(2)

        @pl.when(t == 0)
        def _():
            m_sc[...] = jnp.full(m_sc.shape, NEG, F32)
            l_sc[...] = jnp.zeros(l_sc.shape, F32)
            acc_sc[...] = jnp.zeros(acc_sc.shape, F32)

        def step():
            qa, ka, va = q_ref[...].astype(BF16), k_ref[...].astype(BF16), v_ref[...].astype(BF16)
            qpa = qp_ref[...].astype(BF16) if rope else None
            kpa = kp_ref[...].astype(BF16) if rope else None
            kb = geom.kv_block(i, t)
            for h in range(hq):
                g = h // grp
                s, msk = _scores(geom, scale, qa, ka, qpa, kpa, h, g, hd, rope, i, kb)
                if msk is not None:
                    s = jnp.where(msk, s, NEG)
                m_old = m_sc[h]
                m_new = jnp.maximum(m_old, jnp.max(s, axis=1, keepdims=True))
                p = jnp.exp(s - m_new)
                alpha = jnp.exp(m_old - m_new)
                l_sc[h] = alpha * l_sc[h] + jnp.sum(p, axis=1, keepdims=True)
                pv = jnp.dot(p.astype(BF16), va[:, g * hdv:(g + 1) * hdv], preferred_element_type=F32)
                acc_sc[h] = alpha * acc_sc[h] + pv
                m_sc[h] = m_new

        _when(geom.kv_active(i, t), step)

        @pl.when(t == nt - 1)
        def _():
            for h in range(hq):
                l = l_sc[h]
                o_ref[:, h * hdv:(h + 1) * hdv] = acc_sc[h] / l
                lse_ref[:, h * hdv:(h + 1) * hdv] = jnp.broadcast_to(m_sc[h] + jnp.log(l), (bq, hdv))

    o, lse = _pcall(
        body, name=name, grid=(dil, geom.nq, nt), in_specs=specs, out_specs=[o_spec, o_spec],
        out_shape=[jax.ShapeDtypeStruct((R // dil, dil * ow), F32)] * 2,
        scratch_shapes=[pltpu.VMEM((hq, bq, 1), F32), pltpu.VMEM((hq, bq, 1), F32), pltpu.VMEM((hq, bq, hdv), F32)],
        compiler_params=_cparams(("parallel", "parallel", "arbitrary")),
    )(*[o_[0] for o_ in ops])
    return o.reshape(R, ow), lse.reshape(R, ow)


def _attn_dq(geom, q, k, v, do, o, lse, dlse, *, hq, hk, hd, hdv, scale, dil=1, qp=None, kp=None, rope=0,
             out_dtype=F32, name):
    qv, kv, vv = _dil_view(q, dil), _dil_view(k, dil), _dil_view(v, dil)
    dov, ov, lv = _dil_view(do, dil), _dil_view(o, dil), _dil_view(lse, dil)
    R = _piece(q)[0].shape[0]
    grp = hq // hk
    bq, bk, nt = geom.bq, geom.bk, geom.nt_q
    qi = lambda i, t: i
    ops = [qv, kv, vv, dov, ov, lv]
    specs = [_seq_spec(qv, bq, qi), _seq_spec(kv, bk, geom.kv_block), _seq_spec(vv, bk, geom.kv_block),
             _seq_spec(dov, bq, qi), _seq_spec(ov, bq, qi), _seq_spec(lv, bq, qi)]
    has_dl = dlse is not None
    if has_dl:
        dlv = _dil_view(dlse, dil)
        ops.append(dlv)
        specs.append(_seq_spec(dlv, bq, qi))
    if rope:
        qpv, kpv = _dil_view(qp, dil), _dil_view(kp, dil)
        ops += [qpv, kpv]
        specs += [_seq_spec(qpv, bq, qi), _seq_spec(kpv, bk, geom.kv_block)]
    qw = hq * hd
    out_specs = [pl.BlockSpec((bq, qw), lambda s, i, t: (i, s))]
    out_shape = [jax.ShapeDtypeStruct((R // dil, dil * qw), out_dtype)]
    scratch = [pltpu.VMEM((hq, bq, 1), F32), pltpu.VMEM((hq, bq, hd), F32)]
    if rope:
        out_specs.append(pl.BlockSpec((bq, hq * rope), lambda s, i, t: (i, s)))
        out_shape.append(jax.ShapeDtypeStruct((R // dil, dil * hq * rope), out_dtype))
        scratch.append(pltpu.VMEM((hq, bq, rope), F32))

    def body(*refs):
        refs = list(refs)
        q_ref, k_ref, v_ref, do_ref, o_ref, l_ref = refs[:6]
        pos = 6
        dl_ref = None
        if has_dl:
            dl_ref = refs[pos]
            pos += 1
        qp_ref = kp_ref = None
        if rope:
            qp_ref, kp_ref = refs[pos:pos + 2]
            pos += 2
        dq_ref = refs[pos]
        pos += 1
        dqp_ref = None
        if rope:
            dqp_ref = refs[pos]
            pos += 1
        dl_sc, dq_sc = refs[pos:pos + 2]
        dqp_sc = refs[pos + 2] if rope else None
        i, t = pl.program_id(1), pl.program_id(2)

        @pl.when(t == 0)
        def _():
            dov_, ov_ = do_ref[...].astype(F32), o_ref[...].astype(F32)
            prod = dov_ * ov_
            for h in range(hq):
                d = jnp.sum(prod[:, h * hdv:(h + 1) * hdv], axis=1, keepdims=True)
                if has_dl:
                    d = d - jnp.sum(dl_ref[:, h * hdv:(h + 1) * hdv].astype(F32), axis=1, keepdims=True)
                dl_sc[h] = d
            dq_sc[...] = jnp.zeros(dq_sc.shape, F32)
            if rope:
                dqp_sc[...] = jnp.zeros(dqp_sc.shape, F32)

        def step():
            qa, ka, va = q_ref[...].astype(BF16), k_ref[...].astype(BF16), v_ref[...].astype(BF16)
            doa = do_ref[...].astype(BF16)
            qpa = qp_ref[...].astype(BF16) if rope else None
            kpa = kp_ref[...].astype(BF16) if rope else None
            kb = geom.kv_block(i, t)
            for h in range(hq):
                g = h // grp
                s, msk = _scores(geom, scale, qa, ka, qpa, kpa, h, g, hd, rope, i, kb)
                p = jnp.exp(s - l_ref[:, h * hdv:h * hdv + 1])
                if msk is not None:
                    p = jnp.where(msk, p, 0.0)
                dp = lax.dot_general(doa[:, h * hdv:(h + 1) * hdv], va[:, g * hdv:(g + 1) * hdv], _NT,
                                     preferred_element_type=F32)
                ds = (p * (dp - dl_sc[h]) * scale).astype(BF16)
                dq_sc[h] += jnp.dot(ds, ka[:, g * hd:(g + 1) * hd], preferred_element_type=F32)
                if rope:
                    dqp_sc[h] += jnp.dot(ds, kpa[:, :rope], preferred_element_type=F32)

        _when(geom.kv_active(i, t), step)

        @pl.when(t == nt - 1)
        def _():
            for h in range(hq):
                dq_ref[:, h * hd:(h + 1) * hd] = dq_sc[h].astype(dq_ref.dtype)
                if rope:
                    dqp_ref[:, h * rope:(h + 1) * rope] = dqp_sc[h].astype(dqp_ref.dtype)

    outs = _pcall(
        body, name=name, grid=(dil, geom.nq, nt), in_specs=specs, out_specs=out_specs, out_shape=out_shape,
        scratch_shapes=scratch, compiler_params=_cparams(("parallel", "parallel", "arbitrary")),
    )(*[o_[0] for o_ in ops])
    dq = outs[0].reshape(R, qw)
    if rope:
        return dq, outs[1].reshape(R, hq * rope)
    return dq


def _attn_dkv(geom, q, k, v, do, o, lse, dlse, *, hq, hk, hd, hdv, scale, dil=1, qp=None, kp=None, rope=0,
              out_dtype=F32, name):
    qv, kv, vv = _dil_view(q, dil), _dil_view(k, dil), _dil_view(v, dil)
    dov, ov, lv = _dil_view(do, dil), _dil_view(o, dil), _dil_view(lse, dil)
    Rk = _piece(k)[0].shape[0]
    grp = hq // hk
    bq, bk, nt = geom.bq, geom.bk, geom.nt_k
    kj = lambda j, t: j
    ops = [qv, kv, vv, dov, ov, lv]
    specs = [_seq_spec(qv, bq, geom.q_block), _seq_spec(kv, bk, kj), _seq_spec(vv, bk, kj),
             _seq_spec(dov, bq, geom.q_block), _seq_spec(ov, bq, geom.q_block), _seq_spec(lv, bq, geom.q_block)]
    has_dl = dlse is not None
    if has_dl:
        dlv = _dil_view(dlse, dil)
        ops.append(dlv)
        specs.append(_seq_spec(dlv, bq, geom.q_block))
    if rope:
        qpv, kpv = _dil_view(qp, dil), _dil_view(kp, dil)
        ops += [qpv, kpv]
        specs += [_seq_spec(qpv, bq, geom.q_block), _seq_spec(kpv, bk, kj)]
    kw, vw = hk * hd, hk * hdv
    out_specs = [pl.BlockSpec((bk, kw), lambda s, j, t: (j, s)), pl.BlockSpec((bk, vw), lambda s, j, t: (j, s))]
    out_shape = [jax.ShapeDtypeStruct((Rk // dil, dil * kw), out_dtype), jax.ShapeDtypeStruct((Rk // dil, dil * vw), out_dtype)]
    scratch = [pltpu.VMEM((hk, bk, hd), F32), pltpu.VMEM((hk, bk, hdv), F32)]
    if rope:
        out_specs.append(pl.BlockSpec((bk, LANE), lambda s, j, t: (j, s)))
        out_shape.append(jax.ShapeDtypeStruct((Rk // dil, dil * LANE), out_dtype))
        scratch.append(pltpu.VMEM((bk, rope), F32))

    def body(*refs):
        refs = list(refs)
        q_ref, k_ref, v_ref, do_ref, o_ref, l_ref = refs[:6]
        pos = 6
        dl_ref = None
        if has_dl:
            dl_ref = refs[pos]
            pos += 1
        qp_ref = kp_ref = None
        if rope:
            qp_ref, kp_ref = refs[pos:pos + 2]
            pos += 2
        dk_ref, dv_ref = refs[pos:pos + 2]
        pos += 2
        dkp_ref = None
        if rope:
            dkp_ref = refs[pos]
            pos += 1
        dk_sc, dv_sc = refs[pos:pos + 2]
        dkp_sc = refs[pos + 2] if rope else None
        j, t = pl.program_id(1), pl.program_id(2)

        @pl.when(t == 0)
        def _():
            dk_sc[...] = jnp.zeros(dk_sc.shape, F32)
            dv_sc[...] = jnp.zeros(dv_sc.shape, F32)
            if rope:
                dkp_sc[...] = jnp.zeros(dkp_sc.shape, F32)

        def step():
            qa, ka, va = q_ref[...].astype(BF16), k_ref[...].astype(BF16), v_ref[...].astype(BF16)
            dof = do_ref[...].astype(F32)
            doa = dof.astype(BF16)
            prod = dof * o_ref[...].astype(F32)
            qpa = qp_ref[...].astype(BF16) if rope else None
            kpa = kp_ref[...].astype(BF16) if rope else None
            qb = geom.q_block(j, t)
            for h in range(hq):
                g = h // grp
                s, msk = _scores(geom, scale, qa, ka, qpa, kpa, h, g, hd, rope, qb, j)
                p = jnp.exp(s - l_ref[:, h * hdv:h * hdv + 1])
                if msk is not None:
                    p = jnp.where(msk, p, 0.0)
                delta = jnp.sum(prod[:, h * hdv:(h + 1) * hdv], axis=1, keepdims=True)
                if has_dl:
                    delta = delta - jnp.sum(dl_ref[:, h * hdv:(h + 1) * hdv].astype(F32), axis=1, keepdims=True)
                do_h = doa[:, h * hdv:(h + 1) * hdv]
                dv_sc[g] += lax.dot_general(p.astype(BF16), do_h, _TN, preferred_element_type=F32)
                dp = lax.dot_general(do_h, va[:, g * hdv:(g + 1) * hdv], _NT, preferred_element_type=F32)
                ds = (p * (dp - delta) * scale).astype(BF16)
                dk_sc[g] += lax.dot_general(ds, qa[:, h * hd:(h + 1) * hd], _TN, preferred_element_type=F32)
                if rope:
                    dkp_sc[...] += lax.dot_general(ds, qpa[:, h * rope:(h + 1) * rope], _TN, preferred_element_type=F32)

        _when(geom.q_active(j, t), step)

        @pl.when(t == nt - 1)
        def _():
            for g in range(hk):
                dk_ref[:, g * hd:(g + 1) * hd] = dk_sc[g].astype(dk_ref.dtype)
                dv_ref[:, g * hdv:(g + 1) * hdv] = dv_sc[g].astype(dv_ref.dtype)
            if rope:
                dkp_ref[...] = jnp.zeros(dkp_ref.shape, dkp_ref.dtype)
                dkp_ref[:, :rope] = dkp_sc[...].astype(dkp_ref.dtype)

    outs = _pcall(
        body, name=name, grid=(dil, geom.nk, nt), in_specs=specs, out_specs=out_specs, out_shape=out_shape,
        scratch_shapes=scratch, compiler_params=_cparams(("parallel", "parallel", "arbitrary")),
    )(*[o_[0] for o_ in ops])
    dk, dv = outs[0].reshape(Rk, kw), outs[1].reshape(Rk, vw)
    if rope:
        return dk, dv, outs[2].reshape(Rk, LANE)
    return dk, dv


class _Viewed(tuple):
    pass


class _BandPlan:
    def __init__(self, S, dil, max_dist):
        self.L, self.dil, self.max_dist = S // dil, dil, max_dist
        self.nblk = self.L // BLOCK
        self.nb = min(4, self.nblk)
        self.ns = min(dil, max(1, 4 // self.nb))
        self.grid = (dil // self.ns, self.nblk // self.nb)
        self.rows = self.nb * BLOCK

    def view(self, p):
        if isinstance(p, _Viewed):
            return p
        arr, w, idx = _piece(p)
        R, C = arr.shape
        assert C % w == 0 and (self.ns == 1 or (C == w and idx == 0)), (C, w, idx, self.ns)
        return _Viewed((arr.reshape(R // self.dil, self.dil * C), w, idx, C // w))

    def main(self, view):
        _, w, idx, cpw = view
        if self.ns == 1:
            return pl.BlockSpec((self.rows, w), lambda s, i: (i, s * cpw + idx))
        return pl.BlockSpec((self.rows, self.ns * w), lambda s, i: (i, s))

    def edge(self, view, nxt):
        _, w, idx, cpw = view
        nb, last = self.nb, self.nblk - 1
        rb = (lambda i: jnp.minimum((i + 1) * nb, last)) if nxt else (lambda i: jnp.maximum(i * nb - 1, 0))
        if self.ns == 1:
            return pl.BlockSpec((BLOCK, w), lambda s, i: (rb(i), s * cpw + idx))
        return pl.BlockSpec((BLOCK, self.ns * w), lambda s, i: (rb(i), s))

    def out(self, w):
        return pl.BlockSpec((self.rows, self.ns * w), lambda s, i: (i, s))

    def masks(self):
        qi = lax.broadcasted_iota(jnp.int32, (BLOCK, BLOCK), 0)
        kj = lax.broadcasted_iota(jnp.int32, (BLOCK, BLOCK), 1)
        return kj <= qi, (qi - kj + BLOCK) <= self.max_dist


def _half(e, rows=BLOCK):
    lane = lax.broadcasted_iota(jnp.int32, (rows, LANE), 1)
    return (lane < LANE // 2) if e == 0 else (lane >= LANE // 2)


def _swap_halves(t):
    return pltpu.roll(t, LANE // 2, axis=1)


def _kv_group(ref, rows, col0, pr, grp, hq, hk):
    if hk == hq:
        return ref[rows, col0 + pr * LANE:col0 + (pr + 1) * LANE].astype(BF16)
    g = (2 * pr) // grp
    t = ref[rows, col0 + (g // 2) * LANE:col0 + (g // 2 + 1) * LANE].astype(BF16)
    sw = _swap_halves(t)
    h0 = _half(0, t.shape[0])
    return jnp.where(h0, t, sw) if g % 2 == 0 else jnp.where(h0, sw, t)


def _stack_heads(t2):
    z = jnp.zeros_like(t2)
    h0 = _half(0, t2.shape[0])
    return jnp.concatenate([jnp.where(h0, t2, z), jnp.where(h0, z, t2)], axis=0)


def _unstack_heads(t, rows=BLOCK):
    return jnp.where(_half(0, rows), t[:rows], t[rows:])


def _per_head_col(t2):
    return jnp.concatenate([t2[:, :1], t2[:, LANE // 2:LANE // 2 + 1]], axis=0)


def _per_head_sum(t2):
    h0 = _half(0, t2.shape[0])
    return jnp.concatenate([jnp.sum(jnp.where(h0, t2, 0.0), axis=1, keepdims=True),
                            jnp.sum(jnp.where(h0, 0.0, t2), axis=1, keepdims=True)], axis=0)


def _band_fwd(q, k, v, *, S, dil, max_dist, hq, hk, scale, name):
    hd = 64
    plan = _BandPlan(S, dil, max_dist)
    qv, kv, vv = plan.view(q), plan.view(k), plan.view(v)
    wq, wk = hq * hd, hk * hd
    grp = hq // hk
    ns, nb = plan.ns, plan.nb

    def body(q_ref, k_ref, kp_ref, v_ref, vp_ref, o_ref, l_ref):
        i = pl.program_id(1)
        m_cur, m_band = plan.masks()
        has_prev = i > 0
        for sg in range(ns):
            for b in range(nb):
                rows = slice(b * BLOCK, (b + 1) * BLOCK)
                prows = slice((b - 1) * BLOCK, b * BLOCK) if b > 0 else slice(0, BLOCK)
                m_prev = m_band if b > 0 else (m_band & has_prev)
                msk = jnp.concatenate([m_prev, m_cur], axis=1)
                msk = jnp.concatenate([msk, msk], axis=0)
                for pr in range(wq // LANE):
                    cols = slice(sg * wq + pr * LANE, sg * wq + (pr + 1) * LANE)
                    kcat = jnp.concatenate([_kv_group(k_ref if b > 0 else kp_ref, prows, sg * wk, pr, grp, hq, hk),
                                            _kv_group(k_ref, rows, sg * wk, pr, grp, hq, hk)], axis=0)
                    vcat = jnp.concatenate([_kv_group(v_ref if b > 0 else vp_ref, prows, sg * wk, pr, grp, hq, hk),
                                            _kv_group(v_ref, rows, sg * wk, pr, grp, hq, hk)], axis=0)
                    qs = _stack_heads(q_ref[rows, cols].astype(BF16))
                    s = lax.dot_general(qs, kcat, _NT, preferred_element_type=F32) * scale
                    s = jnp.where(msk, s, NEG)
                    mx = jnp.max(s, axis=1, keepdims=True)
                    p = jnp.exp(s - mx)
                    l = jnp.sum(p, axis=1, keepdims=True)
                    acc = jnp.dot(p.astype(BF16), vcat, preferred_element_type=F32)
                    o_ref[rows, cols] = _unstack_heads(acc / l)
                    l_ref[rows, cols] = _unstack_heads(jnp.broadcast_to(mx + jnp.log(l), (2 * BLOCK, LANE)))

    o, lse = _pcall(
        body, name=name, grid=plan.grid,
        in_specs=[plan.main(qv), plan.main(kv), plan.edge(kv, False), plan.main(vv), plan.edge(vv, False)],
        out_specs=[plan.out(wq)] * 2, out_shape=[jax.ShapeDtypeStruct((S // dil, dil * wq), F32)] * 2,
        compiler_params=_cparams(("parallel", "parallel")),
    )(qv[0], kv[0], kv[0], vv[0], vv[0])
    return o.reshape(S, wq), lse.reshape(S, wq), (_Viewed((o, wq, 0, 1)), _Viewed((lse, wq, 0, 1)))


def _band_views(arrays, *, S, dil, max_dist):
    plan = _BandPlan(S, dil, max_dist)
    return [None if a is None else plan.view(a) for a in arrays]


def _band_dq(q, k, v, do, o, lse, dlse, *, S, dil, max_dist, hq, hk, scale, out_dtype=F32, name):
    hd = 64
    plan = _BandPlan(S, dil, max_dist)
    qv, kv, vv = plan.view(q), plan.view(k), plan.view(v)
    dov, ov, lv = plan.view(do), plan.view(o), plan.view(lse)
    has_dl = dlse is not None
    wq, wk = hq * hd, hk * hd
    grp = hq // hk
    ns, nb = plan.ns, plan.nb
    ops = [qv, kv, kv, vv, vv, dov, ov, lv]
    specs = [plan.main(qv), plan.main(kv), plan.edge(kv, False), plan.main(vv), plan.edge(vv, False), plan.main(dov),
             plan.main(ov), plan.main(lv)]
    if has_dl:
        dlv = plan.view(dlse)
        ops.append(dlv)
        specs.append(plan.main(dlv))

    def body(*refs):
        q_ref, k_ref, kp_ref, v_ref, vp_ref, do_ref, o_ref, l_ref = refs[:8]
        dl_ref = refs[8] if has_dl else None
        dq_ref = refs[-1]
        i = pl.program_id(1)
        m_cur, m_band = plan.masks()
        has_prev = i > 0
        for sg in range(ns):
            for b in range(nb):
                rows = slice(b * BLOCK, (b + 1) * BLOCK)
                prows = slice((b - 1) * BLOCK, b * BLOCK) if b > 0 else slice(0, BLOCK)
                m_prev = m_band if b > 0 else (m_band & has_prev)
                msk = jnp.concatenate([m_prev, m_cur], axis=1)
                msk = jnp.concatenate([msk, msk], axis=0)
                for pr in range(wq // LANE):
                    cols = slice(sg * wq + pr * LANE, sg * wq + (pr + 1) * LANE)
                    kcat = jnp.concatenate([_kv_group(k_ref if b > 0 else kp_ref, prows, sg * wk, pr, grp, hq, hk),
                                            _kv_group(k_ref, rows, sg * wk, pr, grp, hq, hk)], axis=0)
                    vcat = jnp.concatenate([_kv_group(v_ref if b > 0 else vp_ref, prows, sg * wk, pr, grp, hq, hk),
                                            _kv_group(v_ref, rows, sg * wk, pr, grp, hq, hk)], axis=0)
                    qs = _stack_heads(q_ref[rows, cols].astype(BF16))
                    do2 = do_ref[rows, cols].astype(F32)
                    prod = do2 * o_ref[rows, cols].astype(F32)
                    if has_dl:
                        prod = prod - dl_ref[rows, cols].astype(F32)
                    delta = _per_head_sum(prod)
                    lse_rows = _per_head_col(l_ref[rows, cols])
                    dos = _stack_heads(do2.astype(BF16))
                    s = lax.dot_general(qs, kcat, _NT, preferred_element_type=F32) * scale
                    pm = jnp.where(msk, jnp.exp(s - lse_rows), 0.0)
                    dp = lax.dot_general(dos, vcat, _NT, preferred_element_type=F32)
                    ds = (pm * (dp - delta) * scale).astype(BF16)
                    dq_ref[rows, cols] = _unstack_heads(jnp.dot(ds, kcat, preferred_element_type=F32)).astype(dq_ref.dtype)

    dq = _pcall(
        body, name=name, grid=plan.grid, in_specs=specs, out_specs=plan.out(wq),
        out_shape=jax.ShapeDtypeStruct((S // dil, dil * wq), out_dtype), compiler_params=_cparams(("parallel", "parallel")),
    )(*[o_[0] for o_ in ops])
    return dq.reshape(S, wq)


def _band_dkv(q, k, v, do, o, lse, dlse, *, S, dil, max_dist, hq, hk, scale, out_dtype=F32, name):
    hd = 64
    plan = _BandPlan(S, dil, max_dist)
    qv, kv, vv = plan.view(q), plan.view(k), plan.view(v)
    dov, ov, lv = plan.view(do), plan.view(o), plan.view(lse)
    has_dl = dlse is not None
    wq, wk = hq * hd, hk * hd
    grp = hq // hk
    ns, nb = plan.ns, plan.nb
    qlike = [qv, dov, ov, lv] + ([plan.view(dlse)] if has_dl else [])
    ops = [kv, vv] + qlike + qlike
    specs = [plan.main(kv), plan.main(vv)] + [plan.main(t) for t in qlike] + [plan.edge(t, True) for t in qlike]
    nql = len(qlike)
    nkg = wk // LANE

    def body(*refs):
        k_ref, v_ref = refs[:2]
        mains, edges = refs[2:2 + nql], refs[2 + nql:2 + 2 * nql]
        dk_ref, dv_ref = refs[2 + 2 * nql:]
        i = pl.program_id(1)
        m_cur, m_band = plan.masks()
        has_next = i < plan.grid[1] - 1
        for sg in range(ns):
            for b in range(nb):
                rows = slice(b * BLOCK, (b + 1) * BLOCK)
                nxt_in_main = b + 1 < nb
                nrows = slice((b + 1) * BLOCK, (b + 2) * BLOCK) if nxt_in_main else slice(0, BLOCK)
                nsrc = mains if nxt_in_main else edges
                m_next = m_band if nxt_in_main else (m_band & has_next)
                mt_cur, mt_next = m_cur.T, m_next.T
                msk = jnp.concatenate([mt_cur, mt_cur, mt_next, mt_next], axis=1)
                nacc = nkg if hk == hq else hk
                dk_acc, dv_acc = [None] * nacc, [None] * nacc
                for pr in range(wq // LANE):
                    cols = slice(sg * wq + pr * LANE, sg * wq + (pr + 1) * LANE)
                    kop = _kv_group(k_ref, rows, sg * wk, pr, grp, hq, hk)
                    vop = _kv_group(v_ref, rows, sg * wk, pr, grp, hq, hk)
                    qs, dos, deltas, lses = [], [], [], []
                    for src, r in ((mains, rows), (nsrc, nrows)):
                        qs.append(_stack_heads(src[0][r, cols].astype(BF16)))
                        do2 = src[1][r, cols].astype(F32)
                        prod = do2 * src[2][r, cols].astype(F32)
                        if has_dl:
                            prod = prod - src[4][r, cols].astype(F32)
                        prod_t, lse_t = prod.T, src[3][r, cols].T
                        for e in range(2):
                            deltas.append(jnp.sum(prod_t[e * 64:(e + 1) * 64], axis=0, keepdims=True))
                            lses.append(lse_t[e * 64:e * 64 + 1])
                        dos.append(_stack_heads(do2.astype(BF16)))
                    qs4, dos4 = jnp.concatenate(qs, axis=0), jnp.concatenate(dos, axis=0)
                    delta4, lse4 = jnp.concatenate(deltas, axis=1), jnp.concatenate(lses, axis=1)
                    st = lax.dot_general(kop, qs4, _NT, preferred_element_type=F32) * scale
                    pm = jnp.where(msk, jnp.exp(st - lse4), 0.0)
                    dpt = lax.dot_general(vop, dos4, _NT, preferred_element_type=F32)
                    ds = (pm * (dpt - delta4) * scale).astype(BF16)
                    tv = jnp.dot(pm.astype(BF16), dos4, preferred_element_type=F32)
                    tk = jnp.dot(ds, qs4, preferred_element_type=F32)
                    ai = pr if hk == hq else (2 * pr) // grp
                    dv_acc[ai] = tv if dv_acc[ai] is None else dv_acc[ai] + tv
                    dk_acc[ai] = tk if dk_acc[ai] is None else dk_acc[ai] + tk
                for kg in range(nkg):
                    cols = slice(sg * wk + kg * LANE, sg * wk + (kg + 1) * LANE)
                    if hk == hq:
                        dkt, dvt = dk_acc[kg], dv_acc[kg]
                    else:
                        both = lambda t: t + _swap_halves(t)
                        h0 = _half(0)
                        dkt = jnp.where(h0, both(dk_acc[2 * kg]), both(dk_acc[2 * kg + 1]))
                        dvt = jnp.where(h0, both(dv_acc[2 * kg]), both(dv_acc[2 * kg + 1]))
                    dk_ref[rows, cols] = dkt.astype(dk_ref.dtype)
                    dv_ref[rows, cols] = dvt.astype(dv_ref.dtype)

    dk, dv = _pcall(
        body, name=name, grid=plan.grid, in_specs=specs, out_specs=[plan.out(wk)] * 2,
        out_shape=[jax.ShapeDtypeStruct((S // dil, dil * wk), out_dtype)] * 2,
        compiler_params=_cparams(("parallel", "parallel")),
    )(*[o_[0] for o_ in ops])
    return dk.reshape(S, wk), dv.reshape(S, wk)


def _causal_block(S):
    return min(512, S)


def _causal_mask(bq):
    qi = lax.broadcasted_iota(jnp.int32, (bq, bq), 0)
    kj = lax.broadcasted_iota(jnp.int32, (bq, bq), 1)
    return kj <= qi


def _half_of(rows, e):
    lane = lax.broadcasted_iota(jnp.int32, (rows, LANE), 1)
    return (lane < LANE // 2) if e == 0 else (lane >= LANE // 2)


def _causal_fwd(q, k, v, *, heads, scale, name, bk=None):
    S = q.shape[0]
    bq = _causal_block(S)
    bk = bq if bk is None else min(bk, bq)
    r = bq // bk
    nq, nk = S // bq, S // bk
    npair = heads // 2
    wv = heads * 64
    c2 = scale * math.log2(math.e)

    def body(q_ref, k_ref, v_ref, o_ref, l_ref, m_sc, l_sc, acc_sc):
        i, t = pl.program_id(0), pl.program_id(1)

        @pl.when(t == 0)
        def _():
            m_sc[...] = jnp.full(m_sc.shape, NEG, F32)
            l_sc[...] = jnp.zeros(l_sc.shape, F32)
            acc_sc[...] = jnp.zeros(acc_sc.shape, F32)

        def step(masked):
            if masked:
                qi = lax.broadcasted_iota(jnp.int32, (bq, bk), 0)
                kj = lax.broadcasted_iota(jnp.int32, (bq, bk), 1) + (t - r * i) * bk
                msk = kj <= qi
            h0 = _half_of(bq, 0)
            for hs in (range(0, heads // 2), range(heads // 2, heads)):
                ss, m_old, m_new, ps, alpha = {}, {}, {}, {}, {}
                for h in hs:
                    cols = slice(h * LANE, (h + 1) * LANE)
                    sh = lax.dot_general(q_ref[:, cols], k_ref[:, cols], _NT, preferred_element_type=F32)
                    ss[h] = jnp.where(msk, sh, NEG) if masked else sh
                for h in hs:
                    m_old[h] = m_sc[h]
                    m_new[h] = jnp.maximum(m_old[h], jnp.max(ss[h], axis=1, keepdims=True))
                for h in hs:
                    ps[h] = jnp.exp2((ss[h] - m_new[h]) * c2)
                    alpha[h] = jnp.exp2((m_old[h] - m_new[h]) * c2)
                for h in hs:
                    l_sc[h] = alpha[h] * l_sc[h] + jnp.sum(ps[h], axis=1, keepdims=True)
                    m_sc[h] = m_new[h]
                for pr in range(hs[0] // 2, hs[-1] // 2 + 1):
                    v2 = v_ref[:, pr * LANE:(pr + 1) * LANE].astype(BF16)
                    pv = [jnp.dot(ps[2 * pr + e].astype(BF16), v2, preferred_element_type=F32) for e in range(2)]
                    acc = acc_sc[pr]
                    acc_sc[pr] = jnp.where(h0, alpha[2 * pr] * acc + pv[0], alpha[2 * pr + 1] * acc + pv[1])

        pl.when(t < r * i)(lambda: step(False))
        pl.when((t >= r * i) & (t < r * (i + 1)))(lambda: step(True))

        @pl.when(t == nk - 1)
        def _():
            h0 = _half_of(bq, 0)
            for pr in range(npair):
                l0, l1 = l_sc[2 * pr], l_sc[2 * pr + 1]
                acc = acc_sc[pr]
                cols = slice(pr * LANE, (pr + 1) * LANE)
                o_ref[:, cols] = jnp.where(h0, acc / l0, acc / l1)
                l_ref[:, cols] = jnp.where(h0, m_sc[2 * pr] * scale + jnp.log(l0), m_sc[2 * pr + 1] * scale + jnp.log(l1))

    kvi = lambda i, t: (jnp.minimum(t, r * (i + 1) - 1), 0)
    qs = pl.BlockSpec((bq, heads * LANE), lambda i, t: (i, 0))
    ks = pl.BlockSpec((bk, heads * LANE), kvi)
    vs = pl.BlockSpec((bk, wv), kvi)
    os_ = pl.BlockSpec((bq, wv), lambda i, t: (i, 0))
    return _pcall(
        body, name=name, grid=(nq, nk), in_specs=[qs, ks, vs], out_specs=[os_, os_],
        out_shape=[jax.ShapeDtypeStruct((S, wv), F32)] * 2,
        scratch_shapes=[pltpu.VMEM((heads, bq, 1), F32), pltpu.VMEM((heads, bq, 1), F32), pltpu.VMEM((npair, bq, LANE), F32)],
        compiler_params=_cparams(("parallel", "arbitrary")),
    )(q, k, v)


def _causal_bwd_tile(q_ref, k_ref, v2, do2, prod, l2, h, e, scale, msk, bq):
    cols = slice(h * LANE, (h + 1) * LANE)
    hm = _half_of(bq, e)
    s = lax.dot_general(q_ref[:, cols], k_ref[:, cols], _NT, preferred_element_type=F32)
    c2 = scale * math.log2(math.e)
    p = jnp.exp2((s - l2[:, e * 64:e * 64 + 1] * (1.0 / scale)) * c2)
    if msk is not None:
        p = jnp.where(msk, p, 0.0)
    dom = jnp.where(hm, do2, jnp.zeros_like(do2))
    delta = jnp.sum(jnp.where(hm, prod, 0.0), axis=1, keepdims=True)
    dp = lax.dot_general(dom, v2, _NT, preferred_element_type=F32)
    ds = (p * (dp - delta) * scale).astype(BF16)
    return p, ds, dom


def _causal_dq(q, k, v, do, o, lse, *, heads, scale, out_dtype=BF16, name):
    S = q.shape[0]
    bq = _causal_block(S)
    nq = S // bq
    npair = heads // 2
    wv = heads * 64

    def body(q_ref, k_ref, v_ref, do_ref, o_ref, l_ref, dq_ref, dq_sc):
        i, t = pl.program_id(0), pl.program_id(1)

        @pl.when(t == 0)
        def _():
            dq_sc[...] = jnp.zeros(dq_sc.shape, F32)

        def step(masked):
            msk = _causal_mask(bq) if masked else None
            for pr in range(npair):
                pc = slice(pr * LANE, (pr + 1) * LANE)
                v2 = v_ref[:, pc].astype(BF16)
                dof = do_ref[:, pc].astype(F32)
                prod = dof * o_ref[:, pc]
                do2 = dof.astype(BF16)
                l2 = l_ref[:, pc]
                for e in range(2):
                    h = 2 * pr + e
                    _, ds, _ = _causal_bwd_tile(q_ref, k_ref, v2, do2, prod, l2, h, e, scale, msk, bq)
                    dq_sc[h] += jnp.dot(ds, k_ref[:, h * LANE:(h + 1) * LANE], preferred_element_type=F32)

        pl.when(t < i)(lambda: step(False))
        pl.when(t == i)(lambda: step(True))

        @pl.when(t == nq - 1)
        def _():
            for h in range(heads):
                dq_ref[:, h * LANE:(h + 1) * LANE] = dq_sc[h].astype(dq_ref.dtype)

    qs = pl.BlockSpec((bq, heads * LANE), lambda i, t: (i, 0))
    ks = pl.BlockSpec((bq, heads * LANE), lambda i, t: (jnp.minimum(t, i), 0))
    vs = pl.BlockSpec((bq, wv), lambda i, t: (jnp.minimum(t, i), 0))
    os_ = pl.BlockSpec((bq, wv), lambda i, t: (i, 0))
    return _pcall(
        body, name=name, grid=(nq, nq), in_specs=[qs, ks, vs, os_, os_, os_], out_specs=qs,
        out_shape=jax.ShapeDtypeStruct((S, heads * LANE), out_dtype),
        scratch_shapes=[pltpu.VMEM((heads, bq, LANE), F32)],
        compiler_params=_cparams(("parallel", "arbitrary")),
    )(q, k, v, do, o, lse)


def _causal_dkv(q, k, v, do, o, lse, *, heads, scale, out_dtype=BF16, name):
    S = q.shape[0]
    bq = _causal_block(S)
    nq = S // bq
    npair = heads // 2
    wv = heads * 64

    def body(q_ref, k_ref, v_ref, do_ref, o_ref, l_ref, dk_ref, dv_ref, dk_sc, dv_sc):
        j, t = pl.program_id(0), pl.program_id(1)

        @pl.when(t == 0)
        def _():
            dk_sc[...] = jnp.zeros(dk_sc.shape, F32)
            dv_sc[...] = jnp.zeros(dv_sc.shape, F32)

        def step(masked):
            c2 = scale * math.log2(math.e)
            if masked:
                msk = lax.broadcasted_iota(jnp.int32, (bq, bq), 0) <= lax.broadcasted_iota(jnp.int32, (bq, bq), 1)
            sts, dpts, doms, deltas, lses = [], [], [], [], []
            for pr in range(npair):
                pc = slice(pr * LANE, (pr + 1) * LANE)
                v2 = v_ref[:, pc].astype(BF16)
                dof = do_ref[:, pc].astype(F32)
                prod_t = (dof * o_ref[:, pc]).T
                lse_t = l_ref[:, pc].T
                do2 = dof.astype(BF16)
                for e in range(2):
                    h = 2 * pr + e
                    cols = slice(h * LANE, (h + 1) * LANE)
                    dom = jnp.where(_half_of(bq, e), do2, jnp.zeros_like(do2))
                    sts.append(lax.dot_general(k_ref[:, cols], q_ref[:, cols], _NT, preferred_element_type=F32))
                    dpts.append(lax.dot_general(v2, dom, _NT, preferred_element_type=F32))
                    doms.append(dom)
                    deltas.append(jnp.sum(prod_t[e * 64:(e + 1) * 64], axis=0, keepdims=True))
                    lses.append(lse_t[e * 64:e * 64 + 1] * (1.0 / scale))
            pts = [jnp.exp2((sts[h] - lses[h]) * c2) for h in range(heads)]
            if masked:
                pts = [jnp.where(msk, t_, 0.0) for t_ in pts]
            dsts = [(pts[h] * (dpts[h] - deltas[h]) * scale).astype(BF16) for h in range(heads)]
            for pr in range(npair):
                tv = [jnp.dot(pts[2 * pr + e].astype(BF16), doms[2 * pr + e], preferred_element_type=F32) for e in range(2)]
                dv_sc[pr] += tv[0] + tv[1]
            for h in range(heads):
                dk_sc[h] += jnp.dot(dsts[h], q_ref[:, h * LANE:(h + 1) * LANE], preferred_element_type=F32)

        pl.when(t > j)(lambda: step(False))
        pl.when(t == j)(lambda: step(True))

        @pl.when(t == nq - 1)
        def _():
            for h in range(heads):
                dk_ref[:, h * LANE:(h + 1) * LANE] = dk_sc[h].astype(dk_ref.dtype)
            for pr in range(npair):
                dv_ref[:, pr * LANE:(pr + 1) * LANE] = dv_sc[pr].astype(dv_ref.dtype)

    qi = lambda j, t: (jnp.maximum(t, j), 0)
    qs = pl.BlockSpec((bq, heads * LANE), qi)
    os_ = pl.BlockSpec((bq, wv), qi)
    ks = pl.BlockSpec((bq, heads * LANE), lambda j, t: (j, 0))
    vs = pl.BlockSpec((bq, wv), lambda j, t: (j, 0))
    return _pcall(
        body, name=name, grid=(nq, nq), in_specs=[qs, ks, vs, os_, os_, os_], out_specs=[ks, vs],
        out_shape=[jax.ShapeDtypeStruct((S, heads * LANE), out_dtype), jax.ShapeDtypeStruct((S, wv), out_dtype)],
        scratch_shapes=[pltpu.VMEM((heads, bq, LANE), F32), pltpu.VMEM((npair, bq, LANE), F32)],
        compiler_params=_cparams(("parallel", "arbitrary")),
    )(q, k, v, do, o, lse)


@jax.custom_vjp
def _bdot(x, w):
    return jnp.dot(x.astype(BF16), w.astype(BF16), preferred_element_type=F32)


def _bdot_fwd(x, w):
    return _bdot(x, w), (x, w)


def _bdot_bwd(res, g):
    x, w = res
    gb = g.astype(BF16)
    dx = lax.dot_general(gb, w.astype(BF16), _NT, preferred_element_type=F32)
    dw = lax.dot_general(x.astype(BF16), gb, _TN, preferred_element_type=F32)
    return dx.astype(x.dtype), dw.astype(w.dtype)


_bdot.defvjp(_bdot_fwd, _bdot_bwd)


def _tile_matrix(hd, width):
    r = lax.broadcasted_iota(jnp.int32, (hd, width), 0)
    c = lax.broadcasted_iota(jnp.int32, (hd, width), 1) % hd
    return jnp.where(r == c, 1.0, 0.0).astype(F32)


def _spread_matrix(heads, width):
    per = width // heads
    r = lax.broadcasted_iota(jnp.int32, (heads, width), 0)
    c = lax.broadcasted_iota(jnp.int32, (heads, width), 1) // per
    return jnp.where(r == c, 1.0, 0.0).astype(F32)


def _wide(t, width):
    n = width // t.shape[-1]
    return jnp.concatenate([t] * n, axis=1) if n > 1 else t


def _norm_heads(x, gain, hd):
    return _head_norm(x, _exact_dot(gain, _tile_matrix(hd, x.shape[-1])), hd)


def _norm_rope(x, gain, cos, sin, hd):
    w = x.shape[-1]
    return _rope(_norm_heads(x, gain, hd), _wide(cos, w), _wide(sin, w), hd)


def _f_norm(x, g):
    return (_row_norm(x, g),)


def _f_prep_acm(aq, ak, c0q, c0k, c0v, c1q, c1k, c1v, c2q, c2k, c2v, mq, cos, sin, g_aq, g_ak, g0q, g0k, g1q, g1k, g2q, g2k, g_mq):
    outs = [_norm_rope(aq, g_aq, cos, sin, A_HD), _norm_rope(ak, g_ak, cos, sin, A_HD)]
    for cq, ck, cv, gq, gk in ((c0q, c0k, c0v, g0q, g0k), (c1q, c1k, c1v, g1q, g1k), (c2q, c2k, c2v, g2q, g2k)):
        outs += [_norm_rope(cq, gq, cos, sin, C_HD), _norm_rope(ck, gk, cos, sin, C_HD), cv]
    outs.append(_norm_heads(mq, g_mq, M_HD))
    return tuple(outs)


def _rope_mla_q(x, cos, sin):
    lane = lax.broadcasted_iota(jnp.int32, x.shape, x.ndim - 1) % LANE
    half = B_ROPE // 2
    first = (lane >= B_NOPE) & (lane < B_NOPE + half)
    other = jnp.where(first, -_lane_roll(x, -half), _lane_roll(x, half))
    return x * cos + other * sin


def _f_prep_b(ckv, cq, kr, cosq, sinq, cosr, sinr, g_qa, g_kva, w_uq, w_ukv, g_q, g_k, g_kr):
    wq = B_HEADS * LANE
    q_up = _bdot(_row_norm(cq, g_qa), w_uq)
    gq = _exact_dot(g_q, _tile_matrix(LANE, wq))
    qf = _rope_mla_q(_head_norm(q_up, gq, "mla"), _wide(cosq, wq), _wide(sinq, wq))
    kv_up = _bdot(_row_norm(ckv, g_kva), w_ukv)
    kn = _head_norm(kv_up[:, :wq], _exact_dot(g_k, _tile_matrix(LANE, wq)), B_NOPE)
    vb = kv_up[:, wq:]
    kp = _rope(_head_norm(kr, g_kr, B_ROPE), cosr, sinr, B_ROPE)
    kp = _lane_roll(kp, B_NOPE)
    return qf, kn + _wide(kp, wq), vb


def _f_mem_k(k, g):
    return (_norm_heads(k, g, M_HD),)


def _f_sink(o, lse, sink):
    sb = _exact_dot(sink, _spread_matrix(A_HEADS, o.shape[-1]))
    m = jnp.maximum(lse, sb)
    tot = m + jnp.log(jnp.exp(lse - m) + jnp.exp(sb - m))
    return (o * jnp.exp(lse - tot),)


def _f_combine(o0, o1, o2, l0, l1, l2):
    m = jnp.maximum(jnp.maximum(l0, l1), l2)
    e0, e1, e2 = jnp.exp(l0 - m), jnp.exp(l1 - m), jnp.exp(l2 - m)
    inv = 1.0 / (e0 + e1 + e2)
    return ((e0 * o0 + e1 * o1 + e2 * o2) * inv,)


def _f_gatemix(gp, y0, y1, y2, y3, bg):
    d = y0.shape[-1]
    gates = 1.0 / (1.0 + jnp.exp(-(gp + bg)))
    mix = gates[:, :d] * y0
    for n, y in enumerate((y1, y2, y3), start=1):
        mix = mix + gates[:, n * d:(n + 1) * d] * y
    return (mix,)


def _relu2(u):
    return jnp.square(jnp.maximum(u, 0.0))


def _add(r, e):
    return r + e.astype(F32)


def _relu2_grad(r, u):
    return r * (2.0 * jnp.maximum(u, 0.0))


def _loss_and_grad(y, target, *, tb=512):
    R, D = y.shape
    tb = min(tb, R)

    def body(y_ref, t_ref, dy_ref, l_ref):
        err = y_ref[...] - t_ref[...]
        dy_ref[...] = err * (1.0 / D)
        part = 0.5 * jnp.sum(jnp.sum(err * err, axis=1, keepdims=True) * (1.0 / D), axis=0, keepdims=True)
        first = pl.program_id(0) == 0

        @pl.when(first)
        def _():
            l_ref[...] = jnp.broadcast_to(part, l_ref.shape)

        @pl.when(jnp.logical_not(first))
        def _():
            l_ref[...] += jnp.broadcast_to(part, l_ref.shape)

    dy, l = _pcall(
        body, name="loss", grid=(R // tb,),
        in_specs=[pl.BlockSpec((tb, D), lambda i: (i, 0))] * 2,
        out_specs=[pl.BlockSpec((tb, D), lambda i: (i, 0)), pl.BlockSpec((8, LANE), lambda i: (0, 0))],
        out_shape=[jax.ShapeDtypeStruct((R, D), F32), jax.ShapeDtypeStruct((8, LANE), F32)],
        compiler_params=_cparams(("arbitrary",)),
    )(y, target)
    return l[0, 0], dy


def _z_layout(d):
    assert d == 1024, "the aligned layout below is laid out for D_MODEL = 1024"
    lay = {"gates": (4 * d, 0)}
    for g in range(3):
        for n, nm in enumerate("qkv"):
            lay[f"c{g}{nm}"] = (512, 8 + 3 * g + n)
    lay.update(aq=(512, 17), mq=(512, 18), ckv=(256, 38), cq=(384, 26), ak=(128, 81), av=(128, 82), kr=(128, 83))
    return lay, 10752


_KW_A = dict(hq=A_HEADS, hk=A_KV_HEADS, scale=A_HD ** -0.5)
_KW_B = dict(heads=B_HEADS, scale=(B_NOPE + B_ROPE) ** -0.5)
_KW_C = dict(hq=C_HEADS, hk=C_HEADS, scale=C_HD ** -0.5)
_KW_M = dict(hq=M_HEADS, hk=M_HEADS, hd=M_HD, hdv=M_HD, scale=M_HD ** -0.5)


def _layer_fwd(l, x, mem, w, tabs, late=None):
    S, D = x.shape
    lay, _ = _z_layout(D)
    cosA, sinA, cosB, sinB, cosQ, sinQ = tabs
    gM = _AttnGeom("full", S, mem.shape[0])
    nm = lambda s: f"l{l}_{s}"
    sv = {}
    hn = _rowmap(_f_norm, [x], [w["g_mix"]], [(D, BF16)], tb=512, name=nm("norm1"))[0]
    z = _mm(hn, w["in"], name=nm("in"), tn=1536)
    zp = {k: (z, wd, idx) for k, (wd, idx) in lay.items()}
    acm_rows = [zp[k] for k in ("aq", "ak", "c0q", "c0k", "c0v", "c1q", "c1k", "c1v", "c2q", "c2k", "c2v", "mq")] + [cosA, sinA]
    acm_par = [w["a_qn"], w["a_kn"], w["c0q"], w["c0k"], w["c1q"], w["c1k"], w["c2q"], w["c2k"], w["m_qn"]]
    acm = _rowmap(_f_prep_acm, acm_rows, acm_par, [(p[1], BF16) for p in acm_rows[:12]], tb=256, name=nm("prep_acm"))
    qa, ka, qc0, kc0, vc0, qc1, kc1, vc1, qc2, kc2, vc2, mq = acm
    oa_raw, lse_a, _ = _band_fwd(qa, ka, zp["av"], S=S, dil=1, max_dist=A_WINDOW - 1, name=nm("attn_a"), **_KW_A)
    o_a = _rowmap(_f_sink, [oa_raw, lse_a], [w["a_sink"]], [(512, BF16)], tb=512, name=nm("sink"))[0]
    oc, lc, cviews = [], [], []
    for g, ((win, dil), qc, kc, vc) in enumerate(zip(C_PATTERNS, (qc0, qc1, qc2), (kc0, kc1, kc2), (vc0, vc1, vc2))):
        qkv = _band_views([qc, kc, vc], S=S, dil=dil, max_dist=win // dil)
        o_g, l_g, ol_views = _band_fwd(*qkv, S=S, dil=dil, max_dist=win // dil, name=nm(f"attn_c{g}"), **_KW_C)
        oc.append(o_g)
        lc.append(l_g)
        cviews.append(tuple(qkv) + tuple(ol_views))
    o_c = _rowmap(_f_combine, oc + lc, [], [(512, BF16)], tb=512, name=nm("combine"))[0]
    if late is not None:
        w = dict(w, **late(o_c))

    b_rows = [zp["ckv"], zp["cq"], zp["kr"], cosQ, sinQ, cosB, sinB]
    b_par = [w["b_qa"], w["b_kva"], w["uq"], w["ukv"], w["b_q"], w["b_k"], w["b_kr"]]
    qb, kb, vb = _rowmap(_f_prep_b, b_rows, b_par, [(B_HEADS * LANE, BF16), (B_HEADS * LANE, BF16), (512, BF16)], tb=256,
                         name=nm("prep_b"))
    memn = _rowmap(_f_norm, [mem], [w["m_g_mem"]], [(D, BF16)], tb=256, name=nm("mem_norm"))[0]
    mkv = _mm(memn, w["mkv"], name=nm("mem_kv"))
    mk = _rowmap(_f_mem_k, [(mkv, 512, 0)], [w["m_kn"]], [(512, BF16)], tb=256, name=nm("mem_k"))[0]
    mv = (mkv, 512, 1)
    o_b, lse_b = _causal_fwd(qb, kb, vb, name=nm("attn_b"), **_KW_B)
    o_m, lse_m = _attn_fwd(gM, mq, mk, mv, name=nm("attn_m"), **_KW_M)

    o_n = [o_a, o_b, o_c, o_m]
    ys = [_mm(o_n[n], w["branch"][n], name=nm(f"branch{n}")) for n in range(N_BRANCH)]
    mix = _rowmap(_f_gatemix, [zp["gates"]] + ys, [w["b_gate"]], [(D, BF16)], tb=256, name=nm("gatemix"))[0]
    x1 = _mm(mix, w["out"], extra=x, epi=_add, name=nm("out"))
    hn2 = _rowmap(_f_norm, [x1], [w["g_mlp"]], [(D, BF16)], tb=512, name=nm("norm2"))[0]
    u = _mm(hn2, w["up"], name=nm("up"))
    x2 = _mm(u, w["down"], pro_a=_relu2, extra=x1, epi=_add, name=nm("down"))
    sv.update(x=x, hn=hn, z=z, acm=acm, bq=(qb, kb, vb), memn=memn, mkv=mkv, mk=mk, oa_raw=oa_raw, lse_a=lse_a,
              o_b=o_b, lse_b=lse_b, oc=oc, lc=lc, cviews=cviews, o_m=o_m, lse_m=lse_m, o_n=o_n, ys=ys, mix=mix, x1=x1, hn2=hn2, u=u)
    return x2, sv, w


def _layer_bwd(l, dx2, mem, w, tabs, sv, early=None):
    x, z, x1, u = sv["x"], sv["z"], sv["x1"], sv["u"]
    S, D = x.shape
    lay, zw = _z_layout(D)
    cosA, sinA, cosB, sinB, cosQ, sinQ = tabs
    gM = _AttnGeom("full", S, mem.shape[0])
    nm = lambda s: f"l{l}_{s}"
    zp = {k: (z, wd, idx) for k, (wd, idx) in lay.items()}
    g = {}
    du = _mm(dx2, w["down"], tb=True, extra=u, epi=_relu2_grad, out_dtype=BF16, name=nm("d_down_x"))
    g["down"] = _mm(u, dx2, ta=True, pro_a=_relu2, name=nm("d_down_w"))
    dhn2 = _mm(du, w["up"], tb=True, name=nm("d_up_x"))
    g["up"] = _mm(sv["hn2"], du, ta=True, name=nm("d_up_w"))
    (dx1,), (g["g_mlp"],) = _rowmap_bwd(_f_norm, [x1], [w["g_mlp"]], [dhn2], diff=[True], out_dtypes=[F32], adds=[dx2],
                                        tb=256, name=nm("d_norm2"))
    dmix = _mm(dx1, w["out"], tb=True, name=nm("d_out_x"))
    g["out"] = _mm(sv["mix"], dx1, ta=True, name=nm("d_out_w"))
    (dgates, dy0, dy1, dy2, dy3), (g["b_gate"],) = _rowmap_bwd(
        _f_gatemix, [zp["gates"]] + sv["ys"], [w["b_gate"]], [dmix], diff=[True] * 5, out_dtypes=[BF16] * 5,
        tb=128, name=nm("d_gatemix"))
    dys = [dy0, dy1, dy2, dy3]
    do = [_mm(dys[n], w["branch"][n], tb=True, name=nm(f"d_branch{n}_x")) for n in range(N_BRANCH)]
    g["branch"] = [_mm(sv["o_n"][n], dys[n], ta=True, name=nm(f"d_branch{n}_w")) for n in range(N_BRANCH)]
    qa, ka, qc0, kc0, vc0, qc1, kc1, vc1, qc2, kc2, vc2, mq = sv["acm"]
    qb, kb, vb = sv["bq"]
    mkv, mk = sv["mkv"], sv["mk"]
    mv = (mkv, 512, 1)
    dmq = _attn_dq(gM, mq, mk, mv, do[3], sv["o_m"], sv["lse_m"], None, name=nm("attn_m_dq"), **_KW_M)
    dmk, dmv = _attn_dkv(gM, mq, mk, mv, do[3], sv["o_m"], sv["lse_m"], None, name=nm("attn_m_dkv"), **_KW_M)
    (doc0, doc1, doc2, dl0, dl1, dl2), _ = _rowmap_bwd(_f_combine, sv["oc"] + sv["lc"], [], [do[2]], diff=[True] * 6,
                                                      out_dtypes=[F32] * 6, tb=256, name=nm("d_combine"))
    dqc, dkc, dvc = [], [], []
    for gi, ((win, dil), qc, kc, vc, doc, dl) in enumerate(zip(C_PATTERNS, (qc0, qc1, qc2), (kc0, kc1, kc2), (vc0, vc1, vc2),
                                                              (doc0, doc1, doc2), (dl0, dl1, dl2))):
        qv_, kv_, vv_, ov_, lv_ = sv["cviews"][gi]
        dov_, dlv_ = _band_views([doc, dl], S=S, dil=dil, max_dist=win // dil)
        args = (qv_, kv_, vv_, dov_, ov_, lv_, dlv_)
        kwc = dict(S=S, dil=dil, max_dist=win // dil, out_dtype=BF16, **_KW_C)
        dqc.append(_band_dq(*args, name=nm(f"attn_c{gi}_dq"), **kwc))
        dk_, dv_ = _band_dkv(*args, name=nm(f"attn_c{gi}_dkv"), **kwc)
        dkc.append(dk_)
        dvc.append(dv_)
    argsb = (qb, kb, vb, do[1], sv["o_b"], sv["lse_b"])
    dqb = _causal_dq(*argsb, name=nm("attn_b_dq"), **_KW_B)
    dkb, dvb = _causal_dkv(*argsb, name=nm("attn_b_dkv"), **_KW_B)
    (doa_raw, dlse_a), (g["a_sink"],) = _rowmap_bwd(_f_sink, [sv["oa_raw"], sv["lse_a"]], [w["a_sink"]], [do[0]],
                                                   diff=[True, True], out_dtypes=[F32, F32], tb=256, name=nm("d_sink"))
    argsa = (qa, ka, zp["av"], doa_raw, sv["oa_raw"], sv["lse_a"], dlse_a)
    kwa = dict(S=S, dil=1, max_dist=A_WINDOW - 1, out_dtype=BF16, **_KW_A)
    dqa = _band_dq(*argsa, name=nm("attn_a_dq"), **kwa)
    dka, dva = _band_dkv(*argsa, name=nm("attn_a_dkv"), **kwa)
    acm_rows = [zp[k] for k in ("aq", "ak", "c0q", "c0k", "c0v", "c1q", "c1k", "c1v", "c2q", "c2k", "c2v", "mq")] + [cosA, sinA]
    acm_par = [w["a_qn"], w["a_kn"], w["c0q"], w["c0k"], w["c1q"], w["c1k"], w["c2q"], w["c2k"], w["m_qn"]]
    acm_ct = [dqa, dka, dqc[0], dkc[0], dvc[0], dqc[1], dkc[1], dvc[1], dqc[2], dkc[2], dvc[2], dmq]
    dacm, (g["a_qn"], g["a_kn"], g["c0q"], g["c0k"], g["c1q"], g["c1k"], g["c2q"], g["c2k"], g["m_qn"]) = _rowmap_bwd(
        _f_prep_acm, acm_rows, acm_par, acm_ct, diff=[True] * 12 + [False, False], out_dtypes=[BF16] * 12, tb=256,
        name=nm("d_prep_acm"))
    d_aq, d_ak, d_c0q, d_c0k, d_c0v, d_c1q, d_c1k, d_c1v, d_c2q, d_c2k, d_c2v, d_mq = dacm
    b_rows = [zp["ckv"], zp["cq"], zp["kr"], cosQ, sinQ, cosB, sinB]
    b_par = [w["b_qa"], w["b_kva"], w["uq"], w["ukv"], w["b_q"], w["b_k"], w["b_kr"]]
    (d_ckv, d_cq, d_kr), gb = _rowmap_bwd(_f_prep_b, b_rows, b_par, [dqb, dkb, dvb], diff=[True] * 3 + [False] * 4,
                                          out_dtypes=[BF16] * 3, tb=256, name=nm("d_prep_b"))
    g["b_qa"], g["b_kva"], g["uq"], g["ukv"], g["b_q"], g["b_k"], g["b_kr"] = gb
    (dmkv_k,), (g["m_kn"],) = _rowmap_bwd(_f_mem_k, [(mkv, 512, 0)], [w["m_kn"]], [dmk], diff=[True], out_dtypes=[F32],
                                          tb=256, name=nm("d_mem_k"))
    dmkv = jnp.concatenate([dmkv_k, dmv], axis=1)
    dmemn = _mm(dmkv, w["mkv"], tb=True, name=nm("d_mem_kv_x"))
    g["mkv"] = _mm(sv["memn"], dmkv, ta=True, name=nm("d_mem_kv_w"))
    _, (g["m_g_mem"],) = _rowmap_bwd(_f_norm, [mem], [w["m_g_mem"]], [dmemn], diff=[True], out_dtypes=[F32], tb=256,
                                     name=nm("d_mem_norm"))
    if early is not None:
        early(g)
    dz = jnp.concatenate([dgates, d_c0q, d_c0k, d_c0v, d_c1q, d_c1k, d_c1v, d_c2q, d_c2k, d_c2v, d_aq, d_mq, d_ckv, d_cq,
                          d_ak, dva, d_kr], axis=1)
    assert dz.shape[1] == zw
    dhn = _mm(dz, w["in"], tb=True, name=nm("d_in_x"), tk=1536)
    g["in"] = _mm(sv["hn"], dz, ta=True, name=nm("d_in_w"), tn=1536)
    (dx,), (g["g_mix"],) = _rowmap_bwd(_f_norm, [x], [w["g_mix"]], [dhn], diff=[True], out_dtypes=[F32], adds=[dx1],
                                       tb=256, name=nm("d_norm1"))
    return dx, g


_IN_ORIG = dict(aq=(0, 512), ak=(512, 640), av=(640, 768), cq=(768, 1152), ckv=(1152, 1408), kr=(1408, 1440),
                c=(1440, 6048), mq=(6048, 6560), gates=(6560, 10656))
_IN_OURS = dict(gates=(0, 4096), c=(4096, 8704), aq=(8704, 9216), mq=(9216, 9728), ckv=(9728, 9984), cq=(9984, 10368),
                ak=(10368, 10496), av=(10496, 10624), kr=(10624, 10656))
_IN_ORDER_ORIG = ("aq", "ak", "av", "cq", "ckv", "kr", "c", "mq", "gates")


def _in_to_ours(w_in):
    pc = {k: w_in[..., a:b] for k, (a, b) in _IN_ORIG.items()}
    zeros = lambda n: jnp.zeros(w_in.shape[:-1] + (n,), w_in.dtype)
    return jnp.concatenate([pc["gates"], pc["c"], pc["aq"], pc["mq"], pc["ckv"], pc["cq"], pc["ak"], pc["av"], pc["kr"],
                            zeros(96)], axis=-1)


def _in_from_ours(g_in):
    return jnp.concatenate([g_in[..., _IN_OURS[k][0]:_IN_OURS[k][1]] for k in _IN_ORDER_ORIG], axis=-1)


def _uq_to_ours(w):
    per = B_NOPE + B_ROPE
    w4 = w.reshape(w.shape[:-1] + (B_HEADS, per))
    w4 = jnp.pad(w4, [(0, 0)] * (w4.ndim - 1) + [(0, LANE - per)])
    return w4.reshape(w.shape[:-1] + (B_HEADS * LANE,))


def _uq_from_ours(g):
    per = B_NOPE + B_ROPE
    g4 = g.reshape(g.shape[:-1] + (B_HEADS, LANE))[..., :per]
    return g4.reshape(g.shape[:-1] + (B_HEADS * per,))


def _ukv_to_ours(w):
    w4 = w.reshape(w.shape[:-1] + (B_HEADS, B_NOPE + B_V))
    keys = jnp.pad(w4[..., :B_NOPE], [(0, 0)] * (w4.ndim - 1) + [(0, LANE - B_NOPE)])
    vals = w4[..., B_NOPE:]
    return jnp.concatenate([keys.reshape(w.shape[:-1] + (B_HEADS * LANE,)), vals.reshape(w.shape[:-1] + (B_HEADS * B_V,))],
                           axis=-1)


def _ukv_from_ours(g):
    wq = B_HEADS * LANE
    keys = g[..., :wq].reshape(g.shape[:-1] + (B_HEADS, LANE))[..., :B_NOPE]
    vals = g[..., wq:].reshape(g.shape[:-1] + (B_HEADS, B_V))
    return jnp.concatenate([keys, vals], axis=-1).reshape(g.shape[:-1] + (B_HEADS * (B_NOPE + B_V),))


def _layer_weights(big_l, small, l):
    row = lambda a: a[l].reshape(1, -1)
    w = dict(big_l)
    w.update(g_mix=row(small["g_mix"]), b_gate=row(small["b_gate"]), a_qn=row(small["a_qn"]), a_kn=row(small["a_kn"]),
             a_sink=row(small["a_sink"]), b_qa=row(small["b_qa_norm"]), b_kva=row(small["b_kva_norm"]),
             b_q=jnp.pad(small["b_qn"][l], (0, LANE - B_NOPE - B_ROPE)).reshape(1, -1),
             b_k=jnp.pad(small["b_kn"][l, :B_NOPE], (0, LANE - B_NOPE)).reshape(1, -1),
             b_kr=jnp.pad(small["b_kn"][l, B_NOPE:], (0, LANE - B_ROPE)).reshape(1, -1),
             m_g_mem=row(small["m_g_mem"]), m_qn=row(small["m_qn"]), m_kn=row(small["m_kn"]), g_mlp=row(small["g_mlp"]))
    for g in range(3):
        w[f"c{g}q"] = small["c_qn"][l, g].reshape(1, -1)
        w[f"c{g}k"] = small["c_kn"][l, g].reshape(1, -1)
    return w


def _rope_tables(positions):
    pos = positions.astype(F32)[:, None]
    tabs = []
    for dim in (A_HD, B_ROPE):
        inv = ROPE_THETA ** (-jnp.arange(0, dim, 2, dtype=F32) / dim)
        ang = pos * inv
        reps = LANE // (dim // 2)
        tabs += [jnp.tile(jnp.cos(ang), (1, reps)), jnp.tile(jnp.sin(ang), (1, reps))]
    half = B_ROPE // 2
    cb, sb = tabs[2][:, :half], tabs[3][:, :half]
    ones, zeros = jnp.ones((pos.shape[0], B_NOPE), F32), jnp.zeros((pos.shape[0], B_NOPE), F32)
    pad = LANE - B_NOPE - B_ROPE
    tabs.append(jnp.concatenate([ones, cb, cb, ones[:, :pad]], axis=1))
    tabs.append(jnp.concatenate([zeros, sb, sb, zeros[:, :pad]], axis=1))
    return tuple(tabs)


def _local_step(x, mem, positions, small, loss_target, get_big, put_grads, early_grads=None):
    depth = small["g_mix"].shape[0]
    tabs = _rope_tables(positions)
    ws, saved = [], []
    h = x
    for l in range(depth):
        first, late = get_big(l, h)
        h, sv, wl = _layer_fwd(l, h, mem, _layer_weights(first, small, l), tabs, late)
        ws.append(wl)
        saved.append(sv)
    loss, dh = _loss_and_grad(h, loss_target)
    small_grads = [None] * depth
    for l in reversed(range(depth)):
        dh, g = _layer_bwd(l, dh, mem, ws[l], tabs, saved[l], early_grads if l == 0 else None)
        zero = put_grads(l, g)
        if l > 0:
            ws[l - 1] = dict(ws[l - 1], g_mlp=ws[l - 1]["g_mlp"] + zero)
        small_grads[l] = g
    return loss, dh, small_grads


def _small_grads_to_reference_layout(grads):
    flat = lambda k: jnp.stack([g[k].reshape(-1) for g in grads])
    return dict(
        g_mix=flat("g_mix"), b_gate=flat("b_gate"), a_qn=flat("a_qn"), a_kn=flat("a_kn"), a_sink=flat("a_sink"),
        b_qa_norm=flat("b_qa"), b_kva_norm=flat("b_kva"), b_qn=flat("b_q")[:, :B_NOPE + B_ROPE],
        b_kn=jnp.concatenate([flat("b_k")[:, :B_NOPE], flat("b_kr")[:, :B_ROPE]], axis=1),
        c_qn=jnp.stack([jnp.stack([g[f"c{i}q"].reshape(-1) for i in range(3)]) for g in grads]),
        c_kn=jnp.stack([jnp.stack([g[f"c{i}k"].reshape(-1) for i in range(3)]) for g in grads]),
        m_g_mem=flat("m_g_mem"), m_qn=flat("m_qn"), m_kn=flat("m_kn"), g_mlp=flat("g_mlp"))


_BIG_GRAD = {"w_in": lambda g: _in_from_ours(g["in"]), "b_w_uq": lambda g: _uq_from_ours(g["uq"]),
             "b_w_ukv": lambda g: _ukv_from_ours(g["ukv"]), "m_w_kv": lambda g: g["mkv"], "w_branch": lambda g: jnp.stack(g["branch"]),
             "w_out": lambda g: g["out"], "w_up": lambda g: g["up"], "w_down": lambda g: g["down"]}


def _big_grads_to_reference_layout(g, names=None):
    return {k: _BIG_GRAD[k](g) for k in (_BIG_GRAD if names is None else names)}


def _big_to_kernel_layout(full):
    conv = {"w_in": ("in", _in_to_ours), "b_w_uq": ("uq", _uq_to_ours), "b_w_ukv": ("ukv", _ukv_to_ours),
            "m_w_kv": ("mkv", None), "w_branch": ("branch", None), "w_out": ("out", None), "w_up": ("up", None),
            "w_down": ("down", None)}
    out = {}
    for k, a in full.items():
        name, fn = conv[k]
        out[name] = (a if fn is None else fn(a)).astype(BF16)
    return out


MESH = pl.DeviceIdType.MESH
N_CHIPS = 4
N_DEV = 8
_ANY = pl.BlockSpec(memory_space=pl.ANY)


_HBM = pl.BlockSpec(memory_space=pltpu.HBM)
_SEM = pl.BlockSpec(memory_space=pltpu.SEMAPHORE)
_EFFECT = pltpu.SideEffectType.DATAFLOW_SIDE_EFFECTING


def _chip_peers():
    x, y, c = lax.axis_index("x"), lax.axis_index("y"), lax.axis_index("c")
    return 2 * x + y, [((1 - x, y, c), 2 * (1 - x) + y), ((x, 1 - y, c), 2 * x + 1 - y), ((1 - x, 1 - y, c), 2 * (1 - x) + 1 - y)]


def _exchange_start(srcs, lands, *, gather, name):
    n = len(srcs)

    def body(*refs):
        ins, land = refs[:n], refs[n:2 * n]
        send_sems, recv_sems = refs[2 * n], refs[2 * n + 1]
        token = refs[-1]
        me, peers = _chip_peers()
        for a in range(n):
            for j, (dev, chip) in enumerate(peers):
                src = ins[a] if gather else ins[a].at[chip]
                pltpu.make_async_remote_copy(src_ref=src, dst_ref=land[a].at[me], send_sem=send_sems.at[3 * a + j],
                                             recv_sem=recv_sems.at[3 * a + j], device_id=dev, device_id_type=MESH).start()
        token[...] = jnp.zeros(token.shape, token.dtype)

    hbm = lambda a: pltpu.HBM(a.shape, a.dtype)
    outs = _pcall(
        body, name=name,
        out_shape=(pltpu.SemaphoreType.DMA((3 * n,)), pltpu.SemaphoreType.DMA((3 * n,)), *[hbm(a) for a in srcs],
                   *[hbm(a) for a in lands], jax.ShapeDtypeStruct((8, LANE), F32)),
        in_specs=[_HBM] * (2 * n), out_specs=(_SEM, _SEM, *([_HBM] * (2 * n)), pl.BlockSpec(memory_space=pltpu.VMEM)),
        input_output_aliases={a: 2 + a for a in range(2 * n)},
        compiler_params=pltpu.CompilerParams(has_side_effects=_EFFECT),
    )(*[pltpu.with_memory_space_constraint(a, pltpu.HBM) for a in srcs],
      *[pltpu.with_memory_space_constraint(a, pltpu.HBM) for a in lands])
    return outs[0], outs[1], list(outs[2:2 + n]), list(outs[2 + n:2 + 2 * n]), outs[-1]


def _exchange_wait(state, after, *, gather, name):
    send_sems, recv_sems, srcs, lands, _ = state
    n = len(lands)

    def body(*refs):
        src_refs, land = refs[:n], refs[n:2 * n]
        ssem, rsem = refs[2 * n], refs[2 * n + 1]
        me, peers = _chip_peers()
        for a in range(n):
            for j, (dev, chip) in enumerate(peers):
                src = src_refs[a] if gather else src_refs[a].at[chip]
                cp = pltpu.make_async_remote_copy(src_ref=src, dst_ref=land[a].at[chip], send_sem=ssem.at[3 * a + j],
                                                  recv_sem=rsem.at[3 * a + j], device_id=dev, device_id_type=MESH)
                cp.wait_send()
                cp.wait_recv()

    outs = _pcall(
        body, name=name,
        out_shape=tuple(pltpu.HBM(a.shape, a.dtype) for a in list(srcs) + list(lands)),
        in_specs=[_HBM] * (2 * n) + [_SEM, _SEM, pl.BlockSpec(memory_space=pl.ANY)],
        out_specs=tuple([_HBM] * (2 * n)), input_output_aliases={a: a for a in range(2 * n)},
        compiler_params=pltpu.CompilerParams(has_side_effects=_EFFECT),
    )(*srcs, *lands, send_sems, recv_sems, after)
    return list(outs[:n]), list(outs[n:])


def _sibling_exchange(arrays, *, name):
    n = len(arrays)

    def body(*refs):
        ins, outs = refs[:n], refs[n:2 * n]
        send_sems, recv_sems = refs[2 * n:]
        x, y, c = lax.axis_index("x"), lax.axis_index("y"), lax.axis_index("c")
        cps = []
        for a in range(n):
            cp = pltpu.make_async_remote_copy(src_ref=ins[a], dst_ref=outs[a], send_sem=send_sems.at[a], recv_sem=recv_sems.at[a],
                                              device_id=(x, y, 1 - c), device_id_type=MESH)
            cp.start()
            cps.append(cp)
        for cp in cps:
            cp.wait()

    return _pcall(
        body, name=name, in_specs=[_ANY] * n, out_specs=[_ANY] * n,
        out_shape=[jax.ShapeDtypeStruct(a.shape, a.dtype) for a in arrays],
        scratch_shapes=[pltpu.SemaphoreType.DMA((n,)), pltpu.SemaphoreType.DMA((n,))],
        compiler_params=pltpu.CompilerParams(has_side_effects=True),
    )(*arrays)


def _allreduce_small(v, *, name):
    rows = v.shape[0]

    def body(v_ref, o_ref, slots, send_sems, recv_sems):
        x, y, c = lax.axis_index("x"), lax.axis_index("y"), lax.axis_index("c")
        me = 4 * x + 2 * y + c
        slots[me] = v_ref[...]
        cps = []
        for k in range(1, N_DEV):
            fx, fy, fc = (k >> 2) & 1, (k >> 1) & 1, k & 1
            peer = (x ^ fx, y ^ fy, c ^ fc)
            cp = pltpu.make_async_remote_copy(src_ref=v_ref, dst_ref=slots.at[me], send_sem=send_sems.at[k - 1],
                                              recv_sem=recv_sems.at[k - 1], device_id=peer, device_id_type=MESH)
            cp.start()
            cps.append((cp, peer))
        for k, (cp, (px, py, pc)) in enumerate(cps):
            pltpu.make_async_remote_copy(src_ref=v_ref, dst_ref=slots.at[4 * px + 2 * py + pc], send_sem=send_sems.at[k],
                                         recv_sem=recv_sems.at[k], device_id=(px, py, pc), device_id_type=MESH).wait_recv()
        for cp, _ in cps:
            cp.wait_send()
        tot = slots[0]
        for d in range(1, N_DEV):
            tot = tot + slots[d]
        o_ref[...] = tot

    vm = pl.BlockSpec(memory_space=pltpu.VMEM)
    return _pcall(
        body, name=name, in_specs=[vm], out_specs=vm, out_shape=jax.ShapeDtypeStruct(v.shape, F32),
        scratch_shapes=[pltpu.VMEM((N_DEV, rows, LANE), F32), pltpu.SemaphoreType.DMA((N_DEV - 1,)),
                        pltpu.SemaphoreType.DMA((N_DEV - 1,))],
        compiler_params=pltpu.CompilerParams(has_side_effects=True),
    )(v)


def _rows_block(rows, cols, itemsize=4, target_bytes=1 << 20, step=16):
    want = max(16, target_bytes // max(1, cols * itemsize))
    best = rows
    for t in range(step, rows, step):
        if rows % t == 0 and t <= want:
            best = t
    return best if best <= want or rows <= want else rows


def _sum_slots(recvs, parts, me, *, name):
    nl = len(recvs)
    shp = recvs[0].shape[1:]
    r3 = [r.reshape(N_CHIPS, -1, shp[-1]) for r in recvs]
    p3 = [q.reshape(N_CHIPS, -1, shp[-1]) for q in parts]
    rows, cols = r3[0].shape[1:]
    if rows % 16 == 0:
        rb, cb = _rows_block(rows, cols), cols
    else:
        rb, cb = rows, _tile(cols, LANE)
    ncb = cols // cb
    nblk = (rows // rb) * ncb
    per = N_CHIPS

    def body(me_ref, *refs):
        o_ref = refs[-1]
        lg = pl.program_id(0)
        for l in range(nl):
            r = refs[per * l:per * (l + 1)]

            @pl.when(lg == l)
            def _(r=r):
                tot = (r[0][...].astype(F32) + r[1][...].astype(F32)) + r[2][...].astype(F32)
                o_ref[...] = tot + r[3][...].astype(F32)

    def blk(l, lg, i):
        j = jnp.where(lg < l, 0, jnp.where(lg > l, nblk - 1, i))
        return j // ncb, j % ncb

    in_specs, args = [], []
    for l in range(nl):
        for k in range(1, N_CHIPS):
            in_specs.append(pl.BlockSpec((None, rb, cb), lambda lg, i, me_ref, l=l, k=k: (me_ref[0] ^ k, *blk(l, lg, i))))
            args.append(r3[l])
        in_specs.append(pl.BlockSpec((None, rb, cb), lambda lg, i, me_ref, l=l: (me_ref[0], *blk(l, lg, i))))
        args.append(p3[l])
    out = _pcall(
        body, name=name,
        grid_spec=pltpu.PrefetchScalarGridSpec(
            num_scalar_prefetch=1, grid=(nl, nblk), in_specs=in_specs,
            out_specs=pl.BlockSpec((None, rb, cb), lambda lg, i, me_ref: (lg, i // ncb, i % ncb))),
        out_shape=jax.ShapeDtypeStruct((nl, rows, cols), F32), compiler_params=_cparams(("arbitrary", "arbitrary")),
    )(me, *args)
    return out.reshape((nl,) + shp)


def _adamw(w, g_parts, m, v, *, layer0=0, prev=None, name):
    shp = w.shape
    two = lambda a: a.reshape(-1, shp[-1])
    rows, cols = two(w).shape
    grows = two(g_parts[0]).shape[0]
    per_layer = rows // shp[0] if layer0 or prev is not None or grows != rows else rows
    tb = _rows_block(per_layer, cols, target_bytes=1 << 19, step=8)
    off = (layer0 * per_layer) // tb if per_layer != rows else 0
    npart = len(g_parts)
    nprev = 0 if prev is None else 4

    def body(*refs):
        w_ref = refs[0]
        gp = refs[1:1 + npart]
        m_ref, v_ref = refs[1 + npart:3 + npart]
        g_out, d_out, m_out, v_out = refs[3 + npart + nprev:]
        g = gp[0][...]
        for r in gp[1:]:
            g = g + r[...]
        wv = w_ref[...]
        m2 = ADAM_B1 * m_ref[...] + (1.0 - ADAM_B1) * g
        v2 = ADAM_B2 * v_ref[...] + (1.0 - ADAM_B2) * jnp.square(g)
        m_hat = m2 / (1.0 - ADAM_B1 ** ADAM_STEP)
        v_hat = v2 / (1.0 - ADAM_B2 ** ADAM_STEP)
        g_out[...] = g
        d_out[...] = -ADAM_LR * (m_hat / (jnp.sqrt(v_hat) + ADAM_EPS) + ADAM_WD * wv)
        m_out[...] = m2
        v_out[...] = v2

    wspec = pl.BlockSpec((tb, cols), lambda i: (i + off, 0))
    gspec = pl.BlockSpec((tb, cols), lambda i: (i, 0))
    in_specs = [wspec] + [gspec] * npart + [wspec, wspec] + [_ANY] * nprev
    args = [two(w)] + [two(p) for p in g_parts] + [two(m), two(v)] + ([two(a) for a in prev] if prev is not None else [])
    outs = _pcall(
        body, name=name, grid=(grows // tb,), in_specs=in_specs, out_specs=[wspec] * 4,
        out_shape=[jax.ShapeDtypeStruct((rows, cols), F32)] * 4,
        input_output_aliases={3 + npart + k: k for k in range(nprev)}, compiler_params=_cparams(("parallel",)),
    )(*args)
    return [o.reshape(shp) for o in outs]


BIG = ("w_in", "b_w_uq", "b_w_ukv", "m_w_kv", "w_branch", "w_out", "w_up", "w_down")
_SHARD_AXIS = dict(w_in=2, b_w_uq=2, b_w_ukv=2, m_w_kv=1, w_branch=3, w_out=1, w_up=2, w_down=1)
SMALL = ("g_mix", "b_gate", "a_qn", "a_kn", "a_sink", "b_qa_norm", "b_kva_norm", "b_qn", "b_kn", "c_qn", "c_kn",
         "m_g_mem", "m_qn", "m_kn", "g_mlp")
WEIGHTS = ("g_mix", "w_in", "b_gate", "a_qn", "a_kn", "a_sink", "b_qa_norm", "b_kva_norm", "b_w_uq", "b_w_ukv", "b_qn", "b_kn",
           "c_qn", "c_kn", "m_g_mem", "m_w_kv", "m_qn", "m_kn", "w_branch", "w_out", "g_mlp", "w_up", "w_down")


def _to_entry(name, a):
    return jnp.swapaxes(a, -1, -2) if name == "w_in" else a


def _unshard(gathered, axis):
    moved = jnp.moveaxis(gathered, 0, axis)
    shp = list(gathered.shape[1:])
    shp[axis] *= N_CHIPS
    return moved.reshape(shp)


def _shard_parts(full, axis):
    shp = list(full.shape)
    shp[axis:axis + 1] = [N_CHIPS, shp[axis] // N_CHIPS]
    return jnp.moveaxis(full.reshape(shp), axis, 0)


def _pack_small(d):
    flat = jnp.concatenate([d[k].reshape(-1).astype(F32) for k in SMALL])
    n = flat.shape[0]
    pad = (-n) % (8 * LANE)
    return jnp.pad(flat, (0, pad)).reshape(-1, LANE)


def _unpack_small(packed, like):
    flat = packed.reshape(-1)
    out, off = {}, 0
    for k in SMALL:
        n = int(np.prod(like[k].shape))
        out[k] = flat[off:off + n].reshape(like[k].shape)
        off += n
    return out


def _train_step(x, mem, positions, loss_target, w, m, v):
    depth = w["g_mix"].shape[0]
    me = 2 * lax.axis_index("x") + lax.axis_index("y")
    landing = lambda a: lax.empty((N_CHIPS,) + a.shape, a.dtype)
    def with_own(land, mine):
        slot = lax.broadcasted_iota(jnp.int32, (N_CHIPS,) + (1,) * mine.ndim, 0)
        return jnp.where(slot == me, mine[None], land)

    gathers = {}
    for l in range(depth):
        for names in ((BIG[:1], BIG[1:]) if l == 0 else (BIG,)):
            own = [w[k][l].astype(BF16) for k in names]
            tag = f"{l}" if len(names) == len(BIG) else f"{l}_{names[0]}"
            gathers[(l, names)] = _exchange_start(own, [landing(a) for a in own], gather=True, name=f"gather_start{tag}")

    def gathered(l, names, after):
        tag = f"{l}" if len(names) == len(BIG) else f"{l}_{names[0]}"
        mine, lands = _exchange_wait(gathers[(l, names)], after, gather=True, name=f"gather_wait{tag}")
        full = {k: _unshard(with_own(g, o), _SHARD_AXIS[k] - 1) for k, g, o in zip(names, lands, mine)}
        return {kk: vv for kk, vv in _big_to_kernel_layout(full).items()}

    def get_big(l, after):
        if l == 0:
            return gathered(0, BIG[:1], after), lambda later: gathered(0, BIG[1:], later)
        return gathered(l, BIG, after), None

    scatters = {}

    def scatter(l, g, names, tag):
        gref = _big_grads_to_reference_layout(g, names)
        parts = [_to_entry(k, _shard_parts(gref[k], _SHARD_AXIS[k] - 1)).astype(BF16) for k in names]
        scatters[(l, names)] = _exchange_start(parts, [landing(p[0]) for p in parts], gather=False, name=f"scatter_start{tag}")
        return scatters[(l, names)][4]

    last_names = (BIG[1:], BIG[:1]) if depth > 1 else (BIG,)

    def early_grads(g):
        if len(last_names) > 1:
            scatter(0, g, last_names[0], "0_rest")

    def put_grads(l, g):
        names = last_names[-1] if l == 0 else BIG
        return scatter(l, g, names, f"{l}")[:1, :1]

    small = {k: w[k] for k in SMALL}
    loss, gx, grads = _local_step(x[0], mem[0], positions[0], small, loss_target[0], get_big, put_grads, early_grads)
    loss = lax.psum(loss, ("x", "y", "c"))
    me1 = me.reshape(1).astype(jnp.int32)
    res = {k: None for k in BIG}
    after = scatters[(0, last_names[-1])][4]
    for lo, n in ([(1, depth - 1), (0, 1)] if depth > 1 else [(0, 1)]):
        parts, recv = [], []
        for l in range(lo, lo + n):
            pl_, rl_ = {}, {}
            for names in ((BIG,) if l > 0 else last_names):
                tag = f"{l}" if (l > 0 or names == last_names[-1]) else "0_rest"
                ps_, rs_ = _exchange_wait(scatters[(l, names)], after, gather=False, name=f"scatter_wait{tag}")
                pl_.update(zip(names, ps_))
                rl_.update(zip(names, rs_))
            parts.append([pl_[k] for k in BIG])
            recv.append([rl_[k] for k in BIG])
        mine = [_sum_slots([recv[l][a] for l in range(n)], [parts[l][a] for l in range(n)], me1, name=f"sum_{k}_{lo}")
                for a, k in enumerate(BIG)]
        theirs = _sibling_exchange(mine, name=f"sibling_grads{lo}")
        for k, p, q in zip(BIG, mine, theirs):
            res[k] = _adamw(_to_entry(k, w[k]), [p, q], _to_entry(k, m[k]), _to_entry(k, v[k]), layer0=lo, prev=res[k],
                            name=f"adamw_{k}_{lo}")
        after = res[BIG[-1]][0]
    res = {k: [_to_entry(k, a) for a in r] for k, r in res.items()}
    gsmall = _small_grads_to_reference_layout(grads)
    g_small = _allreduce_small(_pack_small(gsmall), name="allreduce_small")
    packed = _adamw(_pack_small(small), [g_small], _pack_small({k: m[k] for k in SMALL}), _pack_small({k: v[k] for k in SMALL}),
                    name="adamw_small")
    unpacked = [_unpack_small(p, small) for p in packed]
    for k in SMALL:
        res[k] = [u[k] for u in unpacked]
    outs = [loss, gx[None]]
    for i in range(4):
        outs += [res[k][i] for k in WEIGHTS]
    return tuple(outs)

def kernel(x, mem, positions, g_mix, w_in, b_gate, a_qn, a_kn, a_sink, b_qa_norm, b_kva_norm, b_w_uq, b_w_ukv, b_qn, b_kn, c_qn, c_kn, m_g_mem, m_w_kv, m_qn, m_kn, w_branch, w_out, g_mlp, w_up, w_down, loss_target, m_g_mix, m_w_in, m_b_gate, m_a_qn, m_a_kn, m_a_sink, m_b_qa_norm, m_b_kva_norm, m_b_w_uq, m_b_w_ukv, m_b_qn, m_b_kn, m_c_qn, m_c_kn, m_m_g_mem, m_m_w_kv, m_m_qn, m_m_kn, m_w_branch, m_w_out, m_g_mlp, m_w_up, m_w_down, v_g_mix, v_w_in, v_b_gate, v_a_qn, v_a_kn, v_a_sink, v_b_qa_norm, v_b_kva_norm, v_b_w_uq, v_b_w_ukv, v_b_qn, v_b_kn, v_c_qn, v_c_kn, v_m_g_mem, v_m_w_kv, v_m_qn, v_m_kn, v_w_branch, v_w_out, v_g_mlp, v_w_up, v_w_down):
    args = dict(locals())
    w = {k: args[k] for k in WEIGHTS}
    m = {k: args["m_" + k] for k in WEIGHTS}
    v = {k: args["v_" + k] for k in WEIGHTS}
    return _train_step(x, mem, positions, loss_target, w, m, v)
```
